```python
import math
import jax, jax.numpy as jnp
from jax import lax
import numpy as np

D_MODEL = 2048
BATCH = 4
SEQ = 2048
DEPTH = 1
DEC_BATCH = 128
DEC_SEQ = 8
PAST_LEN = 16384
PAGE_SIZE = 128

POOL_WIDTH = D_MODEL // 2
POOL_WINDOWS = (2, 4, 8, 16)
POOL_GROUPS = len(POOL_WINDOWS)
POOL_GROUP_CH = POOL_WIDTH // POOL_GROUPS
POOL_OUT_CH = D_MODEL // POOL_GROUPS
POOL_BUF = max(POOL_WINDOWS) - 1
SSM_WIDTH = D_MODEL // 2
SSM_GROUP_CH = 16
SSM_GROUPS = SSM_WIDTH // SSM_GROUP_CH
SSM_STATE = 64
DT_MIN = 0.001
DT_MAX = 0.1
IN_WIDTH = POOL_WIDTH + SSM_WIDTH + 2 * D_MODEL
MOE_GROUPS = 4
MOE_EXPERTS_PER_GROUP = 8
MOE_EXPERTS = MOE_GROUPS * MOE_EXPERTS_PER_GROUP
MOE_TOPK = 2
MOE_FF = D_MODEL // 4
EPS = 1e-6

kernel_name = 'hybrid_pool_s5_hmoe_step'


def rmsnorm(x, g):
    xf = x.astype(jnp.float32)
    inv = lax.rsqrt(jnp.mean(xf * xf, axis=-1, keepdims=True) + EPS)
    return (xf * inv).astype(x.dtype) * g


def pool_mixer(u, buf, start_pos, w_pool, pool_scale):
    bsz, seq_len, _ = u.shape
    full = jnp.concatenate([buf.astype(u.dtype), u], axis=1)
    cs = jnp.cumsum(full.astype(jnp.float32), axis=1)
    cs = jnp.concatenate([jnp.zeros((bsz, 1, POOL_WIDTH), jnp.float32), cs], axis=1)
    end = cs[:, POOL_BUF + 1:]
    pos = start_pos + jnp.arange(seq_len, dtype=jnp.int32)
    uf = u.astype(jnp.float32)
    outs = []
    for g, w in enumerate(POOL_WINDOWS):
        sl = slice(g * POOL_GROUP_CH, (g + 1) * POOL_GROUP_CH)
        win = end[:, :, sl] - cs[:, POOL_BUF + 1 - w:POOL_BUF + 1 - w + seq_len, sl]
        cnt = jnp.minimum(w, pos + 1).astype(jnp.float32)[None, :, None]
        outs.append(win / cnt - uf[:, :, sl])
    pooled = jnp.stack(outs, axis=2).astype(u.dtype)
    y = jnp.einsum('blgc,gcd->blgd', pooled, w_pool).reshape(bsz, seq_len, D_MODEL) * pool_scale
    new_buf = full[:, -POOL_BUF:]
    return y, new_buf


def _cscan_combine(e1, e2):
    a1r, a1i, b1r, b1i = e1
    a2r, a2i, b2r, b2i = e2
    return (a2r * a1r - a2i * a1i,
            a2r * a1i + a2i * a1r,
            a2r * b1r - a2i * b1i + b2r,
            a2r * b1i + a2i * b1r + b2i)


def ssm_mixer(u, h0_re, h0_im, a_re, a_im, log_dt, b_re, b_im, c_re, c_im, d_skip, w_glu_a, w_glu_b):
    bsz, seq_len, _ = u.shape
    uf = u.astype(jnp.float32).reshape(bsz, seq_len, SSM_GROUPS, SSM_GROUP_CH)
    dt = jnp.exp(log_dt.astype(jnp.float32))[:, None]
    lr = a_re.astype(jnp.float32)
    li = a_im.astype(jnp.float32)
    mag = jnp.exp(lr * dt)
    ab_re = mag * jnp.cos(li * dt)
    ab_im = mag * jnp.sin(li * dt)
    den = lr * lr + li * li
    nr = ab_re - 1.0
    ni = ab_im
    q_re = (nr * lr + ni * li) / den
    q_im = (ni * lr - nr * li) / den
    br = b_re.astype(jnp.float32)
    bi = b_im.astype(jnp.float32)
    bb_re = q_re[..., None] * br - q_im[..., None] * bi
    bb_im = q_re[..., None] * bi + q_im[..., None] * br
    bu_re = jnp.einsum('blgc,gpc->blgp', uf, bb_re)
    bu_im = jnp.einsum('blgc,gpc->blgp', uf, bb_im)
    h0r = h0_re.astype(jnp.float32)
    h0i = h0_im.astype(jnp.float32)
    bu_re = bu_re.at[:, 0].add(ab_re * h0r - ab_im * h0i)
    bu_im = bu_im.at[:, 0].add(ab_re * h0i + ab_im * h0r)
    a_re_b = jnp.broadcast_to(ab_re, bu_re.shape)
    a_im_b = jnp.broadcast_to(ab_im, bu_im.shape)
    _, _, h_re, h_im = lax.associative_scan(_cscan_combine, (a_re_b, a_im_b, bu_re, bu_im), axis=1)
    y = (jnp.einsum('blgp,gcp->blgc', h_re, c_re.astype(jnp.float32))
         - jnp.einsum('blgp,gcp->blgc', h_im, c_im.astype(jnp.float32))
         + d_skip.astype(jnp.float32).reshape(SSM_GROUPS, SSM_GROUP_CH) * uf)
    y = jax.nn.gelu(y.reshape(bsz, seq_len, SSM_WIDTH)).astype(u.dtype)
    out = (y @ w_glu_a) * jax.nn.sigmoid(y @ w_glu_b)
    return out, h_re[:, -1], h_im[:, -1]


def hier_moe(x, w_rg, b_rg, w_rexp, b_rexp, w_eg, w_eu, w_ed):
    bsz, seq_len, _ = x.shape
    t = x.reshape(-1, D_MODEL)
    n_tok = t.shape[0]
    p_group = jax.nn.softmax((t @ w_rg + b_rg).astype(jnp.float32), axis=-1)
    g_val, g_idx = lax.top_k(p_group, 1)
    e_logits = (t @ w_rexp + b_rexp).astype(jnp.float32).reshape(n_tok, MOE_GROUPS, MOE_EXPERTS_PER_GROUP)
    e_sel = jnp.take_along_axis(e_logits, g_idx[:, :, None], axis=1)[:, 0]
    p_exp = jax.nn.softmax(e_sel, axis=-1)
    e_val, e_idx = lax.top_k(p_exp, MOE_TOPK)
    e_val = e_val / jnp.sum(e_val, axis=-1, keepdims=True)
    eid = g_idx * MOE_EXPERTS_PER_GROUP + e_idx
    comb = jnp.sum(jax.nn.one_hot(eid, MOE_EXPERTS, dtype=jnp.float32) * (g_val * e_val)[..., None], axis=1)
    comb = comb.astype(x.dtype).reshape(n_tok, MOE_GROUPS, MOE_EXPERTS_PER_GROUP)
    out = jnp.zeros_like(t)
    for g in range(MOE_GROUPS):
        es = slice(g * MOE_EXPERTS_PER_GROUP, (g + 1) * MOE_EXPERTS_PER_GROUP)
        hg = jnp.einsum('td,edf->tef', t, w_eg[es])
        hu = jnp.einsum('td,edf->tef', t, w_eu[es])
        act = jax.nn.silu(hg) * hu * comb[:, g, :, None]
        out = out + jnp.einsum('tef,efd->td', act, w_ed[es])
    return out.reshape(bsz, seq_len, D_MODEL)


def layer(x, buf, start_pos, h0_re, h0_im, p):
    (g_mix, w_in, w_pool, pool_scale, a_re, a_im, log_dt, b_re, b_im, c_re, c_im, d_skip,
     w_glu_a, w_glu_b, w_out, g_ffn, w_rg, b_rg, w_rexp, b_rexp, w_eg, w_eu, w_ed) = p
    xn = rmsnorm(x, g_mix)
    proj = xn @ w_in
    s1 = POOL_WIDTH
    s2 = s1 + SSM_WIDTH
    s3 = s2 + D_MODEL
    u_pool = proj[..., :s1]
    u_ssm = proj[..., s1:s2]
    gate_pool = jax.nn.sigmoid(proj[..., s2:s3])
    gate_ssm = jax.nn.sigmoid(proj[..., s3:])
    y_pool, new_buf = pool_mixer(u_pool, buf, start_pos, w_pool, pool_scale)
    y_ssm, h_re, h_im = ssm_mixer(u_ssm, h0_re, h0_im, a_re, a_im, log_dt, b_re, b_im,
                                  c_re, c_im, d_skip, w_glu_a, w_glu_b)
    merged = gate_pool * y_pool + gate_ssm * y_ssm.astype(y_pool.dtype)
    h = x + merged @ w_out
    h = h + hier_moe(rmsnorm(h, g_ffn), w_rg, b_rg, w_rexp, b_rexp, w_eg, w_eu, w_ed)
    return h, new_buf, h_re, h_im


def setup_inputs(seed: int = 0) -> dict:
    key = jax.random.key(seed)
    ks = jax.random.split(key, 32)
    f32 = jnp.float32
    nrm = lambda k, shape, s: jax.random.normal(k, shape, f32) * s
    n_idx = jnp.arange(SSM_STATE, dtype=f32) * math.pi
    return {
        'x_prompt': nrm(ks[0], (BATCH, SEQ, D_MODEL), 1.0),
        'x_sample': nrm(ks[1], (DEC_BATCH, DEC_SEQ, D_MODEL), 1.0),
        'state_pool': nrm(ks[2], (DEPTH, DEC_BATCH, POOL_BUF, POOL_WIDTH), 1.0),
        'state_ssm_re': nrm(ks[3], (DEPTH, DEC_BATCH, SSM_GROUPS, SSM_STATE), 1.0),
        'state_ssm_im': nrm(ks[4], (DEPTH, DEC_BATCH, SSM_GROUPS, SSM_STATE), 1.0),
        'g_mix': 1.0 + nrm(ks[5], (DEPTH, D_MODEL), 0.02),
        'w_in': nrm(ks[6], (DEPTH, D_MODEL, IN_WIDTH), D_MODEL ** -0.5),
        'w_pool': nrm(ks[7], (DEPTH, POOL_GROUPS, POOL_GROUP_CH, POOL_OUT_CH), POOL_GROUP_CH ** -0.5),
        'pool_scale': 1.0 + nrm(ks[8], (DEPTH, D_MODEL), 0.02),
        'ssm_a_re': -0.5 + nrm(ks[9], (DEPTH, SSM_GROUPS, SSM_STATE), 0.01),
        'ssm_a_im': n_idx + nrm(ks[10], (DEPTH, SSM_GROUPS, SSM_STATE), 0.01),
        'ssm_log_dt': jax.random.uniform(ks[11], (DEPTH, SSM_GROUPS), f32, math.log(DT_MIN), math.log(DT_MAX)),
        'ssm_b_re': nrm(ks[12], (DEPTH, SSM_GROUPS, SSM_STATE, SSM_GROUP_CH), (2 * SSM_GROUP_CH) ** -0.5),
        'ssm_b_im': nrm(ks[13], (DEPTH, SSM_GROUPS, SSM_STATE, SSM_GROUP_CH), (2 * SSM_GROUP_CH) ** -0.5),
        'ssm_c_re': nrm(ks[14], (DEPTH, SSM_GROUPS, SSM_GROUP_CH, SSM_STATE), (2 * SSM_STATE) ** -0.5),
        'ssm_c_im': nrm(ks[15], (DEPTH, SSM_GROUPS, SSM_GROUP_CH, SSM_STATE), (2 * SSM_STATE) ** -0.5),
        'ssm_d': nrm(ks[16], (DEPTH, SSM_WIDTH), 1.0),
        'w_glu_a': nrm(ks[17], (DEPTH, SSM_WIDTH, D_MODEL), SSM_WIDTH ** -0.5),
        'w_glu_b': nrm(ks[18], (DEPTH, SSM_WIDTH, D_MODEL), SSM_WIDTH ** -0.5),
        'w_out': nrm(ks[19], (DEPTH, D_MODEL, D_MODEL), D_MODEL ** -0.5),
        'g_ffn': 1.0 + nrm(ks[20], (DEPTH, D_MODEL), 0.02),
        'w_router_group': nrm(ks[21], (DEPTH, D_MODEL, MOE_GROUPS), D_MODEL ** -0.5),
        'b_router_group': nrm(ks[22], (DEPTH, MOE_GROUPS), 0.01),
        'w_router_expert': nrm(ks[23], (DEPTH, D_MODEL, MOE_EXPERTS), D_MODEL ** -0.5),
        'b_router_expert': nrm(ks[24], (DEPTH, MOE_EXPERTS), 0.01),
        'w_exp_gate': nrm(ks[25], (DEPTH, MOE_EXPERTS, D_MODEL, MOE_FF), D_MODEL ** -0.5),
        'w_exp_up': nrm(ks[26], (DEPTH, MOE_EXPERTS, D_MODEL, MOE_FF), D_MODEL ** -0.5),
        'w_exp_down': nrm(ks[27], (DEPTH, MOE_EXPERTS, MOE_FF, D_MODEL), MOE_FF ** -0.5),
        'g_final': 1.0 + nrm(ks[28], (D_MODEL,), 0.02),
    }


def reference(x_prompt, x_sample, state_pool, state_ssm_re, state_ssm_im,
              g_mix, w_in, w_pool, pool_scale, ssm_a_re, ssm_a_im, ssm_log_dt,
              ssm_b_re, ssm_b_im, ssm_c_re, ssm_c_im, ssm_d, w_glu_a, w_glu_b, w_out,
              g_ffn, w_router_group, b_router_group, w_router_expert, b_router_expert,
              w_exp_gate, w_exp_up, w_exp_down, g_final):
    xp = x_prompt
    xs = x_sample
    pp_buf, pp_re, pp_im = [], [], []
    ps_buf, ps_re, ps_im = [], [], []
    for l in range(DEPTH):
        p = (g_mix[l], w_in[l], w_pool[l], pool_scale[l], ssm_a_re[l], ssm_a_im[l], ssm_log_dt[l],
             ssm_b_re[l], ssm_b_im[l], ssm_c_re[l], ssm_c_im[l], ssm_d[l], w_glu_a[l], w_glu_b[l],
             w_out[l], g_ffn[l], w_router_group[l], b_router_group[l], w_router_expert[l],
             b_router_expert[l], w_exp_gate[l], w_exp_up[l], w_exp_down[l])
        buf0 = jnp.zeros((xp.shape[0], POOL_BUF, POOL_WIDTH), xp.dtype)
        h0 = jnp.zeros((xp.shape[0], SSM_GROUPS, SSM_STATE), jnp.float32)
        xp, b_p, r_p, i_p = layer(xp, buf0, 0, h0, h0, p)
        xs, b_s, r_s, i_s = layer(xs, state_pool[l], PAST_LEN, state_ssm_re[l], state_ssm_im[l], p)
        pp_buf.append(b_p); pp_re.append(r_p); pp_im.append(i_p)
        ps_buf.append(b_s); ps_re.append(r_s); ps_im.append(i_s)
    y_prompt = rmsnorm(xp, g_final)
    y_sample = rmsnorm(xs, g_final)
    new_pool_prompt = jnp.stack(pp_buf, axis=0)
    new_ssm_re_prompt = jnp.stack(pp_re, axis=0)
    new_ssm_im_prompt = jnp.stack(pp_im, axis=0)
    new_pool_sample = jnp.stack(ps_buf, axis=0)
    new_ssm_re_sample = jnp.stack(ps_re, axis=0)
    new_ssm_im_sample = jnp.stack(ps_im, axis=0)
    return (y_prompt, y_sample, new_pool_prompt, new_ssm_re_prompt, new_ssm_im_prompt,
            new_pool_sample, new_ssm_re_sample, new_ssm_im_sample)
```

```python
import functools
import math

import jax
import jax.numpy as jnp
from jax import lax
from jax.experimental import pallas as pl
from jax.experimental.pallas import tpu as pltpu

F32 = jnp.float32
BF16 = jnp.bfloat16
I32 = jnp.int32

D_MODEL = 2048
BATCH = 4
SEQ = 2048
DEC_BATCH = 128
DEC_SEQ = 8
PAST_LEN = 16384
POOL_WIDTH = D_MODEL // 2
POOL_WINDOWS = (2, 4, 8, 16)
POOL_GROUPS = len(POOL_WINDOWS)
POOL_GROUP_CH = POOL_WIDTH // POOL_GROUPS
POOL_OUT_CH = D_MODEL // POOL_GROUPS
POOL_BUF = max(POOL_WINDOWS) - 1
SSM_WIDTH = D_MODEL // 2
SSM_GROUP_CH = 16
SSM_GROUPS = SSM_WIDTH // SSM_GROUP_CH
SSM_STATE = 64
IN_WIDTH = POOL_WIDTH + SSM_WIDTH + 2 * D_MODEL
MOE_GROUPS = 4
MOE_EPG = 8
MOE_EXPERTS = MOE_GROUPS * MOE_EPG
MOE_FF = D_MODEL // 4
EPS = 1e-6

T_PROMPT = BATCH * SEQ
T_SAMPLE = DEC_BATCH * DEC_SEQ
T_ALL = T_PROMPT + T_SAMPLE

LANES = 128
SUBLANES = 8
VMEM_LIMIT = 56 * 1024 * 1024

CHUNK = 8
OCT = LANES
N_OCT = SSM_WIDTH // OCT
OCT_GROUPS = OCT // SSM_GROUP_CH
OCT_STATES = OCT_GROUPS * SSM_STATE
CW = CHUNK * OCT
SW = 2 * OCT_STATES

ROUTE_LANES = LANES
EXP_LANE0 = MOE_GROUPS
N_ASSIGN = 2 * T_ALL
TME = 256
N_ITEMS_MAX = N_ASSIGN // TME + MOE_EXPERTS
Y_PAD = 1024
Y_ROWS = T_ALL + Y_PAD


def _cparams(n_axes):
    return pltpu.CompilerParams(dimension_semantics=("arbitrary",) * n_axes,
                                vmem_limit_bytes=VMEM_LIMIT)


def _sigmoid(x):
    return 1.0 / (1.0 + jnp.exp(-x))


def _gelu_tanh(x):
    c = math.sqrt(2.0 / math.pi)
    return 0.5 * x * (1.0 + jnp.tanh(c * (x + 0.044715 * (x * x * x))))


IN_TM = 1024
IN_TN = 512


def _inproj_body(x_ref, g_ref, w_ref, *rest):
    o_ref, xn_ref = rest[-2], rest[-1]
    @pl.when(pl.program_id(1) == 0)
    def _():
        x = x_ref[...]
        inv = lax.rsqrt(jnp.mean(x * x, axis=-1, keepdims=True) + EPS)
        xn_ref[...] = ((x * inv) * g_ref[...]).astype(BF16)
    o_ref[...] = jnp.dot(xn_ref[...], w_ref[...], preferred_element_type=F32)


def _inproj(x, g, w_bf, row_block0, dst=None):
    n_i = x.shape[0] // IN_TM
    in_specs = [
        pl.BlockSpec((IN_TM, D_MODEL), lambda i, j: (i, 0)),
        pl.BlockSpec((1, D_MODEL), lambda i, j: (0, 0)),
        pl.BlockSpec((D_MODEL, IN_TN), lambda i, j: (0, j)),
    ]
    args = [x, g, w_bf]
    aliases = {}
    if dst is not None:
        in_specs.append(pl.BlockSpec(memory_space=pl.ANY))
        args.append(dst)
        aliases = {3: 0}
    return pl.pallas_call(
        _inproj_body,
        grid=(n_i, IN_WIDTH // IN_TN),
        in_specs=in_specs,
        out_specs=pl.BlockSpec((IN_TM, IN_TN), lambda i, j: (i + row_block0, j)),
        out_shape=jax.ShapeDtypeStruct((T_ALL, IN_WIDTH), F32),
        scratch_shapes=[pltpu.VMEM((IN_TM, D_MODEL), BF16)],
        input_output_aliases=aliases,
        compiler_params=_cparams(2),
        name="inproj",
    )(*args)


PP_TM = 512
HIST = 16


def _pool_project(pooled_g, g, w_ref, sc_ref, o_ref):
    y = jnp.dot(pooled_g.astype(BF16), w_ref[g], preferred_element_type=F32)
    lo, hi = g * POOL_OUT_CH, (g + 1) * POOL_OUT_CH
    o_ref[:, lo:hi] = y * sc_ref[:, lo:hi]


def _pool_prompt_body(u_ref, w_ref, sc_ref, o_ref, hist_ref):
    i = pl.program_id(1)
    @pl.when(i == 0)
    def _():
        hist_ref[...] = jnp.zeros_like(hist_ref)
    u = u_ref[...]
    ext = jnp.concatenate([hist_ref[...], u], axis=0)
    hist_ref[...] = u[PP_TM - HIST:, :]
    pos = i * PP_TM + lax.broadcasted_iota(I32, (PP_TM, 1), 0)
    for g, w in enumerate(POOL_WINDOWS):
        lo, hi = g * POOL_GROUP_CH, (g + 1) * POOL_GROUP_CH
        s = ext[:, lo:hi]
        d = 1
        while d < w:
            s = s + pltpu.roll(s, d, axis=0)
            d *= 2
        cnt = jnp.minimum(w, pos + 1).astype(F32)
        pooled = s[HIST:, :] / cnt - u[:, lo:hi]
        _pool_project(pooled, g, w_ref, sc_ref, o_ref)


def _pool_prompt(proj, w_pool_bf, pool_scale):
    n_i = SEQ // PP_TM
    return pl.pallas_call(
        _pool_prompt_body,
        grid=(BATCH, n_i),
        in_specs=[
            pl.BlockSpec((PP_TM, POOL_WIDTH), lambda b, i: (b * n_i + i, 0)),
            pl.BlockSpec((POOL_GROUPS, POOL_GROUP_CH, POOL_OUT_CH), lambda b, i: (0, 0, 0)),
            pl.BlockSpec((1, D_MODEL), lambda b, i: (0, 0)),
        ],
        out_specs=pl.BlockSpec((PP_TM, D_MODEL), lambda b, i: (b * n_i + i, 0)),
        out_shape=jax.ShapeDtypeStruct((T_ALL, D_MODEL), F32),
        scratch_shapes=[pltpu.VMEM((HIST, POOL_WIDTH), F32)],
        compiler_params=_cparams(2),
        name="pool_prompt",
    )(proj, w_pool_bf, pool_scale)


def _pool_sample_body(u_ref, hist_ref, w_ref, sc_ref, _dst, o_ref):
    rows = [hist_ref[k] for k in range(POOL_BUF)]
    rows += [u_ref[DEC_BATCH * t:DEC_BATCH * (t + 1), :] for t in range(DEC_SEQ)]
    n = len(rows)
    for g, w in enumerate(POOL_WINDOWS):
        lo, hi = g * POOL_GROUP_CH, (g + 1) * POOL_GROUP_CH
        f = [r[:, lo:hi] for r in rows]
        cur = f
        d = 1
        while d < w:
            cur = [cur[k] + cur[k - d] if k - d >= 0 else cur[k] for k in range(n)]
            d *= 2
        pooled = jnp.concatenate(
            [cur[POOL_BUF + t] / float(w) - f[POOL_BUF + t] for t in range(DEC_SEQ)], axis=0)
        _pool_project(pooled, g, w_ref, sc_ref, o_ref)


def _pool_sample(proj, hist_tm, w_pool_bf, pool_scale, y_pool):
    blk = T_PROMPT // T_SAMPLE
    return pl.pallas_call(
        _pool_sample_body,
        grid=(1,),
        in_specs=[
            pl.BlockSpec((T_SAMPLE, POOL_WIDTH), lambda i: (blk, 0)),
            pl.BlockSpec((POOL_BUF, DEC_BATCH, POOL_WIDTH), lambda i: (0, 0, 0)),
            pl.BlockSpec((POOL_GROUPS, POOL_GROUP_CH, POOL_OUT_CH), lambda i: (0, 0, 0)),
            pl.BlockSpec((1, D_MODEL), lambda i: (0, 0)),
            pl.BlockSpec(memory_space=pl.ANY),
        ],
        out_specs=pl.BlockSpec((T_SAMPLE, D_MODEL), lambda i: (blk, 0)),
        out_shape=jax.ShapeDtypeStruct((T_ALL, D_MODEL), F32),
        input_output_aliases={4: 0},
        compiler_params=_cparams(1),
        name="pool_sample",
    )(proj, hist_tm, w_pool_bf, pool_scale, y_pool)


def _ssm_tables(a_re, a_im, log_dt, b_re, b_im, c_re, c_im, d_skip):
    hp = lax.Precision.HIGHEST
    dt = jnp.exp(log_dt)[:, None]
    lr, li = a_re, a_im
    ab_re = jnp.exp(lr * dt) * jnp.cos(li * dt)
    ab_im = jnp.exp(lr * dt) * jnp.sin(li * dt)
    den = lr * lr + li * li
    nr, ni = ab_re - 1.0, ab_im
    q_re = (nr * lr + ni * li) / den
    q_im = (ni * lr - nr * li) / den
    bb_re = q_re[..., None] * b_re - q_im[..., None] * b_im
    bb_im = q_re[..., None] * b_im + q_im[..., None] * b_re

    def lam_pow(k):
        k = jnp.asarray(k, F32)[:, None, None]
        m = jnp.exp(k * lr * dt)
        return m * jnp.cos(k * li * dt), m * jnp.sin(k * li * dt)

    G, P, C = SSM_GROUPS, SSM_STATE, SSM_GROUP_CH
    pr, pi_ = lam_pow(list(range(CHUNK - 1, -1, -1)))
    kb_re = pr[..., None] * bb_re - pi_[..., None] * bb_im
    kb_im = pr[..., None] * bb_im + pi_[..., None] * bb_re
    f = jnp.stack([kb_re, kb_im], axis=0)
    f = f.reshape(2, CHUNK, N_OCT, OCT_GROUPS, P, C).transpose(2, 1, 3, 5, 0, 4)
    fc = f.reshape(N_OCT, CW, 2 * P)
    pr, pi_ = lam_pow(list(range(1, CHUNK + 1)))
    cl_re = c_re[None] * pr[:, :, None, :] - c_im[None] * pi_[:, :, None, :]
    cl_im = c_re[None] * pi_[:, :, None, :] + c_im[None] * pr[:, :, None, :]
    g = jnp.stack([cl_re, -cl_im], axis=0)
    g = g.reshape(2, CHUNK, N_OCT, OCT_GROUPS, C, P).transpose(2, 0, 5, 1, 3, 4)
    gc = g.reshape(N_OCT, 2 * P, CW)
    pr, pi_ = lam_pow(list(range(CHUNK)))
    pb_re = pr[..., None] * bb_re - pi_[..., None] * bb_im
    pb_im = pr[..., None] * bb_im + pi_[..., None] * bb_re
    k = (jnp.einsum('gcp,tgpd->tgcd', c_re, pb_re, precision=hp)
         - jnp.einsum('gcp,tgpd->tgcd', c_im, pb_im, precision=hp))
    s_idx = jnp.arange(CHUNK)[:, None]
    t_idx = jnp.arange(CHUNK)[None, :]
    lag = t_idx - s_idx
    m = jnp.where((lag >= 0)[:, :, None, None, None], k[jnp.clip(lag, 0, CHUNK - 1)], 0.0)
    m = m.reshape(CHUNK, CHUNK, N_OCT, OCT_GROUPS, C, C).transpose(2, 0, 5, 1, 3, 4)
    mc = m.reshape(N_OCT, CHUNK * C, CW)
    def lam8_pow(k):
        re, im = lam_pow([CHUNK * kk for kk in k])
        v = jnp.stack([re, im], axis=1).reshape(len(k), 2, N_OCT, OCT_STATES)
        return v.transpose(2, 0, 1, 3).reshape(N_OCT, len(k), SW)
    r = jnp.arange(SUBLANES)[None, :, None]
    parts = []
    for dd in (1, 2, 4):
        parts.append(jnp.where(r >= dd, lam8_pow([dd]), 0.0))
    parts.append(lam8_pow(list(range(1, SUBLANES + 1))))
    tab = jnp.concatenate(parts, axis=1)
    dsk = d_skip.reshape(N_OCT, 1, OCT)
    return fc.astype(BF16), gc.astype(BF16), mc.astype(BF16), tab, dsk


def _expand_weights(fc_ref, gc_ref, mc_ref, f_ref, g_ref, m_ref):
    row = lax.broadcasted_iota(I32, (CW, 1), 0)
    col = lax.broadcasted_iota(I32, (1, CW), 1)
    chan_gi_row = (row >> 4) & 7
    chan_gi_col = (col >> 4) & 7
    st_gi_row = (row >> 6) & 7
    st_gi_col = (col >> 6) & 7
    st_src_row = ((row >> 9) << 6) | (row & 63)
    st_src_col = ((col >> 9) << 6) | (col & 63)
    k128r = lax.broadcasted_iota(I32, (2 * SSM_STATE, 1), 0)
    k128c = lax.broadcasted_iota(I32, (1, 2 * SSM_STATE), 1)
    e_f = jnp.where(k128r == st_src_col, 1.0, 0.0).astype(BF16)
    f = jnp.dot(fc_ref[...], e_f, preferred_element_type=F32)
    f_ref[...] = jnp.where(chan_gi_row == st_gi_col, f, 0.0).astype(BF16)
    e_g = jnp.where(st_src_row == k128c, 1.0, 0.0).astype(BF16)
    g = jnp.dot(e_g, gc_ref[...], preferred_element_type=F32)
    g_ref[...] = jnp.where(st_gi_row == chan_gi_col, g, 0.0).astype(BF16)
    e_m = jnp.where((((row >> 7) << 4) | (row & 15)) == k128c, 1.0, 0.0).astype(BF16)
    m = jnp.dot(e_m, mc_ref[...], preferred_element_type=F32)
    m_ref[...] = jnp.where(chan_gi_row == chan_gi_col, m, 0.0).astype(BF16)


def _cmul(ar, ai, br, bi):
    return ar * br - ai * bi, ar * bi + ai * br


def _chunk_scan(sloc, tab_ref):
    R = sloc.shape[0]
    nb = R // SUBLANES
    sr, si = sloc[:, :OCT_STATES], sloc[:, OCT_STATES:]
    rowi = lax.broadcasted_iota(I32, (R, 1), 0)
    tr = jnp.where(rowi == 0, 0.0, pltpu.roll(sr, 1, axis=0))
    ti = jnp.where(rowi == 0, 0.0, pltpu.roll(si, 1, axis=0))
    for lvl, d in enumerate((1, 2, 4)):
        mr = tab_ref[lvl * SUBLANES:(lvl + 1) * SUBLANES, :OCT_STATES]
        mi = tab_ref[lvl * SUBLANES:(lvl + 1) * SUBLANES, OCT_STATES:]
        mr = jnp.concatenate([mr] * nb, axis=0)
        mi = jnp.concatenate([mi] * nb, axis=0)
        pr, pi_ = _cmul(mr, mi, pltpu.roll(tr, d, axis=0), pltpu.roll(ti, d, axis=0))
        tr, ti = tr + pr, ti + pi_
    pwr = tab_ref[3 * SUBLANES:4 * SUBLANES, :OCT_STATES]
    pwi = tab_ref[3 * SUBLANES:4 * SUBLANES, OCT_STATES:]
    cr = jnp.zeros((1, OCT_STATES), F32)
    ci = jnp.zeros((1, OCT_STATES), F32)
    out_r, out_i = [], []
    for k in range(nb):
        ar = tr[k * SUBLANES:(k + 1) * SUBLANES, :]
        ai = ti[k * SUBLANES:(k + 1) * SUBLANES, :]
        pr, pi_ = _cmul(pwr, pwi, jnp.broadcast_to(cr, ar.shape), jnp.broadcast_to(ci, ai.shape))
        hr, hi = ar + pr, ai + pi_
        out_r.append(hr)
        out_i.append(hi)
        cr, ci = hr[SUBLANES - 1:, :], hi[SUBLANES - 1:, :]
    hin = jnp.concatenate([jnp.concatenate(out_r, axis=0), jnp.concatenate(out_i, axis=0)], axis=1)
    lr, li = pwr[0:1, :], pwi[0:1, :]
    fr, fi = _cmul(lr, li, cr, ci)
    fin = jnp.concatenate([fr + sr[R - 1:, :], fi + si[R - 1:, :]], axis=1)
    return hin, fin


def _ssm_prompt_body(u_ref, fc_ref, gc_ref, mc_ref, tab_ref, d_ref, y_ref, hout_ref,
                     f_ref, g_ref, m_ref):
    @pl.when(pl.program_id(1) == 0)
    def _():
        _expand_weights(fc_ref, gc_ref, mc_ref, f_ref, g_ref, m_ref)
    R = SEQ // CHUNK
    xs = [u_ref[pl.ds(s, R, stride=CHUNK), :] for s in range(CHUNK)]
    xb = jnp.concatenate(xs, axis=1).astype(BF16)
    sloc = jnp.dot(xb, f_ref[...], preferred_element_type=F32)
    hin, fin = _chunk_scan(sloc, tab_ref)
    y = (jnp.dot(xb, m_ref[...], preferred_element_type=F32)
         + jnp.dot(hin.astype(BF16), g_ref[...], preferred_element_type=F32))
    for t in range(CHUNK):
        yt = y[:, t * OCT:(t + 1) * OCT] + d_ref[...] * xs[t]
        y_ref[pl.ds(t, R, stride=CHUNK), :] = _gelu_tanh(yt)
    hout_ref[...] = fin


def _ssm_weight_specs(n_axes):
    if n_axes == 2:
        im3 = lambda o, b: (o, 0, 0)
    else:
        im3 = lambda o: (o, 0, 0)
    return [
        pl.BlockSpec((None, CW, 2 * SSM_STATE), im3),
        pl.BlockSpec((None, 2 * SSM_STATE, CW), im3),
        pl.BlockSpec((None, CHUNK * SSM_GROUP_CH, CW), im3),
        pl.BlockSpec((None, 4 * SUBLANES, SW), im3),
        pl.BlockSpec((None, 1, OCT), im3),
    ]


_SSM_SCRATCH = [pltpu.VMEM((CW, SW), BF16), pltpu.VMEM((SW, CW), BF16), pltpu.VMEM((CW, CW), BF16)]


def _ssm_prompt(proj, tables):
    col0 = POOL_WIDTH // OCT
    return pl.pallas_call(
        _ssm_prompt_body,
        grid=(N_OCT, BATCH),
        in_specs=[pl.BlockSpec((SEQ, OCT), lambda o, b: (b, col0 + o))] + _ssm_weight_specs(2),
        out_specs=(
            pl.BlockSpec((SEQ, OCT), lambda o, b: (b, o)),
            pl.BlockSpec((None, 1, SW), lambda o, b: (b * N_OCT + o, 0, 0)),
        ),
        out_shape=(
            jax.ShapeDtypeStruct((T_ALL, SSM_WIDTH), F32),
            jax.ShapeDtypeStruct((BATCH * N_OCT, 1, SW), F32),
        ),
        scratch_shapes=_SSM_SCRATCH,
        compiler_params=_cparams(2),
        name="ssm_prompt",
    )(proj, *tables)


def _ssm_sample_body(u_ref, h0r_ref, h0i_ref, fc_ref, gc_ref, mc_ref, tab_ref, d_ref, _dst,
                     y_ref, hr_ref, hi_ref, f_ref, g_ref, m_ref):
    _expand_weights(fc_ref, gc_ref, mc_ref, f_ref, g_ref, m_ref)
    B = DEC_BATCH
    xs = [u_ref[B * s:B * (s + 1), :] for s in range(CHUNK)]
    xb = jnp.concatenate(xs, axis=1).astype(BF16)
    sloc = jnp.dot(xb, f_ref[...], preferred_element_type=F32)
    h0r, h0i = h0r_ref[...], h0i_ref[...]
    hin = jnp.concatenate([h0r, h0i], axis=1).astype(BF16)
    y = (jnp.dot(xb, m_ref[...], preferred_element_type=F32)
         + jnp.dot(hin, g_ref[...], preferred_element_type=F32))
    for t in range(CHUNK):
        yt = y[:, t * OCT:(t + 1) * OCT] + d_ref[...] * xs[t]
        y_ref[B * t:B * (t + 1), :] = _gelu_tanh(yt)
    lr = tab_ref[3 * SUBLANES:3 * SUBLANES + 1, :OCT_STATES]
    li = tab_ref[3 * SUBLANES:3 * SUBLANES + 1, OCT_STATES:]
    nr, ni = _cmul(lr, li, h0r, h0i)
    hr_ref[...] = nr + sloc[:, :OCT_STATES]
    hi_ref[...] = ni + sloc[:, OCT_STATES:]


def _ssm_sample(proj, h0r, h0i, tables, y_act):
    col0 = POOL_WIDTH // OCT
    blk = T_PROMPT // T_SAMPLE
    st_spec = pl.BlockSpec((DEC_BATCH, OCT_STATES), lambda o: (0, o))
    return pl.pallas_call(
        _ssm_sample_body,
        grid=(N_OCT,),
        in_specs=[pl.BlockSpec((T_SAMPLE, OCT), lambda o: (blk, col0 + o)), st_spec, st_spec]
        + _ssm_weight_specs(1) + [pl.BlockSpec(memory_space=pl.ANY)],
        out_specs=(pl.BlockSpec((T_SAMPLE, OCT), lambda o: (blk, o)), st_spec, st_spec),
        out_shape=(
            jax.ShapeDtypeStruct((T_ALL, SSM_WIDTH), F32),
            jax.ShapeDtypeStruct((DEC_BATCH, SSM_GROUPS * SSM_STATE), F32),
            jax.ShapeDtypeStruct((DEC_BATCH, SSM_GROUPS * SSM_STATE), F32),
        ),
        scratch_shapes=_SSM_SCRATCH,
        input_output_aliases={8: 0},
        compiler_params=_cparams(1),
        name="ssm_sample",
    )(proj, h0r, h0i, *tables, y_act)


PM_TM = 256
PM_PROMPT_BLOCKS = T_PROMPT // PM_TM


def _postmix_body(ya_ref, gp_ref, gs_ref, yp_ref, xp_ref, xs_ref, wa_ref, wb_ref, wo_ref,
                  gf_ref, wrh_ref, wrl_ref, br_ref, h_ref, tn_ref, rt_ref, cnt_out_ref, cnt_ref):
    i = pl.program_id(0)
    @pl.when(i == 0)
    def _():
        cnt_ref[...] = jnp.zeros_like(cnt_ref)
    ya = ya_ref[...].astype(BF16)
    a = jnp.dot(ya, wa_ref[...], preferred_element_type=F32)
    bg = jnp.dot(ya, wb_ref[...], preferred_element_type=F32)
    y_ssm = a * _sigmoid(bg)
    merged = _sigmoid(gp_ref[...]) * yp_ref[...] + _sigmoid(gs_ref[...]) * y_ssm
    x = jnp.where(i < PM_PROMPT_BLOCKS, xp_ref[...], xs_ref[...])
    h = x + jnp.dot(merged.astype(BF16), wo_ref[...], preferred_element_type=F32)
    h_ref[...] = h
    inv = lax.rsqrt(jnp.mean(h * h, axis=-1, keepdims=True) + EPS)
    tn = (h * inv) * gf_ref[...]
    tn_ref[...] = tn
    t_hi = tn.astype(BF16)
    t_lo = (tn - t_hi.astype(F32)).astype(BF16)
    wrh = wrh_ref[...]
    logits = (jnp.dot(t_hi, wrh, preferred_element_type=F32)
              + jnp.dot(t_lo, wrh, preferred_element_type=F32)
              + jnp.dot(t_hi, wrl_ref[...], preferred_element_type=F32)) + br_ref[...]
    lane = lax.broadcasted_iota(I32, (PM_TM, ROUTE_LANES), 1)
    neg = jnp.float32(-jnp.inf)
    big = jnp.int32(1 << 20)
    is_g = lane < MOE_GROUPS
    gmax = jnp.max(jnp.where(is_g, logits, neg), axis=1, keepdims=True)
    g_idx = jnp.min(jnp.where(is_g & (logits == gmax), lane, big), axis=1, keepdims=True)
    g_den = jnp.sum(jnp.where(is_g, jnp.exp(logits - gmax), 0.0), axis=1, keepdims=True)
    g_val = 1.0 / g_den
    e_lane = lane - EXP_LANE0
    sel = (e_lane >= 0) & (e_lane < MOE_EXPERTS) & ((e_lane >> 3) == g_idx)
    m1 = jnp.max(jnp.where(sel, logits, neg), axis=1, keepdims=True)
    i1 = jnp.min(jnp.where(sel & (logits == m1), lane, big), axis=1, keepdims=True)
    sel2 = sel & (lane != i1)
    m2 = jnp.max(jnp.where(sel2, logits, neg), axis=1, keepdims=True)
    i2 = jnp.min(jnp.where(sel2 & (logits == m2), lane, big), axis=1, keepdims=True)
    e2 = jnp.exp(m2 - m1)
    w1 = g_val / (1.0 + e2)
    w2 = g_val * e2 / (1.0 + e2)
    oh1 = lane == i1
    oh2 = lane == i2
    oh = jnp.where(oh1 | oh2, 1.0, 0.0)
    rr = lax.broadcasted_iota(I32, (PM_TM, PM_TM), 0)
    cc = lax.broadcasted_iota(I32, (PM_TM, PM_TM), 1)
    tri = jnp.where(cc < rr, 1.0, 0.0).astype(BF16)
    base = cnt_ref[...] + jnp.dot(tri, oh.astype(BF16), preferred_element_type=F32)
    rank1 = jnp.sum(jnp.where(oh1, base, 0.0), axis=1, keepdims=True)
    rank2 = jnp.sum(jnp.where(oh2, base, 0.0), axis=1, keepdims=True)
    cnt_ref[...] = cnt_ref[...] + jnp.sum(oh, axis=0, keepdims=True)
    cnt_out_ref[...] = cnt_ref[...]
    rt = jnp.where(lane == 0, w1, 0.0)
    rt = jnp.where(lane == 1, w2, rt)
    rt = jnp.where(lane == 2, rank1, rt)
    rt = jnp.where(lane == 3, rank2, rt)
    rt = jnp.where(lane == 4, (i1 - EXP_LANE0).astype(F32), rt)
    rt = jnp.where(lane == 5, (i2 - EXP_LANE0).astype(F32), rt)
    rt_ref[...] = rt


def _postmix(y_act, proj, y_pool, xp, xs, wa, wb, wo, g_ffn, wr_hi, wr_lo, b_r):
    n = T_ALL // PM_TM
    npb = PM_PROMPT_BLOCKS
    const2 = lambda i: (0, 0)
    return pl.pallas_call(
        _postmix_body,
        grid=(n,),
        in_specs=[
            pl.BlockSpec((PM_TM, SSM_WIDTH), lambda i: (i, 0)),
            pl.BlockSpec((PM_TM, D_MODEL), lambda i: (i, 1)),
            pl.BlockSpec((PM_TM, D_MODEL), lambda i: (i, 2)),
            pl.BlockSpec((PM_TM, D_MODEL), lambda i: (i, 0)),
            pl.BlockSpec((PM_TM, D_MODEL), lambda i: (jnp.minimum(i, npb - 1), 0)),
            pl.BlockSpec((PM_TM, D_MODEL), lambda i: (jnp.maximum(i - npb, 0), 0)),
            pl.BlockSpec((SSM_WIDTH, D_MODEL), const2, pipeline_mode=pl.Buffered(1)),
            pl.BlockSpec((SSM_WIDTH, D_MODEL), const2, pipeline_mode=pl.Buffered(1)),
            pl.BlockSpec((D_MODEL, D_MODEL), const2, pipeline_mode=pl.Buffered(1)),
            pl.BlockSpec((1, D_MODEL), const2),
            pl.BlockSpec((D_MODEL, ROUTE_LANES), const2),
            pl.BlockSpec((D_MODEL, ROUTE_LANES), const2),
            pl.BlockSpec((1, ROUTE_LANES), const2),
        ],
        out_specs=(
            pl.BlockSpec((PM_TM, D_MODEL), lambda i: (i, 0)),
            pl.BlockSpec((PM_TM, D_MODEL), lambda i: (i, 0)),
            pl.BlockSpec((PM_TM, ROUTE_LANES), lambda i: (i, 0)),
            pl.BlockSpec((1, ROUTE_LANES), const2),
        ),
        out_shape=(
            jax.ShapeDtypeStruct((T_ALL, D_MODEL), F32),
            jax.ShapeDtypeStruct((T_ALL, D_MODEL), F32),
            jax.ShapeDtypeStruct((T_ALL, ROUTE_LANES), F32),
            jax.ShapeDtypeStruct((1, ROUTE_LANES), F32),
        ),
        scratch_shapes=[pltpu.VMEM((1, ROUTE_LANES), F32)],
        compiler_params=_cparams(1),
        name="postmix",
    )(y_act, proj, proj, y_pool, xp, xs, wa, wb, wo, g_ffn, wr_hi, wr_lo, b_r)


def _row_copy(src, src_row, dst, dst_row, sem):
    return pltpu.make_async_copy(src.at[pl.ds(src_row, 1)], dst.at[pl.ds(dst_row, 1)], sem)


def _expert_body(q_ref, ie_ref, q0_ref, nv_ref, ni_ref, t_hbm, wg_ref, wu_ref, wd_ref, y_hbm,
                 order_ref, xs_ref, yb_ref, wgb_ref, wub_ref, wdb_ref, gsem, ssem):
    w = pl.program_id(0)

    @pl.when(w == 0)
    def _():
        def zero(i, c):
            order_ref[N_ASSIGN + i] = 0
            return c
        lax.fori_loop(0, TME, zero, 0)
        def invert(a, c):
            order_ref[q_ref[a]] = a
            return c
        lax.fori_loop(0, N_ASSIGN, invert, 0, unroll=8)

    @pl.when(w < ni_ref[0])
    def _():
        e = ie_ref[w]
        prev = ie_ref[jnp.maximum(w - 1, 0)]
        @pl.when((w == 0) | (prev != e))
        def _():
            wgb_ref[...] = wg_ref[...].astype(BF16)
            wub_ref[...] = wu_ref[...].astype(BF16)
            wdb_ref[...] = wd_ref[...].astype(BF16)
        q0 = q0_ref[w]
        nv = nv_ref[w]

        def gather(r, c):
            tok = order_ref[q0 + r] >> 1
            _row_copy(t_hbm, tok, xs_ref, r, gsem).start()
            return c
        lax.fori_loop(0, TME, gather, 0, unroll=8)
        def gather_wait(r, c):
            _row_copy(t_hbm, 0, xs_ref, 0, gsem).wait()
            return c
        lax.fori_loop(0, TME, gather_wait, 0)

        x = xs_ref[...].astype(BF16)
        hg = jnp.dot(x, wgb_ref[...], preferred_element_type=F32)
        hu = jnp.dot(x, wub_ref[...], preferred_element_type=F32)
        act = (hg * _sigmoid(hg)) * hu
        yb_ref[...] = jnp.dot(act.astype(BF16), wdb_ref[...], preferred_element_type=F32)

        def scatter(r, c):
            a = order_ref[q0 + r]
            row = jnp.where(r < nv, (a & 1) * Y_ROWS + (a >> 1), T_ALL + r)
            _row_copy(yb_ref, r, y_hbm, row, ssem).start()
            return c
        lax.fori_loop(0, TME, scatter, 0, unroll=8)
        def scatter_wait(r, c):
            _row_copy(yb_ref, 0, y_hbm, 0, ssem).wait()
            return c
        lax.fori_loop(0, TME, scatter_wait, 0)


def _experts(q_flat, item_e, item_q0, item_nv, n_items, tn, w_eg, w_eu, w_ed):
    wmap = lambda w, q, ie, q0, nv, ni: (ie[w], 0, 0)
    grid_spec = pltpu.PrefetchScalarGridSpec(
        num_scalar_prefetch=5,
        grid=(N_ITEMS_MAX,),
        in_specs=[
            pl.BlockSpec(memory_space=pl.ANY),
            pl.BlockSpec((None, D_MODEL, MOE_FF), wmap),
            pl.BlockSpec((None, D_MODEL, MOE_FF), wmap),
            pl.BlockSpec((None, MOE_FF, D_MODEL), wmap),
        ],
        out_specs=pl.BlockSpec(memory_space=pl.ANY),
        scratch_shapes=[
            pltpu.SMEM((N_ASSIGN + TME,), I32),
            pltpu.VMEM((TME, D_MODEL), F32),
            pltpu.VMEM((TME, D_MODEL), F32),
            pltpu.VMEM((D_MODEL, MOE_FF), BF16),
            pltpu.VMEM((D_MODEL, MOE_FF), BF16),
            pltpu.VMEM((MOE_FF, D_MODEL), BF16),
            pltpu.SemaphoreType.DMA(()),
            pltpu.SemaphoreType.DMA(()),
        ],
    )
    return pl.pallas_call(
        _expert_body,
        grid_spec=grid_spec,
        out_shape=jax.ShapeDtypeStruct((2 * Y_ROWS, D_MODEL), F32),
        compiler_params=_cparams(1),
        name="experts",
    )(q_flat, item_e, item_q0, item_nv, n_items, tn, w_eg, w_eu, w_ed)


FN_TM = 256
FN_PROMPT_BLOCKS = T_PROMPT // FN_TM


def _final_body(h_ref, y0_ref, y1_ref, rt_ref, g_ref, op_ref, os_ref):
    i = pl.program_id(0)
    rt = rt_ref[...]
    z = h_ref[...] + rt[:, 0:1] * y0_ref[...] + rt[:, 1:2] * y1_ref[...]
    inv = lax.rsqrt(jnp.mean(z * z, axis=-1, keepdims=True) + EPS)
    out = (z * inv) * g_ref[...]
    @pl.when(i < FN_PROMPT_BLOCKS)
    def _():
        op_ref[...] = out
    @pl.when(i >= FN_PROMPT_BLOCKS)
    def _():
        os_ref[...] = out


def _final(h, y, route, g_final):
    n = T_ALL // FN_TM
    npb = FN_PROMPT_BLOCKS
    yoff = Y_ROWS // FN_TM
    return pl.pallas_call(
        _final_body,
        grid=(n,),
        in_specs=[
            pl.BlockSpec((FN_TM, D_MODEL), lambda i: (i, 0)),
            pl.BlockSpec((FN_TM, D_MODEL), lambda i: (i, 0)),
            pl.BlockSpec((FN_TM, D_MODEL), lambda i: (yoff + i, 0)),
            pl.BlockSpec((FN_TM, ROUTE_LANES), lambda i: (i, 0)),
            pl.BlockSpec((1, D_MODEL), lambda i: (0, 0)),
        ],
        out_specs=(
            pl.BlockSpec((FN_TM, D_MODEL), lambda i: (jnp.minimum(i, npb - 1), 0)),
            pl.BlockSpec((FN_TM, D_MODEL), lambda i: (jnp.maximum(i - npb, 0), 0)),
        ),
        out_shape=(
            jax.ShapeDtypeStruct((T_PROMPT, D_MODEL), F32),
            jax.ShapeDtypeStruct((T_SAMPLE, D_MODEL), F32),
        ),
        compiler_params=_cparams(1),
        name="final",
    )(h, y, y, route, g_final)


def _dispatch_plan(route, cnt):
    counts = cnt[0, EXP_LANE0:EXP_LANE0 + MOE_EXPERTS].astype(I32)
    cum = jnp.cumsum(counts)
    cumex = cum - counts
    rank = route[:, 2:4].astype(I32)
    eid = route[:, 4:6].astype(I32)
    onehot = eid[..., None] == jnp.arange(MOE_EXPERTS, dtype=I32)
    q = rank + jnp.sum(jnp.where(onehot, cumex, 0), axis=-1)
    q_flat = q.reshape(N_ASSIGN)
    tiles = (counts + (TME - 1)) // TME
    cumt = jnp.cumsum(tiles)
    n_items = cumt[-1]
    w = jnp.arange(N_ITEMS_MAX, dtype=I32)
    w_eff = jnp.minimum(w, n_items - 1)
    item_e = jnp.sum((w_eff[:, None] >= cumt[None, :]).astype(I32), axis=1)
    j = w_eff - (cumt - tiles)[item_e]
    item_q0 = cumex[item_e] + TME * j
    item_nv = jnp.clip(counts[item_e] - TME * j, 0, TME)
    return q_flat, item_e, item_q0, item_nv, n_items.reshape(1)


def kernel(x_prompt, x_sample, state_pool, state_ssm_re, state_ssm_im, g_mix, w_in, w_pool,
           pool_scale, ssm_a_re, ssm_a_im, ssm_log_dt, ssm_b_re, ssm_b_im, ssm_c_re, ssm_c_im,
           ssm_d, w_glu_a, w_glu_b, w_out, g_ffn, w_router_group, b_router_group,
           w_router_expert, b_router_expert, w_exp_gate, w_exp_up, w_exp_down, g_final):
    l = 0
    xp = x_prompt.reshape(T_PROMPT, D_MODEL)
    xs = x_sample.transpose(1, 0, 2).reshape(T_SAMPLE, D_MODEL)
    w_in_bf = w_in[l].astype(BF16)
    w_pool_bf = w_pool[l].astype(BF16)
    wa_bf = w_glu_a[l].astype(BF16)
    wb_bf = w_glu_b[l].astype(BF16)
    wo_bf = w_out[l].astype(BF16)
    g_mix2 = g_mix[l].reshape(1, D_MODEL)
    scale2 = pool_scale[l].reshape(1, D_MODEL)

    proj = _inproj(xp, g_mix2, w_in_bf, 0)
    proj = _inproj(xs, g_mix2, w_in_bf, T_PROMPT // IN_TM, dst=proj)

    y_pool = _pool_prompt(proj, w_pool_bf, scale2)
    hist_tm = state_pool[l].transpose(1, 0, 2)
    y_pool = _pool_sample(proj, hist_tm, w_pool_bf, scale2, y_pool)

    tables = _ssm_tables(ssm_a_re[l], ssm_a_im[l], ssm_log_dt[l], ssm_b_re[l], ssm_b_im[l],
                         ssm_c_re[l], ssm_c_im[l], ssm_d[l])
    y_act, h_prompt = _ssm_prompt(proj, tables)
    h0r = state_ssm_re[l].reshape(DEC_BATCH, SSM_GROUPS * SSM_STATE)
    h0i = state_ssm_im[l].reshape(DEC_BATCH, SSM_GROUPS * SSM_STATE)
    y_act, hs_re, hs_im = _ssm_sample(proj, h0r, h0i, tables, y_act)

    w_r = jnp.zeros((D_MODEL, ROUTE_LANES), F32)
    w_r = w_r.at[:, :MOE_GROUPS].set(w_router_group[l])
    w_r = w_r.at[:, EXP_LANE0:EXP_LANE0 + MOE_EXPERTS].set(w_router_expert[l])
    wr_hi = w_r.astype(BF16)
    wr_lo = (w_r - wr_hi.astype(F32)).astype(BF16)
    b_r = jnp.zeros((1, ROUTE_LANES), F32)
    b_r = b_r.at[0, :MOE_GROUPS].set(b_router_group[l])
    b_r = b_r.at[0, EXP_LANE0:EXP_LANE0 + MOE_EXPERTS].set(b_router_expert[l])

    h, tn, route, cnt = _postmix(y_act, proj, y_pool, xp, xs, wa_bf, wb_bf, wo_bf,
                                 g_ffn[l].reshape(1, D_MODEL), wr_hi, wr_lo, b_r)
    plan = _dispatch_plan(route, cnt)
    y = _experts(*plan, tn, w_exp_gate[l], w_exp_up[l], w_exp_down[l])
    yp, ys = _final(h, y, route, g_final.reshape(1, D_MODEL))

    y_prompt = yp.reshape(BATCH, SEQ, D_MODEL)
    y_sample = ys.reshape(DEC_SEQ, DEC_BATCH, D_MODEL).transpose(1, 0, 2)
    up = proj[:T_PROMPT, :POOL_WIDTH].reshape(BATCH, SEQ, POOL_WIDTH)
    new_pool_prompt = up[:, SEQ - POOL_BUF:, :][None]
    us = proj[T_PROMPT:, :POOL_WIDTH].reshape(DEC_SEQ, DEC_BATCH, POOL_WIDTH).transpose(1, 0, 2)
    new_pool_sample = jnp.concatenate([state_pool[l][:, DEC_SEQ:, :], us], axis=1)[None]
    hp = h_prompt.reshape(BATCH, N_OCT, 2, OCT_GROUPS, SSM_STATE).transpose(2, 0, 1, 3, 4)
    hp = hp.reshape(2, BATCH, SSM_GROUPS, SSM_STATE)
    shp = (1, DEC_BATCH, SSM_GROUPS, SSM_STATE)
    return (y_prompt, y_sample, new_pool_prompt, hp[0][None], hp[1][None], new_pool_sample,
            hs_re.reshape(shp), hs_im.reshape(shp))
```

```python
import functools
import math

import jax
import jax.numpy as jnp
from jax import lax
from jax.experimental import pallas as pl
from jax.experimental.pallas import tpu as pltpu

F32 = jnp.float32
BF16 = jnp.bfloat16
I32 = jnp.int32

D_MODEL = 2048
BATCH = 4
SEQ = 2048
DEC_BATCH = 128
DEC_SEQ = 8
PAST_LEN = 16384
POOL_WIDTH = D_MODEL // 2
POOL_WINDOWS = (2, 4, 8, 16)
POOL_GROUPS = len(POOL_WINDOWS)
POOL_GROUP_CH = POOL_WIDTH // POOL_GROUPS
POOL_OUT_CH = D_MODEL // POOL_GROUPS
POOL_BUF = max(POOL_WINDOWS) - 1
SSM_WIDTH = D_MODEL // 2
SSM_GROUP_CH = 16
SSM_GROUPS = SSM_WIDTH // SSM_GROUP_CH
SSM_STATE = 64
IN_WIDTH = POOL_WIDTH + SSM_WIDTH + 2 * D_MODEL
MOE_GROUPS = 4
MOE_EPG = 8
MOE_EXPERTS = MOE_GROUPS * MOE_EPG
MOE_FF = D_MODEL // 4
EPS = 1e-6

T_PROMPT = BATCH * SEQ
T_SAMPLE = DEC_BATCH * DEC_SEQ
T_ALL = T_PROMPT + T_SAMPLE

LANES = 128
SUBLANES = 8
VMEM_LIMIT = 56 * 1024 * 1024

CHUNK = 8
OCT = LANES
N_OCT = SSM_WIDTH // OCT
OCT_GROUPS = OCT // SSM_GROUP_CH
OCT_STATES = OCT_GROUPS * SSM_STATE
CW = CHUNK * OCT
SW = 2 * OCT_STATES

ROUTE_LANES = LANES
EXP_LANE0 = MOE_GROUPS
N_ASSIGN = 2 * T_ALL
TME = 256
N_ITEMS_MAX = N_ASSIGN // TME + MOE_EXPERTS
Y_PAD = 1024
Y_ROWS = T_ALL + Y_PAD


def _cparams(n_axes):
    return pltpu.CompilerParams(dimension_semantics=("arbitrary",) * n_axes,
                                vmem_limit_bytes=VMEM_LIMIT)


def _sigmoid(x):
    return 1.0 / (1.0 + jnp.exp(-x))


def _gelu_tanh(x):
    c = math.sqrt(2.0 / math.pi)
    return 0.5 * x * (1.0 + jnp.tanh(c * (x + 0.044715 * (x * x * x))))


IN_TM = 1024
IN_TN = 512


def _inproj_body(x_ref, g_ref, w_ref, *rest):
    o_ref, xn_ref = rest[-2], rest[-1]
    @pl.when(pl.program_id(1) == 0)
    def _():
        x = x_ref[...]
        inv = lax.rsqrt(jnp.mean(x * x, axis=-1, keepdims=True) + EPS)
        xn_ref[...] = ((x * inv) * g_ref[...]).astype(BF16)
    o_ref[...] = jnp.dot(xn_ref[...], w_ref[...], preferred_element_type=F32)


def _inproj(x, g, w_bf, row_block0, dst=None):
    n_i = x.shape[0] // IN_TM
    in_specs = [
        pl.BlockSpec((IN_TM, D_MODEL), lambda i, j: (i, 0)),
        pl.BlockSpec((1, D_MODEL), lambda i, j: (0, 0)),
        pl.BlockSpec((D_MODEL, IN_TN), lambda i, j: (0, j)),
    ]
    args = [x, g, w_bf]
    aliases = {}
    if dst is not None:
        in_specs.append(pl.BlockSpec(memory_space=pl.ANY))
        args.append(dst)
        aliases = {3: 0}
    return pl.pallas_call(
        _inproj_body,
        grid=(n_i, IN_WIDTH // IN_TN),
        in_specs=in_specs,
        out_specs=pl.BlockSpec((IN_TM, IN_TN), lambda i, j: (i + row_block0, j)),
        out_shape=jax.ShapeDtypeStruct((T_ALL, IN_WIDTH), F32),
        scratch_shapes=[pltpu.VMEM((IN_TM, D_MODEL), BF16)],
        input_output_aliases=aliases,
        compiler_params=_cparams(2),
        name="inproj",
    )(*args)


PP_TM = 512
HIST = 16


def _pool_project(pooled_g, g, w_ref, sc_ref, o_ref):
    y = jnp.dot(pooled_g.astype(BF16), w_ref[g], preferred_element_type=F32)
    lo, hi = g * POOL_OUT_CH, (g + 1) * POOL_OUT_CH
    o_ref[:, lo:hi] = y * sc_ref[:, lo:hi]


def _pool_prompt_body(u_ref, w_ref, sc_ref, o_ref, tail_ref, hist_ref):
    i = pl.program_id(1)
    @pl.when(i == 0)
    def _():
        hist_ref[...] = jnp.zeros_like(hist_ref)
    u = u_ref[...]
    ext = jnp.concatenate([hist_ref[...], u], axis=0)
    hist_ref[...] = u[PP_TM - HIST:, :]
    tail_ref[...] = u[PP_TM - HIST:, :]
    pos = i * PP_TM + lax.broadcasted_iota(I32, (PP_TM, 1), 0)
    for g, w in enumerate(POOL_WINDOWS):
        lo, hi = g * POOL_GROUP_CH, (g + 1) * POOL_GROUP_CH
        s = ext[:, lo:hi]
        d = 1
        while d < w:
            s = s + pltpu.roll(s, d, axis=0)
            d *= 2
        cnt = jnp.minimum(w, pos + 1).astype(F32)
        pooled = s[HIST:, :] / cnt - u[:, lo:hi]
        _pool_project(pooled, g, w_ref, sc_ref, o_ref)


def _pool_prompt(proj, w_pool_bf, pool_scale):
    n_i = SEQ // PP_TM
    return pl.pallas_call(
        _pool_prompt_body,
        grid=(BATCH, n_i),
        in_specs=[
            pl.BlockSpec((PP_TM, POOL_WIDTH), lambda b, i: (b * n_i + i, 0)),
            pl.BlockSpec((POOL_GROUPS, POOL_GROUP_CH, POOL_OUT_CH), lambda b, i: (0, 0, 0)),
            pl.BlockSpec((1, D_MODEL), lambda b, i: (0, 0)),
        ],
        out_specs=(
            pl.BlockSpec((PP_TM, D_MODEL), lambda b, i: (b * n_i + i, 0)),
            pl.BlockSpec((None, HIST, POOL_WIDTH), lambda b, i: (b, 0, 0)),
        ),
        out_shape=(
            jax.ShapeDtypeStruct((T_ALL, D_MODEL), F32),
            jax.ShapeDtypeStruct((BATCH, HIST, POOL_WIDTH), F32),
        ),
        scratch_shapes=[pltpu.VMEM((HIST, POOL_WIDTH), F32)],
        compiler_params=_cparams(2),
        name="pool_prompt",
    )(proj, w_pool_bf, pool_scale)


def _pool_sample_body(u_ref, hist_ref, w_ref, sc_ref, _dst, o_ref):
    rows = [hist_ref[k] for k in range(POOL_BUF)]
    rows += [u_ref[DEC_BATCH * t:DEC_BATCH * (t + 1), :] for t in range(DEC_SEQ)]
    n = len(rows)
    for g, w in enumerate(POOL_WINDOWS):
        lo, hi = g * POOL_GROUP_CH, (g + 1) * POOL_GROUP_CH
        f = [r[:, lo:hi] for r in rows]
        cur = f
        d = 1
        while d < w:
            cur = [cur[k] + cur[k - d] if k - d >= 0 else cur[k] for k in range(n)]
            d *= 2
        pooled = jnp.concatenate(
            [cur[POOL_BUF + t] / float(w) - f[POOL_BUF + t] for t in range(DEC_SEQ)], axis=0)
        _pool_project(pooled, g, w_ref, sc_ref, o_ref)


def _pool_sample(proj, hist_tm, w_pool_bf, pool_scale, y_pool):
    blk = T_PROMPT // T_SAMPLE
    return pl.pallas_call(
        _pool_sample_body,
        grid=(1,),
        in_specs=[
            pl.BlockSpec((T_SAMPLE, POOL_WIDTH), lambda i: (blk, 0)),
            pl.BlockSpec((POOL_BUF, DEC_BATCH, POOL_WIDTH), lambda i: (0, 0, 0)),
            pl.BlockSpec((POOL_GROUPS, POOL_GROUP_CH, POOL_OUT_CH), lambda i: (0, 0, 0)),
            pl.BlockSpec((1, D_MODEL), lambda i: (0, 0)),
            pl.BlockSpec(memory_space=pl.ANY),
        ],
        out_specs=pl.BlockSpec((T_SAMPLE, D_MODEL), lambda i: (blk, 0)),
        out_shape=jax.ShapeDtypeStruct((T_ALL, D_MODEL), F32),
        input_output_aliases={4: 0},
        compiler_params=_cparams(1),
        name="pool_sample",
    )(proj, hist_tm, w_pool_bf, pool_scale, y_pool)


def _ssm_tables(a_re, a_im, log_dt, b_re, b_im, c_re, c_im, d_skip):
    hp = lax.Precision.HIGHEST
    dt = jnp.exp(log_dt)[:, None]
    lr, li = a_re, a_im
    ab_re = jnp.exp(lr * dt) * jnp.cos(li * dt)
    ab_im = jnp.exp(lr * dt) * jnp.sin(li * dt)
    den = lr * lr + li * li
    nr, ni = ab_re - 1.0, ab_im
    q_re = (nr * lr + ni * li) / den
    q_im = (ni * lr - nr * li) / den
    bb_re = q_re[..., None] * b_re - q_im[..., None] * b_im
    bb_im = q_re[..., None] * b_im + q_im[..., None] * b_re

    def lam_pow(k):
        k = jnp.asarray(k, F32)[:, None, None]
        m = jnp.exp(k * lr * dt)
        return m * jnp.cos(k * li * dt), m * jnp.sin(k * li * dt)

    G, P, C = SSM_GROUPS, SSM_STATE, SSM_GROUP_CH
    pr, pi_ = lam_pow(list(range(CHUNK - 1, -1, -1)))
    kb_re = pr[..., None] * bb_re - pi_[..., None] * bb_im
    kb_im = pr[..., None] * bb_im + pi_[..., None] * bb_re
    f = jnp.stack([kb_re, kb_im], axis=0)
    f = f.reshape(2, CHUNK, N_OCT, OCT_GROUPS, P, C).transpose(2, 1, 3, 5, 0, 4)
    fc = f.reshape(N_OCT, CW, 2 * P)
    pr, pi_ = lam_pow(list(range(1, CHUNK + 1)))
    cl_re = c_re[None] * pr[:, :, None, :] - c_im[None] * pi_[:, :, None, :]
    cl_im = c_re[None] * pi_[:, :, None, :] + c_im[None] * pr[:, :, None, :]
    g = jnp.stack([cl_re, -cl_im], axis=0)
    g = g.reshape(2, CHUNK, N_OCT, OCT_GROUPS, C, P).transpose(2, 0, 5, 1, 3, 4)
    gc = g.reshape(N_OCT, 2 * P, CW)
    pr, pi_ = lam_pow(list(range(CHUNK)))
    pb_re = pr[..., None] * bb_re - pi_[..., None] * bb_im
    pb_im = pr[..., None] * bb_im + pi_[..., None] * bb_re
    k = (jnp.einsum('gcp,tgpd->tgcd', c_re, pb_re, precision=hp)
         - jnp.einsum('gcp,tgpd->tgcd', c_im, pb_im, precision=hp))
    s_idx = jnp.arange(CHUNK)[:, None]
    t_idx = jnp.arange(CHUNK)[None, :]
    lag = t_idx - s_idx
    m = jnp.where((lag >= 0)[:, :, None, None, None], k[jnp.clip(lag, 0, CHUNK - 1)], 0.0)
    m = m.reshape(CHUNK, CHUNK, N_OCT, OCT_GROUPS, C, C).transpose(2, 0, 5, 1, 3, 4)
    mc = m.reshape(N_OCT, CHUNK * C, CW)
    def lam8_pow(k):
        re, im = lam_pow([CHUNK * kk for kk in k])
        v = jnp.stack([re, im], axis=1).reshape(len(k), 2, N_OCT, OCT_STATES)
        return v.transpose(2, 0, 1, 3).reshape(N_OCT, len(k), SW)
    r = jnp.arange(SUBLANES)[None, :, None]
    parts = []
    for dd in (1, 2, 4):
        parts.append(jnp.where(r >= dd, lam8_pow([dd]), 0.0))
    parts.append(lam8_pow(list(range(1, SUBLANES + 1))))
    tab = jnp.concatenate(parts, axis=1)
    dsk = d_skip.reshape(N_OCT, 1, OCT)
    return fc.astype(BF16), gc.astype(BF16), mc.astype(BF16), tab, dsk


def _expand_weights(fc_ref, gc_ref, mc_ref, f_ref, g_ref, m_ref):
    row = lax.broadcasted_iota(I32, (CW, 1), 0)
    col = lax.broadcasted_iota(I32, (1, CW), 1)
    chan_gi_row = (row >> 4) & 7
    chan_gi_col = (col >> 4) & 7
    st_gi_row = (row >> 6) & 7
    st_gi_col = (col >> 6) & 7
    st_src_row = ((row >> 9) << 6) | (row & 63)
    st_src_col = ((col >> 9) << 6) | (col & 63)
    k128r = lax.broadcasted_iota(I32, (2 * SSM_STATE, 1), 0)
    k128c = lax.broadcasted_iota(I32, (1, 2 * SSM_STATE), 1)
    e_f = jnp.where(k128r == st_src_col, 1.0, 0.0).astype(BF16)
    f = jnp.dot(fc_ref[...], e_f, preferred_element_type=F32)
    f_ref[...] = jnp.where(chan_gi_row == st_gi_col, f, 0.0).astype(BF16)
    e_g = jnp.where(st_src_row == k128c, 1.0, 0.0).astype(BF16)
    g = jnp.dot(e_g, gc_ref[...], preferred_element_type=F32)
    g_ref[...] = jnp.where(st_gi_row == chan_gi_col, g, 0.0).astype(BF16)
    e_m = jnp.where((((row >> 7) << 4) | (row & 15)) == k128c, 1.0, 0.0).astype(BF16)
    m = jnp.dot(e_m, mc_ref[...], preferred_element_type=F32)
    m_ref[...] = jnp.where(chan_gi_row == chan_gi_col, m, 0.0).astype(BF16)


def _cmul(ar, ai, br, bi):
    return ar * br - ai * bi, ar * bi + ai * br


def _chunk_scan(sloc, tab_ref):
    R = sloc.shape[0]
    nb = R // SUBLANES
    sr, si = sloc[:, :OCT_STATES], sloc[:, OCT_STATES:]
    rowi = lax.broadcasted_iota(I32, (R, 1), 0)
    tr = jnp.where(rowi == 0, 0.0, pltpu.roll(sr, 1, axis=0))
    ti = jnp.where(rowi == 0, 0.0, pltpu.roll(si, 1, axis=0))
    for lvl, d in enumerate((1, 2, 4)):
        mr = tab_ref[lvl * SUBLANES:(lvl + 1) * SUBLANES, :OCT_STATES]
        mi = tab_ref[lvl * SUBLANES:(lvl + 1) * SUBLANES, OCT_STATES:]
        mr = jnp.concatenate([mr] * nb, axis=0)
        mi = jnp.concatenate([mi] * nb, axis=0)
        pr, pi_ = _cmul(mr, mi, pltpu.roll(tr, d, axis=0), pltpu.roll(ti, d, axis=0))
        tr, ti = tr + pr, ti + pi_
    pwr = tab_ref[3 * SUBLANES:4 * SUBLANES, :OCT_STATES]
    pwi = tab_ref[3 * SUBLANES:4 * SUBLANES, OCT_STATES:]
    cr = jnp.zeros((1, OCT_STATES), F32)
    ci = jnp.zeros((1, OCT_STATES), F32)
    out_r, out_i = [], []
    for k in range(nb):
        ar = tr[k * SUBLANES:(k + 1) * SUBLANES, :]
        ai = ti[k * SUBLANES:(k + 1) * SUBLANES, :]
        pr, pi_ = _cmul(pwr, pwi, jnp.broadcast_to(cr, ar.shape), jnp.broadcast_to(ci, ai.shape))
        hr, hi = ar + pr, ai + pi_
        out_r.append(hr)
        out_i.append(hi)
        cr, ci = hr[SUBLANES - 1:, :], hi[SUBLANES - 1:, :]
    hin = jnp.concatenate([jnp.concatenate(out_r, axis=0), jnp.concatenate(out_i, axis=0)], axis=1)
    lr, li = pwr[0:1, :], pwi[0:1, :]
    fr, fi = _cmul(lr, li, cr, ci)
    fin = jnp.concatenate([fr + sr[R - 1:, :], fi + si[R - 1:, :]], axis=1)
    return hin, fin


def _ssm_prompt_body(u_ref, fc_ref, gc_ref, mc_ref, tab_ref, d_ref, y_ref, hout_ref,
                     f_ref, g_ref, m_ref):
    @pl.when(pl.program_id(1) == 0)
    def _():
        _expand_weights(fc_ref, gc_ref, mc_ref, f_ref, g_ref, m_ref)
    R = SEQ // CHUNK
    xs = [u_ref[pl.ds(s, R, stride=CHUNK), :] for s in range(CHUNK)]
    xb = jnp.concatenate(xs, axis=1).astype(BF16)
    sloc = jnp.dot(xb, f_ref[...], preferred_element_type=F32)
    hin, fin = _chunk_scan(sloc, tab_ref)
    y = (jnp.dot(xb, m_ref[...], preferred_element_type=F32)
         + jnp.dot(hin.astype(BF16), g_ref[...], preferred_element_type=F32))
    for t in range(CHUNK):
        yt = y[:, t * OCT:(t + 1) * OCT] + d_ref[...] * xs[t]
        y_ref[pl.ds(t, R, stride=CHUNK), :] = _gelu_tanh(yt)
    hout_ref[...] = fin


def _ssm_weight_specs(n_axes):
    if n_axes == 2:
        im3 = lambda o, b: (o, 0, 0)
    else:
        im3 = lambda o: (o, 0, 0)
    return [
        pl.BlockSpec((None, CW, 2 * SSM_STATE), im3),
        pl.BlockSpec((None, 2 * SSM_STATE, CW), im3),
        pl.BlockSpec((None, CHUNK * SSM_GROUP_CH, CW), im3),
        pl.BlockSpec((None, 4 * SUBLANES, SW), im3),
        pl.BlockSpec((None, 1, OCT), im3),
    ]


_SSM_SCRATCH = [pltpu.VMEM((CW, SW), BF16), pltpu.VMEM((SW, CW), BF16), pltpu.VMEM((CW, CW), BF16)]


def _ssm_prompt(proj, tables):
    col0 = POOL_WIDTH // OCT
    return pl.pallas_call(
        _ssm_prompt_body,
        grid=(N_OCT, BATCH),
        in_specs=[pl.BlockSpec((SEQ, OCT), lambda o, b: (b, col0 + o))] + _ssm_weight_specs(2),
        out_specs=(
            pl.BlockSpec((SEQ, OCT), lambda o, b: (b, o)),
            pl.BlockSpec((None, 1, SW), lambda o, b: (b * N_OCT + o, 0, 0)),
        ),
        out_shape=(
            jax.ShapeDtypeStruct((T_ALL, SSM_WIDTH), F32),
            jax.ShapeDtypeStruct((BATCH * N_OCT, 1, SW), F32),
        ),
        scratch_shapes=_SSM_SCRATCH,
        compiler_params=_cparams(2),
        name="ssm_prompt",
    )(proj, *tables)


def _ssm_sample_body(u_ref, h0r_ref, h0i_ref, fc_ref, gc_ref, mc_ref, tab_ref, d_ref, _dst,
                     y_ref, hr_ref, hi_ref, f_ref, g_ref, m_ref):
    _expand_weights(fc_ref, gc_ref, mc_ref, f_ref, g_ref, m_ref)
    B = DEC_BATCH
    xs = [u_ref[B * s:B * (s + 1), :] for s in range(CHUNK)]
    xb = jnp.concatenate(xs, axis=1).astype(BF16)
    sloc = jnp.dot(xb, f_ref[...], preferred_element_type=F32)
    h0r, h0i = h0r_ref[...], h0i_ref[...]
    hin = jnp.concatenate([h0r, h0i], axis=1).astype(BF16)
    y = (jnp.dot(xb, m_ref[...], preferred_element_type=F32)
         + jnp.dot(hin, g_ref[...], preferred_element_type=F32))
    for t in range(CHUNK):
        yt = y[:, t * OCT:(t + 1) * OCT] + d_ref[...] * xs[t]
        y_ref[B * t:B * (t + 1), :] = _gelu_tanh(yt)
    lr = tab_ref[3 * SUBLANES:3 * SUBLANES + 1, :OCT_STATES]
    li = tab_ref[3 * SUBLANES:3 * SUBLANES + 1, OCT_STATES:]
    nr, ni = _cmul(lr, li, h0r, h0i)
    hr_ref[...] = nr + sloc[:, :OCT_STATES]
    hi_ref[...] = ni + sloc[:, OCT_STATES:]


def _ssm_sample(proj, h0r, h0i, tables, y_act):
    col0 = POOL_WIDTH // OCT
    blk = T_PROMPT // T_SAMPLE
    st_spec = pl.BlockSpec((DEC_BATCH, OCT_STATES), lambda o: (0, o))
    return pl.pallas_call(
        _ssm_sample_body,
        grid=(N_OCT,),
        in_specs=[pl.BlockSpec((T_SAMPLE, OCT), lambda o: (blk, col0 + o)), st_spec, st_spec]
        + _ssm_weight_specs(1) + [pl.BlockSpec(memory_space=pl.ANY)],
        out_specs=(pl.BlockSpec((T_SAMPLE, OCT), lambda o: (blk, o)), st_spec, st_spec),
        out_shape=(
            jax.ShapeDtypeStruct((T_ALL, SSM_WIDTH), F32),
            jax.ShapeDtypeStruct((DEC_BATCH, SSM_GROUPS * SSM_STATE), F32),
            jax.ShapeDtypeStruct((DEC_BATCH, SSM_GROUPS * SSM_STATE), F32),
        ),
        scratch_shapes=_SSM_SCRATCH,
        input_output_aliases={8: 0},
        compiler_params=_cparams(1),
        name="ssm_sample",
    )(proj, h0r, h0i, *tables, y_act)


PM_TM = 256
PM_PROMPT_BLOCKS = T_PROMPT // PM_TM


def _postmix_body(ya_ref, gp_ref, gs_ref, yp_ref, xp_ref, xs_ref, wa_ref, wb_ref, wo_ref,
                  gf_ref, wrh_ref, wrl_ref, br_ref, h_ref, tn_ref, rt_ref, cnt_out_ref, cnt_ref):
    i = pl.program_id(0)
    @pl.when(i == 0)
    def _():
        cnt_ref[...] = jnp.zeros_like(cnt_ref)
    ya = ya_ref[...].astype(BF16)
    a = jnp.dot(ya, wa_ref[...], preferred_element_type=F32)
    bg = jnp.dot(ya, wb_ref[...], preferred_element_type=F32)
    y_ssm = a * _sigmoid(bg)
    merged = _sigmoid(gp_ref[...]) * yp_ref[...] + _sigmoid(gs_ref[...]) * y_ssm
    x = jnp.where(i < PM_PROMPT_BLOCKS, xp_ref[...], xs_ref[...])
    h = x + jnp.dot(merged.astype(BF16), wo_ref[...], preferred_element_type=F32)
    h_ref[...] = h
    inv = lax.rsqrt(jnp.mean(h * h, axis=-1, keepdims=True) + EPS)
    tn = (h * inv) * gf_ref[...]
    tn_ref[...] = tn
    t_hi = tn.astype(BF16)
    t_lo = (tn - t_hi.astype(F32)).astype(BF16)
    wrh = wrh_ref[...]
    logits = (jnp.dot(t_hi, wrh, preferred_element_type=F32)
              + jnp.dot(t_lo, wrh, preferred_element_type=F32)
              + jnp.dot(t_hi, wrl_ref[...], preferred_element_type=F32)) + br_ref[...]
    lane = lax.broadcasted_iota(I32, (PM_TM, ROUTE_LANES), 1)
    neg = jnp.float32(-jnp.inf)
    big = jnp.int32(1 << 20)
    is_g = lane < MOE_GROUPS
    gmax = jnp.max(jnp.where(is_g, logits, neg), axis=1, keepdims=True)
    g_idx = jnp.min(jnp.where(is_g & (logits == gmax), lane, big), axis=1, keepdims=True)
    g_den = jnp.sum(jnp.where(is_g, jnp.exp(logits - gmax), 0.0), axis=1, keepdims=True)
    g_val = 1.0 / g_den
    e_lane = lane - EXP_LANE0
    sel = (e_lane >= 0) & (e_lane < MOE_EXPERTS) & ((e_lane >> 3) == g_idx)
    m1 = jnp.max(jnp.where(sel, logits, neg), axis=1, keepdims=True)
    i1 = jnp.min(jnp.where(sel & (logits == m1), lane, big), axis=1, keepdims=True)
    sel2 = sel & (lane != i1)
    m2 = jnp.max(jnp.where(sel2, logits, neg), axis=1, keepdims=True)
    i2 = jnp.min(jnp.where(sel2 & (logits == m2), lane, big), axis=1, keepdims=True)
    e2 = jnp.exp(m2 - m1)
    w1 = g_val / (1.0 + e2)
    w2 = g_val * e2 / (1.0 + e2)
    oh1 = lane == i1
    oh2 = lane == i2
    oh = jnp.where(oh1 | oh2, 1.0, 0.0)
    rr = lax.broadcasted_iota(I32, (PM_TM, PM_TM), 0)
    cc = lax.broadcasted_iota(I32, (PM_TM, PM_TM), 1)
    tri = jnp.where(cc < rr, 1.0, 0.0).astype(BF16)
    base = cnt_ref[...] + jnp.dot(tri, oh.astype(BF16), preferred_element_type=F32)
    rank1 = jnp.sum(jnp.where(oh1, base, 0.0), axis=1, keepdims=True)
    rank2 = jnp.sum(jnp.where(oh2, base, 0.0), axis=1, keepdims=True)
    cnt_ref[...] = cnt_ref[...] + jnp.sum(oh, axis=0, keepdims=True)
    cnt_out_ref[...] = cnt_ref[...]
    rt = jnp.where(lane == 0, w1, 0.0)
    rt = jnp.where(lane == 1, w2, rt)
    rt = jnp.where(lane == 2, rank1, rt)
    rt = jnp.where(lane == 3, rank2, rt)
    rt = jnp.where(lane == 4, (i1 - EXP_LANE0).astype(F32), rt)
    rt = jnp.where(lane == 5, (i2 - EXP_LANE0).astype(F32), rt)
    rt_ref[...] = rt


def _postmix(y_act, proj, y_pool, xp, xs, wa, wb, wo, g_ffn, wr_hi, wr_lo, b_r):
    n = T_ALL // PM_TM
    npb = PM_PROMPT_BLOCKS
    const2 = lambda i: (0, 0)
    return pl.pallas_call(
        _postmix_body,
        grid=(n,),
        in_specs=[
            pl.BlockSpec((PM_TM, SSM_WIDTH), lambda i: (i, 0)),
            pl.BlockSpec((PM_TM, D_MODEL), lambda i: (i, 1)),
            pl.BlockSpec((PM_TM, D_MODEL), lambda i: (i, 2)),
            pl.BlockSpec((PM_TM, D_MODEL), lambda i: (i, 0)),
            pl.BlockSpec((PM_TM, D_MODEL), lambda i: (jnp.minimum(i, npb - 1), 0)),
            pl.BlockSpec((PM_TM, D_MODEL), lambda i: (jnp.maximum(i - npb, 0), 0)),
            pl.BlockSpec((SSM_WIDTH, D_MODEL), const2, pipeline_mode=pl.Buffered(1)),
            pl.BlockSpec((SSM_WIDTH, D_MODEL), const2, pipeline_mode=pl.Buffered(1)),
            pl.BlockSpec((D_MODEL, D_MODEL), const2, pipeline_mode=pl.Buffered(1)),
            pl.BlockSpec((1, D_MODEL), const2),
            pl.BlockSpec((D_MODEL, ROUTE_LANES), const2),
            pl.BlockSpec((D_MODEL, ROUTE_LANES), const2),
            pl.BlockSpec((1, ROUTE_LANES), const2),
        ],
        out_specs=(
            pl.BlockSpec((PM_TM, D_MODEL), lambda i: (i, 0)),
            pl.BlockSpec((PM_TM, D_MODEL), lambda i: (i, 0)),
            pl.BlockSpec((PM_TM, ROUTE_LANES), lambda i: (i, 0)),
            pl.BlockSpec((1, ROUTE_LANES), const2),
        ),
        out_shape=(
            jax.ShapeDtypeStruct((T_ALL, D_MODEL), F32),
            jax.ShapeDtypeStruct((T_ALL, D_MODEL), F32),
            jax.ShapeDtypeStruct((T_ALL, ROUTE_LANES), F32),
            jax.ShapeDtypeStruct((1, ROUTE_LANES), F32),
        ),
        scratch_shapes=[pltpu.VMEM((1, ROUTE_LANES), F32)],
        compiler_params=_cparams(1),
        name="postmix",
    )(y_act, proj, proj, y_pool, xp, xs, wa, wb, wo, g_ffn, wr_hi, wr_lo, b_r)


def _expert_body(q_ref, ie_ref, q0_ref, nv_ref, ni_ref, t_hbm, wg_ref, wu_ref, wd_ref, y_hbm,
                 order_ref, xs_ref, yb_ref, wgb_ref, wub_ref, wdb_ref, gsem, ssem):
    w = pl.program_id(0)
    n_items = ni_ref[0]
    slot = w & 1

    def gather_copy(tok, s, r):
        return pltpu.make_async_copy(t_hbm.at[pl.ds(tok, 1)], xs_ref.at[s, pl.ds(r, 1)], gsem.at[s])

    def scatter_copy(s, r, row):
        return pltpu.make_async_copy(yb_ref.at[s, pl.ds(r, 1)], y_hbm.at[pl.ds(row, 1)], ssem.at[s])

    def start_gather(item, s):
        q0 = q0_ref[item]
        for r in range(TME):
            gather_copy(order_ref[q0 + r] >> 1, s, r).start()

    @pl.when(w == 0)
    def _():
        def zero(i, c):
            order_ref[N_ASSIGN + i] = 0
            return c
        lax.fori_loop(0, TME, zero, 0)
        def invert(a, c):
            order_ref[q_ref[a]] = a
            return c
        lax.fori_loop(0, N_ASSIGN, invert, 0, unroll=8)
        start_gather(0, 0)

    @pl.when(w < n_items)
    def _():
        e = ie_ref[w]
        prev = ie_ref[jnp.maximum(w - 1, 0)]
        @pl.when((w == 0) | (prev != e))
        def _():
            wgb_ref[...] = wg_ref[...].astype(BF16)
            wub_ref[...] = wu_ref[...].astype(BF16)
            wdb_ref[...] = wd_ref[...].astype(BF16)
        for r in range(TME):
            gather_copy(0, slot, r).wait()
        @pl.when(w + 1 < n_items)
        def _():
            start_gather(w + 1, 1 - slot)

        x = xs_ref[slot].astype(BF16)
        hg = jnp.dot(x, wgb_ref[...], preferred_element_type=F32)
        hu = jnp.dot(x, wub_ref[...], preferred_element_type=F32)
        act = (hg * _sigmoid(hg)) * hu
        yb_ref[slot] = jnp.dot(act.astype(BF16), wdb_ref[...], preferred_element_type=F32)

        @pl.when(w > 0)
        def _():
            for r in range(TME):
                scatter_copy(1 - slot, r, 0).wait()
        q0 = q0_ref[w]
        nv = nv_ref[w]
        for r in range(TME):
            a = order_ref[q0 + r]
            row = jnp.where(r < nv, (a & 1) * Y_ROWS + (a >> 1), T_ALL + r)
            scatter_copy(slot, r, row).start()
        @pl.when(w == n_items - 1)
        def _():
            for r in range(TME):
                scatter_copy(slot, r, 0).wait()


def _experts(q_flat, item_e, item_q0, item_nv, n_items, tn, w_eg, w_eu, w_ed):
    wmap = lambda w, q, ie, q0, nv, ni: (ie[w], 0, 0)
    grid_spec = pltpu.PrefetchScalarGridSpec(
        num_scalar_prefetch=5,
        grid=(N_ITEMS_MAX,),
        in_specs=[
            pl.BlockSpec(memory_space=pl.ANY),
            pl.BlockSpec((None, D_MODEL, MOE_FF), wmap),
            pl.BlockSpec((None, D_MODEL, MOE_FF), wmap),
            pl.BlockSpec((None, MOE_FF, D_MODEL), wmap),
        ],
        out_specs=pl.BlockSpec(memory_space=pl.ANY),
        scratch_shapes=[
            pltpu.SMEM((N_ASSIGN + TME,), I32),
            pltpu.VMEM((2, TME, D_MODEL), F32),
            pltpu.VMEM((2, TME, D_MODEL), F32),
            pltpu.VMEM((D_MODEL, MOE_FF), BF16),
            pltpu.VMEM((D_MODEL, MOE_FF), BF16),
            pltpu.VMEM((MOE_FF, D_MODEL), BF16),
            pltpu.SemaphoreType.DMA((2,)),
            pltpu.SemaphoreType.DMA((2,)),
        ],
    )
    return pl.pallas_call(
        _expert_body,
        grid_spec=grid_spec,
        out_shape=jax.ShapeDtypeStruct((2 * Y_ROWS, D_MODEL), F32),
        compiler_params=_cparams(1),
        name="experts",
    )(q_flat, item_e, item_q0, item_nv, n_items, tn, w_eg, w_eu, w_ed)


FN_TM = 256
FN_PROMPT_BLOCKS = T_PROMPT // FN_TM


def _final_body(h_ref, y0_ref, y1_ref, rt_ref, g_ref, op_ref, os_ref):
    i = pl.program_id(0)
    rt = rt_ref[...]
    z = h_ref[...] + rt[:, 0:1] * y0_ref[...] + rt[:, 1:2] * y1_ref[...]
    inv = lax.rsqrt(jnp.mean(z * z, axis=-1, keepdims=True) + EPS)
    out = (z * inv) * g_ref[...]
    @pl.when(i < FN_PROMPT_BLOCKS)
    def _():
        op_ref[...] = out
    @pl.when(i >= FN_PROMPT_BLOCKS)
    def _():
        os_ref[...] = out


def _final(h, y, route, g_final):
    n = T_ALL // FN_TM
    npb = FN_PROMPT_BLOCKS
    yoff = Y_ROWS // FN_TM
    return pl.pallas_call(
        _final_body,
        grid=(n,),
        in_specs=[
            pl.BlockSpec((FN_TM, D_MODEL), lambda i: (i, 0)),
            pl.BlockSpec((FN_TM, D_MODEL), lambda i: (i, 0)),
            pl.BlockSpec((FN_TM, D_MODEL), lambda i: (yoff + i, 0)),
            pl.BlockSpec((FN_TM, ROUTE_LANES), lambda i: (i, 0)),
            pl.BlockSpec((1, D_MODEL), lambda i: (0, 0)),
        ],
        out_specs=(
            pl.BlockSpec((FN_TM, D_MODEL), lambda i: (jnp.minimum(i, npb - 1), 0)),
            pl.BlockSpec((FN_TM, D_MODEL), lambda i: (jnp.maximum(i - npb, 0), 0)),
        ),
        out_shape=(
            jax.ShapeDtypeStruct((T_PROMPT, D_MODEL), F32),
            jax.ShapeDtypeStruct((T_SAMPLE, D_MODEL), F32),
        ),
        compiler_params=_cparams(1),
        name="final",
    )(h, y, y, route, g_final)


def _dispatch_plan(route, cnt):
    counts = cnt[0, EXP_LANE0:EXP_LANE0 + MOE_EXPERTS].astype(I32)
    cum = jnp.cumsum(counts)
    cumex = cum - counts
    rank = route[:, 2:4].astype(I32)
    eid = route[:, 4:6].astype(I32)
    onehot = eid[..., None] == jnp.arange(MOE_EXPERTS, dtype=I32)
    q = rank + jnp.sum(jnp.where(onehot, cumex, 0), axis=-1)
    q_flat = q.reshape(N_ASSIGN)
    tiles = (counts + (TME - 1)) // TME
    cumt = jnp.cumsum(tiles)
    n_items = cumt[-1]
    w = jnp.arange(N_ITEMS_MAX, dtype=I32)
    w_eff = jnp.minimum(w, n_items - 1)
    item_e = jnp.sum((w_eff[:, None] >= cumt[None, :]).astype(I32), axis=1)
    j = w_eff - (cumt - tiles)[item_e]
    item_q0 = cumex[item_e] + TME * j
    item_nv = jnp.clip(counts[item_e] - TME * j, 0, TME)
    return q_flat, item_e, item_q0, item_nv, n_items.reshape(1)


def kernel(x_prompt, x_sample, state_pool, state_ssm_re, state_ssm_im, g_mix, w_in, w_pool,
           pool_scale, ssm_a_re, ssm_a_im, ssm_log_dt, ssm_b_re, ssm_b_im, ssm_c_re, ssm_c_im,
           ssm_d, w_glu_a, w_glu_b, w_out, g_ffn, w_router_group, b_router_group,
           w_router_expert, b_router_expert, w_exp_gate, w_exp_up, w_exp_down, g_final):
    l = 0
    xp = x_prompt.reshape(T_PROMPT, D_MODEL)
    xs = x_sample.transpose(1, 0, 2).reshape(T_SAMPLE, D_MODEL)
    w_in_bf = w_in[l].astype(BF16)
    w_pool_bf = w_pool[l].astype(BF16)
    wa_bf = w_glu_a[l].astype(BF16)
    wb_bf = w_glu_b[l].astype(BF16)
    wo_bf = w_out[l].astype(BF16)
    g_mix2 = g_mix[l].reshape(1, D_MODEL)
    scale2 = pool_scale[l].reshape(1, D_MODEL)

    proj = _inproj(xp, g_mix2, w_in_bf, 0)
    proj = _inproj(xs, g_mix2, w_in_bf, T_PROMPT // IN_TM, dst=proj)

    y_pool, pool_tail = _pool_prompt(proj, w_pool_bf, scale2)
    hist_tm = state_pool[l].transpose(1, 0, 2)
    y_pool = _pool_sample(proj, hist_tm, w_pool_bf, scale2, y_pool)

    tables = _ssm_tables(ssm_a_re[l], ssm_a_im[l], ssm_log_dt[l], ssm_b_re[l], ssm_b_im[l],
                         ssm_c_re[l], ssm_c_im[l], ssm_d[l])
    y_act, h_prompt = _ssm_prompt(proj, tables)
    h0r = state_ssm_re[l].reshape(DEC_BATCH, SSM_GROUPS * SSM_STATE)
    h0i = state_ssm_im[l].reshape(DEC_BATCH, SSM_GROUPS * SSM_STATE)
    y_act, hs_re, hs_im = _ssm_sample(proj, h0r, h0i, tables, y_act)

    w_r = jnp.zeros((D_MODEL, ROUTE_LANES), F32)
    w_r = w_r.at[:, :MOE_GROUPS].set(w_router_group[l])
    w_r = w_r.at[:, EXP_LANE0:EXP_LANE0 + MOE_EXPERTS].set(w_router_expert[l])
    wr_hi = w_r.astype(BF16)
    wr_lo = (w_r - wr_hi.astype(F32)).astype(BF16)
    b_r = jnp.zeros((1, ROUTE_LANES), F32)
    b_r = b_r.at[0, :MOE_GROUPS].set(b_router_group[l])
    b_r = b_r.at[0, EXP_LANE0:EXP_LANE0 + MOE_EXPERTS].set(b_router_expert[l])

    h, tn, route, cnt = _postmix(y_act, proj, y_pool, xp, xs, wa_bf, wb_bf, wo_bf,
                                 g_ffn[l].reshape(1, D_MODEL), wr_hi, wr_lo, b_r)
    plan = _dispatch_plan(route, cnt)
    y = _experts(*plan, tn, w_exp_gate[l], w_exp_up[l], w_exp_down[l])
    yp, ys = _final(h, y, route, g_final.reshape(1, D_MODEL))

    y_prompt = yp.reshape(BATCH, SEQ, D_MODEL)
    y_sample = ys.reshape(DEC_SEQ, DEC_BATCH, D_MODEL).transpose(1, 0, 2)
    new_pool_prompt = pool_tail[:, HIST - POOL_BUF:, :][None]
    us = proj[T_PROMPT:, :POOL_WIDTH].reshape(DEC_SEQ, DEC_BATCH, POOL_WIDTH).transpose(1, 0, 2)
    new_pool_sample = jnp.concatenate([state_pool[l][:, DEC_SEQ:, :], us], axis=1)[None]
    hp = h_prompt.reshape(BATCH, N_OCT, 2, OCT_GROUPS, SSM_STATE).transpose(2, 0, 1, 3, 4)
    hp = hp.reshape(2, BATCH, SSM_GROUPS, SSM_STATE)
    shp = (1, DEC_BATCH, SSM_GROUPS, SSM_STATE)
    return (y_prompt, y_sample, new_pool_prompt, hp[0][None], hp[1][None], new_pool_sample,
            hs_re.reshape(shp), hs_im.reshape(shp))
```

```python
import functools
import math

import jax
import jax.numpy as jnp
from jax import lax
from jax.experimental import pallas as pl
from jax.experimental.pallas import tpu as pltpu

F32 = jnp.float32
BF16 = jnp.bfloat16
I32 = jnp.int32

D_MODEL = 2048
BATCH = 4
SEQ = 2048
DEC_BATCH = 128
DEC_SEQ = 8
PAST_LEN = 16384
POOL_WIDTH = D_MODEL // 2
POOL_WINDOWS = (2, 4, 8, 16)
POOL_GROUPS = len(POOL_WINDOWS)
POOL_GROUP_CH = POOL_WIDTH // POOL_GROUPS
POOL_OUT_CH = D_MODEL // POOL_GROUPS
POOL_BUF = max(POOL_WINDOWS) - 1
SSM_WIDTH = D_MODEL // 2
SSM_GROUP_CH = 16
SSM_GROUPS = SSM_WIDTH // SSM_GROUP_CH
SSM_STATE = 64
IN_WIDTH = POOL_WIDTH + SSM_WIDTH + 2 * D_MODEL
MOE_GROUPS = 4
MOE_EPG = 8
MOE_EXPERTS = MOE_GROUPS * MOE_EPG
MOE_FF = D_MODEL // 4
EPS = 1e-6

T_PROMPT = BATCH * SEQ
T_SAMPLE = DEC_BATCH * DEC_SEQ
T_ALL = T_PROMPT + T_SAMPLE

LANES = 128
SUBLANES = 8
VMEM_LIMIT = 56 * 1024 * 1024

CHUNK = 8
OCT = LANES
N_OCT = SSM_WIDTH // OCT
OCT_GROUPS = OCT // SSM_GROUP_CH
OCT_STATES = OCT_GROUPS * SSM_STATE
CW = CHUNK * OCT
SW = 2 * OCT_STATES

ROUTE_LANES = LANES
EXP_LANE0 = MOE_GROUPS
N_ASSIGN = 2 * T_ALL
TME = 256
N_ITEMS_MAX = N_ASSIGN // TME + MOE_EXPERTS
Y_PAD = 1024
Y_ROWS = T_ALL + Y_PAD
SRC_BITS = 14
SRC_MASK = (1 << SRC_BITS) - 1
assert T_ALL <= 1 << SRC_BITS and 2 * Y_ROWS < 1 << (31 - SRC_BITS)


def _cparams(n_axes):
    return pltpu.CompilerParams(dimension_semantics=("arbitrary",) * n_axes,
                                vmem_limit_bytes=VMEM_LIMIT)


def _sigmoid(x):
    return 1.0 / (1.0 + jnp.exp(-x))


def _gelu_tanh(x):
    c = math.sqrt(2.0 / math.pi)
    return 0.5 * x * (1.0 + jnp.tanh(c * (x + 0.044715 * (x * x * x))))


IN_TM = 1024
IN_TN = 512


def _inproj_body(x_ref, g_ref, w_ref, *rest):
    o_ref, xn_ref = rest[-2], rest[-1]
    @pl.when(pl.program_id(1) == 0)
    def _():
        x = x_ref[...]
        inv = lax.rsqrt(jnp.mean(x * x, axis=-1, keepdims=True) + EPS)
        xn_ref[...] = ((x * inv) * g_ref[...]).astype(BF16)
    o_ref[...] = jnp.dot(xn_ref[...], w_ref[...], preferred_element_type=F32)


def _inproj(x, g, w_bf, row_block0, dst=None):
    n_i = x.shape[0] // IN_TM
    in_specs = [
        pl.BlockSpec((IN_TM, D_MODEL), lambda i, j: (i, 0)),
        pl.BlockSpec((1, D_MODEL), lambda i, j: (0, 0)),
        pl.BlockSpec((D_MODEL, IN_TN), lambda i, j: (0, j)),
    ]
    args = [x, g, w_bf]
    aliases = {}
    if dst is not None:
        in_specs.append(pl.BlockSpec(memory_space=pl.ANY))
        args.append(dst)
        aliases = {3: 0}
    return pl.pallas_call(
        _inproj_body,
        grid=(n_i, IN_WIDTH // IN_TN),
        in_specs=in_specs,
        out_specs=pl.BlockSpec((IN_TM, IN_TN), lambda i, j: (i + row_block0, j)),
        out_shape=jax.ShapeDtypeStruct((T_ALL, IN_WIDTH), F32),
        scratch_shapes=[pltpu.VMEM((IN_TM, D_MODEL), BF16)],
        input_output_aliases=aliases,
        compiler_params=_cparams(2),
        name="inproj",
    )(*args)


PP_TM = 512
HIST = 16


def _pool_project(pooled_g, g, w_ref, sc_ref, o_ref):
    y = jnp.dot(pooled_g.astype(BF16), w_ref[g], preferred_element_type=F32)
    lo, hi = g * POOL_OUT_CH, (g + 1) * POOL_OUT_CH
    o_ref[:, lo:hi] = y * sc_ref[:, lo:hi]


def _pool_prompt_body(u_ref, w_ref, sc_ref, o_ref, tail_ref, hist_ref):
    i = pl.program_id(1)
    @pl.when(i == 0)
    def _():
        hist_ref[...] = jnp.zeros_like(hist_ref)
    u = u_ref[...]
    ext = jnp.concatenate([hist_ref[...], u], axis=0)
    hist_ref[...] = u[PP_TM - HIST:, :]
    tail_ref[...] = u[PP_TM - HIST:, :]
    pos = i * PP_TM + lax.broadcasted_iota(I32, (PP_TM, 1), 0)
    for g, w in enumerate(POOL_WINDOWS):
        lo, hi = g * POOL_GROUP_CH, (g + 1) * POOL_GROUP_CH
        s = ext[:, lo:hi]
        d = 1
        while d < w:
            s = s + pltpu.roll(s, d, axis=0)
            d *= 2
        cnt = jnp.minimum(w, pos + 1).astype(F32)
        pooled = s[HIST:, :] / cnt - u[:, lo:hi]
        _pool_project(pooled, g, w_ref, sc_ref, o_ref)


def _pool_prompt(proj, w_pool_bf, pool_scale):
    n_i = SEQ // PP_TM
    return pl.pallas_call(
        _pool_prompt_body,
        grid=(BATCH, n_i),
        in_specs=[
            pl.BlockSpec((PP_TM, POOL_WIDTH), lambda b, i: (b * n_i + i, 0)),
            pl.BlockSpec((POOL_GROUPS, POOL_GROUP_CH, POOL_OUT_CH), lambda b, i: (0, 0, 0)),
            pl.BlockSpec((1, D_MODEL), lambda b, i: (0, 0)),
        ],
        out_specs=(
            pl.BlockSpec((PP_TM, D_MODEL), lambda b, i: (b * n_i + i, 0)),
            pl.BlockSpec((None, HIST, POOL_WIDTH), lambda b, i: (b, 0, 0)),
        ),
        out_shape=(
            jax.ShapeDtypeStruct((T_ALL, D_MODEL), F32),
            jax.ShapeDtypeStruct((BATCH, HIST, POOL_WIDTH), F32),
        ),
        scratch_shapes=[pltpu.VMEM((HIST, POOL_WIDTH), F32)],
        compiler_params=_cparams(2),
        name="pool_prompt",
    )(proj, w_pool_bf, pool_scale)


def _pool_sample_body(u_ref, hist_ref, w_ref, sc_ref, _dst, o_ref):
    rows = [hist_ref[k] for k in range(POOL_BUF)]
    rows += [u_ref[DEC_BATCH * t:DEC_BATCH * (t + 1), :] for t in range(DEC_SEQ)]
    n = len(rows)
    for g, w in enumerate(POOL_WINDOWS):
        lo, hi = g * POOL_GROUP_CH, (g + 1) * POOL_GROUP_CH
        f = [r[:, lo:hi] for r in rows]
        cur = f
        d = 1
        while d < w:
            cur = [cur[k] + cur[k - d] if k - d >= 0 else cur[k] for k in range(n)]
            d *= 2
        pooled = jnp.concatenate(
            [cur[POOL_BUF + t] / float(w) - f[POOL_BUF + t] for t in range(DEC_SEQ)], axis=0)
        _pool_project(pooled, g, w_ref, sc_ref, o_ref)


def _pool_sample(proj, hist_tm, w_pool_bf, pool_scale, y_pool):
    blk = T_PROMPT // T_SAMPLE
    return pl.pallas_call(
        _pool_sample_body,
        grid=(1,),
        in_specs=[
            pl.BlockSpec((T_SAMPLE, POOL_WIDTH), lambda i: (blk, 0)),
            pl.BlockSpec((POOL_BUF, DEC_BATCH, POOL_WIDTH), lambda i: (0, 0, 0)),
            pl.BlockSpec((POOL_GROUPS, POOL_GROUP_CH, POOL_OUT_CH), lambda i: (0, 0, 0)),
            pl.BlockSpec((1, D_MODEL), lambda i: (0, 0)),
            pl.BlockSpec(memory_space=pl.ANY),
        ],
        out_specs=pl.BlockSpec((T_SAMPLE, D_MODEL), lambda i: (blk, 0)),
        out_shape=jax.ShapeDtypeStruct((T_ALL, D_MODEL), F32),
        input_output_aliases={4: 0},
        compiler_params=_cparams(1),
        name="pool_sample",
    )(proj, hist_tm, w_pool_bf, pool_scale, y_pool)


def _ssm_tables(a_re, a_im, log_dt, b_re, b_im, c_re, c_im, d_skip):
    dt = jnp.exp(log_dt)[:, None]
    lr, li = a_re, a_im
    ab_re = jnp.exp(lr * dt) * jnp.cos(li * dt)
    ab_im = jnp.exp(lr * dt) * jnp.sin(li * dt)
    den = lr * lr + li * li
    nr, ni = ab_re - 1.0, ab_im
    q_re = (nr * lr + ni * li) / den
    q_im = (ni * lr - nr * li) / den
    bb_re = q_re[..., None] * b_re - q_im[..., None] * b_im
    bb_im = q_re[..., None] * b_im + q_im[..., None] * b_re

    def lam_rows(ks):
        k = jnp.asarray(ks, F32)[:, None, None]
        m = jnp.exp(k * lr * dt)
        re = (m * jnp.cos(k * li * dt)).reshape(len(ks), N_OCT, OCT_STATES)
        im = (m * jnp.sin(k * li * dt)).reshape(len(ks), N_OCT, OCT_STATES)
        return jnp.concatenate([re, im], axis=-1).transpose(1, 0, 2)

    def compact(re, im):
        v = jnp.concatenate([re, im], axis=-1)
        return v.reshape(N_OCT, OCT, 2 * SSM_STATE)

    bbc = compact(jnp.swapaxes(bb_re, 1, 2), jnp.swapaxes(bb_im, 1, 2))
    ccc = compact(c_re, c_im)
    pw = lam_rows(list(range(2 * SUBLANES)))
    r = jnp.arange(SUBLANES)[None, :, None]
    parts = [jnp.where(r >= dd, lam_rows([CHUNK * dd]), 0.0) for dd in (1, 2, 4)]
    parts.append(lam_rows([CHUNK * kk for kk in range(1, SUBLANES + 1)]))
    tab = jnp.concatenate(parts, axis=1)
    dsk = d_skip.reshape(N_OCT, 1, OCT)
    return bbc, ccc, pw, tab, dsk


def _split_bf16(x):
    hi = x.astype(BF16)
    return hi, (x - hi.astype(F32)).astype(BF16)


def _dot_nt(a, b):
    return lax.dot_general(a, b, (((1,), (1,)), ((), ())), preferred_element_type=F32)


def _build_weights(bbc_ref, ccc_ref, pw_ref, f_ref, gt_ref, m_ref):
    row_gi = lax.broadcasted_iota(I32, (OCT, 1), 0) >> 4
    col = lax.broadcasted_iota(I32, (1, SW), 1)
    col_gi = (col >> 6) & 7
    src = ((col >> 9) << 6) | (col & 63)
    k128 = lax.broadcasted_iota(I32, (2 * SSM_STATE, 1), 0)
    spread = jnp.where(k128 == src, 1.0, 0.0).astype(BF16)
    diag = row_gi == col_gi

    def expand(c_ref):
        hi, lo = _split_bf16(c_ref[...])
        d = (jnp.dot(hi, spread, preferred_element_type=F32)
             + jnp.dot(lo, spread, preferred_element_type=F32))
        d = jnp.where(diag, d, 0.0)
        return d[:, :OCT_STATES], d[:, OCT_STATES:]

    br, bi = expand(bbc_ref)
    cr, ci = expand(ccc_ref)
    chi_r, clo_r = _split_bf16(cr)
    chi_i, clo_i = _split_bf16(ci)

    def lam(k):
        return pw_ref[k:k + 1, :OCT_STATES], pw_ref[k:k + 1, OCT_STATES:]

    def dot3(a, bhi, blo):
        ahi, alo = _split_bf16(a)
        return _dot_nt(ahi, bhi) + _dot_nt(alo, bhi) + _dot_nt(ahi, blo)

    lags = []
    for k in range(CHUNK):
        pr, pi_ = lam(k)
        fr, fi = _cmul(br, bi, pr, pi_)
        s = CHUNK - 1 - k
        f_ref[s * OCT:(s + 1) * OCT, :] = jnp.concatenate([fr, fi], axis=1).astype(BF16)
        lags.append((dot3(fr, chi_r, clo_r) - dot3(fi, chi_i, clo_i)).astype(BF16))
        pr, pi_ = lam(k + 1)
        gr, gi = _cmul(cr, ci, pr, pi_)
        gt_ref[k * OCT:(k + 1) * OCT, :] = jnp.concatenate([gr, -gi], axis=1).astype(BF16)
    zero = jnp.zeros((OCT, OCT), BF16)
    for s in range(CHUNK):
        for t in range(CHUNK):
            m_ref[s * OCT:(s + 1) * OCT, t * OCT:(t + 1) * OCT] = lags[t - s] if t >= s else zero


def _cmul(ar, ai, br, bi):
    return ar * br - ai * bi, ar * bi + ai * br


def _chunk_scan(sloc, tab_ref):
    R = sloc.shape[0]
    nb = R // SUBLANES
    sr, si = sloc[:, :OCT_STATES], sloc[:, OCT_STATES:]
    rowi = lax.broadcasted_iota(I32, (R, 1), 0)
    tr = jnp.where(rowi == 0, 0.0, pltpu.roll(sr, 1, axis=0))
    ti = jnp.where(rowi == 0, 0.0, pltpu.roll(si, 1, axis=0))
    for lvl, d in enumerate((1, 2, 4)):
        mr = tab_ref[lvl * SUBLANES:(lvl + 1) * SUBLANES, :OCT_STATES]
        mi = tab_ref[lvl * SUBLANES:(lvl + 1) * SUBLANES, OCT_STATES:]
        mr = jnp.concatenate([mr] * nb, axis=0)
        mi = jnp.concatenate([mi] * nb, axis=0)
        pr, pi_ = _cmul(mr, mi, pltpu.roll(tr, d, axis=0), pltpu.roll(ti, d, axis=0))
        tr, ti = tr + pr, ti + pi_
    pwr = tab_ref[3 * SUBLANES:4 * SUBLANES, :OCT_STATES]
    pwi = tab_ref[3 * SUBLANES:4 * SUBLANES, OCT_STATES:]
    cr = jnp.zeros((1, OCT_STATES), F32)
    ci = jnp.zeros((1, OCT_STATES), F32)
    out_r, out_i = [], []
    for k in range(nb):
        ar = tr[k * SUBLANES:(k + 1) * SUBLANES, :]
        ai = ti[k * SUBLANES:(k + 1) * SUBLANES, :]
        pr, pi_ = _cmul(pwr, pwi, jnp.broadcast_to(cr, ar.shape), jnp.broadcast_to(ci, ai.shape))
        hr, hi = ar + pr, ai + pi_
        out_r.append(hr)
        out_i.append(hi)
        cr, ci = hr[SUBLANES - 1:, :], hi[SUBLANES - 1:, :]
    hin = jnp.concatenate([jnp.concatenate(out_r, axis=0), jnp.concatenate(out_i, axis=0)], axis=1)
    lr, li = pwr[0:1, :], pwi[0:1, :]
    fr, fi = _cmul(lr, li, cr, ci)
    fin = jnp.concatenate([fr + sr[R - 1:, :], fi + si[R - 1:, :]], axis=1)
    return hin, fin


def _ssm_prompt_body(u_ref, bbc_ref, ccc_ref, pw_ref, tab_ref, d_ref, y_ref, hout_ref,
                     f_ref, gt_ref, m_ref):
    @pl.when(pl.program_id(1) == 0)
    def _():
        _build_weights(bbc_ref, ccc_ref, pw_ref, f_ref, gt_ref, m_ref)
    R = SEQ // CHUNK
    xs = [u_ref[pl.ds(s, R, stride=CHUNK), :] for s in range(CHUNK)]
    xb = jnp.concatenate(xs, axis=1).astype(BF16)
    sloc = jnp.dot(xb, f_ref[...], preferred_element_type=F32)
    hin, fin = _chunk_scan(sloc, tab_ref)
    y = (jnp.dot(xb, m_ref[...], preferred_element_type=F32)
         + _dot_nt(hin.astype(BF16), gt_ref[...]))
    for t in range(CHUNK):
        yt = y[:, t * OCT:(t + 1) * OCT] + d_ref[...] * xs[t]
        y_ref[pl.ds(t, R, stride=CHUNK), :] = _gelu_tanh(yt)
    hout_ref[...] = fin


def _ssm_weight_specs(n_axes):
    if n_axes == 2:
        im3 = lambda o, b: (o, 0, 0)
    else:
        im3 = lambda o: (o, 0, 0)
    return [
        pl.BlockSpec((None, OCT, 2 * SSM_STATE), im3),
        pl.BlockSpec((None, OCT, 2 * SSM_STATE), im3),
        pl.BlockSpec((None, 2 * SUBLANES, SW), im3),
        pl.BlockSpec((None, 4 * SUBLANES, SW), im3),
        pl.BlockSpec((None, 1, OCT), im3),
    ]


_SSM_SCRATCH = [pltpu.VMEM((CW, SW), BF16), pltpu.VMEM((CW, SW), BF16), pltpu.VMEM((CW, CW), BF16)]


def _ssm_prompt(proj, tables):
    col0 = POOL_WIDTH // OCT
    return pl.pallas_call(
        _ssm_prompt_body,
        grid=(N_OCT, BATCH),
        in_specs=[pl.BlockSpec((SEQ, OCT), lambda o, b: (b, col0 + o))] + _ssm_weight_specs(2),
        out_specs=(
            pl.BlockSpec((SEQ, OCT), lambda o, b: (b, o)),
            pl.BlockSpec((None, 1, SW), lambda o, b: (b * N_OCT + o, 0, 0)),
        ),
        out_shape=(
            jax.ShapeDtypeStruct((T_ALL, SSM_WIDTH), F32),
            jax.ShapeDtypeStruct((BATCH * N_OCT, 1, SW), F32),
        ),
        scratch_shapes=_SSM_SCRATCH,
        compiler_params=_cparams(2),
        name="ssm_prompt",
    )(proj, *tables)


def _ssm_sample_body(u_ref, h0r_ref, h0i_ref, bbc_ref, ccc_ref, pw_ref, tab_ref, d_ref, _dst,
                     y_ref, hr_ref, hi_ref, f_ref, gt_ref, m_ref):
    _build_weights(bbc_ref, ccc_ref, pw_ref, f_ref, gt_ref, m_ref)
    B = DEC_BATCH
    xs = [u_ref[B * s:B * (s + 1), :] for s in range(CHUNK)]
    xb = jnp.concatenate(xs, axis=1).astype(BF16)
    sloc = jnp.dot(xb, f_ref[...], preferred_element_type=F32)
    h0r, h0i = h0r_ref[...], h0i_ref[...]
    hin = jnp.concatenate([h0r, h0i], axis=1).astype(BF16)
    y = (jnp.dot(xb, m_ref[...], preferred_element_type=F32)
         + _dot_nt(hin, gt_ref[...]))
    for t in range(CHUNK):
        yt = y[:, t * OCT:(t + 1) * OCT] + d_ref[...] * xs[t]
        y_ref[B * t:B * (t + 1), :] = _gelu_tanh(yt)
    lr = tab_ref[3 * SUBLANES:3 * SUBLANES + 1, :OCT_STATES]
    li = tab_ref[3 * SUBLANES:3 * SUBLANES + 1, OCT_STATES:]
    nr, ni = _cmul(lr, li, h0r, h0i)
    hr_ref[...] = nr + sloc[:, :OCT_STATES]
    hi_ref[...] = ni + sloc[:, OCT_STATES:]


def _ssm_sample(proj, h0r, h0i, tables, y_act):
    col0 = POOL_WIDTH // OCT
    blk = T_PROMPT // T_SAMPLE
    st_spec = pl.BlockSpec((DEC_BATCH, OCT_STATES), lambda o: (0, o))
    return pl.pallas_call(
        _ssm_sample_body,
        grid=(N_OCT,),
        in_specs=[pl.BlockSpec((T_SAMPLE, OCT), lambda o: (blk, col0 + o)), st_spec, st_spec]
        + _ssm_weight_specs(1) + [pl.BlockSpec(memory_space=pl.ANY)],
        out_specs=(pl.BlockSpec((T_SAMPLE, OCT), lambda o: (blk, o)), st_spec, st_spec),
        out_shape=(
            jax.ShapeDtypeStruct((T_ALL, SSM_WIDTH), F32),
            jax.ShapeDtypeStruct((DEC_BATCH, SSM_GROUPS * SSM_STATE), F32),
            jax.ShapeDtypeStruct((DEC_BATCH, SSM_GROUPS * SSM_STATE), F32),
        ),
        scratch_shapes=_SSM_SCRATCH,
        input_output_aliases={8: 0},
        compiler_params=_cparams(1),
        name="ssm_sample",
    )(proj, h0r, h0i, *tables, y_act)


PM_TM = 256
PM_PROMPT_BLOCKS = T_PROMPT // PM_TM


def _postmix_body(ya_ref, gp_ref, gs_ref, yp_ref, xp_ref, xs_ref, wa_ref, wb_ref, wo_ref,
                  gf_ref, wrh_ref, wrl_ref, br_ref, h_ref, tn_ref, rt_ref, cnt_out_ref, cnt_ref):
    i = pl.program_id(0)
    @pl.when(i == 0)
    def _():
        cnt_ref[...] = jnp.zeros_like(cnt_ref)
    ya = ya_ref[...].astype(BF16)
    a = jnp.dot(ya, wa_ref[...], preferred_element_type=F32)
    bg = jnp.dot(ya, wb_ref[...], preferred_element_type=F32)
    y_ssm = a * _sigmoid(bg)
    merged = _sigmoid(gp_ref[...]) * yp_ref[...] + _sigmoid(gs_ref[...]) * y_ssm
    x = jnp.where(i < PM_PROMPT_BLOCKS, xp_ref[...], xs_ref[...])
    h = x + jnp.dot(merged.astype(BF16), wo_ref[...], preferred_element_type=F32)
    h_ref[...] = h
    inv = lax.rsqrt(jnp.mean(h * h, axis=-1, keepdims=True) + EPS)
    tn = (h * inv) * gf_ref[...]
    tn_ref[...] = tn
    t_hi = tn.astype(BF16)
    t_lo = (tn - t_hi.astype(F32)).astype(BF16)
    wrh = wrh_ref[...]
    logits = (jnp.dot(t_hi, wrh, preferred_element_type=F32)
              + jnp.dot(t_lo, wrh, preferred_element_type=F32)
              + jnp.dot(t_hi, wrl_ref[...], preferred_element_type=F32)) + br_ref[...]
    lane = lax.broadcasted_iota(I32, (PM_TM, ROUTE_LANES), 1)
    neg = jnp.float32(-jnp.inf)
    big = jnp.int32(1 << 20)
    is_g = lane < MOE_GROUPS
    gmax = jnp.max(jnp.where(is_g, logits, neg), axis=1, keepdims=True)
    g_idx = jnp.min(jnp.where(is_g & (logits == gmax), lane, big), axis=1, keepdims=True)
    g_den = jnp.sum(jnp.where(is_g, jnp.exp(logits - gmax), 0.0), axis=1, keepdims=True)
    g_val = 1.0 / g_den
    e_lane = lane - EXP_LANE0
    sel = (e_lane >= 0) & (e_lane < MOE_EXPERTS) & ((e_lane >> 3) == g_idx)
    m1 = jnp.max(jnp.where(sel, logits, neg), axis=1, keepdims=True)
    i1 = jnp.min(jnp.where(sel & (logits == m1), lane, big), axis=1, keepdims=True)
    sel2 = sel & (lane != i1)
    m2 = jnp.max(jnp.where(sel2, logits, neg), axis=1, keepdims=True)
    i2 = jnp.min(jnp.where(sel2 & (logits == m2), lane, big), axis=1, keepdims=True)
    e2 = jnp.exp(m2 - m1)
    w1 = g_val / (1.0 + e2)
    w2 = g_val * e2 / (1.0 + e2)
    oh1 = lane == i1
    oh2 = lane == i2
    oh = jnp.where(oh1 | oh2, 1.0, 0.0)
    rr = lax.broadcasted_iota(I32, (PM_TM, PM_TM), 0)
    cc = lax.broadcasted_iota(I32, (PM_TM, PM_TM), 1)
    tri = jnp.where(cc < rr, 1.0, 0.0).astype(BF16)
    base = cnt_ref[...] + jnp.dot(tri, oh.astype(BF16), preferred_element_type=F32)
    rank1 = jnp.sum(jnp.where(oh1, base, 0.0), axis=1, keepdims=True)
    rank2 = jnp.sum(jnp.where(oh2, base, 0.0), axis=1, keepdims=True)
    cnt_ref[...] = cnt_ref[...] + jnp.sum(oh, axis=0, keepdims=True)
    cnt_out_ref[...] = cnt_ref[...]
    rt = jnp.where(lane == 0, w1, 0.0)
    rt = jnp.where(lane == 1, w2, rt)
    rt = jnp.where(lane == 2, rank1, rt)
    rt = jnp.where(lane == 3, rank2, rt)
    rt = jnp.where(lane == 4, (i1 - EXP_LANE0).astype(F32), rt)
    rt = jnp.where(lane == 5, (i2 - EXP_LANE0).astype(F32), rt)
    rt_ref[...] = rt


def _postmix(y_act, proj, y_pool, xp, xs, wa, wb, wo, g_ffn, wr_hi, wr_lo, b_r):
    n = T_ALL // PM_TM
    npb = PM_PROMPT_BLOCKS
    const2 = lambda i: (0, 0)
    return pl.pallas_call(
        _postmix_body,
        grid=(n,),
        in_specs=[
            pl.BlockSpec((PM_TM, SSM_WIDTH), lambda i: (i, 0)),
            pl.BlockSpec((PM_TM, D_MODEL), lambda i: (i, 1)),
            pl.BlockSpec((PM_TM, D_MODEL), lambda i: (i, 2)),
            pl.BlockSpec((PM_TM, D_MODEL), lambda i: (i, 0)),
            pl.BlockSpec((PM_TM, D_MODEL), lambda i: (jnp.minimum(i, npb - 1), 0)),
            pl.BlockSpec((PM_TM, D_MODEL), lambda i: (jnp.maximum(i - npb, 0), 0)),
            pl.BlockSpec((SSM_WIDTH, D_MODEL), const2, pipeline_mode=pl.Buffered(1)),
            pl.BlockSpec((SSM_WIDTH, D_MODEL), const2, pipeline_mode=pl.Buffered(1)),
            pl.BlockSpec((D_MODEL, D_MODEL), const2, pipeline_mode=pl.Buffered(1)),
            pl.BlockSpec((1, D_MODEL), const2),
            pl.BlockSpec((D_MODEL, ROUTE_LANES), const2),
            pl.BlockSpec((D_MODEL, ROUTE_LANES), const2),
            pl.BlockSpec((1, ROUTE_LANES), const2),
        ],
        out_specs=(
            pl.BlockSpec((PM_TM, D_MODEL), lambda i: (i, 0)),
            pl.BlockSpec((PM_TM, D_MODEL), lambda i: (i, 0)),
            pl.BlockSpec((PM_TM, ROUTE_LANES), lambda i: (i, 0)),
            pl.BlockSpec((1, ROUTE_LANES), const2),
        ),
        out_shape=(
            jax.ShapeDtypeStruct((T_ALL, D_MODEL), F32),
            jax.ShapeDtypeStruct((T_ALL, D_MODEL), F32),
            jax.ShapeDtypeStruct((T_ALL, ROUTE_LANES), F32),
            jax.ShapeDtypeStruct((1, ROUTE_LANES), F32),
        ),
        scratch_shapes=[pltpu.VMEM((1, ROUTE_LANES), F32)],
        compiler_params=_cparams(1),
        name="postmix",
    )(y_act, proj, proj, y_pool, xp, xs, wa, wb, wo, g_ffn, wr_hi, wr_lo, b_r)


def _expert_body(q_ref, ie_ref, q0_ref, ni_ref, t_hbm, wg_ref, wu_ref, wd_ref, y_hbm,
                 plan_ref, xs_ref, yb_ref, wgb_ref, wub_ref, wdb_ref, gsem, ssem):
    w = pl.program_id(0)
    n_items = ni_ref[0]
    slot = w & 1

    def gather_copy(tok, s, r):
        return pltpu.make_async_copy(t_hbm.at[pl.ds(tok, 1)], xs_ref.at[s, pl.ds(r, 1)], gsem.at[s])

    def scatter_copy(s, r, row):
        return pltpu.make_async_copy(yb_ref.at[s, pl.ds(r, 1)], y_hbm.at[pl.ds(row, 1)], ssem.at[s])

    def start_gather(item, s):
        q0 = q0_ref[item]
        for r in range(TME):
            gather_copy(plan_ref[q0 + r] & SRC_MASK, s, r).start()

    @pl.when(w == 0)
    def _():
        def tail(i, c):
            plan_ref[N_ASSIGN + i] = (T_ALL + i) << SRC_BITS
            return c
        lax.fori_loop(0, TME, tail, 0)
        def invert(a, c):
            tok = a >> 1
            plan_ref[q_ref[a]] = (((a & 1) * Y_ROWS + tok) << SRC_BITS) | tok
            return c
        lax.fori_loop(0, N_ASSIGN, invert, 0, unroll=8)
        start_gather(0, 0)

    @pl.when(w < n_items)
    def _():
        e = ie_ref[w]
        prev = ie_ref[jnp.maximum(w - 1, 0)]
        @pl.when((w == 0) | (prev != e))
        def _():
            wgb_ref[...] = wg_ref[...].astype(BF16)
            wub_ref[...] = wu_ref[...].astype(BF16)
            wdb_ref[...] = wd_ref[...].astype(BF16)
        for r in range(TME):
            gather_copy(0, slot, r).wait()
        @pl.when(w + 1 < n_items)
        def _():
            start_gather(w + 1, 1 - slot)

        x = xs_ref[slot].astype(BF16)
        hg = jnp.dot(x, wgb_ref[...], preferred_element_type=F32)
        hu = jnp.dot(x, wub_ref[...], preferred_element_type=F32)
        act = (hg * _sigmoid(hg)) * hu
        yb_ref[slot] = jnp.dot(act.astype(BF16), wdb_ref[...], preferred_element_type=F32)

        @pl.when(w > 0)
        def _():
            for r in range(TME):
                scatter_copy(1 - slot, r, 0).wait()
        q0 = q0_ref[w]
        for r in range(TME):
            scatter_copy(slot, r, plan_ref[q0 + r] >> SRC_BITS).start()
        @pl.when(w == n_items - 1)
        def _():
            for r in range(TME):
                scatter_copy(slot, r, 0).wait()


def _experts(q_flat, item_e, item_q0, n_items, tn, w_eg, w_eu, w_ed):
    wmap = lambda w, q, ie, q0, ni: (ie[w], 0, 0)
    grid_spec = pltpu.PrefetchScalarGridSpec(
        num_scalar_prefetch=4,
        grid=(N_ITEMS_MAX,),
        in_specs=[
            pl.BlockSpec(memory_space=pl.ANY),
            pl.BlockSpec((None, D_MODEL, MOE_FF), wmap),
            pl.BlockSpec((None, D_MODEL, MOE_FF), wmap),
            pl.BlockSpec((None, MOE_FF, D_MODEL), wmap),
        ],
        out_specs=pl.BlockSpec(memory_space=pl.ANY),
        scratch_shapes=[
            pltpu.SMEM((N_ASSIGN + TME,), I32),
            pltpu.VMEM((2, TME, D_MODEL), F32),
            pltpu.VMEM((2, TME, D_MODEL), F32),
            pltpu.VMEM((D_MODEL, MOE_FF), BF16),
            pltpu.VMEM((D_MODEL, MOE_FF), BF16),
            pltpu.VMEM((MOE_FF, D_MODEL), BF16),
            pltpu.SemaphoreType.DMA((2,)),
            pltpu.SemaphoreType.DMA((2,)),
        ],
    )
    return pl.pallas_call(
        _expert_body,
        grid_spec=grid_spec,
        out_shape=jax.ShapeDtypeStruct((2 * Y_ROWS, D_MODEL), F32),
        compiler_params=_cparams(1),
        name="experts",
    )(q_flat, item_e, item_q0, n_items, tn, w_eg, w_eu, w_ed)


FN_TM = 256
FN_PROMPT_BLOCKS = T_PROMPT // FN_TM


def _final_body(h_ref, y0_ref, y1_ref, rt_ref, g_ref, op_ref, os_ref):
    i = pl.program_id(0)
    rt = rt_ref[...]
    z = h_ref[...] + rt[:, 0:1] * y0_ref[...] + rt[:, 1:2] * y1_ref[...]
    inv = lax.rsqrt(jnp.mean(z * z, axis=-1, keepdims=True) + EPS)
    out = (z * inv) * g_ref[...]
    @pl.when(i < FN_PROMPT_BLOCKS)
    def _():
        op_ref[...] = out
    @pl.when(i >= FN_PROMPT_BLOCKS)
    def _():
        os_ref[...] = out


def _final(h, y, route, g_final):
    n = T_ALL // FN_TM
    npb = FN_PROMPT_BLOCKS
    yoff = Y_ROWS // FN_TM
    return pl.pallas_call(
        _final_body,
        grid=(n,),
        in_specs=[
            pl.BlockSpec((FN_TM, D_MODEL), lambda i: (i, 0)),
            pl.BlockSpec((FN_TM, D_MODEL), lambda i: (i, 0)),
            pl.BlockSpec((FN_TM, D_MODEL), lambda i: (yoff + i, 0)),
            pl.BlockSpec((FN_TM, ROUTE_LANES), lambda i: (i, 0)),
            pl.BlockSpec((1, D_MODEL), lambda i: (0, 0)),
        ],
        out_specs=(
            pl.BlockSpec((FN_TM, D_MODEL), lambda i: (jnp.minimum(i, npb - 1), 0)),
            pl.BlockSpec((FN_TM, D_MODEL), lambda i: (jnp.maximum(i - npb, 0), 0)),
        ),
        out_shape=(
            jax.ShapeDtypeStruct((T_PROMPT, D_MODEL), F32),
            jax.ShapeDtypeStruct((T_SAMPLE, D_MODEL), F32),
        ),
        compiler_params=_cparams(1),
        name="final",
    )(h, y, y, route, g_final)


def _dispatch_plan(route, cnt):
    counts = cnt[0, EXP_LANE0:EXP_LANE0 + MOE_EXPERTS].astype(I32)
    cum = jnp.cumsum(counts)
    cumex = cum - counts
    rank = route[:, 2:4].astype(I32)
    eid = route[:, 4:6].astype(I32)
    onehot = eid[..., None] == jnp.arange(MOE_EXPERTS, dtype=I32)
    q = rank + jnp.sum(jnp.where(onehot, cumex, 0), axis=-1)
    q_flat = q.reshape(N_ASSIGN)
    tiles = (counts + (TME - 1)) // TME
    cumt = jnp.cumsum(tiles)
    n_items = cumt[-1]
    w = jnp.arange(N_ITEMS_MAX, dtype=I32)
    w_eff = jnp.minimum(w, n_items - 1)
    item_e = jnp.sum((w_eff[:, None] >= cumt[None, :]).astype(I32), axis=1)
    j = w_eff - (cumt - tiles)[item_e]
    item_q0 = cumex[item_e] + TME * j
    return q_flat, item_e, item_q0, n_items.reshape(1)


def kernel(x_prompt, x_sample, state_pool, state_ssm_re, state_ssm_im, g_mix, w_in, w_pool,
           pool_scale, ssm_a_re, ssm_a_im, ssm_log_dt, ssm_b_re, ssm_b_im, ssm_c_re, ssm_c_im,
           ssm_d, w_glu_a, w_glu_b, w_out, g_ffn, w_router_group, b_router_group,
           w_router_expert, b_router_expert, w_exp_gate, w_exp_up, w_exp_down, g_final):
    l = 0
    xp = x_prompt.reshape(T_PROMPT, D_MODEL)
    xs = x_sample.transpose(1, 0, 2).reshape(T_SAMPLE, D_MODEL)
    w_in_bf = w_in[l].astype(BF16)
    w_pool_bf = w_pool[l].astype(BF16)
    wa_bf = w_glu_a[l].astype(BF16)
    wb_bf = w_glu_b[l].astype(BF16)
    wo_bf = w_out[l].astype(BF16)
    g_mix2 = g_mix[l].reshape(1, D_MODEL)
    scale2 = pool_scale[l].reshape(1, D_MODEL)

    proj = _inproj(xp, g_mix2, w_in_bf, 0)
    proj = _inproj(xs, g_mix2, w_in_bf, T_PROMPT // IN_TM, dst=proj)

    y_pool, pool_tail = _pool_prompt(proj, w_pool_bf, scale2)
    hist_tm = state_pool[l].transpose(1, 0, 2)
    y_pool = _pool_sample(proj, hist_tm, w_pool_bf, scale2, y_pool)

    tables = _ssm_tables(ssm_a_re[l], ssm_a_im[l], ssm_log_dt[l], ssm_b_re[l], ssm_b_im[l],
                         ssm_c_re[l], ssm_c_im[l], ssm_d[l])
    y_act, h_prompt = _ssm_prompt(proj, tables)
    h0r = state_ssm_re[l].reshape(DEC_BATCH, SSM_GROUPS * SSM_STATE)
    h0i = state_ssm_im[l].reshape(DEC_BATCH, SSM_GROUPS * SSM_STATE)
    y_act, hs_re, hs_im = _ssm_sample(proj, h0r, h0i, tables, y_act)

    w_r = jnp.zeros((D_MODEL, ROUTE_LANES), F32)
    w_r = w_r.at[:, :MOE_GROUPS].set(w_router_group[l])
    w_r = w_r.at[:, EXP_LANE0:EXP_LANE0 + MOE_EXPERTS].set(w_router_expert[l])
    wr_hi = w_r.astype(BF16)
    wr_lo = (w_r - wr_hi.astype(F32)).astype(BF16)
    b_r = jnp.zeros((1, ROUTE_LANES), F32)
    b_r = b_r.at[0, :MOE_GROUPS].set(b_router_group[l])
    b_r = b_r.at[0, EXP_LANE0:EXP_LANE0 + MOE_EXPERTS].set(b_router_expert[l])

    h, tn, route, cnt = _postmix(y_act, proj, y_pool, xp, xs, wa_bf, wb_bf, wo_bf,
                                 g_ffn[l].reshape(1, D_MODEL), wr_hi, wr_lo, b_r)
    plan = _dispatch_plan(route, cnt)
    y = _experts(*plan, tn, w_exp_gate[l], w_exp_up[l], w_exp_down[l])
    yp, ys = _final(h, y, route, g_final.reshape(1, D_MODEL))

    y_prompt = yp.reshape(BATCH, SEQ, D_MODEL)
    y_sample = ys.reshape(DEC_SEQ, DEC_BATCH, D_MODEL).transpose(1, 0, 2)
    new_pool_prompt = pool_tail[:, HIST - POOL_BUF:, :][None]
    us = proj[T_PROMPT:, :POOL_WIDTH].reshape(DEC_SEQ, DEC_BATCH, POOL_WIDTH).transpose(1, 0, 2)
    new_pool_sample = jnp.concatenate([state_pool[l][:, DEC_SEQ:, :], us], axis=1)[None]
    hp = h_prompt.reshape(BATCH, N_OCT, 2, OCT_GROUPS, SSM_STATE).transpose(2, 0, 1, 3, 4)
    hp = hp.reshape(2, BATCH, SSM_GROUPS, SSM_STATE)
    shp = (1, DEC_BATCH, SSM_GROUPS, SSM_STATE)
    return (y_prompt, y_sample, new_pool_prompt, hp[0][None], hp[1][None], new_pool_sample,
            hs_re.reshape(shp), hs_im.reshape(shp))
```

```python
import functools
import math

import jax
import jax.numpy as jnp
from jax import lax
from jax.experimental import pallas as pl
from jax.experimental.pallas import tpu as pltpu

F32 = jnp.float32
BF16 = jnp.bfloat16
I32 = jnp.int32
U32 = jnp.uint32

D_MODEL = 2048
BATCH = 4
SEQ = 2048
DEC_BATCH = 128
DEC_SEQ = 8
PAST_LEN = 16384
POOL_WIDTH = D_MODEL // 2
POOL_WINDOWS = (2, 4, 8, 16)
POOL_GROUPS = len(POOL_WINDOWS)
POOL_GROUP_CH = POOL_WIDTH // POOL_GROUPS
POOL_OUT_CH = D_MODEL // POOL_GROUPS
POOL_BUF = max(POOL_WINDOWS) - 1
SSM_WIDTH = D_MODEL // 2
SSM_GROUP_CH = 16
SSM_GROUPS = SSM_WIDTH // SSM_GROUP_CH
SSM_STATE = 64
IN_WIDTH = POOL_WIDTH + SSM_WIDTH + 2 * D_MODEL
D_PACK = D_MODEL // 2
MOE_GROUPS = 4
MOE_EPG = 8
MOE_EXPERTS = MOE_GROUPS * MOE_EPG
MOE_FF = D_MODEL // 4
EPS = 1e-6

T_PROMPT = BATCH * SEQ
T_SAMPLE = DEC_BATCH * DEC_SEQ
T_ALL = T_PROMPT + T_SAMPLE

LANES = 128
SUBLANES = 8
VMEM_LIMIT = 56 * 1024 * 1024

CHUNK = 8
OCT = LANES
N_OCT = SSM_WIDTH // OCT
OCT_GROUPS = OCT // SSM_GROUP_CH
OCT_STATES = OCT_GROUPS * SSM_STATE
CW = CHUNK * OCT
SW = 2 * OCT_STATES

ROUTE_LANES = LANES
EXP_LANE0 = MOE_GROUPS
N_ASSIGN = 2 * T_ALL
TME = 256
N_ITEMS_MAX = N_ASSIGN // TME + MOE_EXPERTS
Y_PAD = 1024
Y_ROWS = T_ALL + Y_PAD
SRC_BITS = 14
SRC_MASK = (1 << SRC_BITS) - 1
assert T_ALL <= 1 << SRC_BITS and 2 * Y_ROWS < 1 << (31 - SRC_BITS)


def _cparams(n_axes):
    return pltpu.CompilerParams(dimension_semantics=("arbitrary",) * n_axes,
                                vmem_limit_bytes=VMEM_LIMIT)


def _sigmoid(x):
    return 1.0 / (1.0 + jnp.exp(-x))


def _pack_pairs(x):
    c = x.shape[1] // 2
    hi = lax.bitcast_convert_type(x[:, :c].astype(BF16).astype(F32), U32)
    lo = lax.bitcast_convert_type(x[:, c:].astype(BF16).astype(F32), U32)
    return hi | (lo >> 16)


def _unpack_pairs(u, dtype):
    hi = lax.bitcast_convert_type(u & jnp.uint32(0xFFFF0000), F32)
    lo = lax.bitcast_convert_type(u << 16, F32)
    return jnp.concatenate([hi, lo], axis=1).astype(dtype)


def _gelu_tanh(x):
    c = math.sqrt(2.0 / math.pi)
    return 0.5 * x * (1.0 + jnp.tanh(c * (x + 0.044715 * (x * x * x))))


IN_TM = 1024
IN_TN = 512


def _inproj_body(x_ref, g_ref, w_ref, *rest):
    o_ref, xn_ref = rest[-2], rest[-1]
    @pl.when(pl.program_id(1) == 0)
    def _():
        x = x_ref[...]
        inv = lax.rsqrt(jnp.mean(x * x, axis=-1, keepdims=True) + EPS)
        xn_ref[...] = ((x * inv) * g_ref[...]).astype(BF16)
    o_ref[...] = jnp.dot(xn_ref[...], w_ref[...], preferred_element_type=F32)


def _inproj(x, g, w_bf, row_block0, dst=None):
    n_i = x.shape[0] // IN_TM
    in_specs = [
        pl.BlockSpec((IN_TM, D_MODEL), lambda i, j: (i, 0)),
        pl.BlockSpec((1, D_MODEL), lambda i, j: (0, 0)),
        pl.BlockSpec((D_MODEL, IN_TN), lambda i, j: (0, j)),
    ]
    args = [x, g, w_bf]
    aliases = {}
    if dst is not None:
        in_specs.append(pl.BlockSpec(memory_space=pl.ANY))
        args.append(dst)
        aliases = {3: 0}
    return pl.pallas_call(
        _inproj_body,
        grid=(n_i, IN_WIDTH // IN_TN),
        in_specs=in_specs,
        out_specs=pl.BlockSpec((IN_TM, IN_TN), lambda i, j: (i + row_block0, j)),
        out_shape=jax.ShapeDtypeStruct((T_ALL, IN_WIDTH), F32),
        scratch_shapes=[pltpu.VMEM((IN_TM, D_MODEL), BF16)],
        input_output_aliases=aliases,
        compiler_params=_cparams(2),
        name="inproj",
    )(*args)


PP_TM = 512
HIST = 16


def _pool_project(pooled_g, g, w_ref, sc_ref, o_ref):
    y = jnp.dot(pooled_g.astype(BF16), w_ref[g], preferred_element_type=F32)
    lo, hi = g * POOL_OUT_CH, (g + 1) * POOL_OUT_CH
    o_ref[:, lo:hi] = y * sc_ref[:, lo:hi]


def _pool_prompt_body(u_ref, w_ref, sc_ref, o_ref, tail_ref, hist_ref):
    i = pl.program_id(1)
    @pl.when(i == 0)
    def _():
        hist_ref[...] = jnp.zeros_like(hist_ref)
    u = u_ref[...]
    ext = jnp.concatenate([hist_ref[...], u], axis=0)
    hist_ref[...] = u[PP_TM - HIST:, :]
    tail_ref[...] = u[PP_TM - HIST:, :]
    pos = i * PP_TM + lax.broadcasted_iota(I32, (PP_TM, 1), 0)
    for g, w in enumerate(POOL_WINDOWS):
        lo, hi = g * POOL_GROUP_CH, (g + 1) * POOL_GROUP_CH
        s = ext[:, lo:hi]
        d = 1
        while d < w:
            s = s + pltpu.roll(s, d, axis=0)
            d *= 2
        cnt = jnp.minimum(w, pos + 1).astype(F32)
        pooled = s[HIST:, :] / cnt - u[:, lo:hi]
        _pool_project(pooled, g, w_ref, sc_ref, o_ref)


def _pool_prompt(proj, w_pool_bf, pool_scale):
    n_i = SEQ // PP_TM
    return pl.pallas_call(
        _pool_prompt_body,
        grid=(BATCH, n_i),
        in_specs=[
            pl.BlockSpec((PP_TM, POOL_WIDTH), lambda b, i: (b * n_i + i, 0)),
            pl.BlockSpec((POOL_GROUPS, POOL_GROUP_CH, POOL_OUT_CH), lambda b, i: (0, 0, 0)),
            pl.BlockSpec((1, D_MODEL), lambda b, i: (0, 0)),
        ],
        out_specs=(
            pl.BlockSpec((PP_TM, D_MODEL), lambda b, i: (b * n_i + i, 0)),
            pl.BlockSpec((None, HIST, POOL_WIDTH), lambda b, i: (b, 0, 0)),
        ),
        out_shape=(
            jax.ShapeDtypeStruct((T_ALL, D_MODEL), F32),
            jax.ShapeDtypeStruct((BATCH, HIST, POOL_WIDTH), F32),
        ),
        scratch_shapes=[pltpu.VMEM((HIST, POOL_WIDTH), F32)],
        compiler_params=_cparams(2),
        name="pool_prompt",
    )(proj, w_pool_bf, pool_scale)


def _pool_sample_body(u_ref, hist_ref, w_ref, sc_ref, _dst, o_ref):
    rows = [hist_ref[k] for k in range(POOL_BUF)]
    rows += [u_ref[DEC_BATCH * t:DEC_BATCH * (t + 1), :] for t in range(DEC_SEQ)]
    n = len(rows)
    for g, w in enumerate(POOL_WINDOWS):
        lo, hi = g * POOL_GROUP_CH, (g + 1) * POOL_GROUP_CH
        f = [r[:, lo:hi] for r in rows]
        cur = f
        d = 1
        while d < w:
            cur = [cur[k] + cur[k - d] if k - d >= 0 else cur[k] for k in range(n)]
            d *= 2
        pooled = jnp.concatenate(
            [cur[POOL_BUF + t] / float(w) - f[POOL_BUF + t] for t in range(DEC_SEQ)], axis=0)
        _pool_project(pooled, g, w_ref, sc_ref, o_ref)


def _pool_sample(proj, hist_tm, w_pool_bf, pool_scale, y_pool):
    blk = T_PROMPT // T_SAMPLE
    return pl.pallas_call(
        _pool_sample_body,
        grid=(1,),
        in_specs=[
            pl.BlockSpec((T_SAMPLE, POOL_WIDTH), lambda i: (blk, 0)),
            pl.BlockSpec((POOL_BUF, DEC_BATCH, POOL_WIDTH), lambda i: (0, 0, 0)),
            pl.BlockSpec((POOL_GROUPS, POOL_GROUP_CH, POOL_OUT_CH), lambda i: (0, 0, 0)),
            pl.BlockSpec((1, D_MODEL), lambda i: (0, 0)),
            pl.BlockSpec(memory_space=pl.ANY),
        ],
        out_specs=pl.BlockSpec((T_SAMPLE, D_MODEL), lambda i: (blk, 0)),
        out_shape=jax.ShapeDtypeStruct((T_ALL, D_MODEL), F32),
        input_output_aliases={4: 0},
        compiler_params=_cparams(1),
        name="pool_sample",
    )(proj, hist_tm, w_pool_bf, pool_scale, y_pool)


def _ssm_tables(a_re, a_im, log_dt, b_re, b_im, c_re, c_im, d_skip):
    dt = jnp.exp(log_dt)[:, None]
    lr, li = a_re, a_im
    ab_re = jnp.exp(lr * dt) * jnp.cos(li * dt)
    ab_im = jnp.exp(lr * dt) * jnp.sin(li * dt)
    den = lr * lr + li * li
    nr, ni = ab_re - 1.0, ab_im
    q_re = (nr * lr + ni * li) / den
    q_im = (ni * lr - nr * li) / den
    bb_re = q_re[..., None] * b_re - q_im[..., None] * b_im
    bb_im = q_re[..., None] * b_im + q_im[..., None] * b_re

    def lam_rows(ks):
        k = jnp.asarray(ks, F32)[:, None, None]
        m = jnp.exp(k * lr * dt)
        re = (m * jnp.cos(k * li * dt)).reshape(len(ks), N_OCT, OCT_STATES)
        im = (m * jnp.sin(k * li * dt)).reshape(len(ks), N_OCT, OCT_STATES)
        return jnp.concatenate([re, im], axis=-1).transpose(1, 0, 2)

    def compact(re, im):
        v = jnp.concatenate([re, im], axis=-1)
        return v.reshape(N_OCT, OCT, 2 * SSM_STATE)

    bbc = compact(jnp.swapaxes(bb_re, 1, 2), jnp.swapaxes(bb_im, 1, 2))
    ccc = compact(c_re, c_im)
    pw = lam_rows(list(range(2 * SUBLANES)))
    r = jnp.arange(SUBLANES)[None, :, None]
    parts = [jnp.where(r >= dd, lam_rows([CHUNK * dd]), 0.0) for dd in (1, 2, 4)]
    parts.append(lam_rows([CHUNK * kk for kk in range(1, SUBLANES + 1)]))
    tab = jnp.concatenate(parts, axis=1)
    dsk = d_skip.reshape(N_OCT, 1, OCT)
    return bbc, ccc, pw, tab, dsk


def _split_bf16(x):
    hi = x.astype(BF16)
    return hi, (x - hi.astype(F32)).astype(BF16)


def _dot_nt(a, b):
    return lax.dot_general(a, b, (((1,), (1,)), ((), ())), preferred_element_type=F32)


def _build_weights(bbc_ref, ccc_ref, pw_ref, f_ref, gt_ref, m_ref):
    row_gi = lax.broadcasted_iota(I32, (OCT, 1), 0) >> 4
    col = lax.broadcasted_iota(I32, (1, SW), 1)
    col_gi = (col >> 6) & 7
    src = ((col >> 9) << 6) | (col & 63)
    k128 = lax.broadcasted_iota(I32, (2 * SSM_STATE, 1), 0)
    spread = jnp.where(k128 == src, 1.0, 0.0).astype(BF16)
    diag = row_gi == col_gi

    def expand(c_ref):
        hi, lo = _split_bf16(c_ref[...])
        d = (jnp.dot(hi, spread, preferred_element_type=F32)
             + jnp.dot(lo, spread, preferred_element_type=F32))
        d = jnp.where(diag, d, 0.0)
        return d[:, :OCT_STATES], d[:, OCT_STATES:]

    br, bi = expand(bbc_ref)
    cr, ci = expand(ccc_ref)
    chi_r, clo_r = _split_bf16(cr)
    chi_i, clo_i = _split_bf16(ci)

    def lam(k):
        return pw_ref[k:k + 1, :OCT_STATES], pw_ref[k:k + 1, OCT_STATES:]

    def dot3(a, bhi, blo):
        ahi, alo = _split_bf16(a)
        return _dot_nt(ahi, bhi) + _dot_nt(alo, bhi) + _dot_nt(ahi, blo)

    lags = []
    for k in range(CHUNK):
        pr, pi_ = lam(k)
        fr, fi = _cmul(br, bi, pr, pi_)
        s = CHUNK - 1 - k
        f_ref[s * OCT:(s + 1) * OCT, :] = jnp.concatenate([fr, fi], axis=1).astype(BF16)
        lags.append((dot3(fr, chi_r, clo_r) - dot3(fi, chi_i, clo_i)).astype(BF16))
        pr, pi_ = lam(k + 1)
        gr, gi = _cmul(cr, ci, pr, pi_)
        gt_ref[k * OCT:(k + 1) * OCT, :] = jnp.concatenate([gr, -gi], axis=1).astype(BF16)
    zero = jnp.zeros((OCT, OCT), BF16)
    for s in range(CHUNK):
        for t in range(CHUNK):
            m_ref[s * OCT:(s + 1) * OCT, t * OCT:(t + 1) * OCT] = lags[t - s] if t >= s else zero


def _cmul(ar, ai, br, bi):
    return ar * br - ai * bi, ar * bi + ai * br


def _chunk_scan(sloc, tab_ref):
    R = sloc.shape[0]
    nb = R // SUBLANES
    sr, si = sloc[:, :OCT_STATES], sloc[:, OCT_STATES:]
    rowi = lax.broadcasted_iota(I32, (R, 1), 0)
    tr = jnp.where(rowi == 0, 0.0, pltpu.roll(sr, 1, axis=0))
    ti = jnp.where(rowi == 0, 0.0, pltpu.roll(si, 1, axis=0))
    for lvl, d in enumerate((1, 2, 4)):
        mr = tab_ref[lvl * SUBLANES:(lvl + 1) * SUBLANES, :OCT_STATES]
        mi = tab_ref[lvl * SUBLANES:(lvl + 1) * SUBLANES, OCT_STATES:]
        mr = jnp.concatenate([mr] * nb, axis=0)
        mi = jnp.concatenate([mi] * nb, axis=0)
        pr, pi_ = _cmul(mr, mi, pltpu.roll(tr, d, axis=0), pltpu.roll(ti, d, axis=0))
        tr, ti = tr + pr, ti + pi_
    pwr = tab_ref[3 * SUBLANES:4 * SUBLANES, :OCT_STATES]
    pwi = tab_ref[3 * SUBLANES:4 * SUBLANES, OCT_STATES:]
    cr = jnp.zeros((1, OCT_STATES), F32)
    ci = jnp.zeros((1, OCT_STATES), F32)
    out_r, out_i = [], []
    for k in range(nb):
        ar = tr[k * SUBLANES:(k + 1) * SUBLANES, :]
        ai = ti[k * SUBLANES:(k + 1) * SUBLANES, :]
        pr, pi_ = _cmul(pwr, pwi, jnp.broadcast_to(cr, ar.shape), jnp.broadcast_to(ci, ai.shape))
        hr, hi = ar + pr, ai + pi_
        out_r.append(hr)
        out_i.append(hi)
        cr, ci = hr[SUBLANES - 1:, :], hi[SUBLANES - 1:, :]
    hin = jnp.concatenate([jnp.concatenate(out_r, axis=0), jnp.concatenate(out_i, axis=0)], axis=1)
    lr, li = pwr[0:1, :], pwi[0:1, :]
    fr, fi = _cmul(lr, li, cr, ci)
    fin = jnp.concatenate([fr + sr[R - 1:, :], fi + si[R - 1:, :]], axis=1)
    return hin, fin


def _ssm_prompt_body(u_ref, bbc_ref, ccc_ref, pw_ref, tab_ref, d_ref, y_ref, hout_ref,
                     f_ref, gt_ref, m_ref):
    @pl.when(pl.program_id(1) == 0)
    def _():
        _build_weights(bbc_ref, ccc_ref, pw_ref, f_ref, gt_ref, m_ref)
    R = SEQ // CHUNK
    xs = [u_ref[pl.ds(s, R, stride=CHUNK), :] for s in range(CHUNK)]
    xb = jnp.concatenate(xs, axis=1).astype(BF16)
    sloc = jnp.dot(xb, f_ref[...], preferred_element_type=F32)
    hin, fin = _chunk_scan(sloc, tab_ref)
    y = (jnp.dot(xb, m_ref[...], preferred_element_type=F32)
         + _dot_nt(hin.astype(BF16), gt_ref[...]))
    for t in range(CHUNK):
        yt = y[:, t * OCT:(t + 1) * OCT] + d_ref[...] * xs[t]
        y_ref[pl.ds(t, R, stride=CHUNK), :] = _gelu_tanh(yt)
    hout_ref[...] = fin


def _ssm_weight_specs(n_axes):
    if n_axes == 2:
        im3 = lambda o, b: (o, 0, 0)
    else:
        im3 = lambda o: (o, 0, 0)
    return [
        pl.BlockSpec((None, OCT, 2 * SSM_STATE), im3),
        pl.BlockSpec((None, OCT, 2 * SSM_STATE), im3),
        pl.BlockSpec((None, 2 * SUBLANES, SW), im3),
        pl.BlockSpec((None, 4 * SUBLANES, SW), im3),
        pl.BlockSpec((None, 1, OCT), im3),
    ]


_SSM_SCRATCH = [pltpu.VMEM((CW, SW), BF16), pltpu.VMEM((CW, SW), BF16), pltpu.VMEM((CW, CW), BF16)]


def _ssm_prompt(proj, tables):
    col0 = POOL_WIDTH // OCT
    return pl.pallas_call(
        _ssm_prompt_body,
        grid=(N_OCT, BATCH),
        in_specs=[pl.BlockSpec((SEQ, OCT), lambda o, b: (b, col0 + o))] + _ssm_weight_specs(2),
        out_specs=(
            pl.BlockSpec((SEQ, OCT), lambda o, b: (b, o)),
            pl.BlockSpec((None, 1, SW), lambda o, b: (b * N_OCT + o, 0, 0)),
        ),
        out_shape=(
            jax.ShapeDtypeStruct((T_ALL, SSM_WIDTH), F32),
            jax.ShapeDtypeStruct((BATCH * N_OCT, 1, SW), F32),
        ),
        scratch_shapes=_SSM_SCRATCH,
        compiler_params=_cparams(2),
        name="ssm_prompt",
    )(proj, *tables)


def _ssm_sample_body(u_ref, h0r_ref, h0i_ref, bbc_ref, ccc_ref, pw_ref, tab_ref, d_ref, _dst,
                     y_ref, hr_ref, hi_ref, f_ref, gt_ref, m_ref):
    _build_weights(bbc_ref, ccc_ref, pw_ref, f_ref, gt_ref, m_ref)
    B = DEC_BATCH
    xs = [u_ref[B * s:B * (s + 1), :] for s in range(CHUNK)]
    xb = jnp.concatenate(xs, axis=1).astype(BF16)
    sloc = jnp.dot(xb, f_ref[...], preferred_element_type=F32)
    h0r, h0i = h0r_ref[...], h0i_ref[...]
    hin = jnp.concatenate([h0r, h0i], axis=1).astype(BF16)
    y = (jnp.dot(xb, m_ref[...], preferred_element_type=F32)
         + _dot_nt(hin, gt_ref[...]))
    for t in range(CHUNK):
        yt = y[:, t * OCT:(t + 1) * OCT] + d_ref[...] * xs[t]
        y_ref[B * t:B * (t + 1), :] = _gelu_tanh(yt)
    lr = tab_ref[3 * SUBLANES:3 * SUBLANES + 1, :OCT_STATES]
    li = tab_ref[3 * SUBLANES:3 * SUBLANES + 1, OCT_STATES:]
    nr, ni = _cmul(lr, li, h0r, h0i)
    hr_ref[...] = nr + sloc[:, :OCT_STATES]
    hi_ref[...] = ni + sloc[:, OCT_STATES:]


def _ssm_sample(proj, h0r, h0i, tables, y_act):
    col0 = POOL_WIDTH // OCT
    blk = T_PROMPT // T_SAMPLE
    st_spec = pl.BlockSpec((DEC_BATCH, OCT_STATES), lambda o: (0, o))
    return pl.pallas_call(
        _ssm_sample_body,
        grid=(N_OCT,),
        in_specs=[pl.BlockSpec((T_SAMPLE, OCT), lambda o: (blk, col0 + o)), st_spec, st_spec]
        + _ssm_weight_specs(1) + [pl.BlockSpec(memory_space=pl.ANY)],
        out_specs=(pl.BlockSpec((T_SAMPLE, OCT), lambda o: (blk, o)), st_spec, st_spec),
        out_shape=(
            jax.ShapeDtypeStruct((T_ALL, SSM_WIDTH), F32),
            jax.ShapeDtypeStruct((DEC_BATCH, SSM_GROUPS * SSM_STATE), F32),
            jax.ShapeDtypeStruct((DEC_BATCH, SSM_GROUPS * SSM_STATE), F32),
        ),
        scratch_shapes=_SSM_SCRATCH,
        input_output_aliases={8: 0},
        compiler_params=_cparams(1),
        name="ssm_sample",
    )(proj, h0r, h0i, *tables, y_act)


PM_TM = 256
PM_PROMPT_BLOCKS = T_PROMPT // PM_TM


def _postmix_body(ya_ref, gp_ref, gs_ref, yp_ref, xp_ref, xs_ref, wa_ref, wb_ref, wo_ref,
                  gf_ref, wrh_ref, wrl_ref, br_ref, h_ref, tn_ref, rt_ref, cnt_out_ref, cnt_ref):
    i = pl.program_id(0)
    @pl.when(i == 0)
    def _():
        cnt_ref[...] = jnp.zeros_like(cnt_ref)
    ya = ya_ref[...].astype(BF16)
    a = jnp.dot(ya, wa_ref[...], preferred_element_type=F32)
    bg = jnp.dot(ya, wb_ref[...], preferred_element_type=F32)
    y_ssm = a * _sigmoid(bg)
    merged = _sigmoid(gp_ref[...]) * yp_ref[...] + _sigmoid(gs_ref[...]) * y_ssm
    x = jnp.where(i < PM_PROMPT_BLOCKS, xp_ref[...], xs_ref[...])
    h = x + jnp.dot(merged.astype(BF16), wo_ref[...], preferred_element_type=F32)
    h_ref[...] = h
    inv = lax.rsqrt(jnp.mean(h * h, axis=-1, keepdims=True) + EPS)
    tn = (h * inv) * gf_ref[...]
    tn_ref[...] = _pack_pairs(tn)
    t_hi = tn.astype(BF16)
    t_lo = (tn - t_hi.astype(F32)).astype(BF16)
    wrh = wrh_ref[...]
    logits = (jnp.dot(t_hi, wrh, preferred_element_type=F32)
              + jnp.dot(t_lo, wrh, preferred_element_type=F32)
              + jnp.dot(t_hi, wrl_ref[...], preferred_element_type=F32)) + br_ref[...]
    lane = lax.broadcasted_iota(I32, (PM_TM, ROUTE_LANES), 1)
    neg = jnp.float32(-jnp.inf)
    big = jnp.int32(1 << 20)
    is_g = lane < MOE_GROUPS
    gmax = jnp.max(jnp.where(is_g, logits, neg), axis=1, keepdims=True)
    g_idx = jnp.min(jnp.where(is_g & (logits == gmax), lane, big), axis=1, keepdims=True)
    g_den = jnp.sum(jnp.where(is_g, jnp.exp(logits - gmax), 0.0), axis=1, keepdims=True)
    g_val = 1.0 / g_den
    e_lane = lane - EXP_LANE0
    sel = (e_lane >= 0) & (e_lane < MOE_EXPERTS) & ((e_lane >> 3) == g_idx)
    m1 = jnp.max(jnp.where(sel, logits, neg), axis=1, keepdims=True)
    i1 = jnp.min(jnp.where(sel & (logits == m1), lane, big), axis=1, keepdims=True)
    sel2 = sel & (lane != i1)
    m2 = jnp.max(jnp.where(sel2, logits, neg), axis=1, keepdims=True)
    i2 = jnp.min(jnp.where(sel2 & (logits == m2), lane, big), axis=1, keepdims=True)
    e2 = jnp.exp(m2 - m1)
    w1 = g_val / (1.0 + e2)
    w2 = g_val * e2 / (1.0 + e2)
    oh1 = lane == i1
    oh2 = lane == i2
    oh = jnp.where(oh1 | oh2, 1.0, 0.0)
    rr = lax.broadcasted_iota(I32, (PM_TM, PM_TM), 0)
    cc = lax.broadcasted_iota(I32, (PM_TM, PM_TM), 1)
    tri = jnp.where(cc < rr, 1.0, 0.0).astype(BF16)
    base = cnt_ref[...] + jnp.dot(tri, oh.astype(BF16), preferred_element_type=F32)
    rank1 = jnp.sum(jnp.where(oh1, base, 0.0), axis=1, keepdims=True)
    rank2 = jnp.sum(jnp.where(oh2, base, 0.0), axis=1, keepdims=True)
    cnt_ref[...] = cnt_ref[...] + jnp.sum(oh, axis=0, keepdims=True)
    cnt_out_ref[...] = cnt_ref[...]
    rt = jnp.where(lane == 0, w1, 0.0)
    rt = jnp.where(lane == 1, w2, rt)
    rt = jnp.where(lane == 2, rank1, rt)
    rt = jnp.where(lane == 3, rank2, rt)
    rt = jnp.where(lane == 4, (i1 - EXP_LANE0).astype(F32), rt)
    rt = jnp.where(lane == 5, (i2 - EXP_LANE0).astype(F32), rt)
    rt_ref[...] = rt


def _postmix(y_act, proj, y_pool, xp, xs, wa, wb, wo, g_ffn, wr_hi, wr_lo, b_r):
    n = T_ALL // PM_TM
    npb = PM_PROMPT_BLOCKS
    const2 = lambda i: (0, 0)
    return pl.pallas_call(
        _postmix_body,
        grid=(n,),
        in_specs=[
            pl.BlockSpec((PM_TM, SSM_WIDTH), lambda i: (i, 0)),
            pl.BlockSpec((PM_TM, D_MODEL), lambda i: (i, 1)),
            pl.BlockSpec((PM_TM, D_MODEL), lambda i: (i, 2)),
            pl.BlockSpec((PM_TM, D_MODEL), lambda i: (i, 0)),
            pl.BlockSpec((PM_TM, D_MODEL), lambda i: (jnp.minimum(i, npb - 1), 0)),
            pl.BlockSpec((PM_TM, D_MODEL), lambda i: (jnp.maximum(i - npb, 0), 0)),
            pl.BlockSpec((SSM_WIDTH, D_MODEL), const2, pipeline_mode=pl.Buffered(1)),
            pl.BlockSpec((SSM_WIDTH, D_MODEL), const2, pipeline_mode=pl.Buffered(1)),
            pl.BlockSpec((D_MODEL, D_MODEL), const2, pipeline_mode=pl.Buffered(1)),
            pl.BlockSpec((1, D_MODEL), const2),
            pl.BlockSpec((D_MODEL, ROUTE_LANES), const2),
            pl.BlockSpec((D_MODEL, ROUTE_LANES), const2),
            pl.BlockSpec((1, ROUTE_LANES), const2),
        ],
        out_specs=(
            pl.BlockSpec((PM_TM, D_MODEL), lambda i: (i, 0)),
            pl.BlockSpec((PM_TM, D_PACK), lambda i: (i, 0)),
            pl.BlockSpec((PM_TM, ROUTE_LANES), lambda i: (i, 0)),
            pl.BlockSpec((1, ROUTE_LANES), const2),
        ),
        out_shape=(
            jax.ShapeDtypeStruct((T_ALL, D_MODEL), F32),
            jax.ShapeDtypeStruct((T_ALL, D_PACK), U32),
            jax.ShapeDtypeStruct((T_ALL, ROUTE_LANES), F32),
            jax.ShapeDtypeStruct((1, ROUTE_LANES), F32),
        ),
        scratch_shapes=[pltpu.VMEM((1, ROUTE_LANES), F32)],
        compiler_params=_cparams(1),
        name="postmix",
    )(y_act, proj, proj, y_pool, xp, xs, wa, wb, wo, g_ffn, wr_hi, wr_lo, b_r)


def _expert_body(q_ref, ie_ref, q0_ref, ni_ref, t_hbm, wg_ref, wu_ref, wd_ref, y_hbm,
                 plan_ref, xs_ref, yb_ref, wgb_ref, wub_ref, wdb_ref, gsem, ssem):
    w = pl.program_id(0)
    n_items = ni_ref[0]

    def gather_copy(tok, s, r):
        return pltpu.make_async_copy(t_hbm.at[pl.ds(tok, 1)], xs_ref.at[s, pl.ds(r, 1)], gsem.at[s])

    def scatter_copy(s, r, row):
        return pltpu.make_async_copy(yb_ref.at[s, pl.ds(r, 1)], y_hbm.at[pl.ds(row, 1)], ssem.at[s])

    def start_gather(item, s):
        q0 = q0_ref[item]
        for r in range(TME):
            gather_copy(plan_ref[q0 + r] & SRC_MASK, s, r).start()

    @pl.when(w == 0)
    def _():
        def tail(i, c):
            plan_ref[N_ASSIGN + i] = (T_ALL + i) << SRC_BITS
            return c
        lax.fori_loop(0, TME, tail, 0)
        def invert(a, c):
            tok = a >> 1
            plan_ref[q_ref[a]] = (((a & 1) * Y_ROWS + tok) << SRC_BITS) | tok
            return c
        lax.fori_loop(0, N_ASSIGN, invert, 0, unroll=8)
        start_gather(0, 0)

    def item(s):
        for r in range(TME):
            gather_copy(0, s, r).wait()
        @pl.when(w + 1 < n_items)
        def _():
            start_gather(w + 1, 1 - s)

        x = _unpack_pairs(xs_ref[s], BF16)
        hg = jnp.dot(x, wgb_ref[...], preferred_element_type=F32)
        hu = jnp.dot(x, wub_ref[...], preferred_element_type=F32)
        act = (hg * _sigmoid(hg)) * hu
        yb_ref[s] = _pack_pairs(jnp.dot(act.astype(BF16), wdb_ref[...], preferred_element_type=F32))

        @pl.when(w > 0)
        def _():
            for r in range(TME):
                scatter_copy(1 - s, r, 0).wait()
        q0 = q0_ref[w]
        for r in range(TME):
            scatter_copy(s, r, plan_ref[q0 + r] >> SRC_BITS).start(priority=r % 2)
        @pl.when(w == n_items - 1)
        def _():
            for r in range(TME):
                scatter_copy(s, r, 0).wait()

    @pl.when(w < n_items)
    def _():
        e = ie_ref[w]
        prev = ie_ref[jnp.maximum(w - 1, 0)]
        @pl.when((w == 0) | (prev != e))
        def _():
            wgb_ref[...] = wg_ref[...].astype(BF16)
            wub_ref[...] = wu_ref[...].astype(BF16)
            wdb_ref[...] = wd_ref[...].astype(BF16)
        for s in range(2):
            pl.when((w & 1) == s)(functools.partial(item, s))


def _experts(q_flat, item_e, item_q0, n_items, tn, w_eg, w_eu, w_ed):
    wmap = lambda w, q, ie, q0, ni: (ie[w], 0, 0)
    grid_spec = pltpu.PrefetchScalarGridSpec(
        num_scalar_prefetch=4,
        grid=(N_ITEMS_MAX,),
        in_specs=[
            pl.BlockSpec(memory_space=pl.ANY),
            pl.BlockSpec((None, D_MODEL, MOE_FF), wmap),
            pl.BlockSpec((None, D_MODEL, MOE_FF), wmap),
            pl.BlockSpec((None, MOE_FF, D_MODEL), wmap),
        ],
        out_specs=pl.BlockSpec(memory_space=pl.ANY),
        scratch_shapes=[
            pltpu.SMEM((N_ASSIGN + TME,), I32),
            pltpu.VMEM((2, TME, D_PACK), U32),
            pltpu.VMEM((2, TME, D_PACK), U32),
            pltpu.VMEM((D_MODEL, MOE_FF), BF16),
            pltpu.VMEM((D_MODEL, MOE_FF), BF16),
            pltpu.VMEM((MOE_FF, D_MODEL), BF16),
            pltpu.SemaphoreType.DMA((2,)),
            pltpu.SemaphoreType.DMA((2,)),
        ],
    )
    return pl.pallas_call(
        _expert_body,
        grid_spec=grid_spec,
        out_shape=jax.ShapeDtypeStruct((2 * Y_ROWS, D_PACK), U32),
        compiler_params=_cparams(1),
        name="experts",
    )(q_flat, item_e, item_q0, n_items, tn, w_eg, w_eu, w_ed)


FN_TM = 256
FN_PROMPT_BLOCKS = T_PROMPT // FN_TM


def _final_body(h_ref, y0_ref, y1_ref, rt_ref, g_ref, op_ref, os_ref):
    i = pl.program_id(0)
    rt = rt_ref[...]
    z = (h_ref[...] + rt[:, 0:1] * _unpack_pairs(y0_ref[...], F32)
         + rt[:, 1:2] * _unpack_pairs(y1_ref[...], F32))
    inv = lax.rsqrt(jnp.mean(z * z, axis=-1, keepdims=True) + EPS)
    out = (z * inv) * g_ref[...]
    @pl.when(i < FN_PROMPT_BLOCKS)
    def _():
        op_ref[...] = out
    @pl.when(i >= FN_PROMPT_BLOCKS)
    def _():
        os_ref[...] = out


def _final(h, y, route, g_final):
    n = T_ALL // FN_TM
    npb = FN_PROMPT_BLOCKS
    yoff = Y_ROWS // FN_TM
    return pl.pallas_call(
        _final_body,
        grid=(n,),
        in_specs=[
            pl.BlockSpec((FN_TM, D_MODEL), lambda i: (i, 0)),
            pl.BlockSpec((FN_TM, D_PACK), lambda i: (i, 0)),
            pl.BlockSpec((FN_TM, D_PACK), lambda i: (yoff + i, 0)),
            pl.BlockSpec((FN_TM, ROUTE_LANES), lambda i: (i, 0)),
            pl.BlockSpec((1, D_MODEL), lambda i: (0, 0)),
        ],
        out_specs=(
            pl.BlockSpec((FN_TM, D_MODEL), lambda i: (jnp.minimum(i, npb - 1), 0)),
            pl.BlockSpec((FN_TM, D_MODEL), lambda i: (jnp.maximum(i - npb, 0), 0)),
        ),
        out_shape=(
            jax.ShapeDtypeStruct((T_PROMPT, D_MODEL), F32),
            jax.ShapeDtypeStruct((T_SAMPLE, D_MODEL), F32),
        ),
        compiler_params=_cparams(1),
        name="final",
    )(h, y, y, route, g_final)


def _dispatch_plan(route, cnt):
    counts = cnt[0, EXP_LANE0:EXP_LANE0 + MOE_EXPERTS].astype(I32)
    cum = jnp.cumsum(counts)
    cumex = cum - counts
    rank = route[:, 2:4].astype(I32)
    eid = route[:, 4:6].astype(I32)
    onehot = eid[..., None] == jnp.arange(MOE_EXPERTS, dtype=I32)
    q = rank + jnp.sum(jnp.where(onehot, cumex, 0), axis=-1)
    q_flat = q.reshape(N_ASSIGN)
    tiles = (counts + (TME - 1)) // TME
    cumt = jnp.cumsum(tiles)
    n_items = cumt[-1]
    w = jnp.arange(N_ITEMS_MAX, dtype=I32)
    w_eff = jnp.minimum(w, n_items - 1)
    item_e = jnp.sum((w_eff[:, None] >= cumt[None, :]).astype(I32), axis=1)
    j = w_eff - (cumt - tiles)[item_e]
    item_q0 = cumex[item_e] + TME * j
    return q_flat, item_e, item_q0, n_items.reshape(1)


def kernel(x_prompt, x_sample, state_pool, state_ssm_re, state_ssm_im, g_mix, w_in, w_pool,
           pool_scale, ssm_a_re, ssm_a_im, ssm_log_dt, ssm_b_re, ssm_b_im, ssm_c_re, ssm_c_im,
           ssm_d, w_glu_a, w_glu_b, w_out, g_ffn, w_router_group, b_router_group,
           w_router_expert, b_router_expert, w_exp_gate, w_exp_up, w_exp_down, g_final):
    l = 0
    xp = x_prompt.reshape(T_PROMPT, D_MODEL)
    xs = x_sample.transpose(1, 0, 2).reshape(T_SAMPLE, D_MODEL)
    w_in_bf = w_in[l].astype(BF16)
    w_pool_bf = w_pool[l].astype(BF16)
    wa_bf = w_glu_a[l].astype(BF16)
    wb_bf = w_glu_b[l].astype(BF16)
    wo_bf = w_out[l].astype(BF16)
    g_mix2 = g_mix[l].reshape(1, D_MODEL)
    scale2 = pool_scale[l].reshape(1, D_MODEL)

    proj = _inproj(xp, g_mix2, w_in_bf, 0)
    proj = _inproj(xs, g_mix2, w_in_bf, T_PROMPT // IN_TM, dst=proj)

    y_pool, pool_tail = _pool_prompt(proj, w_pool_bf, scale2)
    hist_tm = state_pool[l].transpose(1, 0, 2)
    y_pool = _pool_sample(proj, hist_tm, w_pool_bf, scale2, y_pool)

    tables = _ssm_tables(ssm_a_re[l], ssm_a_im[l], ssm_log_dt[l], ssm_b_re[l], ssm_b_im[l],
                         ssm_c_re[l], ssm_c_im[l], ssm_d[l])
    y_act, h_prompt = _ssm_prompt(proj, tables)
    h0r = state_ssm_re[l].reshape(DEC_BATCH, SSM_GROUPS * SSM_STATE)
    h0i = state_ssm_im[l].reshape(DEC_BATCH, SSM_GROUPS * SSM_STATE)
    y_act, hs_re, hs_im = _ssm_sample(proj, h0r, h0i, tables, y_act)

    w_r = jnp.zeros((D_MODEL, ROUTE_LANES), F32)
    w_r = w_r.at[:, :MOE_GROUPS].set(w_router_group[l])
    w_r = w_r.at[:, EXP_LANE0:EXP_LANE0 + MOE_EXPERTS].set(w_router_expert[l])
    wr_hi = w_r.astype(BF16)
    wr_lo = (w_r - wr_hi.astype(F32)).astype(BF16)
    b_r = jnp.zeros((1, ROUTE_LANES), F32)
    b_r = b_r.at[0, :MOE_GROUPS].set(b_router_group[l])
    b_r = b_r.at[0, EXP_LANE0:EXP_LANE0 + MOE_EXPERTS].set(b_router_expert[l])

    h, tn, route, cnt = _postmix(y_act, proj, y_pool, xp, xs, wa_bf, wb_bf, wo_bf,
                                 g_ffn[l].reshape(1, D_MODEL), wr_hi, wr_lo, b_r)
    plan = _dispatch_plan(route, cnt)
    y = _experts(*plan, tn, w_exp_gate[l], w_exp_up[l], w_exp_down[l])
    yp, ys = _final(h, y, route, g_final.reshape(1, D_MODEL))

    y_prompt = yp.reshape(BATCH, SEQ, D_MODEL)
    y_sample = ys.reshape(DEC_SEQ, DEC_BATCH, D_MODEL).transpose(1, 0, 2)
    new_pool_prompt = pool_tail[:, HIST - POOL_BUF:, :][None]
    us = proj[T_PROMPT:, :POOL_WIDTH].reshape(DEC_SEQ, DEC_BATCH, POOL_WIDTH).transpose(1, 0, 2)
    new_pool_sample = jnp.concatenate([state_pool[l][:, DEC_SEQ:, :], us], axis=1)[None]
    hp = h_prompt.reshape(BATCH, N_OCT, 2, OCT_GROUPS, SSM_STATE).transpose(2, 0, 1, 3, 4)
    hp = hp.reshape(2, BATCH, SSM_GROUPS, SSM_STATE)
    shp = (1, DEC_BATCH, SSM_GROUPS, SSM_STATE)
    return (y_prompt, y_sample, new_pool_prompt, hp[0][None], hp[1][None], new_pool_sample,
            hs_re.reshape(shp), hs_im.reshape(shp))
```

```python
import functools
import math

import jax
import jax.numpy as jnp
from jax import lax
from jax.experimental import pallas as pl
from jax.experimental.pallas import tpu as pltpu

F32 = jnp.float32
BF16 = jnp.bfloat16
I32 = jnp.int32

D_MODEL = 2048
BATCH = 4
SEQ = 2048
DEC_BATCH = 128
DEC_SEQ = 8
PAST_LEN = 16384
POOL_WIDTH = D_MODEL // 2
POOL_WINDOWS = (2, 4, 8, 16)
POOL_GROUPS = len(POOL_WINDOWS)
POOL_GROUP_CH = POOL_WIDTH // POOL_GROUPS
POOL_OUT_CH = D_MODEL // POOL_GROUPS
POOL_BUF = max(POOL_WINDOWS) - 1
SSM_WIDTH = D_MODEL // 2
SSM_GROUP_CH = 16
SSM_GROUPS = SSM_WIDTH // SSM_GROUP_CH
SSM_STATE = 64
IN_WIDTH = POOL_WIDTH + SSM_WIDTH + 2 * D_MODEL
MOE_GROUPS = 4
MOE_EPG = 8
MOE_EXPERTS = MOE_GROUPS * MOE_EPG
MOE_FF = D_MODEL // 4
EPS = 1e-6

T_PROMPT = BATCH * SEQ
T_SAMPLE = DEC_BATCH * DEC_SEQ
T_ALL = T_PROMPT + T_SAMPLE

LANES = 128
SUBLANES = 8
VMEM_LIMIT = 56 * 1024 * 1024

CHUNK = 8
OCT = LANES
N_OCT = SSM_WIDTH // OCT
OCT_GROUPS = OCT // SSM_GROUP_CH
OCT_STATES = OCT_GROUPS * SSM_STATE
CW = CHUNK * OCT
SW = 2 * OCT_STATES

ROUTE_LANES = LANES
EXP_LANE0 = MOE_GROUPS
N_ASSIGN = 2 * T_ALL
TME = 256
N_ITEMS_MAX = N_ASSIGN // TME + MOE_EXPERTS
Y_PAD = 1024
Y_ROWS = T_ALL + Y_PAD
SRC_BITS = 14
SRC_MASK = (1 << SRC_BITS) - 1
assert T_ALL <= 1 << SRC_BITS and 2 * Y_ROWS < 1 << (31 - SRC_BITS)
EXP_UP_CHUNKS = 2
EXP_DOWN_CHUNKS = 4
EXP_PHASES = 2 * EXP_UP_CHUNKS + EXP_DOWN_CHUNKS
assert (2 * TME) % EXP_PHASES == 0


def _cparams(n_axes):
    return pltpu.CompilerParams(dimension_semantics=("arbitrary",) * n_axes,
                                vmem_limit_bytes=VMEM_LIMIT)


def _sigmoid(x):
    return 1.0 / (1.0 + jnp.exp(-x))


def _gelu_tanh(x):
    c = math.sqrt(2.0 / math.pi)
    return 0.5 * x * (1.0 + jnp.tanh(c * (x + 0.044715 * (x * x * x))))


IN_TM = 1024
IN_TN = 512


def _inproj_body(x_ref, g_ref, w_ref, *rest):
    o_ref, xn_ref = rest[-2], rest[-1]
    @pl.when(pl.program_id(1) == 0)
    def _():
        x = x_ref[...]
        inv = lax.rsqrt(jnp.mean(x * x, axis=-1, keepdims=True) + EPS)
        xn_ref[...] = ((x * inv) * g_ref[...]).astype(BF16)
    o_ref[...] = jnp.dot(xn_ref[...], w_ref[...], preferred_element_type=F32)


def _inproj(x, g, w_bf, row_block0, dst=None):
    n_i = x.shape[0] // IN_TM
    in_specs = [
        pl.BlockSpec((IN_TM, D_MODEL), lambda i, j: (i, 0)),
        pl.BlockSpec((1, D_MODEL), lambda i, j: (0, 0)),
        pl.BlockSpec((D_MODEL, IN_TN), lambda i, j: (0, j)),
    ]
    args = [x, g, w_bf]
    aliases = {}
    if dst is not None:
        in_specs.append(pl.BlockSpec(memory_space=pl.ANY))
        args.append(dst)
        aliases = {3: 0}
    return pl.pallas_call(
        _inproj_body,
        grid=(n_i, IN_WIDTH // IN_TN),
        in_specs=in_specs,
        out_specs=pl.BlockSpec((IN_TM, IN_TN), lambda i, j: (i + row_block0, j)),
        out_shape=jax.ShapeDtypeStruct((T_ALL, IN_WIDTH), F32),
        scratch_shapes=[pltpu.VMEM((IN_TM, D_MODEL), BF16)],
        input_output_aliases=aliases,
        compiler_params=_cparams(2),
        name="inproj",
    )(*args)


PP_TM = 512
HIST = 16


def _pool_project(pooled_g, g, w_ref, sc_ref, o_ref):
    y = jnp.dot(pooled_g.astype(BF16), w_ref[g], preferred_element_type=F32)
    lo, hi = g * POOL_OUT_CH, (g + 1) * POOL_OUT_CH
    o_ref[:, lo:hi] = y * sc_ref[:, lo:hi]


def _pool_prompt_body(u_ref, w_ref, sc_ref, o_ref, tail_ref, hist_ref):
    i = pl.program_id(1)
    @pl.when(i == 0)
    def _():
        hist_ref[...] = jnp.zeros_like(hist_ref)
    u = u_ref[...]
    ext = jnp.concatenate([hist_ref[...], u], axis=0)
    hist_ref[...] = u[PP_TM - HIST:, :]
    tail_ref[...] = u[PP_TM - HIST:, :]
    pos = i * PP_TM + lax.broadcasted_iota(I32, (PP_TM, 1), 0)
    for g, w in enumerate(POOL_WINDOWS):
        lo, hi = g * POOL_GROUP_CH, (g + 1) * POOL_GROUP_CH
        s = ext[:, lo:hi]
        d = 1
        while d < w:
            s = s + pltpu.roll(s, d, axis=0)
            d *= 2
        cnt = jnp.minimum(w, pos + 1).astype(F32)
        pooled = s[HIST:, :] / cnt - u[:, lo:hi]
        _pool_project(pooled, g, w_ref, sc_ref, o_ref)


def _pool_prompt(proj, w_pool_bf, pool_scale):
    n_i = SEQ // PP_TM
    return pl.pallas_call(
        _pool_prompt_body,
        grid=(BATCH, n_i),
        in_specs=[
            pl.BlockSpec((PP_TM, POOL_WIDTH), lambda b, i: (b * n_i + i, 0)),
            pl.BlockSpec((POOL_GROUPS, POOL_GROUP_CH, POOL_OUT_CH), lambda b, i: (0, 0, 0)),
            pl.BlockSpec((1, D_MODEL), lambda b, i: (0, 0)),
        ],
        out_specs=(
            pl.BlockSpec((PP_TM, D_MODEL), lambda b, i: (b * n_i + i, 0)),
            pl.BlockSpec((None, HIST, POOL_WIDTH), lambda b, i: (b, 0, 0)),
        ),
        out_shape=(
            jax.ShapeDtypeStruct((T_ALL, D_MODEL), F32),
            jax.ShapeDtypeStruct((BATCH, HIST, POOL_WIDTH), F32),
        ),
        scratch_shapes=[pltpu.VMEM((HIST, POOL_WIDTH), F32)],
        compiler_params=_cparams(2),
        name="pool_prompt",
    )(proj, w_pool_bf, pool_scale)


def _pool_sample_body(u_ref, hist_ref, w_ref, sc_ref, _dst, o_ref):
    rows = [hist_ref[k] for k in range(POOL_BUF)]
    rows += [u_ref[DEC_BATCH * t:DEC_BATCH * (t + 1), :] for t in range(DEC_SEQ)]
    n = len(rows)
    for g, w in enumerate(POOL_WINDOWS):
        lo, hi = g * POOL_GROUP_CH, (g + 1) * POOL_GROUP_CH
        f = [r[:, lo:hi] for r in rows]
        cur = f
        d = 1
        while d < w:
            cur = [cur[k] + cur[k - d] if k - d >= 0 else cur[k] for k in range(n)]
            d *= 2
        pooled = jnp.concatenate(
            [cur[POOL_BUF + t] / float(w) - f[POOL_BUF + t] for t in range(DEC_SEQ)], axis=0)
        _pool_project(pooled, g, w_ref, sc_ref, o_ref)


def _pool_sample(proj, hist_tm, w_pool_bf, pool_scale, y_pool):
    blk = T_PROMPT // T_SAMPLE
    return pl.pallas_call(
        _pool_sample_body,
        grid=(1,),
        in_specs=[
            pl.BlockSpec((T_SAMPLE, POOL_WIDTH), lambda i: (blk, 0)),
            pl.BlockSpec((POOL_BUF, DEC_BATCH, POOL_WIDTH), lambda i: (0, 0, 0)),
            pl.BlockSpec((POOL_GROUPS, POOL_GROUP_CH, POOL_OUT_CH), lambda i: (0, 0, 0)),
            pl.BlockSpec((1, D_MODEL), lambda i: (0, 0)),
            pl.BlockSpec(memory_space=pl.ANY),
        ],
        out_specs=pl.BlockSpec((T_SAMPLE, D_MODEL), lambda i: (blk, 0)),
        out_shape=jax.ShapeDtypeStruct((T_ALL, D_MODEL), F32),
        input_output_aliases={4: 0},
        compiler_params=_cparams(1),
        name="pool_sample",
    )(proj, hist_tm, w_pool_bf, pool_scale, y_pool)


def _ssm_tables(a_re, a_im, log_dt, b_re, b_im, c_re, c_im, d_skip):
    dt = jnp.exp(log_dt)[:, None]
    lr, li = a_re, a_im
    ab_re = jnp.exp(lr * dt) * jnp.cos(li * dt)
    ab_im = jnp.exp(lr * dt) * jnp.sin(li * dt)
    den = lr * lr + li * li
    nr, ni = ab_re - 1.0, ab_im
    q_re = (nr * lr + ni * li) / den
    q_im = (ni * lr - nr * li) / den
    bb_re = q_re[..., None] * b_re - q_im[..., None] * b_im
    bb_im = q_re[..., None] * b_im + q_im[..., None] * b_re

    def lam_rows(ks):
        k = jnp.asarray(ks, F32)[:, None, None]
        m = jnp.exp(k * lr * dt)
        re = (m * jnp.cos(k * li * dt)).reshape(len(ks), N_OCT, OCT_STATES)
        im = (m * jnp.sin(k * li * dt)).reshape(len(ks), N_OCT, OCT_STATES)
        return jnp.concatenate([re, im], axis=-1).transpose(1, 0, 2)

    def compact(re, im):
        v = jnp.concatenate([re, im], axis=-1)
        return v.reshape(N_OCT, OCT, 2 * SSM_STATE)

    bbc = compact(jnp.swapaxes(bb_re, 1, 2), jnp.swapaxes(bb_im, 1, 2))
    ccc = compact(c_re, c_im)
    pw = lam_rows(list(range(2 * SUBLANES)))
    r = jnp.arange(SUBLANES)[None, :, None]
    parts = [jnp.where(r >= dd, lam_rows([CHUNK * dd]), 0.0) for dd in (1, 2, 4)]
    parts.append(lam_rows([CHUNK * kk for kk in range(1, SUBLANES + 1)]))
    tab = jnp.concatenate(parts, axis=1)
    dsk = d_skip.reshape(N_OCT, 1, OCT)
    return bbc, ccc, pw, tab, dsk


def _split_bf16(x):
    hi = x.astype(BF16)
    return hi, (x - hi.astype(F32)).astype(BF16)


def _dot_nt(a, b):
    return lax.dot_general(a, b, (((1,), (1,)), ((), ())), preferred_element_type=F32)


def _build_weights(bbc_ref, ccc_ref, pw_ref, f_ref, gt_ref, m_ref):
    row_gi = lax.broadcasted_iota(I32, (OCT, 1), 0) >> 4
    col = lax.broadcasted_iota(I32, (1, SW), 1)
    col_gi = (col >> 6) & 7
    src = ((col >> 9) << 6) | (col & 63)
    k128 = lax.broadcasted_iota(I32, (2 * SSM_STATE, 1), 0)
    spread = jnp.where(k128 == src, 1.0, 0.0).astype(BF16)
    diag = row_gi == col_gi

    def expand(c_ref):
        hi, lo = _split_bf16(c_ref[...])
        d = (jnp.dot(hi, spread, preferred_element_type=F32)
             + jnp.dot(lo, spread, preferred_element_type=F32))
        d = jnp.where(diag, d, 0.0)
        return d[:, :OCT_STATES], d[:, OCT_STATES:]

    br, bi = expand(bbc_ref)
    cr, ci = expand(ccc_ref)
    chi_r, clo_r = _split_bf16(cr)
    chi_i, clo_i = _split_bf16(ci)

    def lam(k):
        return pw_ref[k:k + 1, :OCT_STATES], pw_ref[k:k + 1, OCT_STATES:]

    def dot3(a, bhi, blo):
        ahi, alo = _split_bf16(a)
        return _dot_nt(ahi, bhi) + _dot_nt(alo, bhi) + _dot_nt(ahi, blo)

    lags = []
    for k in range(CHUNK):
        pr, pi_ = lam(k)
        fr, fi = _cmul(br, bi, pr, pi_)
        s = CHUNK - 1 - k
        f_ref[s * OCT:(s + 1) * OCT, :] = jnp.concatenate([fr, fi], axis=1).astype(BF16)
        lags.append((dot3(fr, chi_r, clo_r) - dot3(fi, chi_i, clo_i)).astype(BF16))
        pr, pi_ = lam(k + 1)
        gr, gi = _cmul(cr, ci, pr, pi_)
        gt_ref[k * OCT:(k + 1) * OCT, :] = jnp.concatenate([gr, -gi], axis=1).astype(BF16)
    zero = jnp.zeros((OCT, OCT), BF16)
    for s in range(CHUNK):
        for t in range(CHUNK):
            m_ref[s * OCT:(s + 1) * OCT, t * OCT:(t + 1) * OCT] = lags[t - s] if t >= s else zero


def _cmul(ar, ai, br, bi):
    return ar * br - ai * bi, ar * bi + ai * br


def _chunk_scan(sloc, tab_ref):
    R = sloc.shape[0]
    nb = R // SUBLANES
    sr, si = sloc[:, :OCT_STATES], sloc[:, OCT_STATES:]
    rowi = lax.broadcasted_iota(I32, (R, 1), 0)
    tr = jnp.where(rowi == 0, 0.0, pltpu.roll(sr, 1, axis=0))
    ti = jnp.where(rowi == 0, 0.0, pltpu.roll(si, 1, axis=0))
    for lvl, d in enumerate((1, 2, 4)):
        mr = tab_ref[lvl * SUBLANES:(lvl + 1) * SUBLANES, :OCT_STATES]
        mi = tab_ref[lvl * SUBLANES:(lvl + 1) * SUBLANES, OCT_STATES:]
        mr = jnp.concatenate([mr] * nb, axis=0)
        mi = jnp.concatenate([mi] * nb, axis=0)
        pr, pi_ = _cmul(mr, mi, pltpu.roll(tr, d, axis=0), pltpu.roll(ti, d, axis=0))
        tr, ti = tr + pr, ti + pi_
    pwr = tab_ref[3 * SUBLANES:4 * SUBLANES, :OCT_STATES]
    pwi = tab_ref[3 * SUBLANES:4 * SUBLANES, OCT_STATES:]
    cr = jnp.zeros((1, OCT_STATES), F32)
    ci = jnp.zeros((1, OCT_STATES), F32)
    out_r, out_i = [], []
    for k in range(nb):
        ar = tr[k * SUBLANES:(k + 1) * SUBLANES, :]
        ai = ti[k * SUBLANES:(k + 1) * SUBLANES, :]
        pr, pi_ = _cmul(pwr, pwi, jnp.broadcast_to(cr, ar.shape), jnp.broadcast_to(ci, ai.shape))
        hr, hi = ar + pr, ai + pi_
        out_r.append(hr)
        out_i.append(hi)
        cr, ci = hr[SUBLANES - 1:, :], hi[SUBLANES - 1:, :]
    hin = jnp.concatenate([jnp.concatenate(out_r, axis=0), jnp.concatenate(out_i, axis=0)], axis=1)
    lr, li = pwr[0:1, :], pwi[0:1, :]
    fr, fi = _cmul(lr, li, cr, ci)
    fin = jnp.concatenate([fr + sr[R - 1:, :], fi + si[R - 1:, :]], axis=1)
    return hin, fin


def _ssm_prompt_body(u_ref, bbc_ref, ccc_ref, pw_ref, tab_ref, d_ref, y_ref, hout_ref,
                     f_ref, gt_ref, m_ref):
    @pl.when(pl.program_id(1) == 0)
    def _():
        _build_weights(bbc_ref, ccc_ref, pw_ref, f_ref, gt_ref, m_ref)
    R = SEQ // CHUNK
    xs = [u_ref[pl.ds(s, R, stride=CHUNK), :] for s in range(CHUNK)]
    xb = jnp.concatenate(xs, axis=1).astype(BF16)
    sloc = jnp.dot(xb, f_ref[...], preferred_element_type=F32)
    hin, fin = _chunk_scan(sloc, tab_ref)
    y = (jnp.dot(xb, m_ref[...], preferred_element_type=F32)
         + _dot_nt(hin.astype(BF16), gt_ref[...]))
    for t in range(CHUNK):
        yt = y[:, t * OCT:(t + 1) * OCT] + d_ref[...] * xs[t]
        y_ref[pl.ds(t, R, stride=CHUNK), :] = _gelu_tanh(yt)
    hout_ref[...] = fin


def _ssm_weight_specs(n_axes):
    if n_axes == 2:
        im3 = lambda o, b: (o, 0, 0)
    else:
        im3 = lambda o: (o, 0, 0)
    return [
        pl.BlockSpec((None, OCT, 2 * SSM_STATE), im3),
        pl.BlockSpec((None, OCT, 2 * SSM_STATE), im3),
        pl.BlockSpec((None, 2 * SUBLANES, SW), im3),
        pl.BlockSpec((None, 4 * SUBLANES, SW), im3),
        pl.BlockSpec((None, 1, OCT), im3),
    ]


_SSM_SCRATCH = [pltpu.VMEM((CW, SW), BF16), pltpu.VMEM((CW, SW), BF16), pltpu.VMEM((CW, CW), BF16)]


def _ssm_prompt(proj, tables):
    col0 = POOL_WIDTH // OCT
    return pl.pallas_call(
        _ssm_prompt_body,
        grid=(N_OCT, BATCH),
        in_specs=[pl.BlockSpec((SEQ, OCT), lambda o, b: (b, col0 + o))] + _ssm_weight_specs(2),
        out_specs=(
            pl.BlockSpec((SEQ, OCT), lambda o, b: (b, o)),
            pl.BlockSpec((None, 1, SW), lambda o, b: (b * N_OCT + o, 0, 0)),
        ),
        out_shape=(
            jax.ShapeDtypeStruct((T_ALL, SSM_WIDTH), F32),
            jax.ShapeDtypeStruct((BATCH * N_OCT, 1, SW), F32),
        ),
        scratch_shapes=_SSM_SCRATCH,
        compiler_params=_cparams(2),
        name="ssm_prompt",
    )(proj, *tables)


def _ssm_sample_body(u_ref, h0r_ref, h0i_ref, bbc_ref, ccc_ref, pw_ref, tab_ref, d_ref, _dst,
                     y_ref, hr_ref, hi_ref, f_ref, gt_ref, m_ref):
    _build_weights(bbc_ref, ccc_ref, pw_ref, f_ref, gt_ref, m_ref)
    B = DEC_BATCH
    xs = [u_ref[B * s:B * (s + 1), :] for s in range(CHUNK)]
    xb = jnp.concatenate(xs, axis=1).astype(BF16)
    sloc = jnp.dot(xb, f_ref[...], preferred_element_type=F32)
    h0r, h0i = h0r_ref[...], h0i_ref[...]
    hin = jnp.concatenate([h0r, h0i], axis=1).astype(BF16)
    y = (jnp.dot(xb, m_ref[...], preferred_element_type=F32)
         + _dot_nt(hin, gt_ref[...]))
    for t in range(CHUNK):
        yt = y[:, t * OCT:(t + 1) * OCT] + d_ref[...] * xs[t]
        y_ref[B * t:B * (t + 1), :] = _gelu_tanh(yt)
    lr = tab_ref[3 * SUBLANES:3 * SUBLANES + 1, :OCT_STATES]
    li = tab_ref[3 * SUBLANES:3 * SUBLANES + 1, OCT_STATES:]
    nr, ni = _cmul(lr, li, h0r, h0i)
    hr_ref[...] = nr + sloc[:, :OCT_STATES]
    hi_ref[...] = ni + sloc[:, OCT_STATES:]


def _ssm_sample(proj, h0r, h0i, tables, y_act):
    col0 = POOL_WIDTH // OCT
    blk = T_PROMPT // T_SAMPLE
    st_spec = pl.BlockSpec((DEC_BATCH, OCT_STATES), lambda o: (0, o))
    return pl.pallas_call(
        _ssm_sample_body,
        grid=(N_OCT,),
        in_specs=[pl.BlockSpec((T_SAMPLE, OCT), lambda o: (blk, col0 + o)), st_spec, st_spec]
        + _ssm_weight_specs(1) + [pl.BlockSpec(memory_space=pl.ANY)],
        out_specs=(pl.BlockSpec((T_SAMPLE, OCT), lambda o: (blk, o)), st_spec, st_spec),
        out_shape=(
            jax.ShapeDtypeStruct((T_ALL, SSM_WIDTH), F32),
            jax.ShapeDtypeStruct((DEC_BATCH, SSM_GROUPS * SSM_STATE), F32),
            jax.ShapeDtypeStruct((DEC_BATCH, SSM_GROUPS * SSM_STATE), F32),
        ),
        scratch_shapes=_SSM_SCRATCH,
        input_output_aliases={8: 0},
        compiler_params=_cparams(1),
        name="ssm_sample",
    )(proj, h0r, h0i, *tables, y_act)


PM_TM = 256
PM_PROMPT_BLOCKS = T_PROMPT // PM_TM


def _postmix_body(ya_ref, gp_ref, gs_ref, yp_ref, xp_ref, xs_ref, wa_ref, wb_ref, wo_ref,
                  gf_ref, wrh_ref, wrl_ref, br_ref, h_ref, tn_ref, rt_ref, cnt_out_ref, cnt_ref):
    i = pl.program_id(0)
    @pl.when(i == 0)
    def _():
        cnt_ref[...] = jnp.zeros_like(cnt_ref)
    ya = ya_ref[...].astype(BF16)
    a = jnp.dot(ya, wa_ref[...], preferred_element_type=F32)
    bg = jnp.dot(ya, wb_ref[...], preferred_element_type=F32)
    y_ssm = a * _sigmoid(bg)
    merged = _sigmoid(gp_ref[...]) * yp_ref[...] + _sigmoid(gs_ref[...]) * y_ssm
    x = jnp.where(i < PM_PROMPT_BLOCKS, xp_ref[...], xs_ref[...])
    h = x + jnp.dot(merged.astype(BF16), wo_ref[...], preferred_element_type=F32)
    h_ref[...] = h
    inv = lax.rsqrt(jnp.mean(h * h, axis=-1, keepdims=True) + EPS)
    tn = (h * inv) * gf_ref[...]
    tn_ref[...] = tn
    t_hi = tn.astype(BF16)
    t_lo = (tn - t_hi.astype(F32)).astype(BF16)
    wrh = wrh_ref[...]
    logits = (jnp.dot(t_hi, wrh, preferred_element_type=F32)
              + jnp.dot(t_lo, wrh, preferred_element_type=F32)
              + jnp.dot(t_hi, wrl_ref[...], preferred_element_type=F32)) + br_ref[...]
    lane = lax.broadcasted_iota(I32, (PM_TM, ROUTE_LANES), 1)
    neg = jnp.float32(-jnp.inf)
    big = jnp.int32(1 << 20)
    is_g = lane < MOE_GROUPS
    gmax = jnp.max(jnp.where(is_g, logits, neg), axis=1, keepdims=True)
    g_idx = jnp.min(jnp.where(is_g & (logits == gmax), lane, big), axis=1, keepdims=True)
    g_den = jnp.sum(jnp.where(is_g, jnp.exp(logits - gmax), 0.0), axis=1, keepdims=True)
    g_val = 1.0 / g_den
    e_lane = lane - EXP_LANE0
    sel = (e_lane >= 0) & (e_lane < MOE_EXPERTS) & ((e_lane >> 3) == g_idx)
    m1 = jnp.max(jnp.where(sel, logits, neg), axis=1, keepdims=True)
    i1 = jnp.min(jnp.where(sel & (logits == m1), lane, big), axis=1, keepdims=True)
    sel2 = sel & (lane != i1)
    m2 = jnp.max(jnp.where(sel2, logits, neg), axis=1, keepdims=True)
    i2 = jnp.min(jnp.where(sel2 & (logits == m2), lane, big), axis=1, keepdims=True)
    e2 = jnp.exp(m2 - m1)
    w1 = g_val / (1.0 + e2)
    w2 = g_val * e2 / (1.0 + e2)
    oh1 = lane == i1
    oh2 = lane == i2
    oh = jnp.where(oh1 | oh2, 1.0, 0.0)
    rr = lax.broadcasted_iota(I32, (PM_TM, PM_TM), 0)
    cc = lax.broadcasted_iota(I32, (PM_TM, PM_TM), 1)
    tri = jnp.where(cc < rr, 1.0, 0.0).astype(BF16)
    base = cnt_ref[...] + jnp.dot(tri, oh.astype(BF16), preferred_element_type=F32)
    rank1 = jnp.sum(jnp.where(oh1, base, 0.0), axis=1, keepdims=True)
    rank2 = jnp.sum(jnp.where(oh2, base, 0.0), axis=1, keepdims=True)
    cnt_ref[...] = cnt_ref[...] + jnp.sum(oh, axis=0, keepdims=True)
    cnt_out_ref[...] = cnt_ref[...]
    rt = jnp.where(lane == 0, w1, 0.0)
    rt = jnp.where(lane == 1, w2, rt)
    rt = jnp.where(lane == 2, rank1, rt)
    rt = jnp.where(lane == 3, rank2, rt)
    rt = jnp.where(lane == 4, (i1 - EXP_LANE0).astype(F32), rt)
    rt = jnp.where(lane == 5, (i2 - EXP_LANE0).astype(F32), rt)
    rt_ref[...] = rt


def _postmix(y_act, proj, y_pool, xp, xs, wa, wb, wo, g_ffn, wr_hi, wr_lo, b_r):
    n = T_ALL // PM_TM
    npb = PM_PROMPT_BLOCKS
    const2 = lambda i: (0, 0)
    return pl.pallas_call(
        _postmix_body,
        grid=(n,),
        in_specs=[
            pl.BlockSpec((PM_TM, SSM_WIDTH), lambda i: (i, 0)),
            pl.BlockSpec((PM_TM, D_MODEL), lambda i: (i, 1)),
            pl.BlockSpec((PM_TM, D_MODEL), lambda i: (i, 2)),
            pl.BlockSpec((PM_TM, D_MODEL), lambda i: (i, 0)),
            pl.BlockSpec((PM_TM, D_MODEL), lambda i: (jnp.minimum(i, npb - 1), 0)),
            pl.BlockSpec((PM_TM, D_MODEL), lambda i: (jnp.maximum(i - npb, 0), 0)),
            pl.BlockSpec((SSM_WIDTH, D_MODEL), const2, pipeline_mode=pl.Buffered(1)),
            pl.BlockSpec((SSM_WIDTH, D_MODEL), const2, pipeline_mode=pl.Buffered(1)),
            pl.BlockSpec((D_MODEL, D_MODEL), const2, pipeline_mode=pl.Buffered(1)),
            pl.BlockSpec((1, D_MODEL), const2),
            pl.BlockSpec((D_MODEL, ROUTE_LANES), const2),
            pl.BlockSpec((D_MODEL, ROUTE_LANES), const2),
            pl.BlockSpec((1, ROUTE_LANES), const2),
        ],
        out_specs=(
            pl.BlockSpec((PM_TM, D_MODEL), lambda i: (i, 0)),
            pl.BlockSpec((PM_TM, D_MODEL), lambda i: (i, 0)),
            pl.BlockSpec((PM_TM, ROUTE_LANES), lambda i: (i, 0)),
            pl.BlockSpec((1, ROUTE_LANES), const2),
        ),
        out_shape=(
            jax.ShapeDtypeStruct((T_ALL, D_MODEL), F32),
            jax.ShapeDtypeStruct((T_ALL, D_MODEL), F32),
            jax.ShapeDtypeStruct((T_ALL, ROUTE_LANES), F32),
            jax.ShapeDtypeStruct((1, ROUTE_LANES), F32),
        ),
        scratch_shapes=[pltpu.VMEM((1, ROUTE_LANES), F32)],
        compiler_params=_cparams(1),
        name="postmix",
    )(y_act, proj, proj, y_pool, xp, xs, wa, wb, wo, g_ffn, wr_hi, wr_lo, b_r)


def _expert_body(q_ref, ie_ref, q0_ref, ni_ref, t_hbm, wg_ref, wu_ref, wd_ref, y_hbm,
                 plan_ref, xs_ref, yb_ref, wgb_ref, wub_ref, wdb_ref, gsem, ssem):
    w = pl.program_id(0)
    n_items = ni_ref[0]

    def gather_copy(tok, s, r):
        return pltpu.make_async_copy(t_hbm.at[pl.ds(tok, 1)], xs_ref.at[s, pl.ds(r, 1)], gsem.at[s])

    def scatter_copy(s, r, row):
        return pltpu.make_async_copy(yb_ref.at[s, pl.ds(r, 1)], y_hbm.at[pl.ds(row, 1)], ssem.at[s])

    def gather_starts(q0, s):
        return [functools.partial(lambda r: gather_copy(plan_ref[q0 + r] & SRC_MASK, s, r).start(), r)
                for r in range(TME)]

    def scatter_starts(q0, s):
        return [functools.partial(
            lambda r: scatter_copy(s, r, plan_ref[q0 + r] >> SRC_BITS).start(priority=r % 2), r)
            for r in range(TME)]

    def wait_gather(s):
        for r in range(TME):
            gather_copy(0, s, r).wait()

    def wait_scatter(s):
        for r in range(TME):
            scatter_copy(s, r, 0).wait()

    @pl.when(w == 0)
    def _():
        def tail(i, c):
            plan_ref[N_ASSIGN + i] = (T_ALL + i) << SRC_BITS
            return c
        lax.fori_loop(0, TME, tail, 0)
        def invert(a, c):
            tok = a >> 1
            plan_ref[q_ref[a]] = (((a & 1) * Y_ROWS + tok) << SRC_BITS) | tok
            return c
        lax.fori_loop(0, N_ASSIGN, invert, 0, unroll=8)
        yb_ref[...] = jnp.zeros_like(yb_ref)
        for start in gather_starts(q0_ref[0], 0) + scatter_starts(N_ASSIGN, 0):
            start()

    def item(s):
        wait_gather(s)
        wait_scatter(s)
        nxt = jnp.minimum(w + 1, n_items - 1)
        prev_q0 = jnp.where(w > 0, q0_ref[jnp.maximum(w - 1, 0)], N_ASSIGN)
        starts = gather_starts(q0_ref[nxt], 1 - s) + scatter_starts(prev_q0, 1 - s)
        per_phase = len(starts) // EXP_PHASES

        def copies(phase):
            for start in starts[phase * per_phase:(phase + 1) * per_phase]:
                start()

        x = xs_ref[s].astype(BF16)
        acts = []
        nc = MOE_FF // EXP_UP_CHUNKS
        for c in range(EXP_UP_CHUNKS):
            hg = jnp.dot(x, wgb_ref[:, c * nc:(c + 1) * nc], preferred_element_type=F32)
            copies(2 * c)
            hu = jnp.dot(x, wub_ref[:, c * nc:(c + 1) * nc], preferred_element_type=F32)
            copies(2 * c + 1)
            acts.append(((hg * _sigmoid(hg)) * hu).astype(BF16))
        act = jnp.concatenate(acts, axis=1)
        nd = D_MODEL // EXP_DOWN_CHUNKS
        for c in range(EXP_DOWN_CHUNKS):
            yb_ref[s, :, c * nd:(c + 1) * nd] = jnp.dot(
                act, wdb_ref[:, c * nd:(c + 1) * nd], preferred_element_type=F32)
            copies(2 * EXP_UP_CHUNKS + c)

        @pl.when(w == n_items - 1)
        def _():
            for start in scatter_starts(q0_ref[w], s):
                start()
            wait_scatter(1 - s)
            wait_scatter(s)
            wait_gather(1 - s)

    @pl.when(w < n_items)
    def _():
        e = ie_ref[w]
        prev = ie_ref[jnp.maximum(w - 1, 0)]
        @pl.when((w == 0) | (prev != e))
        def _():
            wgb_ref[...] = wg_ref[...].astype(BF16)
            wub_ref[...] = wu_ref[...].astype(BF16)
            wdb_ref[...] = wd_ref[...].astype(BF16)
        for s in range(2):
            pl.when((w & 1) == s)(functools.partial(item, s))


def _experts(q_flat, item_e, item_q0, n_items, tn, w_eg, w_eu, w_ed):
    wmap = lambda w, q, ie, q0, ni: (ie[w], 0, 0)
    grid_spec = pltpu.PrefetchScalarGridSpec(
        num_scalar_prefetch=4,
        grid=(N_ITEMS_MAX,),
        in_specs=[
            pl.BlockSpec(memory_space=pl.ANY),
            pl.BlockSpec((None, D_MODEL, MOE_FF), wmap),
            pl.BlockSpec((None, D_MODEL, MOE_FF), wmap),
            pl.BlockSpec((None, MOE_FF, D_MODEL), wmap),
        ],
        out_specs=pl.BlockSpec(memory_space=pl.ANY),
        scratch_shapes=[
            pltpu.SMEM((N_ASSIGN + TME,), I32),
            pltpu.VMEM((2, TME, D_MODEL), F32),
            pltpu.VMEM((2, TME, D_MODEL), F32),
            pltpu.VMEM((D_MODEL, MOE_FF), BF16),
            pltpu.VMEM((D_MODEL, MOE_FF), BF16),
            pltpu.VMEM((MOE_FF, D_MODEL), BF16),
            pltpu.SemaphoreType.DMA((2,)),
            pltpu.SemaphoreType.DMA((2,)),
        ],
    )
    return pl.pallas_call(
        _expert_body,
        grid_spec=grid_spec,
        out_shape=jax.ShapeDtypeStruct((2 * Y_ROWS, D_MODEL), F32),
        compiler_params=_cparams(1),
        name="experts",
    )(q_flat, item_e, item_q0, n_items, tn, w_eg, w_eu, w_ed)


FN_TM = 256
FN_PROMPT_BLOCKS = T_PROMPT // FN_TM


def _final_body(h_ref, y0_ref, y1_ref, rt_ref, g_ref, op_ref, os_ref):
    i = pl.program_id(0)
    rt = rt_ref[...]
    z = h_ref[...] + rt[:, 0:1] * y0_ref[...] + rt[:, 1:2] * y1_ref[...]
    inv = lax.rsqrt(jnp.mean(z * z, axis=-1, keepdims=True) + EPS)
    out = (z * inv) * g_ref[...]
    @pl.when(i < FN_PROMPT_BLOCKS)
    def _():
        op_ref[...] = out
    @pl.when(i >= FN_PROMPT_BLOCKS)
    def _():
        os_ref[...] = out


def _final(h, y, route, g_final):
    n = T_ALL // FN_TM
    npb = FN_PROMPT_BLOCKS
    yoff = Y_ROWS // FN_TM
    return pl.pallas_call(
        _final_body,
        grid=(n,),
        in_specs=[
            pl.BlockSpec((FN_TM, D_MODEL), lambda i: (i, 0)),
            pl.BlockSpec((FN_TM, D_MODEL), lambda i: (i, 0)),
            pl.BlockSpec((FN_TM, D_MODEL), lambda i: (yoff + i, 0)),
            pl.BlockSpec((FN_TM, ROUTE_LANES), lambda i: (i, 0)),
            pl.BlockSpec((1, D_MODEL), lambda i: (0, 0)),
        ],
        out_specs=(
            pl.BlockSpec((FN_TM, D_MODEL), lambda i: (jnp.minimum(i, npb - 1), 0)),
            pl.BlockSpec((FN_TM, D_MODEL), lambda i: (jnp.maximum(i - npb, 0), 0)),
        ),
        out_shape=(
            jax.ShapeDtypeStruct((T_PROMPT, D_MODEL), F32),
            jax.ShapeDtypeStruct((T_SAMPLE, D_MODEL), F32),
        ),
        compiler_params=_cparams(1),
        name="final",
    )(h, y, y, route, g_final)


def _dispatch_plan(route, cnt):
    counts = cnt[0, EXP_LANE0:EXP_LANE0 + MOE_EXPERTS].astype(I32)
    cum = jnp.cumsum(counts)
    cumex = cum - counts
    rank = route[:, 2:4].astype(I32)
    eid = route[:, 4:6].astype(I32)
    onehot = eid[..., None] == jnp.arange(MOE_EXPERTS, dtype=I32)
    q = rank + jnp.sum(jnp.where(onehot, cumex, 0), axis=-1)
    q_flat = q.reshape(N_ASSIGN)
    tiles = (counts + (TME - 1)) // TME
    cumt = jnp.cumsum(tiles)
    n_items = cumt[-1]
    w = jnp.arange(N_ITEMS_MAX, dtype=I32)
    w_eff = jnp.minimum(w, n_items - 1)
    item_e = jnp.sum((w_eff[:, None] >= cumt[None, :]).astype(I32), axis=1)
    j = w_eff - (cumt - tiles)[item_e]
    item_q0 = cumex[item_e] + TME * j
    return q_flat, item_e, item_q0, n_items.reshape(1)


def kernel(x_prompt, x_sample, state_pool, state_ssm_re, state_ssm_im, g_mix, w_in, w_pool,
           pool_scale, ssm_a_re, ssm_a_im, ssm_log_dt, ssm_b_re, ssm_b_im, ssm_c_re, ssm_c_im,
           ssm_d, w_glu_a, w_glu_b, w_out, g_ffn, w_router_group, b_router_group,
           w_router_expert, b_router_expert, w_exp_gate, w_exp_up, w_exp_down, g_final):
    l = 0
    xp = x_prompt.reshape(T_PROMPT, D_MODEL)
    xs = x_sample.transpose(1, 0, 2).reshape(T_SAMPLE, D_MODEL)
    w_in_bf = w_in[l].astype(BF16)
    w_pool_bf = w_pool[l].astype(BF16)
    wa_bf = w_glu_a[l].astype(BF16)
    wb_bf = w_glu_b[l].astype(BF16)
    wo_bf = w_out[l].astype(BF16)
    g_mix2 = g_mix[l].reshape(1, D_MODEL)
    scale2 = pool_scale[l].reshape(1, D_MODEL)

    proj = _inproj(xp, g_mix2, w_in_bf, 0)
    proj = _inproj(xs, g_mix2, w_in_bf, T_PROMPT // IN_TM, dst=proj)

    y_pool, pool_tail = _pool_prompt(proj, w_pool_bf, scale2)
    hist_tm = state_pool[l].transpose(1, 0, 2)
    y_pool = _pool_sample(proj, hist_tm, w_pool_bf, scale2, y_pool)

    tables = _ssm_tables(ssm_a_re[l], ssm_a_im[l], ssm_log_dt[l], ssm_b_re[l], ssm_b_im[l],
                         ssm_c_re[l], ssm_c_im[l], ssm_d[l])
    y_act, h_prompt = _ssm_prompt(proj, tables)
    h0r = state_ssm_re[l].reshape(DEC_BATCH, SSM_GROUPS * SSM_STATE)
    h0i = state_ssm_im[l].reshape(DEC_BATCH, SSM_GROUPS * SSM_STATE)
    y_act, hs_re, hs_im = _ssm_sample(proj, h0r, h0i, tables, y_act)

    w_r = jnp.zeros((D_MODEL, ROUTE_LANES), F32)
    w_r = w_r.at[:, :MOE_GROUPS].set(w_router_group[l])
    w_r = w_r.at[:, EXP_LANE0:EXP_LANE0 + MOE_EXPERTS].set(w_router_expert[l])
    wr_hi = w_r.astype(BF16)
    wr_lo = (w_r - wr_hi.astype(F32)).astype(BF16)
    b_r = jnp.zeros((1, ROUTE_LANES), F32)
    b_r = b_r.at[0, :MOE_GROUPS].set(b_router_group[l])
    b_r = b_r.at[0, EXP_LANE0:EXP_LANE0 + MOE_EXPERTS].set(b_router_expert[l])

    h, tn, route, cnt = _postmix(y_act, proj, y_pool, xp, xs, wa_bf, wb_bf, wo_bf,
                                 g_ffn[l].reshape(1, D_MODEL), wr_hi, wr_lo, b_r)
    plan = _dispatch_plan(route, cnt)
    y = _experts(*plan, tn, w_exp_gate[l], w_exp_up[l], w_exp_down[l])
    yp, ys = _final(h, y, route, g_final.reshape(1, D_MODEL))

    y_prompt = yp.reshape(BATCH, SEQ, D_MODEL)
    y_sample = ys.reshape(DEC_SEQ, DEC_BATCH, D_MODEL).transpose(1, 0, 2)
    new_pool_prompt = pool_tail[:, HIST - POOL_BUF:, :][None]
    us = proj[T_PROMPT:, :POOL_WIDTH].reshape(DEC_SEQ, DEC_BATCH, POOL_WIDTH).transpose(1, 0, 2)
    new_pool_sample = jnp.concatenate([state_pool[l][:, DEC_SEQ:, :], us], axis=1)[None]
    hp = h_prompt.reshape(BATCH, N_OCT, 2, OCT_GROUPS, SSM_STATE).transpose(2, 0, 1, 3, 4)
    hp = hp.reshape(2, BATCH, SSM_GROUPS, SSM_STATE)
    shp = (1, DEC_BATCH, SSM_GROUPS, SSM_STATE)
    return (y_prompt, y_sample, new_pool_prompt, hp[0][None], hp[1][None], new_pool_sample,
            hs_re.reshape(shp), hs_im.reshape(shp))
```

```python
import functools
import math

import jax
import jax.numpy as jnp
from jax import lax
from jax.experimental import pallas as pl
from jax.experimental.pallas import tpu as pltpu

F32 = jnp.float32
BF16 = jnp.bfloat16
I32 = jnp.int32

D_MODEL = 2048
BATCH = 4
SEQ = 2048
DEC_BATCH = 128
DEC_SEQ = 8
PAST_LEN = 16384
POOL_WIDTH = D_MODEL // 2
POOL_WINDOWS = (2, 4, 8, 16)
POOL_GROUPS = len(POOL_WINDOWS)
POOL_GROUP_CH = POOL_WIDTH // POOL_GROUPS
POOL_OUT_CH = D_MODEL // POOL_GROUPS
POOL_BUF = max(POOL_WINDOWS) - 1
SSM_WIDTH = D_MODEL // 2
SSM_GROUP_CH = 16
SSM_GROUPS = SSM_WIDTH // SSM_GROUP_CH
SSM_STATE = 64
IN_WIDTH = POOL_WIDTH + SSM_WIDTH + 2 * D_MODEL
MOE_GROUPS = 4
MOE_EPG = 8
MOE_EXPERTS = MOE_GROUPS * MOE_EPG
MOE_FF = D_MODEL // 4
EPS = 1e-6

T_PROMPT = BATCH * SEQ
T_SAMPLE = DEC_BATCH * DEC_SEQ
T_ALL = T_PROMPT + T_SAMPLE

LANES = 128
SUBLANES = 8
VMEM_LIMIT = 56 * 1024 * 1024

CHUNK = 8
OCT = LANES
N_OCT = SSM_WIDTH // OCT
OCT_GROUPS = OCT // SSM_GROUP_CH
OCT_STATES = OCT_GROUPS * SSM_STATE
CW = CHUNK * OCT
SW = 2 * OCT_STATES

ROUTE_LANES = LANES
EXP_LANE0 = MOE_GROUPS
N_ASSIGN = 2 * T_ALL
TME = 256
N_ITEMS_MAX = N_ASSIGN // TME + MOE_EXPERTS
Y_PAD = 1024
Y_ROWS = T_ALL + Y_PAD
SRC_BITS = 14
SRC_MASK = (1 << SRC_BITS) - 1
assert T_ALL <= 1 << SRC_BITS and 2 * Y_ROWS < 1 << (31 - SRC_BITS)


def _cparams(n_axes):
    return pltpu.CompilerParams(dimension_semantics=("arbitrary",) * n_axes,
                                vmem_limit_bytes=VMEM_LIMIT)


def _sigmoid(x):
    return 1.0 / (1.0 + jnp.exp(-x))


def _gelu_tanh(x):
    c = math.sqrt(2.0 / math.pi)
    return 0.5 * x * (1.0 + jnp.tanh(c * (x + 0.044715 * (x * x * x))))


IN_TM = 1024
IN_TN = 512


def _inproj_body(x_ref, g_ref, w_ref, *rest):
    o_ref, xn_ref = rest[-2], rest[-1]
    @pl.when(pl.program_id(1) == 0)
    def _():
        x = x_ref[...]
        inv = lax.rsqrt(jnp.mean(x * x, axis=-1, keepdims=True) + EPS)
        xn_ref[...] = ((x * inv) * g_ref[...]).astype(BF16)
    o_ref[...] = jnp.dot(xn_ref[...], w_ref[...], preferred_element_type=F32)


def _inproj(x, g, w_bf, row_block0, dst=None):
    n_i = x.shape[0] // IN_TM
    in_specs = [
        pl.BlockSpec((IN_TM, D_MODEL), lambda i, j: (i, 0)),
        pl.BlockSpec((1, D_MODEL), lambda i, j: (0, 0)),
        pl.BlockSpec((D_MODEL, IN_TN), lambda i, j: (0, j)),
    ]
    args = [x, g, w_bf]
    aliases = {}
    if dst is not None:
        in_specs.append(pl.BlockSpec(memory_space=pl.ANY))
        args.append(dst)
        aliases = {3: 0}
    return pl.pallas_call(
        _inproj_body,
        grid=(n_i, IN_WIDTH // IN_TN),
        in_specs=in_specs,
        out_specs=pl.BlockSpec((IN_TM, IN_TN), lambda i, j: (i + row_block0, j)),
        out_shape=jax.ShapeDtypeStruct((T_ALL, IN_WIDTH), F32),
        scratch_shapes=[pltpu.VMEM((IN_TM, D_MODEL), BF16)],
        input_output_aliases=aliases,
        compiler_params=_cparams(2),
        name="inproj",
    )(*args)


PP_TM = 512
HIST = 16


def _pool_project(pooled_g, g, w_ref, sc_ref, o_ref):
    y = jnp.dot(pooled_g.astype(BF16), w_ref[g], preferred_element_type=F32)
    lo, hi = g * POOL_OUT_CH, (g + 1) * POOL_OUT_CH
    o_ref[:, lo:hi] = y * sc_ref[:, lo:hi]


def _pool_prompt_body(u_ref, w_ref, sc_ref, o_ref, tail_ref, hist_ref):
    i = pl.program_id(1)
    @pl.when(i == 0)
    def _():
        hist_ref[...] = jnp.zeros_like(hist_ref)
    u = u_ref[...]
    ext = jnp.concatenate([hist_ref[...], u], axis=0)
    hist_ref[...] = u[PP_TM - HIST:, :]
    tail_ref[...] = u[PP_TM - HIST:, :]
    pos = i * PP_TM + lax.broadcasted_iota(I32, (PP_TM, 1), 0)
    for g, w in enumerate(POOL_WINDOWS):
        lo, hi = g * POOL_GROUP_CH, (g + 1) * POOL_GROUP_CH
        s = ext[:, lo:hi]
        d = 1
        while d < w:
            s = s + pltpu.roll(s, d, axis=0)
            d *= 2
        cnt = jnp.minimum(w, pos + 1).astype(F32)
        pooled = s[HIST:, :] / cnt - u[:, lo:hi]
        _pool_project(pooled, g, w_ref, sc_ref, o_ref)


def _pool_prompt(proj, w_pool_bf, pool_scale):
    n_i = SEQ // PP_TM
    return pl.pallas_call(
        _pool_prompt_body,
        grid=(BATCH, n_i),
        in_specs=[
            pl.BlockSpec((PP_TM, POOL_WIDTH), lambda b, i: (b * n_i + i, 0)),
            pl.BlockSpec((POOL_GROUPS, POOL_GROUP_CH, POOL_OUT_CH), lambda b, i: (0, 0, 0)),
            pl.BlockSpec((1, D_MODEL), lambda b, i: (0, 0)),
        ],
        out_specs=(
            pl.BlockSpec((PP_TM, D_MODEL), lambda b, i: (b * n_i + i, 0)),
            pl.BlockSpec((None, HIST, POOL_WIDTH), lambda b, i: (b, 0, 0)),
        ),
        out_shape=(
            jax.ShapeDtypeStruct((T_ALL, D_MODEL), F32),
            jax.ShapeDtypeStruct((BATCH, HIST, POOL_WIDTH), F32),
        ),
        scratch_shapes=[pltpu.VMEM((HIST, POOL_WIDTH), F32)],
        compiler_params=_cparams(2),
        name="pool_prompt",
    )(proj, w_pool_bf, pool_scale)


def _pool_sample_body(u_ref, hist_ref, w_ref, sc_ref, _dst, o_ref):
    rows = [hist_ref[k] for k in range(POOL_BUF)]
    rows += [u_ref[DEC_BATCH * t:DEC_BATCH * (t + 1), :] for t in range(DEC_SEQ)]
    n = len(rows)
    for g, w in enumerate(POOL_WINDOWS):
        lo, hi = g * POOL_GROUP_CH, (g + 1) * POOL_GROUP_CH
        f = [r[:, lo:hi] for r in rows]
        cur = f
        d = 1
        while d < w:
            cur = [cur[k] + cur[k - d] if k - d >= 0 else cur[k] for k in range(n)]
            d *= 2
        pooled = jnp.concatenate(
            [cur[POOL_BUF + t] / float(w) - f[POOL_BUF + t] for t in range(DEC_SEQ)], axis=0)
        _pool_project(pooled, g, w_ref, sc_ref, o_ref)


def _pool_sample(proj, hist_tm, w_pool_bf, pool_scale, y_pool):
    blk = T_PROMPT // T_SAMPLE
    return pl.pallas_call(
        _pool_sample_body,
        grid=(1,),
        in_specs=[
            pl.BlockSpec((T_SAMPLE, POOL_WIDTH), lambda i: (blk, 0)),
            pl.BlockSpec((POOL_BUF, DEC_BATCH, POOL_WIDTH), lambda i: (0, 0, 0)),
            pl.BlockSpec((POOL_GROUPS, POOL_GROUP_CH, POOL_OUT_CH), lambda i: (0, 0, 0)),
            pl.BlockSpec((1, D_MODEL), lambda i: (0, 0)),
            pl.BlockSpec(memory_space=pl.ANY),
        ],
        out_specs=pl.BlockSpec((T_SAMPLE, D_MODEL), lambda i: (blk, 0)),
        out_shape=jax.ShapeDtypeStruct((T_ALL, D_MODEL), F32),
        input_output_aliases={4: 0},
        compiler_params=_cparams(1),
        name="pool_sample",
    )(proj, hist_tm, w_pool_bf, pool_scale, y_pool)


def _ssm_tables(a_re, a_im, log_dt, b_re, b_im, c_re, c_im, d_skip):
    dt = jnp.exp(log_dt)[:, None]
    lr, li = a_re, a_im
    ab_re = jnp.exp(lr * dt) * jnp.cos(li * dt)
    ab_im = jnp.exp(lr * dt) * jnp.sin(li * dt)
    den = lr * lr + li * li
    nr, ni = ab_re - 1.0, ab_im
    q_re = (nr * lr + ni * li) / den
    q_im = (ni * lr - nr * li) / den
    bb_re = q_re[..., None] * b_re - q_im[..., None] * b_im
    bb_im = q_re[..., None] * b_im + q_im[..., None] * b_re

    def lam_rows(ks):
        k = jnp.asarray(ks, F32)[:, None, None]
        m = jnp.exp(k * lr * dt)
        re = (m * jnp.cos(k * li * dt)).reshape(len(ks), N_OCT, OCT_STATES)
        im = (m * jnp.sin(k * li * dt)).reshape(len(ks), N_OCT, OCT_STATES)
        return jnp.concatenate([re, im], axis=-1).transpose(1, 0, 2)

    def compact(re, im):
        v = jnp.concatenate([re, im], axis=-1)
        return v.reshape(N_OCT, OCT, 2 * SSM_STATE)

    bbc = compact(jnp.swapaxes(bb_re, 1, 2), jnp.swapaxes(bb_im, 1, 2))
    ccc = compact(c_re, c_im)
    pw = lam_rows(list(range(2 * SUBLANES)))
    r = jnp.arange(SUBLANES)[None, :, None]
    parts = [jnp.where(r >= dd, lam_rows([CHUNK * dd]), 0.0) for dd in (1, 2, 4)]
    parts.append(lam_rows([CHUNK * kk for kk in range(1, SUBLANES + 1)]))
    tab = jnp.concatenate(parts, axis=1)
    dsk = d_skip.reshape(N_OCT, 1, OCT)
    return bbc, ccc, pw, tab, dsk


def _split_bf16(x):
    hi = x.astype(BF16)
    return hi, (x - hi.astype(F32)).astype(BF16)


def _dot_nt(a, b):
    return lax.dot_general(a, b, (((1,), (1,)), ((), ())), preferred_element_type=F32)


def _build_weights(bbc_ref, ccc_ref, pw_ref, f_ref, gt_ref, m_ref):
    row_gi = lax.broadcasted_iota(I32, (OCT, 1), 0) >> 4
    col = lax.broadcasted_iota(I32, (1, SW), 1)
    col_gi = (col >> 6) & 7
    src = ((col >> 9) << 6) | (col & 63)
    k128 = lax.broadcasted_iota(I32, (2 * SSM_STATE, 1), 0)
    spread = jnp.where(k128 == src, 1.0, 0.0).astype(BF16)
    diag = row_gi == col_gi

    def expand(c_ref):
        hi, lo = _split_bf16(c_ref[...])
        d = (jnp.dot(hi, spread, preferred_element_type=F32)
             + jnp.dot(lo, spread, preferred_element_type=F32))
        d = jnp.where(diag, d, 0.0)
        return d[:, :OCT_STATES], d[:, OCT_STATES:]

    br, bi = expand(bbc_ref)
    cr, ci = expand(ccc_ref)
    chi_r, clo_r = _split_bf16(cr)
    chi_i, clo_i = _split_bf16(ci)

    def lam(k):
        return pw_ref[k:k + 1, :OCT_STATES], pw_ref[k:k + 1, OCT_STATES:]

    def dot3(a, bhi, blo):
        ahi, alo = _split_bf16(a)
        return _dot_nt(ahi, bhi) + _dot_nt(alo, bhi) + _dot_nt(ahi, blo)

    lags = []
    for k in range(CHUNK):
        pr, pi_ = lam(k)
        fr, fi = _cmul(br, bi, pr, pi_)
        s = CHUNK - 1 - k
        f_ref[s * OCT:(s + 1) * OCT, :] = jnp.concatenate([fr, fi], axis=1).astype(BF16)
        lags.append((dot3(fr, chi_r, clo_r) - dot3(fi, chi_i, clo_i)).astype(BF16))
        pr, pi_ = lam(k + 1)
        gr, gi = _cmul(cr, ci, pr, pi_)
        gt_ref[k * OCT:(k + 1) * OCT, :] = jnp.concatenate([gr, -gi], axis=1).astype(BF16)
    zero = jnp.zeros((OCT, OCT), BF16)
    for s in range(CHUNK):
        for t in range(CHUNK):
            m_ref[s * OCT:(s + 1) * OCT, t * OCT:(t + 1) * OCT] = lags[t - s] if t >= s else zero


def _cmul(ar, ai, br, bi):
    return ar * br - ai * bi, ar * bi + ai * br


def _chunk_scan(sloc, tab_ref):
    R = sloc.shape[0]
    nb = R // SUBLANES
    sr, si = sloc[:, :OCT_STATES], sloc[:, OCT_STATES:]
    rowi = lax.broadcasted_iota(I32, (R, 1), 0)
    tr = jnp.where(rowi == 0, 0.0, pltpu.roll(sr, 1, axis=0))
    ti = jnp.where(rowi == 0, 0.0, pltpu.roll(si, 1, axis=0))
    for lvl, d in enumerate((1, 2, 4)):
        mr = tab_ref[lvl * SUBLANES:(lvl + 1) * SUBLANES, :OCT_STATES]
        mi = tab_ref[lvl * SUBLANES:(lvl + 1) * SUBLANES, OCT_STATES:]
        mr = jnp.concatenate([mr] * nb, axis=0)
        mi = jnp.concatenate([mi] * nb, axis=0)
        pr, pi_ = _cmul(mr, mi, pltpu.roll(tr, d, axis=0), pltpu.roll(ti, d, axis=0))
        tr, ti = tr + pr, ti + pi_
    pwr = tab_ref[3 * SUBLANES:4 * SUBLANES, :OCT_STATES]
    pwi = tab_ref[3 * SUBLANES:4 * SUBLANES, OCT_STATES:]
    cr = jnp.zeros((1, OCT_STATES), F32)
    ci = jnp.zeros((1, OCT_STATES), F32)
    out_r, out_i = [], []
    for k in range(nb):
        ar = tr[k * SUBLANES:(k + 1) * SUBLANES, :]
        ai = ti[k * SUBLANES:(k + 1) * SUBLANES, :]
        pr, pi_ = _cmul(pwr, pwi, jnp.broadcast_to(cr, ar.shape), jnp.broadcast_to(ci, ai.shape))
        hr, hi = ar + pr, ai + pi_
        out_r.append(hr)
        out_i.append(hi)
        cr, ci = hr[SUBLANES - 1:, :], hi[SUBLANES - 1:, :]
    hin = jnp.concatenate([jnp.concatenate(out_r, axis=0), jnp.concatenate(out_i, axis=0)], axis=1)
    lr, li = pwr[0:1, :], pwi[0:1, :]
    fr, fi = _cmul(lr, li, cr, ci)
    fin = jnp.concatenate([fr + sr[R - 1:, :], fi + si[R - 1:, :]], axis=1)
    return hin, fin


def _ssm_prompt_body(u_ref, bbc_ref, ccc_ref, pw_ref, tab_ref, d_ref, y_ref, hout_ref,
                     f_ref, gt_ref, m_ref):
    @pl.when(pl.program_id(1) == 0)
    def _():
        _build_weights(bbc_ref, ccc_ref, pw_ref, f_ref, gt_ref, m_ref)
    R = SEQ // CHUNK
    xs = [u_ref[pl.ds(s, R, stride=CHUNK), :] for s in range(CHUNK)]
    xb = jnp.concatenate(xs, axis=1).astype(BF16)
    sloc = jnp.dot(xb, f_ref[...], preferred_element_type=F32)
    hin, fin = _chunk_scan(sloc, tab_ref)
    y = (jnp.dot(xb, m_ref[...], preferred_element_type=F32)
         + _dot_nt(hin.astype(BF16), gt_ref[...]))
    for t in range(CHUNK):
        yt = y[:, t * OCT:(t + 1) * OCT] + d_ref[...] * xs[t]
        y_ref[pl.ds(t, R, stride=CHUNK), :] = _gelu_tanh(yt)
    hout_ref[...] = fin


def _ssm_weight_specs(n_axes):
    if n_axes == 2:
        im3 = lambda o, b: (o, 0, 0)
    else:
        im3 = lambda o: (o, 0, 0)
    return [
        pl.BlockSpec((None, OCT, 2 * SSM_STATE), im3),
        pl.BlockSpec((None, OCT, 2 * SSM_STATE), im3),
        pl.BlockSpec((None, 2 * SUBLANES, SW), im3),
        pl.BlockSpec((None, 4 * SUBLANES, SW), im3),
        pl.BlockSpec((None, 1, OCT), im3),
    ]


_SSM_SCRATCH = [pltpu.VMEM((CW, SW), BF16), pltpu.VMEM((CW, SW), BF16), pltpu.VMEM((CW, CW), BF16)]


def _ssm_prompt(proj, tables):
    col0 = POOL_WIDTH // OCT
    return pl.pallas_call(
        _ssm_prompt_body,
        grid=(N_OCT, BATCH),
        in_specs=[pl.BlockSpec((SEQ, OCT), lambda o, b: (b, col0 + o))] + _ssm_weight_specs(2),
        out_specs=(
            pl.BlockSpec((SEQ, OCT), lambda o, b: (b, o)),
            pl.BlockSpec((None, 1, SW), lambda o, b: (b * N_OCT + o, 0, 0)),
        ),
        out_shape=(
            jax.ShapeDtypeStruct((T_ALL, SSM_WIDTH), F32),
            jax.ShapeDtypeStruct((BATCH * N_OCT, 1, SW), F32),
        ),
        scratch_shapes=_SSM_SCRATCH,
        compiler_params=_cparams(2),
        name="ssm_prompt",
    )(proj, *tables)


def _ssm_sample_body(u_ref, h0r_ref, h0i_ref, bbc_ref, ccc_ref, pw_ref, tab_ref, d_ref, _dst,
                     y_ref, hr_ref, hi_ref, f_ref, gt_ref, m_ref):
    _build_weights(bbc_ref, ccc_ref, pw_ref, f_ref, gt_ref, m_ref)
    B = DEC_BATCH
    xs = [u_ref[B * s:B * (s + 1), :] for s in range(CHUNK)]
    xb = jnp.concatenate(xs, axis=1).astype(BF16)
    sloc = jnp.dot(xb, f_ref[...], preferred_element_type=F32)
    h0r, h0i = h0r_ref[...], h0i_ref[...]
    hin = jnp.concatenate([h0r, h0i], axis=1).astype(BF16)
    y = (jnp.dot(xb, m_ref[...], preferred_element_type=F32)
         + _dot_nt(hin, gt_ref[...]))
    for t in range(CHUNK):
        yt = y[:, t * OCT:(t + 1) * OCT] + d_ref[...] * xs[t]
        y_ref[B * t:B * (t + 1), :] = _gelu_tanh(yt)
    lr = tab_ref[3 * SUBLANES:3 * SUBLANES + 1, :OCT_STATES]
    li = tab_ref[3 * SUBLANES:3 * SUBLANES + 1, OCT_STATES:]
    nr, ni = _cmul(lr, li, h0r, h0i)
    hr_ref[...] = nr + sloc[:, :OCT_STATES]
    hi_ref[...] = ni + sloc[:, OCT_STATES:]


def _ssm_sample(proj, h0r, h0i, tables, y_act):
    col0 = POOL_WIDTH // OCT
    blk = T_PROMPT // T_SAMPLE
    st_spec = pl.BlockSpec((DEC_BATCH, OCT_STATES), lambda o: (0, o))
    return pl.pallas_call(
        _ssm_sample_body,
        grid=(N_OCT,),
        in_specs=[pl.BlockSpec((T_SAMPLE, OCT), lambda o: (blk, col0 + o)), st_spec, st_spec]
        + _ssm_weight_specs(1) + [pl.BlockSpec(memory_space=pl.ANY)],
        out_specs=(pl.BlockSpec((T_SAMPLE, OCT), lambda o: (blk, o)), st_spec, st_spec),
        out_shape=(
            jax.ShapeDtypeStruct((T_ALL, SSM_WIDTH), F32),
            jax.ShapeDtypeStruct((DEC_BATCH, SSM_GROUPS * SSM_STATE), F32),
            jax.ShapeDtypeStruct((DEC_BATCH, SSM_GROUPS * SSM_STATE), F32),
        ),
        scratch_shapes=_SSM_SCRATCH,
        input_output_aliases={8: 0},
        compiler_params=_cparams(1),
        name="ssm_sample",
    )(proj, h0r, h0i, *tables, y_act)


PM_TM = 256
PM_PROMPT_BLOCKS = T_PROMPT // PM_TM


def _postmix_body(ya_ref, gp_ref, gs_ref, yp_ref, xp_ref, xs_ref, wa_ref, wb_ref, wo_ref,
                  gf_ref, wrh_ref, wrl_ref, br_ref, h_ref, tn_ref, rt_ref, cnt_out_ref, cnt_ref):
    i = pl.program_id(0)
    @pl.when(i == 0)
    def _():
        cnt_ref[...] = jnp.zeros_like(cnt_ref)
    ya = ya_ref[...].astype(BF16)
    a = jnp.dot(ya, wa_ref[...], preferred_element_type=F32)
    bg = jnp.dot(ya, wb_ref[...], preferred_element_type=F32)
    y_ssm = a * _sigmoid(bg)
    merged = _sigmoid(gp_ref[...]) * yp_ref[...] + _sigmoid(gs_ref[...]) * y_ssm
    x = jnp.where(i < PM_PROMPT_BLOCKS, xp_ref[...], xs_ref[...])
    h = x + jnp.dot(merged.astype(BF16), wo_ref[...], preferred_element_type=F32)
    h_ref[...] = h
    inv = lax.rsqrt(jnp.mean(h * h, axis=-1, keepdims=True) + EPS)
    tn = (h * inv) * gf_ref[...]
    tn_ref[...] = tn
    t_hi = tn.astype(BF16)
    t_lo = (tn - t_hi.astype(F32)).astype(BF16)
    wrh = wrh_ref[...]
    logits = (jnp.dot(t_hi, wrh, preferred_element_type=F32)
              + jnp.dot(t_lo, wrh, preferred_element_type=F32)
              + jnp.dot(t_hi, wrl_ref[...], preferred_element_type=F32)) + br_ref[...]
    lane = lax.broadcasted_iota(I32, (PM_TM, ROUTE_LANES), 1)
    neg = jnp.float32(-jnp.inf)
    big = jnp.int32(1 << 20)
    is_g = lane < MOE_GROUPS
    gmax = jnp.max(jnp.where(is_g, logits, neg), axis=1, keepdims=True)
    g_idx = jnp.min(jnp.where(is_g & (logits == gmax), lane, big), axis=1, keepdims=True)
    g_den = jnp.sum(jnp.where(is_g, jnp.exp(logits - gmax), 0.0), axis=1, keepdims=True)
    g_val = 1.0 / g_den
    e_lane = lane - EXP_LANE0
    sel = (e_lane >= 0) & (e_lane < MOE_EXPERTS) & ((e_lane >> 3) == g_idx)
    m1 = jnp.max(jnp.where(sel, logits, neg), axis=1, keepdims=True)
    i1 = jnp.min(jnp.where(sel & (logits == m1), lane, big), axis=1, keepdims=True)
    sel2 = sel & (lane != i1)
    m2 = jnp.max(jnp.where(sel2, logits, neg), axis=1, keepdims=True)
    i2 = jnp.min(jnp.where(sel2 & (logits == m2), lane, big), axis=1, keepdims=True)
    e2 = jnp.exp(m2 - m1)
    w1 = g_val / (1.0 + e2)
    w2 = g_val * e2 / (1.0 + e2)
    oh1 = lane == i1
    oh2 = lane == i2
    oh = jnp.where(oh1 | oh2, 1.0, 0.0)
    rr = lax.broadcasted_iota(I32, (PM_TM, PM_TM), 0)
    cc = lax.broadcasted_iota(I32, (PM_TM, PM_TM), 1)
    tri = jnp.where(cc < rr, 1.0, 0.0).astype(BF16)
    base = cnt_ref[...] + jnp.dot(tri, oh.astype(BF16), preferred_element_type=F32)
    rank1 = jnp.sum(jnp.where(oh1, base, 0.0), axis=1, keepdims=True)
    rank2 = jnp.sum(jnp.where(oh2, base, 0.0), axis=1, keepdims=True)
    cnt_ref[...] = cnt_ref[...] + jnp.sum(oh, axis=0, keepdims=True)
    cnt_out_ref[...] = cnt_ref[...]
    rt = jnp.where(lane == 0, w1, 0.0)
    rt = jnp.where(lane == 1, w2, rt)
    rt = jnp.where(lane == 2, rank1, rt)
    rt = jnp.where(lane == 3, rank2, rt)
    rt = jnp.where(lane == 4, (i1 - EXP_LANE0).astype(F32), rt)
    rt = jnp.where(lane == 5, (i2 - EXP_LANE0).astype(F32), rt)
    rt_ref[...] = rt


def _postmix(y_act, proj, y_pool, xp, xs, wa, wb, wo, g_ffn, wr_hi, wr_lo, b_r):
    n = T_ALL // PM_TM
    npb = PM_PROMPT_BLOCKS
    const2 = lambda i: (0, 0)
    return pl.pallas_call(
        _postmix_body,
        grid=(n,),
        in_specs=[
            pl.BlockSpec((PM_TM, SSM_WIDTH), lambda i: (i, 0)),
            pl.BlockSpec((PM_TM, D_MODEL), lambda i: (i, 1)),
            pl.BlockSpec((PM_TM, D_MODEL), lambda i: (i, 2)),
            pl.BlockSpec((PM_TM, D_MODEL), lambda i: (i, 0)),
            pl.BlockSpec((PM_TM, D_MODEL), lambda i: (jnp.minimum(i, npb - 1), 0)),
            pl.BlockSpec((PM_TM, D_MODEL), lambda i: (jnp.maximum(i - npb, 0), 0)),
            pl.BlockSpec((SSM_WIDTH, D_MODEL), const2, pipeline_mode=pl.Buffered(1)),
            pl.BlockSpec((SSM_WIDTH, D_MODEL), const2, pipeline_mode=pl.Buffered(1)),
            pl.BlockSpec((D_MODEL, D_MODEL), const2, pipeline_mode=pl.Buffered(1)),
            pl.BlockSpec((1, D_MODEL), const2),
            pl.BlockSpec((D_MODEL, ROUTE_LANES), const2),
            pl.BlockSpec((D_MODEL, ROUTE_LANES), const2),
            pl.BlockSpec((1, ROUTE_LANES), const2),
        ],
        out_specs=(
            pl.BlockSpec((PM_TM, D_MODEL), lambda i: (i, 0)),
            pl.BlockSpec((PM_TM, D_MODEL), lambda i: (i, 0)),
            pl.BlockSpec((PM_TM, ROUTE_LANES), lambda i: (i, 0)),
            pl.BlockSpec((1, ROUTE_LANES), const2),
        ),
        out_shape=(
            jax.ShapeDtypeStruct((T_ALL, D_MODEL), F32),
            jax.ShapeDtypeStruct((T_ALL, D_MODEL), F32),
            jax.ShapeDtypeStruct((T_ALL, ROUTE_LANES), F32),
            jax.ShapeDtypeStruct((1, ROUTE_LANES), F32),
        ),
        scratch_shapes=[pltpu.VMEM((1, ROUTE_LANES), F32)],
        compiler_params=_cparams(1),
        name="postmix",
    )(y_act, proj, proj, y_pool, xp, xs, wa, wb, wo, g_ffn, wr_hi, wr_lo, b_r)


def _expert_body(q_ref, ie_ref, q0_ref, ni_ref, t_hbm, wg_ref, wu_ref, wd_ref, y_hbm,
                 plan_ref, xs_ref, yb_ref, wgb_ref, wub_ref, wdb_ref, gsem, ssem):
    w = pl.program_id(0)
    n_items = ni_ref[0]

    def gather_copy(tok, s, r):
        return pltpu.make_async_copy(t_hbm.at[pl.ds(tok, 1)], xs_ref.at[s, pl.ds(r, 1)], gsem.at[s])

    def scatter_copy(s, r, row):
        return pltpu.make_async_copy(yb_ref.at[s, pl.ds(r, 1)], y_hbm.at[pl.ds(row, 1)], ssem.at[s])

    def gather_starts(q0, s):
        return [functools.partial(
            lambda r: gather_copy(plan_ref[q0 + r] & SRC_MASK, s, r).start(priority=r % 2), r)
            for r in range(TME)]

    def scatter_starts(q0, s):
        return [functools.partial(
            lambda r: scatter_copy(s, r, plan_ref[q0 + r] >> SRC_BITS).start(priority=r % 2), r)
            for r in range(TME)]

    def wait_gather(s):
        for r in range(TME):
            gather_copy(0, s, r).wait()

    def wait_scatter(s):
        for r in range(TME):
            scatter_copy(s, r, 0).wait()

    @pl.when(w == 0)
    def _():
        def tail(i, c):
            plan_ref[N_ASSIGN + i] = (T_ALL + i) << SRC_BITS
            return c
        lax.fori_loop(0, TME, tail, 0)
        def invert(a, c):
            tok = a >> 1
            plan_ref[q_ref[a]] = (((a & 1) * Y_ROWS + tok) << SRC_BITS) | tok
            return c
        lax.fori_loop(0, N_ASSIGN, invert, 0, unroll=8)
        for start in gather_starts(q0_ref[0], 0):
            start()

    def item(s):
        wait_gather(s)
        @pl.when(w >= 2)
        def _():
            wait_scatter(s)
        @pl.when(w + 1 < n_items)
        def _():
            for start in gather_starts(q0_ref[w + 1], 1 - s):
                start()
        @pl.when(w >= 1)
        def _():
            for start in scatter_starts(q0_ref[w - 1], 1 - s):
                start()

        x = xs_ref[s].astype(BF16)
        hg = jnp.dot(x, wgb_ref[...], preferred_element_type=F32)
        hu = jnp.dot(x, wub_ref[...], preferred_element_type=F32)
        act = (hg * _sigmoid(hg)) * hu
        yb_ref[s] = jnp.dot(act.astype(BF16), wdb_ref[...], preferred_element_type=F32)

        @pl.when(w == n_items - 1)
        def _():
            @pl.when(w >= 1)
            def _():
                wait_scatter(1 - s)
            for start in scatter_starts(q0_ref[w], s):
                start()
            wait_scatter(s)

    @pl.when(w < n_items)
    def _():
        e = ie_ref[w]
        prev = ie_ref[jnp.maximum(w - 1, 0)]
        @pl.when((w == 0) | (prev != e))
        def _():
            wgb_ref[...] = wg_ref[...].astype(BF16)
            wub_ref[...] = wu_ref[...].astype(BF16)
            wdb_ref[...] = wd_ref[...].astype(BF16)
        for s in range(2):
            pl.when((w & 1) == s)(functools.partial(item, s))


def _experts(q_flat, item_e, item_q0, n_items, tn, w_eg, w_eu, w_ed):
    wmap = lambda w, q, ie, q0, ni: (ie[w], 0, 0)
    grid_spec = pltpu.PrefetchScalarGridSpec(
        num_scalar_prefetch=4,
        grid=(N_ITEMS_MAX,),
        in_specs=[
            pl.BlockSpec(memory_space=pl.ANY),
            pl.BlockSpec((None, D_MODEL, MOE_FF), wmap),
            pl.BlockSpec((None, D_MODEL, MOE_FF), wmap),
            pl.BlockSpec((None, MOE_FF, D_MODEL), wmap),
        ],
        out_specs=pl.BlockSpec(memory_space=pl.ANY),
        scratch_shapes=[
            pltpu.SMEM((N_ASSIGN + TME,), I32),
            pltpu.VMEM((2, TME, D_MODEL), F32),
            pltpu.VMEM((2, TME, D_MODEL), F32),
            pltpu.VMEM((D_MODEL, MOE_FF), BF16),
            pltpu.VMEM((D_MODEL, MOE_FF), BF16),
            pltpu.VMEM((MOE_FF, D_MODEL), BF16),
            pltpu.SemaphoreType.DMA((2,)),
            pltpu.SemaphoreType.DMA((2,)),
        ],
    )
    return pl.pallas_call(
        _expert_body,
        grid_spec=grid_spec,
        out_shape=jax.ShapeDtypeStruct((2 * Y_ROWS, D_MODEL), F32),
        compiler_params=_cparams(1),
        name="experts",
    )(q_flat, item_e, item_q0, n_items, tn, w_eg, w_eu, w_ed)


FN_TM = 256
FN_PROMPT_BLOCKS = T_PROMPT // FN_TM


def _final_body(h_ref, y0_ref, y1_ref, rt_ref, g_ref, op_ref, os_ref):
    i = pl.program_id(0)
    rt = rt_ref[...]
    z = h_ref[...] + rt[:, 0:1] * y0_ref[...] + rt[:, 1:2] * y1_ref[...]
    inv = lax.rsqrt(jnp.mean(z * z, axis=-1, keepdims=True) + EPS)
    out = (z * inv) * g_ref[...]
    @pl.when(i < FN_PROMPT_BLOCKS)
    def _():
        op_ref[...] = out
    @pl.when(i >= FN_PROMPT_BLOCKS)
    def _():
        os_ref[...] = out


def _final(h, y, route, g_final):
    n = T_ALL // FN_TM
    npb = FN_PROMPT_BLOCKS
    yoff = Y_ROWS // FN_TM
    return pl.pallas_call(
        _final_body,
        grid=(n,),
        in_specs=[
            pl.BlockSpec((FN_TM, D_MODEL), lambda i: (i, 0)),
            pl.BlockSpec((FN_TM, D_MODEL), lambda i: (i, 0)),
            pl.BlockSpec((FN_TM, D_MODEL), lambda i: (yoff + i, 0)),
            pl.BlockSpec((FN_TM, ROUTE_LANES), lambda i: (i, 0)),
            pl.BlockSpec((1, D_MODEL), lambda i: (0, 0)),
        ],
        out_specs=(
            pl.BlockSpec((FN_TM, D_MODEL), lambda i: (jnp.minimum(i, npb - 1), 0)),
            pl.BlockSpec((FN_TM, D_MODEL), lambda i: (jnp.maximum(i - npb, 0), 0)),
        ),
        out_shape=(
            jax.ShapeDtypeStruct((T_PROMPT, D_MODEL), F32),
            jax.ShapeDtypeStruct((T_SAMPLE, D_MODEL), F32),
        ),
        compiler_params=_cparams(1),
        name="final",
    )(h, y, y, route, g_final)


def _dispatch_plan(route, cnt):
    counts = cnt[0, EXP_LANE0:EXP_LANE0 + MOE_EXPERTS].astype(I32)
    cum = jnp.cumsum(counts)
    cumex = cum - counts
    rank = route[:, 2:4].astype(I32)
    eid = route[:, 4:6].astype(I32)
    onehot = eid[..., None] == jnp.arange(MOE_EXPERTS, dtype=I32)
    q = rank + jnp.sum(jnp.where(onehot, cumex, 0), axis=-1)
    q_flat = q.reshape(N_ASSIGN)
    tiles = (counts + (TME - 1)) // TME
    cumt = jnp.cumsum(tiles)
    n_items = cumt[-1]
    w = jnp.arange(N_ITEMS_MAX, dtype=I32)
    w_eff = jnp.minimum(w, n_items - 1)
    item_e = jnp.sum((w_eff[:, None] >= cumt[None, :]).astype(I32), axis=1)
    j = w_eff - (cumt - tiles)[item_e]
    item_q0 = cumex[item_e] + TME * j
    return q_flat, item_e, item_q0, n_items.reshape(1)


def kernel(x_prompt, x_sample, state_pool, state_ssm_re, state_ssm_im, g_mix, w_in, w_pool,
           pool_scale, ssm_a_re, ssm_a_im, ssm_log_dt, ssm_b_re, ssm_b_im, ssm_c_re, ssm_c_im,
           ssm_d, w_glu_a, w_glu_b, w_out, g_ffn, w_router_group, b_router_group,
           w_router_expert, b_router_expert, w_exp_gate, w_exp_up, w_exp_down, g_final):
    l = 0
    xp = x_prompt.reshape(T_PROMPT, D_MODEL)
    xs = x_sample.transpose(1, 0, 2).reshape(T_SAMPLE, D_MODEL)
    w_in_bf = w_in[l].astype(BF16)
    w_pool_bf = w_pool[l].astype(BF16)
    wa_bf = w_glu_a[l].astype(BF16)
    wb_bf = w_glu_b[l].astype(BF16)
    wo_bf = w_out[l].astype(BF16)
    g_mix2 = g_mix[l].reshape(1, D_MODEL)
    scale2 = pool_scale[l].reshape(1, D_MODEL)

    proj = _inproj(xp, g_mix2, w_in_bf, 0)
    proj = _inproj(xs, g_mix2, w_in_bf, T_PROMPT // IN_TM, dst=proj)

    y_pool, pool_tail = _pool_prompt(proj, w_pool_bf, scale2)
    hist_tm = state_pool[l].transpose(1, 0, 2)
    y_pool = _pool_sample(proj, hist_tm, w_pool_bf, scale2, y_pool)

    tables = _ssm_tables(ssm_a_re[l], ssm_a_im[l], ssm_log_dt[l], ssm_b_re[l], ssm_b_im[l],
                         ssm_c_re[l], ssm_c_im[l], ssm_d[l])
    y_act, h_prompt = _ssm_prompt(proj, tables)
    h0r = state_ssm_re[l].reshape(DEC_BATCH, SSM_GROUPS * SSM_STATE)
    h0i = state_ssm_im[l].reshape(DEC_BATCH, SSM_GROUPS * SSM_STATE)
    y_act, hs_re, hs_im = _ssm_sample(proj, h0r, h0i, tables, y_act)

    w_r = jnp.zeros((D_MODEL, ROUTE_LANES), F32)
    w_r = w_r.at[:, :MOE_GROUPS].set(w_router_group[l])
    w_r = w_r.at[:, EXP_LANE0:EXP_LANE0 + MOE_EXPERTS].set(w_router_expert[l])
    wr_hi = w_r.astype(BF16)
    wr_lo = (w_r - wr_hi.astype(F32)).astype(BF16)
    b_r = jnp.zeros((1, ROUTE_LANES), F32)
    b_r = b_r.at[0, :MOE_GROUPS].set(b_router_group[l])
    b_r = b_r.at[0, EXP_LANE0:EXP_LANE0 + MOE_EXPERTS].set(b_router_expert[l])

    h, tn, route, cnt = _postmix(y_act, proj, y_pool, xp, xs, wa_bf, wb_bf, wo_bf,
                                 g_ffn[l].reshape(1, D_MODEL), wr_hi, wr_lo, b_r)
    plan = _dispatch_plan(route, cnt)
    y = _experts(*plan, tn, w_exp_gate[l], w_exp_up[l], w_exp_down[l])
    yp, ys = _final(h, y, route, g_final.reshape(1, D_MODEL))

    y_prompt = yp.reshape(BATCH, SEQ, D_MODEL)
    y_sample = ys.reshape(DEC_SEQ, DEC_BATCH, D_MODEL).transpose(1, 0, 2)
    new_pool_prompt = pool_tail[:, HIST - POOL_BUF:, :][None]
    us = proj[T_PROMPT:, :POOL_WIDTH].reshape(DEC_SEQ, DEC_BATCH, POOL_WIDTH).transpose(1, 0, 2)
    new_pool_sample = jnp.concatenate([state_pool[l][:, DEC_SEQ:, :], us], axis=1)[None]
    hp = h_prompt.reshape(BATCH, N_OCT, 2, OCT_GROUPS, SSM_STATE).transpose(2, 0, 1, 3, 4)
    hp = hp.reshape(2, BATCH, SSM_GROUPS, SSM_STATE)
    shp = (1, DEC_BATCH, SSM_GROUPS, SSM_STATE)
    return (y_prompt, y_sample, new_pool_prompt, hp[0][None], hp[1][None], new_pool_sample,
            hs_re.reshape(shp), hs_im.reshape(shp))
```

```python
import functools
import math

import jax
import jax.numpy as jnp
from jax import lax
from jax.experimental import pallas as pl
from jax.experimental.pallas import tpu as pltpu

F32 = jnp.float32
BF16 = jnp.bfloat16
I32 = jnp.int32

D_MODEL = 2048
BATCH = 4
SEQ = 2048
DEC_BATCH = 128
DEC_SEQ = 8
PAST_LEN = 16384
POOL_WIDTH = D_MODEL // 2
POOL_WINDOWS = (2, 4, 8, 16)
POOL_GROUPS = len(POOL_WINDOWS)
POOL_GROUP_CH = POOL_WIDTH // POOL_GROUPS
POOL_OUT_CH = D_MODEL // POOL_GROUPS
POOL_BUF = max(POOL_WINDOWS) - 1
SSM_WIDTH = D_MODEL // 2
SSM_GROUP_CH = 16
SSM_GROUPS = SSM_WIDTH // SSM_GROUP_CH
SSM_STATE = 64
IN_WIDTH = POOL_WIDTH + SSM_WIDTH + 2 * D_MODEL
MOE_GROUPS = 4
MOE_EPG = 8
MOE_EXPERTS = MOE_GROUPS * MOE_EPG
MOE_FF = D_MODEL // 4
EPS = 1e-6

T_PROMPT = BATCH * SEQ
T_SAMPLE = DEC_BATCH * DEC_SEQ
T_ALL = T_PROMPT + T_SAMPLE

LANES = 128
SUBLANES = 8
VMEM_LIMIT = 56 * 1024 * 1024

CHUNK = 8
OCT = LANES
N_OCT = SSM_WIDTH // OCT
OCT_GROUPS = OCT // SSM_GROUP_CH
OCT_STATES = OCT_GROUPS * SSM_STATE
CW = CHUNK * OCT
SW = 2 * OCT_STATES

ROUTE_LANES = LANES
EXP_LANE0 = MOE_GROUPS
N_ASSIGN = 2 * T_ALL
TME = 256
N_ITEMS_MAX = N_ASSIGN // TME + MOE_EXPERTS
Y_PAD = 1024
Y_ROWS = T_ALL + Y_PAD
SRC_BITS = 14
SRC_MASK = (1 << SRC_BITS) - 1
assert T_ALL <= 1 << SRC_BITS and 2 * Y_ROWS < 1 << (31 - SRC_BITS)


def _cparams(n_axes):
    return pltpu.CompilerParams(dimension_semantics=("arbitrary",) * n_axes,
                                vmem_limit_bytes=VMEM_LIMIT)


def _sigmoid(x):
    return 1.0 / (1.0 + jnp.exp(-x))


def _gelu_tanh(x):
    c = math.sqrt(2.0 / math.pi)
    return 0.5 * x * (1.0 + jnp.tanh(c * (x + 0.044715 * (x * x * x))))


IN_TM = 1024
IN_TN = 1024
U_WIDTH = POOL_WIDTH + SSM_WIDTH
GATE_WIDTH = 2 * D_MODEL
IN_U_STEPS = U_WIDTH // IN_TN


def _inproj_body(x_ref, g_ref, w_ref, *rest):
    u_ref, gate_ref, xn_ref = rest[-3:]
    j = pl.program_id(1)
    @pl.when(j == 0)
    def _():
        x = x_ref[...]
        inv = lax.rsqrt(jnp.mean(x * x, axis=-1, keepdims=True) + EPS)
        xn_ref[...] = ((x * inv) * g_ref[...]).astype(BF16)
    acc = jnp.dot(xn_ref[...], w_ref[...], preferred_element_type=F32)
    @pl.when(j < IN_U_STEPS)
    def _():
        u_ref[...] = acc
    @pl.when(j >= IN_U_STEPS)
    def _():
        gate_ref[...] = _sigmoid(acc).astype(BF16)


def _inproj(x, g, w_bf, row_block0, dst=None):
    n_i = x.shape[0] // IN_TM
    in_specs = [
        pl.BlockSpec((IN_TM, D_MODEL), lambda i, j: (i, 0)),
        pl.BlockSpec((1, D_MODEL), lambda i, j: (0, 0)),
        pl.BlockSpec((D_MODEL, IN_TN), lambda i, j: (0, j)),
    ]
    args = [x, g, w_bf]
    aliases = {}
    if dst is not None:
        in_specs += [pl.BlockSpec(memory_space=pl.ANY)] * 2
        args += list(dst)
        aliases = {3: 0, 4: 1}
    return pl.pallas_call(
        _inproj_body,
        grid=(n_i, IN_WIDTH // IN_TN),
        in_specs=in_specs,
        out_specs=(
            pl.BlockSpec((IN_TM, IN_TN), lambda i, j: (i + row_block0, jnp.minimum(j, IN_U_STEPS - 1))),
            pl.BlockSpec((IN_TM, IN_TN), lambda i, j: (i + row_block0, jnp.maximum(j - IN_U_STEPS, 0))),
        ),
        out_shape=(
            jax.ShapeDtypeStruct((T_ALL, U_WIDTH), F32),
            jax.ShapeDtypeStruct((T_ALL, GATE_WIDTH), BF16),
        ),
        scratch_shapes=[pltpu.VMEM((IN_TM, D_MODEL), BF16)],
        input_output_aliases=aliases,
        compiler_params=_cparams(2),
        name="inproj",
    )(*args)


PP_TM = 512
HIST = 16


def _pool_project(pooled_g, g, w_ref, sc_ref, o_ref):
    y = jnp.dot(pooled_g.astype(BF16), w_ref[g], preferred_element_type=F32)
    lo, hi = g * POOL_OUT_CH, (g + 1) * POOL_OUT_CH
    o_ref[:, lo:hi] = (y * sc_ref[:, lo:hi]).astype(o_ref.dtype)


def _pool_prompt_body(u_ref, w_ref, sc_ref, o_ref, tail_ref, hist_ref):
    i = pl.program_id(1)
    @pl.when(i == 0)
    def _():
        hist_ref[...] = jnp.zeros_like(hist_ref)
    u = u_ref[...]
    ext = jnp.concatenate([hist_ref[...], u], axis=0)
    hist_ref[...] = u[PP_TM - HIST:, :]
    tail_ref[...] = u[PP_TM - HIST:, :]
    pos = i * PP_TM + lax.broadcasted_iota(I32, (PP_TM, 1), 0)
    for g, w in enumerate(POOL_WINDOWS):
        lo, hi = g * POOL_GROUP_CH, (g + 1) * POOL_GROUP_CH
        s = ext[:, lo:hi]
        d = 1
        while d < w:
            s = s + pltpu.roll(s, d, axis=0)
            d *= 2
        cnt = jnp.minimum(w, pos + 1).astype(F32)
        pooled = s[HIST:, :] / cnt - u[:, lo:hi]
        _pool_project(pooled, g, w_ref, sc_ref, o_ref)


def _pool_prompt(u, w_pool_bf, pool_scale):
    n_i = SEQ // PP_TM
    return pl.pallas_call(
        _pool_prompt_body,
        grid=(BATCH, n_i),
        in_specs=[
            pl.BlockSpec((PP_TM, POOL_WIDTH), lambda b, i: (b * n_i + i, 0)),
            pl.BlockSpec((POOL_GROUPS, POOL_GROUP_CH, POOL_OUT_CH), lambda b, i: (0, 0, 0)),
            pl.BlockSpec((1, D_MODEL), lambda b, i: (0, 0)),
        ],
        out_specs=(
            pl.BlockSpec((PP_TM, D_MODEL), lambda b, i: (b * n_i + i, 0)),
            pl.BlockSpec((None, HIST, POOL_WIDTH), lambda b, i: (b, 0, 0)),
        ),
        out_shape=(
            jax.ShapeDtypeStruct((T_ALL, D_MODEL), BF16),
            jax.ShapeDtypeStruct((BATCH, HIST, POOL_WIDTH), F32),
        ),
        scratch_shapes=[pltpu.VMEM((HIST, POOL_WIDTH), F32)],
        compiler_params=_cparams(2),
        name="pool_prompt",
    )(u, w_pool_bf, pool_scale)


def _pool_sample_body(u_ref, hist_ref, w_ref, sc_ref, _dst, o_ref):
    rows = [hist_ref[k] for k in range(POOL_BUF)]
    rows += [u_ref[DEC_BATCH * t:DEC_BATCH * (t + 1), :] for t in range(DEC_SEQ)]
    n = len(rows)
    for g, w in enumerate(POOL_WINDOWS):
        lo, hi = g * POOL_GROUP_CH, (g + 1) * POOL_GROUP_CH
        f = [r[:, lo:hi] for r in rows]
        cur = f
        d = 1
        while d < w:
            cur = [cur[k] + cur[k - d] if k - d >= 0 else cur[k] for k in range(n)]
            d *= 2
        pooled = jnp.concatenate(
            [cur[POOL_BUF + t] / float(w) - f[POOL_BUF + t] for t in range(DEC_SEQ)], axis=0)
        _pool_project(pooled, g, w_ref, sc_ref, o_ref)


def _pool_sample(u, hist_tm, w_pool_bf, pool_scale, y_pool):
    blk = T_PROMPT // T_SAMPLE
    return pl.pallas_call(
        _pool_sample_body,
        grid=(1,),
        in_specs=[
            pl.BlockSpec((T_SAMPLE, POOL_WIDTH), lambda i: (blk, 0)),
            pl.BlockSpec((POOL_BUF, DEC_BATCH, POOL_WIDTH), lambda i: (0, 0, 0)),
            pl.BlockSpec((POOL_GROUPS, POOL_GROUP_CH, POOL_OUT_CH), lambda i: (0, 0, 0)),
            pl.BlockSpec((1, D_MODEL), lambda i: (0, 0)),
            pl.BlockSpec(memory_space=pl.ANY),
        ],
        out_specs=pl.BlockSpec((T_SAMPLE, D_MODEL), lambda i: (blk, 0)),
        out_shape=jax.ShapeDtypeStruct((T_ALL, D_MODEL), BF16),
        input_output_aliases={4: 0},
        compiler_params=_cparams(1),
        name="pool_sample",
    )(u, hist_tm, w_pool_bf, pool_scale, y_pool)


def _ssm_tables(a_re, a_im, log_dt, b_re, b_im, c_re, c_im, d_skip):
    dt = jnp.exp(log_dt)[:, None]
    lr, li = a_re, a_im
    ab_re = jnp.exp(lr * dt) * jnp.cos(li * dt)
    ab_im = jnp.exp(lr * dt) * jnp.sin(li * dt)
    den = lr * lr + li * li
    nr, ni = ab_re - 1.0, ab_im
    q_re = (nr * lr + ni * li) / den
    q_im = (ni * lr - nr * li) / den
    bb_re = q_re[..., None] * b_re - q_im[..., None] * b_im
    bb_im = q_re[..., None] * b_im + q_im[..., None] * b_re

    def lam_rows(ks):
        k = jnp.asarray(ks, F32)[:, None, None]
        m = jnp.exp(k * lr * dt)
        re = (m * jnp.cos(k * li * dt)).reshape(len(ks), N_OCT, OCT_STATES)
        im = (m * jnp.sin(k * li * dt)).reshape(len(ks), N_OCT, OCT_STATES)
        return jnp.concatenate([re, im], axis=-1).transpose(1, 0, 2)

    def compact(re, im):
        v = jnp.concatenate([re, im], axis=-1)
        return v.reshape(N_OCT, OCT, 2 * SSM_STATE)

    bbc = compact(jnp.swapaxes(bb_re, 1, 2), jnp.swapaxes(bb_im, 1, 2))
    ccc = compact(c_re, c_im)
    pw = lam_rows(list(range(2 * SUBLANES)))
    r = jnp.arange(SUBLANES)[None, :, None]
    parts = [jnp.where(r >= dd, lam_rows([CHUNK * dd]), 0.0) for dd in (1, 2, 4)]
    parts.append(lam_rows([CHUNK * kk for kk in range(1, SUBLANES + 1)]))
    tab = jnp.concatenate(parts, axis=1)
    dsk = d_skip.reshape(N_OCT, 1, OCT)
    return bbc, ccc, pw, tab, dsk


def _split_bf16(x):
    hi = x.astype(BF16)
    return hi, (x - hi.astype(F32)).astype(BF16)


def _dot_nt(a, b):
    return lax.dot_general(a, b, (((1,), (1,)), ((), ())), preferred_element_type=F32)


def _build_weights(bbc_ref, ccc_ref, pw_ref, f_ref, gt_ref, m_ref):
    row_gi = lax.broadcasted_iota(I32, (OCT, 1), 0) >> 4
    col = lax.broadcasted_iota(I32, (1, SW), 1)
    col_gi = (col >> 6) & 7
    src = ((col >> 9) << 6) | (col & 63)
    k128 = lax.broadcasted_iota(I32, (2 * SSM_STATE, 1), 0)
    spread = jnp.where(k128 == src, 1.0, 0.0).astype(BF16)
    diag = row_gi == col_gi

    def expand(c_ref):
        hi, lo = _split_bf16(c_ref[...])
        d = (jnp.dot(hi, spread, preferred_element_type=F32)
             + jnp.dot(lo, spread, preferred_element_type=F32))
        d = jnp.where(diag, d, 0.0)
        return d[:, :OCT_STATES], d[:, OCT_STATES:]

    br, bi = expand(bbc_ref)
    cr, ci = expand(ccc_ref)
    chi_r, clo_r = _split_bf16(cr)
    chi_i, clo_i = _split_bf16(ci)

    def lam(k):
        return pw_ref[k:k + 1, :OCT_STATES], pw_ref[k:k + 1, OCT_STATES:]

    def dot3(a, bhi, blo):
        ahi, alo = _split_bf16(a)
        return _dot_nt(ahi, bhi) + _dot_nt(alo, bhi) + _dot_nt(ahi, blo)

    lags = []
    for k in range(CHUNK):
        pr, pi_ = lam(k)
        fr, fi = _cmul(br, bi, pr, pi_)
        s = CHUNK - 1 - k
        f_ref[s * OCT:(s + 1) * OCT, :] = jnp.concatenate([fr, fi], axis=1).astype(BF16)
        lags.append((dot3(fr, chi_r, clo_r) - dot3(fi, chi_i, clo_i)).astype(BF16))
        pr, pi_ = lam(k + 1)
        gr, gi = _cmul(cr, ci, pr, pi_)
        gt_ref[k * OCT:(k + 1) * OCT, :] = jnp.concatenate([gr, -gi], axis=1).astype(BF16)
    zero = jnp.zeros((OCT, OCT), BF16)
    for s in range(CHUNK):
        for t in range(CHUNK):
            m_ref[s * OCT:(s + 1) * OCT, t * OCT:(t + 1) * OCT] = lags[t - s] if t >= s else zero


def _cmul(ar, ai, br, bi):
    return ar * br - ai * bi, ar * bi + ai * br


def _chunk_scan(sloc, tab_ref):
    R = sloc.shape[0]
    nb = R // SUBLANES
    sr, si = sloc[:, :OCT_STATES], sloc[:, OCT_STATES:]
    rowi = lax.broadcasted_iota(I32, (R, 1), 0)
    tr = jnp.where(rowi == 0, 0.0, pltpu.roll(sr, 1, axis=0))
    ti = jnp.where(rowi == 0, 0.0, pltpu.roll(si, 1, axis=0))
    for lvl, d in enumerate((1, 2, 4)):
        mr = tab_ref[lvl * SUBLANES:(lvl + 1) * SUBLANES, :OCT_STATES]
        mi = tab_ref[lvl * SUBLANES:(lvl + 1) * SUBLANES, OCT_STATES:]
        mr = jnp.concatenate([mr] * nb, axis=0)
        mi = jnp.concatenate([mi] * nb, axis=0)
        pr, pi_ = _cmul(mr, mi, pltpu.roll(tr, d, axis=0), pltpu.roll(ti, d, axis=0))
        tr, ti = tr + pr, ti + pi_
    pwr = tab_ref[3 * SUBLANES:4 * SUBLANES, :OCT_STATES]
    pwi = tab_ref[3 * SUBLANES:4 * SUBLANES, OCT_STATES:]
    cr = jnp.zeros((1, OCT_STATES), F32)
    ci = jnp.zeros((1, OCT_STATES), F32)
    out_r, out_i = [], []
    for k in range(nb):
        ar = tr[k * SUBLANES:(k + 1) * SUBLANES, :]
        ai = ti[k * SUBLANES:(k + 1) * SUBLANES, :]
        pr, pi_ = _cmul(pwr, pwi, jnp.broadcast_to(cr, ar.shape), jnp.broadcast_to(ci, ai.shape))
        hr, hi = ar + pr, ai + pi_
        out_r.append(hr)
        out_i.append(hi)
        cr, ci = hr[SUBLANES - 1:, :], hi[SUBLANES - 1:, :]
    hin = jnp.concatenate([jnp.concatenate(out_r, axis=0), jnp.concatenate(out_i, axis=0)], axis=1)
    lr, li = pwr[0:1, :], pwi[0:1, :]
    fr, fi = _cmul(lr, li, cr, ci)
    fin = jnp.concatenate([fr + sr[R - 1:, :], fi + si[R - 1:, :]], axis=1)
    return hin, fin


def _ssm_prompt_body(u_ref, bbc_ref, ccc_ref, pw_ref, tab_ref, d_ref, y_ref, hout_ref,
                     f_ref, gt_ref, m_ref):
    @pl.when(pl.program_id(1) == 0)
    def _():
        _build_weights(bbc_ref, ccc_ref, pw_ref, f_ref, gt_ref, m_ref)
    R = SEQ // CHUNK
    xs = [u_ref[pl.ds(s, R, stride=CHUNK), :] for s in range(CHUNK)]
    xb = jnp.concatenate(xs, axis=1).astype(BF16)
    sloc = jnp.dot(xb, f_ref[...], preferred_element_type=F32)
    hin, fin = _chunk_scan(sloc, tab_ref)
    y = (jnp.dot(xb, m_ref[...], preferred_element_type=F32)
         + _dot_nt(hin.astype(BF16), gt_ref[...]))
    for t in range(CHUNK):
        yt = y[:, t * OCT:(t + 1) * OCT] + d_ref[...] * xs[t]
        y_ref[pl.ds(t, R, stride=CHUNK), :] = _gelu_tanh(yt)
    hout_ref[...] = fin


def _ssm_weight_specs(n_axes):
    if n_axes == 2:
        im3 = lambda o, b: (o, 0, 0)
    else:
        im3 = lambda o: (o, 0, 0)
    return [
        pl.BlockSpec((None, OCT, 2 * SSM_STATE), im3),
        pl.BlockSpec((None, OCT, 2 * SSM_STATE), im3),
        pl.BlockSpec((None, 2 * SUBLANES, SW), im3),
        pl.BlockSpec((None, 4 * SUBLANES, SW), im3),
        pl.BlockSpec((None, 1, OCT), im3),
    ]


_SSM_SCRATCH = [pltpu.VMEM((CW, SW), BF16), pltpu.VMEM((CW, SW), BF16), pltpu.VMEM((CW, CW), BF16)]


def _ssm_prompt(u, tables):
    col0 = POOL_WIDTH // OCT
    return pl.pallas_call(
        _ssm_prompt_body,
        grid=(N_OCT, BATCH),
        in_specs=[pl.BlockSpec((SEQ, OCT), lambda o, b: (b, col0 + o))] + _ssm_weight_specs(2),
        out_specs=(
            pl.BlockSpec((SEQ, OCT), lambda o, b: (b, o)),
            pl.BlockSpec((None, 1, SW), lambda o, b: (b * N_OCT + o, 0, 0)),
        ),
        out_shape=(
            jax.ShapeDtypeStruct((T_ALL, SSM_WIDTH), F32),
            jax.ShapeDtypeStruct((BATCH * N_OCT, 1, SW), F32),
        ),
        scratch_shapes=_SSM_SCRATCH,
        compiler_params=_cparams(2),
        name="ssm_prompt",
    )(u, *tables)


def _ssm_sample_body(u_ref, h0r_ref, h0i_ref, bbc_ref, ccc_ref, pw_ref, tab_ref, d_ref, _dst,
                     y_ref, hr_ref, hi_ref, f_ref, gt_ref, m_ref):
    _build_weights(bbc_ref, ccc_ref, pw_ref, f_ref, gt_ref, m_ref)
    B = DEC_BATCH
    xs = [u_ref[B * s:B * (s + 1), :] for s in range(CHUNK)]
    xb = jnp.concatenate(xs, axis=1).astype(BF16)
    sloc = jnp.dot(xb, f_ref[...], preferred_element_type=F32)
    h0r, h0i = h0r_ref[...], h0i_ref[...]
    hin = jnp.concatenate([h0r, h0i], axis=1).astype(BF16)
    y = (jnp.dot(xb, m_ref[...], preferred_element_type=F32)
         + _dot_nt(hin, gt_ref[...]))
    for t in range(CHUNK):
        yt = y[:, t * OCT:(t + 1) * OCT] + d_ref[...] * xs[t]
        y_ref[B * t:B * (t + 1), :] = _gelu_tanh(yt)
    lr = tab_ref[3 * SUBLANES:3 * SUBLANES + 1, :OCT_STATES]
    li = tab_ref[3 * SUBLANES:3 * SUBLANES + 1, OCT_STATES:]
    nr, ni = _cmul(lr, li, h0r, h0i)
    hr_ref[...] = nr + sloc[:, :OCT_STATES]
    hi_ref[...] = ni + sloc[:, OCT_STATES:]


def _ssm_sample(u, h0r, h0i, tables, y_act):
    col0 = POOL_WIDTH // OCT
    blk = T_PROMPT // T_SAMPLE
    st_spec = pl.BlockSpec((DEC_BATCH, OCT_STATES), lambda o: (0, o))
    return pl.pallas_call(
        _ssm_sample_body,
        grid=(N_OCT,),
        in_specs=[pl.BlockSpec((T_SAMPLE, OCT), lambda o: (blk, col0 + o)), st_spec, st_spec]
        + _ssm_weight_specs(1) + [pl.BlockSpec(memory_space=pl.ANY)],
        out_specs=(pl.BlockSpec((T_SAMPLE, OCT), lambda o: (blk, o)), st_spec, st_spec),
        out_shape=(
            jax.ShapeDtypeStruct((T_ALL, SSM_WIDTH), F32),
            jax.ShapeDtypeStruct((DEC_BATCH, SSM_GROUPS * SSM_STATE), F32),
            jax.ShapeDtypeStruct((DEC_BATCH, SSM_GROUPS * SSM_STATE), F32),
        ),
        scratch_shapes=_SSM_SCRATCH,
        input_output_aliases={8: 0},
        compiler_params=_cparams(1),
        name="ssm_sample",
    )(u, h0r, h0i, *tables, y_act)


PM_TM = 256
PM_PROMPT_BLOCKS = T_PROMPT // PM_TM


def _postmix_body(ya_ref, gp_ref, gs_ref, yp_ref, xp_ref, xs_ref, wa_ref, wb_ref, wo_ref,
                  gf_ref, wrh_ref, wrl_ref, br_ref, h_ref, tn_ref, rt_ref, cnt_out_ref, cnt_ref):
    i = pl.program_id(0)
    @pl.when(i == 0)
    def _():
        cnt_ref[...] = jnp.zeros_like(cnt_ref)
    ya = ya_ref[...].astype(BF16)
    a = jnp.dot(ya, wa_ref[...], preferred_element_type=F32)
    bg = jnp.dot(ya, wb_ref[...], preferred_element_type=F32)
    y_ssm = a * _sigmoid(bg)
    merged = (gp_ref[...].astype(F32) * yp_ref[...].astype(F32)
              + gs_ref[...].astype(F32) * y_ssm)
    x = jnp.where(i < PM_PROMPT_BLOCKS, xp_ref[...], xs_ref[...])
    h = x + jnp.dot(merged.astype(BF16), wo_ref[...], preferred_element_type=F32)
    h_ref[...] = h
    inv = lax.rsqrt(jnp.mean(h * h, axis=-1, keepdims=True) + EPS)
    tn = (h * inv) * gf_ref[...]
    tn_ref[...] = tn
    t_hi = tn.astype(BF16)
    t_lo = (tn - t_hi.astype(F32)).astype(BF16)
    wrh = wrh_ref[...]
    logits = (jnp.dot(t_hi, wrh, preferred_element_type=F32)
              + jnp.dot(t_lo, wrh, preferred_element_type=F32)
              + jnp.dot(t_hi, wrl_ref[...], preferred_element_type=F32)) + br_ref[...]
    lane = lax.broadcasted_iota(I32, (PM_TM, ROUTE_LANES), 1)
    neg = jnp.float32(-jnp.inf)
    big = jnp.int32(1 << 20)
    is_g = lane < MOE_GROUPS
    gmax = jnp.max(jnp.where(is_g, logits, neg), axis=1, keepdims=True)
    g_idx = jnp.min(jnp.where(is_g & (logits == gmax), lane, big), axis=1, keepdims=True)
    g_den = jnp.sum(jnp.where(is_g, jnp.exp(logits - gmax), 0.0), axis=1, keepdims=True)
    g_val = 1.0 / g_den
    e_lane = lane - EXP_LANE0
    sel = (e_lane >= 0) & (e_lane < MOE_EXPERTS) & ((e_lane >> 3) == g_idx)
    m1 = jnp.max(jnp.where(sel, logits, neg), axis=1, keepdims=True)
    i1 = jnp.min(jnp.where(sel & (logits == m1), lane, big), axis=1, keepdims=True)
    sel2 = sel & (lane != i1)
    m2 = jnp.max(jnp.where(sel2, logits, neg), axis=1, keepdims=True)
    i2 = jnp.min(jnp.where(sel2 & (logits == m2), lane, big), axis=1, keepdims=True)
    e2 = jnp.exp(m2 - m1)
    w1 = g_val / (1.0 + e2)
    w2 = g_val * e2 / (1.0 + e2)
    oh1 = lane == i1
    oh2 = lane == i2
    oh = jnp.where(oh1 | oh2, 1.0, 0.0)
    rr = lax.broadcasted_iota(I32, (PM_TM, PM_TM), 0)
    cc = lax.broadcasted_iota(I32, (PM_TM, PM_TM), 1)
    tri = jnp.where(cc < rr, 1.0, 0.0).astype(BF16)
    base = cnt_ref[...] + jnp.dot(tri, oh.astype(BF16), preferred_element_type=F32)
    rank1 = jnp.sum(jnp.where(oh1, base, 0.0), axis=1, keepdims=True)
    rank2 = jnp.sum(jnp.where(oh2, base, 0.0), axis=1, keepdims=True)
    cnt_ref[...] = cnt_ref[...] + jnp.sum(oh, axis=0, keepdims=True)
    cnt_out_ref[...] = cnt_ref[...]
    rt = jnp.where(lane == 0, w1, 0.0)
    rt = jnp.where(lane == 1, w2, rt)
    rt = jnp.where(lane == 2, rank1, rt)
    rt = jnp.where(lane == 3, rank2, rt)
    rt = jnp.where(lane == 4, (i1 - EXP_LANE0).astype(F32), rt)
    rt = jnp.where(lane == 5, (i2 - EXP_LANE0).astype(F32), rt)
    rt_ref[...] = rt


def _postmix(y_act, gates, y_pool, xp, xs, wa, wb, wo, g_ffn, wr_hi, wr_lo, b_r):
    n = T_ALL // PM_TM
    npb = PM_PROMPT_BLOCKS
    const2 = lambda i: (0, 0)
    return pl.pallas_call(
        _postmix_body,
        grid=(n,),
        in_specs=[
            pl.BlockSpec((PM_TM, SSM_WIDTH), lambda i: (i, 0)),
            pl.BlockSpec((PM_TM, D_MODEL), lambda i: (i, 0)),
            pl.BlockSpec((PM_TM, D_MODEL), lambda i: (i, 1)),
            pl.BlockSpec((PM_TM, D_MODEL), lambda i: (i, 0)),
            pl.BlockSpec((PM_TM, D_MODEL), lambda i: (jnp.minimum(i, npb - 1), 0)),
            pl.BlockSpec((PM_TM, D_MODEL), lambda i: (jnp.maximum(i - npb, 0), 0)),
            pl.BlockSpec((SSM_WIDTH, D_MODEL), const2, pipeline_mode=pl.Buffered(1)),
            pl.BlockSpec((SSM_WIDTH, D_MODEL), const2, pipeline_mode=pl.Buffered(1)),
            pl.BlockSpec((D_MODEL, D_MODEL), const2, pipeline_mode=pl.Buffered(1)),
            pl.BlockSpec((1, D_MODEL), const2),
            pl.BlockSpec((D_MODEL, ROUTE_LANES), const2),
            pl.BlockSpec((D_MODEL, ROUTE_LANES), const2),
            pl.BlockSpec((1, ROUTE_LANES), const2),
        ],
        out_specs=(
            pl.BlockSpec((PM_TM, D_MODEL), lambda i: (i, 0)),
            pl.BlockSpec((PM_TM, D_MODEL), lambda i: (i, 0)),
            pl.BlockSpec((PM_TM, ROUTE_LANES), lambda i: (i, 0)),
            pl.BlockSpec((1, ROUTE_LANES), const2),
        ),
        out_shape=(
            jax.ShapeDtypeStruct((T_ALL, D_MODEL), F32),
            jax.ShapeDtypeStruct((T_ALL, D_MODEL), F32),
            jax.ShapeDtypeStruct((T_ALL, ROUTE_LANES), F32),
            jax.ShapeDtypeStruct((1, ROUTE_LANES), F32),
        ),
        scratch_shapes=[pltpu.VMEM((1, ROUTE_LANES), F32)],
        compiler_params=_cparams(1),
        name="postmix",
    )(y_act, gates, gates, y_pool, xp, xs, wa, wb, wo, g_ffn, wr_hi, wr_lo, b_r)


def _expert_body(q_ref, ie_ref, q0_ref, ni_ref, t_hbm, wg_ref, wu_ref, wd_ref, y_hbm,
                 plan_ref, xs_ref, yb_ref, wgb_ref, wub_ref, wdb_ref, gsem, ssem):
    w = pl.program_id(0)
    n_items = ni_ref[0]

    def gather_copy(tok, s, r):
        return pltpu.make_async_copy(t_hbm.at[pl.ds(tok, 1)], xs_ref.at[s, pl.ds(r, 1)], gsem.at[s])

    def scatter_copy(s, r, row):
        return pltpu.make_async_copy(yb_ref.at[s, pl.ds(r, 1)], y_hbm.at[pl.ds(row, 1)], ssem.at[s])

    def gather_starts(q0, s):
        return [functools.partial(
            lambda r: gather_copy(plan_ref[q0 + r] & SRC_MASK, s, r).start(priority=r % 2), r)
            for r in range(TME)]

    def scatter_starts(q0, s):
        return [functools.partial(
            lambda r: scatter_copy(s, r, plan_ref[q0 + r] >> SRC_BITS).start(priority=r % 2), r)
            for r in range(TME)]

    def wait_gather(s):
        for r in range(TME):
            gather_copy(0, s, r).wait()

    def wait_scatter(s):
        for r in range(TME):
            scatter_copy(s, r, 0).wait()

    @pl.when(w == 0)
    def _():
        def tail(i, c):
            plan_ref[N_ASSIGN + i] = (T_ALL + i) << SRC_BITS
            return c
        lax.fori_loop(0, TME, tail, 0)
        def invert(a, c):
            tok = a >> 1
            plan_ref[q_ref[a]] = (((a & 1) * Y_ROWS + tok) << SRC_BITS) | tok
            return c
        lax.fori_loop(0, N_ASSIGN, invert, 0, unroll=8)
        for start in gather_starts(q0_ref[0], 0):
            start()

    def item(s):
        wait_gather(s)
        @pl.when(w >= 2)
        def _():
            wait_scatter(s)
        @pl.when(w + 1 < n_items)
        def _():
            for start in gather_starts(q0_ref[w + 1], 1 - s):
                start()
        @pl.when(w >= 1)
        def _():
            for start in scatter_starts(q0_ref[w - 1], 1 - s):
                start()

        x = xs_ref[s].astype(BF16)
        hg = jnp.dot(x, wgb_ref[...], preferred_element_type=F32)
        hu = jnp.dot(x, wub_ref[...], preferred_element_type=F32)
        act = (hg * _sigmoid(hg)) * hu
        yb_ref[s] = jnp.dot(act.astype(BF16), wdb_ref[...], preferred_element_type=F32)

        @pl.when(w == n_items - 1)
        def _():
            @pl.when(w >= 1)
            def _():
                wait_scatter(1 - s)
            for start in scatter_starts(q0_ref[w], s):
                start()
            wait_scatter(s)

    @pl.when(w < n_items)
    def _():
        e = ie_ref[w]
        prev = ie_ref[jnp.maximum(w - 1, 0)]
        @pl.when((w == 0) | (prev != e))
        def _():
            wgb_ref[...] = wg_ref[...].astype(BF16)
            wub_ref[...] = wu_ref[...].astype(BF16)
            wdb_ref[...] = wd_ref[...].astype(BF16)
        for s in range(2):
            pl.when((w & 1) == s)(functools.partial(item, s))


def _experts(q_flat, item_e, item_q0, n_items, tn, w_eg, w_eu, w_ed):
    wmap = lambda w, q, ie, q0, ni: (ie[w], 0, 0)
    grid_spec = pltpu.PrefetchScalarGridSpec(
        num_scalar_prefetch=4,
        grid=(N_ITEMS_MAX,),
        in_specs=[
            pl.BlockSpec(memory_space=pl.ANY),
            pl.BlockSpec((None, D_MODEL, MOE_FF), wmap),
            pl.BlockSpec((None, D_MODEL, MOE_FF), wmap),
            pl.BlockSpec((None, MOE_FF, D_MODEL), wmap),
        ],
        out_specs=pl.BlockSpec(memory_space=pl.ANY),
        scratch_shapes=[
            pltpu.SMEM((N_ASSIGN + TME,), I32),
            pltpu.VMEM((2, TME, D_MODEL), F32),
            pltpu.VMEM((2, TME, D_MODEL), F32),
            pltpu.VMEM((D_MODEL, MOE_FF), BF16),
            pltpu.VMEM((D_MODEL, MOE_FF), BF16),
            pltpu.VMEM((MOE_FF, D_MODEL), BF16),
            pltpu.SemaphoreType.DMA((2,)),
            pltpu.SemaphoreType.DMA((2,)),
        ],
    )
    return pl.pallas_call(
        _expert_body,
        grid_spec=grid_spec,
        out_shape=jax.ShapeDtypeStruct((2 * Y_ROWS, D_MODEL), F32),
        compiler_params=_cparams(1),
        name="experts",
    )(q_flat, item_e, item_q0, n_items, tn, w_eg, w_eu, w_ed)


FN_TM = 256
FN_PROMPT_BLOCKS = T_PROMPT // FN_TM


def _final_body(h_ref, y0_ref, y1_ref, rt_ref, g_ref, op_ref, os_ref):
    i = pl.program_id(0)
    rt = rt_ref[...]
    z = h_ref[...] + rt[:, 0:1] * y0_ref[...] + rt[:, 1:2] * y1_ref[...]
    inv = lax.rsqrt(jnp.mean(z * z, axis=-1, keepdims=True) + EPS)
    out = (z * inv) * g_ref[...]
    @pl.when(i < FN_PROMPT_BLOCKS)
    def _():
        op_ref[...] = out
    @pl.when(i >= FN_PROMPT_BLOCKS)
    def _():
        os_ref[...] = out


def _final(h, y, route, g_final):
    n = T_ALL // FN_TM
    npb = FN_PROMPT_BLOCKS
    yoff = Y_ROWS // FN_TM
    return pl.pallas_call(
        _final_body,
        grid=(n,),
        in_specs=[
            pl.BlockSpec((FN_TM, D_MODEL), lambda i: (i, 0)),
            pl.BlockSpec((FN_TM, D_MODEL), lambda i: (i, 0)),
            pl.BlockSpec((FN_TM, D_MODEL), lambda i: (yoff + i, 0)),
            pl.BlockSpec((FN_TM, ROUTE_LANES), lambda i: (i, 0)),
            pl.BlockSpec((1, D_MODEL), lambda i: (0, 0)),
        ],
        out_specs=(
            pl.BlockSpec((FN_TM, D_MODEL), lambda i: (jnp.minimum(i, npb - 1), 0)),
            pl.BlockSpec((FN_TM, D_MODEL), lambda i: (jnp.maximum(i - npb, 0), 0)),
        ),
        out_shape=(
            jax.ShapeDtypeStruct((T_PROMPT, D_MODEL), F32),
            jax.ShapeDtypeStruct((T_SAMPLE, D_MODEL), F32),
        ),
        compiler_params=_cparams(1),
        name="final",
    )(h, y, y, route, g_final)


def _dispatch_plan(route, cnt):
    counts = cnt[0, EXP_LANE0:EXP_LANE0 + MOE_EXPERTS].astype(I32)
    cum = jnp.cumsum(counts)
    cumex = cum - counts
    rank = route[:, 2:4].astype(I32)
    eid = route[:, 4:6].astype(I32)
    onehot = eid[..., None] == jnp.arange(MOE_EXPERTS, dtype=I32)
    q = rank + jnp.sum(jnp.where(onehot, cumex, 0), axis=-1)
    q_flat = q.reshape(N_ASSIGN)
    tiles = (counts + (TME - 1)) // TME
    cumt = jnp.cumsum(tiles)
    n_items = cumt[-1]
    w = jnp.arange(N_ITEMS_MAX, dtype=I32)
    w_eff = jnp.minimum(w, n_items - 1)
    item_e = jnp.sum((w_eff[:, None] >= cumt[None, :]).astype(I32), axis=1)
    j = w_eff - (cumt - tiles)[item_e]
    item_q0 = cumex[item_e] + TME * j
    return q_flat, item_e, item_q0, n_items.reshape(1)


def kernel(x_prompt, x_sample, state_pool, state_ssm_re, state_ssm_im, g_mix, w_in, w_pool,
           pool_scale, ssm_a_re, ssm_a_im, ssm_log_dt, ssm_b_re, ssm_b_im, ssm_c_re, ssm_c_im,
           ssm_d, w_glu_a, w_glu_b, w_out, g_ffn, w_router_group, b_router_group,
           w_router_expert, b_router_expert, w_exp_gate, w_exp_up, w_exp_down, g_final):
    l = 0
    xp = x_prompt.reshape(T_PROMPT, D_MODEL)
    xs = x_sample.transpose(1, 0, 2).reshape(T_SAMPLE, D_MODEL)
    w_in_bf = w_in[l].astype(BF16)
    w_pool_bf = w_pool[l].astype(BF16)
    wa_bf = w_glu_a[l].astype(BF16)
    wb_bf = w_glu_b[l].astype(BF16)
    wo_bf = w_out[l].astype(BF16)
    g_mix2 = g_mix[l].reshape(1, D_MODEL)
    scale2 = pool_scale[l].reshape(1, D_MODEL)

    u, gates = _inproj(xp, g_mix2, w_in_bf, 0)
    u, gates = _inproj(xs, g_mix2, w_in_bf, T_PROMPT // IN_TM, dst=(u, gates))

    y_pool, pool_tail = _pool_prompt(u, w_pool_bf, scale2)
    hist_tm = state_pool[l].transpose(1, 0, 2)
    y_pool = _pool_sample(u, hist_tm, w_pool_bf, scale2, y_pool)

    tables = _ssm_tables(ssm_a_re[l], ssm_a_im[l], ssm_log_dt[l], ssm_b_re[l], ssm_b_im[l],
                         ssm_c_re[l], ssm_c_im[l], ssm_d[l])
    y_act, h_prompt = _ssm_prompt(u, tables)
    h0r = state_ssm_re[l].reshape(DEC_BATCH, SSM_GROUPS * SSM_STATE)
    h0i = state_ssm_im[l].reshape(DEC_BATCH, SSM_GROUPS * SSM_STATE)
    y_act, hs_re, hs_im = _ssm_sample(u, h0r, h0i, tables, y_act)

    w_r = jnp.zeros((D_MODEL, ROUTE_LANES), F32)
    w_r = w_r.at[:, :MOE_GROUPS].set(w_router_group[l])
    w_r = w_r.at[:, EXP_LANE0:EXP_LANE0 + MOE_EXPERTS].set(w_router_expert[l])
    wr_hi = w_r.astype(BF16)
    wr_lo = (w_r - wr_hi.astype(F32)).astype(BF16)
    b_r = jnp.zeros((1, ROUTE_LANES), F32)
    b_r = b_r.at[0, :MOE_GROUPS].set(b_router_group[l])
    b_r = b_r.at[0, EXP_LANE0:EXP_LANE0 + MOE_EXPERTS].set(b_router_expert[l])

    h, tn, route, cnt = _postmix(y_act, gates, y_pool, xp, xs, wa_bf, wb_bf, wo_bf,
                                 g_ffn[l].reshape(1, D_MODEL), wr_hi, wr_lo, b_r)
    plan = _dispatch_plan(route, cnt)
    y = _experts(*plan, tn, w_exp_gate[l], w_exp_up[l], w_exp_down[l])
    yp, ys = _final(h, y, route, g_final.reshape(1, D_MODEL))

    y_prompt = yp.reshape(BATCH, SEQ, D_MODEL)
    y_sample = ys.reshape(DEC_SEQ, DEC_BATCH, D_MODEL).transpose(1, 0, 2)
    new_pool_prompt = pool_tail[:, HIST - POOL_BUF:, :][None]
    us = u[T_PROMPT:, :POOL_WIDTH].reshape(DEC_SEQ, DEC_BATCH, POOL_WIDTH).transpose(1, 0, 2)
    new_pool_sample = jnp.concatenate([state_pool[l][:, DEC_SEQ:, :], us], axis=1)[None]
    hp = h_prompt.reshape(BATCH, N_OCT, 2, OCT_GROUPS, SSM_STATE).transpose(2, 0, 1, 3, 4)
    hp = hp.reshape(2, BATCH, SSM_GROUPS, SSM_STATE)
    shp = (1, DEC_BATCH, SSM_GROUPS, SSM_STATE)
    return (y_prompt, y_sample, new_pool_prompt, hp[0][None], hp[1][None], new_pool_sample,
            hs_re.reshape(shp), hs_im.reshape(shp))
```

```python
import functools
import math

import jax
import jax.numpy as jnp
from jax import lax
from jax.experimental import pallas as pl
from jax.experimental.pallas import tpu as pltpu
from jax.experimental.pallas import tpu_sc as plsc

F32 = jnp.float32
BF16 = jnp.bfloat16
I32 = jnp.int32

D_MODEL = 2048
BATCH = 4
SEQ = 2048
DEC_BATCH = 128
DEC_SEQ = 8
PAST_LEN = 16384
POOL_WIDTH = D_MODEL // 2
POOL_WINDOWS = (2, 4, 8, 16)
POOL_GROUPS = len(POOL_WINDOWS)
POOL_GROUP_CH = POOL_WIDTH // POOL_GROUPS
POOL_OUT_CH = D_MODEL // POOL_GROUPS
POOL_BUF = max(POOL_WINDOWS) - 1
SSM_WIDTH = D_MODEL // 2
SSM_GROUP_CH = 16
SSM_GROUPS = SSM_WIDTH // SSM_GROUP_CH
SSM_STATE = 64
IN_WIDTH = POOL_WIDTH + SSM_WIDTH + 2 * D_MODEL
MOE_GROUPS = 4
MOE_EPG = 8
MOE_EXPERTS = MOE_GROUPS * MOE_EPG
MOE_FF = D_MODEL // 4
EPS = 1e-6

T_PROMPT = BATCH * SEQ
T_SAMPLE = DEC_BATCH * DEC_SEQ
T_ALL = T_PROMPT + T_SAMPLE

LANES = 128
SUBLANES = 8
VMEM_LIMIT = 56 * 1024 * 1024

CHUNK = 8
OCT = LANES
N_OCT = SSM_WIDTH // OCT
OCT_GROUPS = OCT // SSM_GROUP_CH
OCT_STATES = OCT_GROUPS * SSM_STATE
CW = CHUNK * OCT
SW = 2 * OCT_STATES

ROUTE_LANES = LANES
EXP_LANE0 = MOE_GROUPS
N_ASSIGN = 2 * T_ALL
TME = 256
N_ITEMS_MAX = N_ASSIGN // TME + MOE_EXPERTS
N_SLOTS = N_ITEMS_MAX * TME


def _cparams(n_axes):
    return pltpu.CompilerParams(dimension_semantics=("arbitrary",) * n_axes,
                                vmem_limit_bytes=VMEM_LIMIT)


def _sigmoid(x):
    return 1.0 / (1.0 + jnp.exp(-x))


def _gelu_tanh(x):
    c = math.sqrt(2.0 / math.pi)
    return 0.5 * x * (1.0 + jnp.tanh(c * (x + 0.044715 * (x * x * x))))


IN_TM = 1024
IN_TN = 1024
U_WIDTH = POOL_WIDTH + SSM_WIDTH
GATE_WIDTH = 2 * D_MODEL
IN_U_STEPS = U_WIDTH // IN_TN


def _inproj_body(x_ref, g_ref, w_ref, *rest):
    u_ref, gate_ref, xn_ref = rest[-3:]
    j = pl.program_id(1)
    @pl.when(j == 0)
    def _():
        x = x_ref[...]
        inv = lax.rsqrt(jnp.mean(x * x, axis=-1, keepdims=True) + EPS)
        xn_ref[...] = ((x * inv) * g_ref[...]).astype(BF16)
    acc = jnp.dot(xn_ref[...], w_ref[...], preferred_element_type=F32)
    @pl.when(j < IN_U_STEPS)
    def _():
        u_ref[...] = acc
    @pl.when(j >= IN_U_STEPS)
    def _():
        gate_ref[...] = _sigmoid(acc).astype(BF16)


def _inproj(x, g, w_bf, row_block0, dst=None):
    n_i = x.shape[0] // IN_TM
    in_specs = [
        pl.BlockSpec((IN_TM, D_MODEL), lambda i, j: (i, 0)),
        pl.BlockSpec((1, D_MODEL), lambda i, j: (0, 0)),
        pl.BlockSpec((D_MODEL, IN_TN), lambda i, j: (0, j)),
    ]
    args = [x, g, w_bf]
    aliases = {}
    if dst is not None:
        in_specs += [pl.BlockSpec(memory_space=pl.ANY)] * 2
        args += list(dst)
        aliases = {3: 0, 4: 1}
    return pl.pallas_call(
        _inproj_body,
        grid=(n_i, IN_WIDTH // IN_TN),
        in_specs=in_specs,
        out_specs=(
            pl.BlockSpec((IN_TM, IN_TN), lambda i, j: (i + row_block0, jnp.minimum(j, IN_U_STEPS - 1))),
            pl.BlockSpec((IN_TM, IN_TN), lambda i, j: (i + row_block0, jnp.maximum(j - IN_U_STEPS, 0))),
        ),
        out_shape=(
            jax.ShapeDtypeStruct((T_ALL, U_WIDTH), F32),
            jax.ShapeDtypeStruct((T_ALL, GATE_WIDTH), BF16),
        ),
        scratch_shapes=[pltpu.VMEM((IN_TM, D_MODEL), BF16)],
        input_output_aliases=aliases,
        compiler_params=_cparams(2),
        name="inproj",
    )(*args)


PP_TM = 512
HIST = 16


def _pool_project(pooled_g, g, w_ref, sc_ref, o_ref):
    y = jnp.dot(pooled_g.astype(BF16), w_ref[g], preferred_element_type=F32)
    lo, hi = g * POOL_OUT_CH, (g + 1) * POOL_OUT_CH
    o_ref[:, lo:hi] = (y * sc_ref[:, lo:hi]).astype(o_ref.dtype)


def _pool_prompt_body(u_ref, w_ref, sc_ref, o_ref, tail_ref, hist_ref):
    i = pl.program_id(1)
    @pl.when(i == 0)
    def _():
        hist_ref[...] = jnp.zeros_like(hist_ref)
    u = u_ref[...]
    ext = jnp.concatenate([hist_ref[...], u], axis=0)
    hist_ref[...] = u[PP_TM - HIST:, :]
    tail_ref[...] = u[PP_TM - HIST:, :]
    pos = i * PP_TM + lax.broadcasted_iota(I32, (PP_TM, 1), 0)
    for g, w in enumerate(POOL_WINDOWS):
        lo, hi = g * POOL_GROUP_CH, (g + 1) * POOL_GROUP_CH
        s = ext[:, lo:hi]
        d = 1
        while d < w:
            s = s + pltpu.roll(s, d, axis=0)
            d *= 2
        cnt = jnp.minimum(w, pos + 1).astype(F32)
        pooled = s[HIST:, :] / cnt - u[:, lo:hi]
        _pool_project(pooled, g, w_ref, sc_ref, o_ref)


def _pool_prompt(u, w_pool_bf, pool_scale):
    n_i = SEQ // PP_TM
    return pl.pallas_call(
        _pool_prompt_body,
        grid=(BATCH, n_i),
        in_specs=[
            pl.BlockSpec((PP_TM, POOL_WIDTH), lambda b, i: (b * n_i + i, 0)),
            pl.BlockSpec((POOL_GROUPS, POOL_GROUP_CH, POOL_OUT_CH), lambda b, i: (0, 0, 0)),
            pl.BlockSpec((1, D_MODEL), lambda b, i: (0, 0)),
        ],
        out_specs=(
            pl.BlockSpec((PP_TM, D_MODEL), lambda b, i: (b * n_i + i, 0)),
            pl.BlockSpec((None, HIST, POOL_WIDTH), lambda b, i: (b, 0, 0)),
        ),
        out_shape=(
            jax.ShapeDtypeStruct((T_ALL, D_MODEL), BF16),
            jax.ShapeDtypeStruct((BATCH, HIST, POOL_WIDTH), F32),
        ),
        scratch_shapes=[pltpu.VMEM((HIST, POOL_WIDTH), F32)],
        compiler_params=_cparams(2),
        name="pool_prompt",
    )(u, w_pool_bf, pool_scale)


def _pool_sample_body(u_ref, hist_ref, w_ref, sc_ref, _dst, o_ref):
    rows = [hist_ref[k] for k in range(POOL_BUF)]
    rows += [u_ref[DEC_BATCH * t:DEC_BATCH * (t + 1), :] for t in range(DEC_SEQ)]
    n = len(rows)
    for g, w in enumerate(POOL_WINDOWS):
        lo, hi = g * POOL_GROUP_CH, (g + 1) * POOL_GROUP_CH
        f = [r[:, lo:hi] for r in rows]
        cur = f
        d = 1
        while d < w:
            cur = [cur[k] + cur[k - d] if k - d >= 0 else cur[k] for k in range(n)]
            d *= 2
        pooled = jnp.concatenate(
            [cur[POOL_BUF + t] / float(w) - f[POOL_BUF + t] for t in range(DEC_SEQ)], axis=0)
        _pool_project(pooled, g, w_ref, sc_ref, o_ref)


def _pool_sample(u, hist_tm, w_pool_bf, pool_scale, y_pool):
    blk = T_PROMPT // T_SAMPLE
    return pl.pallas_call(
        _pool_sample_body,
        grid=(1,),
        in_specs=[
            pl.BlockSpec((T_SAMPLE, POOL_WIDTH), lambda i: (blk, 0)),
            pl.BlockSpec((POOL_BUF, DEC_BATCH, POOL_WIDTH), lambda i: (0, 0, 0)),
            pl.BlockSpec((POOL_GROUPS, POOL_GROUP_CH, POOL_OUT_CH), lambda i: (0, 0, 0)),
            pl.BlockSpec((1, D_MODEL), lambda i: (0, 0)),
            pl.BlockSpec(memory_space=pl.ANY),
        ],
        out_specs=pl.BlockSpec((T_SAMPLE, D_MODEL), lambda i: (blk, 0)),
        out_shape=jax.ShapeDtypeStruct((T_ALL, D_MODEL), BF16),
        input_output_aliases={4: 0},
        compiler_params=_cparams(1),
        name="pool_sample",
    )(u, hist_tm, w_pool_bf, pool_scale, y_pool)


def _ssm_tables(a_re, a_im, log_dt, b_re, b_im, c_re, c_im, d_skip):
    dt = jnp.exp(log_dt)[:, None]
    lr, li = a_re, a_im
    ab_re = jnp.exp(lr * dt) * jnp.cos(li * dt)
    ab_im = jnp.exp(lr * dt) * jnp.sin(li * dt)
    den = lr * lr + li * li
    nr, ni = ab_re - 1.0, ab_im
    q_re = (nr * lr + ni * li) / den
    q_im = (ni * lr - nr * li) / den
    bb_re = q_re[..., None] * b_re - q_im[..., None] * b_im
    bb_im = q_re[..., None] * b_im + q_im[..., None] * b_re

    def lam_rows(ks):
        k = jnp.asarray(ks, F32)[:, None, None]
        m = jnp.exp(k * lr * dt)
        re = (m * jnp.cos(k * li * dt)).reshape(len(ks), N_OCT, OCT_STATES)
        im = (m * jnp.sin(k * li * dt)).reshape(len(ks), N_OCT, OCT_STATES)
        return jnp.concatenate([re, im], axis=-1).transpose(1, 0, 2)

    def compact(re, im):
        v = jnp.concatenate([re, im], axis=-1)
        return v.reshape(N_OCT, OCT, 2 * SSM_STATE)

    bbc = compact(jnp.swapaxes(bb_re, 1, 2), jnp.swapaxes(bb_im, 1, 2))
    ccc = compact(c_re, c_im)
    pw = lam_rows(list(range(2 * SUBLANES)))
    r = jnp.arange(SUBLANES)[None, :, None]
    parts = [jnp.where(r >= dd, lam_rows([CHUNK * dd]), 0.0) for dd in (1, 2, 4)]
    parts.append(lam_rows([CHUNK * kk for kk in range(1, SUBLANES + 1)]))
    tab = jnp.concatenate(parts, axis=1)
    dsk = d_skip.reshape(N_OCT, 1, OCT)
    return bbc, ccc, pw, tab, dsk


def _split_bf16(x):
    hi = x.astype(BF16)
    return hi, (x - hi.astype(F32)).astype(BF16)


def _dot_nt(a, b):
    return lax.dot_general(a, b, (((1,), (1,)), ((), ())), preferred_element_type=F32)


def _build_weights(bbc_ref, ccc_ref, pw_ref, f_ref, gt_ref, m_ref):
    row_gi = lax.broadcasted_iota(I32, (OCT, 1), 0) >> 4
    col = lax.broadcasted_iota(I32, (1, SW), 1)
    col_gi = (col >> 6) & 7
    src = ((col >> 9) << 6) | (col & 63)
    k128 = lax.broadcasted_iota(I32, (2 * SSM_STATE, 1), 0)
    spread = jnp.where(k128 == src, 1.0, 0.0).astype(BF16)
    diag = row_gi == col_gi

    def expand(c_ref):
        hi, lo = _split_bf16(c_ref[...])
        d = (jnp.dot(hi, spread, preferred_element_type=F32)
             + jnp.dot(lo, spread, preferred_element_type=F32))
        d = jnp.where(diag, d, 0.0)
        return d[:, :OCT_STATES], d[:, OCT_STATES:]

    br, bi = expand(bbc_ref)
    cr, ci = expand(ccc_ref)
    chi_r, clo_r = _split_bf16(cr)
    chi_i, clo_i = _split_bf16(ci)

    def lam(k):
        return pw_ref[k:k + 1, :OCT_STATES], pw_ref[k:k + 1, OCT_STATES:]

    def dot3(a, bhi, blo):
        ahi, alo = _split_bf16(a)
        return _dot_nt(ahi, bhi) + _dot_nt(alo, bhi) + _dot_nt(ahi, blo)

    lags = []
    for k in range(CHUNK):
        pr, pi_ = lam(k)
        fr, fi = _cmul(br, bi, pr, pi_)
        s = CHUNK - 1 - k
        f_ref[s * OCT:(s + 1) * OCT, :] = jnp.concatenate([fr, fi], axis=1).astype(BF16)
        lags.append((dot3(fr, chi_r, clo_r) - dot3(fi, chi_i, clo_i)).astype(BF16))
        pr, pi_ = lam(k + 1)
        gr, gi = _cmul(cr, ci, pr, pi_)
        gt_ref[k * OCT:(k + 1) * OCT, :] = jnp.concatenate([gr, -gi], axis=1).astype(BF16)
    zero = jnp.zeros((OCT, OCT), BF16)
    for s in range(CHUNK):
        for t in range(CHUNK):
            m_ref[s * OCT:(s + 1) * OCT, t * OCT:(t + 1) * OCT] = lags[t - s] if t >= s else zero


def _cmul(ar, ai, br, bi):
    return ar * br - ai * bi, ar * bi + ai * br


def _chunk_scan(sloc, tab_ref):
    R = sloc.shape[0]
    nb = R // SUBLANES
    sr, si = sloc[:, :OCT_STATES], sloc[:, OCT_STATES:]
    rowi = lax.broadcasted_iota(I32, (R, 1), 0)
    tr = jnp.where(rowi == 0, 0.0, pltpu.roll(sr, 1, axis=0))
    ti = jnp.where(rowi == 0, 0.0, pltpu.roll(si, 1, axis=0))
    for lvl, d in enumerate((1, 2, 4)):
        mr = tab_ref[lvl * SUBLANES:(lvl + 1) * SUBLANES, :OCT_STATES]
        mi = tab_ref[lvl * SUBLANES:(lvl + 1) * SUBLANES, OCT_STATES:]
        mr = jnp.concatenate([mr] * nb, axis=0)
        mi = jnp.concatenate([mi] * nb, axis=0)
        pr, pi_ = _cmul(mr, mi, pltpu.roll(tr, d, axis=0), pltpu.roll(ti, d, axis=0))
        tr, ti = tr + pr, ti + pi_
    pwr = tab_ref[3 * SUBLANES:4 * SUBLANES, :OCT_STATES]
    pwi = tab_ref[3 * SUBLANES:4 * SUBLANES, OCT_STATES:]
    cr = jnp.zeros((1, OCT_STATES), F32)
    ci = jnp.zeros((1, OCT_STATES), F32)
    out_r, out_i = [], []
    for k in range(nb):
        ar = tr[k * SUBLANES:(k + 1) * SUBLANES, :]
        ai = ti[k * SUBLANES:(k + 1) * SUBLANES, :]
        pr, pi_ = _cmul(pwr, pwi, jnp.broadcast_to(cr, ar.shape), jnp.broadcast_to(ci, ai.shape))
        hr, hi = ar + pr, ai + pi_
        out_r.append(hr)
        out_i.append(hi)
        cr, ci = hr[SUBLANES - 1:, :], hi[SUBLANES - 1:, :]
    hin = jnp.concatenate([jnp.concatenate(out_r, axis=0), jnp.concatenate(out_i, axis=0)], axis=1)
    lr, li = pwr[0:1, :], pwi[0:1, :]
    fr, fi = _cmul(lr, li, cr, ci)
    fin = jnp.concatenate([fr + sr[R - 1:, :], fi + si[R - 1:, :]], axis=1)
    return hin, fin


def _ssm_prompt_body(u_ref, bbc_ref, ccc_ref, pw_ref, tab_ref, d_ref, y_ref, hout_ref,
                     f_ref, gt_ref, m_ref):
    @pl.when(pl.program_id(1) == 0)
    def _():
        _build_weights(bbc_ref, ccc_ref, pw_ref, f_ref, gt_ref, m_ref)
    R = SEQ // CHUNK
    xs = [u_ref[pl.ds(s, R, stride=CHUNK), :] for s in range(CHUNK)]
    xb = jnp.concatenate(xs, axis=1).astype(BF16)
    sloc = jnp.dot(xb, f_ref[...], preferred_element_type=F32)
    hin, fin = _chunk_scan(sloc, tab_ref)
    y = (jnp.dot(xb, m_ref[...], preferred_element_type=F32)
         + _dot_nt(hin.astype(BF16), gt_ref[...]))
    for t in range(CHUNK):
        yt = y[:, t * OCT:(t + 1) * OCT] + d_ref[...] * xs[t]
        y_ref[pl.ds(t, R, stride=CHUNK), :] = _gelu_tanh(yt)
    hout_ref[...] = fin


def _ssm_weight_specs(n_axes):
    if n_axes == 2:
        im3 = lambda o, b: (o, 0, 0)
    else:
        im3 = lambda o: (o, 0, 0)
    return [
        pl.BlockSpec((None, OCT, 2 * SSM_STATE), im3),
        pl.BlockSpec((None, OCT, 2 * SSM_STATE), im3),
        pl.BlockSpec((None, 2 * SUBLANES, SW), im3),
        pl.BlockSpec((None, 4 * SUBLANES, SW), im3),
        pl.BlockSpec((None, 1, OCT), im3),
    ]


_SSM_SCRATCH = [pltpu.VMEM((CW, SW), BF16), pltpu.VMEM((CW, SW), BF16), pltpu.VMEM((CW, CW), BF16)]


def _ssm_prompt(u, tables):
    col0 = POOL_WIDTH // OCT
    return pl.pallas_call(
        _ssm_prompt_body,
        grid=(N_OCT, BATCH),
        in_specs=[pl.BlockSpec((SEQ, OCT), lambda o, b: (b, col0 + o))] + _ssm_weight_specs(2),
        out_specs=(
            pl.BlockSpec((SEQ, OCT), lambda o, b: (b, o)),
            pl.BlockSpec((None, 1, SW), lambda o, b: (b * N_OCT + o, 0, 0)),
        ),
        out_shape=(
            jax.ShapeDtypeStruct((T_ALL, SSM_WIDTH), F32),
            jax.ShapeDtypeStruct((BATCH * N_OCT, 1, SW), F32),
        ),
        scratch_shapes=_SSM_SCRATCH,
        compiler_params=_cparams(2),
        name="ssm_prompt",
    )(u, *tables)


def _ssm_sample_body(u_ref, h0r_ref, h0i_ref, bbc_ref, ccc_ref, pw_ref, tab_ref, d_ref, _dst,
                     y_ref, hr_ref, hi_ref, f_ref, gt_ref, m_ref):
    _build_weights(bbc_ref, ccc_ref, pw_ref, f_ref, gt_ref, m_ref)
    B = DEC_BATCH
    xs = [u_ref[B * s:B * (s + 1), :] for s in range(CHUNK)]
    xb = jnp.concatenate(xs, axis=1).astype(BF16)
    sloc = jnp.dot(xb, f_ref[...], preferred_element_type=F32)
    h0r, h0i = h0r_ref[...], h0i_ref[...]
    hin = jnp.concatenate([h0r, h0i], axis=1).astype(BF16)
    y = (jnp.dot(xb, m_ref[...], preferred_element_type=F32)
         + _dot_nt(hin, gt_ref[...]))
    for t in range(CHUNK):
        yt = y[:, t * OCT:(t + 1) * OCT] + d_ref[...] * xs[t]
        y_ref[B * t:B * (t + 1), :] = _gelu_tanh(yt)
    lr = tab_ref[3 * SUBLANES:3 * SUBLANES + 1, :OCT_STATES]
    li = tab_ref[3 * SUBLANES:3 * SUBLANES + 1, OCT_STATES:]
    nr, ni = _cmul(lr, li, h0r, h0i)
    hr_ref[...] = nr + sloc[:, :OCT_STATES]
    hi_ref[...] = ni + sloc[:, OCT_STATES:]


def _ssm_sample(u, h0r, h0i, tables, y_act):
    col0 = POOL_WIDTH // OCT
    blk = T_PROMPT // T_SAMPLE
    st_spec = pl.BlockSpec((DEC_BATCH, OCT_STATES), lambda o: (0, o))
    return pl.pallas_call(
        _ssm_sample_body,
        grid=(N_OCT,),
        in_specs=[pl.BlockSpec((T_SAMPLE, OCT), lambda o: (blk, col0 + o)), st_spec, st_spec]
        + _ssm_weight_specs(1) + [pl.BlockSpec(memory_space=pl.ANY)],
        out_specs=(pl.BlockSpec((T_SAMPLE, OCT), lambda o: (blk, o)), st_spec, st_spec),
        out_shape=(
            jax.ShapeDtypeStruct((T_ALL, SSM_WIDTH), F32),
            jax.ShapeDtypeStruct((DEC_BATCH, SSM_GROUPS * SSM_STATE), F32),
            jax.ShapeDtypeStruct((DEC_BATCH, SSM_GROUPS * SSM_STATE), F32),
        ),
        scratch_shapes=_SSM_SCRATCH,
        input_output_aliases={8: 0},
        compiler_params=_cparams(1),
        name="ssm_sample",
    )(u, h0r, h0i, *tables, y_act)


PM_TM = 256
PM_PROMPT_BLOCKS = T_PROMPT // PM_TM


def _postmix_body(ya_ref, gp_ref, gs_ref, yp_ref, xp_ref, xs_ref, wa_ref, wb_ref, wo_ref,
                  gf_ref, wrh_ref, wrl_ref, br_ref, h_ref, tn_ref, rt_ref, cnt_out_ref, cnt_ref):
    i = pl.program_id(0)
    @pl.when(i == 0)
    def _():
        cnt_ref[...] = jnp.zeros_like(cnt_ref)
    ya = ya_ref[...].astype(BF16)
    a = jnp.dot(ya, wa_ref[...], preferred_element_type=F32)
    bg = jnp.dot(ya, wb_ref[...], preferred_element_type=F32)
    y_ssm = a * _sigmoid(bg)
    merged = (gp_ref[...].astype(F32) * yp_ref[...].astype(F32)
              + gs_ref[...].astype(F32) * y_ssm)
    x = jnp.where(i < PM_PROMPT_BLOCKS, xp_ref[...], xs_ref[...])
    h = x + jnp.dot(merged.astype(BF16), wo_ref[...], preferred_element_type=F32)
    h_ref[...] = h
    inv = lax.rsqrt(jnp.mean(h * h, axis=-1, keepdims=True) + EPS)
    tn = (h * inv) * gf_ref[...]
    tn_ref[...] = tn
    t_hi = tn.astype(BF16)
    t_lo = (tn - t_hi.astype(F32)).astype(BF16)
    wrh = wrh_ref[...]
    logits = (jnp.dot(t_hi, wrh, preferred_element_type=F32)
              + jnp.dot(t_lo, wrh, preferred_element_type=F32)
              + jnp.dot(t_hi, wrl_ref[...], preferred_element_type=F32)) + br_ref[...]
    lane = lax.broadcasted_iota(I32, (PM_TM, ROUTE_LANES), 1)
    neg = jnp.float32(-jnp.inf)
    big = jnp.int32(1 << 20)
    is_g = lane < MOE_GROUPS
    gmax = jnp.max(jnp.where(is_g, logits, neg), axis=1, keepdims=True)
    g_idx = jnp.min(jnp.where(is_g & (logits == gmax), lane, big), axis=1, keepdims=True)
    g_den = jnp.sum(jnp.where(is_g, jnp.exp(logits - gmax), 0.0), axis=1, keepdims=True)
    g_val = 1.0 / g_den
    e_lane = lane - EXP_LANE0
    sel = (e_lane >= 0) & (e_lane < MOE_EXPERTS) & ((e_lane >> 3) == g_idx)
    m1 = jnp.max(jnp.where(sel, logits, neg), axis=1, keepdims=True)
    i1 = jnp.min(jnp.where(sel & (logits == m1), lane, big), axis=1, keepdims=True)
    sel2 = sel & (lane != i1)
    m2 = jnp.max(jnp.where(sel2, logits, neg), axis=1, keepdims=True)
    i2 = jnp.min(jnp.where(sel2 & (logits == m2), lane, big), axis=1, keepdims=True)
    e2 = jnp.exp(m2 - m1)
    w1 = g_val / (1.0 + e2)
    w2 = g_val * e2 / (1.0 + e2)
    oh1 = lane == i1
    oh2 = lane == i2
    oh = jnp.where(oh1 | oh2, 1.0, 0.0)
    rr = lax.broadcasted_iota(I32, (PM_TM, PM_TM), 0)
    cc = lax.broadcasted_iota(I32, (PM_TM, PM_TM), 1)
    tri = jnp.where(cc < rr, 1.0, 0.0).astype(BF16)
    base = cnt_ref[...] + jnp.dot(tri, oh.astype(BF16), preferred_element_type=F32)
    rank1 = jnp.sum(jnp.where(oh1, base, 0.0), axis=1, keepdims=True)
    rank2 = jnp.sum(jnp.where(oh2, base, 0.0), axis=1, keepdims=True)
    cnt_ref[...] = cnt_ref[...] + jnp.sum(oh, axis=0, keepdims=True)
    cnt_out_ref[...] = cnt_ref[...]
    rt = jnp.where(lane == 0, w1, 0.0)
    rt = jnp.where(lane == 1, w2, rt)
    rt = jnp.where(lane == 2, rank1, rt)
    rt = jnp.where(lane == 3, rank2, rt)
    rt = jnp.where(lane == 4, (i1 - EXP_LANE0).astype(F32), rt)
    rt = jnp.where(lane == 5, (i2 - EXP_LANE0).astype(F32), rt)
    rt_ref[...] = rt


def _postmix(y_act, gates, y_pool, xp, xs, wa, wb, wo, g_ffn, wr_hi, wr_lo, b_r):
    n = T_ALL // PM_TM
    npb = PM_PROMPT_BLOCKS
    const2 = lambda i: (0, 0)
    return pl.pallas_call(
        _postmix_body,
        grid=(n,),
        in_specs=[
            pl.BlockSpec((PM_TM, SSM_WIDTH), lambda i: (i, 0)),
            pl.BlockSpec((PM_TM, D_MODEL), lambda i: (i, 0)),
            pl.BlockSpec((PM_TM, D_MODEL), lambda i: (i, 1)),
            pl.BlockSpec((PM_TM, D_MODEL), lambda i: (i, 0)),
            pl.BlockSpec((PM_TM, D_MODEL), lambda i: (jnp.minimum(i, npb - 1), 0)),
            pl.BlockSpec((PM_TM, D_MODEL), lambda i: (jnp.maximum(i - npb, 0), 0)),
            pl.BlockSpec((SSM_WIDTH, D_MODEL), const2, pipeline_mode=pl.Buffered(1)),
            pl.BlockSpec((SSM_WIDTH, D_MODEL), const2, pipeline_mode=pl.Buffered(1)),
            pl.BlockSpec((D_MODEL, D_MODEL), const2, pipeline_mode=pl.Buffered(1)),
            pl.BlockSpec((1, D_MODEL), const2),
            pl.BlockSpec((D_MODEL, ROUTE_LANES), const2),
            pl.BlockSpec((D_MODEL, ROUTE_LANES), const2),
            pl.BlockSpec((1, ROUTE_LANES), const2),
        ],
        out_specs=(
            pl.BlockSpec((PM_TM, D_MODEL), lambda i: (i, 0)),
            pl.BlockSpec((PM_TM, D_MODEL), lambda i: (i, 0)),
            pl.BlockSpec((PM_TM, ROUTE_LANES), lambda i: (i, 0)),
            pl.BlockSpec((1, ROUTE_LANES), const2),
        ),
        out_shape=(
            jax.ShapeDtypeStruct((T_ALL, D_MODEL), F32),
            jax.ShapeDtypeStruct((T_ALL, D_MODEL), F32),
            jax.ShapeDtypeStruct((T_ALL, ROUTE_LANES), F32),
            jax.ShapeDtypeStruct((1, ROUTE_LANES), F32),
        ),
        scratch_shapes=[pltpu.VMEM((1, ROUTE_LANES), F32)],
        compiler_params=_cparams(1),
        name="postmix",
    )(y_act, gates, gates, y_pool, xp, xs, wa, wb, wo, g_ffn, wr_hi, wr_lo, b_r)


SC_CH = 16


def _sc_workers():
    info = plsc.get_sparse_core_info()
    return info.num_cores, info.num_cores * info.num_subcores


def _sc_dispatch(tn, slots):
    n_cores, n_workers = _sc_workers()
    per_w = (T_ALL // SC_CH) // n_workers
    assert per_w * n_workers * SC_CH == T_ALL
    slots = slots.reshape(2, n_workers, per_w, SC_CH)

    @functools.partial(
        pl.kernel,
        mesh=plsc.VectorSubcoreMesh(core_axis_name="c", subcore_axis_name="s"),
        out_type=jax.ShapeDtypeStruct((N_SLOTS, D_MODEL), F32),
        scratch_types=[pltpu.VMEM((2, per_w, SC_CH), I32), pltpu.VMEM((SC_CH, D_MODEL), F32)],
    )
    def k(tn_hbm, slots_hbm, xs_hbm, idx_v, rows_v):
        wid = lax.axis_index("s") * n_cores + lax.axis_index("c")
        c0 = wid * per_w
        pltpu.sync_copy(slots_hbm.at[0, wid], idx_v.at[0])
        pltpu.sync_copy(slots_hbm.at[1, wid], idx_v.at[1])

        @pl.loop(0, per_w)
        def _(c):
            row0 = pl.multiple_of((c0 + c) * SC_CH, SC_CH)
            pltpu.sync_copy(tn_hbm.at[pl.ds(row0, SC_CH)], rows_v)
            pltpu.sync_copy(rows_v, xs_hbm.at[idx_v.at[0, c]])
            pltpu.sync_copy(rows_v, xs_hbm.at[idx_v.at[1, c]])

    return k(tn, slots)


def _sc_collect(ys, slots):
    n_cores, n_workers = _sc_workers()
    per_w = (N_ASSIGN // SC_CH) // n_workers
    assert per_w * n_workers * SC_CH == N_ASSIGN
    slots = slots.reshape(n_workers, per_w, SC_CH)

    @functools.partial(
        pl.kernel,
        mesh=plsc.VectorSubcoreMesh(core_axis_name="c", subcore_axis_name="s"),
        out_type=jax.ShapeDtypeStruct((N_ASSIGN, D_MODEL), F32),
        scratch_types=[pltpu.VMEM((per_w, SC_CH), I32), pltpu.VMEM((SC_CH, D_MODEL), F32)],
    )
    def k(ys_hbm, slots_hbm, out_hbm, idx_v, rows_v):
        wid = lax.axis_index("s") * n_cores + lax.axis_index("c")
        c0 = wid * per_w
        pltpu.sync_copy(slots_hbm.at[wid], idx_v)

        @pl.loop(0, per_w)
        def _(c):
            row0 = pl.multiple_of((c0 + c) * SC_CH, SC_CH)
            pltpu.sync_copy(ys_hbm.at[idx_v.at[c]], rows_v)
            pltpu.sync_copy(rows_v, out_hbm.at[pl.ds(row0, SC_CH)])

    return k(ys, slots)


def _expert_body(ie_ref, ni_ref, xs_ref, wg_ref, wu_ref, wd_ref, ys_ref, wgb_ref, wub_ref, wdb_ref):
    w = pl.program_id(0)

    @pl.when(w < ni_ref[0])
    def _():
        e = ie_ref[w]
        prev = ie_ref[jnp.maximum(w - 1, 0)]
        @pl.when((w == 0) | (prev != e))
        def _():
            wgb_ref[...] = wg_ref[...].astype(BF16)
            wub_ref[...] = wu_ref[...].astype(BF16)
            wdb_ref[...] = wd_ref[...].astype(BF16)
        x = xs_ref[...].astype(BF16)
        hg = jnp.dot(x, wgb_ref[...], preferred_element_type=F32)
        hu = jnp.dot(x, wub_ref[...], preferred_element_type=F32)
        act = (hg * _sigmoid(hg)) * hu
        ys_ref[...] = jnp.dot(act.astype(BF16), wdb_ref[...], preferred_element_type=F32)


def _experts(item_e, n_items, xs, w_eg, w_eu, w_ed):
    wmap = lambda w, ie, ni: (ie[w], 0, 0)
    rmap = lambda w, ie, ni: (jnp.minimum(w, ni[0] - 1), 0)
    grid_spec = pltpu.PrefetchScalarGridSpec(
        num_scalar_prefetch=2,
        grid=(N_ITEMS_MAX,),
        in_specs=[
            pl.BlockSpec((TME, D_MODEL), rmap),
            pl.BlockSpec((None, D_MODEL, MOE_FF), wmap),
            pl.BlockSpec((None, D_MODEL, MOE_FF), wmap),
            pl.BlockSpec((None, MOE_FF, D_MODEL), wmap),
        ],
        out_specs=pl.BlockSpec((TME, D_MODEL), rmap),
        scratch_shapes=[
            pltpu.VMEM((D_MODEL, MOE_FF), BF16),
            pltpu.VMEM((D_MODEL, MOE_FF), BF16),
            pltpu.VMEM((MOE_FF, D_MODEL), BF16),
        ],
    )
    return pl.pallas_call(
        _expert_body,
        grid_spec=grid_spec,
        out_shape=jax.ShapeDtypeStruct((N_SLOTS, D_MODEL), F32),
        compiler_params=_cparams(1),
        name="experts",
    )(item_e, n_items, xs, w_eg, w_eu, w_ed)


FN_TM = 256
FN_PROMPT_BLOCKS = T_PROMPT // FN_TM


def _final_body(h_ref, y0_ref, y1_ref, rt_ref, g_ref, op_ref, os_ref):
    i = pl.program_id(0)
    rt = rt_ref[...]
    z = h_ref[...] + rt[:, 0:1] * y0_ref[...] + rt[:, 1:2] * y1_ref[...]
    inv = lax.rsqrt(jnp.mean(z * z, axis=-1, keepdims=True) + EPS)
    out = (z * inv) * g_ref[...]
    @pl.when(i < FN_PROMPT_BLOCKS)
    def _():
        op_ref[...] = out
    @pl.when(i >= FN_PROMPT_BLOCKS)
    def _():
        os_ref[...] = out


def _final(h, y, route, g_final):
    n = T_ALL // FN_TM
    npb = FN_PROMPT_BLOCKS
    yoff = T_ALL // FN_TM
    return pl.pallas_call(
        _final_body,
        grid=(n,),
        in_specs=[
            pl.BlockSpec((FN_TM, D_MODEL), lambda i: (i, 0)),
            pl.BlockSpec((FN_TM, D_MODEL), lambda i: (i, 0)),
            pl.BlockSpec((FN_TM, D_MODEL), lambda i: (yoff + i, 0)),
            pl.BlockSpec((FN_TM, ROUTE_LANES), lambda i: (i, 0)),
            pl.BlockSpec((1, D_MODEL), lambda i: (0, 0)),
        ],
        out_specs=(
            pl.BlockSpec((FN_TM, D_MODEL), lambda i: (jnp.minimum(i, npb - 1), 0)),
            pl.BlockSpec((FN_TM, D_MODEL), lambda i: (jnp.maximum(i - npb, 0), 0)),
        ),
        out_shape=(
            jax.ShapeDtypeStruct((T_PROMPT, D_MODEL), F32),
            jax.ShapeDtypeStruct((T_SAMPLE, D_MODEL), F32),
        ),
        compiler_params=_cparams(1),
        name="final",
    )(h, y, y, route, g_final)


def _dispatch_plan(route, cnt):
    counts = cnt[0, EXP_LANE0:EXP_LANE0 + MOE_EXPERTS].astype(I32)
    tiles = (counts + (TME - 1)) // TME
    cumt = jnp.cumsum(tiles)
    pad_off = (cumt - tiles) * TME
    rank = route[:, 2:4].astype(I32)
    eid = route[:, 4:6].astype(I32)
    onehot = eid[..., None] == jnp.arange(MOE_EXPERTS, dtype=I32)
    slots = (rank + jnp.sum(jnp.where(onehot, pad_off, 0), axis=-1)).T
    n_items = cumt[-1]
    w_eff = jnp.minimum(jnp.arange(N_ITEMS_MAX, dtype=I32), n_items - 1)
    item_e = jnp.sum((w_eff[:, None] >= cumt[None, :]).astype(I32), axis=1)
    return slots, item_e, n_items.reshape(1)


def kernel(x_prompt, x_sample, state_pool, state_ssm_re, state_ssm_im, g_mix, w_in, w_pool,
           pool_scale, ssm_a_re, ssm_a_im, ssm_log_dt, ssm_b_re, ssm_b_im, ssm_c_re, ssm_c_im,
           ssm_d, w_glu_a, w_glu_b, w_out, g_ffn, w_router_group, b_router_group,
           w_router_expert, b_router_expert, w_exp_gate, w_exp_up, w_exp_down, g_final):
    l = 0
    xp = x_prompt.reshape(T_PROMPT, D_MODEL)
    xs = x_sample.transpose(1, 0, 2).reshape(T_SAMPLE, D_MODEL)
    w_in_bf = w_in[l].astype(BF16)
    w_pool_bf = w_pool[l].astype(BF16)
    wa_bf = w_glu_a[l].astype(BF16)
    wb_bf = w_glu_b[l].astype(BF16)
    wo_bf = w_out[l].astype(BF16)
    g_mix2 = g_mix[l].reshape(1, D_MODEL)
    scale2 = pool_scale[l].reshape(1, D_MODEL)

    u, gates = _inproj(xp, g_mix2, w_in_bf, 0)
    u, gates = _inproj(xs, g_mix2, w_in_bf, T_PROMPT // IN_TM, dst=(u, gates))

    y_pool, pool_tail = _pool_prompt(u, w_pool_bf, scale2)
    hist_tm = state_pool[l].transpose(1, 0, 2)
    y_pool = _pool_sample(u, hist_tm, w_pool_bf, scale2, y_pool)

    tables = _ssm_tables(ssm_a_re[l], ssm_a_im[l], ssm_log_dt[l], ssm_b_re[l], ssm_b_im[l],
                         ssm_c_re[l], ssm_c_im[l], ssm_d[l])
    y_act, h_prompt = _ssm_prompt(u, tables)
    h0r = state_ssm_re[l].reshape(DEC_BATCH, SSM_GROUPS * SSM_STATE)
    h0i = state_ssm_im[l].reshape(DEC_BATCH, SSM_GROUPS * SSM_STATE)
    y_act, hs_re, hs_im = _ssm_sample(u, h0r, h0i, tables, y_act)

    w_r = jnp.zeros((D_MODEL, ROUTE_LANES), F32)
    w_r = w_r.at[:, :MOE_GROUPS].set(w_router_group[l])
    w_r = w_r.at[:, EXP_LANE0:EXP_LANE0 + MOE_EXPERTS].set(w_router_expert[l])
    wr_hi = w_r.astype(BF16)
    wr_lo = (w_r - wr_hi.astype(F32)).astype(BF16)
    b_r = jnp.zeros((1, ROUTE_LANES), F32)
    b_r = b_r.at[0, :MOE_GROUPS].set(b_router_group[l])
    b_r = b_r.at[0, EXP_LANE0:EXP_LANE0 + MOE_EXPERTS].set(b_router_expert[l])

    h, tn, route, cnt = _postmix(y_act, gates, y_pool, xp, xs, wa_bf, wb_bf, wo_bf,
                                 g_ffn[l].reshape(1, D_MODEL), wr_hi, wr_lo, b_r)
    slots, item_e, n_items = _dispatch_plan(route, cnt)
    xs_sorted = _sc_dispatch(tn, slots)
    ys_sorted = _experts(item_e, n_items, xs_sorted, w_exp_gate[l], w_exp_up[l], w_exp_down[l])
    y = _sc_collect(ys_sorted, slots)
    yp, ys = _final(h, y, route, g_final.reshape(1, D_MODEL))

    y_prompt = yp.reshape(BATCH, SEQ, D_MODEL)
    y_sample = ys.reshape(DEC_SEQ, DEC_BATCH, D_MODEL).transpose(1, 0, 2)
    new_pool_prompt = pool_tail[:, HIST - POOL_BUF:, :][None]
    us = u[T_PROMPT:, :POOL_WIDTH].reshape(DEC_SEQ, DEC_BATCH, POOL_WIDTH).transpose(1, 0, 2)
    new_pool_sample = jnp.concatenate([state_pool[l][:, DEC_SEQ:, :], us], axis=1)[None]
    hp = h_prompt.reshape(BATCH, N_OCT, 2, OCT_GROUPS, SSM_STATE).transpose(2, 0, 1, 3, 4)
    hp = hp.reshape(2, BATCH, SSM_GROUPS, SSM_STATE)
    shp = (1, DEC_BATCH, SSM_GROUPS, SSM_STATE)
    return (y_prompt, y_sample, new_pool_prompt, hp[0][None], hp[1][None], new_pool_sample,
            hs_re.reshape(shp), hs_im.reshape(shp))
```

```python
import functools
import math

import jax
import jax.numpy as jnp
from jax import lax
from jax.experimental import pallas as pl
from jax.experimental.pallas import tpu as pltpu
from jax.experimental.pallas import tpu_sc as plsc

F32 = jnp.float32
BF16 = jnp.bfloat16
I32 = jnp.int32

D_MODEL = 2048
BATCH = 4
SEQ = 2048
DEC_BATCH = 128
DEC_SEQ = 8
PAST_LEN = 16384
POOL_WIDTH = D_MODEL // 2
POOL_WINDOWS = (2, 4, 8, 16)
POOL_GROUPS = len(POOL_WINDOWS)
POOL_GROUP_CH = POOL_WIDTH // POOL_GROUPS
POOL_OUT_CH = D_MODEL // POOL_GROUPS
POOL_BUF = max(POOL_WINDOWS) - 1
SSM_WIDTH = D_MODEL // 2
SSM_GROUP_CH = 16
SSM_GROUPS = SSM_WIDTH // SSM_GROUP_CH
SSM_STATE = 64
IN_WIDTH = POOL_WIDTH + SSM_WIDTH + 2 * D_MODEL
MOE_GROUPS = 4
MOE_EPG = 8
MOE_EXPERTS = MOE_GROUPS * MOE_EPG
MOE_FF = D_MODEL // 4
EPS = 1e-6

T_PROMPT = BATCH * SEQ
T_SAMPLE = DEC_BATCH * DEC_SEQ
T_ALL = T_PROMPT + T_SAMPLE

LANES = 128
SUBLANES = 8
VMEM_LIMIT = 56 * 1024 * 1024

CHUNK = 8
OCT = LANES
N_OCT = SSM_WIDTH // OCT
OCT_GROUPS = OCT // SSM_GROUP_CH
OCT_STATES = OCT_GROUPS * SSM_STATE
CW = CHUNK * OCT
SW = 2 * OCT_STATES

ROUTE_LANES = LANES
EXP_LANE0 = MOE_GROUPS
N_ASSIGN = 2 * T_ALL
TME = 256
N_ITEMS_MAX = N_ASSIGN // TME + MOE_EXPERTS
N_SLOTS = N_ITEMS_MAX * TME


def _cparams(n_axes):
    return pltpu.CompilerParams(dimension_semantics=("arbitrary",) * n_axes,
                                vmem_limit_bytes=VMEM_LIMIT)


def _sigmoid(x):
    return 1.0 / (1.0 + jnp.exp(-x))


def _gelu_tanh(x):
    c = math.sqrt(2.0 / math.pi)
    return 0.5 * x * (1.0 + jnp.tanh(c * (x + 0.044715 * (x * x * x))))


IN_TM = 1024
IN_TN = 1024
U_WIDTH = POOL_WIDTH + SSM_WIDTH
GATE_WIDTH = 2 * D_MODEL
IN_U_STEPS = U_WIDTH // IN_TN


def _inproj_body(x_ref, g_ref, w_ref, *rest):
    u_ref, gate_ref, xn_ref = rest[-3:]
    j = pl.program_id(1)
    @pl.when(j == 0)
    def _():
        x = x_ref[...]
        inv = lax.rsqrt(jnp.mean(x * x, axis=-1, keepdims=True) + EPS)
        xn_ref[...] = ((x * inv) * g_ref[...]).astype(BF16)
    acc = jnp.dot(xn_ref[...], w_ref[...], preferred_element_type=F32)
    @pl.when(j < IN_U_STEPS)
    def _():
        u_ref[...] = acc
    @pl.when(j >= IN_U_STEPS)
    def _():
        gate_ref[...] = _sigmoid(acc).astype(BF16)


def _inproj(x, g, w_bf, row_block0, dst=None):
    n_i = x.shape[0] // IN_TM
    in_specs = [
        pl.BlockSpec((IN_TM, D_MODEL), lambda i, j: (i, 0)),
        pl.BlockSpec((1, D_MODEL), lambda i, j: (0, 0)),
        pl.BlockSpec((D_MODEL, IN_TN), lambda i, j: (0, j)),
    ]
    args = [x, g, w_bf]
    aliases = {}
    if dst is not None:
        in_specs += [pl.BlockSpec(memory_space=pl.ANY)] * 2
        args += list(dst)
        aliases = {3: 0, 4: 1}
    return pl.pallas_call(
        _inproj_body,
        grid=(n_i, IN_WIDTH // IN_TN),
        in_specs=in_specs,
        out_specs=(
            pl.BlockSpec((IN_TM, IN_TN), lambda i, j: (i + row_block0, jnp.minimum(j, IN_U_STEPS - 1))),
            pl.BlockSpec((IN_TM, IN_TN), lambda i, j: (i + row_block0, jnp.maximum(j - IN_U_STEPS, 0))),
        ),
        out_shape=(
            jax.ShapeDtypeStruct((T_ALL, U_WIDTH), F32),
            jax.ShapeDtypeStruct((T_ALL, GATE_WIDTH), BF16),
        ),
        scratch_shapes=[pltpu.VMEM((IN_TM, D_MODEL), BF16)],
        input_output_aliases=aliases,
        compiler_params=_cparams(2),
        name="inproj",
    )(*args)


PP_TM = 512
HIST = 16


def _pool_project(pooled_g, g, w_ref, sc_ref, o_ref):
    y = jnp.dot(pooled_g.astype(BF16), w_ref[g], preferred_element_type=F32)
    lo, hi = g * POOL_OUT_CH, (g + 1) * POOL_OUT_CH
    o_ref[:, lo:hi] = (y * sc_ref[:, lo:hi]).astype(o_ref.dtype)


def _pool_prompt_body(u_ref, w_ref, sc_ref, o_ref, tail_ref, hist_ref):
    i = pl.program_id(1)
    @pl.when(i == 0)
    def _():
        hist_ref[...] = jnp.zeros_like(hist_ref)
    u = u_ref[...]
    ext = jnp.concatenate([hist_ref[...], u], axis=0)
    hist_ref[...] = u[PP_TM - HIST:, :]
    tail_ref[...] = u[PP_TM - HIST:, :]
    pos = i * PP_TM + lax.broadcasted_iota(I32, (PP_TM, 1), 0)
    for g, w in enumerate(POOL_WINDOWS):
        lo, hi = g * POOL_GROUP_CH, (g + 1) * POOL_GROUP_CH
        s = ext[:, lo:hi]
        d = 1
        while d < w:
            s = s + pltpu.roll(s, d, axis=0)
            d *= 2
        cnt = jnp.minimum(w, pos + 1).astype(F32)
        pooled = s[HIST:, :] / cnt - u[:, lo:hi]
        _pool_project(pooled, g, w_ref, sc_ref, o_ref)


def _pool_prompt(u, w_pool_bf, pool_scale):
    n_i = SEQ // PP_TM
    return pl.pallas_call(
        _pool_prompt_body,
        grid=(BATCH, n_i),
        in_specs=[
            pl.BlockSpec((PP_TM, POOL_WIDTH), lambda b, i: (b * n_i + i, 0)),
            pl.BlockSpec((POOL_GROUPS, POOL_GROUP_CH, POOL_OUT_CH), lambda b, i: (0, 0, 0)),
            pl.BlockSpec((1, D_MODEL), lambda b, i: (0, 0)),
        ],
        out_specs=(
            pl.BlockSpec((PP_TM, D_MODEL), lambda b, i: (b * n_i + i, 0)),
            pl.BlockSpec((None, HIST, POOL_WIDTH), lambda b, i: (b, 0, 0)),
        ),
        out_shape=(
            jax.ShapeDtypeStruct((T_ALL, D_MODEL), BF16),
            jax.ShapeDtypeStruct((BATCH, HIST, POOL_WIDTH), F32),
        ),
        scratch_shapes=[pltpu.VMEM((HIST, POOL_WIDTH), F32)],
        compiler_params=_cparams(2),
        name="pool_prompt",
    )(u, w_pool_bf, pool_scale)


def _pool_sample_body(u_ref, hist_ref, w_ref, sc_ref, _dst, o_ref):
    rows = [hist_ref[k] for k in range(POOL_BUF)]
    rows += [u_ref[DEC_BATCH * t:DEC_BATCH * (t + 1), :] for t in range(DEC_SEQ)]
    n = len(rows)
    for g, w in enumerate(POOL_WINDOWS):
        lo, hi = g * POOL_GROUP_CH, (g + 1) * POOL_GROUP_CH
        f = [r[:, lo:hi] for r in rows]
        cur = f
        d = 1
        while d < w:
            cur = [cur[k] + cur[k - d] if k - d >= 0 else cur[k] for k in range(n)]
            d *= 2
        pooled = jnp.concatenate(
            [cur[POOL_BUF + t] / float(w) - f[POOL_BUF + t] for t in range(DEC_SEQ)], axis=0)
        _pool_project(pooled, g, w_ref, sc_ref, o_ref)


def _pool_sample(u, hist_tm, w_pool_bf, pool_scale, y_pool):
    blk = T_PROMPT // T_SAMPLE
    return pl.pallas_call(
        _pool_sample_body,
        grid=(1,),
        in_specs=[
            pl.BlockSpec((T_SAMPLE, POOL_WIDTH), lambda i: (blk, 0)),
            pl.BlockSpec((POOL_BUF, DEC_BATCH, POOL_WIDTH), lambda i: (0, 0, 0)),
            pl.BlockSpec((POOL_GROUPS, POOL_GROUP_CH, POOL_OUT_CH), lambda i: (0, 0, 0)),
            pl.BlockSpec((1, D_MODEL), lambda i: (0, 0)),
            pl.BlockSpec(memory_space=pl.ANY),
        ],
        out_specs=pl.BlockSpec((T_SAMPLE, D_MODEL), lambda i: (blk, 0)),
        out_shape=jax.ShapeDtypeStruct((T_ALL, D_MODEL), BF16),
        input_output_aliases={4: 0},
        compiler_params=_cparams(1),
        name="pool_sample",
    )(u, hist_tm, w_pool_bf, pool_scale, y_pool)


def _ssm_tables(a_re, a_im, log_dt, b_re, b_im, c_re, c_im, d_skip):
    dt = jnp.exp(log_dt)[:, None]
    lr, li = a_re, a_im
    ab_re = jnp.exp(lr * dt) * jnp.cos(li * dt)
    ab_im = jnp.exp(lr * dt) * jnp.sin(li * dt)
    den = lr * lr + li * li
    nr, ni = ab_re - 1.0, ab_im
    q_re = (nr * lr + ni * li) / den
    q_im = (ni * lr - nr * li) / den
    bb_re = q_re[..., None] * b_re - q_im[..., None] * b_im
    bb_im = q_re[..., None] * b_im + q_im[..., None] * b_re

    def lam_rows(ks):
        k = jnp.asarray(ks, F32)[:, None, None]
        m = jnp.exp(k * lr * dt)
        re = (m * jnp.cos(k * li * dt)).reshape(len(ks), N_OCT, OCT_STATES)
        im = (m * jnp.sin(k * li * dt)).reshape(len(ks), N_OCT, OCT_STATES)
        return jnp.concatenate([re, im], axis=-1).transpose(1, 0, 2)

    def compact(re, im):
        v = jnp.concatenate([re, im], axis=-1)
        return v.reshape(N_OCT, OCT, 2 * SSM_STATE)

    bbc = compact(jnp.swapaxes(bb_re, 1, 2), jnp.swapaxes(bb_im, 1, 2))
    ccc = compact(c_re, c_im)
    pw = lam_rows(list(range(2 * SUBLANES)))
    r = jnp.arange(SUBLANES)[None, :, None]
    parts = [jnp.where(r >= dd, lam_rows([CHUNK * dd]), 0.0) for dd in (1, 2, 4)]
    parts.append(lam_rows([CHUNK * kk for kk in range(1, SUBLANES + 1)]))
    tab = jnp.concatenate(parts, axis=1)
    dsk = d_skip.reshape(N_OCT, 1, OCT)
    return bbc, ccc, pw, tab, dsk


def _split_bf16(x):
    hi = x.astype(BF16)
    return hi, (x - hi.astype(F32)).astype(BF16)


def _dot_nt(a, b):
    return lax.dot_general(a, b, (((1,), (1,)), ((), ())), preferred_element_type=F32)


def _build_weights(bbc_ref, ccc_ref, pw_ref, f_ref, gt_ref, m_ref):
    row_gi = lax.broadcasted_iota(I32, (OCT, 1), 0) >> 4
    col = lax.broadcasted_iota(I32, (1, SW), 1)
    col_gi = (col >> 6) & 7
    src = ((col >> 9) << 6) | (col & 63)
    k128 = lax.broadcasted_iota(I32, (2 * SSM_STATE, 1), 0)
    spread = jnp.where(k128 == src, 1.0, 0.0).astype(BF16)
    diag = row_gi == col_gi

    def expand(c_ref):
        hi, lo = _split_bf16(c_ref[...])
        d = (jnp.dot(hi, spread, preferred_element_type=F32)
             + jnp.dot(lo, spread, preferred_element_type=F32))
        d = jnp.where(diag, d, 0.0)
        return d[:, :OCT_STATES], d[:, OCT_STATES:]

    br, bi = expand(bbc_ref)
    cr, ci = expand(ccc_ref)
    chi_r, clo_r = _split_bf16(cr)
    chi_i, clo_i = _split_bf16(ci)

    def lam(k):
        return pw_ref[k:k + 1, :OCT_STATES], pw_ref[k:k + 1, OCT_STATES:]

    def dot3(a, bhi, blo):
        ahi, alo = _split_bf16(a)
        return _dot_nt(ahi, bhi) + _dot_nt(alo, bhi) + _dot_nt(ahi, blo)

    lags = []
    for k in range(CHUNK):
        pr, pi_ = lam(k)
        fr, fi = _cmul(br, bi, pr, pi_)
        s = CHUNK - 1 - k
        f_ref[s * OCT:(s + 1) * OCT, :] = jnp.concatenate([fr, fi], axis=1).astype(BF16)
        lags.append((dot3(fr, chi_r, clo_r) - dot3(fi, chi_i, clo_i)).astype(BF16))
        pr, pi_ = lam(k + 1)
        gr, gi = _cmul(cr, ci, pr, pi_)
        gt_ref[k * OCT:(k + 1) * OCT, :] = jnp.concatenate([gr, -gi], axis=1).astype(BF16)
    zero = jnp.zeros((OCT, OCT), BF16)
    for s in range(CHUNK):
        for t in range(CHUNK):
            m_ref[s * OCT:(s + 1) * OCT, t * OCT:(t + 1) * OCT] = lags[t - s] if t >= s else zero


def _cmul(ar, ai, br, bi):
    return ar * br - ai * bi, ar * bi + ai * br


def _chunk_scan(sloc, tab_ref):
    R = sloc.shape[0]
    nb = R // SUBLANES
    sr, si = sloc[:, :OCT_STATES], sloc[:, OCT_STATES:]
    rowi = lax.broadcasted_iota(I32, (R, 1), 0)
    tr = jnp.where(rowi == 0, 0.0, pltpu.roll(sr, 1, axis=0))
    ti = jnp.where(rowi == 0, 0.0, pltpu.roll(si, 1, axis=0))
    for lvl, d in enumerate((1, 2, 4)):
        mr = tab_ref[lvl * SUBLANES:(lvl + 1) * SUBLANES, :OCT_STATES]
        mi = tab_ref[lvl * SUBLANES:(lvl + 1) * SUBLANES, OCT_STATES:]
        mr = jnp.concatenate([mr] * nb, axis=0)
        mi = jnp.concatenate([mi] * nb, axis=0)
        pr, pi_ = _cmul(mr, mi, pltpu.roll(tr, d, axis=0), pltpu.roll(ti, d, axis=0))
        tr, ti = tr + pr, ti + pi_
    pwr = tab_ref[3 * SUBLANES:4 * SUBLANES, :OCT_STATES]
    pwi = tab_ref[3 * SUBLANES:4 * SUBLANES, OCT_STATES:]
    cr = jnp.zeros((1, OCT_STATES), F32)
    ci = jnp.zeros((1, OCT_STATES), F32)
    out_r, out_i = [], []
    for k in range(nb):
        ar = tr[k * SUBLANES:(k + 1) * SUBLANES, :]
        ai = ti[k * SUBLANES:(k + 1) * SUBLANES, :]
        pr, pi_ = _cmul(pwr, pwi, jnp.broadcast_to(cr, ar.shape), jnp.broadcast_to(ci, ai.shape))
        hr, hi = ar + pr, ai + pi_
        out_r.append(hr)
        out_i.append(hi)
        cr, ci = hr[SUBLANES - 1:, :], hi[SUBLANES - 1:, :]
    hin = jnp.concatenate([jnp.concatenate(out_r, axis=0), jnp.concatenate(out_i, axis=0)], axis=1)
    lr, li = pwr[0:1, :], pwi[0:1, :]
    fr, fi = _cmul(lr, li, cr, ci)
    fin = jnp.concatenate([fr + sr[R - 1:, :], fi + si[R - 1:, :]], axis=1)
    return hin, fin


def _ssm_prompt_body(u_ref, bbc_ref, ccc_ref, pw_ref, tab_ref, d_ref, y_ref, hout_ref,
                     f_ref, gt_ref, m_ref):
    @pl.when(pl.program_id(1) == 0)
    def _():
        _build_weights(bbc_ref, ccc_ref, pw_ref, f_ref, gt_ref, m_ref)
    R = SEQ // CHUNK
    xs = [u_ref[pl.ds(s, R, stride=CHUNK), :] for s in range(CHUNK)]
    xb = jnp.concatenate(xs, axis=1).astype(BF16)
    sloc = jnp.dot(xb, f_ref[...], preferred_element_type=F32)
    hin, fin = _chunk_scan(sloc, tab_ref)
    y = (jnp.dot(xb, m_ref[...], preferred_element_type=F32)
         + _dot_nt(hin.astype(BF16), gt_ref[...]))
    for t in range(CHUNK):
        yt = y[:, t * OCT:(t + 1) * OCT] + d_ref[...] * xs[t]
        y_ref[pl.ds(t, R, stride=CHUNK), :] = _gelu_tanh(yt)
    hout_ref[...] = fin


def _ssm_weight_specs(n_axes):
    if n_axes == 2:
        im3 = lambda o, b: (o, 0, 0)
    else:
        im3 = lambda o: (o, 0, 0)
    return [
        pl.BlockSpec((None, OCT, 2 * SSM_STATE), im3),
        pl.BlockSpec((None, OCT, 2 * SSM_STATE), im3),
        pl.BlockSpec((None, 2 * SUBLANES, SW), im3),
        pl.BlockSpec((None, 4 * SUBLANES, SW), im3),
        pl.BlockSpec((None, 1, OCT), im3),
    ]


_SSM_SCRATCH = [pltpu.VMEM((CW, SW), BF16), pltpu.VMEM((CW, SW), BF16), pltpu.VMEM((CW, CW), BF16)]


def _ssm_prompt(u, tables):
    col0 = POOL_WIDTH // OCT
    return pl.pallas_call(
        _ssm_prompt_body,
        grid=(N_OCT, BATCH),
        in_specs=[pl.BlockSpec((SEQ, OCT), lambda o, b: (b, col0 + o))] + _ssm_weight_specs(2),
        out_specs=(
            pl.BlockSpec((SEQ, OCT), lambda o, b: (b, o)),
            pl.BlockSpec((None, 1, SW), lambda o, b: (b * N_OCT + o, 0, 0)),
        ),
        out_shape=(
            jax.ShapeDtypeStruct((T_ALL, SSM_WIDTH), F32),
            jax.ShapeDtypeStruct((BATCH * N_OCT, 1, SW), F32),
        ),
        scratch_shapes=_SSM_SCRATCH,
        compiler_params=_cparams(2),
        name="ssm_prompt",
    )(u, *tables)


def _ssm_sample_body(u_ref, h0r_ref, h0i_ref, bbc_ref, ccc_ref, pw_ref, tab_ref, d_ref, _dst,
                     y_ref, hr_ref, hi_ref, f_ref, gt_ref, m_ref):
    _build_weights(bbc_ref, ccc_ref, pw_ref, f_ref, gt_ref, m_ref)
    B = DEC_BATCH
    xs = [u_ref[B * s:B * (s + 1), :] for s in range(CHUNK)]
    xb = jnp.concatenate(xs, axis=1).astype(BF16)
    sloc = jnp.dot(xb, f_ref[...], preferred_element_type=F32)
    h0r, h0i = h0r_ref[...], h0i_ref[...]
    hin = jnp.concatenate([h0r, h0i], axis=1).astype(BF16)
    y = (jnp.dot(xb, m_ref[...], preferred_element_type=F32)
         + _dot_nt(hin, gt_ref[...]))
    for t in range(CHUNK):
        yt = y[:, t * OCT:(t + 1) * OCT] + d_ref[...] * xs[t]
        y_ref[B * t:B * (t + 1), :] = _gelu_tanh(yt)
    lr = tab_ref[3 * SUBLANES:3 * SUBLANES + 1, :OCT_STATES]
    li = tab_ref[3 * SUBLANES:3 * SUBLANES + 1, OCT_STATES:]
    nr, ni = _cmul(lr, li, h0r, h0i)
    hr_ref[...] = nr + sloc[:, :OCT_STATES]
    hi_ref[...] = ni + sloc[:, OCT_STATES:]


def _ssm_sample(u, h0r, h0i, tables, y_act):
    col0 = POOL_WIDTH // OCT
    blk = T_PROMPT // T_SAMPLE
    st_spec = pl.BlockSpec((DEC_BATCH, OCT_STATES), lambda o: (0, o))
    return pl.pallas_call(
        _ssm_sample_body,
        grid=(N_OCT,),
        in_specs=[pl.BlockSpec((T_SAMPLE, OCT), lambda o: (blk, col0 + o)), st_spec, st_spec]
        + _ssm_weight_specs(1) + [pl.BlockSpec(memory_space=pl.ANY)],
        out_specs=(pl.BlockSpec((T_SAMPLE, OCT), lambda o: (blk, o)), st_spec, st_spec),
        out_shape=(
            jax.ShapeDtypeStruct((T_ALL, SSM_WIDTH), F32),
            jax.ShapeDtypeStruct((DEC_BATCH, SSM_GROUPS * SSM_STATE), F32),
            jax.ShapeDtypeStruct((DEC_BATCH, SSM_GROUPS * SSM_STATE), F32),
        ),
        scratch_shapes=_SSM_SCRATCH,
        input_output_aliases={8: 0},
        compiler_params=_cparams(1),
        name="ssm_sample",
    )(u, h0r, h0i, *tables, y_act)


PM_TM = 256
PM_PROMPT_BLOCKS = T_PROMPT // PM_TM


def _postmix_body(ya_ref, gp_ref, gs_ref, yp_ref, xp_ref, xs_ref, wa_ref, wb_ref, wo_ref,
                  gf_ref, wrh_ref, wrl_ref, br_ref, h_ref, tn_ref, rt_ref, cnt_out_ref, cnt_ref):
    i = pl.program_id(0)
    @pl.when(i == 0)
    def _():
        cnt_ref[...] = jnp.zeros_like(cnt_ref)
    ya = ya_ref[...].astype(BF16)
    a = jnp.dot(ya, wa_ref[...], preferred_element_type=F32)
    bg = jnp.dot(ya, wb_ref[...], preferred_element_type=F32)
    y_ssm = a * _sigmoid(bg)
    merged = (gp_ref[...].astype(F32) * yp_ref[...].astype(F32)
              + gs_ref[...].astype(F32) * y_ssm)
    x = jnp.where(i < PM_PROMPT_BLOCKS, xp_ref[...], xs_ref[...])
    h = x + jnp.dot(merged.astype(BF16), wo_ref[...], preferred_element_type=F32)
    h_ref[...] = h
    inv = lax.rsqrt(jnp.mean(h * h, axis=-1, keepdims=True) + EPS)
    tn = (h * inv) * gf_ref[...]
    tn_ref[...] = tn
    t_hi = tn.astype(BF16)
    t_lo = (tn - t_hi.astype(F32)).astype(BF16)
    wrh = wrh_ref[...]
    logits = (jnp.dot(t_hi, wrh, preferred_element_type=F32)
              + jnp.dot(t_lo, wrh, preferred_element_type=F32)
              + jnp.dot(t_hi, wrl_ref[...], preferred_element_type=F32)) + br_ref[...]
    lane = lax.broadcasted_iota(I32, (PM_TM, ROUTE_LANES), 1)
    neg = jnp.float32(-jnp.inf)
    big = jnp.int32(1 << 20)
    is_g = lane < MOE_GROUPS
    gmax = jnp.max(jnp.where(is_g, logits, neg), axis=1, keepdims=True)
    g_idx = jnp.min(jnp.where(is_g & (logits == gmax), lane, big), axis=1, keepdims=True)
    g_den = jnp.sum(jnp.where(is_g, jnp.exp(logits - gmax), 0.0), axis=1, keepdims=True)
    g_val = 1.0 / g_den
    e_lane = lane - EXP_LANE0
    sel = (e_lane >= 0) & (e_lane < MOE_EXPERTS) & ((e_lane >> 3) == g_idx)
    m1 = jnp.max(jnp.where(sel, logits, neg), axis=1, keepdims=True)
    i1 = jnp.min(jnp.where(sel & (logits == m1), lane, big), axis=1, keepdims=True)
    sel2 = sel & (lane != i1)
    m2 = jnp.max(jnp.where(sel2, logits, neg), axis=1, keepdims=True)
    i2 = jnp.min(jnp.where(sel2 & (logits == m2), lane, big), axis=1, keepdims=True)
    e2 = jnp.exp(m2 - m1)
    w1 = g_val / (1.0 + e2)
    w2 = g_val * e2 / (1.0 + e2)
    oh1 = lane == i1
    oh2 = lane == i2
    oh = jnp.where(oh1 | oh2, 1.0, 0.0)
    rr = lax.broadcasted_iota(I32, (PM_TM, PM_TM), 0)
    cc = lax.broadcasted_iota(I32, (PM_TM, PM_TM), 1)
    tri = jnp.where(cc < rr, 1.0, 0.0).astype(BF16)
    base = cnt_ref[...] + jnp.dot(tri, oh.astype(BF16), preferred_element_type=F32)
    rank1 = jnp.sum(jnp.where(oh1, base, 0.0), axis=1, keepdims=True)
    rank2 = jnp.sum(jnp.where(oh2, base, 0.0), axis=1, keepdims=True)
    cnt_ref[...] = cnt_ref[...] + jnp.sum(oh, axis=0, keepdims=True)
    cnt_out_ref[...] = cnt_ref[...]
    rt = jnp.where(lane == 0, w1, 0.0)
    rt = jnp.where(lane == 1, w2, rt)
    rt = jnp.where(lane == 2, rank1, rt)
    rt = jnp.where(lane == 3, rank2, rt)
    rt = jnp.where(lane == 4, (i1 - EXP_LANE0).astype(F32), rt)
    rt = jnp.where(lane == 5, (i2 - EXP_LANE0).astype(F32), rt)
    rt_ref[...] = rt


def _postmix(y_act, gates, y_pool, xp, xs, wa, wb, wo, g_ffn, wr_hi, wr_lo, b_r):
    n = T_ALL // PM_TM
    npb = PM_PROMPT_BLOCKS
    const2 = lambda i: (0, 0)
    return pl.pallas_call(
        _postmix_body,
        grid=(n,),
        in_specs=[
            pl.BlockSpec((PM_TM, SSM_WIDTH), lambda i: (i, 0)),
            pl.BlockSpec((PM_TM, D_MODEL), lambda i: (i, 0)),
            pl.BlockSpec((PM_TM, D_MODEL), lambda i: (i, 1)),
            pl.BlockSpec((PM_TM, D_MODEL), lambda i: (i, 0)),
            pl.BlockSpec((PM_TM, D_MODEL), lambda i: (jnp.minimum(i, npb - 1), 0)),
            pl.BlockSpec((PM_TM, D_MODEL), lambda i: (jnp.maximum(i - npb, 0), 0)),
            pl.BlockSpec((SSM_WIDTH, D_MODEL), const2, pipeline_mode=pl.Buffered(1)),
            pl.BlockSpec((SSM_WIDTH, D_MODEL), const2, pipeline_mode=pl.Buffered(1)),
            pl.BlockSpec((D_MODEL, D_MODEL), const2, pipeline_mode=pl.Buffered(1)),
            pl.BlockSpec((1, D_MODEL), const2),
            pl.BlockSpec((D_MODEL, ROUTE_LANES), const2),
            pl.BlockSpec((D_MODEL, ROUTE_LANES), const2),
            pl.BlockSpec((1, ROUTE_LANES), const2),
        ],
        out_specs=(
            pl.BlockSpec((PM_TM, D_MODEL), lambda i: (i, 0)),
            pl.BlockSpec((PM_TM, D_MODEL), lambda i: (i, 0)),
            pl.BlockSpec((PM_TM, ROUTE_LANES), lambda i: (i, 0)),
            pl.BlockSpec((1, ROUTE_LANES), const2),
        ),
        out_shape=(
            jax.ShapeDtypeStruct((T_ALL, D_MODEL), F32),
            jax.ShapeDtypeStruct((T_ALL, D_MODEL), F32),
            jax.ShapeDtypeStruct((T_ALL, ROUTE_LANES), F32),
            jax.ShapeDtypeStruct((1, ROUTE_LANES), F32),
        ),
        scratch_shapes=[pltpu.VMEM((1, ROUTE_LANES), F32)],
        compiler_params=_cparams(1),
        name="postmix",
    )(y_act, gates, gates, y_pool, xp, xs, wa, wb, wo, g_ffn, wr_hi, wr_lo, b_r)


SC_CH = 16


def _sc_workers():
    info = plsc.get_sparse_core_info()
    return info.num_cores, info.num_cores * info.num_subcores


def _sc_dispatch(tn, slots):
    n_cores, n_workers = _sc_workers()
    per_w = (T_ALL // SC_CH) // n_workers
    assert per_w * n_workers * SC_CH == T_ALL
    slots = slots.reshape(2, n_workers, per_w, SC_CH)

    @functools.partial(
        pl.kernel,
        mesh=plsc.VectorSubcoreMesh(core_axis_name="c", subcore_axis_name="s"),
        out_type=jax.ShapeDtypeStruct((N_SLOTS, D_MODEL), F32),
        scratch_types=[pltpu.VMEM((2, per_w, SC_CH), I32), pltpu.VMEM((SC_CH, D_MODEL), F32)],
    )
    def k(tn_hbm, slots_hbm, xs_hbm, idx_v, rows_v):
        wid = lax.axis_index("s") * n_cores + lax.axis_index("c")
        c0 = wid * per_w
        pltpu.sync_copy(slots_hbm.at[0, wid], idx_v.at[0])
        pltpu.sync_copy(slots_hbm.at[1, wid], idx_v.at[1])

        @pl.loop(0, per_w)
        def _(c):
            row0 = pl.multiple_of((c0 + c) * SC_CH, SC_CH)
            pltpu.sync_copy(tn_hbm.at[pl.ds(row0, SC_CH)], rows_v)
            pltpu.sync_copy(rows_v, xs_hbm.at[idx_v.at[0, c]])
            pltpu.sync_copy(rows_v, xs_hbm.at[idx_v.at[1, c]])

    return k(tn, slots)


def _sc_collect(ys, slots):
    n_cores, n_workers = _sc_workers()
    per_w = (N_ASSIGN // SC_CH) // n_workers
    assert per_w * n_workers * SC_CH == N_ASSIGN
    slots = slots.reshape(n_workers, per_w, SC_CH)

    @functools.partial(
        pl.kernel,
        mesh=plsc.VectorSubcoreMesh(core_axis_name="c", subcore_axis_name="s"),
        out_type=jax.ShapeDtypeStruct((N_ASSIGN, D_MODEL), F32),
        scratch_types=[pltpu.VMEM((per_w, SC_CH), I32), pltpu.VMEM((SC_CH, D_MODEL), F32)],
    )
    def k(ys_hbm, slots_hbm, out_hbm, idx_v, rows_v):
        wid = lax.axis_index("s") * n_cores + lax.axis_index("c")
        c0 = wid * per_w
        pltpu.sync_copy(slots_hbm.at[wid], idx_v)

        @pl.loop(0, per_w)
        def _(c):
            row0 = pl.multiple_of((c0 + c) * SC_CH, SC_CH)
            pltpu.sync_copy(ys_hbm.at[idx_v.at[c]], rows_v)
            pltpu.sync_copy(rows_v, out_hbm.at[pl.ds(row0, SC_CH)])

    return k(ys, slots)


def _expert_body(t0_ref, nt_ref, xs_hbm, wg_ref, wu_ref, wd_ref, ys_hbm,
                 xb_ref, yb_ref, wgb_ref, wub_ref, wdb_ref, xsem, ysem):
    e = pl.program_id(0)
    n = nt_ref[e]
    g0 = t0_ref[e]

    def rows(j):
        return pl.ds(pl.multiple_of((g0 + j) * TME, TME), TME)

    def x_copy(j, s):
        return pltpu.make_async_copy(xs_hbm.at[rows(j)], xb_ref.at[s], xsem.at[s])

    def y_copy(j, s):
        return pltpu.make_async_copy(yb_ref.at[s], ys_hbm.at[rows(j)], ysem.at[s])

    @pl.when(n > 0)
    def _():
        x_copy(0, 0).start()
        wgb_ref[...] = wg_ref[...].astype(BF16)
        wub_ref[...] = wu_ref[...].astype(BF16)
        wdb_ref[...] = wd_ref[...].astype(BF16)

        def tile(j, c):
            s = j & 1
            x_copy(j, s).wait()
            @pl.when(j + 1 < n)
            def _():
                x_copy(j + 1, 1 - s).start()
            @pl.when(j >= 2)
            def _():
                y_copy(j - 2, s).wait()
            x = xb_ref[s].astype(BF16)
            hg = jnp.dot(x, wgb_ref[...], preferred_element_type=F32)
            hu = jnp.dot(x, wub_ref[...], preferred_element_type=F32)
            act = (hg * _sigmoid(hg)) * hu
            yb_ref[s] = jnp.dot(act.astype(BF16), wdb_ref[...], preferred_element_type=F32)
            y_copy(j, s).start()
            return c
        lax.fori_loop(0, n, tile, 0)

        @pl.when(n >= 2)
        def _():
            y_copy(n - 2, n & 1).wait()
        y_copy(n - 1, (n - 1) & 1).wait()


def _experts(tile0, tiles, xs, w_eg, w_eu, w_ed):
    wmap = lambda e, t0, nt: (e, 0, 0)
    grid_spec = pltpu.PrefetchScalarGridSpec(
        num_scalar_prefetch=2,
        grid=(MOE_EXPERTS,),
        in_specs=[
            pl.BlockSpec(memory_space=pl.ANY),
            pl.BlockSpec((None, D_MODEL, MOE_FF), wmap),
            pl.BlockSpec((None, D_MODEL, MOE_FF), wmap),
            pl.BlockSpec((None, MOE_FF, D_MODEL), wmap),
        ],
        out_specs=pl.BlockSpec(memory_space=pl.ANY),
        scratch_shapes=[
            pltpu.VMEM((2, TME, D_MODEL), F32),
            pltpu.VMEM((2, TME, D_MODEL), F32),
            pltpu.VMEM((D_MODEL, MOE_FF), BF16),
            pltpu.VMEM((D_MODEL, MOE_FF), BF16),
            pltpu.VMEM((MOE_FF, D_MODEL), BF16),
            pltpu.SemaphoreType.DMA((2,)),
            pltpu.SemaphoreType.DMA((2,)),
        ],
    )
    return pl.pallas_call(
        _expert_body,
        grid_spec=grid_spec,
        out_shape=jax.ShapeDtypeStruct((N_SLOTS, D_MODEL), F32),
        compiler_params=_cparams(1),
        name="experts",
    )(tile0, tiles, xs, w_eg, w_eu, w_ed)


FN_TM = 256
FN_PROMPT_BLOCKS = T_PROMPT // FN_TM


def _final_body(h_ref, y0_ref, y1_ref, rt_ref, g_ref, op_ref, os_ref):
    i = pl.program_id(0)
    rt = rt_ref[...]
    z = h_ref[...] + rt[:, 0:1] * y0_ref[...] + rt[:, 1:2] * y1_ref[...]
    inv = lax.rsqrt(jnp.mean(z * z, axis=-1, keepdims=True) + EPS)
    out = (z * inv) * g_ref[...]
    @pl.when(i < FN_PROMPT_BLOCKS)
    def _():
        op_ref[...] = out
    @pl.when(i >= FN_PROMPT_BLOCKS)
    def _():
        os_ref[...] = out


def _final(h, y, route, g_final):
    n = T_ALL // FN_TM
    npb = FN_PROMPT_BLOCKS
    yoff = T_ALL // FN_TM
    return pl.pallas_call(
        _final_body,
        grid=(n,),
        in_specs=[
            pl.BlockSpec((FN_TM, D_MODEL), lambda i: (i, 0)),
            pl.BlockSpec((FN_TM, D_MODEL), lambda i: (i, 0)),
            pl.BlockSpec((FN_TM, D_MODEL), lambda i: (yoff + i, 0)),
            pl.BlockSpec((FN_TM, ROUTE_LANES), lambda i: (i, 0)),
            pl.BlockSpec((1, D_MODEL), lambda i: (0, 0)),
        ],
        out_specs=(
            pl.BlockSpec((FN_TM, D_MODEL), lambda i: (jnp.minimum(i, npb - 1), 0)),
            pl.BlockSpec((FN_TM, D_MODEL), lambda i: (jnp.maximum(i - npb, 0), 0)),
        ),
        out_shape=(
            jax.ShapeDtypeStruct((T_PROMPT, D_MODEL), F32),
            jax.ShapeDtypeStruct((T_SAMPLE, D_MODEL), F32),
        ),
        compiler_params=_cparams(1),
        name="final",
    )(h, y, y, route, g_final)


def _dispatch_plan(route, cnt):
    counts = cnt[0, EXP_LANE0:EXP_LANE0 + MOE_EXPERTS].astype(I32)
    tiles = (counts + (TME - 1)) // TME
    cumt = jnp.cumsum(tiles)
    pad_off = (cumt - tiles) * TME
    rank = route[:, 2:4].astype(I32)
    eid = route[:, 4:6].astype(I32)
    onehot = eid[..., None] == jnp.arange(MOE_EXPERTS, dtype=I32)
    slots = (rank + jnp.sum(jnp.where(onehot, pad_off, 0), axis=-1)).T
    return slots, cumt - tiles, tiles


def kernel(x_prompt, x_sample, state_pool, state_ssm_re, state_ssm_im, g_mix, w_in, w_pool,
           pool_scale, ssm_a_re, ssm_a_im, ssm_log_dt, ssm_b_re, ssm_b_im, ssm_c_re, ssm_c_im,
           ssm_d, w_glu_a, w_glu_b, w_out, g_ffn, w_router_group, b_router_group,
           w_router_expert, b_router_expert, w_exp_gate, w_exp_up, w_exp_down, g_final):
    l = 0
    xp = x_prompt.reshape(T_PROMPT, D_MODEL)
    xs = x_sample.transpose(1, 0, 2).reshape(T_SAMPLE, D_MODEL)
    w_in_bf = w_in[l].astype(BF16)
    w_pool_bf = w_pool[l].astype(BF16)
    wa_bf = w_glu_a[l].astype(BF16)
    wb_bf = w_glu_b[l].astype(BF16)
    wo_bf = w_out[l].astype(BF16)
    g_mix2 = g_mix[l].reshape(1, D_MODEL)
    scale2 = pool_scale[l].reshape(1, D_MODEL)

    u, gates = _inproj(xp, g_mix2, w_in_bf, 0)
    u, gates = _inproj(xs, g_mix2, w_in_bf, T_PROMPT // IN_TM, dst=(u, gates))

    y_pool, pool_tail = _pool_prompt(u, w_pool_bf, scale2)
    hist_tm = state_pool[l].transpose(1, 0, 2)
    y_pool = _pool_sample(u, hist_tm, w_pool_bf, scale2, y_pool)

    tables = _ssm_tables(ssm_a_re[l], ssm_a_im[l], ssm_log_dt[l], ssm_b_re[l], ssm_b_im[l],
                         ssm_c_re[l], ssm_c_im[l], ssm_d[l])
    y_act, h_prompt = _ssm_prompt(u, tables)
    h0r = state_ssm_re[l].reshape(DEC_BATCH, SSM_GROUPS * SSM_STATE)
    h0i = state_ssm_im[l].reshape(DEC_BATCH, SSM_GROUPS * SSM_STATE)
    y_act, hs_re, hs_im = _ssm_sample(u, h0r, h0i, tables, y_act)

    w_r = jnp.zeros((D_MODEL, ROUTE_LANES), F32)
    w_r = w_r.at[:, :MOE_GROUPS].set(w_router_group[l])
    w_r = w_r.at[:, EXP_LANE0:EXP_LANE0 + MOE_EXPERTS].set(w_router_expert[l])
    wr_hi = w_r.astype(BF16)
    wr_lo = (w_r - wr_hi.astype(F32)).astype(BF16)
    b_r = jnp.zeros((1, ROUTE_LANES), F32)
    b_r = b_r.at[0, :MOE_GROUPS].set(b_router_group[l])
    b_r = b_r.at[0, EXP_LANE0:EXP_LANE0 + MOE_EXPERTS].set(b_router_expert[l])

    h, tn, route, cnt = _postmix(y_act, gates, y_pool, xp, xs, wa_bf, wb_bf, wo_bf,
                                 g_ffn[l].reshape(1, D_MODEL), wr_hi, wr_lo, b_r)
    slots, tile0, tiles = _dispatch_plan(route, cnt)
    xs_sorted = _sc_dispatch(tn, slots)
    ys_sorted = _experts(tile0, tiles, xs_sorted, w_exp_gate[l], w_exp_up[l], w_exp_down[l])
    y = _sc_collect(ys_sorted, slots)
    yp, ys = _final(h, y, route, g_final.reshape(1, D_MODEL))

    y_prompt = yp.reshape(BATCH, SEQ, D_MODEL)
    y_sample = ys.reshape(DEC_SEQ, DEC_BATCH, D_MODEL).transpose(1, 0, 2)
    new_pool_prompt = pool_tail[:, HIST - POOL_BUF:, :][None]
    us = u[T_PROMPT:, :POOL_WIDTH].reshape(DEC_SEQ, DEC_BATCH, POOL_WIDTH).transpose(1, 0, 2)
    new_pool_sample = jnp.concatenate([state_pool[l][:, DEC_SEQ:, :], us], axis=1)[None]
    hp = h_prompt.reshape(BATCH, N_OCT, 2, OCT_GROUPS, SSM_STATE).transpose(2, 0, 1, 3, 4)
    hp = hp.reshape(2, BATCH, SSM_GROUPS, SSM_STATE)
    shp = (1, DEC_BATCH, SSM_GROUPS, SSM_STATE)
    return (y_prompt, y_sample, new_pool_prompt, hp[0][None], hp[1][None], new_pool_sample,
            hs_re.reshape(shp), hs_im.reshape(shp))
```

```python
import functools
import math

import jax
import jax.numpy as jnp
from jax import lax
from jax.experimental import pallas as pl
from jax.experimental.pallas import tpu as pltpu
from jax.experimental.pallas import tpu_sc as plsc

F32 = jnp.float32
BF16 = jnp.bfloat16
I32 = jnp.int32

D_MODEL = 2048
BATCH = 4
SEQ = 2048
DEC_BATCH = 128
DEC_SEQ = 8
PAST_LEN = 16384
POOL_WIDTH = D_MODEL // 2
POOL_WINDOWS = (2, 4, 8, 16)
POOL_GROUPS = len(POOL_WINDOWS)
POOL_GROUP_CH = POOL_WIDTH // POOL_GROUPS
POOL_OUT_CH = D_MODEL // POOL_GROUPS
POOL_BUF = max(POOL_WINDOWS) - 1
SSM_WIDTH = D_MODEL // 2
SSM_GROUP_CH = 16
SSM_GROUPS = SSM_WIDTH // SSM_GROUP_CH
SSM_STATE = 64
IN_WIDTH = POOL_WIDTH + SSM_WIDTH + 2 * D_MODEL
MOE_GROUPS = 4
MOE_EPG = 8
MOE_EXPERTS = MOE_GROUPS * MOE_EPG
MOE_FF = D_MODEL // 4
EPS = 1e-6

T_PROMPT = BATCH * SEQ
T_SAMPLE = DEC_BATCH * DEC_SEQ
T_ALL = T_PROMPT + T_SAMPLE

LANES = 128
SUBLANES = 8
VMEM_LIMIT = 56 * 1024 * 1024

CHUNK = 8
OCT = LANES
N_OCT = SSM_WIDTH // OCT
OCT_GROUPS = OCT // SSM_GROUP_CH
OCT_STATES = OCT_GROUPS * SSM_STATE
CW = CHUNK * OCT
SW = 2 * OCT_STATES

ROUTE_LANES = LANES
EXP_LANE0 = MOE_GROUPS
N_ASSIGN = 2 * T_ALL
TME = 512
N_ITEMS_MAX = N_ASSIGN // TME + MOE_EXPERTS
N_SLOTS = N_ITEMS_MAX * TME


def _cparams(n_axes):
    return pltpu.CompilerParams(dimension_semantics=("arbitrary",) * n_axes,
                                vmem_limit_bytes=VMEM_LIMIT)


def _sigmoid(x):
    return 1.0 / (1.0 + jnp.exp(-x))


def _gelu_tanh(x):
    c = math.sqrt(2.0 / math.pi)
    return 0.5 * x * (1.0 + jnp.tanh(c * (x + 0.044715 * (x * x * x))))


IN_TM = 1024
IN_TN = 1024
U_WIDTH = POOL_WIDTH + SSM_WIDTH
GATE_WIDTH = 2 * D_MODEL
IN_U_STEPS = U_WIDTH // IN_TN


def _inproj_body(x_ref, g_ref, w_ref, *rest):
    u_ref, gate_ref, xn_ref = rest[-3:]
    j = pl.program_id(1)
    @pl.when(j == 0)
    def _():
        x = x_ref[...]
        inv = lax.rsqrt(jnp.mean(x * x, axis=-1, keepdims=True) + EPS)
        xn_ref[...] = ((x * inv) * g_ref[...]).astype(BF16)
    acc = jnp.dot(xn_ref[...], w_ref[...], preferred_element_type=F32)
    @pl.when(j < IN_U_STEPS)
    def _():
        u_ref[...] = acc
    @pl.when(j >= IN_U_STEPS)
    def _():
        gate_ref[...] = _sigmoid(acc).astype(BF16)


def _inproj(x, g, w_bf, row_block0, dst=None):
    n_i = x.shape[0] // IN_TM
    in_specs = [
        pl.BlockSpec((IN_TM, D_MODEL), lambda i, j: (i, 0)),
        pl.BlockSpec((1, D_MODEL), lambda i, j: (0, 0)),
        pl.BlockSpec((D_MODEL, IN_TN), lambda i, j: (0, j)),
    ]
    args = [x, g, w_bf]
    aliases = {}
    if dst is not None:
        in_specs += [pl.BlockSpec(memory_space=pl.ANY)] * 2
        args += list(dst)
        aliases = {3: 0, 4: 1}
    return pl.pallas_call(
        _inproj_body,
        grid=(n_i, IN_WIDTH // IN_TN),
        in_specs=in_specs,
        out_specs=(
            pl.BlockSpec((IN_TM, IN_TN), lambda i, j: (i + row_block0, jnp.minimum(j, IN_U_STEPS - 1))),
            pl.BlockSpec((IN_TM, IN_TN), lambda i, j: (i + row_block0, jnp.maximum(j - IN_U_STEPS, 0))),
        ),
        out_shape=(
            jax.ShapeDtypeStruct((T_ALL, U_WIDTH), F32),
            jax.ShapeDtypeStruct((T_ALL, GATE_WIDTH), BF16),
        ),
        scratch_shapes=[pltpu.VMEM((IN_TM, D_MODEL), BF16)],
        input_output_aliases=aliases,
        compiler_params=_cparams(2),
        name="inproj",
    )(*args)


PP_TM = 512
HIST = 16


def _pool_project(pooled_g, g, w_ref, sc_ref, o_ref):
    y = jnp.dot(pooled_g.astype(BF16), w_ref[g], preferred_element_type=F32)
    lo, hi = g * POOL_OUT_CH, (g + 1) * POOL_OUT_CH
    o_ref[:, lo:hi] = (y * sc_ref[:, lo:hi]).astype(o_ref.dtype)


def _pool_prompt_body(u_ref, w_ref, sc_ref, o_ref, tail_ref, hist_ref):
    i = pl.program_id(1)
    @pl.when(i == 0)
    def _():
        hist_ref[...] = jnp.zeros_like(hist_ref)
    u = u_ref[...]
    ext = jnp.concatenate([hist_ref[...], u], axis=0)
    hist_ref[...] = u[PP_TM - HIST:, :]
    tail_ref[...] = u[PP_TM - HIST:, :]
    pos = i * PP_TM + lax.broadcasted_iota(I32, (PP_TM, 1), 0)
    for g, w in enumerate(POOL_WINDOWS):
        lo, hi = g * POOL_GROUP_CH, (g + 1) * POOL_GROUP_CH
        s = ext[:, lo:hi]
        d = 1
        while d < w:
            s = s + pltpu.roll(s, d, axis=0)
            d *= 2
        cnt = jnp.minimum(w, pos + 1).astype(F32)
        pooled = s[HIST:, :] / cnt - u[:, lo:hi]
        _pool_project(pooled, g, w_ref, sc_ref, o_ref)


def _pool_prompt(u, w_pool_bf, pool_scale):
    n_i = SEQ // PP_TM
    return pl.pallas_call(
        _pool_prompt_body,
        grid=(BATCH, n_i),
        in_specs=[
            pl.BlockSpec((PP_TM, POOL_WIDTH), lambda b, i: (b * n_i + i, 0)),
            pl.BlockSpec((POOL_GROUPS, POOL_GROUP_CH, POOL_OUT_CH), lambda b, i: (0, 0, 0)),
            pl.BlockSpec((1, D_MODEL), lambda b, i: (0, 0)),
        ],
        out_specs=(
            pl.BlockSpec((PP_TM, D_MODEL), lambda b, i: (b * n_i + i, 0)),
            pl.BlockSpec((None, HIST, POOL_WIDTH), lambda b, i: (b, 0, 0)),
        ),
        out_shape=(
            jax.ShapeDtypeStruct((T_ALL, D_MODEL), BF16),
            jax.ShapeDtypeStruct((BATCH, HIST, POOL_WIDTH), F32),
        ),
        scratch_shapes=[pltpu.VMEM((HIST, POOL_WIDTH), F32)],
        compiler_params=_cparams(2),
        name="pool_prompt",
    )(u, w_pool_bf, pool_scale)


def _pool_sample_body(u_ref, hist_ref, w_ref, sc_ref, _dst, o_ref):
    rows = [hist_ref[k] for k in range(POOL_BUF)]
    rows += [u_ref[DEC_BATCH * t:DEC_BATCH * (t + 1), :] for t in range(DEC_SEQ)]
    n = len(rows)
    for g, w in enumerate(POOL_WINDOWS):
        lo, hi = g * POOL_GROUP_CH, (g + 1) * POOL_GROUP_CH
        f = [r[:, lo:hi] for r in rows]
        cur = f
        d = 1
        while d < w:
            cur = [cur[k] + cur[k - d] if k - d >= 0 else cur[k] for k in range(n)]
            d *= 2
        pooled = jnp.concatenate(
            [cur[POOL_BUF + t] / float(w) - f[POOL_BUF + t] for t in range(DEC_SEQ)], axis=0)
        _pool_project(pooled, g, w_ref, sc_ref, o_ref)


def _pool_sample(u, hist_tm, w_pool_bf, pool_scale, y_pool):
    blk = T_PROMPT // T_SAMPLE
    return pl.pallas_call(
        _pool_sample_body,
        grid=(1,),
        in_specs=[
            pl.BlockSpec((T_SAMPLE, POOL_WIDTH), lambda i: (blk, 0)),
            pl.BlockSpec((POOL_BUF, DEC_BATCH, POOL_WIDTH), lambda i: (0, 0, 0)),
            pl.BlockSpec((POOL_GROUPS, POOL_GROUP_CH, POOL_OUT_CH), lambda i: (0, 0, 0)),
            pl.BlockSpec((1, D_MODEL), lambda i: (0, 0)),
            pl.BlockSpec(memory_space=pl.ANY),
        ],
        out_specs=pl.BlockSpec((T_SAMPLE, D_MODEL), lambda i: (blk, 0)),
        out_shape=jax.ShapeDtypeStruct((T_ALL, D_MODEL), BF16),
        input_output_aliases={4: 0},
        compiler_params=_cparams(1),
        name="pool_sample",
    )(u, hist_tm, w_pool_bf, pool_scale, y_pool)


def _ssm_tables(a_re, a_im, log_dt, b_re, b_im, c_re, c_im, d_skip):
    dt = jnp.exp(log_dt)[:, None]
    lr, li = a_re, a_im
    ab_re = jnp.exp(lr * dt) * jnp.cos(li * dt)
    ab_im = jnp.exp(lr * dt) * jnp.sin(li * dt)
    den = lr * lr + li * li
    nr, ni = ab_re - 1.0, ab_im
    q_re = (nr * lr + ni * li) / den
    q_im = (ni * lr - nr * li) / den
    bb_re = q_re[..., None] * b_re - q_im[..., None] * b_im
    bb_im = q_re[..., None] * b_im + q_im[..., None] * b_re

    def lam_rows(ks):
        k = jnp.asarray(ks, F32)[:, None, None]
        m = jnp.exp(k * lr * dt)
        re = (m * jnp.cos(k * li * dt)).reshape(len(ks), N_OCT, OCT_STATES)
        im = (m * jnp.sin(k * li * dt)).reshape(len(ks), N_OCT, OCT_STATES)
        return jnp.concatenate([re, im], axis=-1).transpose(1, 0, 2)

    def compact(re, im):
        v = jnp.concatenate([re, im], axis=-1)
        return v.reshape(N_OCT, OCT, 2 * SSM_STATE)

    bbc = compact(jnp.swapaxes(bb_re, 1, 2), jnp.swapaxes(bb_im, 1, 2))
    ccc = compact(c_re, c_im)
    pw = lam_rows(list(range(2 * SUBLANES)))
    r = jnp.arange(SUBLANES)[None, :, None]
    parts = [jnp.where(r >= dd, lam_rows([CHUNK * dd]), 0.0) for dd in (1, 2, 4)]
    parts.append(lam_rows([CHUNK * kk for kk in range(1, SUBLANES + 1)]))
    tab = jnp.concatenate(parts, axis=1)
    dsk = d_skip.reshape(N_OCT, 1, OCT)
    return bbc, ccc, pw, tab, dsk


def _split_bf16(x):
    hi = x.astype(BF16)
    return hi, (x - hi.astype(F32)).astype(BF16)


def _dot_nt(a, b):
    return lax.dot_general(a, b, (((1,), (1,)), ((), ())), preferred_element_type=F32)


def _build_weights(bbc_ref, ccc_ref, pw_ref, f_ref, gt_ref, m_ref):
    row_gi = lax.broadcasted_iota(I32, (OCT, 1), 0) >> 4
    col = lax.broadcasted_iota(I32, (1, SW), 1)
    col_gi = (col >> 6) & 7
    src = ((col >> 9) << 6) | (col & 63)
    k128 = lax.broadcasted_iota(I32, (2 * SSM_STATE, 1), 0)
    spread = jnp.where(k128 == src, 1.0, 0.0).astype(BF16)
    diag = row_gi == col_gi

    def expand(c_ref):
        hi, lo = _split_bf16(c_ref[...])
        d = (jnp.dot(hi, spread, preferred_element_type=F32)
             + jnp.dot(lo, spread, preferred_element_type=F32))
        d = jnp.where(diag, d, 0.0)
        return d[:, :OCT_STATES], d[:, OCT_STATES:]

    br, bi = expand(bbc_ref)
    cr, ci = expand(ccc_ref)
    chi_r, clo_r = _split_bf16(cr)
    chi_i, clo_i = _split_bf16(ci)

    def lam(k):
        return pw_ref[k:k + 1, :OCT_STATES], pw_ref[k:k + 1, OCT_STATES:]

    def dot3(a, bhi, blo):
        ahi, alo = _split_bf16(a)
        return _dot_nt(ahi, bhi) + _dot_nt(alo, bhi) + _dot_nt(ahi, blo)

    lags = []
    for k in range(CHUNK):
        pr, pi_ = lam(k)
        fr, fi = _cmul(br, bi, pr, pi_)
        s = CHUNK - 1 - k
        f_ref[s * OCT:(s + 1) * OCT, :] = jnp.concatenate([fr, fi], axis=1).astype(BF16)
        lags.append((dot3(fr, chi_r, clo_r) - dot3(fi, chi_i, clo_i)).astype(BF16))
        pr, pi_ = lam(k + 1)
        gr, gi = _cmul(cr, ci, pr, pi_)
        gt_ref[k * OCT:(k + 1) * OCT, :] = jnp.concatenate([gr, -gi], axis=1).astype(BF16)
    zero = jnp.zeros((OCT, OCT), BF16)
    for s in range(CHUNK):
        for t in range(CHUNK):
            m_ref[s * OCT:(s + 1) * OCT, t * OCT:(t + 1) * OCT] = lags[t - s] if t >= s else zero


def _cmul(ar, ai, br, bi):
    return ar * br - ai * bi, ar * bi + ai * br


def _chunk_scan(sloc, tab_ref):
    R = sloc.shape[0]
    nb = R // SUBLANES
    sr, si = sloc[:, :OCT_STATES], sloc[:, OCT_STATES:]
    rowi = lax.broadcasted_iota(I32, (R, 1), 0)
    tr = jnp.where(rowi == 0, 0.0, pltpu.roll(sr, 1, axis=0))
    ti = jnp.where(rowi == 0, 0.0, pltpu.roll(si, 1, axis=0))
    for lvl, d in enumerate((1, 2, 4)):
        mr = tab_ref[lvl * SUBLANES:(lvl + 1) * SUBLANES, :OCT_STATES]
        mi = tab_ref[lvl * SUBLANES:(lvl + 1) * SUBLANES, OCT_STATES:]
        mr = jnp.concatenate([mr] * nb, axis=0)
        mi = jnp.concatenate([mi] * nb, axis=0)
        pr, pi_ = _cmul(mr, mi, pltpu.roll(tr, d, axis=0), pltpu.roll(ti, d, axis=0))
        tr, ti = tr + pr, ti + pi_
    pwr = tab_ref[3 * SUBLANES:4 * SUBLANES, :OCT_STATES]
    pwi = tab_ref[3 * SUBLANES:4 * SUBLANES, OCT_STATES:]
    cr = jnp.zeros((1, OCT_STATES), F32)
    ci = jnp.zeros((1, OCT_STATES), F32)
    out_r, out_i = [], []
    for k in range(nb):
        ar = tr[k * SUBLANES:(k + 1) * SUBLANES, :]
        ai = ti[k * SUBLANES:(k + 1) * SUBLANES, :]
        pr, pi_ = _cmul(pwr, pwi, jnp.broadcast_to(cr, ar.shape), jnp.broadcast_to(ci, ai.shape))
        hr, hi = ar + pr, ai + pi_
        out_r.append(hr)
        out_i.append(hi)
        cr, ci = hr[SUBLANES - 1:, :], hi[SUBLANES - 1:, :]
    hin = jnp.concatenate([jnp.concatenate(out_r, axis=0), jnp.concatenate(out_i, axis=0)], axis=1)
    lr, li = pwr[0:1, :], pwi[0:1, :]
    fr, fi = _cmul(lr, li, cr, ci)
    fin = jnp.concatenate([fr + sr[R - 1:, :], fi + si[R - 1:, :]], axis=1)
    return hin, fin


def _ssm_prompt_body(u_ref, bbc_ref, ccc_ref, pw_ref, tab_ref, d_ref, y_ref, hout_ref,
                     f_ref, gt_ref, m_ref):
    @pl.when(pl.program_id(1) == 0)
    def _():
        _build_weights(bbc_ref, ccc_ref, pw_ref, f_ref, gt_ref, m_ref)
    R = SEQ // CHUNK
    xs = [u_ref[pl.ds(s, R, stride=CHUNK), :] for s in range(CHUNK)]
    xb = jnp.concatenate(xs, axis=1).astype(BF16)
    sloc = jnp.dot(xb, f_ref[...], preferred_element_type=F32)
    hin, fin = _chunk_scan(sloc, tab_ref)
    y = (jnp.dot(xb, m_ref[...], preferred_element_type=F32)
         + _dot_nt(hin.astype(BF16), gt_ref[...]))
    for t in range(CHUNK):
        yt = y[:, t * OCT:(t + 1) * OCT] + d_ref[...] * xs[t]
        y_ref[pl.ds(t, R, stride=CHUNK), :] = _gelu_tanh(yt)
    hout_ref[...] = fin


def _ssm_weight_specs(n_axes):
    if n_axes == 2:
        im3 = lambda o, b: (o, 0, 0)
    else:
        im3 = lambda o: (o, 0, 0)
    return [
        pl.BlockSpec((None, OCT, 2 * SSM_STATE), im3),
        pl.BlockSpec((None, OCT, 2 * SSM_STATE), im3),
        pl.BlockSpec((None, 2 * SUBLANES, SW), im3),
        pl.BlockSpec((None, 4 * SUBLANES, SW), im3),
        pl.BlockSpec((None, 1, OCT), im3),
    ]


_SSM_SCRATCH = [pltpu.VMEM((CW, SW), BF16), pltpu.VMEM((CW, SW), BF16), pltpu.VMEM((CW, CW), BF16)]


def _ssm_prompt(u, tables):
    col0 = POOL_WIDTH // OCT
    return pl.pallas_call(
        _ssm_prompt_body,
        grid=(N_OCT, BATCH),
        in_specs=[pl.BlockSpec((SEQ, OCT), lambda o, b: (b, col0 + o))] + _ssm_weight_specs(2),
        out_specs=(
            pl.BlockSpec((SEQ, OCT), lambda o, b: (b, o)),
            pl.BlockSpec((None, 1, SW), lambda o, b: (b * N_OCT + o, 0, 0)),
        ),
        out_shape=(
            jax.ShapeDtypeStruct((T_ALL, SSM_WIDTH), F32),
            jax.ShapeDtypeStruct((BATCH * N_OCT, 1, SW), F32),
        ),
        scratch_shapes=_SSM_SCRATCH,
        compiler_params=_cparams(2),
        name="ssm_prompt",
    )(u, *tables)


def _ssm_sample_body(u_ref, h0r_ref, h0i_ref, bbc_ref, ccc_ref, pw_ref, tab_ref, d_ref, _dst,
                     y_ref, hr_ref, hi_ref, f_ref, gt_ref, m_ref):
    _build_weights(bbc_ref, ccc_ref, pw_ref, f_ref, gt_ref, m_ref)
    B = DEC_BATCH
    xs = [u_ref[B * s:B * (s + 1), :] for s in range(CHUNK)]
    xb = jnp.concatenate(xs, axis=1).astype(BF16)
    sloc = jnp.dot(xb, f_ref[...], preferred_element_type=F32)
    h0r, h0i = h0r_ref[...], h0i_ref[...]
    hin = jnp.concatenate([h0r, h0i], axis=1).astype(BF16)
    y = (jnp.dot(xb, m_ref[...], preferred_element_type=F32)
         + _dot_nt(hin, gt_ref[...]))
    for t in range(CHUNK):
        yt = y[:, t * OCT:(t + 1) * OCT] + d_ref[...] * xs[t]
        y_ref[B * t:B * (t + 1), :] = _gelu_tanh(yt)
    lr = tab_ref[3 * SUBLANES:3 * SUBLANES + 1, :OCT_STATES]
    li = tab_ref[3 * SUBLANES:3 * SUBLANES + 1, OCT_STATES:]
    nr, ni = _cmul(lr, li, h0r, h0i)
    hr_ref[...] = nr + sloc[:, :OCT_STATES]
    hi_ref[...] = ni + sloc[:, OCT_STATES:]


def _ssm_sample(u, h0r, h0i, tables, y_act):
    col0 = POOL_WIDTH // OCT
    blk = T_PROMPT // T_SAMPLE
    st_spec = pl.BlockSpec((DEC_BATCH, OCT_STATES), lambda o: (0, o))
    return pl.pallas_call(
        _ssm_sample_body,
        grid=(N_OCT,),
        in_specs=[pl.BlockSpec((T_SAMPLE, OCT), lambda o: (blk, col0 + o)), st_spec, st_spec]
        + _ssm_weight_specs(1) + [pl.BlockSpec(memory_space=pl.ANY)],
        out_specs=(pl.BlockSpec((T_SAMPLE, OCT), lambda o: (blk, o)), st_spec, st_spec),
        out_shape=(
            jax.ShapeDtypeStruct((T_ALL, SSM_WIDTH), F32),
            jax.ShapeDtypeStruct((DEC_BATCH, SSM_GROUPS * SSM_STATE), F32),
            jax.ShapeDtypeStruct((DEC_BATCH, SSM_GROUPS * SSM_STATE), F32),
        ),
        scratch_shapes=_SSM_SCRATCH,
        input_output_aliases={8: 0},
        compiler_params=_cparams(1),
        name="ssm_sample",
    )(u, h0r, h0i, *tables, y_act)


PM_TM = 256
PM_PROMPT_BLOCKS = T_PROMPT // PM_TM


def _postmix_body(ya_ref, gp_ref, gs_ref, yp_ref, xp_ref, xs_ref, wa_ref, wb_ref, wo_ref,
                  gf_ref, wrh_ref, wrl_ref, br_ref, h_ref, tn_ref, rt_ref, cnt_out_ref, cnt_ref):
    i = pl.program_id(0)
    @pl.when(i == 0)
    def _():
        cnt_ref[...] = jnp.zeros_like(cnt_ref)
    ya = ya_ref[...].astype(BF16)
    a = jnp.dot(ya, wa_ref[...], preferred_element_type=F32)
    bg = jnp.dot(ya, wb_ref[...], preferred_element_type=F32)
    y_ssm = a * _sigmoid(bg)
    merged = (gp_ref[...].astype(F32) * yp_ref[...].astype(F32)
              + gs_ref[...].astype(F32) * y_ssm)
    x = jnp.where(i < PM_PROMPT_BLOCKS, xp_ref[...], xs_ref[...])
    h = x + jnp.dot(merged.astype(BF16), wo_ref[...], preferred_element_type=F32)
    h_ref[...] = h
    inv = lax.rsqrt(jnp.mean(h * h, axis=-1, keepdims=True) + EPS)
    tn = (h * inv) * gf_ref[...]
    tn_ref[...] = tn
    t_hi = tn.astype(BF16)
    t_lo = (tn - t_hi.astype(F32)).astype(BF16)
    wrh = wrh_ref[...]
    logits = (jnp.dot(t_hi, wrh, preferred_element_type=F32)
              + jnp.dot(t_lo, wrh, preferred_element_type=F32)
              + jnp.dot(t_hi, wrl_ref[...], preferred_element_type=F32)) + br_ref[...]
    lane = lax.broadcasted_iota(I32, (PM_TM, ROUTE_LANES), 1)
    neg = jnp.float32(-jnp.inf)
    big = jnp.int32(1 << 20)
    is_g = lane < MOE_GROUPS
    gmax = jnp.max(jnp.where(is_g, logits, neg), axis=1, keepdims=True)
    g_idx = jnp.min(jnp.where(is_g & (logits == gmax), lane, big), axis=1, keepdims=True)
    g_den = jnp.sum(jnp.where(is_g, jnp.exp(logits - gmax), 0.0), axis=1, keepdims=True)
    g_val = 1.0 / g_den
    e_lane = lane - EXP_LANE0
    sel = (e_lane >= 0) & (e_lane < MOE_EXPERTS) & ((e_lane >> 3) == g_idx)
    m1 = jnp.max(jnp.where(sel, logits, neg), axis=1, keepdims=True)
    i1 = jnp.min(jnp.where(sel & (logits == m1), lane, big), axis=1, keepdims=True)
    sel2 = sel & (lane != i1)
    m2 = jnp.max(jnp.where(sel2, logits, neg), axis=1, keepdims=True)
    i2 = jnp.min(jnp.where(sel2 & (logits == m2), lane, big), axis=1, keepdims=True)
    e2 = jnp.exp(m2 - m1)
    w1 = g_val / (1.0 + e2)
    w2 = g_val * e2 / (1.0 + e2)
    oh1 = lane == i1
    oh2 = lane == i2
    oh = jnp.where(oh1 | oh2, 1.0, 0.0)
    rr = lax.broadcasted_iota(I32, (PM_TM, PM_TM), 0)
    cc = lax.broadcasted_iota(I32, (PM_TM, PM_TM), 1)
    tri = jnp.where(cc < rr, 1.0, 0.0).astype(BF16)
    base = cnt_ref[...] + jnp.dot(tri, oh.astype(BF16), preferred_element_type=F32)
    rank1 = jnp.sum(jnp.where(oh1, base, 0.0), axis=1, keepdims=True)
    rank2 = jnp.sum(jnp.where(oh2, base, 0.0), axis=1, keepdims=True)
    cnt_ref[...] = cnt_ref[...] + jnp.sum(oh, axis=0, keepdims=True)
    cnt_out_ref[...] = cnt_ref[...]
    rt = jnp.where(lane == 0, w1, 0.0)
    rt = jnp.where(lane == 1, w2, rt)
    rt = jnp.where(lane == 2, rank1, rt)
    rt = jnp.where(lane == 3, rank2, rt)
    rt = jnp.where(lane == 4, (i1 - EXP_LANE0).astype(F32), rt)
    rt = jnp.where(lane == 5, (i2 - EXP_LANE0).astype(F32), rt)
    rt_ref[...] = rt


def _postmix(y_act, gates, y_pool, xp, xs, wa, wb, wo, g_ffn, wr_hi, wr_lo, b_r):
    n = T_ALL // PM_TM
    npb = PM_PROMPT_BLOCKS
    const2 = lambda i: (0, 0)
    return pl.pallas_call(
        _postmix_body,
        grid=(n,),
        in_specs=[
            pl.BlockSpec((PM_TM, SSM_WIDTH), lambda i: (i, 0)),
            pl.BlockSpec((PM_TM, D_MODEL), lambda i: (i, 0)),
            pl.BlockSpec((PM_TM, D_MODEL), lambda i: (i, 1)),
            pl.BlockSpec((PM_TM, D_MODEL), lambda i: (i, 0)),
            pl.BlockSpec((PM_TM, D_MODEL), lambda i: (jnp.minimum(i, npb - 1), 0)),
            pl.BlockSpec((PM_TM, D_MODEL), lambda i: (jnp.maximum(i - npb, 0), 0)),
            pl.BlockSpec((SSM_WIDTH, D_MODEL), const2, pipeline_mode=pl.Buffered(1)),
            pl.BlockSpec((SSM_WIDTH, D_MODEL), const2, pipeline_mode=pl.Buffered(1)),
            pl.BlockSpec((D_MODEL, D_MODEL), const2, pipeline_mode=pl.Buffered(1)),
            pl.BlockSpec((1, D_MODEL), const2),
            pl.BlockSpec((D_MODEL, ROUTE_LANES), const2),
            pl.BlockSpec((D_MODEL, ROUTE_LANES), const2),
            pl.BlockSpec((1, ROUTE_LANES), const2),
        ],
        out_specs=(
            pl.BlockSpec((PM_TM, D_MODEL), lambda i: (i, 0)),
            pl.BlockSpec((PM_TM, D_MODEL), lambda i: (i, 0)),
            pl.BlockSpec((PM_TM, ROUTE_LANES), lambda i: (i, 0)),
            pl.BlockSpec((1, ROUTE_LANES), const2),
        ),
        out_shape=(
            jax.ShapeDtypeStruct((T_ALL, D_MODEL), F32),
            jax.ShapeDtypeStruct((T_ALL, D_MODEL), F32),
            jax.ShapeDtypeStruct((T_ALL, ROUTE_LANES), F32),
            jax.ShapeDtypeStruct((1, ROUTE_LANES), F32),
        ),
        scratch_shapes=[pltpu.VMEM((1, ROUTE_LANES), F32)],
        compiler_params=_cparams(1),
        name="postmix",
    )(y_act, gates, gates, y_pool, xp, xs, wa, wb, wo, g_ffn, wr_hi, wr_lo, b_r)


SC_CH = 16
W_SPLIT = 4


def _sc_workers():
    info = plsc.get_sparse_core_info()
    return info.num_cores, info.num_cores * info.num_subcores


def _sc_dispatch(tn, slots):
    n_cores, n_workers = _sc_workers()
    per_w = (T_ALL // SC_CH) // n_workers
    assert per_w * n_workers * SC_CH == T_ALL
    slots = slots.reshape(2, n_workers, per_w, SC_CH)

    @functools.partial(
        pl.kernel,
        mesh=plsc.VectorSubcoreMesh(core_axis_name="c", subcore_axis_name="s"),
        out_type=jax.ShapeDtypeStruct((N_SLOTS, D_MODEL), F32),
        scratch_types=[pltpu.VMEM((2, per_w, SC_CH), I32), pltpu.VMEM((SC_CH, D_MODEL), F32)],
    )
    def k(tn_hbm, slots_hbm, xs_hbm, idx_v, rows_v):
        wid = lax.axis_index("s") * n_cores + lax.axis_index("c")
        c0 = wid * per_w
        pltpu.sync_copy(slots_hbm.at[0, wid], idx_v.at[0])
        pltpu.sync_copy(slots_hbm.at[1, wid], idx_v.at[1])

        @pl.loop(0, per_w)
        def _(c):
            row0 = pl.multiple_of((c0 + c) * SC_CH, SC_CH)
            pltpu.sync_copy(tn_hbm.at[pl.ds(row0, SC_CH)], rows_v)
            pltpu.sync_copy(rows_v, xs_hbm.at[idx_v.at[0, c]])
            pltpu.sync_copy(rows_v, xs_hbm.at[idx_v.at[1, c]])

    return k(tn, slots)


def _sc_collect(ys, slots):
    n_cores, n_workers = _sc_workers()
    per_w = (N_ASSIGN // SC_CH) // n_workers
    assert per_w * n_workers * SC_CH == N_ASSIGN
    slots = slots.reshape(n_workers, per_w, SC_CH)

    @functools.partial(
        pl.kernel,
        mesh=plsc.VectorSubcoreMesh(core_axis_name="c", subcore_axis_name="s"),
        out_type=jax.ShapeDtypeStruct((N_ASSIGN, D_MODEL), F32),
        scratch_types=[pltpu.VMEM((per_w, SC_CH), I32), pltpu.VMEM((SC_CH, D_MODEL), F32)],
    )
    def k(ys_hbm, slots_hbm, out_hbm, idx_v, rows_v):
        wid = lax.axis_index("s") * n_cores + lax.axis_index("c")
        c0 = wid * per_w
        pltpu.sync_copy(slots_hbm.at[wid], idx_v)

        @pl.loop(0, per_w)
        def _(c):
            row0 = pl.multiple_of((c0 + c) * SC_CH, SC_CH)
            pltpu.sync_copy(ys_hbm.at[idx_v.at[c]], rows_v)
            pltpu.sync_copy(rows_v, out_hbm.at[pl.ds(row0, SC_CH)])

    return k(ys, slots)


def _expert_body(t0_ref, nt_ref, xs_hbm, *rest):
    w_refs = rest[:3 * W_SPLIT]
    ys_hbm, xb_ref, yb_ref, wgb_ref, wub_ref, wdb_ref, xsem, ysem = rest[3 * W_SPLIT:]
    e = pl.program_id(0)
    n = nt_ref[e]
    g0 = t0_ref[e]

    def rows(j):
        return pl.ds(pl.multiple_of((g0 + j) * TME, TME), TME)

    def x_copy(j, s):
        return pltpu.make_async_copy(xs_hbm.at[rows(j)], xb_ref.at[s], xsem.at[s])

    def y_copy(j, s):
        return pltpu.make_async_copy(yb_ref.at[s], ys_hbm.at[rows(j)], ysem.at[s])

    @pl.when(n > 0)
    def _():
        x_copy(0, 0).start()
        for m, dst in enumerate((wgb_ref, wub_ref, wdb_ref)):
            rb = dst.shape[0] // W_SPLIT
            for p in range(W_SPLIT):
                dst[p * rb:(p + 1) * rb, :] = w_refs[m * W_SPLIT + p][...].astype(BF16)

        def tile(j, c):
            s = j & 1
            x_copy(j, s).wait()
            @pl.when(j + 1 < n)
            def _():
                x_copy(j + 1, 1 - s).start()
            @pl.when(j >= 2)
            def _():
                y_copy(j - 2, s).wait()
            x = xb_ref[s].astype(BF16)
            hg = jnp.dot(x, wgb_ref[...], preferred_element_type=F32)
            hu = jnp.dot(x, wub_ref[...], preferred_element_type=F32)
            act = (hg * _sigmoid(hg)) * hu
            yb_ref[s] = jnp.dot(act.astype(BF16), wdb_ref[...], preferred_element_type=F32)
            y_copy(j, s).start()
            return c
        lax.fori_loop(0, n, tile, 0)

        @pl.when(n >= 2)
        def _():
            y_copy(n - 2, n & 1).wait()
        y_copy(n - 1, (n - 1) & 1).wait()


def _experts(tile0, tiles, xs, w_eg, w_eu, w_ed):
    def w_specs(rows, cols):
        return [pl.BlockSpec((None, rows // W_SPLIT, cols), functools.partial(lambda p, e, t0, nt: (e, p, 0), p))
                for p in range(W_SPLIT)]
    grid_spec = pltpu.PrefetchScalarGridSpec(
        num_scalar_prefetch=2,
        grid=(MOE_EXPERTS,),
        in_specs=[pl.BlockSpec(memory_space=pl.ANY)]
        + w_specs(D_MODEL, MOE_FF) + w_specs(D_MODEL, MOE_FF) + w_specs(MOE_FF, D_MODEL),
        out_specs=pl.BlockSpec(memory_space=pl.ANY),
        scratch_shapes=[
            pltpu.VMEM((2, TME, D_MODEL), F32),
            pltpu.VMEM((2, TME, D_MODEL), F32),
            pltpu.VMEM((D_MODEL, MOE_FF), BF16),
            pltpu.VMEM((D_MODEL, MOE_FF), BF16),
            pltpu.VMEM((MOE_FF, D_MODEL), BF16),
            pltpu.SemaphoreType.DMA((2,)),
            pltpu.SemaphoreType.DMA((2,)),
        ],
    )
    return pl.pallas_call(
        _expert_body,
        grid_spec=grid_spec,
        out_shape=jax.ShapeDtypeStruct((N_SLOTS, D_MODEL), F32),
        compiler_params=_cparams(1),
        name="experts",
    )(tile0, tiles, xs, *([w_eg] * W_SPLIT + [w_eu] * W_SPLIT + [w_ed] * W_SPLIT))


FN_TM = 256
FN_PROMPT_BLOCKS = T_PROMPT // FN_TM


def _final_body(h_ref, y0_ref, y1_ref, rt_ref, g_ref, op_ref, os_ref):
    i = pl.program_id(0)
    rt = rt_ref[...]
    z = h_ref[...] + rt[:, 0:1] * y0_ref[...] + rt[:, 1:2] * y1_ref[...]
    inv = lax.rsqrt(jnp.mean(z * z, axis=-1, keepdims=True) + EPS)
    out = (z * inv) * g_ref[...]
    @pl.when(i < FN_PROMPT_BLOCKS)
    def _():
        op_ref[...] = out
    @pl.when(i >= FN_PROMPT_BLOCKS)
    def _():
        os_ref[...] = out


def _final(h, y, route, g_final):
    n = T_ALL // FN_TM
    npb = FN_PROMPT_BLOCKS
    yoff = T_ALL // FN_TM
    return pl.pallas_call(
        _final_body,
        grid=(n,),
        in_specs=[
            pl.BlockSpec((FN_TM, D_MODEL), lambda i: (i, 0)),
            pl.BlockSpec((FN_TM, D_MODEL), lambda i: (i, 0)),
            pl.BlockSpec((FN_TM, D_MODEL), lambda i: (yoff + i, 0)),
            pl.BlockSpec((FN_TM, ROUTE_LANES), lambda i: (i, 0)),
            pl.BlockSpec((1, D_MODEL), lambda i: (0, 0)),
        ],
        out_specs=(
            pl.BlockSpec((FN_TM, D_MODEL), lambda i: (jnp.minimum(i, npb - 1), 0)),
            pl.BlockSpec((FN_TM, D_MODEL), lambda i: (jnp.maximum(i - npb, 0), 0)),
        ),
        out_shape=(
            jax.ShapeDtypeStruct((T_PROMPT, D_MODEL), F32),
            jax.ShapeDtypeStruct((T_SAMPLE, D_MODEL), F32),
        ),
        compiler_params=_cparams(1),
        name="final",
    )(h, y, y, route, g_final)


def _dispatch_plan(route, cnt):
    counts = cnt[0, EXP_LANE0:EXP_LANE0 + MOE_EXPERTS].astype(I32)
    tiles = (counts + (TME - 1)) // TME
    cumt = jnp.cumsum(tiles)
    pad_off = (cumt - tiles) * TME
    rank = route[:, 2:4].astype(I32)
    eid = route[:, 4:6].astype(I32)
    onehot = eid[..., None] == jnp.arange(MOE_EXPERTS, dtype=I32)
    slots = (rank + jnp.sum(jnp.where(onehot, pad_off, 0), axis=-1)).T
    return slots, cumt - tiles, tiles


def kernel(x_prompt, x_sample, state_pool, state_ssm_re, state_ssm_im, g_mix, w_in, w_pool,
           pool_scale, ssm_a_re, ssm_a_im, ssm_log_dt, ssm_b_re, ssm_b_im, ssm_c_re, ssm_c_im,
           ssm_d, w_glu_a, w_glu_b, w_out, g_ffn, w_router_group, b_router_group,
           w_router_expert, b_router_expert, w_exp_gate, w_exp_up, w_exp_down, g_final):
    l = 0
    xp = x_prompt.reshape(T_PROMPT, D_MODEL)
    xs = x_sample.transpose(1, 0, 2).reshape(T_SAMPLE, D_MODEL)
    w_in_bf = w_in[l].astype(BF16)
    w_pool_bf = w_pool[l].astype(BF16)
    wa_bf = w_glu_a[l].astype(BF16)
    wb_bf = w_glu_b[l].astype(BF16)
    wo_bf = w_out[l].astype(BF16)
    g_mix2 = g_mix[l].reshape(1, D_MODEL)
    scale2 = pool_scale[l].reshape(1, D_MODEL)

    u, gates = _inproj(xp, g_mix2, w_in_bf, 0)
    u, gates = _inproj(xs, g_mix2, w_in_bf, T_PROMPT // IN_TM, dst=(u, gates))

    y_pool, pool_tail = _pool_prompt(u, w_pool_bf, scale2)
    hist_tm = state_pool[l].transpose(1, 0, 2)
    y_pool = _pool_sample(u, hist_tm, w_pool_bf, scale2, y_pool)

    tables = _ssm_tables(ssm_a_re[l], ssm_a_im[l], ssm_log_dt[l], ssm_b_re[l], ssm_b_im[l],
                         ssm_c_re[l], ssm_c_im[l], ssm_d[l])
    y_act, h_prompt = _ssm_prompt(u, tables)
    h0r = state_ssm_re[l].reshape(DEC_BATCH, SSM_GROUPS * SSM_STATE)
    h0i = state_ssm_im[l].reshape(DEC_BATCH, SSM_GROUPS * SSM_STATE)
    y_act, hs_re, hs_im = _ssm_sample(u, h0r, h0i, tables, y_act)

    w_r = jnp.zeros((D_MODEL, ROUTE_LANES), F32)
    w_r = w_r.at[:, :MOE_GROUPS].set(w_router_group[l])
    w_r = w_r.at[:, EXP_LANE0:EXP_LANE0 + MOE_EXPERTS].set(w_router_expert[l])
    wr_hi = w_r.astype(BF16)
    wr_lo = (w_r - wr_hi.astype(F32)).astype(BF16)
    b_r = jnp.zeros((1, ROUTE_LANES), F32)
    b_r = b_r.at[0, :MOE_GROUPS].set(b_router_group[l])
    b_r = b_r.at[0, EXP_LANE0:EXP_LANE0 + MOE_EXPERTS].set(b_router_expert[l])

    h, tn, route, cnt = _postmix(y_act, gates, y_pool, xp, xs, wa_bf, wb_bf, wo_bf,
                                 g_ffn[l].reshape(1, D_MODEL), wr_hi, wr_lo, b_r)
    slots, tile0, tiles = _dispatch_plan(route, cnt)
    xs_sorted = _sc_dispatch(tn, slots)
    ys_sorted = _experts(tile0, tiles, xs_sorted, w_exp_gate[l], w_exp_up[l], w_exp_down[l])
    y = _sc_collect(ys_sorted, slots)
    yp, ys = _final(h, y, route, g_final.reshape(1, D_MODEL))

    y_prompt = yp.reshape(BATCH, SEQ, D_MODEL)
    y_sample = ys.reshape(DEC_SEQ, DEC_BATCH, D_MODEL).transpose(1, 0, 2)
    new_pool_prompt = pool_tail[:, HIST - POOL_BUF:, :][None]
    us = u[T_PROMPT:, :POOL_WIDTH].reshape(DEC_SEQ, DEC_BATCH, POOL_WIDTH).transpose(1, 0, 2)
    new_pool_sample = jnp.concatenate([state_pool[l][:, DEC_SEQ:, :], us], axis=1)[None]
    hp = h_prompt.reshape(BATCH, N_OCT, 2, OCT_GROUPS, SSM_STATE).transpose(2, 0, 1, 3, 4)
    hp = hp.reshape(2, BATCH, SSM_GROUPS, SSM_STATE)
    shp = (1, DEC_BATCH, SSM_GROUPS, SSM_STATE)
    return (y_prompt, y_sample, new_pool_prompt, hp[0][None], hp[1][None], new_pool_sample,
            hs_re.reshape(shp), hs_im.reshape(shp))
```

```python
import functools
import math

import jax
import jax.numpy as jnp
from jax import lax
from jax.experimental import pallas as pl
from jax.experimental.pallas import tpu as pltpu
from jax.experimental.pallas import tpu_sc as plsc

F32 = jnp.float32
BF16 = jnp.bfloat16
I32 = jnp.int32
U32 = jnp.uint32

D_MODEL = 2048
BATCH = 4
SEQ = 2048
DEC_BATCH = 128
DEC_SEQ = 8
PAST_LEN = 16384
POOL_WIDTH = D_MODEL // 2
POOL_WINDOWS = (2, 4, 8, 16)
POOL_GROUPS = len(POOL_WINDOWS)
POOL_GROUP_CH = POOL_WIDTH // POOL_GROUPS
POOL_OUT_CH = D_MODEL // POOL_GROUPS
POOL_BUF = max(POOL_WINDOWS) - 1
SSM_WIDTH = D_MODEL // 2
SSM_GROUP_CH = 16
SSM_GROUPS = SSM_WIDTH // SSM_GROUP_CH
SSM_STATE = 64
IN_WIDTH = POOL_WIDTH + SSM_WIDTH + 2 * D_MODEL
D_PACK = D_MODEL // 2
MOE_GROUPS = 4
MOE_EPG = 8
MOE_EXPERTS = MOE_GROUPS * MOE_EPG
MOE_FF = D_MODEL // 4
EPS = 1e-6

T_PROMPT = BATCH * SEQ
T_SAMPLE = DEC_BATCH * DEC_SEQ
T_ALL = T_PROMPT + T_SAMPLE

LANES = 128
SUBLANES = 8
VMEM_LIMIT = 56 * 1024 * 1024

CHUNK = 8
OCT = LANES
N_OCT = SSM_WIDTH // OCT
OCT_GROUPS = OCT // SSM_GROUP_CH
OCT_STATES = OCT_GROUPS * SSM_STATE
CW = CHUNK * OCT
SW = 2 * OCT_STATES

ROUTE_LANES = LANES
EXP_LANE0 = MOE_GROUPS
N_ASSIGN = 2 * T_ALL
TME = 256
N_ITEMS_MAX = N_ASSIGN // TME + MOE_EXPERTS
N_SLOTS = N_ITEMS_MAX * TME


def _cparams(n_axes):
    return pltpu.CompilerParams(dimension_semantics=("arbitrary",) * n_axes,
                                vmem_limit_bytes=VMEM_LIMIT)


def _sigmoid(x):
    return 1.0 / (1.0 + jnp.exp(-x))


def _pack_pairs(x):
    c = x.shape[1] // 2
    hi = lax.bitcast_convert_type(x[:, :c].astype(BF16).astype(F32), U32)
    lo = lax.bitcast_convert_type(x[:, c:].astype(BF16).astype(F32), U32)
    return hi | (lo >> 16)


def _unpack_pairs(u, dtype):
    hi = lax.bitcast_convert_type(u & jnp.uint32(0xFFFF0000), F32)
    lo = lax.bitcast_convert_type(u << 16, F32)
    return jnp.concatenate([hi, lo], axis=1).astype(dtype)


def _gelu_tanh(x):
    c = math.sqrt(2.0 / math.pi)
    return 0.5 * x * (1.0 + jnp.tanh(c * (x + 0.044715 * (x * x * x))))


IN_TM = 1024
IN_TN = 1024
U_WIDTH = POOL_WIDTH + SSM_WIDTH
GATE_WIDTH = 2 * D_MODEL
IN_U_STEPS = U_WIDTH // IN_TN


def _inproj_body(x_ref, g_ref, w_ref, *rest):
    u_ref, gate_ref, xn_ref = rest[-3:]
    j = pl.program_id(1)
    @pl.when(j == 0)
    def _():
        x = x_ref[...]
        inv = lax.rsqrt(jnp.mean(x * x, axis=-1, keepdims=True) + EPS)
        xn_ref[...] = ((x * inv) * g_ref[...]).astype(BF16)
    acc = jnp.dot(xn_ref[...], w_ref[...], preferred_element_type=F32)
    @pl.when(j < IN_U_STEPS)
    def _():
        u_ref[...] = acc
    @pl.when(j >= IN_U_STEPS)
    def _():
        gate_ref[...] = _sigmoid(acc).astype(BF16)


def _inproj(x, g, w_bf, row_block0, dst=None):
    n_i = x.shape[0] // IN_TM
    in_specs = [
        pl.BlockSpec((IN_TM, D_MODEL), lambda i, j: (i, 0)),
        pl.BlockSpec((1, D_MODEL), lambda i, j: (0, 0)),
        pl.BlockSpec((D_MODEL, IN_TN), lambda i, j: (0, j)),
    ]
    args = [x, g, w_bf]
    aliases = {}
    if dst is not None:
        in_specs += [pl.BlockSpec(memory_space=pl.ANY)] * 2
        args += list(dst)
        aliases = {3: 0, 4: 1}
    return pl.pallas_call(
        _inproj_body,
        grid=(n_i, IN_WIDTH // IN_TN),
        in_specs=in_specs,
        out_specs=(
            pl.BlockSpec((IN_TM, IN_TN), lambda i, j: (i + row_block0, jnp.minimum(j, IN_U_STEPS - 1))),
            pl.BlockSpec((IN_TM, IN_TN), lambda i, j: (i + row_block0, jnp.maximum(j - IN_U_STEPS, 0))),
        ),
        out_shape=(
            jax.ShapeDtypeStruct((T_ALL, U_WIDTH), F32),
            jax.ShapeDtypeStruct((T_ALL, GATE_WIDTH), BF16),
        ),
        scratch_shapes=[pltpu.VMEM((IN_TM, D_MODEL), BF16)],
        input_output_aliases=aliases,
        compiler_params=_cparams(2),
        name="inproj",
    )(*args)


PP_TM = 512
HIST = 16


def _pool_project(pooled_g, g, w_ref, sc_ref, o_ref):
    y = jnp.dot(pooled_g.astype(BF16), w_ref[g], preferred_element_type=F32)
    lo, hi = g * POOL_OUT_CH, (g + 1) * POOL_OUT_CH
    o_ref[:, lo:hi] = (y * sc_ref[:, lo:hi]).astype(o_ref.dtype)


def _pool_prompt_body(u_ref, w_ref, sc_ref, o_ref, tail_ref, hist_ref):
    i = pl.program_id(1)
    @pl.when(i == 0)
    def _():
        hist_ref[...] = jnp.zeros_like(hist_ref)
    u = u_ref[...]
    ext = jnp.concatenate([hist_ref[...], u], axis=0)
    hist_ref[...] = u[PP_TM - HIST:, :]
    tail_ref[...] = u[PP_TM - HIST:, :]
    pos = i * PP_TM + lax.broadcasted_iota(I32, (PP_TM, 1), 0)
    for g, w in enumerate(POOL_WINDOWS):
        lo, hi = g * POOL_GROUP_CH, (g + 1) * POOL_GROUP_CH
        s = ext[:, lo:hi]
        d = 1
        while d < w:
            s = s + pltpu.roll(s, d, axis=0)
            d *= 2
        cnt = jnp.minimum(w, pos + 1).astype(F32)
        pooled = s[HIST:, :] / cnt - u[:, lo:hi]
        _pool_project(pooled, g, w_ref, sc_ref, o_ref)


def _pool_prompt(u, w_pool_bf, pool_scale):
    n_i = SEQ // PP_TM
    return pl.pallas_call(
        _pool_prompt_body,
        grid=(BATCH, n_i),
        in_specs=[
            pl.BlockSpec((PP_TM, POOL_WIDTH), lambda b, i: (b * n_i + i, 0)),
            pl.BlockSpec((POOL_GROUPS, POOL_GROUP_CH, POOL_OUT_CH), lambda b, i: (0, 0, 0)),
            pl.BlockSpec((1, D_MODEL), lambda b, i: (0, 0)),
        ],
        out_specs=(
            pl.BlockSpec((PP_TM, D_MODEL), lambda b, i: (b * n_i + i, 0)),
            pl.BlockSpec((None, HIST, POOL_WIDTH), lambda b, i: (b, 0, 0)),
        ),
        out_shape=(
            jax.ShapeDtypeStruct((T_ALL, D_MODEL), BF16),
            jax.ShapeDtypeStruct((BATCH, HIST, POOL_WIDTH), F32),
        ),
        scratch_shapes=[pltpu.VMEM((HIST, POOL_WIDTH), F32)],
        compiler_params=_cparams(2),
        name="pool_prompt",
    )(u, w_pool_bf, pool_scale)


def _pool_sample_body(u_ref, hist_ref, w_ref, sc_ref, _dst, o_ref):
    rows = [hist_ref[k] for k in range(POOL_BUF)]
    rows += [u_ref[DEC_BATCH * t:DEC_BATCH * (t + 1), :] for t in range(DEC_SEQ)]
    n = len(rows)
    for g, w in enumerate(POOL_WINDOWS):
        lo, hi = g * POOL_GROUP_CH, (g + 1) * POOL_GROUP_CH
        f = [r[:, lo:hi] for r in rows]
        cur = f
        d = 1
        while d < w:
            cur = [cur[k] + cur[k - d] if k - d >= 0 else cur[k] for k in range(n)]
            d *= 2
        pooled = jnp.concatenate(
            [cur[POOL_BUF + t] / float(w) - f[POOL_BUF + t] for t in range(DEC_SEQ)], axis=0)
        _pool_project(pooled, g, w_ref, sc_ref, o_ref)


def _pool_sample(u, hist_tm, w_pool_bf, pool_scale, y_pool):
    blk = T_PROMPT // T_SAMPLE
    return pl.pallas_call(
        _pool_sample_body,
        grid=(1,),
        in_specs=[
            pl.BlockSpec((T_SAMPLE, POOL_WIDTH), lambda i: (blk, 0)),
            pl.BlockSpec((POOL_BUF, DEC_BATCH, POOL_WIDTH), lambda i: (0, 0, 0)),
            pl.BlockSpec((POOL_GROUPS, POOL_GROUP_CH, POOL_OUT_CH), lambda i: (0, 0, 0)),
            pl.BlockSpec((1, D_MODEL), lambda i: (0, 0)),
            pl.BlockSpec(memory_space=pl.ANY),
        ],
        out_specs=pl.BlockSpec((T_SAMPLE, D_MODEL), lambda i: (blk, 0)),
        out_shape=jax.ShapeDtypeStruct((T_ALL, D_MODEL), BF16),
        input_output_aliases={4: 0},
        compiler_params=_cparams(1),
        name="pool_sample",
    )(u, hist_tm, w_pool_bf, pool_scale, y_pool)


def _ssm_tables(a_re, a_im, log_dt, b_re, b_im, c_re, c_im, d_skip):
    dt = jnp.exp(log_dt)[:, None]
    lr, li = a_re, a_im
    ab_re = jnp.exp(lr * dt) * jnp.cos(li * dt)
    ab_im = jnp.exp(lr * dt) * jnp.sin(li * dt)
    den = lr * lr + li * li
    nr, ni = ab_re - 1.0, ab_im
    q_re = (nr * lr + ni * li) / den
    q_im = (ni * lr - nr * li) / den
    bb_re = q_re[..., None] * b_re - q_im[..., None] * b_im
    bb_im = q_re[..., None] * b_im + q_im[..., None] * b_re

    def lam_rows(ks):
        k = jnp.asarray(ks, F32)[:, None, None]
        m = jnp.exp(k * lr * dt)
        re = (m * jnp.cos(k * li * dt)).reshape(len(ks), N_OCT, OCT_STATES)
        im = (m * jnp.sin(k * li * dt)).reshape(len(ks), N_OCT, OCT_STATES)
        return jnp.concatenate([re, im], axis=-1).transpose(1, 0, 2)

    def compact(re, im):
        v = jnp.concatenate([re, im], axis=-1)
        return v.reshape(N_OCT, OCT, 2 * SSM_STATE)

    bbc = compact(jnp.swapaxes(bb_re, 1, 2), jnp.swapaxes(bb_im, 1, 2))
    ccc = compact(c_re, c_im)
    pw = lam_rows(list(range(2 * SUBLANES)))
    r = jnp.arange(SUBLANES)[None, :, None]
    parts = [jnp.where(r >= dd, lam_rows([CHUNK * dd]), 0.0) for dd in (1, 2, 4)]
    parts.append(lam_rows([CHUNK * kk for kk in range(1, SUBLANES + 1)]))
    tab = jnp.concatenate(parts, axis=1)
    dsk = d_skip.reshape(N_OCT, 1, OCT)
    return bbc, ccc, pw, tab, dsk


def _split_bf16(x):
    hi = x.astype(BF16)
    return hi, (x - hi.astype(F32)).astype(BF16)


def _dot_nt(a, b):
    return lax.dot_general(a, b, (((1,), (1,)), ((), ())), preferred_element_type=F32)


def _build_weights(bbc_ref, ccc_ref, pw_ref, f_ref, gt_ref, m_ref):
    row_gi = lax.broadcasted_iota(I32, (OCT, 1), 0) >> 4
    col = lax.broadcasted_iota(I32, (1, SW), 1)
    col_gi = (col >> 6) & 7
    src = ((col >> 9) << 6) | (col & 63)
    k128 = lax.broadcasted_iota(I32, (2 * SSM_STATE, 1), 0)
    spread = jnp.where(k128 == src, 1.0, 0.0).astype(BF16)
    diag = row_gi == col_gi

    def expand(c_ref):
        hi, lo = _split_bf16(c_ref[...])
        d = (jnp.dot(hi, spread, preferred_element_type=F32)
             + jnp.dot(lo, spread, preferred_element_type=F32))
        d = jnp.where(diag, d, 0.0)
        return d[:, :OCT_STATES], d[:, OCT_STATES:]

    br, bi = expand(bbc_ref)
    cr, ci = expand(ccc_ref)
    chi_r, clo_r = _split_bf16(cr)
    chi_i, clo_i = _split_bf16(ci)

    def lam(k):
        return pw_ref[k:k + 1, :OCT_STATES], pw_ref[k:k + 1, OCT_STATES:]

    def dot3(a, bhi, blo):
        ahi, alo = _split_bf16(a)
        return _dot_nt(ahi, bhi) + _dot_nt(alo, bhi) + _dot_nt(ahi, blo)

    lags = []
    for k in range(CHUNK):
        pr, pi_ = lam(k)
        fr, fi = _cmul(br, bi, pr, pi_)
        s = CHUNK - 1 - k
        f_ref[s * OCT:(s + 1) * OCT, :] = jnp.concatenate([fr, fi], axis=1).astype(BF16)
        lags.append((dot3(fr, chi_r, clo_r) - dot3(fi, chi_i, clo_i)).astype(BF16))
        pr, pi_ = lam(k + 1)
        gr, gi = _cmul(cr, ci, pr, pi_)
        gt_ref[k * OCT:(k + 1) * OCT, :] = jnp.concatenate([gr, -gi], axis=1).astype(BF16)
    zero = jnp.zeros((OCT, OCT), BF16)
    for s in range(CHUNK):
        for t in range(CHUNK):
            m_ref[s * OCT:(s + 1) * OCT, t * OCT:(t + 1) * OCT] = lags[t - s] if t >= s else zero


def _cmul(ar, ai, br, bi):
    return ar * br - ai * bi, ar * bi + ai * br


def _chunk_scan(sloc, tab_ref):
    R = sloc.shape[0]
    nb = R // SUBLANES
    sr, si = sloc[:, :OCT_STATES], sloc[:, OCT_STATES:]
    rowi = lax.broadcasted_iota(I32, (R, 1), 0)
    tr = jnp.where(rowi == 0, 0.0, pltpu.roll(sr, 1, axis=0))
    ti = jnp.where(rowi == 0, 0.0, pltpu.roll(si, 1, axis=0))
    for lvl, d in enumerate((1, 2, 4)):
        mr = tab_ref[lvl * SUBLANES:(lvl + 1) * SUBLANES, :OCT_STATES]
        mi = tab_ref[lvl * SUBLANES:(lvl + 1) * SUBLANES, OCT_STATES:]
        mr = jnp.concatenate([mr] * nb, axis=0)
        mi = jnp.concatenate([mi] * nb, axis=0)
        pr, pi_ = _cmul(mr, mi, pltpu.roll(tr, d, axis=0), pltpu.roll(ti, d, axis=0))
        tr, ti = tr + pr, ti + pi_
    pwr = tab_ref[3 * SUBLANES:4 * SUBLANES, :OCT_STATES]
    pwi = tab_ref[3 * SUBLANES:4 * SUBLANES, OCT_STATES:]
    cr = jnp.zeros((1, OCT_STATES), F32)
    ci = jnp.zeros((1, OCT_STATES), F32)
    out_r, out_i = [], []
    for k in range(nb):
        ar = tr[k * SUBLANES:(k + 1) * SUBLANES, :]
        ai = ti[k * SUBLANES:(k + 1) * SUBLANES, :]
        pr, pi_ = _cmul(pwr, pwi, jnp.broadcast_to(cr, ar.shape), jnp.broadcast_to(ci, ai.shape))
        hr, hi = ar + pr, ai + pi_
        out_r.append(hr)
        out_i.append(hi)
        cr, ci = hr[SUBLANES - 1:, :], hi[SUBLANES - 1:, :]
    hin = jnp.concatenate([jnp.concatenate(out_r, axis=0), jnp.concatenate(out_i, axis=0)], axis=1)
    lr, li = pwr[0:1, :], pwi[0:1, :]
    fr, fi = _cmul(lr, li, cr, ci)
    fin = jnp.concatenate([fr + sr[R - 1:, :], fi + si[R - 1:, :]], axis=1)
    return hin, fin


def _ssm_prompt_body(u_ref, bbc_ref, ccc_ref, pw_ref, tab_ref, d_ref, y_ref, hout_ref,
                     f_ref, gt_ref, m_ref):
    @pl.when(pl.program_id(1) == 0)
    def _():
        _build_weights(bbc_ref, ccc_ref, pw_ref, f_ref, gt_ref, m_ref)
    R = SEQ // CHUNK
    xs = [u_ref[pl.ds(s, R, stride=CHUNK), :] for s in range(CHUNK)]
    xb = jnp.concatenate(xs, axis=1).astype(BF16)
    sloc = jnp.dot(xb, f_ref[...], preferred_element_type=F32)
    hin, fin = _chunk_scan(sloc, tab_ref)
    y = (jnp.dot(xb, m_ref[...], preferred_element_type=F32)
         + _dot_nt(hin.astype(BF16), gt_ref[...]))
    for t in range(CHUNK):
        yt = y[:, t * OCT:(t + 1) * OCT] + d_ref[...] * xs[t]
        y_ref[pl.ds(t, R, stride=CHUNK), :] = _gelu_tanh(yt)
    hout_ref[...] = fin


def _ssm_weight_specs(n_axes):
    if n_axes == 2:
        im3 = lambda o, b: (o, 0, 0)
    else:
        im3 = lambda o: (o, 0, 0)
    return [
        pl.BlockSpec((None, OCT, 2 * SSM_STATE), im3),
        pl.BlockSpec((None, OCT, 2 * SSM_STATE), im3),
        pl.BlockSpec((None, 2 * SUBLANES, SW), im3),
        pl.BlockSpec((None, 4 * SUBLANES, SW), im3),
        pl.BlockSpec((None, 1, OCT), im3),
    ]


_SSM_SCRATCH = [pltpu.VMEM((CW, SW), BF16), pltpu.VMEM((CW, SW), BF16), pltpu.VMEM((CW, CW), BF16)]


def _ssm_prompt(u, tables):
    col0 = POOL_WIDTH // OCT
    return pl.pallas_call(
        _ssm_prompt_body,
        grid=(N_OCT, BATCH),
        in_specs=[pl.BlockSpec((SEQ, OCT), lambda o, b: (b, col0 + o))] + _ssm_weight_specs(2),
        out_specs=(
            pl.BlockSpec((SEQ, OCT), lambda o, b: (b, o)),
            pl.BlockSpec((None, 1, SW), lambda o, b: (b * N_OCT + o, 0, 0)),
        ),
        out_shape=(
            jax.ShapeDtypeStruct((T_ALL, SSM_WIDTH), F32),
            jax.ShapeDtypeStruct((BATCH * N_OCT, 1, SW), F32),
        ),
        scratch_shapes=_SSM_SCRATCH,
        compiler_params=_cparams(2),
        name="ssm_prompt",
    )(u, *tables)


def _ssm_sample_body(u_ref, h0r_ref, h0i_ref, bbc_ref, ccc_ref, pw_ref, tab_ref, d_ref, _dst,
                     y_ref, hr_ref, hi_ref, f_ref, gt_ref, m_ref):
    _build_weights(bbc_ref, ccc_ref, pw_ref, f_ref, gt_ref, m_ref)
    B = DEC_BATCH
    xs = [u_ref[B * s:B * (s + 1), :] for s in range(CHUNK)]
    xb = jnp.concatenate(xs, axis=1).astype(BF16)
    sloc = jnp.dot(xb, f_ref[...], preferred_element_type=F32)
    h0r, h0i = h0r_ref[...], h0i_ref[...]
    hin = jnp.concatenate([h0r, h0i], axis=1).astype(BF16)
    y = (jnp.dot(xb, m_ref[...], preferred_element_type=F32)
         + _dot_nt(hin, gt_ref[...]))
    for t in range(CHUNK):
        yt = y[:, t * OCT:(t + 1) * OCT] + d_ref[...] * xs[t]
        y_ref[B * t:B * (t + 1), :] = _gelu_tanh(yt)
    lr = tab_ref[3 * SUBLANES:3 * SUBLANES + 1, :OCT_STATES]
    li = tab_ref[3 * SUBLANES:3 * SUBLANES + 1, OCT_STATES:]
    nr, ni = _cmul(lr, li, h0r, h0i)
    hr_ref[...] = nr + sloc[:, :OCT_STATES]
    hi_ref[...] = ni + sloc[:, OCT_STATES:]


def _ssm_sample(u, h0r, h0i, tables, y_act):
    col0 = POOL_WIDTH // OCT
    blk = T_PROMPT // T_SAMPLE
    st_spec = pl.BlockSpec((DEC_BATCH, OCT_STATES), lambda o: (0, o))
    return pl.pallas_call(
        _ssm_sample_body,
        grid=(N_OCT,),
        in_specs=[pl.BlockSpec((T_SAMPLE, OCT), lambda o: (blk, col0 + o)), st_spec, st_spec]
        + _ssm_weight_specs(1) + [pl.BlockSpec(memory_space=pl.ANY)],
        out_specs=(pl.BlockSpec((T_SAMPLE, OCT), lambda o: (blk, o)), st_spec, st_spec),
        out_shape=(
            jax.ShapeDtypeStruct((T_ALL, SSM_WIDTH), F32),
            jax.ShapeDtypeStruct((DEC_BATCH, SSM_GROUPS * SSM_STATE), F32),
            jax.ShapeDtypeStruct((DEC_BATCH, SSM_GROUPS * SSM_STATE), F32),
        ),
        scratch_shapes=_SSM_SCRATCH,
        input_output_aliases={8: 0},
        compiler_params=_cparams(1),
        name="ssm_sample",
    )(u, h0r, h0i, *tables, y_act)


PM_TM = 256
PM_PROMPT_BLOCKS = T_PROMPT // PM_TM


def _postmix_body(ya_ref, gp_ref, gs_ref, yp_ref, xp_ref, xs_ref, wa_ref, wb_ref, wo_ref,
                  gf_ref, wrh_ref, wrl_ref, br_ref, h_ref, tn_ref, rt_ref, cnt_out_ref, cnt_ref):
    i = pl.program_id(0)
    @pl.when(i == 0)
    def _():
        cnt_ref[...] = jnp.zeros_like(cnt_ref)
    ya = ya_ref[...].astype(BF16)
    a = jnp.dot(ya, wa_ref[...], preferred_element_type=F32)
    bg = jnp.dot(ya, wb_ref[...], preferred_element_type=F32)
    y_ssm = a * _sigmoid(bg)
    merged = (gp_ref[...].astype(F32) * yp_ref[...].astype(F32)
              + gs_ref[...].astype(F32) * y_ssm)
    x = jnp.where(i < PM_PROMPT_BLOCKS, xp_ref[...], xs_ref[...])
    h = x + jnp.dot(merged.astype(BF16), wo_ref[...], preferred_element_type=F32)
    h_ref[...] = h
    inv = lax.rsqrt(jnp.mean(h * h, axis=-1, keepdims=True) + EPS)
    tn = (h * inv) * gf_ref[...]
    tn_ref[...] = _pack_pairs(tn)
    t_hi = tn.astype(BF16)
    t_lo = (tn - t_hi.astype(F32)).astype(BF16)
    wrh = wrh_ref[...]
    logits = (jnp.dot(t_hi, wrh, preferred_element_type=F32)
              + jnp.dot(t_lo, wrh, preferred_element_type=F32)
              + jnp.dot(t_hi, wrl_ref[...], preferred_element_type=F32)) + br_ref[...]
    lane = lax.broadcasted_iota(I32, (PM_TM, ROUTE_LANES), 1)
    neg = jnp.float32(-jnp.inf)
    big = jnp.int32(1 << 20)
    is_g = lane < MOE_GROUPS
    gmax = jnp.max(jnp.where(is_g, logits, neg), axis=1, keepdims=True)
    g_idx = jnp.min(jnp.where(is_g & (logits == gmax), lane, big), axis=1, keepdims=True)
    g_den = jnp.sum(jnp.where(is_g, jnp.exp(logits - gmax), 0.0), axis=1, keepdims=True)
    g_val = 1.0 / g_den
    e_lane = lane - EXP_LANE0
    sel = (e_lane >= 0) & (e_lane < MOE_EXPERTS) & ((e_lane >> 3) == g_idx)
    m1 = jnp.max(jnp.where(sel, logits, neg), axis=1, keepdims=True)
    i1 = jnp.min(jnp.where(sel & (logits == m1), lane, big), axis=1, keepdims=True)
    sel2 = sel & (lane != i1)
    m2 = jnp.max(jnp.where(sel2, logits, neg), axis=1, keepdims=True)
    i2 = jnp.min(jnp.where(sel2 & (logits == m2), lane, big), axis=1, keepdims=True)
    e2 = jnp.exp(m2 - m1)
    w1 = g_val / (1.0 + e2)
    w2 = g_val * e2 / (1.0 + e2)
    oh1 = lane == i1
    oh2 = lane == i2
    oh = jnp.where(oh1 | oh2, 1.0, 0.0)
    rr = lax.broadcasted_iota(I32, (PM_TM, PM_TM), 0)
    cc = lax.broadcasted_iota(I32, (PM_TM, PM_TM), 1)
    tri = jnp.where(cc < rr, 1.0, 0.0).astype(BF16)
    base = cnt_ref[...] + jnp.dot(tri, oh.astype(BF16), preferred_element_type=F32)
    rank1 = jnp.sum(jnp.where(oh1, base, 0.0), axis=1, keepdims=True)
    rank2 = jnp.sum(jnp.where(oh2, base, 0.0), axis=1, keepdims=True)
    cnt_ref[...] = cnt_ref[...] + jnp.sum(oh, axis=0, keepdims=True)
    cnt_out_ref[...] = cnt_ref[...]
    rt = jnp.where(lane == 0, w1, 0.0)
    rt = jnp.where(lane == 1, w2, rt)
    rt = jnp.where(lane == 2, rank1, rt)
    rt = jnp.where(lane == 3, rank2, rt)
    rt = jnp.where(lane == 4, (i1 - EXP_LANE0).astype(F32), rt)
    rt = jnp.where(lane == 5, (i2 - EXP_LANE0).astype(F32), rt)
    rt_ref[...] = rt


def _postmix(y_act, gates, y_pool, xp, xs, wa, wb, wo, g_ffn, wr_hi, wr_lo, b_r):
    n = T_ALL // PM_TM
    npb = PM_PROMPT_BLOCKS
    const2 = lambda i: (0, 0)
    return pl.pallas_call(
        _postmix_body,
        grid=(n,),
        in_specs=[
            pl.BlockSpec((PM_TM, SSM_WIDTH), lambda i: (i, 0)),
            pl.BlockSpec((PM_TM, D_MODEL), lambda i: (i, 0)),
            pl.BlockSpec((PM_TM, D_MODEL), lambda i: (i, 1)),
            pl.BlockSpec((PM_TM, D_MODEL), lambda i: (i, 0)),
            pl.BlockSpec((PM_TM, D_MODEL), lambda i: (jnp.minimum(i, npb - 1), 0)),
            pl.BlockSpec((PM_TM, D_MODEL), lambda i: (jnp.maximum(i - npb, 0), 0)),
            pl.BlockSpec((SSM_WIDTH, D_MODEL), const2, pipeline_mode=pl.Buffered(1)),
            pl.BlockSpec((SSM_WIDTH, D_MODEL), const2, pipeline_mode=pl.Buffered(1)),
            pl.BlockSpec((D_MODEL, D_MODEL), const2, pipeline_mode=pl.Buffered(1)),
            pl.BlockSpec((1, D_MODEL), const2),
            pl.BlockSpec((D_MODEL, ROUTE_LANES), const2),
            pl.BlockSpec((D_MODEL, ROUTE_LANES), const2),
            pl.BlockSpec((1, ROUTE_LANES), const2),
        ],
        out_specs=(
            pl.BlockSpec((PM_TM, D_MODEL), lambda i: (i, 0)),
            pl.BlockSpec((PM_TM, D_PACK), lambda i: (i, 0)),
            pl.BlockSpec((PM_TM, ROUTE_LANES), lambda i: (i, 0)),
            pl.BlockSpec((1, ROUTE_LANES), const2),
        ),
        out_shape=(
            jax.ShapeDtypeStruct((T_ALL, D_MODEL), F32),
            jax.ShapeDtypeStruct((T_ALL, D_PACK), U32),
            jax.ShapeDtypeStruct((T_ALL, ROUTE_LANES), F32),
            jax.ShapeDtypeStruct((1, ROUTE_LANES), F32),
        ),
        scratch_shapes=[pltpu.VMEM((1, ROUTE_LANES), F32)],
        compiler_params=_cparams(1),
        name="postmix",
    )(y_act, gates, gates, y_pool, xp, xs, wa, wb, wo, g_ffn, wr_hi, wr_lo, b_r)


SC_CH = 16
W_SPLIT = 4


def _sc_workers():
    info = plsc.get_sparse_core_info()
    return info.num_cores, info.num_cores * info.num_subcores


def _sc_dispatch(tn, slots):
    n_cores, n_workers = _sc_workers()
    per_w = (T_ALL // SC_CH) // n_workers
    assert per_w * n_workers * SC_CH == T_ALL
    slots = slots.reshape(2, n_workers, per_w, SC_CH)

    @functools.partial(
        pl.kernel,
        mesh=plsc.VectorSubcoreMesh(core_axis_name="c", subcore_axis_name="s"),
        out_type=jax.ShapeDtypeStruct((N_SLOTS, D_PACK), U32),
        scratch_types=[pltpu.VMEM((2, per_w, SC_CH), I32), pltpu.VMEM((SC_CH, D_PACK), U32)],
    )
    def k(tn_hbm, slots_hbm, xs_hbm, idx_v, rows_v):
        wid = lax.axis_index("s") * n_cores + lax.axis_index("c")
        c0 = wid * per_w
        pltpu.sync_copy(slots_hbm.at[0, wid], idx_v.at[0])
        pltpu.sync_copy(slots_hbm.at[1, wid], idx_v.at[1])

        @pl.loop(0, per_w)
        def _(c):
            row0 = pl.multiple_of((c0 + c) * SC_CH, SC_CH)
            pltpu.sync_copy(tn_hbm.at[pl.ds(row0, SC_CH)], rows_v)
            pltpu.sync_copy(rows_v, xs_hbm.at[idx_v.at[0, c]])
            pltpu.sync_copy(rows_v, xs_hbm.at[idx_v.at[1, c]])

    return k(tn, slots)


def _sc_collect(ys, slots):
    n_cores, n_workers = _sc_workers()
    per_w = (N_ASSIGN // SC_CH) // n_workers
    assert per_w * n_workers * SC_CH == N_ASSIGN
    slots = slots.reshape(n_workers, per_w, SC_CH)

    @functools.partial(
        pl.kernel,
        mesh=plsc.VectorSubcoreMesh(core_axis_name="c", subcore_axis_name="s"),
        out_type=jax.ShapeDtypeStruct((N_ASSIGN, D_PACK), U32),
        scratch_types=[pltpu.VMEM((per_w, SC_CH), I32), pltpu.VMEM((SC_CH, D_PACK), U32)],
    )
    def k(ys_hbm, slots_hbm, out_hbm, idx_v, rows_v):
        wid = lax.axis_index("s") * n_cores + lax.axis_index("c")
        c0 = wid * per_w
        pltpu.sync_copy(slots_hbm.at[wid], idx_v)

        @pl.loop(0, per_w)
        def _(c):
            row0 = pl.multiple_of((c0 + c) * SC_CH, SC_CH)
            pltpu.sync_copy(ys_hbm.at[idx_v.at[c]], rows_v)
            pltpu.sync_copy(rows_v, out_hbm.at[pl.ds(row0, SC_CH)])

    return k(ys, slots)


def _expert_body(t0_ref, nt_ref, xs_hbm, *rest):
    w_refs = rest[:3 * W_SPLIT]
    ys_hbm, xb_ref, yb_ref, wgb_ref, wub_ref, wdb_ref, xsem, ysem = rest[3 * W_SPLIT:]
    e = pl.program_id(0)
    n = nt_ref[e]
    g0 = t0_ref[e]

    def rows(j):
        return pl.ds(pl.multiple_of((g0 + j) * TME, TME), TME)

    def x_copy(j, s):
        return pltpu.make_async_copy(xs_hbm.at[rows(j)], xb_ref.at[s], xsem.at[s])

    def y_copy(j, s):
        return pltpu.make_async_copy(yb_ref.at[s], ys_hbm.at[rows(j)], ysem.at[s])

    @pl.when(n > 0)
    def _():
        x_copy(0, 0).start(priority=1)
        for m, dst in enumerate((wgb_ref, wub_ref, wdb_ref)):
            rb = dst.shape[0] // W_SPLIT
            for p in range(W_SPLIT):
                dst[p * rb:(p + 1) * rb, :] = w_refs[m * W_SPLIT + p][...].astype(BF16)

        def tile(j, c):
            s = j & 1
            x_copy(j, s).wait()
            @pl.when(j + 1 < n)
            def _():
                x_copy(j + 1, 1 - s).start(priority=1)
            @pl.when(j >= 2)
            def _():
                y_copy(j - 2, s).wait()
            x = _unpack_pairs(xb_ref[s], BF16)
            hg = jnp.dot(x, wgb_ref[...], preferred_element_type=F32)
            hu = jnp.dot(x, wub_ref[...], preferred_element_type=F32)
            act = (hg * _sigmoid(hg)) * hu
            yb_ref[s] = _pack_pairs(jnp.dot(act.astype(BF16), wdb_ref[...], preferred_element_type=F32))
            y_copy(j, s).start()
            return c
        lax.fori_loop(0, n, tile, 0)

        @pl.when(n >= 2)
        def _():
            y_copy(n - 2, n & 1).wait()
        y_copy(n - 1, (n - 1) & 1).wait()


def _experts(tile0, tiles, xs, w_eg, w_eu, w_ed):
    def w_specs(rows, cols):
        return [pl.BlockSpec((None, rows // W_SPLIT, cols), functools.partial(lambda p, e, t0, nt: (e, p, 0), p))
                for p in range(W_SPLIT)]
    grid_spec = pltpu.PrefetchScalarGridSpec(
        num_scalar_prefetch=2,
        grid=(MOE_EXPERTS,),
        in_specs=[pl.BlockSpec(memory_space=pl.ANY)]
        + w_specs(D_MODEL, MOE_FF) + w_specs(D_MODEL, MOE_FF) + w_specs(MOE_FF, D_MODEL),
        out_specs=pl.BlockSpec(memory_space=pl.ANY),
        scratch_shapes=[
            pltpu.VMEM((2, TME, D_PACK), U32),
            pltpu.VMEM((2, TME, D_PACK), U32),
            pltpu.VMEM((D_MODEL, MOE_FF), BF16),
            pltpu.VMEM((D_MODEL, MOE_FF), BF16),
            pltpu.VMEM((MOE_FF, D_MODEL), BF16),
            pltpu.SemaphoreType.DMA((2,)),
            pltpu.SemaphoreType.DMA((2,)),
        ],
    )
    return pl.pallas_call(
        _expert_body,
        grid_spec=grid_spec,
        out_shape=jax.ShapeDtypeStruct((N_SLOTS, D_PACK), U32),
        compiler_params=_cparams(1),
        name="experts",
    )(tile0, tiles, xs, *([w_eg] * W_SPLIT + [w_eu] * W_SPLIT + [w_ed] * W_SPLIT))


FN_TM = 256
FN_PROMPT_BLOCKS = T_PROMPT // FN_TM


def _final_body(h_ref, y0_ref, y1_ref, rt_ref, g_ref, op_ref, os_ref):
    i = pl.program_id(0)
    rt = rt_ref[...]
    z = (h_ref[...] + rt[:, 0:1] * _unpack_pairs(y0_ref[...], F32)
         + rt[:, 1:2] * _unpack_pairs(y1_ref[...], F32))
    inv = lax.rsqrt(jnp.mean(z * z, axis=-1, keepdims=True) + EPS)
    out = (z * inv) * g_ref[...]
    @pl.when(i < FN_PROMPT_BLOCKS)
    def _():
        op_ref[...] = out
    @pl.when(i >= FN_PROMPT_BLOCKS)
    def _():
        os_ref[...] = out


def _final(h, y, route, g_final):
    n = T_ALL // FN_TM
    npb = FN_PROMPT_BLOCKS
    yoff = T_ALL // FN_TM
    return pl.pallas_call(
        _final_body,
        grid=(n,),
        in_specs=[
            pl.BlockSpec((FN_TM, D_MODEL), lambda i: (i, 0)),
            pl.BlockSpec((FN_TM, D_PACK), lambda i: (i, 0)),
            pl.BlockSpec((FN_TM, D_PACK), lambda i: (yoff + i, 0)),
            pl.BlockSpec((FN_TM, ROUTE_LANES), lambda i: (i, 0)),
            pl.BlockSpec((1, D_MODEL), lambda i: (0, 0)),
        ],
        out_specs=(
            pl.BlockSpec((FN_TM, D_MODEL), lambda i: (jnp.minimum(i, npb - 1), 0)),
            pl.BlockSpec((FN_TM, D_MODEL), lambda i: (jnp.maximum(i - npb, 0), 0)),
        ),
        out_shape=(
            jax.ShapeDtypeStruct((T_PROMPT, D_MODEL), F32),
            jax.ShapeDtypeStruct((T_SAMPLE, D_MODEL), F32),
        ),
        compiler_params=_cparams(1),
        name="final",
    )(h, y, y, route, g_final)


def _dispatch_plan(route, cnt):
    counts = cnt[0, EXP_LANE0:EXP_LANE0 + MOE_EXPERTS].astype(I32)
    tiles = (counts + (TME - 1)) // TME
    cumt = jnp.cumsum(tiles)
    pad_off = (cumt - tiles) * TME
    rank = route[:, 2:4].astype(I32)
    eid = route[:, 4:6].astype(I32)
    onehot = eid[..., None] == jnp.arange(MOE_EXPERTS, dtype=I32)
    slots = (rank + jnp.sum(jnp.where(onehot, pad_off, 0), axis=-1)).T
    return slots, cumt - tiles, tiles


def kernel(x_prompt, x_sample, state_pool, state_ssm_re, state_ssm_im, g_mix, w_in, w_pool,
           pool_scale, ssm_a_re, ssm_a_im, ssm_log_dt, ssm_b_re, ssm_b_im, ssm_c_re, ssm_c_im,
           ssm_d, w_glu_a, w_glu_b, w_out, g_ffn, w_router_group, b_router_group,
           w_router_expert, b_router_expert, w_exp_gate, w_exp_up, w_exp_down, g_final):
    l = 0
    xp = x_prompt.reshape(T_PROMPT, D_MODEL)
    xs = x_sample.transpose(1, 0, 2).reshape(T_SAMPLE, D_MODEL)
    w_in_bf = w_in[l].astype(BF16)
    w_pool_bf = w_pool[l].astype(BF16)
    wa_bf = w_glu_a[l].astype(BF16)
    wb_bf = w_glu_b[l].astype(BF16)
    wo_bf = w_out[l].astype(BF16)
    g_mix2 = g_mix[l].reshape(1, D_MODEL)
    scale2 = pool_scale[l].reshape(1, D_MODEL)

    u, gates = _inproj(xp, g_mix2, w_in_bf, 0)
    u, gates = _inproj(xs, g_mix2, w_in_bf, T_PROMPT // IN_TM, dst=(u, gates))

    y_pool, pool_tail = _pool_prompt(u, w_pool_bf, scale2)
    hist_tm = state_pool[l].transpose(1, 0, 2)
    y_pool = _pool_sample(u, hist_tm, w_pool_bf, scale2, y_pool)

    tables = _ssm_tables(ssm_a_re[l], ssm_a_im[l], ssm_log_dt[l], ssm_b_re[l], ssm_b_im[l],
                         ssm_c_re[l], ssm_c_im[l], ssm_d[l])
    y_act, h_prompt = _ssm_prompt(u, tables)
    h0r = state_ssm_re[l].reshape(DEC_BATCH, SSM_GROUPS * SSM_STATE)
    h0i = state_ssm_im[l].reshape(DEC_BATCH, SSM_GROUPS * SSM_STATE)
    y_act, hs_re, hs_im = _ssm_sample(u, h0r, h0i, tables, y_act)

    w_r = jnp.zeros((D_MODEL, ROUTE_LANES), F32)
    w_r = w_r.at[:, :MOE_GROUPS].set(w_router_group[l])
    w_r = w_r.at[:, EXP_LANE0:EXP_LANE0 + MOE_EXPERTS].set(w_router_expert[l])
    wr_hi = w_r.astype(BF16)
    wr_lo = (w_r - wr_hi.astype(F32)).astype(BF16)
    b_r = jnp.zeros((1, ROUTE_LANES), F32)
    b_r = b_r.at[0, :MOE_GROUPS].set(b_router_group[l])
    b_r = b_r.at[0, EXP_LANE0:EXP_LANE0 + MOE_EXPERTS].set(b_router_expert[l])

    h, tn, route, cnt = _postmix(y_act, gates, y_pool, xp, xs, wa_bf, wb_bf, wo_bf,
                                 g_ffn[l].reshape(1, D_MODEL), wr_hi, wr_lo, b_r)
    slots, tile0, tiles = _dispatch_plan(route, cnt)
    xs_sorted = _sc_dispatch(tn, slots)
    ys_sorted = _experts(tile0, tiles, xs_sorted, w_exp_gate[l], w_exp_up[l], w_exp_down[l])
    y = _sc_collect(ys_sorted, slots)
    yp, ys = _final(h, y, route, g_final.reshape(1, D_MODEL))

    y_prompt = yp.reshape(BATCH, SEQ, D_MODEL)
    y_sample = ys.reshape(DEC_SEQ, DEC_BATCH, D_MODEL).transpose(1, 0, 2)
    new_pool_prompt = pool_tail[:, HIST - POOL_BUF:, :][None]
    us = u[T_PROMPT:, :POOL_WIDTH].reshape(DEC_SEQ, DEC_BATCH, POOL_WIDTH).transpose(1, 0, 2)
    new_pool_sample = jnp.concatenate([state_pool[l][:, DEC_SEQ:, :], us], axis=1)[None]
    hp = h_prompt.reshape(BATCH, N_OCT, 2, OCT_GROUPS, SSM_STATE).transpose(2, 0, 1, 3, 4)
    hp = hp.reshape(2, BATCH, SSM_GROUPS, SSM_STATE)
    shp = (1, DEC_BATCH, SSM_GROUPS, SSM_STATE)
    return (y_prompt, y_sample, new_pool_prompt, hp[0][None], hp[1][None], new_pool_sample,
            hs_re.reshape(shp), hs_im.reshape(shp))
```

```python
import functools
import math

import jax
import jax.numpy as jnp
from jax import lax
from jax.experimental import pallas as pl
from jax.experimental.pallas import tpu as pltpu
from jax.experimental.pallas import tpu_sc as plsc

F32 = jnp.float32
BF16 = jnp.bfloat16
I32 = jnp.int32
U32 = jnp.uint32

D_MODEL = 2048
BATCH = 4
SEQ = 2048
DEC_BATCH = 128
DEC_SEQ = 8
PAST_LEN = 16384
POOL_WIDTH = D_MODEL // 2
POOL_WINDOWS = (2, 4, 8, 16)
POOL_GROUPS = len(POOL_WINDOWS)
POOL_GROUP_CH = POOL_WIDTH // POOL_GROUPS
POOL_OUT_CH = D_MODEL // POOL_GROUPS
POOL_BUF = max(POOL_WINDOWS) - 1
SSM_WIDTH = D_MODEL // 2
SSM_GROUP_CH = 16
SSM_GROUPS = SSM_WIDTH // SSM_GROUP_CH
SSM_STATE = 64
IN_WIDTH = POOL_WIDTH + SSM_WIDTH + 2 * D_MODEL
D_PACK = D_MODEL // 2
MOE_GROUPS = 4
MOE_EPG = 8
MOE_EXPERTS = MOE_GROUPS * MOE_EPG
MOE_FF = D_MODEL // 4
EPS = 1e-6

T_PROMPT = BATCH * SEQ
T_SAMPLE = DEC_BATCH * DEC_SEQ
T_ALL = T_PROMPT + T_SAMPLE

LANES = 128
SUBLANES = 8
VMEM_LIMIT = 56 * 1024 * 1024

CHUNK = 8
OCT = LANES
N_OCT = SSM_WIDTH // OCT
OCT_GROUPS = OCT // SSM_GROUP_CH
OCT_STATES = OCT_GROUPS * SSM_STATE
CW = CHUNK * OCT
SW = 2 * OCT_STATES

ROUTE_LANES = LANES
EXP_LANE0 = MOE_GROUPS
N_ASSIGN = 2 * T_ALL
TME = 256
N_ITEMS_MAX = N_ASSIGN // TME + MOE_EXPERTS
N_SLOTS = N_ITEMS_MAX * TME


def _cparams(n_axes):
    return pltpu.CompilerParams(dimension_semantics=("arbitrary",) * n_axes,
                                vmem_limit_bytes=VMEM_LIMIT)


def _sigmoid(x):
    return 1.0 / (1.0 + jnp.exp(-x))


def _pack_pairs(x):
    c = x.shape[1] // 2
    hi = lax.bitcast_convert_type(x[:, :c].astype(BF16).astype(F32), U32)
    lo = lax.bitcast_convert_type(x[:, c:].astype(BF16).astype(F32), U32)
    return hi | (lo >> 16)


def _unpack_pairs(u, dtype):
    hi = lax.bitcast_convert_type(u & jnp.uint32(0xFFFF0000), F32)
    lo = lax.bitcast_convert_type(u << 16, F32)
    return jnp.concatenate([hi, lo], axis=1).astype(dtype)


def _gelu_tanh(x):
    c = math.sqrt(2.0 / math.pi)
    return 0.5 * x * (1.0 + jnp.tanh(c * (x + 0.044715 * (x * x * x))))


IN_TM = 1024
IN_TN = 1024
U_WIDTH = POOL_WIDTH + SSM_WIDTH
GATE_WIDTH = 2 * D_MODEL
IN_U_STEPS = U_WIDTH // IN_TN


def _inproj_body(x_ref, g_ref, w_ref, *rest):
    u_ref, gate_ref, xn_ref = rest[-3:]
    j = pl.program_id(1)
    @pl.when(j == 0)
    def _():
        x = x_ref[...]
        inv = lax.rsqrt(jnp.mean(x * x, axis=-1, keepdims=True) + EPS)
        xn_ref[...] = ((x * inv) * g_ref[...]).astype(BF16)
    acc = jnp.dot(xn_ref[...], w_ref[...], preferred_element_type=F32)
    @pl.when(j < IN_U_STEPS)
    def _():
        u_ref[...] = acc
    @pl.when(j >= IN_U_STEPS)
    def _():
        gate_ref[...] = _sigmoid(acc).astype(BF16)


def _inproj(x, g, w_bf, row_block0, dst=None):
    n_i = x.shape[0] // IN_TM
    in_specs = [
        pl.BlockSpec((IN_TM, D_MODEL), lambda i, j: (i, 0)),
        pl.BlockSpec((1, D_MODEL), lambda i, j: (0, 0)),
        pl.BlockSpec((D_MODEL, IN_TN), lambda i, j: (0, j)),
    ]
    args = [x, g, w_bf]
    aliases = {}
    if dst is not None:
        in_specs += [pl.BlockSpec(memory_space=pl.ANY)] * 2
        args += list(dst)
        aliases = {3: 0, 4: 1}
    return pl.pallas_call(
        _inproj_body,
        grid=(n_i, IN_WIDTH // IN_TN),
        in_specs=in_specs,
        out_specs=(
            pl.BlockSpec((IN_TM, IN_TN), lambda i, j: (i + row_block0, jnp.minimum(j, IN_U_STEPS - 1))),
            pl.BlockSpec((IN_TM, IN_TN), lambda i, j: (i + row_block0, jnp.maximum(j - IN_U_STEPS, 0))),
        ),
        out_shape=(
            jax.ShapeDtypeStruct((T_ALL, U_WIDTH), F32),
            jax.ShapeDtypeStruct((T_ALL, GATE_WIDTH), BF16),
        ),
        scratch_shapes=[pltpu.VMEM((IN_TM, D_MODEL), BF16)],
        input_output_aliases=aliases,
        compiler_params=_cparams(2),
        name="inproj",
    )(*args)


PP_TM = 512
HIST = 16


def _pool_project(pooled_g, g, w_ref, sc_ref, o_ref):
    y = jnp.dot(pooled_g.astype(BF16), w_ref[g], preferred_element_type=F32)
    lo, hi = g * POOL_OUT_CH, (g + 1) * POOL_OUT_CH
    o_ref[:, lo:hi] = (y * sc_ref[:, lo:hi]).astype(o_ref.dtype)


def _pool_prompt_body(u_ref, w_ref, sc_ref, o_ref, tail_ref, hist_ref):
    i = pl.program_id(1)
    @pl.when(i == 0)
    def _():
        hist_ref[...] = jnp.zeros_like(hist_ref)
    u = u_ref[...]
    ext = jnp.concatenate([hist_ref[...], u], axis=0)
    hist_ref[...] = u[PP_TM - HIST:, :]
    tail_ref[...] = u[PP_TM - HIST:, :]
    pos = i * PP_TM + lax.broadcasted_iota(I32, (PP_TM, 1), 0)
    for g, w in enumerate(POOL_WINDOWS):
        lo, hi = g * POOL_GROUP_CH, (g + 1) * POOL_GROUP_CH
        s = ext[:, lo:hi]
        d = 1
        while d < w:
            s = s + pltpu.roll(s, d, axis=0)
            d *= 2
        cnt = jnp.minimum(w, pos + 1).astype(F32)
        pooled = s[HIST:, :] / cnt - u[:, lo:hi]
        _pool_project(pooled, g, w_ref, sc_ref, o_ref)


def _pool_prompt(u, w_pool_bf, pool_scale):
    n_i = SEQ // PP_TM
    return pl.pallas_call(
        _pool_prompt_body,
        grid=(BATCH, n_i),
        in_specs=[
            pl.BlockSpec((PP_TM, POOL_WIDTH), lambda b, i: (b * n_i + i, 0)),
            pl.BlockSpec((POOL_GROUPS, POOL_GROUP_CH, POOL_OUT_CH), lambda b, i: (0, 0, 0)),
            pl.BlockSpec((1, D_MODEL), lambda b, i: (0, 0)),
        ],
        out_specs=(
            pl.BlockSpec((PP_TM, D_MODEL), lambda b, i: (b * n_i + i, 0)),
            pl.BlockSpec((None, HIST, POOL_WIDTH), lambda b, i: (b, 0, 0)),
        ),
        out_shape=(
            jax.ShapeDtypeStruct((T_ALL, D_MODEL), BF16),
            jax.ShapeDtypeStruct((BATCH, HIST, POOL_WIDTH), F32),
        ),
        scratch_shapes=[pltpu.VMEM((HIST, POOL_WIDTH), F32)],
        compiler_params=_cparams(2),
        name="pool_prompt",
    )(u, w_pool_bf, pool_scale)


def _pool_sample_body(u_ref, hist_ref, w_ref, sc_ref, _dst, o_ref):
    rows = [hist_ref[k] for k in range(POOL_BUF)]
    rows += [u_ref[DEC_BATCH * t:DEC_BATCH * (t + 1), :] for t in range(DEC_SEQ)]
    n = len(rows)
    for g, w in enumerate(POOL_WINDOWS):
        lo, hi = g * POOL_GROUP_CH, (g + 1) * POOL_GROUP_CH
        f = [r[:, lo:hi] for r in rows]
        cur = f
        d = 1
        while d < w:
            cur = [cur[k] + cur[k - d] if k - d >= 0 else cur[k] for k in range(n)]
            d *= 2
        pooled = jnp.concatenate(
            [cur[POOL_BUF + t] / float(w) - f[POOL_BUF + t] for t in range(DEC_SEQ)], axis=0)
        _pool_project(pooled, g, w_ref, sc_ref, o_ref)


def _pool_sample(u, hist_tm, w_pool_bf, pool_scale, y_pool):
    blk = T_PROMPT // T_SAMPLE
    return pl.pallas_call(
        _pool_sample_body,
        grid=(1,),
        in_specs=[
            pl.BlockSpec((T_SAMPLE, POOL_WIDTH), lambda i: (blk, 0)),
            pl.BlockSpec((POOL_BUF, DEC_BATCH, POOL_WIDTH), lambda i: (0, 0, 0)),
            pl.BlockSpec((POOL_GROUPS, POOL_GROUP_CH, POOL_OUT_CH), lambda i: (0, 0, 0)),
            pl.BlockSpec((1, D_MODEL), lambda i: (0, 0)),
            pl.BlockSpec(memory_space=pl.ANY),
        ],
        out_specs=pl.BlockSpec((T_SAMPLE, D_MODEL), lambda i: (blk, 0)),
        out_shape=jax.ShapeDtypeStruct((T_ALL, D_MODEL), BF16),
        input_output_aliases={4: 0},
        compiler_params=_cparams(1),
        name="pool_sample",
    )(u, hist_tm, w_pool_bf, pool_scale, y_pool)


def _ssm_tables(a_re, a_im, log_dt, b_re, b_im, c_re, c_im, d_skip):
    dt = jnp.exp(log_dt)[:, None]
    lr, li = a_re, a_im
    ab_re = jnp.exp(lr * dt) * jnp.cos(li * dt)
    ab_im = jnp.exp(lr * dt) * jnp.sin(li * dt)
    den = lr * lr + li * li
    nr, ni = ab_re - 1.0, ab_im
    q_re = (nr * lr + ni * li) / den
    q_im = (ni * lr - nr * li) / den
    bb_re = q_re[..., None] * b_re - q_im[..., None] * b_im
    bb_im = q_re[..., None] * b_im + q_im[..., None] * b_re

    def lam_rows(ks):
        k = jnp.asarray(ks, F32)[:, None, None]
        m = jnp.exp(k * lr * dt)
        re = (m * jnp.cos(k * li * dt)).reshape(len(ks), N_OCT, OCT_STATES)
        im = (m * jnp.sin(k * li * dt)).reshape(len(ks), N_OCT, OCT_STATES)
        return jnp.concatenate([re, im], axis=-1).transpose(1, 0, 2)

    def compact(re, im):
        v = jnp.concatenate([re, im], axis=-1)
        return v.reshape(N_OCT, OCT, 2 * SSM_STATE)

    bbc = compact(jnp.swapaxes(bb_re, 1, 2), jnp.swapaxes(bb_im, 1, 2))
    ccc = compact(c_re, c_im)
    pw = lam_rows(list(range(2 * SUBLANES)))
    r = jnp.arange(SUBLANES)[None, :, None]
    parts = [jnp.where(r >= dd, lam_rows([CHUNK * dd]), 0.0) for dd in (1, 2, 4)]
    parts.append(lam_rows([CHUNK * kk for kk in range(1, SUBLANES + 1)]))
    tab = jnp.concatenate(parts, axis=1)
    dsk = d_skip.reshape(N_OCT, 1, OCT)
    return bbc, ccc, pw, tab, dsk


def _split_bf16(x):
    hi = x.astype(BF16)
    return hi, (x - hi.astype(F32)).astype(BF16)


def _dot_nt(a, b):
    return lax.dot_general(a, b, (((1,), (1,)), ((), ())), preferred_element_type=F32)


def _build_weights(bbc_ref, ccc_ref, pw_ref, f_ref, gt_ref, m_ref):
    row_gi = lax.broadcasted_iota(I32, (OCT, 1), 0) >> 4
    col = lax.broadcasted_iota(I32, (1, SW), 1)
    col_gi = (col >> 6) & 7
    src = ((col >> 9) << 6) | (col & 63)
    k128 = lax.broadcasted_iota(I32, (2 * SSM_STATE, 1), 0)
    spread = jnp.where(k128 == src, 1.0, 0.0).astype(BF16)
    diag = row_gi == col_gi

    def expand(c_ref):
        hi, lo = _split_bf16(c_ref[...])
        d = (jnp.dot(hi, spread, preferred_element_type=F32)
             + jnp.dot(lo, spread, preferred_element_type=F32))
        d = jnp.where(diag, d, 0.0)
        return d[:, :OCT_STATES], d[:, OCT_STATES:]

    br, bi = expand(bbc_ref)
    cr, ci = expand(ccc_ref)
    chi_r, clo_r = _split_bf16(cr)
    chi_i, clo_i = _split_bf16(ci)

    def lam(k):
        return pw_ref[k:k + 1, :OCT_STATES], pw_ref[k:k + 1, OCT_STATES:]

    def dot3(a, bhi, blo):
        ahi, alo = _split_bf16(a)
        return _dot_nt(ahi, bhi) + _dot_nt(alo, bhi) + _dot_nt(ahi, blo)

    lags = []
    for k in range(CHUNK):
        pr, pi_ = lam(k)
        fr, fi = _cmul(br, bi, pr, pi_)
        s = CHUNK - 1 - k
        f_ref[s * OCT:(s + 1) * OCT, :] = jnp.concatenate([fr, fi], axis=1).astype(BF16)
        lags.append((dot3(fr, chi_r, clo_r) - dot3(fi, chi_i, clo_i)).astype(BF16))
        pr, pi_ = lam(k + 1)
        gr, gi = _cmul(cr, ci, pr, pi_)
        gt_ref[k * OCT:(k + 1) * OCT, :] = jnp.concatenate([gr, -gi], axis=1).astype(BF16)
    zero = jnp.zeros((OCT, OCT), BF16)
    for s in range(CHUNK):
        for t in range(CHUNK):
            m_ref[s * OCT:(s + 1) * OCT, t * OCT:(t + 1) * OCT] = lags[t - s] if t >= s else zero


def _cmul(ar, ai, br, bi):
    return ar * br - ai * bi, ar * bi + ai * br


def _chunk_scan(sloc, tab_ref):
    R = sloc.shape[0]
    nb = R // SUBLANES
    sr, si = sloc[:, :OCT_STATES], sloc[:, OCT_STATES:]
    rowi = lax.broadcasted_iota(I32, (R, 1), 0)
    tr = jnp.where(rowi == 0, 0.0, pltpu.roll(sr, 1, axis=0))
    ti = jnp.where(rowi == 0, 0.0, pltpu.roll(si, 1, axis=0))
    for lvl, d in enumerate((1, 2, 4)):
        mr = tab_ref[lvl * SUBLANES:(lvl + 1) * SUBLANES, :OCT_STATES]
        mi = tab_ref[lvl * SUBLANES:(lvl + 1) * SUBLANES, OCT_STATES:]
        mr = jnp.concatenate([mr] * nb, axis=0)
        mi = jnp.concatenate([mi] * nb, axis=0)
        pr, pi_ = _cmul(mr, mi, pltpu.roll(tr, d, axis=0), pltpu.roll(ti, d, axis=0))
        tr, ti = tr + pr, ti + pi_
    pwr = tab_ref[3 * SUBLANES:4 * SUBLANES, :OCT_STATES]
    pwi = tab_ref[3 * SUBLANES:4 * SUBLANES, OCT_STATES:]
    cr = jnp.zeros((1, OCT_STATES), F32)
    ci = jnp.zeros((1, OCT_STATES), F32)
    out_r, out_i = [], []
    for k in range(nb):
        ar = tr[k * SUBLANES:(k + 1) * SUBLANES, :]
        ai = ti[k * SUBLANES:(k + 1) * SUBLANES, :]
        pr, pi_ = _cmul(pwr, pwi, jnp.broadcast_to(cr, ar.shape), jnp.broadcast_to(ci, ai.shape))
        hr, hi = ar + pr, ai + pi_
        out_r.append(hr)
        out_i.append(hi)
        cr, ci = hr[SUBLANES - 1:, :], hi[SUBLANES - 1:, :]
    hin = jnp.concatenate([jnp.concatenate(out_r, axis=0), jnp.concatenate(out_i, axis=0)], axis=1)
    lr, li = pwr[0:1, :], pwi[0:1, :]
    fr, fi = _cmul(lr, li, cr, ci)
    fin = jnp.concatenate([fr + sr[R - 1:, :], fi + si[R - 1:, :]], axis=1)
    return hin, fin


def _ssm_prompt_body(u_ref, bbc_ref, ccc_ref, pw_ref, tab_ref, d_ref, y_ref, hout_ref,
                     f_ref, gt_ref, m_ref):
    @pl.when(pl.program_id(1) == 0)
    def _():
        _build_weights(bbc_ref, ccc_ref, pw_ref, f_ref, gt_ref, m_ref)
    R = SEQ // CHUNK
    xs = [u_ref[pl.ds(s, R, stride=CHUNK), :] for s in range(CHUNK)]
    xb = jnp.concatenate(xs, axis=1).astype(BF16)
    sloc = jnp.dot(xb, f_ref[...], preferred_element_type=F32)
    hin, fin = _chunk_scan(sloc, tab_ref)
    y = (jnp.dot(xb, m_ref[...], preferred_element_type=F32)
         + _dot_nt(hin.astype(BF16), gt_ref[...]))
    for t in range(CHUNK):
        yt = y[:, t * OCT:(t + 1) * OCT] + d_ref[...] * xs[t]
        y_ref[pl.ds(t, R, stride=CHUNK), :] = _gelu_tanh(yt)
    hout_ref[...] = fin


def _ssm_weight_specs(n_axes):
    if n_axes == 2:
        im3 = lambda o, b: (o, 0, 0)
    else:
        im3 = lambda o: (o, 0, 0)
    return [
        pl.BlockSpec((None, OCT, 2 * SSM_STATE), im3),
        pl.BlockSpec((None, OCT, 2 * SSM_STATE), im3),
        pl.BlockSpec((None, 2 * SUBLANES, SW), im3),
        pl.BlockSpec((None, 4 * SUBLANES, SW), im3),
        pl.BlockSpec((None, 1, OCT), im3),
    ]


_SSM_SCRATCH = [pltpu.VMEM((CW, SW), BF16), pltpu.VMEM((CW, SW), BF16), pltpu.VMEM((CW, CW), BF16)]


def _ssm_prompt(u, tables):
    col0 = POOL_WIDTH // OCT
    return pl.pallas_call(
        _ssm_prompt_body,
        grid=(N_OCT, BATCH),
        in_specs=[pl.BlockSpec((SEQ, OCT), lambda o, b: (b, col0 + o))] + _ssm_weight_specs(2),
        out_specs=(
            pl.BlockSpec((SEQ, OCT), lambda o, b: (b, o)),
            pl.BlockSpec((None, 1, SW), lambda o, b: (b * N_OCT + o, 0, 0)),
        ),
        out_shape=(
            jax.ShapeDtypeStruct((T_ALL, SSM_WIDTH), F32),
            jax.ShapeDtypeStruct((BATCH * N_OCT, 1, SW), F32),
        ),
        scratch_shapes=_SSM_SCRATCH,
        compiler_params=_cparams(2),
        name="ssm_prompt",
    )(u, *tables)


def _ssm_sample_body(u_ref, h0r_ref, h0i_ref, bbc_ref, ccc_ref, pw_ref, tab_ref, d_ref, _dst,
                     y_ref, hr_ref, hi_ref, f_ref, gt_ref, m_ref):
    _build_weights(bbc_ref, ccc_ref, pw_ref, f_ref, gt_ref, m_ref)
    B = DEC_BATCH
    xs = [u_ref[B * s:B * (s + 1), :] for s in range(CHUNK)]
    xb = jnp.concatenate(xs, axis=1).astype(BF16)
    sloc = jnp.dot(xb, f_ref[...], preferred_element_type=F32)
    h0r, h0i = h0r_ref[...], h0i_ref[...]
    hin = jnp.concatenate([h0r, h0i], axis=1).astype(BF16)
    y = (jnp.dot(xb, m_ref[...], preferred_element_type=F32)
         + _dot_nt(hin, gt_ref[...]))
    for t in range(CHUNK):
        yt = y[:, t * OCT:(t + 1) * OCT] + d_ref[...] * xs[t]
        y_ref[B * t:B * (t + 1), :] = _gelu_tanh(yt)
    lr = tab_ref[3 * SUBLANES:3 * SUBLANES + 1, :OCT_STATES]
    li = tab_ref[3 * SUBLANES:3 * SUBLANES + 1, OCT_STATES:]
    nr, ni = _cmul(lr, li, h0r, h0i)
    hr_ref[...] = nr + sloc[:, :OCT_STATES]
    hi_ref[...] = ni + sloc[:, OCT_STATES:]


def _ssm_sample(u, h0r, h0i, tables, y_act):
    col0 = POOL_WIDTH // OCT
    blk = T_PROMPT // T_SAMPLE
    st_spec = pl.BlockSpec((DEC_BATCH, OCT_STATES), lambda o: (0, o))
    return pl.pallas_call(
        _ssm_sample_body,
        grid=(N_OCT,),
        in_specs=[pl.BlockSpec((T_SAMPLE, OCT), lambda o: (blk, col0 + o)), st_spec, st_spec]
        + _ssm_weight_specs(1) + [pl.BlockSpec(memory_space=pl.ANY)],
        out_specs=(pl.BlockSpec((T_SAMPLE, OCT), lambda o: (blk, o)), st_spec, st_spec),
        out_shape=(
            jax.ShapeDtypeStruct((T_ALL, SSM_WIDTH), F32),
            jax.ShapeDtypeStruct((DEC_BATCH, SSM_GROUPS * SSM_STATE), F32),
            jax.ShapeDtypeStruct((DEC_BATCH, SSM_GROUPS * SSM_STATE), F32),
        ),
        scratch_shapes=_SSM_SCRATCH,
        input_output_aliases={8: 0},
        compiler_params=_cparams(1),
        name="ssm_sample",
    )(u, h0r, h0i, *tables, y_act)


PM_TM = 256
PM_PROMPT_BLOCKS = T_PROMPT // PM_TM


def _postmix_body(ya_ref, gp_ref, gs_ref, yp_ref, xp_ref, xs_ref, wa_ref, wb_ref, wo_ref,
                  gf_ref, wrh_ref, wrl_ref, br_ref, h_ref, tn_ref, rt_ref, cnt_out_ref, cnt_ref):
    i = pl.program_id(0)
    @pl.when(i == 0)
    def _():
        cnt_ref[...] = jnp.zeros_like(cnt_ref)
    ya = ya_ref[...].astype(BF16)
    a = jnp.dot(ya, wa_ref[...], preferred_element_type=F32)
    bg = jnp.dot(ya, wb_ref[...], preferred_element_type=F32)
    y_ssm = a * _sigmoid(bg)
    merged = (gp_ref[...].astype(F32) * yp_ref[...].astype(F32)
              + gs_ref[...].astype(F32) * y_ssm)
    x = jnp.where(i < PM_PROMPT_BLOCKS, xp_ref[...], xs_ref[...])
    h = x + jnp.dot(merged.astype(BF16), wo_ref[...], preferred_element_type=F32)
    h_ref[...] = h
    inv = lax.rsqrt(jnp.mean(h * h, axis=-1, keepdims=True) + EPS)
    tn = (h * inv) * gf_ref[...]
    tn_ref[...] = _pack_pairs(tn)
    t_hi = tn.astype(BF16)
    t_lo = (tn - t_hi.astype(F32)).astype(BF16)
    wrh = wrh_ref[...]
    logits = (jnp.dot(t_hi, wrh, preferred_element_type=F32)
              + jnp.dot(t_lo, wrh, preferred_element_type=F32)
              + jnp.dot(t_hi, wrl_ref[...], preferred_element_type=F32)) + br_ref[...]
    lane = lax.broadcasted_iota(I32, (PM_TM, ROUTE_LANES), 1)
    neg = jnp.float32(-jnp.inf)
    big = jnp.int32(1 << 20)
    is_g = lane < MOE_GROUPS
    gmax = jnp.max(jnp.where(is_g, logits, neg), axis=1, keepdims=True)
    g_idx = jnp.min(jnp.where(is_g & (logits == gmax), lane, big), axis=1, keepdims=True)
    g_den = jnp.sum(jnp.where(is_g, jnp.exp(logits - gmax), 0.0), axis=1, keepdims=True)
    g_val = 1.0 / g_den
    e_lane = lane - EXP_LANE0
    sel = (e_lane >= 0) & (e_lane < MOE_EXPERTS) & ((e_lane >> 3) == g_idx)
    m1 = jnp.max(jnp.where(sel, logits, neg), axis=1, keepdims=True)
    i1 = jnp.min(jnp.where(sel & (logits == m1), lane, big), axis=1, keepdims=True)
    sel2 = sel & (lane != i1)
    m2 = jnp.max(jnp.where(sel2, logits, neg), axis=1, keepdims=True)
    i2 = jnp.min(jnp.where(sel2 & (logits == m2), lane, big), axis=1, keepdims=True)
    e2 = jnp.exp(m2 - m1)
    w1 = g_val / (1.0 + e2)
    w2 = g_val * e2 / (1.0 + e2)
    oh1 = lane == i1
    oh2 = lane == i2
    oh = jnp.where(oh1 | oh2, 1.0, 0.0)
    rr = lax.broadcasted_iota(I32, (PM_TM, PM_TM), 0)
    cc = lax.broadcasted_iota(I32, (PM_TM, PM_TM), 1)
    tri = jnp.where(cc < rr, 1.0, 0.0).astype(BF16)
    base = cnt_ref[...] + jnp.dot(tri, oh.astype(BF16), preferred_element_type=F32)
    rank1 = jnp.sum(jnp.where(oh1, base, 0.0), axis=1, keepdims=True)
    rank2 = jnp.sum(jnp.where(oh2, base, 0.0), axis=1, keepdims=True)
    cnt_ref[...] = cnt_ref[...] + jnp.sum(oh, axis=0, keepdims=True)
    cnt_out_ref[...] = cnt_ref[...]
    rt = jnp.where(lane == 0, w1, 0.0)
    rt = jnp.where(lane == 1, w2, rt)
    rt = jnp.where(lane == 2, rank1, rt)
    rt = jnp.where(lane == 3, rank2, rt)
    rt = jnp.where(lane == 4, (i1 - EXP_LANE0).astype(F32), rt)
    rt = jnp.where(lane == 5, (i2 - EXP_LANE0).astype(F32), rt)
    rt_ref[...] = rt


def _postmix(y_act, gates, y_pool, xp, xs, wa, wb, wo, g_ffn, wr_hi, wr_lo, b_r):
    n = T_ALL // PM_TM
    npb = PM_PROMPT_BLOCKS
    const2 = lambda i: (0, 0)
    return pl.pallas_call(
        _postmix_body,
        grid=(n,),
        in_specs=[
            pl.BlockSpec((PM_TM, SSM_WIDTH), lambda i: (i, 0)),
            pl.BlockSpec((PM_TM, D_MODEL), lambda i: (i, 0)),
            pl.BlockSpec((PM_TM, D_MODEL), lambda i: (i, 1)),
            pl.BlockSpec((PM_TM, D_MODEL), lambda i: (i, 0)),
            pl.BlockSpec((PM_TM, D_MODEL), lambda i: (jnp.minimum(i, npb - 1), 0)),
            pl.BlockSpec((PM_TM, D_MODEL), lambda i: (jnp.maximum(i - npb, 0), 0)),
            pl.BlockSpec((SSM_WIDTH, D_MODEL), const2, pipeline_mode=pl.Buffered(1)),
            pl.BlockSpec((SSM_WIDTH, D_MODEL), const2, pipeline_mode=pl.Buffered(1)),
            pl.BlockSpec((D_MODEL, D_MODEL), const2, pipeline_mode=pl.Buffered(1)),
            pl.BlockSpec((1, D_MODEL), const2),
            pl.BlockSpec((D_MODEL, ROUTE_LANES), const2),
            pl.BlockSpec((D_MODEL, ROUTE_LANES), const2),
            pl.BlockSpec((1, ROUTE_LANES), const2),
        ],
        out_specs=(
            pl.BlockSpec((PM_TM, D_MODEL), lambda i: (i, 0)),
            pl.BlockSpec((PM_TM, D_PACK), lambda i: (i, 0)),
            pl.BlockSpec((PM_TM, ROUTE_LANES), lambda i: (i, 0)),
            pl.BlockSpec((1, ROUTE_LANES), const2),
        ),
        out_shape=(
            jax.ShapeDtypeStruct((T_ALL, D_MODEL), F32),
            jax.ShapeDtypeStruct((T_ALL, D_PACK), U32),
            jax.ShapeDtypeStruct((T_ALL, ROUTE_LANES), F32),
            jax.ShapeDtypeStruct((1, ROUTE_LANES), F32),
        ),
        scratch_shapes=[pltpu.VMEM((1, ROUTE_LANES), F32)],
        compiler_params=_cparams(1),
        name="postmix",
    )(y_act, gates, gates, y_pool, xp, xs, wa, wb, wo, g_ffn, wr_hi, wr_lo, b_r)


SC_CH = 32


def _sc_workers():
    info = plsc.get_sparse_core_info()
    return info.num_cores, info.num_cores * info.num_subcores


def _sc_dispatch(tn, slots):
    n_cores, n_workers = _sc_workers()
    per_w = (T_ALL // SC_CH) // n_workers
    assert per_w * n_workers * SC_CH == T_ALL
    slots = slots.reshape(2, n_workers, per_w, SC_CH)

    @functools.partial(
        pl.kernel,
        mesh=plsc.VectorSubcoreMesh(core_axis_name="c", subcore_axis_name="s"),
        out_type=jax.ShapeDtypeStruct((N_SLOTS, D_PACK), U32),
        scratch_types=[pltpu.VMEM((2, per_w, SC_CH), I32), pltpu.VMEM((SC_CH, D_PACK), U32)],
    )
    def k(tn_hbm, slots_hbm, xs_hbm, idx_v, rows_v):
        wid = lax.axis_index("s") * n_cores + lax.axis_index("c")
        c0 = wid * per_w
        pltpu.sync_copy(slots_hbm.at[0, wid], idx_v.at[0])
        pltpu.sync_copy(slots_hbm.at[1, wid], idx_v.at[1])

        @pl.loop(0, per_w)
        def _(c):
            row0 = pl.multiple_of((c0 + c) * SC_CH, SC_CH)
            pltpu.sync_copy(tn_hbm.at[pl.ds(row0, SC_CH)], rows_v)
            pltpu.sync_copy(rows_v, xs_hbm.at[idx_v.at[0, c]])
            pltpu.sync_copy(rows_v, xs_hbm.at[idx_v.at[1, c]])

    return k(tn, slots)


def _sc_collect(ys, slots):
    n_cores, n_workers = _sc_workers()
    per_w = (N_ASSIGN // SC_CH) // n_workers
    assert per_w * n_workers * SC_CH == N_ASSIGN
    slots = slots.reshape(n_workers, per_w, SC_CH)

    @functools.partial(
        pl.kernel,
        mesh=plsc.VectorSubcoreMesh(core_axis_name="c", subcore_axis_name="s"),
        out_type=jax.ShapeDtypeStruct((N_ASSIGN, D_PACK), U32),
        scratch_types=[pltpu.VMEM((per_w, SC_CH), I32), pltpu.VMEM((SC_CH, D_PACK), U32)],
    )
    def k(ys_hbm, slots_hbm, out_hbm, idx_v, rows_v):
        wid = lax.axis_index("s") * n_cores + lax.axis_index("c")
        c0 = wid * per_w
        pltpu.sync_copy(slots_hbm.at[wid], idx_v)

        @pl.loop(0, per_w)
        def _(c):
            row0 = pl.multiple_of((c0 + c) * SC_CH, SC_CH)
            pltpu.sync_copy(ys_hbm.at[idx_v.at[c]], rows_v)
            pltpu.sync_copy(rows_v, out_hbm.at[pl.ds(row0, SC_CH)])

    return k(ys, slots)


def _expert_body(t0_ref, nt_ref, xs_hbm, wg_ref, wu_ref, wd_ref, ys_hbm,
                 xb_ref, yb_ref, wgb_ref, wub_ref, wdb_ref, xsem, ysem):
    e = pl.program_id(0)
    n = nt_ref[e]
    g0 = t0_ref[e]

    def rows(j):
        return pl.ds(pl.multiple_of((g0 + j) * TME, TME), TME)

    def x_copy(j, s):
        return pltpu.make_async_copy(xs_hbm.at[rows(j)], xb_ref.at[s], xsem.at[s])

    def y_copy(j, s):
        return pltpu.make_async_copy(yb_ref.at[s], ys_hbm.at[rows(j)], ysem.at[s])

    @pl.when(n > 0)
    def _():
        x_copy(0, 0).start()
        wgb_ref[...] = wg_ref[...].astype(BF16)
        wub_ref[...] = wu_ref[...].astype(BF16)
        wdb_ref[...] = wd_ref[...].astype(BF16)

        def tile(j, c):
            s = j & 1
            x_copy(j, s).wait()
            @pl.when(j + 1 < n)
            def _():
                x_copy(j + 1, 1 - s).start()
            @pl.when(j >= 2)
            def _():
                y_copy(j - 2, s).wait()
            x = _unpack_pairs(xb_ref[s], BF16)
            hg = jnp.dot(x, wgb_ref[...], preferred_element_type=F32)
            hu = jnp.dot(x, wub_ref[...], preferred_element_type=F32)
            act = (hg * _sigmoid(hg)) * hu
            yb_ref[s] = _pack_pairs(jnp.dot(act.astype(BF16), wdb_ref[...], preferred_element_type=F32))
            y_copy(j, s).start()
            return c
        lax.fori_loop(0, n, tile, 0)

        @pl.when(n >= 2)
        def _():
            y_copy(n - 2, n & 1).wait()
        y_copy(n - 1, (n - 1) & 1).wait()


def _experts(tile0, tiles, xs, w_eg, w_eu, w_ed):
    wmap = lambda e, t0, nt: (e, 0, 0)
    grid_spec = pltpu.PrefetchScalarGridSpec(
        num_scalar_prefetch=2,
        grid=(MOE_EXPERTS,),
        in_specs=[
            pl.BlockSpec(memory_space=pl.ANY),
            pl.BlockSpec((None, D_MODEL, MOE_FF), wmap),
            pl.BlockSpec((None, D_MODEL, MOE_FF), wmap),
            pl.BlockSpec((None, MOE_FF, D_MODEL), wmap),
        ],
        out_specs=pl.BlockSpec(memory_space=pl.ANY),
        scratch_shapes=[
            pltpu.VMEM((2, TME, D_PACK), U32),
            pltpu.VMEM((2, TME, D_PACK), U32),
            pltpu.VMEM((D_MODEL, MOE_FF), BF16),
            pltpu.VMEM((D_MODEL, MOE_FF), BF16),
            pltpu.VMEM((MOE_FF, D_MODEL), BF16),
            pltpu.SemaphoreType.DMA((2,)),
            pltpu.SemaphoreType.DMA((2,)),
        ],
    )
    return pl.pallas_call(
        _expert_body,
        grid_spec=grid_spec,
        out_shape=jax.ShapeDtypeStruct((N_SLOTS, D_PACK), U32),
        compiler_params=_cparams(1),
        name="experts",
    )(tile0, tiles, xs, w_eg, w_eu, w_ed)


FN_TM = 256
FN_PROMPT_BLOCKS = T_PROMPT // FN_TM


def _final_body(h_ref, y0_ref, y1_ref, rt_ref, g_ref, op_ref, os_ref):
    i = pl.program_id(0)
    rt = rt_ref[...]
    z = (h_ref[...] + rt[:, 0:1] * _unpack_pairs(y0_ref[...], F32)
         + rt[:, 1:2] * _unpack_pairs(y1_ref[...], F32))
    inv = lax.rsqrt(jnp.mean(z * z, axis=-1, keepdims=True) + EPS)
    out = (z * inv) * g_ref[...]
    @pl.when(i < FN_PROMPT_BLOCKS)
    def _():
        op_ref[...] = out
    @pl.when(i >= FN_PROMPT_BLOCKS)
    def _():
        os_ref[...] = out


def _final(h, y, route, g_final):
    n = T_ALL // FN_TM
    npb = FN_PROMPT_BLOCKS
    yoff = T_ALL // FN_TM
    return pl.pallas_call(
        _final_body,
        grid=(n,),
        in_specs=[
            pl.BlockSpec((FN_TM, D_MODEL), lambda i: (i, 0)),
            pl.BlockSpec((FN_TM, D_PACK), lambda i: (i, 0)),
            pl.BlockSpec((FN_TM, D_PACK), lambda i: (yoff + i, 0)),
            pl.BlockSpec((FN_TM, ROUTE_LANES), lambda i: (i, 0)),
            pl.BlockSpec((1, D_MODEL), lambda i: (0, 0)),
        ],
        out_specs=(
            pl.BlockSpec((FN_TM, D_MODEL), lambda i: (jnp.minimum(i, npb - 1), 0)),
            pl.BlockSpec((FN_TM, D_MODEL), lambda i: (jnp.maximum(i - npb, 0), 0)),
        ),
        out_shape=(
            jax.ShapeDtypeStruct((T_PROMPT, D_MODEL), F32),
            jax.ShapeDtypeStruct((T_SAMPLE, D_MODEL), F32),
        ),
        compiler_params=_cparams(1),
        name="final",
    )(h, y, y, route, g_final)


def _dispatch_plan(route, cnt):
    counts = cnt[0, EXP_LANE0:EXP_LANE0 + MOE_EXPERTS].astype(I32)
    tiles = (counts + (TME - 1)) // TME
    cumt = jnp.cumsum(tiles)
    pad_off = (cumt - tiles) * TME
    rank = route[:, 2:4].astype(I32)
    eid = route[:, 4:6].astype(I32)
    onehot = eid[..., None] == jnp.arange(MOE_EXPERTS, dtype=I32)
    slots = (rank + jnp.sum(jnp.where(onehot, pad_off, 0), axis=-1)).T
    return slots, cumt - tiles, tiles


def kernel(x_prompt, x_sample, state_pool, state_ssm_re, state_ssm_im, g_mix, w_in, w_pool,
           pool_scale, ssm_a_re, ssm_a_im, ssm_log_dt, ssm_b_re, ssm_b_im, ssm_c_re, ssm_c_im,
           ssm_d, w_glu_a, w_glu_b, w_out, g_ffn, w_router_group, b_router_group,
           w_router_expert, b_router_expert, w_exp_gate, w_exp_up, w_exp_down, g_final):
    l = 0
    xp = x_prompt.reshape(T_PROMPT, D_MODEL)
    xs = x_sample.transpose(1, 0, 2).reshape(T_SAMPLE, D_MODEL)
    w_in_bf = w_in[l].astype(BF16)
    w_pool_bf = w_pool[l].astype(BF16)
    wa_bf = w_glu_a[l].astype(BF16)
    wb_bf = w_glu_b[l].astype(BF16)
    wo_bf = w_out[l].astype(BF16)
    g_mix2 = g_mix[l].reshape(1, D_MODEL)
    scale2 = pool_scale[l].reshape(1, D_MODEL)

    u, gates = _inproj(xp, g_mix2, w_in_bf, 0)
    u, gates = _inproj(xs, g_mix2, w_in_bf, T_PROMPT // IN_TM, dst=(u, gates))

    y_pool, pool_tail = _pool_prompt(u, w_pool_bf, scale2)
    hist_tm = state_pool[l].transpose(1, 0, 2)
    y_pool = _pool_sample(u, hist_tm, w_pool_bf, scale2, y_pool)

    tables = _ssm_tables(ssm_a_re[l], ssm_a_im[l], ssm_log_dt[l], ssm_b_re[l], ssm_b_im[l],
                         ssm_c_re[l], ssm_c_im[l], ssm_d[l])
    y_act, h_prompt = _ssm_prompt(u, tables)
    h0r = state_ssm_re[l].reshape(DEC_BATCH, SSM_GROUPS * SSM_STATE)
    h0i = state_ssm_im[l].reshape(DEC_BATCH, SSM_GROUPS * SSM_STATE)
    y_act, hs_re, hs_im = _ssm_sample(u, h0r, h0i, tables, y_act)

    w_r = jnp.zeros((D_MODEL, ROUTE_LANES), F32)
    w_r = w_r.at[:, :MOE_GROUPS].set(w_router_group[l])
    w_r = w_r.at[:, EXP_LANE0:EXP_LANE0 + MOE_EXPERTS].set(w_router_expert[l])
    wr_hi = w_r.astype(BF16)
    wr_lo = (w_r - wr_hi.astype(F32)).astype(BF16)
    b_r = jnp.zeros((1, ROUTE_LANES), F32)
    b_r = b_r.at[0, :MOE_GROUPS].set(b_router_group[l])
    b_r = b_r.at[0, EXP_LANE0:EXP_LANE0 + MOE_EXPERTS].set(b_router_expert[l])

    h, tn, route, cnt = _postmix(y_act, gates, y_pool, xp, xs, wa_bf, wb_bf, wo_bf,
                                 g_ffn[l].reshape(1, D_MODEL), wr_hi, wr_lo, b_r)
    slots, tile0, tiles = _dispatch_plan(route, cnt)
    xs_sorted = _sc_dispatch(tn, slots)
    ys_sorted = _experts(tile0, tiles, xs_sorted, w_exp_gate[l], w_exp_up[l], w_exp_down[l])
    y = _sc_collect(ys_sorted, slots)
    yp, ys = _final(h, y, route, g_final.reshape(1, D_MODEL))

    y_prompt = yp.reshape(BATCH, SEQ, D_MODEL)
    y_sample = ys.reshape(DEC_SEQ, DEC_BATCH, D_MODEL).transpose(1, 0, 2)
    new_pool_prompt = pool_tail[:, HIST - POOL_BUF:, :][None]
    us = u[T_PROMPT:, :POOL_WIDTH].reshape(DEC_SEQ, DEC_BATCH, POOL_WIDTH).transpose(1, 0, 2)
    new_pool_sample = jnp.concatenate([state_pool[l][:, DEC_SEQ:, :], us], axis=1)[None]
    hp = h_prompt.reshape(BATCH, N_OCT, 2, OCT_GROUPS, SSM_STATE).transpose(2, 0, 1, 3, 4)
    hp = hp.reshape(2, BATCH, SSM_GROUPS, SSM_STATE)
    shp = (1, DEC_BATCH, SSM_GROUPS, SSM_STATE)
    return (y_prompt, y_sample, new_pool_prompt, hp[0][None], hp[1][None], new_pool_sample,
            hs_re.reshape(shp), hs_im.reshape(shp))
```

```python
import functools
import math

import jax
import jax.numpy as jnp
from jax import lax
from jax.experimental import pallas as pl
from jax.experimental.pallas import tpu as pltpu
from jax.experimental.pallas import tpu_sc as plsc

F32 = jnp.float32
BF16 = jnp.bfloat16
I32 = jnp.int32
U32 = jnp.uint32

D_MODEL = 2048
BATCH = 4
SEQ = 2048
DEC_BATCH = 128
DEC_SEQ = 8
PAST_LEN = 16384
POOL_WIDTH = D_MODEL // 2
POOL_WINDOWS = (2, 4, 8, 16)
POOL_GROUPS = len(POOL_WINDOWS)
POOL_GROUP_CH = POOL_WIDTH // POOL_GROUPS
POOL_OUT_CH = D_MODEL // POOL_GROUPS
POOL_BUF = max(POOL_WINDOWS) - 1
SSM_WIDTH = D_MODEL // 2
SSM_GROUP_CH = 16
SSM_GROUPS = SSM_WIDTH // SSM_GROUP_CH
SSM_STATE = 64
IN_WIDTH = POOL_WIDTH + SSM_WIDTH + 2 * D_MODEL
D_PACK = D_MODEL // 2
MOE_GROUPS = 4
MOE_EPG = 8
MOE_EXPERTS = MOE_GROUPS * MOE_EPG
MOE_FF = D_MODEL // 4
EPS = 1e-6

T_PROMPT = BATCH * SEQ
T_SAMPLE = DEC_BATCH * DEC_SEQ
T_ALL = T_PROMPT + T_SAMPLE
T_PAD = (BATCH + 1) * SEQ

LANES = 128
SUBLANES = 8
VMEM_LIMIT = 56 * 1024 * 1024

CHUNK = 8
OCT = LANES
N_OCT = SSM_WIDTH // OCT
OCT_GROUPS = OCT // SSM_GROUP_CH
OCT_STATES = OCT_GROUPS * SSM_STATE
CW = CHUNK * OCT
SW = 2 * OCT_STATES

ROUTE_LANES = LANES
EXP_LANE0 = MOE_GROUPS
N_ASSIGN = 2 * T_ALL
TME = 256
N_ITEMS_MAX = N_ASSIGN // TME + MOE_EXPERTS
N_SLOTS = N_ITEMS_MAX * TME


def _cparams(n_axes):
    return pltpu.CompilerParams(dimension_semantics=("arbitrary",) * n_axes,
                                vmem_limit_bytes=VMEM_LIMIT)


def _sigmoid(x):
    return 1.0 / (1.0 + jnp.exp(-x))


def _pack_pairs(x):
    c = x.shape[1] // 2
    hi = lax.bitcast_convert_type(x[:, :c].astype(BF16).astype(F32), U32)
    lo = lax.bitcast_convert_type(x[:, c:].astype(BF16).astype(F32), U32)
    return hi | (lo >> 16)


def _unpack_pairs(u, dtype):
    hi = lax.bitcast_convert_type(u & jnp.uint32(0xFFFF0000), F32)
    lo = lax.bitcast_convert_type(u << 16, F32)
    return jnp.concatenate([hi, lo], axis=1).astype(dtype)


def _gelu_tanh(x):
    c = math.sqrt(2.0 / math.pi)
    return 0.5 * x * (1.0 + jnp.tanh(c * (x + 0.044715 * (x * x * x))))


IN_TM = 1024
IN_TN = 1024
U_WIDTH = POOL_WIDTH + SSM_WIDTH
GATE_WIDTH = 2 * D_MODEL
IN_U_STEPS = U_WIDTH // IN_TN


def _inproj_body(x_ref, g_ref, w_ref, *rest):
    u_ref, gate_ref, xn_ref = rest[-3:]
    j = pl.program_id(1)
    @pl.when(j == 0)
    def _():
        x = x_ref[...]
        inv = lax.rsqrt(jnp.mean(x * x, axis=-1, keepdims=True) + EPS)
        xn_ref[...] = ((x * inv) * g_ref[...]).astype(BF16)
    acc = jnp.dot(xn_ref[...], w_ref[...], preferred_element_type=F32)
    @pl.when(j < IN_U_STEPS)
    def _():
        u_ref[...] = acc
    @pl.when(j >= IN_U_STEPS)
    def _():
        gate_ref[...] = _sigmoid(acc).astype(BF16)


def _inproj(x, g, w_bf, row_block0, dst=None):
    n_i = x.shape[0] // IN_TM
    in_specs = [
        pl.BlockSpec((IN_TM, D_MODEL), lambda i, j: (i, 0)),
        pl.BlockSpec((1, D_MODEL), lambda i, j: (0, 0)),
        pl.BlockSpec((D_MODEL, IN_TN), lambda i, j: (0, j)),
    ]
    args = [x, g, w_bf]
    aliases = {}
    if dst is not None:
        in_specs += [pl.BlockSpec(memory_space=pl.ANY)] * 2
        args += list(dst)
        aliases = {3: 0, 4: 1}
    return pl.pallas_call(
        _inproj_body,
        grid=(n_i, IN_WIDTH // IN_TN),
        in_specs=in_specs,
        out_specs=(
            pl.BlockSpec((IN_TM, IN_TN), lambda i, j: (i + row_block0, jnp.minimum(j, IN_U_STEPS - 1))),
            pl.BlockSpec((IN_TM, IN_TN), lambda i, j: (i + row_block0, jnp.maximum(j - IN_U_STEPS, 0))),
        ),
        out_shape=(
            jax.ShapeDtypeStruct((T_PAD, U_WIDTH), F32),
            jax.ShapeDtypeStruct((T_ALL, GATE_WIDTH), BF16),
        ),
        scratch_shapes=[pltpu.VMEM((IN_TM, D_MODEL), BF16)],
        input_output_aliases=aliases,
        compiler_params=_cparams(2),
        name="inproj",
    )(*args)


PP_TM = 512
HIST = 16


def _pool_project(pooled_g, g, w_ref, sc_ref, o_ref):
    y = jnp.dot(pooled_g.astype(BF16), w_ref[g], preferred_element_type=F32)
    lo, hi = g * POOL_OUT_CH, (g + 1) * POOL_OUT_CH
    o_ref[:, lo:hi] = (y * sc_ref[:, lo:hi]).astype(o_ref.dtype)


def _pool_prompt_body(u_ref, w_ref, sc_ref, o_ref, tail_ref, hist_ref):
    i = pl.program_id(1)
    @pl.when(i == 0)
    def _():
        hist_ref[...] = jnp.zeros_like(hist_ref)
    u = u_ref[...]
    ext = jnp.concatenate([hist_ref[...], u], axis=0)
    hist_ref[...] = u[PP_TM - HIST:, :]
    tail_ref[...] = u[PP_TM - HIST:, :]
    pos = i * PP_TM + lax.broadcasted_iota(I32, (PP_TM, 1), 0)
    for g, w in enumerate(POOL_WINDOWS):
        lo, hi = g * POOL_GROUP_CH, (g + 1) * POOL_GROUP_CH
        s = ext[:, lo:hi]
        d = 1
        while d < w:
            s = s + pltpu.roll(s, d, axis=0)
            d *= 2
        cnt = jnp.minimum(w, pos + 1).astype(F32)
        pooled = s[HIST:, :] / cnt - u[:, lo:hi]
        _pool_project(pooled, g, w_ref, sc_ref, o_ref)


def _pool_prompt(u, w_pool_bf, pool_scale):
    n_i = SEQ // PP_TM
    return pl.pallas_call(
        _pool_prompt_body,
        grid=(BATCH, n_i),
        in_specs=[
            pl.BlockSpec((PP_TM, POOL_WIDTH), lambda b, i: (b * n_i + i, 0)),
            pl.BlockSpec((POOL_GROUPS, POOL_GROUP_CH, POOL_OUT_CH), lambda b, i: (0, 0, 0)),
            pl.BlockSpec((1, D_MODEL), lambda b, i: (0, 0)),
        ],
        out_specs=(
            pl.BlockSpec((PP_TM, D_MODEL), lambda b, i: (b * n_i + i, 0)),
            pl.BlockSpec((None, HIST, POOL_WIDTH), lambda b, i: (b, 0, 0)),
        ),
        out_shape=(
            jax.ShapeDtypeStruct((T_ALL, D_MODEL), BF16),
            jax.ShapeDtypeStruct((BATCH, HIST, POOL_WIDTH), F32),
        ),
        scratch_shapes=[pltpu.VMEM((HIST, POOL_WIDTH), F32)],
        compiler_params=_cparams(2),
        name="pool_prompt",
    )(u, w_pool_bf, pool_scale)


def _pool_sample_body(u_ref, hist_ref, w_ref, sc_ref, _dst, o_ref):
    rows = [hist_ref[k] for k in range(POOL_BUF)]
    rows += [u_ref[DEC_BATCH * t:DEC_BATCH * (t + 1), :] for t in range(DEC_SEQ)]
    n = len(rows)
    for g, w in enumerate(POOL_WINDOWS):
        lo, hi = g * POOL_GROUP_CH, (g + 1) * POOL_GROUP_CH
        f = [r[:, lo:hi] for r in rows]
        cur = f
        d = 1
        while d < w:
            cur = [cur[k] + cur[k - d] if k - d >= 0 else cur[k] for k in range(n)]
            d *= 2
        pooled = jnp.concatenate(
            [cur[POOL_BUF + t] / float(w) - f[POOL_BUF + t] for t in range(DEC_SEQ)], axis=0)
        _pool_project(pooled, g, w_ref, sc_ref, o_ref)


def _pool_sample(u, hist_tm, w_pool_bf, pool_scale, y_pool):
    blk = T_PROMPT // T_SAMPLE
    return pl.pallas_call(
        _pool_sample_body,
        grid=(1,),
        in_specs=[
            pl.BlockSpec((T_SAMPLE, POOL_WIDTH), lambda i: (blk, 0)),
            pl.BlockSpec((POOL_BUF, DEC_BATCH, POOL_WIDTH), lambda i: (0, 0, 0)),
            pl.BlockSpec((POOL_GROUPS, POOL_GROUP_CH, POOL_OUT_CH), lambda i: (0, 0, 0)),
            pl.BlockSpec((1, D_MODEL), lambda i: (0, 0)),
            pl.BlockSpec(memory_space=pl.ANY),
        ],
        out_specs=pl.BlockSpec((T_SAMPLE, D_MODEL), lambda i: (blk, 0)),
        out_shape=jax.ShapeDtypeStruct((T_ALL, D_MODEL), BF16),
        input_output_aliases={4: 0},
        compiler_params=_cparams(1),
        name="pool_sample",
    )(u, hist_tm, w_pool_bf, pool_scale, y_pool)


def _ssm_tables(a_re, a_im, log_dt, b_re, b_im, c_re, c_im, d_skip):
    dt = jnp.exp(log_dt)[:, None]
    lr, li = a_re, a_im
    ab_re = jnp.exp(lr * dt) * jnp.cos(li * dt)
    ab_im = jnp.exp(lr * dt) * jnp.sin(li * dt)
    den = lr * lr + li * li
    nr, ni = ab_re - 1.0, ab_im
    q_re = (nr * lr + ni * li) / den
    q_im = (ni * lr - nr * li) / den
    bb_re = q_re[..., None] * b_re - q_im[..., None] * b_im
    bb_im = q_re[..., None] * b_im + q_im[..., None] * b_re

    def lam_rows(ks):
        k = jnp.asarray(ks, F32)[:, None, None]
        m = jnp.exp(k * lr * dt)
        re = (m * jnp.cos(k * li * dt)).reshape(len(ks), N_OCT, OCT_STATES)
        im = (m * jnp.sin(k * li * dt)).reshape(len(ks), N_OCT, OCT_STATES)
        return jnp.concatenate([re, im], axis=-1).transpose(1, 0, 2)

    def compact(re, im):
        v = jnp.concatenate([re, im], axis=-1)
        return v.reshape(N_OCT, OCT, 2 * SSM_STATE)

    bbc = compact(jnp.swapaxes(bb_re, 1, 2), jnp.swapaxes(bb_im, 1, 2))
    ccc = compact(c_re, c_im)
    pw = lam_rows(list(range(2 * SUBLANES)))
    r = jnp.arange(SUBLANES)[None, :, None]
    parts = [jnp.where(r >= dd, lam_rows([CHUNK * dd]), 0.0) for dd in (1, 2, 4)]
    parts.append(lam_rows([CHUNK * kk for kk in range(1, SUBLANES + 1)]))
    tab = jnp.concatenate(parts, axis=1)
    dsk = d_skip.reshape(N_OCT, 1, OCT)
    return bbc, ccc, pw, tab, dsk


def _split_bf16(x):
    hi = x.astype(BF16)
    return hi, (x - hi.astype(F32)).astype(BF16)


def _dot_nt(a, b):
    return lax.dot_general(a, b, (((1,), (1,)), ((), ())), preferred_element_type=F32)


def _build_weights(bbc_ref, ccc_ref, pw_ref, f_ref, gt_ref, m_ref):
    row_gi = lax.broadcasted_iota(I32, (OCT, 1), 0) >> 4
    col = lax.broadcasted_iota(I32, (1, SW), 1)
    col_gi = (col >> 6) & 7
    src = ((col >> 9) << 6) | (col & 63)
    k128 = lax.broadcasted_iota(I32, (2 * SSM_STATE, 1), 0)
    spread = jnp.where(k128 == src, 1.0, 0.0).astype(BF16)
    diag = row_gi == col_gi

    def expand(c_ref):
        hi, lo = _split_bf16(c_ref[...])
        d = (jnp.dot(hi, spread, preferred_element_type=F32)
             + jnp.dot(lo, spread, preferred_element_type=F32))
        d = jnp.where(diag, d, 0.0)
        return d[:, :OCT_STATES], d[:, OCT_STATES:]

    br, bi = expand(bbc_ref)
    cr, ci = expand(ccc_ref)
    chi_r, clo_r = _split_bf16(cr)
    chi_i, clo_i = _split_bf16(ci)

    def lam(k):
        return pw_ref[k:k + 1, :OCT_STATES], pw_ref[k:k + 1, OCT_STATES:]

    def dot3(a, bhi, blo):
        ahi, alo = _split_bf16(a)
        return _dot_nt(ahi, bhi) + _dot_nt(alo, bhi) + _dot_nt(ahi, blo)

    lags = []
    for k in range(CHUNK):
        pr, pi_ = lam(k)
        fr, fi = _cmul(br, bi, pr, pi_)
        s = CHUNK - 1 - k
        f_ref[s * OCT:(s + 1) * OCT, :] = jnp.concatenate([fr, fi], axis=1).astype(BF16)
        lags.append((dot3(fr, chi_r, clo_r) - dot3(fi, chi_i, clo_i)).astype(BF16))
        pr, pi_ = lam(k + 1)
        gr, gi = _cmul(cr, ci, pr, pi_)
        gt_ref[k * OCT:(k + 1) * OCT, :] = jnp.concatenate([gr, -gi], axis=1).astype(BF16)
    zero = jnp.zeros((OCT, OCT), BF16)
    for s in range(CHUNK):
        for t in range(CHUNK):
            m_ref[s * OCT:(s + 1) * OCT, t * OCT:(t + 1) * OCT] = lags[t - s] if t >= s else zero


def _cmul(ar, ai, br, bi):
    return ar * br - ai * bi, ar * bi + ai * br


def _chunk_scan(sloc, tab_ref):
    R = sloc.shape[0]
    nb = R // SUBLANES
    sr, si = sloc[:, :OCT_STATES], sloc[:, OCT_STATES:]
    rowi = lax.broadcasted_iota(I32, (R, 1), 0)
    tr = jnp.where(rowi == 0, 0.0, pltpu.roll(sr, 1, axis=0))
    ti = jnp.where(rowi == 0, 0.0, pltpu.roll(si, 1, axis=0))
    for lvl, d in enumerate((1, 2, 4)):
        mr = tab_ref[lvl * SUBLANES:(lvl + 1) * SUBLANES, :OCT_STATES]
        mi = tab_ref[lvl * SUBLANES:(lvl + 1) * SUBLANES, OCT_STATES:]
        mr = jnp.concatenate([mr] * nb, axis=0)
        mi = jnp.concatenate([mi] * nb, axis=0)
        pr, pi_ = _cmul(mr, mi, pltpu.roll(tr, d, axis=0), pltpu.roll(ti, d, axis=0))
        tr, ti = tr + pr, ti + pi_
    pwr = tab_ref[3 * SUBLANES:4 * SUBLANES, :OCT_STATES]
    pwi = tab_ref[3 * SUBLANES:4 * SUBLANES, OCT_STATES:]
    cr = jnp.zeros((1, OCT_STATES), F32)
    ci = jnp.zeros((1, OCT_STATES), F32)
    out_r, out_i = [], []
    for k in range(nb):
        ar = tr[k * SUBLANES:(k + 1) * SUBLANES, :]
        ai = ti[k * SUBLANES:(k + 1) * SUBLANES, :]
        pr, pi_ = _cmul(pwr, pwi, jnp.broadcast_to(cr, ar.shape), jnp.broadcast_to(ci, ai.shape))
        hr, hi = ar + pr, ai + pi_
        out_r.append(hr)
        out_i.append(hi)
        cr, ci = hr[SUBLANES - 1:, :], hi[SUBLANES - 1:, :]
    hin = jnp.concatenate([jnp.concatenate(out_r, axis=0), jnp.concatenate(out_i, axis=0)], axis=1)
    lr, li = pwr[0:1, :], pwi[0:1, :]
    fr, fi = _cmul(lr, li, cr, ci)
    fin = jnp.concatenate([fr + sr[R - 1:, :], fi + si[R - 1:, :]], axis=1)
    return hin, fin


def _ssm_body(u_ref, h0r_ref, h0i_ref, bbc_ref, ccc_ref, pw_ref, tab_ref, d_ref,
              y_ref, hout_ref, hr_ref, hi_ref, f_ref, gt_ref, m_ref):
    b = pl.program_id(1)

    @pl.when(b == 0)
    def _():
        _build_weights(bbc_ref, ccc_ref, pw_ref, f_ref, gt_ref, m_ref)

    def outputs(xs, xb, hin_bf):
        y = (jnp.dot(xb, m_ref[...], preferred_element_type=F32) + _dot_nt(hin_bf, gt_ref[...]))
        return [_gelu_tanh(y[:, t * OCT:(t + 1) * OCT] + d_ref[...] * xs[t]) for t in range(CHUNK)]

    @pl.when(b < BATCH)
    def _():
        R = SEQ // CHUNK
        xs = [u_ref[pl.ds(s, R, stride=CHUNK), :] for s in range(CHUNK)]
        xb = jnp.concatenate(xs, axis=1).astype(BF16)
        sloc = jnp.dot(xb, f_ref[...], preferred_element_type=F32)
        hin, fin = _chunk_scan(sloc, tab_ref)
        for t, yt in enumerate(outputs(xs, xb, hin.astype(BF16))):
            y_ref[pl.ds(t, R, stride=CHUNK), :] = yt
        hout_ref[...] = fin

    @pl.when(b == BATCH)
    def _():
        B = DEC_BATCH
        xs = [u_ref[B * s:B * (s + 1), :] for s in range(CHUNK)]
        xb = jnp.concatenate(xs, axis=1).astype(BF16)
        sloc = jnp.dot(xb, f_ref[...], preferred_element_type=F32)
        h0r, h0i = h0r_ref[...], h0i_ref[...]
        hin = jnp.concatenate([h0r, h0i], axis=1).astype(BF16)
        for t, yt in enumerate(outputs(xs, xb, hin)):
            y_ref[B * t:B * (t + 1), :] = yt
        lr = tab_ref[3 * SUBLANES:3 * SUBLANES + 1, :OCT_STATES]
        li = tab_ref[3 * SUBLANES:3 * SUBLANES + 1, OCT_STATES:]
        nr, ni = _cmul(lr, li, h0r, h0i)
        hr_ref[...] = nr + sloc[:, :OCT_STATES]
        hi_ref[...] = ni + sloc[:, OCT_STATES:]


def _ssm(u, h0r, h0i, tables):
    col0 = POOL_WIDTH // OCT
    im3 = lambda o, b: (o, 0, 0)
    st_spec = pl.BlockSpec((DEC_BATCH, OCT_STATES), lambda o, b: (0, o))
    return pl.pallas_call(
        _ssm_body,
        grid=(N_OCT, BATCH + 1),
        in_specs=[
            pl.BlockSpec((SEQ, OCT), lambda o, b: (b, col0 + o)), st_spec, st_spec,
            pl.BlockSpec((None, OCT, 2 * SSM_STATE), im3),
            pl.BlockSpec((None, OCT, 2 * SSM_STATE), im3),
            pl.BlockSpec((None, 2 * SUBLANES, SW), im3),
            pl.BlockSpec((None, 4 * SUBLANES, SW), im3),
            pl.BlockSpec((None, 1, OCT), im3),
        ],
        out_specs=(
            pl.BlockSpec((SEQ, OCT), lambda o, b: (b, o)),
            pl.BlockSpec((None, 1, SW), lambda o, b: (jnp.minimum(b, BATCH - 1) * N_OCT + o, 0, 0)),
            st_spec, st_spec,
        ),
        out_shape=(
            jax.ShapeDtypeStruct((T_PAD, SSM_WIDTH), F32),
            jax.ShapeDtypeStruct((BATCH * N_OCT, 1, SW), F32),
            jax.ShapeDtypeStruct((DEC_BATCH, SSM_GROUPS * SSM_STATE), F32),
            jax.ShapeDtypeStruct((DEC_BATCH, SSM_GROUPS * SSM_STATE), F32),
        ),
        scratch_shapes=[pltpu.VMEM((CW, SW), BF16), pltpu.VMEM((CW, SW), BF16), pltpu.VMEM((CW, CW), BF16)],
        compiler_params=_cparams(2),
        name="ssm",
    )(u, h0r, h0i, *tables)


PM_TM = 256
PM_PROMPT_BLOCKS = T_PROMPT // PM_TM
PM_STEPS = T_ALL // PM_TM


def _route(logits, valid, cnt_ref):
    lane = lax.broadcasted_iota(I32, (PM_TM, ROUTE_LANES), 1)
    neg = jnp.float32(-jnp.inf)
    big = jnp.int32(1 << 20)
    is_g = lane < MOE_GROUPS
    gmax = jnp.max(jnp.where(is_g, logits, neg), axis=1, keepdims=True)
    g_idx = jnp.min(jnp.where(is_g & (logits == gmax), lane, big), axis=1, keepdims=True)
    g_den = jnp.sum(jnp.where(is_g, jnp.exp(logits - gmax), 0.0), axis=1, keepdims=True)
    g_val = 1.0 / g_den
    e_lane = lane - EXP_LANE0
    sel = (e_lane >= 0) & (e_lane < MOE_EXPERTS) & ((e_lane >> 3) == g_idx)
    m1 = jnp.max(jnp.where(sel, logits, neg), axis=1, keepdims=True)
    i1 = jnp.min(jnp.where(sel & (logits == m1), lane, big), axis=1, keepdims=True)
    sel2 = sel & (lane != i1)
    m2 = jnp.max(jnp.where(sel2, logits, neg), axis=1, keepdims=True)
    i2 = jnp.min(jnp.where(sel2 & (logits == m2), lane, big), axis=1, keepdims=True)
    e2 = jnp.exp(m2 - m1)
    w1 = g_val / (1.0 + e2)
    w2 = g_val * e2 / (1.0 + e2)
    oh1 = lane == i1
    oh2 = lane == i2
    oh = jnp.where(oh1 | oh2, valid, 0.0)
    rr = lax.broadcasted_iota(I32, (PM_TM, PM_TM), 0)
    cc = lax.broadcasted_iota(I32, (PM_TM, PM_TM), 1)
    tri = jnp.where(cc < rr, 1.0, 0.0).astype(BF16)
    base = cnt_ref[...] + jnp.dot(tri, oh.astype(BF16), preferred_element_type=F32)
    rank1 = jnp.sum(jnp.where(oh1, base, 0.0), axis=1, keepdims=True)
    rank2 = jnp.sum(jnp.where(oh2, base, 0.0), axis=1, keepdims=True)
    cnt_ref[...] = cnt_ref[...] + jnp.sum(oh, axis=0, keepdims=True)
    rt = jnp.where(lane == 0, w1, 0.0)
    rt = jnp.where(lane == 1, w2, rt)
    rt = jnp.where(lane == 2, rank1, rt)
    rt = jnp.where(lane == 3, rank2, rt)
    rt = jnp.where(lane == 4, (i1 - EXP_LANE0).astype(F32), rt)
    rt = jnp.where(lane == 5, (i2 - EXP_LANE0).astype(F32), rt)
    return rt


def _postmix_body(ya_ref, gp_ref, gs_ref, yp_ref, xp_ref, xs_ref, wa_ref, wb_ref, wo_ref,
                  gf_ref, wrh_ref, wrl_ref, br_ref, h_ref, tn_ref, rt_ref, cnt_out_ref,
                  cnt_ref, lg_ref):
    i = pl.program_id(0)
    @pl.when(i == 0)
    def _():
        cnt_ref[...] = jnp.zeros_like(cnt_ref)
        lg_ref[...] = jnp.zeros_like(lg_ref)
    rt_ref[...] = _route(lg_ref[...], jnp.where(i > 0, 1.0, 0.0), cnt_ref)
    cnt_out_ref[...] = cnt_ref[...]

    ya = ya_ref[...].astype(BF16)
    a = jnp.dot(ya, wa_ref[...], preferred_element_type=F32)
    bg = jnp.dot(ya, wb_ref[...], preferred_element_type=F32)
    y_ssm = a * _sigmoid(bg)
    merged = (gp_ref[...].astype(F32) * yp_ref[...].astype(F32)
              + gs_ref[...].astype(F32) * y_ssm)
    x = jnp.where(jnp.minimum(i, PM_STEPS - 1) < PM_PROMPT_BLOCKS, xp_ref[...], xs_ref[...])
    h = x + jnp.dot(merged.astype(BF16), wo_ref[...], preferred_element_type=F32)
    h_ref[...] = h
    inv = lax.rsqrt(jnp.mean(h * h, axis=-1, keepdims=True) + EPS)
    tn = (h * inv) * gf_ref[...]
    tn_ref[...] = _pack_pairs(tn)
    t_hi = tn.astype(BF16)
    t_lo = (tn - t_hi.astype(F32)).astype(BF16)
    wrh = wrh_ref[...]
    lg_ref[...] = (jnp.dot(t_hi, wrh, preferred_element_type=F32)
                   + jnp.dot(t_lo, wrh, preferred_element_type=F32)
                   + jnp.dot(t_hi, wrl_ref[...], preferred_element_type=F32)) + br_ref[...]


def _postmix(y_act, gates, y_pool, xp, xs, wa, wb, wo, g_ffn, wr_hi, wr_lo, b_r):
    npb = PM_PROMPT_BLOCKS
    const2 = lambda i: (0, 0)
    tile = lambda i: jnp.minimum(i, PM_STEPS - 1)
    return pl.pallas_call(
        _postmix_body,
        grid=(PM_STEPS + 1,),
        in_specs=[
            pl.BlockSpec((PM_TM, SSM_WIDTH), lambda i: (tile(i), 0)),
            pl.BlockSpec((PM_TM, D_MODEL), lambda i: (tile(i), 0)),
            pl.BlockSpec((PM_TM, D_MODEL), lambda i: (tile(i), 1)),
            pl.BlockSpec((PM_TM, D_MODEL), lambda i: (tile(i), 0)),
            pl.BlockSpec((PM_TM, D_MODEL), lambda i: (jnp.minimum(i, npb - 1), 0)),
            pl.BlockSpec((PM_TM, D_MODEL), lambda i: (jnp.maximum(tile(i) - npb, 0), 0)),
            pl.BlockSpec((SSM_WIDTH, D_MODEL), const2, pipeline_mode=pl.Buffered(1)),
            pl.BlockSpec((SSM_WIDTH, D_MODEL), const2, pipeline_mode=pl.Buffered(1)),
            pl.BlockSpec((D_MODEL, D_MODEL), const2, pipeline_mode=pl.Buffered(1)),
            pl.BlockSpec((1, D_MODEL), const2),
            pl.BlockSpec((D_MODEL, ROUTE_LANES), const2),
            pl.BlockSpec((D_MODEL, ROUTE_LANES), const2),
            pl.BlockSpec((1, ROUTE_LANES), const2),
        ],
        out_specs=(
            pl.BlockSpec((PM_TM, D_MODEL), lambda i: (tile(i), 0)),
            pl.BlockSpec((PM_TM, D_PACK), lambda i: (tile(i), 0)),
            pl.BlockSpec((PM_TM, ROUTE_LANES), lambda i: (jnp.maximum(i - 1, 0), 0)),
            pl.BlockSpec((1, ROUTE_LANES), const2),
        ),
        out_shape=(
            jax.ShapeDtypeStruct((T_ALL, D_MODEL), F32),
            jax.ShapeDtypeStruct((T_ALL, D_PACK), U32),
            jax.ShapeDtypeStruct((T_ALL, ROUTE_LANES), F32),
            jax.ShapeDtypeStruct((1, ROUTE_LANES), F32),
        ),
        scratch_shapes=[pltpu.VMEM((1, ROUTE_LANES), F32), pltpu.VMEM((PM_TM, ROUTE_LANES), F32)],
        compiler_params=_cparams(1),
        name="postmix",
    )(y_act, gates, gates, y_pool, xp, xs, wa, wb, wo, g_ffn, wr_hi, wr_lo, b_r)


SC_CH = 32


def _sc_workers():
    info = plsc.get_sparse_core_info()
    return info.num_cores, info.num_cores * info.num_subcores


def _sc_dispatch(tn, slots):
    n_cores, n_workers = _sc_workers()
    per_w = (T_ALL // SC_CH) // n_workers
    assert per_w * n_workers * SC_CH == T_ALL
    slots = slots.reshape(2, n_workers, per_w, SC_CH)

    @functools.partial(
        pl.kernel,
        mesh=plsc.VectorSubcoreMesh(core_axis_name="c", subcore_axis_name="s"),
        out_type=jax.ShapeDtypeStruct((N_SLOTS, D_PACK), U32),
        scratch_types=[pltpu.VMEM((2, per_w, SC_CH), I32), pltpu.VMEM((SC_CH, D_PACK), U32)],
    )
    def k(tn_hbm, slots_hbm, xs_hbm, idx_v, rows_v):
        wid = lax.axis_index("s") * n_cores + lax.axis_index("c")
        c0 = wid * per_w
        pltpu.sync_copy(slots_hbm.at[0, wid], idx_v.at[0])
        pltpu.sync_copy(slots_hbm.at[1, wid], idx_v.at[1])

        @pl.loop(0, per_w)
        def _(c):
            row0 = pl.multiple_of((c0 + c) * SC_CH, SC_CH)
            pltpu.sync_copy(tn_hbm.at[pl.ds(row0, SC_CH)], rows_v)
            pltpu.sync_copy(rows_v, xs_hbm.at[idx_v.at[0, c]])
            pltpu.sync_copy(rows_v, xs_hbm.at[idx_v.at[1, c]])

    return k(tn, slots)


def _sc_collect(ys, slots):
    n_cores, n_workers = _sc_workers()
    per_w = (N_ASSIGN // SC_CH) // n_workers
    assert per_w * n_workers * SC_CH == N_ASSIGN
    slots = slots.reshape(n_workers, per_w, SC_CH)

    @functools.partial(
        pl.kernel,
        mesh=plsc.VectorSubcoreMesh(core_axis_name="c", subcore_axis_name="s"),
        out_type=jax.ShapeDtypeStruct((N_ASSIGN, D_PACK), U32),
        scratch_types=[pltpu.VMEM((per_w, SC_CH), I32), pltpu.VMEM((SC_CH, D_PACK), U32)],
    )
    def k(ys_hbm, slots_hbm, out_hbm, idx_v, rows_v):
        wid = lax.axis_index("s") * n_cores + lax.axis_index("c")
        c0 = wid * per_w
        pltpu.sync_copy(slots_hbm.at[wid], idx_v)

        @pl.loop(0, per_w)
        def _(c):
            row0 = pl.multiple_of((c0 + c) * SC_CH, SC_CH)
            pltpu.sync_copy(ys_hbm.at[idx_v.at[c]], rows_v)
            pltpu.sync_copy(rows_v, out_hbm.at[pl.ds(row0, SC_CH)])

    return k(ys, slots)


W_PARTS = 2


def _expert_body(t0_ref, nt_ref, xs_hbm, wg_hbm, wu_hbm, wd_hbm, ys_hbm,
                 wg_ref, wu_ref, wd_ref, xb_ref, yb_ref, wgb_ref, wub_ref, wdb_ref, wsem, xsem, ysem):
    e = pl.program_id(0)
    n = nt_ref[e]
    g0 = t0_ref[e]
    ws = e & 1

    def w_copies(ex, slot):
        out = []
        for hbm, buf in ((wg_hbm, wg_ref), (wu_hbm, wu_ref), (wd_hbm, wd_ref)):
            rb = buf.shape[1] // W_PARTS
            for p in range(W_PARTS):
                out.append((pltpu.make_async_copy(hbm.at[ex, pl.ds(p * rb, rb)],
                                                  buf.at[slot, pl.ds(p * rb, rb)], wsem.at[slot]), p))
        return out

    @pl.when(e == 0)
    def _():
        for cp, p in w_copies(0, 0):
            cp.start(priority=p)

    @pl.when(e + 1 < MOE_EXPERTS)
    def _():
        for cp, p in w_copies(e + 1, 1 - ws):
            cp.start(priority=p)

    for cp, _ in w_copies(e, ws):
        cp.wait()

    def rows(j):
        return pl.ds(pl.multiple_of((g0 + j) * TME, TME), TME)

    def x_copy(j, s):
        return pltpu.make_async_copy(xs_hbm.at[rows(j)], xb_ref.at[s], xsem.at[s])

    def y_copy(j, s):
        return pltpu.make_async_copy(yb_ref.at[s], ys_hbm.at[rows(j)], ysem.at[s])

    @pl.when(n > 0)
    def _():
        x_copy(0, 0).start()
        wgb_ref[...] = wg_ref[ws].astype(BF16)
        wub_ref[...] = wu_ref[ws].astype(BF16)
        wdb_ref[...] = wd_ref[ws].astype(BF16)

        def tile(j, c):
            s = j & 1
            x_copy(j, s).wait()
            @pl.when(j + 1 < n)
            def _():
                x_copy(j + 1, 1 - s).start()
            @pl.when(j >= 2)
            def _():
                y_copy(j - 2, s).wait()
            x = _unpack_pairs(xb_ref[s], BF16)
            hg = jnp.dot(x, wgb_ref[...], preferred_element_type=F32)
            hu = jnp.dot(x, wub_ref[...], preferred_element_type=F32)
            act = (hg * _sigmoid(hg)) * hu
            yb_ref[s] = _pack_pairs(jnp.dot(act.astype(BF16), wdb_ref[...], preferred_element_type=F32))
            y_copy(j, s).start()
            return c
        lax.fori_loop(0, n, tile, 0)

        @pl.when(n >= 2)
        def _():
            y_copy(n - 2, n & 1).wait()
        y_copy(n - 1, (n - 1) & 1).wait()


def _experts(tile0, tiles, xs, w_eg, w_eu, w_ed):
    any_spec = pl.BlockSpec(memory_space=pl.ANY)
    grid_spec = pltpu.PrefetchScalarGridSpec(
        num_scalar_prefetch=2,
        grid=(MOE_EXPERTS,),
        in_specs=[any_spec] * 4,
        out_specs=any_spec,
        scratch_shapes=[
            pltpu.VMEM((2, D_MODEL, MOE_FF), F32),
            pltpu.VMEM((2, D_MODEL, MOE_FF), F32),
            pltpu.VMEM((2, MOE_FF, D_MODEL), F32),
            pltpu.VMEM((2, TME, D_PACK), U32),
            pltpu.VMEM((2, TME, D_PACK), U32),
            pltpu.VMEM((D_MODEL, MOE_FF), BF16),
            pltpu.VMEM((D_MODEL, MOE_FF), BF16),
            pltpu.VMEM((MOE_FF, D_MODEL), BF16),
            pltpu.SemaphoreType.DMA((2,)),
            pltpu.SemaphoreType.DMA((2,)),
            pltpu.SemaphoreType.DMA((2,)),
        ],
    )
    return pl.pallas_call(
        _expert_body,
        grid_spec=grid_spec,
        out_shape=jax.ShapeDtypeStruct((N_SLOTS, D_PACK), U32),
        compiler_params=_cparams(1),
        name="experts",
    )(tile0, tiles, xs, w_eg, w_eu, w_ed)


FN_TM = 256
FN_PROMPT_BLOCKS = T_PROMPT // FN_TM


def _final_body(h_ref, y0_ref, y1_ref, rt_ref, g_ref, op_ref, os_ref):
    i = pl.program_id(0)
    rt = rt_ref[...]
    z = (h_ref[...] + rt[:, 0:1] * _unpack_pairs(y0_ref[...], F32)
         + rt[:, 1:2] * _unpack_pairs(y1_ref[...], F32))
    inv = lax.rsqrt(jnp.mean(z * z, axis=-1, keepdims=True) + EPS)
    out = (z * inv) * g_ref[...]
    @pl.when(i < FN_PROMPT_BLOCKS)
    def _():
        op_ref[...] = out
    @pl.when(i >= FN_PROMPT_BLOCKS)
    def _():
        os_ref[...] = out


def _final(h, y, route, g_final):
    n = T_ALL // FN_TM
    npb = FN_PROMPT_BLOCKS
    yoff = T_ALL // FN_TM
    return pl.pallas_call(
        _final_body,
        grid=(n,),
        in_specs=[
            pl.BlockSpec((FN_TM, D_MODEL), lambda i: (i, 0)),
            pl.BlockSpec((FN_TM, D_PACK), lambda i: (i, 0)),
            pl.BlockSpec((FN_TM, D_PACK), lambda i: (yoff + i, 0)),
            pl.BlockSpec((FN_TM, ROUTE_LANES), lambda i: (i, 0)),
            pl.BlockSpec((1, D_MODEL), lambda i: (0, 0)),
        ],
        out_specs=(
            pl.BlockSpec((FN_TM, D_MODEL), lambda i: (jnp.minimum(i, npb - 1), 0)),
            pl.BlockSpec((FN_TM, D_MODEL), lambda i: (jnp.maximum(i - npb, 0), 0)),
        ),
        out_shape=(
            jax.ShapeDtypeStruct((T_PROMPT, D_MODEL), F32),
            jax.ShapeDtypeStruct((T_SAMPLE, D_MODEL), F32),
        ),
        compiler_params=_cparams(1),
        name="final",
    )(h, y, y, route, g_final)


def _dispatch_plan(route, cnt):
    counts = cnt[0, EXP_LANE0:EXP_LANE0 + MOE_EXPERTS].astype(I32)
    tiles = (counts + (TME - 1)) // TME
    cumt = jnp.cumsum(tiles)
    pad_off = (cumt - tiles) * TME
    rank = route[:, 2:4].astype(I32)
    eid = route[:, 4:6].astype(I32)
    onehot = eid[..., None] == jnp.arange(MOE_EXPERTS, dtype=I32)
    slots = (rank + jnp.sum(jnp.where(onehot, pad_off, 0), axis=-1)).T
    return slots, cumt - tiles, tiles


def kernel(x_prompt, x_sample, state_pool, state_ssm_re, state_ssm_im, g_mix, w_in, w_pool,
           pool_scale, ssm_a_re, ssm_a_im, ssm_log_dt, ssm_b_re, ssm_b_im, ssm_c_re, ssm_c_im,
           ssm_d, w_glu_a, w_glu_b, w_out, g_ffn, w_router_group, b_router_group,
           w_router_expert, b_router_expert, w_exp_gate, w_exp_up, w_exp_down, g_final):
    l = 0
    xp = x_prompt.reshape(T_PROMPT, D_MODEL)
    xs = x_sample.transpose(1, 0, 2).reshape(T_SAMPLE, D_MODEL)
    w_in_bf = w_in[l].astype(BF16)
    w_pool_bf = w_pool[l].astype(BF16)
    wa_bf = w_glu_a[l].astype(BF16)
    wb_bf = w_glu_b[l].astype(BF16)
    wo_bf = w_out[l].astype(BF16)
    g_mix2 = g_mix[l].reshape(1, D_MODEL)
    scale2 = pool_scale[l].reshape(1, D_MODEL)

    u, gates = _inproj(xp, g_mix2, w_in_bf, 0)
    u, gates = _inproj(xs, g_mix2, w_in_bf, T_PROMPT // IN_TM, dst=(u, gates))

    y_pool, pool_tail = _pool_prompt(u, w_pool_bf, scale2)
    hist_tm = state_pool[l].transpose(1, 0, 2)
    y_pool = _pool_sample(u, hist_tm, w_pool_bf, scale2, y_pool)

    tables = _ssm_tables(ssm_a_re[l], ssm_a_im[l], ssm_log_dt[l], ssm_b_re[l], ssm_b_im[l],
                         ssm_c_re[l], ssm_c_im[l], ssm_d[l])
    h0r = state_ssm_re[l].reshape(DEC_BATCH, SSM_GROUPS * SSM_STATE)
    h0i = state_ssm_im[l].reshape(DEC_BATCH, SSM_GROUPS * SSM_STATE)
    y_act, h_prompt, hs_re, hs_im = _ssm(u, h0r, h0i, tables)

    w_r = jnp.zeros((D_MODEL, ROUTE_LANES), F32)
    w_r = w_r.at[:, :MOE_GROUPS].set(w_router_group[l])
    w_r = w_r.at[:, EXP_LANE0:EXP_LANE0 + MOE_EXPERTS].set(w_router_expert[l])
    wr_hi = w_r.astype(BF16)
    wr_lo = (w_r - wr_hi.astype(F32)).astype(BF16)
    b_r = jnp.zeros((1, ROUTE_LANES), F32)
    b_r = b_r.at[0, :MOE_GROUPS].set(b_router_group[l])
    b_r = b_r.at[0, EXP_LANE0:EXP_LANE0 + MOE_EXPERTS].set(b_router_expert[l])

    h, tn, route, cnt = _postmix(y_act, gates, y_pool, xp, xs, wa_bf, wb_bf, wo_bf,
                                 g_ffn[l].reshape(1, D_MODEL), wr_hi, wr_lo, b_r)
    slots, tile0, tiles = _dispatch_plan(route, cnt)
    xs_sorted = _sc_dispatch(tn, slots)
    ys_sorted = _experts(tile0, tiles, xs_sorted, w_exp_gate[l], w_exp_up[l], w_exp_down[l])
    y = _sc_collect(ys_sorted, slots)
    yp, ys = _final(h, y, route, g_final.reshape(1, D_MODEL))

    y_prompt = yp.reshape(BATCH, SEQ, D_MODEL)
    y_sample = ys.reshape(DEC_SEQ, DEC_BATCH, D_MODEL).transpose(1, 0, 2)
    new_pool_prompt = pool_tail[:, HIST - POOL_BUF:, :][None]
    us = u[T_PROMPT:T_ALL, :POOL_WIDTH].reshape(DEC_SEQ, DEC_BATCH, POOL_WIDTH).transpose(1, 0, 2)
    new_pool_sample = jnp.concatenate([state_pool[l][:, DEC_SEQ:, :], us], axis=1)[None]
    hp = h_prompt.reshape(BATCH, N_OCT, 2, OCT_GROUPS, SSM_STATE).transpose(2, 0, 1, 3, 4)
    hp = hp.reshape(2, BATCH, SSM_GROUPS, SSM_STATE)
    shp = (1, DEC_BATCH, SSM_GROUPS, SSM_STATE)
    return (y_prompt, y_sample, new_pool_prompt, hp[0][None], hp[1][None], new_pool_sample,
            hs_re.reshape(shp), hs_im.reshape(shp))
```

```python
import functools
import math

import jax
import jax.numpy as jnp
from jax import lax
from jax.experimental import pallas as pl
from jax.experimental.pallas import tpu as pltpu
from jax.experimental.pallas import tpu_sc as plsc

F32 = jnp.float32
BF16 = jnp.bfloat16
I32 = jnp.int32
U32 = jnp.uint32

D_MODEL = 2048
BATCH = 4
SEQ = 2048
DEC_BATCH = 128
DEC_SEQ = 8
PAST_LEN = 16384
POOL_WIDTH = D_MODEL // 2
POOL_WINDOWS = (2, 4, 8, 16)
POOL_GROUPS = len(POOL_WINDOWS)
POOL_GROUP_CH = POOL_WIDTH // POOL_GROUPS
POOL_OUT_CH = D_MODEL // POOL_GROUPS
POOL_BUF = max(POOL_WINDOWS) - 1
SSM_WIDTH = D_MODEL // 2
SSM_GROUP_CH = 16
SSM_GROUPS = SSM_WIDTH // SSM_GROUP_CH
SSM_STATE = 64
IN_WIDTH = POOL_WIDTH + SSM_WIDTH + 2 * D_MODEL
D_PACK = D_MODEL // 2
MOE_GROUPS = 4
MOE_EPG = 8
MOE_EXPERTS = MOE_GROUPS * MOE_EPG
MOE_FF = D_MODEL // 4
EPS = 1e-6

T_PROMPT = BATCH * SEQ
T_SAMPLE = DEC_BATCH * DEC_SEQ
T_ALL = T_PROMPT + T_SAMPLE
T_PAD = (BATCH + 1) * SEQ

LANES = 128
SUBLANES = 8
VMEM_LIMIT = 56 * 1024 * 1024

CHUNK = 8
OCT = LANES
N_OCT = SSM_WIDTH // OCT
OCT_GROUPS = OCT // SSM_GROUP_CH
OCT_STATES = OCT_GROUPS * SSM_STATE
CW = CHUNK * OCT
SW = 2 * OCT_STATES

ROUTE_LANES = LANES
EXP_LANE0 = MOE_GROUPS
N_ASSIGN = 2 * T_ALL
TME = 256
N_ITEMS_MAX = N_ASSIGN // TME + MOE_EXPERTS
N_SLOTS = N_ITEMS_MAX * TME


def _cparams(n_axes):
    return pltpu.CompilerParams(dimension_semantics=("arbitrary",) * n_axes,
                                vmem_limit_bytes=VMEM_LIMIT)


def _sigmoid(x):
    return 1.0 / (1.0 + jnp.exp(-x))


def _pack_pairs(x):
    c = x.shape[1] // 2
    hi = lax.bitcast_convert_type(x[:, :c].astype(BF16).astype(F32), U32)
    lo = lax.bitcast_convert_type(x[:, c:].astype(BF16).astype(F32), U32)
    return hi | (lo >> 16)


def _unpack_pairs(u, dtype):
    hi = lax.bitcast_convert_type(u & jnp.uint32(0xFFFF0000), F32)
    lo = lax.bitcast_convert_type(u << 16, F32)
    return jnp.concatenate([hi, lo], axis=1).astype(dtype)


def _gelu_tanh(x):
    c = math.sqrt(2.0 / math.pi)
    return 0.5 * x * (1.0 + jnp.tanh(c * (x + 0.044715 * (x * x * x))))


IN_TM = 1024
IN_TN = 1024
U_WIDTH = POOL_WIDTH + SSM_WIDTH
GATE_WIDTH = 2 * D_MODEL
IN_U_STEPS = U_WIDTH // IN_TN


def _inproj_body(x_ref, g_ref, w_ref, *rest):
    u_ref, gate_ref, xn_ref = rest[-3:]
    j = pl.program_id(1)
    @pl.when(j == 0)
    def _():
        x = x_ref[...]
        inv = lax.rsqrt(jnp.mean(x * x, axis=-1, keepdims=True) + EPS)
        xn_ref[...] = ((x * inv) * g_ref[...]).astype(BF16)
    acc = jnp.dot(xn_ref[...], w_ref[...], preferred_element_type=F32)
    @pl.when(j < IN_U_STEPS)
    def _():
        u_ref[...] = acc
    @pl.when(j >= IN_U_STEPS)
    def _():
        gate_ref[...] = _sigmoid(acc).astype(BF16)


def _inproj(x, g, w_bf, row_block0, dst=None):
    n_i = x.shape[0] // IN_TM
    in_specs = [
        pl.BlockSpec((IN_TM, D_MODEL), lambda i, j: (i, 0)),
        pl.BlockSpec((1, D_MODEL), lambda i, j: (0, 0)),
        pl.BlockSpec((D_MODEL, IN_TN), lambda i, j: (0, j)),
    ]
    args = [x, g, w_bf]
    aliases = {}
    if dst is not None:
        in_specs += [pl.BlockSpec(memory_space=pl.ANY)] * 2
        args += list(dst)
        aliases = {3: 0, 4: 1}
    return pl.pallas_call(
        _inproj_body,
        grid=(n_i, IN_WIDTH // IN_TN),
        in_specs=in_specs,
        out_specs=(
            pl.BlockSpec((IN_TM, IN_TN), lambda i, j: (i + row_block0, jnp.minimum(j, IN_U_STEPS - 1))),
            pl.BlockSpec((IN_TM, IN_TN), lambda i, j: (i + row_block0, jnp.maximum(j - IN_U_STEPS, 0))),
        ),
        out_shape=(
            jax.ShapeDtypeStruct((T_PAD, U_WIDTH), F32),
            jax.ShapeDtypeStruct((T_ALL, GATE_WIDTH), BF16),
        ),
        scratch_shapes=[pltpu.VMEM((IN_TM, D_MODEL), BF16)],
        input_output_aliases=aliases,
        compiler_params=_cparams(2),
        name="inproj",
    )(*args)


PP_TM = 512
HIST = 16


def _pool_project(pooled_g, g, w_ref, sc_ref, o_ref):
    y = jnp.dot(pooled_g.astype(BF16), w_ref[g], preferred_element_type=F32)
    lo, hi = g * POOL_OUT_CH, (g + 1) * POOL_OUT_CH
    o_ref[:, lo:hi] = (y * sc_ref[:, lo:hi]).astype(o_ref.dtype)


def _pool_prompt_body(u_ref, w_ref, sc_ref, o_ref, tail_ref, hist_ref):
    i = pl.program_id(1)
    @pl.when(i == 0)
    def _():
        hist_ref[...] = jnp.zeros_like(hist_ref)
    u = u_ref[...]
    ext = jnp.concatenate([hist_ref[...], u], axis=0)
    hist_ref[...] = u[PP_TM - HIST:, :]
    tail_ref[...] = u[PP_TM - HIST:, :]
    pos = i * PP_TM + lax.broadcasted_iota(I32, (PP_TM, 1), 0)
    for g, w in enumerate(POOL_WINDOWS):
        lo, hi = g * POOL_GROUP_CH, (g + 1) * POOL_GROUP_CH
        s = ext[:, lo:hi]
        d = 1
        while d < w:
            s = s + pltpu.roll(s, d, axis=0)
            d *= 2
        cnt = jnp.minimum(w, pos + 1).astype(F32)
        pooled = s[HIST:, :] / cnt - u[:, lo:hi]
        _pool_project(pooled, g, w_ref, sc_ref, o_ref)


def _pool_prompt(u, w_pool_bf, pool_scale):
    n_i = SEQ // PP_TM
    return pl.pallas_call(
        _pool_prompt_body,
        grid=(BATCH, n_i),
        in_specs=[
            pl.BlockSpec((PP_TM, POOL_WIDTH), lambda b, i: (b * n_i + i, 0)),
            pl.BlockSpec((POOL_GROUPS, POOL_GROUP_CH, POOL_OUT_CH), lambda b, i: (0, 0, 0)),
            pl.BlockSpec((1, D_MODEL), lambda b, i: (0, 0)),
        ],
        out_specs=(
            pl.BlockSpec((PP_TM, D_MODEL), lambda b, i: (b * n_i + i, 0)),
            pl.BlockSpec((None, HIST, POOL_WIDTH), lambda b, i: (b, 0, 0)),
        ),
        out_shape=(
            jax.ShapeDtypeStruct((T_ALL, D_MODEL), BF16),
            jax.ShapeDtypeStruct((BATCH, HIST, POOL_WIDTH), F32),
        ),
        scratch_shapes=[pltpu.VMEM((HIST, POOL_WIDTH), F32)],
        compiler_params=_cparams(2),
        name="pool_prompt",
    )(u, w_pool_bf, pool_scale)


def _pool_sample_body(u_ref, hist_ref, w_ref, sc_ref, _dst, o_ref):
    rows = [hist_ref[k] for k in range(POOL_BUF)]
    rows += [u_ref[DEC_BATCH * t:DEC_BATCH * (t + 1), :] for t in range(DEC_SEQ)]
    n = len(rows)
    for g, w in enumerate(POOL_WINDOWS):
        lo, hi = g * POOL_GROUP_CH, (g + 1) * POOL_GROUP_CH
        f = [r[:, lo:hi] for r in rows]
        cur = f
        d = 1
        while d < w:
            cur = [cur[k] + cur[k - d] if k - d >= 0 else cur[k] for k in range(n)]
            d *= 2
        pooled = jnp.concatenate(
            [cur[POOL_BUF + t] / float(w) - f[POOL_BUF + t] for t in range(DEC_SEQ)], axis=0)
        _pool_project(pooled, g, w_ref, sc_ref, o_ref)


def _pool_sample(u, hist_tm, w_pool_bf, pool_scale, y_pool):
    blk = T_PROMPT // T_SAMPLE
    return pl.pallas_call(
        _pool_sample_body,
        grid=(1,),
        in_specs=[
            pl.BlockSpec((T_SAMPLE, POOL_WIDTH), lambda i: (blk, 0)),
            pl.BlockSpec((POOL_BUF, DEC_BATCH, POOL_WIDTH), lambda i: (0, 0, 0)),
            pl.BlockSpec((POOL_GROUPS, POOL_GROUP_CH, POOL_OUT_CH), lambda i: (0, 0, 0)),
            pl.BlockSpec((1, D_MODEL), lambda i: (0, 0)),
            pl.BlockSpec(memory_space=pl.ANY),
        ],
        out_specs=pl.BlockSpec((T_SAMPLE, D_MODEL), lambda i: (blk, 0)),
        out_shape=jax.ShapeDtypeStruct((T_ALL, D_MODEL), BF16),
        input_output_aliases={4: 0},
        compiler_params=_cparams(1),
        name="pool_sample",
    )(u, hist_tm, w_pool_bf, pool_scale, y_pool)


def _ssm_tables(a_re, a_im, log_dt, b_re, b_im, c_re, c_im, d_skip):
    dt = jnp.exp(log_dt)[:, None]
    lr, li = a_re, a_im
    ab_re = jnp.exp(lr * dt) * jnp.cos(li * dt)
    ab_im = jnp.exp(lr * dt) * jnp.sin(li * dt)
    den = lr * lr + li * li
    nr, ni = ab_re - 1.0, ab_im
    q_re = (nr * lr + ni * li) / den
    q_im = (ni * lr - nr * li) / den
    bb_re = q_re[..., None] * b_re - q_im[..., None] * b_im
    bb_im = q_re[..., None] * b_im + q_im[..., None] * b_re

    def lam_rows(ks):
        k = jnp.asarray(ks, F32)[:, None, None]
        m = jnp.exp(k * lr * dt)
        re = (m * jnp.cos(k * li * dt)).reshape(len(ks), N_OCT, OCT_STATES)
        im = (m * jnp.sin(k * li * dt)).reshape(len(ks), N_OCT, OCT_STATES)
        return jnp.concatenate([re, im], axis=-1).transpose(1, 0, 2)

    def compact(re, im):
        v = jnp.concatenate([re, im], axis=-1)
        return v.reshape(N_OCT, OCT, 2 * SSM_STATE)

    bbc = compact(jnp.swapaxes(bb_re, 1, 2), jnp.swapaxes(bb_im, 1, 2))
    ccc = compact(c_re, c_im)
    pw = lam_rows(list(range(2 * SUBLANES)))
    r = jnp.arange(SUBLANES)[None, :, None]
    parts = [jnp.where(r >= dd, lam_rows([CHUNK * dd]), 0.0) for dd in (1, 2, 4)]
    parts.append(lam_rows([CHUNK * kk for kk in range(1, SUBLANES + 1)]))
    tab = jnp.concatenate(parts, axis=1)
    dsk = d_skip.reshape(N_OCT, 1, OCT)
    return bbc, ccc, pw, tab, dsk


def _split_bf16(x):
    hi = x.astype(BF16)
    return hi, (x - hi.astype(F32)).astype(BF16)


def _dot_nt(a, b):
    return lax.dot_general(a, b, (((1,), (1,)), ((), ())), preferred_element_type=F32)


def _build_weights(bbc_ref, ccc_ref, pw_ref, f_ref, gt_ref, m_ref):
    row_gi = lax.broadcasted_iota(I32, (OCT, 1), 0) >> 4
    col = lax.broadcasted_iota(I32, (1, SW), 1)
    col_gi = (col >> 6) & 7
    src = ((col >> 9) << 6) | (col & 63)
    k128 = lax.broadcasted_iota(I32, (2 * SSM_STATE, 1), 0)
    spread = jnp.where(k128 == src, 1.0, 0.0).astype(BF16)
    diag = row_gi == col_gi

    def expand(c_ref):
        hi, lo = _split_bf16(c_ref[...])
        d = (jnp.dot(hi, spread, preferred_element_type=F32)
             + jnp.dot(lo, spread, preferred_element_type=F32))
        d = jnp.where(diag, d, 0.0)
        return d[:, :OCT_STATES], d[:, OCT_STATES:]

    br, bi = expand(bbc_ref)
    cr, ci = expand(ccc_ref)
    chi_r, clo_r = _split_bf16(cr)
    chi_i, clo_i = _split_bf16(ci)

    def lam(k):
        return pw_ref[k:k + 1, :OCT_STATES], pw_ref[k:k + 1, OCT_STATES:]

    def dot3(a, bhi, blo):
        ahi, alo = _split_bf16(a)
        return _dot_nt(ahi, bhi) + _dot_nt(alo, bhi) + _dot_nt(ahi, blo)

    lags = []
    for k in range(CHUNK):
        pr, pi_ = lam(k)
        fr, fi = _cmul(br, bi, pr, pi_)
        s = CHUNK - 1 - k
        f_ref[s * OCT:(s + 1) * OCT, :] = jnp.concatenate([fr, fi], axis=1).astype(BF16)
        lags.append((dot3(fr, chi_r, clo_r) - dot3(fi, chi_i, clo_i)).astype(BF16))
        pr, pi_ = lam(k + 1)
        gr, gi = _cmul(cr, ci, pr, pi_)
        gt_ref[k * OCT:(k + 1) * OCT, :] = jnp.concatenate([gr, -gi], axis=1).astype(BF16)
    zero = jnp.zeros((OCT, OCT), BF16)
    for s in range(CHUNK):
        for t in range(CHUNK):
            m_ref[s * OCT:(s + 1) * OCT, t * OCT:(t + 1) * OCT] = lags[t - s] if t >= s else zero


def _cmul(ar, ai, br, bi):
    return ar * br - ai * bi, ar * bi + ai * br


def _chunk_scan(sloc, tab_ref):
    R = sloc.shape[0]
    nb = R // SUBLANES
    sr, si = sloc[:, :OCT_STATES], sloc[:, OCT_STATES:]
    rowi = lax.broadcasted_iota(I32, (R, 1), 0)
    tr = jnp.where(rowi == 0, 0.0, pltpu.roll(sr, 1, axis=0))
    ti = jnp.where(rowi == 0, 0.0, pltpu.roll(si, 1, axis=0))
    for lvl, d in enumerate((1, 2, 4)):
        mr = tab_ref[lvl * SUBLANES:(lvl + 1) * SUBLANES, :OCT_STATES]
        mi = tab_ref[lvl * SUBLANES:(lvl + 1) * SUBLANES, OCT_STATES:]
        mr = jnp.concatenate([mr] * nb, axis=0)
        mi = jnp.concatenate([mi] * nb, axis=0)
        pr, pi_ = _cmul(mr, mi, pltpu.roll(tr, d, axis=0), pltpu.roll(ti, d, axis=0))
        tr, ti = tr + pr, ti + pi_
    pwr = tab_ref[3 * SUBLANES:4 * SUBLANES, :OCT_STATES]
    pwi = tab_ref[3 * SUBLANES:4 * SUBLANES, OCT_STATES:]
    cr = jnp.zeros((1, OCT_STATES), F32)
    ci = jnp.zeros((1, OCT_STATES), F32)
    out_r, out_i = [], []
    for k in range(nb):
        ar = tr[k * SUBLANES:(k + 1) * SUBLANES, :]
        ai = ti[k * SUBLANES:(k + 1) * SUBLANES, :]
        pr, pi_ = _cmul(pwr, pwi, jnp.broadcast_to(cr, ar.shape), jnp.broadcast_to(ci, ai.shape))
        hr, hi = ar + pr, ai + pi_
        out_r.append(hr)
        out_i.append(hi)
        cr, ci = hr[SUBLANES - 1:, :], hi[SUBLANES - 1:, :]
    hin = jnp.concatenate([jnp.concatenate(out_r, axis=0), jnp.concatenate(out_i, axis=0)], axis=1)
    lr, li = pwr[0:1, :], pwi[0:1, :]
    fr, fi = _cmul(lr, li, cr, ci)
    fin = jnp.concatenate([fr + sr[R - 1:, :], fi + si[R - 1:, :]], axis=1)
    return hin, fin


def _ssm_body(u_ref, h0r_ref, h0i_ref, bbc_ref, ccc_ref, pw_ref, tab_ref, d_ref,
              y_ref, hout_ref, hr_ref, hi_ref, f_ref, gt_ref, m_ref):
    b = pl.program_id(1)

    @pl.when(b == 0)
    def _():
        _build_weights(bbc_ref, ccc_ref, pw_ref, f_ref, gt_ref, m_ref)

    def outputs(xs, xb, hin_bf):
        y = (jnp.dot(xb, m_ref[...], preferred_element_type=F32) + _dot_nt(hin_bf, gt_ref[...]))
        return [_gelu_tanh(y[:, t * OCT:(t + 1) * OCT] + d_ref[...] * xs[t]) for t in range(CHUNK)]

    @pl.when(b < BATCH)
    def _():
        R = SEQ // CHUNK
        xs = [u_ref[pl.ds(s, R, stride=CHUNK), :] for s in range(CHUNK)]
        xb = jnp.concatenate(xs, axis=1).astype(BF16)
        sloc = jnp.dot(xb, f_ref[...], preferred_element_type=F32)
        hin, fin = _chunk_scan(sloc, tab_ref)
        for t, yt in enumerate(outputs(xs, xb, hin.astype(BF16))):
            y_ref[pl.ds(t, R, stride=CHUNK), :] = yt
        hout_ref[...] = fin

    @pl.when(b == BATCH)
    def _():
        B = DEC_BATCH
        xs = [u_ref[B * s:B * (s + 1), :] for s in range(CHUNK)]
        xb = jnp.concatenate(xs, axis=1).astype(BF16)
        sloc = jnp.dot(xb, f_ref[...], preferred_element_type=F32)
        h0r, h0i = h0r_ref[...], h0i_ref[...]
        hin = jnp.concatenate([h0r, h0i], axis=1).astype(BF16)
        for t, yt in enumerate(outputs(xs, xb, hin)):
            y_ref[B * t:B * (t + 1), :] = yt
        lr = tab_ref[3 * SUBLANES:3 * SUBLANES + 1, :OCT_STATES]
        li = tab_ref[3 * SUBLANES:3 * SUBLANES + 1, OCT_STATES:]
        nr, ni = _cmul(lr, li, h0r, h0i)
        hr_ref[...] = nr + sloc[:, :OCT_STATES]
        hi_ref[...] = ni + sloc[:, OCT_STATES:]


def _ssm(u, h0r, h0i, tables):
    col0 = POOL_WIDTH // OCT
    im3 = lambda o, b: (o, 0, 0)
    st_spec = pl.BlockSpec((DEC_BATCH, OCT_STATES), lambda o, b: (0, o))
    return pl.pallas_call(
        _ssm_body,
        grid=(N_OCT, BATCH + 1),
        in_specs=[
            pl.BlockSpec((SEQ, OCT), lambda o, b: (b, col0 + o)), st_spec, st_spec,
            pl.BlockSpec((None, OCT, 2 * SSM_STATE), im3),
            pl.BlockSpec((None, OCT, 2 * SSM_STATE), im3),
            pl.BlockSpec((None, 2 * SUBLANES, SW), im3),
            pl.BlockSpec((None, 4 * SUBLANES, SW), im3),
            pl.BlockSpec((None, 1, OCT), im3),
        ],
        out_specs=(
            pl.BlockSpec((SEQ, OCT), lambda o, b: (b, o)),
            pl.BlockSpec((None, 1, SW), lambda o, b: (jnp.minimum(b, BATCH - 1) * N_OCT + o, 0, 0)),
            st_spec, st_spec,
        ),
        out_shape=(
            jax.ShapeDtypeStruct((T_PAD, SSM_WIDTH), F32),
            jax.ShapeDtypeStruct((BATCH * N_OCT, 1, SW), F32),
            jax.ShapeDtypeStruct((DEC_BATCH, SSM_GROUPS * SSM_STATE), F32),
            jax.ShapeDtypeStruct((DEC_BATCH, SSM_GROUPS * SSM_STATE), F32),
        ),
        scratch_shapes=[pltpu.VMEM((CW, SW), BF16), pltpu.VMEM((CW, SW), BF16), pltpu.VMEM((CW, CW), BF16)],
        compiler_params=_cparams(2),
        name="ssm",
    )(u, h0r, h0i, *tables)


PM_TM = 256
PM_PROMPT_BLOCKS = T_PROMPT // PM_TM
PM_STEPS = T_ALL // PM_TM


def _route(logits, valid, cnt_ref):
    lane = lax.broadcasted_iota(I32, (PM_TM, ROUTE_LANES), 1)
    neg = jnp.float32(-jnp.inf)
    big = jnp.int32(1 << 20)
    is_g = lane < MOE_GROUPS
    gmax = jnp.max(jnp.where(is_g, logits, neg), axis=1, keepdims=True)
    g_idx = jnp.min(jnp.where(is_g & (logits == gmax), lane, big), axis=1, keepdims=True)
    g_den = jnp.sum(jnp.where(is_g, jnp.exp(logits - gmax), 0.0), axis=1, keepdims=True)
    g_val = 1.0 / g_den
    e_lane = lane - EXP_LANE0
    sel = (e_lane >= 0) & (e_lane < MOE_EXPERTS) & ((e_lane >> 3) == g_idx)
    m1 = jnp.max(jnp.where(sel, logits, neg), axis=1, keepdims=True)
    i1 = jnp.min(jnp.where(sel & (logits == m1), lane, big), axis=1, keepdims=True)
    sel2 = sel & (lane != i1)
    m2 = jnp.max(jnp.where(sel2, logits, neg), axis=1, keepdims=True)
    i2 = jnp.min(jnp.where(sel2 & (logits == m2), lane, big), axis=1, keepdims=True)
    e2 = jnp.exp(m2 - m1)
    w1 = g_val / (1.0 + e2)
    w2 = g_val * e2 / (1.0 + e2)
    oh1 = lane == i1
    oh2 = lane == i2
    oh = jnp.where(oh1 | oh2, valid, 0.0)
    rr = lax.broadcasted_iota(I32, (PM_TM, PM_TM), 0)
    cc = lax.broadcasted_iota(I32, (PM_TM, PM_TM), 1)
    tri = jnp.where(cc < rr, 1.0, 0.0).astype(BF16)
    base = cnt_ref[...] + jnp.dot(tri, oh.astype(BF16), preferred_element_type=F32)
    rank1 = jnp.sum(jnp.where(oh1, base, 0.0), axis=1, keepdims=True)
    rank2 = jnp.sum(jnp.where(oh2, base, 0.0), axis=1, keepdims=True)
    cnt_ref[...] = cnt_ref[...] + jnp.sum(oh, axis=0, keepdims=True)
    rt = jnp.where(lane == 0, w1, 0.0)
    rt = jnp.where(lane == 1, w2, rt)
    rt = jnp.where(lane == 2, rank1, rt)
    rt = jnp.where(lane == 3, rank2, rt)
    rt = jnp.where(lane == 4, (i1 - EXP_LANE0).astype(F32), rt)
    rt = jnp.where(lane == 5, (i2 - EXP_LANE0).astype(F32), rt)
    return rt


def _postmix_body(ya_ref, gp_ref, gs_ref, yp_ref, xp_ref, xs_ref, wa_ref, wb_ref, wo_ref,
                  gf_ref, wrh_ref, wrl_ref, br_ref, h_ref, tn_ref, rt_ref, cnt_out_ref,
                  cnt_ref, lg_ref):
    i = pl.program_id(0)
    @pl.when(i == 0)
    def _():
        cnt_ref[...] = jnp.zeros_like(cnt_ref)
        lg_ref[...] = jnp.zeros_like(lg_ref)
    prev_logits = lg_ref[...]

    ya = ya_ref[...].astype(BF16)
    a = jnp.dot(ya, wa_ref[...], preferred_element_type=F32)
    bg = jnp.dot(ya, wb_ref[...], preferred_element_type=F32)
    y_ssm = a * _sigmoid(bg)
    merged = (gp_ref[...].astype(F32) * yp_ref[...].astype(F32)
              + gs_ref[...].astype(F32) * y_ssm)
    x = jnp.where(jnp.minimum(i, PM_STEPS - 1) < PM_PROMPT_BLOCKS, xp_ref[...], xs_ref[...])
    h = x + jnp.dot(merged.astype(BF16), wo_ref[...], preferred_element_type=F32)
    h_ref[...] = h
    inv = lax.rsqrt(jnp.mean(h * h, axis=-1, keepdims=True) + EPS)
    tn = (h * inv) * gf_ref[...]
    tn_ref[...] = _pack_pairs(tn)
    t_hi = tn.astype(BF16)
    t_lo = (tn - t_hi.astype(F32)).astype(BF16)
    wrh = wrh_ref[...]
    lg_ref[...] = (jnp.dot(t_hi, wrh, preferred_element_type=F32)
                   + jnp.dot(t_lo, wrh, preferred_element_type=F32)
                   + jnp.dot(t_hi, wrl_ref[...], preferred_element_type=F32)) + br_ref[...]
    rt_ref[...] = _route(prev_logits, jnp.where(i > 0, 1.0, 0.0), cnt_ref)
    cnt_out_ref[...] = cnt_ref[...]


def _postmix(y_act, gates, y_pool, xp, xs, wa, wb, wo, g_ffn, wr_hi, wr_lo, b_r):
    npb = PM_PROMPT_BLOCKS
    const2 = lambda i: (0, 0)
    tile = lambda i: jnp.minimum(i, PM_STEPS - 1)
    return pl.pallas_call(
        _postmix_body,
        grid=(PM_STEPS + 1,),
        in_specs=[
            pl.BlockSpec((PM_TM, SSM_WIDTH), lambda i: (tile(i), 0)),
            pl.BlockSpec((PM_TM, D_MODEL), lambda i: (tile(i), 0)),
            pl.BlockSpec((PM_TM, D_MODEL), lambda i: (tile(i), 1)),
            pl.BlockSpec((PM_TM, D_MODEL), lambda i: (tile(i), 0)),
            pl.BlockSpec((PM_TM, D_MODEL), lambda i: (jnp.minimum(i, npb - 1), 0)),
            pl.BlockSpec((PM_TM, D_MODEL), lambda i: (jnp.maximum(tile(i) - npb, 0), 0)),
            pl.BlockSpec((SSM_WIDTH, D_MODEL), const2, pipeline_mode=pl.Buffered(1)),
            pl.BlockSpec((SSM_WIDTH, D_MODEL), const2, pipeline_mode=pl.Buffered(1)),
            pl.BlockSpec((D_MODEL, D_MODEL), const2, pipeline_mode=pl.Buffered(1)),
            pl.BlockSpec((1, D_MODEL), const2),
            pl.BlockSpec((D_MODEL, ROUTE_LANES), const2),
            pl.BlockSpec((D_MODEL, ROUTE_LANES), const2),
            pl.BlockSpec((1, ROUTE_LANES), const2),
        ],
        out_specs=(
            pl.BlockSpec((PM_TM, D_MODEL), lambda i: (tile(i), 0)),
            pl.BlockSpec((PM_TM, D_PACK), lambda i: (tile(i), 0)),
            pl.BlockSpec((PM_TM, ROUTE_LANES), lambda i: (jnp.maximum(i - 1, 0), 0)),
            pl.BlockSpec((1, ROUTE_LANES), const2),
        ),
        out_shape=(
            jax.ShapeDtypeStruct((T_ALL, D_MODEL), F32),
            jax.ShapeDtypeStruct((T_ALL, D_PACK), U32),
            jax.ShapeDtypeStruct((T_ALL, ROUTE_LANES), F32),
            jax.ShapeDtypeStruct((1, ROUTE_LANES), F32),
        ),
        scratch_shapes=[pltpu.VMEM((1, ROUTE_LANES), F32), pltpu.VMEM((PM_TM, ROUTE_LANES), F32)],
        compiler_params=_cparams(1),
        name="postmix",
    )(y_act, gates, gates, y_pool, xp, xs, wa, wb, wo, g_ffn, wr_hi, wr_lo, b_r)


SC_CH = 32


def _sc_workers():
    info = plsc.get_sparse_core_info()
    return info.num_cores, info.num_cores * info.num_subcores


def _sc_dispatch(tn, slots):
    n_cores, n_workers = _sc_workers()
    per_w = (T_ALL // SC_CH) // n_workers
    assert per_w * n_workers * SC_CH == T_ALL
    slots = slots.reshape(2, n_workers, per_w, SC_CH)

    @functools.partial(
        pl.kernel,
        mesh=plsc.VectorSubcoreMesh(core_axis_name="c", subcore_axis_name="s"),
        out_type=jax.ShapeDtypeStruct((N_SLOTS, D_PACK), U32),
        scratch_types=[pltpu.VMEM((2, per_w, SC_CH), I32), pltpu.VMEM((SC_CH, D_PACK), U32)],
    )
    def k(tn_hbm, slots_hbm, xs_hbm, idx_v, rows_v):
        wid = lax.axis_index("s") * n_cores + lax.axis_index("c")
        c0 = wid * per_w
        pltpu.sync_copy(slots_hbm.at[0, wid], idx_v.at[0])
        pltpu.sync_copy(slots_hbm.at[1, wid], idx_v.at[1])

        @pl.loop(0, per_w)
        def _(c):
            row0 = pl.multiple_of((c0 + c) * SC_CH, SC_CH)
            pltpu.sync_copy(tn_hbm.at[pl.ds(row0, SC_CH)], rows_v)
            pltpu.sync_copy(rows_v, xs_hbm.at[idx_v.at[0, c]])
            pltpu.sync_copy(rows_v, xs_hbm.at[idx_v.at[1, c]])

    return k(tn, slots)


def _sc_collect(ys, slots):
    n_cores, n_workers = _sc_workers()
    per_w = (N_ASSIGN // SC_CH) // n_workers
    assert per_w * n_workers * SC_CH == N_ASSIGN
    slots = slots.reshape(n_workers, per_w, SC_CH)

    @functools.partial(
        pl.kernel,
        mesh=plsc.VectorSubcoreMesh(core_axis_name="c", subcore_axis_name="s"),
        out_type=jax.ShapeDtypeStruct((N_ASSIGN, D_PACK), U32),
        scratch_types=[pltpu.VMEM((per_w, SC_CH), I32), pltpu.VMEM((SC_CH, D_PACK), U32)],
    )
    def k(ys_hbm, slots_hbm, out_hbm, idx_v, rows_v):
        wid = lax.axis_index("s") * n_cores + lax.axis_index("c")
        c0 = wid * per_w
        pltpu.sync_copy(slots_hbm.at[wid], idx_v)

        @pl.loop(0, per_w)
        def _(c):
            row0 = pl.multiple_of((c0 + c) * SC_CH, SC_CH)
            pltpu.sync_copy(ys_hbm.at[idx_v.at[c]], rows_v)
            pltpu.sync_copy(rows_v, out_hbm.at[pl.ds(row0, SC_CH)])

    return k(ys, slots)


W_PARTS = 2


def _expert_body(t0_ref, nt_ref, xs_hbm, wg_hbm, wu_hbm, wd_hbm, ys_hbm,
                 wg_ref, wu_ref, wd_ref, xb_ref, yb_ref, wgb_ref, wub_ref, wdb_ref, wsem, xsem, ysem):
    e = pl.program_id(0)
    n = nt_ref[e]
    g0 = t0_ref[e]
    ws = e & 1

    def w_copies(ex, slot):
        out = []
        for hbm, buf in ((wg_hbm, wg_ref), (wu_hbm, wu_ref), (wd_hbm, wd_ref)):
            rb = buf.shape[1] // W_PARTS
            for p in range(W_PARTS):
                out.append((pltpu.make_async_copy(hbm.at[ex, pl.ds(p * rb, rb)],
                                                  buf.at[slot, pl.ds(p * rb, rb)], wsem.at[slot]), p))
        return out

    @pl.when(e == 0)
    def _():
        for cp, p in w_copies(0, 0):
            cp.start(priority=p)

    @pl.when(e + 1 < MOE_EXPERTS)
    def _():
        for cp, p in w_copies(e + 1, 1 - ws):
            cp.start(priority=p)

    for cp, _ in w_copies(e, ws):
        cp.wait()

    def rows(j):
        return pl.ds(pl.multiple_of((g0 + j) * TME, TME), TME)

    def x_copy(j, s):
        return pltpu.make_async_copy(xs_hbm.at[rows(j)], xb_ref.at[s], xsem.at[s])

    def y_copy(j, s):
        return pltpu.make_async_copy(yb_ref.at[s], ys_hbm.at[rows(j)], ysem.at[s])

    @pl.when(n > 0)
    def _():
        x_copy(0, 0).start()
        wgb_ref[...] = wg_ref[ws].astype(BF16)
        wub_ref[...] = wu_ref[ws].astype(BF16)
        wdb_ref[...] = wd_ref[ws].astype(BF16)

        def tile(j, c):
            s = j & 1
            x_copy(j, s).wait()
            @pl.when(j + 1 < n)
            def _():
                x_copy(j + 1, 1 - s).start()
            @pl.when(j >= 2)
            def _():
                y_copy(j - 2, s).wait()
            x = _unpack_pairs(xb_ref[s], BF16)
            hg = jnp.dot(x, wgb_ref[...], preferred_element_type=F32)
            hu = jnp.dot(x, wub_ref[...], preferred_element_type=F32)
            act = (hg * _sigmoid(hg)) * hu
            yb_ref[s] = _pack_pairs(jnp.dot(act.astype(BF16), wdb_ref[...], preferred_element_type=F32))
            y_copy(j, s).start()
            return c
        lax.fori_loop(0, n, tile, 0)

        @pl.when(n >= 2)
        def _():
            y_copy(n - 2, n & 1).wait()
        y_copy(n - 1, (n - 1) & 1).wait()


def _experts(tile0, tiles, xs, w_eg, w_eu, w_ed):
    any_spec = pl.BlockSpec(memory_space=pl.ANY)
    grid_spec = pltpu.PrefetchScalarGridSpec(
        num_scalar_prefetch=2,
        grid=(MOE_EXPERTS,),
        in_specs=[any_spec] * 4,
        out_specs=any_spec,
        scratch_shapes=[
            pltpu.VMEM((2, D_MODEL, MOE_FF), F32),
            pltpu.VMEM((2, D_MODEL, MOE_FF), F32),
            pltpu.VMEM((2, MOE_FF, D_MODEL), F32),
            pltpu.VMEM((2, TME, D_PACK), U32),
            pltpu.VMEM((2, TME, D_PACK), U32),
            pltpu.VMEM((D_MODEL, MOE_FF), BF16),
            pltpu.VMEM((D_MODEL, MOE_FF), BF16),
            pltpu.VMEM((MOE_FF, D_MODEL), BF16),
            pltpu.SemaphoreType.DMA((2,)),
            pltpu.SemaphoreType.DMA((2,)),
            pltpu.SemaphoreType.DMA((2,)),
        ],
    )
    return pl.pallas_call(
        _expert_body,
        grid_spec=grid_spec,
        out_shape=jax.ShapeDtypeStruct((N_SLOTS, D_PACK), U32),
        compiler_params=_cparams(1),
        name="experts",
    )(tile0, tiles, xs, w_eg, w_eu, w_ed)


FN_TM = 256
FN_PROMPT_BLOCKS = T_PROMPT // FN_TM


def _final_body(h_ref, y0_ref, y1_ref, rt_ref, g_ref, op_ref, os_ref):
    i = pl.program_id(0)
    rt = rt_ref[...]
    z = (h_ref[...] + rt[:, 0:1] * _unpack_pairs(y0_ref[...], F32)
         + rt[:, 1:2] * _unpack_pairs(y1_ref[...], F32))
    inv = lax.rsqrt(jnp.mean(z * z, axis=-1, keepdims=True) + EPS)
    out = (z * inv) * g_ref[...]
    @pl.when(i < FN_PROMPT_BLOCKS)
    def _():
        op_ref[...] = out
    @pl.when(i >= FN_PROMPT_BLOCKS)
    def _():
        os_ref[...] = out


def _final(h, y, route, g_final):
    n = T_ALL // FN_TM
    npb = FN_PROMPT_BLOCKS
    yoff = T_ALL // FN_TM
    return pl.pallas_call(
        _final_body,
        grid=(n,),
        in_specs=[
            pl.BlockSpec((FN_TM, D_MODEL), lambda i: (i, 0)),
            pl.BlockSpec((FN_TM, D_PACK), lambda i: (i, 0)),
            pl.BlockSpec((FN_TM, D_PACK), lambda i: (yoff + i, 0)),
            pl.BlockSpec((FN_TM, ROUTE_LANES), lambda i: (i, 0)),
            pl.BlockSpec((1, D_MODEL), lambda i: (0, 0)),
        ],
        out_specs=(
            pl.BlockSpec((FN_TM, D_MODEL), lambda i: (jnp.minimum(i, npb - 1), 0)),
            pl.BlockSpec((FN_TM, D_MODEL), lambda i: (jnp.maximum(i - npb, 0), 0)),
        ),
        out_shape=(
            jax.ShapeDtypeStruct((T_PROMPT, D_MODEL), F32),
            jax.ShapeDtypeStruct((T_SAMPLE, D_MODEL), F32),
        ),
        compiler_params=_cparams(1),
        name="final",
    )(h, y, y, route, g_final)


def _dispatch_plan(route, cnt):
    counts = cnt[0, EXP_LANE0:EXP_LANE0 + MOE_EXPERTS].astype(I32)
    tiles = (counts + (TME - 1)) // TME
    cumt = jnp.cumsum(tiles)
    pad_off = (cumt - tiles) * TME
    rank = route[:, 2:4].astype(I32)
    eid = route[:, 4:6].astype(I32)
    onehot = eid[..., None] == jnp.arange(MOE_EXPERTS, dtype=I32)
    slots = (rank + jnp.sum(jnp.where(onehot, pad_off, 0), axis=-1)).T
    return slots, cumt - tiles, tiles


def kernel(x_prompt, x_sample, state_pool, state_ssm_re, state_ssm_im, g_mix, w_in, w_pool,
           pool_scale, ssm_a_re, ssm_a_im, ssm_log_dt, ssm_b_re, ssm_b_im, ssm_c_re, ssm_c_im,
           ssm_d, w_glu_a, w_glu_b, w_out, g_ffn, w_router_group, b_router_group,
           w_router_expert, b_router_expert, w_exp_gate, w_exp_up, w_exp_down, g_final):
    l = 0
    xp = x_prompt.reshape(T_PROMPT, D_MODEL)
    xs = x_sample.transpose(1, 0, 2).reshape(T_SAMPLE, D_MODEL)
    w_in_bf = w_in[l].astype(BF16)
    w_pool_bf = w_pool[l].astype(BF16)
    wa_bf = w_glu_a[l].astype(BF16)
    wb_bf = w_glu_b[l].astype(BF16)
    wo_bf = w_out[l].astype(BF16)
    g_mix2 = g_mix[l].reshape(1, D_MODEL)
    scale2 = pool_scale[l].reshape(1, D_MODEL)

    u, gates = _inproj(xp, g_mix2, w_in_bf, 0)
    u, gates = _inproj(xs, g_mix2, w_in_bf, T_PROMPT // IN_TM, dst=(u, gates))

    y_pool, pool_tail = _pool_prompt(u, w_pool_bf, scale2)
    hist_tm = state_pool[l].transpose(1, 0, 2)
    y_pool = _pool_sample(u, hist_tm, w_pool_bf, scale2, y_pool)

    tables = _ssm_tables(ssm_a_re[l], ssm_a_im[l], ssm_log_dt[l], ssm_b_re[l], ssm_b_im[l],
                         ssm_c_re[l], ssm_c_im[l], ssm_d[l])
    h0r = state_ssm_re[l].reshape(DEC_BATCH, SSM_GROUPS * SSM_STATE)
    h0i = state_ssm_im[l].reshape(DEC_BATCH, SSM_GROUPS * SSM_STATE)
    y_act, h_prompt, hs_re, hs_im = _ssm(u, h0r, h0i, tables)

    w_r = jnp.zeros((D_MODEL, ROUTE_LANES), F32)
    w_r = w_r.at[:, :MOE_GROUPS].set(w_router_group[l])
    w_r = w_r.at[:, EXP_LANE0:EXP_LANE0 + MOE_EXPERTS].set(w_router_expert[l])
    wr_hi = w_r.astype(BF16)
    wr_lo = (w_r - wr_hi.astype(F32)).astype(BF16)
    b_r = jnp.zeros((1, ROUTE_LANES), F32)
    b_r = b_r.at[0, :MOE_GROUPS].set(b_router_group[l])
    b_r = b_r.at[0, EXP_LANE0:EXP_LANE0 + MOE_EXPERTS].set(b_router_expert[l])

    h, tn, route, cnt = _postmix(y_act, gates, y_pool, xp, xs, wa_bf, wb_bf, wo_bf,
                                 g_ffn[l].reshape(1, D_MODEL), wr_hi, wr_lo, b_r)
    slots, tile0, tiles = _dispatch_plan(route, cnt)
    xs_sorted = _sc_dispatch(tn, slots)
    ys_sorted = _experts(tile0, tiles, xs_sorted, w_exp_gate[l], w_exp_up[l], w_exp_down[l])
    y = _sc_collect(ys_sorted, slots)
    yp, ys = _final(h, y, route, g_final.reshape(1, D_MODEL))

    y_prompt = yp.reshape(BATCH, SEQ, D_MODEL)
    y_sample = ys.reshape(DEC_SEQ, DEC_BATCH, D_MODEL).transpose(1, 0, 2)
    new_pool_prompt = pool_tail[:, HIST - POOL_BUF:, :][None]
    us = u[T_PROMPT:T_ALL, :POOL_WIDTH].reshape(DEC_SEQ, DEC_BATCH, POOL_WIDTH).transpose(1, 0, 2)
    new_pool_sample = jnp.concatenate([state_pool[l][:, DEC_SEQ:, :], us], axis=1)[None]
    hp = h_prompt.reshape(BATCH, N_OCT, 2, OCT_GROUPS, SSM_STATE).transpose(2, 0, 1, 3, 4)
    hp = hp.reshape(2, BATCH, SSM_GROUPS, SSM_STATE)
    shp = (1, DEC_BATCH, SSM_GROUPS, SSM_STATE)
    return (y_prompt, y_sample, new_pool_prompt, hp[0][None], hp[1][None], new_pool_sample,
            hs_re.reshape(shp), hs_im.reshape(shp))
```

```python
import functools
import math

import jax
import jax.numpy as jnp
from jax import lax
from jax.experimental import pallas as pl
from jax.experimental.pallas import tpu as pltpu
from jax.experimental.pallas import tpu_sc as plsc

F32 = jnp.float32
BF16 = jnp.bfloat16
I32 = jnp.int32
U32 = jnp.uint32

D_MODEL = 2048
BATCH = 4
SEQ = 2048
DEC_BATCH = 128
DEC_SEQ = 8
PAST_LEN = 16384
POOL_WIDTH = D_MODEL // 2
POOL_WINDOWS = (2, 4, 8, 16)
POOL_GROUPS = len(POOL_WINDOWS)
POOL_GROUP_CH = POOL_WIDTH // POOL_GROUPS
POOL_OUT_CH = D_MODEL // POOL_GROUPS
POOL_BUF = max(POOL_WINDOWS) - 1
SSM_WIDTH = D_MODEL // 2
SSM_GROUP_CH = 16
SSM_GROUPS = SSM_WIDTH // SSM_GROUP_CH
SSM_STATE = 64
IN_WIDTH = POOL_WIDTH + SSM_WIDTH + 2 * D_MODEL
D_PACK = D_MODEL // 2
MOE_GROUPS = 4
MOE_EPG = 8
MOE_EXPERTS = MOE_GROUPS * MOE_EPG
MOE_FF = D_MODEL // 4
EPS = 1e-6

T_PROMPT = BATCH * SEQ
T_SAMPLE = DEC_BATCH * DEC_SEQ
T_ALL = T_PROMPT + T_SAMPLE
T_PAD = (BATCH + 1) * SEQ

LANES = 128
SUBLANES = 8
VMEM_LIMIT = 56 * 1024 * 1024

CHUNK = 8
OCT = LANES
N_OCT = SSM_WIDTH // OCT
OCT_GROUPS = OCT // SSM_GROUP_CH
OCT_STATES = OCT_GROUPS * SSM_STATE
CW = CHUNK * OCT
SW = 2 * OCT_STATES

ROUTE_LANES = LANES
EXP_LANE0 = MOE_GROUPS
N_ASSIGN = 2 * T_ALL
TME = 256
N_ITEMS_MAX = N_ASSIGN // TME + MOE_EXPERTS
N_SLOTS = N_ITEMS_MAX * TME


def _cparams(n_axes):
    return pltpu.CompilerParams(dimension_semantics=("arbitrary",) * n_axes,
                                vmem_limit_bytes=VMEM_LIMIT)


def _sigmoid(x):
    return 1.0 / (1.0 + jnp.exp(-x))


def _pack_pairs(x):
    c = x.shape[1] // 2
    hi = lax.bitcast_convert_type(x[:, :c].astype(BF16).astype(F32), U32)
    lo = lax.bitcast_convert_type(x[:, c:].astype(BF16).astype(F32), U32)
    return hi | (lo >> 16)


def _unpack_pairs(u, dtype):
    hi = lax.bitcast_convert_type(u & jnp.uint32(0xFFFF0000), F32)
    lo = lax.bitcast_convert_type(u << 16, F32)
    return jnp.concatenate([hi, lo], axis=1).astype(dtype)


def _gelu_tanh(x):
    c = math.sqrt(2.0 / math.pi)
    return 0.5 * x * (1.0 + jnp.tanh(c * (x + 0.044715 * (x * x * x))))


IN_TM = 1024
IN_TN = 1024
U_WIDTH = POOL_WIDTH + SSM_WIDTH
GATE_WIDTH = 2 * D_MODEL
IN_U_STEPS = U_WIDTH // IN_TN


def _inproj_body(x_ref, g_ref, w_ref, *rest):
    u_ref, gate_ref, xn_ref = rest[-3:]
    j = pl.program_id(1)
    @pl.when(j == 0)
    def _():
        x = x_ref[...]
        inv = lax.rsqrt(jnp.mean(x * x, axis=-1, keepdims=True) + EPS)
        xn_ref[...] = ((x * inv) * g_ref[...]).astype(BF16)
    acc = jnp.dot(xn_ref[...], w_ref[...], preferred_element_type=F32)
    @pl.when(j < IN_U_STEPS)
    def _():
        u_ref[...] = acc
    @pl.when(j >= IN_U_STEPS)
    def _():
        gate_ref[...] = _sigmoid(acc).astype(BF16)


def _inproj(x, g, w_bf, row_block0, dst=None):
    n_i = x.shape[0] // IN_TM
    in_specs = [
        pl.BlockSpec((IN_TM, D_MODEL), lambda i, j: (i, 0)),
        pl.BlockSpec((1, D_MODEL), lambda i, j: (0, 0)),
        pl.BlockSpec((D_MODEL, IN_TN), lambda i, j: (0, j)),
    ]
    args = [x, g, w_bf]
    aliases = {}
    if dst is not None:
        in_specs += [pl.BlockSpec(memory_space=pl.ANY)] * 2
        args += list(dst)
        aliases = {3: 0, 4: 1}
    return pl.pallas_call(
        _inproj_body,
        grid=(n_i, IN_WIDTH // IN_TN),
        in_specs=in_specs,
        out_specs=(
            pl.BlockSpec((IN_TM, IN_TN), lambda i, j: (i + row_block0, jnp.minimum(j, IN_U_STEPS - 1))),
            pl.BlockSpec((IN_TM, IN_TN), lambda i, j: (i + row_block0, jnp.maximum(j - IN_U_STEPS, 0))),
        ),
        out_shape=(
            jax.ShapeDtypeStruct((T_PAD, U_WIDTH), F32),
            jax.ShapeDtypeStruct((T_ALL, GATE_WIDTH), BF16),
        ),
        scratch_shapes=[pltpu.VMEM((IN_TM, D_MODEL), BF16)],
        input_output_aliases=aliases,
        compiler_params=_cparams(2),
        name="inproj",
    )(*args)


PP_TM = 512
HIST = 16


def _pool_project(pooled_g, g, w_ref, sc_ref, o_ref):
    y = jnp.dot(pooled_g.astype(BF16), w_ref[g], preferred_element_type=F32)
    lo, hi = g * POOL_OUT_CH, (g + 1) * POOL_OUT_CH
    o_ref[:, lo:hi] = (y * sc_ref[:, lo:hi]).astype(o_ref.dtype)


def _pool_prompt_body(u_ref, w_ref, sc_ref, o_ref, tail_ref, hist_ref):
    i = pl.program_id(1)
    @pl.when(i == 0)
    def _():
        hist_ref[...] = jnp.zeros_like(hist_ref)
    u = u_ref[...]
    ext = jnp.concatenate([hist_ref[...], u], axis=0)
    hist_ref[...] = u[PP_TM - HIST:, :]
    tail_ref[...] = u[PP_TM - HIST:, :]
    pos = i * PP_TM + lax.broadcasted_iota(I32, (PP_TM, 1), 0)
    for g, w in enumerate(POOL_WINDOWS):
        lo, hi = g * POOL_GROUP_CH, (g + 1) * POOL_GROUP_CH
        s = ext[:, lo:hi]
        d = 1
        while d < w:
            s = s + pltpu.roll(s, d, axis=0)
            d *= 2
        cnt = jnp.minimum(w, pos + 1).astype(F32)
        pooled = s[HIST:, :] / cnt - u[:, lo:hi]
        _pool_project(pooled, g, w_ref, sc_ref, o_ref)


def _pool_prompt(u, w_pool_bf, pool_scale):
    n_i = SEQ // PP_TM
    return pl.pallas_call(
        _pool_prompt_body,
        grid=(BATCH, n_i),
        in_specs=[
            pl.BlockSpec((PP_TM, POOL_WIDTH), lambda b, i: (b * n_i + i, 0)),
            pl.BlockSpec((POOL_GROUPS, POOL_GROUP_CH, POOL_OUT_CH), lambda b, i: (0, 0, 0)),
            pl.BlockSpec((1, D_MODEL), lambda b, i: (0, 0)),
        ],
        out_specs=(
            pl.BlockSpec((PP_TM, D_MODEL), lambda b, i: (b * n_i + i, 0)),
            pl.BlockSpec((None, HIST, POOL_WIDTH), lambda b, i: (b, 0, 0)),
        ),
        out_shape=(
            jax.ShapeDtypeStruct((T_ALL, D_MODEL), BF16),
            jax.ShapeDtypeStruct((BATCH, HIST, POOL_WIDTH), F32),
        ),
        scratch_shapes=[pltpu.VMEM((HIST, POOL_WIDTH), F32)],
        compiler_params=_cparams(2),
        name="pool_prompt",
    )(u, w_pool_bf, pool_scale)


def _pool_sample_body(u_ref, hist_ref, w_ref, sc_ref, _dst, o_ref):
    rows = [hist_ref[k] for k in range(POOL_BUF)]
    rows += [u_ref[DEC_BATCH * t:DEC_BATCH * (t + 1), :] for t in range(DEC_SEQ)]
    n = len(rows)
    for g, w in enumerate(POOL_WINDOWS):
        lo, hi = g * POOL_GROUP_CH, (g + 1) * POOL_GROUP_CH
        f = [r[:, lo:hi] for r in rows]
        cur = f
        d = 1
        while d < w:
            cur = [cur[k] + cur[k - d] if k - d >= 0 else cur[k] for k in range(n)]
            d *= 2
        pooled = jnp.concatenate(
            [cur[POOL_BUF + t] / float(w) - f[POOL_BUF + t] for t in range(DEC_SEQ)], axis=0)
        _pool_project(pooled, g, w_ref, sc_ref, o_ref)


def _pool_sample(u, hist_tm, w_pool_bf, pool_scale, y_pool):
    blk = T_PROMPT // T_SAMPLE
    return pl.pallas_call(
        _pool_sample_body,
        grid=(1,),
        in_specs=[
            pl.BlockSpec((T_SAMPLE, POOL_WIDTH), lambda i: (blk, 0)),
            pl.BlockSpec((POOL_BUF, DEC_BATCH, POOL_WIDTH), lambda i: (0, 0, 0)),
            pl.BlockSpec((POOL_GROUPS, POOL_GROUP_CH, POOL_OUT_CH), lambda i: (0, 0, 0)),
            pl.BlockSpec((1, D_MODEL), lambda i: (0, 0)),
            pl.BlockSpec(memory_space=pl.ANY),
        ],
        out_specs=pl.BlockSpec((T_SAMPLE, D_MODEL), lambda i: (blk, 0)),
        out_shape=jax.ShapeDtypeStruct((T_ALL, D_MODEL), BF16),
        input_output_aliases={4: 0},
        compiler_params=_cparams(1),
        name="pool_sample",
    )(u, hist_tm, w_pool_bf, pool_scale, y_pool)


def _ssm_tables(a_re, a_im, log_dt, b_re, b_im, c_re, c_im, d_skip):
    dt = jnp.exp(log_dt)[:, None]
    lr, li = a_re, a_im
    ab_re = jnp.exp(lr * dt) * jnp.cos(li * dt)
    ab_im = jnp.exp(lr * dt) * jnp.sin(li * dt)
    den = lr * lr + li * li
    nr, ni = ab_re - 1.0, ab_im
    q_re = (nr * lr + ni * li) / den
    q_im = (ni * lr - nr * li) / den
    bb_re = q_re[..., None] * b_re - q_im[..., None] * b_im
    bb_im = q_re[..., None] * b_im + q_im[..., None] * b_re

    def lam_rows(ks):
        k = jnp.asarray(ks, F32)[:, None, None]
        m = jnp.exp(k * lr * dt)
        re = (m * jnp.cos(k * li * dt)).reshape(len(ks), N_OCT, OCT_STATES)
        im = (m * jnp.sin(k * li * dt)).reshape(len(ks), N_OCT, OCT_STATES)
        return jnp.concatenate([re, im], axis=-1).transpose(1, 0, 2)

    def compact(re, im):
        v = jnp.concatenate([re, im], axis=-1)
        return v.reshape(N_OCT, OCT, 2 * SSM_STATE)

    bbc = compact(jnp.swapaxes(bb_re, 1, 2), jnp.swapaxes(bb_im, 1, 2))
    ccc = compact(c_re, c_im)
    pw = lam_rows(list(range(2 * SUBLANES)))
    r = jnp.arange(SUBLANES)[None, :, None]
    parts = [jnp.where(r >= dd, lam_rows([CHUNK * dd]), 0.0) for dd in (1, 2, 4)]
    parts.append(lam_rows([CHUNK * kk for kk in range(1, SUBLANES + 1)]))
    tab = jnp.concatenate(parts, axis=1)
    dsk = d_skip.reshape(N_OCT, 1, OCT)
    return bbc, ccc, pw, tab, dsk


def _split_bf16(x):
    hi = x.astype(BF16)
    return hi, (x - hi.astype(F32)).astype(BF16)


def _dot_nt(a, b):
    return lax.dot_general(a, b, (((1,), (1,)), ((), ())), preferred_element_type=F32)


def _build_weights(bbc_ref, ccc_ref, pw_ref, f_ref, gt_ref, m_ref):
    row_gi = lax.broadcasted_iota(I32, (OCT, 1), 0) >> 4
    col = lax.broadcasted_iota(I32, (1, SW), 1)
    col_gi = (col >> 6) & 7
    src = ((col >> 9) << 6) | (col & 63)
    k128 = lax.broadcasted_iota(I32, (2 * SSM_STATE, 1), 0)
    spread = jnp.where(k128 == src, 1.0, 0.0).astype(BF16)
    diag = row_gi == col_gi

    def expand(c_ref):
        hi, lo = _split_bf16(c_ref[...])
        d = (jnp.dot(hi, spread, preferred_element_type=F32)
             + jnp.dot(lo, spread, preferred_element_type=F32))
        d = jnp.where(diag, d, 0.0)
        return d[:, :OCT_STATES], d[:, OCT_STATES:]

    br, bi = expand(bbc_ref)
    cr, ci = expand(ccc_ref)
    chi_r, clo_r = _split_bf16(cr)
    chi_i, clo_i = _split_bf16(ci)

    def lam(k):
        return pw_ref[k:k + 1, :OCT_STATES], pw_ref[k:k + 1, OCT_STATES:]

    def dot3(a, bhi, blo):
        ahi, alo = _split_bf16(a)
        return _dot_nt(ahi, bhi) + _dot_nt(alo, bhi) + _dot_nt(ahi, blo)

    lags = []
    for k in range(CHUNK):
        pr, pi_ = lam(k)
        fr, fi = _cmul(br, bi, pr, pi_)
        s = CHUNK - 1 - k
        f_ref[s * OCT:(s + 1) * OCT, :] = jnp.concatenate([fr, fi], axis=1).astype(BF16)
        lags.append((dot3(fr, chi_r, clo_r) - dot3(fi, chi_i, clo_i)).astype(BF16))
        pr, pi_ = lam(k + 1)
        gr, gi = _cmul(cr, ci, pr, pi_)
        gt_ref[k * OCT:(k + 1) * OCT, :] = jnp.concatenate([gr, -gi], axis=1).astype(BF16)
    zero = jnp.zeros((OCT, OCT), BF16)
    for s in range(CHUNK):
        for t in range(CHUNK):
            m_ref[s * OCT:(s + 1) * OCT, t * OCT:(t + 1) * OCT] = lags[t - s] if t >= s else zero


def _cmul(ar, ai, br, bi):
    return ar * br - ai * bi, ar * bi + ai * br


def _chunk_scan(sloc, tab_ref):
    R = sloc.shape[0]
    nb = R // SUBLANES
    sr, si = sloc[:, :OCT_STATES], sloc[:, OCT_STATES:]
    rowi = lax.broadcasted_iota(I32, (R, 1), 0)
    tr = jnp.where(rowi == 0, 0.0, pltpu.roll(sr, 1, axis=0))
    ti = jnp.where(rowi == 0, 0.0, pltpu.roll(si, 1, axis=0))
    for lvl, d in enumerate((1, 2, 4)):
        mr = tab_ref[lvl * SUBLANES:(lvl + 1) * SUBLANES, :OCT_STATES]
        mi = tab_ref[lvl * SUBLANES:(lvl + 1) * SUBLANES, OCT_STATES:]
        mr = jnp.concatenate([mr] * nb, axis=0)
        mi = jnp.concatenate([mi] * nb, axis=0)
        pr, pi_ = _cmul(mr, mi, pltpu.roll(tr, d, axis=0), pltpu.roll(ti, d, axis=0))
        tr, ti = tr + pr, ti + pi_
    pwr = tab_ref[3 * SUBLANES:4 * SUBLANES, :OCT_STATES]
    pwi = tab_ref[3 * SUBLANES:4 * SUBLANES, OCT_STATES:]
    cr = jnp.zeros((1, OCT_STATES), F32)
    ci = jnp.zeros((1, OCT_STATES), F32)
    out_r, out_i = [], []
    for k in range(nb):
        ar = tr[k * SUBLANES:(k + 1) * SUBLANES, :]
        ai = ti[k * SUBLANES:(k + 1) * SUBLANES, :]
        pr, pi_ = _cmul(pwr, pwi, jnp.broadcast_to(cr, ar.shape), jnp.broadcast_to(ci, ai.shape))
        hr, hi = ar + pr, ai + pi_
        out_r.append(hr)
        out_i.append(hi)
        cr, ci = hr[SUBLANES - 1:, :], hi[SUBLANES - 1:, :]
    hin = jnp.concatenate([jnp.concatenate(out_r, axis=0), jnp.concatenate(out_i, axis=0)], axis=1)
    lr, li = pwr[0:1, :], pwi[0:1, :]
    fr, fi = _cmul(lr, li, cr, ci)
    fin = jnp.concatenate([fr + sr[R - 1:, :], fi + si[R - 1:, :]], axis=1)
    return hin, fin


def _ssm_body(u_ref, h0r_ref, h0i_ref, bbc_ref, ccc_ref, pw_ref, tab_ref, d_ref,
              y_ref, hout_ref, hr_ref, hi_ref, f_ref, gt_ref, m_ref):
    b = pl.program_id(1)

    @pl.when(b == 0)
    def _():
        _build_weights(bbc_ref, ccc_ref, pw_ref, f_ref, gt_ref, m_ref)

    def outputs(xs, xb, hin_bf):
        y = (jnp.dot(xb, m_ref[...], preferred_element_type=F32) + _dot_nt(hin_bf, gt_ref[...]))
        return [_gelu_tanh(y[:, t * OCT:(t + 1) * OCT] + d_ref[...] * xs[t]) for t in range(CHUNK)]

    @pl.when(b < BATCH)
    def _():
        R = SEQ // CHUNK
        xs = [u_ref[pl.ds(s, R, stride=CHUNK), :] for s in range(CHUNK)]
        xb = jnp.concatenate(xs, axis=1).astype(BF16)
        sloc = jnp.dot(xb, f_ref[...], preferred_element_type=F32)
        hin, fin = _chunk_scan(sloc, tab_ref)
        for t, yt in enumerate(outputs(xs, xb, hin.astype(BF16))):
            y_ref[pl.ds(t, R, stride=CHUNK), :] = yt
        hout_ref[...] = fin

    @pl.when(b == BATCH)
    def _():
        B = DEC_BATCH
        xs = [u_ref[B * s:B * (s + 1), :] for s in range(CHUNK)]
        xb = jnp.concatenate(xs, axis=1).astype(BF16)
        sloc = jnp.dot(xb, f_ref[...], preferred_element_type=F32)
        h0r, h0i = h0r_ref[...], h0i_ref[...]
        hin = jnp.concatenate([h0r, h0i], axis=1).astype(BF16)
        for t, yt in enumerate(outputs(xs, xb, hin)):
            y_ref[B * t:B * (t + 1), :] = yt
        lr = tab_ref[3 * SUBLANES:3 * SUBLANES + 1, :OCT_STATES]
        li = tab_ref[3 * SUBLANES:3 * SUBLANES + 1, OCT_STATES:]
        nr, ni = _cmul(lr, li, h0r, h0i)
        hr_ref[...] = nr + sloc[:, :OCT_STATES]
        hi_ref[...] = ni + sloc[:, OCT_STATES:]


def _ssm(u, h0r, h0i, tables):
    col0 = POOL_WIDTH // OCT
    im3 = lambda o, b: (o, 0, 0)
    st_spec = pl.BlockSpec((DEC_BATCH, OCT_STATES), lambda o, b: (0, o))
    return pl.pallas_call(
        _ssm_body,
        grid=(N_OCT, BATCH + 1),
        in_specs=[
            pl.BlockSpec((SEQ, OCT), lambda o, b: (b, col0 + o)), st_spec, st_spec,
            pl.BlockSpec((None, OCT, 2 * SSM_STATE), im3),
            pl.BlockSpec((None, OCT, 2 * SSM_STATE), im3),
            pl.BlockSpec((None, 2 * SUBLANES, SW), im3),
            pl.BlockSpec((None, 4 * SUBLANES, SW), im3),
            pl.BlockSpec((None, 1, OCT), im3),
        ],
        out_specs=(
            pl.BlockSpec((SEQ, OCT), lambda o, b: (b, o)),
            pl.BlockSpec((None, 1, SW), lambda o, b: (jnp.minimum(b, BATCH - 1) * N_OCT + o, 0, 0)),
            st_spec, st_spec,
        ),
        out_shape=(
            jax.ShapeDtypeStruct((T_PAD, SSM_WIDTH), F32),
            jax.ShapeDtypeStruct((BATCH * N_OCT, 1, SW), F32),
            jax.ShapeDtypeStruct((DEC_BATCH, SSM_GROUPS * SSM_STATE), F32),
            jax.ShapeDtypeStruct((DEC_BATCH, SSM_GROUPS * SSM_STATE), F32),
        ),
        scratch_shapes=[pltpu.VMEM((CW, SW), BF16), pltpu.VMEM((CW, SW), BF16), pltpu.VMEM((CW, CW), BF16)],
        compiler_params=_cparams(2),
        name="ssm",
    )(u, h0r, h0i, *tables)


PM_TM = 256
PM_PROMPT_BLOCKS = T_PROMPT // PM_TM
PM_STEPS = T_ALL // PM_TM


def _route(logits, valid, cnt_ref):
    lane = lax.broadcasted_iota(I32, (PM_TM, ROUTE_LANES), 1)
    neg = jnp.float32(-jnp.inf)
    big = jnp.int32(1 << 20)
    is_g = lane < MOE_GROUPS
    gmax = jnp.max(jnp.where(is_g, logits, neg), axis=1, keepdims=True)
    g_idx = jnp.min(jnp.where(is_g & (logits == gmax), lane, big), axis=1, keepdims=True)
    g_den = jnp.sum(jnp.where(is_g, jnp.exp(logits - gmax), 0.0), axis=1, keepdims=True)
    g_val = 1.0 / g_den
    e_lane = lane - EXP_LANE0
    sel = (e_lane >= 0) & (e_lane < MOE_EXPERTS) & ((e_lane >> 3) == g_idx)
    m1 = jnp.max(jnp.where(sel, logits, neg), axis=1, keepdims=True)
    i1 = jnp.min(jnp.where(sel & (logits == m1), lane, big), axis=1, keepdims=True)
    sel2 = sel & (lane != i1)
    m2 = jnp.max(jnp.where(sel2, logits, neg), axis=1, keepdims=True)
    i2 = jnp.min(jnp.where(sel2 & (logits == m2), lane, big), axis=1, keepdims=True)
    e2 = jnp.exp(m2 - m1)
    w1 = g_val / (1.0 + e2)
    w2 = g_val * e2 / (1.0 + e2)
    oh1 = lane == i1
    oh2 = lane == i2
    oh = jnp.where(oh1 | oh2, valid, 0.0)
    rr = lax.broadcasted_iota(I32, (PM_TM, PM_TM), 0)
    cc = lax.broadcasted_iota(I32, (PM_TM, PM_TM), 1)
    tri = jnp.where(cc < rr, 1.0, 0.0).astype(BF16)
    base = cnt_ref[...] + jnp.dot(tri, oh.astype(BF16), preferred_element_type=F32)
    rank1 = jnp.sum(jnp.where(oh1, base, 0.0), axis=1, keepdims=True)
    rank2 = jnp.sum(jnp.where(oh2, base, 0.0), axis=1, keepdims=True)
    cnt_ref[...] = cnt_ref[...] + jnp.sum(oh, axis=0, keepdims=True)
    rt = jnp.where(lane == 0, w1, 0.0)
    rt = jnp.where(lane == 1, w2, rt)
    rt = jnp.where(lane == 2, rank1, rt)
    rt = jnp.where(lane == 3, rank2, rt)
    rt = jnp.where(lane == 4, (i1 - EXP_LANE0).astype(F32), rt)
    rt = jnp.where(lane == 5, (i2 - EXP_LANE0).astype(F32), rt)
    return rt


def _postmix_body(ya_ref, gp_ref, gs_ref, yp_ref, xp_ref, xs_ref, wa_ref, wb_ref, wo_ref,
                  gf_ref, wrh_ref, wrl_ref, br_ref, h_ref, tn_ref, rt_ref, cnt_out_ref,
                  cnt_ref, lg_ref, mg_ref):
    i = pl.program_id(0)
    @pl.when(i == 0)
    def _():
        cnt_ref[...] = jnp.zeros_like(cnt_ref)
        lg_ref[...] = jnp.zeros_like(lg_ref)
        mg_ref[...] = jnp.zeros_like(mg_ref)
    prev_logits = lg_ref[...]
    t_b = jnp.clip(i - 1, 0, PM_STEPS - 1)

    x = jnp.where(t_b < PM_PROMPT_BLOCKS, xp_ref[...], xs_ref[...])
    h = x + jnp.dot(mg_ref[...], wo_ref[...], preferred_element_type=F32)
    h_ref[...] = h
    ya = ya_ref[...].astype(BF16)
    a = jnp.dot(ya, wa_ref[...], preferred_element_type=F32)
    bg = jnp.dot(ya, wb_ref[...], preferred_element_type=F32)
    inv = lax.rsqrt(jnp.mean(h * h, axis=-1, keepdims=True) + EPS)
    tn = (h * inv) * gf_ref[...]
    tn_ref[...] = _pack_pairs(tn)
    t_hi = tn.astype(BF16)
    t_lo = (tn - t_hi.astype(F32)).astype(BF16)
    y_ssm = a * _sigmoid(bg)
    merged = (gp_ref[...].astype(F32) * yp_ref[...].astype(F32)
              + gs_ref[...].astype(F32) * y_ssm)
    mg_ref[...] = merged.astype(BF16)
    wrh = wrh_ref[...]
    lg_ref[...] = (jnp.dot(t_hi, wrh, preferred_element_type=F32)
                   + jnp.dot(t_lo, wrh, preferred_element_type=F32)
                   + jnp.dot(t_hi, wrl_ref[...], preferred_element_type=F32)) + br_ref[...]
    rt_ref[...] = _route(prev_logits, jnp.where(i > 1, 1.0, 0.0), cnt_ref)
    cnt_out_ref[...] = cnt_ref[...]


def _postmix(y_act, gates, y_pool, xp, xs, wa, wb, wo, g_ffn, wr_hi, wr_lo, b_r):
    npb = PM_PROMPT_BLOCKS
    const2 = lambda i: (0, 0)
    tile = lambda i: jnp.minimum(i, PM_STEPS - 1)
    tile_b = lambda i: jnp.clip(i - 1, 0, PM_STEPS - 1)
    return pl.pallas_call(
        _postmix_body,
        grid=(PM_STEPS + 2,),
        in_specs=[
            pl.BlockSpec((PM_TM, SSM_WIDTH), lambda i: (tile(i), 0)),
            pl.BlockSpec((PM_TM, D_MODEL), lambda i: (tile(i), 0)),
            pl.BlockSpec((PM_TM, D_MODEL), lambda i: (tile(i), 1)),
            pl.BlockSpec((PM_TM, D_MODEL), lambda i: (tile(i), 0)),
            pl.BlockSpec((PM_TM, D_MODEL), lambda i: (jnp.minimum(tile_b(i), npb - 1), 0)),
            pl.BlockSpec((PM_TM, D_MODEL), lambda i: (jnp.maximum(tile_b(i) - npb, 0), 0)),
            pl.BlockSpec((SSM_WIDTH, D_MODEL), const2, pipeline_mode=pl.Buffered(1)),
            pl.BlockSpec((SSM_WIDTH, D_MODEL), const2, pipeline_mode=pl.Buffered(1)),
            pl.BlockSpec((D_MODEL, D_MODEL), const2, pipeline_mode=pl.Buffered(1)),
            pl.BlockSpec((1, D_MODEL), const2),
            pl.BlockSpec((D_MODEL, ROUTE_LANES), const2),
            pl.BlockSpec((D_MODEL, ROUTE_LANES), const2),
            pl.BlockSpec((1, ROUTE_LANES), const2),
        ],
        out_specs=(
            pl.BlockSpec((PM_TM, D_MODEL), lambda i: (tile_b(i), 0)),
            pl.BlockSpec((PM_TM, D_PACK), lambda i: (tile_b(i), 0)),
            pl.BlockSpec((PM_TM, ROUTE_LANES), lambda i: (jnp.maximum(i - 2, 0), 0)),
            pl.BlockSpec((1, ROUTE_LANES), const2),
        ),
        out_shape=(
            jax.ShapeDtypeStruct((T_ALL, D_MODEL), F32),
            jax.ShapeDtypeStruct((T_ALL, D_PACK), U32),
            jax.ShapeDtypeStruct((T_ALL, ROUTE_LANES), F32),
            jax.ShapeDtypeStruct((1, ROUTE_LANES), F32),
        ),
        scratch_shapes=[pltpu.VMEM((1, ROUTE_LANES), F32), pltpu.VMEM((PM_TM, ROUTE_LANES), F32),
                        pltpu.VMEM((PM_TM, D_MODEL), BF16)],
        compiler_params=_cparams(1),
        name="postmix",
    )(y_act, gates, gates, y_pool, xp, xs, wa, wb, wo, g_ffn, wr_hi, wr_lo, b_r)


SC_CH = 32


def _sc_workers():
    info = plsc.get_sparse_core_info()
    return info.num_cores, info.num_cores * info.num_subcores


def _sc_dispatch(tn, slots):
    n_cores, n_workers = _sc_workers()
    per_w = (T_ALL // SC_CH) // n_workers
    assert per_w * n_workers * SC_CH == T_ALL
    slots = slots.reshape(2, n_workers, per_w, SC_CH)

    @functools.partial(
        pl.kernel,
        mesh=plsc.VectorSubcoreMesh(core_axis_name="c", subcore_axis_name="s"),
        out_type=jax.ShapeDtypeStruct((N_SLOTS, D_PACK), U32),
        scratch_types=[pltpu.VMEM((2, per_w, SC_CH), I32), pltpu.VMEM((SC_CH, D_PACK), U32)],
    )
    def k(tn_hbm, slots_hbm, xs_hbm, idx_v, rows_v):
        wid = lax.axis_index("s") * n_cores + lax.axis_index("c")
        c0 = wid * per_w
        pltpu.sync_copy(slots_hbm.at[0, wid], idx_v.at[0])
        pltpu.sync_copy(slots_hbm.at[1, wid], idx_v.at[1])

        @pl.loop(0, per_w)
        def _(c):
            row0 = pl.multiple_of((c0 + c) * SC_CH, SC_CH)
            pltpu.sync_copy(tn_hbm.at[pl.ds(row0, SC_CH)], rows_v)
            pltpu.sync_copy(rows_v, xs_hbm.at[idx_v.at[0, c]])
            pltpu.sync_copy(rows_v, xs_hbm.at[idx_v.at[1, c]])

    return k(tn, slots)


def _sc_collect(ys, slots):
    n_cores, n_workers = _sc_workers()
    per_w = (N_ASSIGN // SC_CH) // n_workers
    assert per_w * n_workers * SC_CH == N_ASSIGN
    slots = slots.reshape(n_workers, per_w, SC_CH)

    @functools.partial(
        pl.kernel,
        mesh=plsc.VectorSubcoreMesh(core_axis_name="c", subcore_axis_name="s"),
        out_type=jax.ShapeDtypeStruct((N_ASSIGN, D_PACK), U32),
        scratch_types=[pltpu.VMEM((per_w, SC_CH), I32), pltpu.VMEM((SC_CH, D_PACK), U32)],
    )
    def k(ys_hbm, slots_hbm, out_hbm, idx_v, rows_v):
        wid = lax.axis_index("s") * n_cores + lax.axis_index("c")
        c0 = wid * per_w
        pltpu.sync_copy(slots_hbm.at[wid], idx_v)

        @pl.loop(0, per_w)
        def _(c):
            row0 = pl.multiple_of((c0 + c) * SC_CH, SC_CH)
            pltpu.sync_copy(ys_hbm.at[idx_v.at[c]], rows_v)
            pltpu.sync_copy(rows_v, out_hbm.at[pl.ds(row0, SC_CH)])

    return k(ys, slots)


W_PARTS = 2


def _expert_body(t0_ref, nt_ref, xs_hbm, wg_hbm, wu_hbm, wd_hbm, ys_hbm,
                 wg_ref, wu_ref, wd_ref, xb_ref, yb_ref, wgb_ref, wub_ref, wdb_ref, wsem, xsem, ysem):
    e = pl.program_id(0)
    n = nt_ref[e]
    g0 = t0_ref[e]
    ws = e & 1

    def w_copies(ex, slot):
        out = []
        for hbm, buf in ((wg_hbm, wg_ref), (wu_hbm, wu_ref), (wd_hbm, wd_ref)):
            rb = buf.shape[1] // W_PARTS
            for p in range(W_PARTS):
                out.append((pltpu.make_async_copy(hbm.at[ex, pl.ds(p * rb, rb)],
                                                  buf.at[slot, pl.ds(p * rb, rb)], wsem.at[slot]), p))
        return out

    @pl.when(e == 0)
    def _():
        for cp, p in w_copies(0, 0):
            cp.start(priority=p)

    @pl.when(e + 1 < MOE_EXPERTS)
    def _():
        for cp, p in w_copies(e + 1, 1 - ws):
            cp.start(priority=p)

    for cp, _ in w_copies(e, ws):
        cp.wait()

    def rows(j):
        return pl.ds(pl.multiple_of((g0 + j) * TME, TME), TME)

    def x_copy(j, s):
        return pltpu.make_async_copy(xs_hbm.at[rows(j)], xb_ref.at[s], xsem.at[s])

    def y_copy(j, s):
        return pltpu.make_async_copy(yb_ref.at[s], ys_hbm.at[rows(j)], ysem.at[s])

    @pl.when(n > 0)
    def _():
        x_copy(0, 0).start()
        wgb_ref[...] = wg_ref[ws].astype(BF16)
        wub_ref[...] = wu_ref[ws].astype(BF16)
        wdb_ref[...] = wd_ref[ws].astype(BF16)

        def tile(j, c):
            s = j & 1
            x_copy(j, s).wait()
            @pl.when(j + 1 < n)
            def _():
                x_copy(j + 1, 1 - s).start()
            @pl.when(j >= 2)
            def _():
                y_copy(j - 2, s).wait()
            x = _unpack_pairs(xb_ref[s], BF16)
            hg = jnp.dot(x, wgb_ref[...], preferred_element_type=F32)
            hu = jnp.dot(x, wub_ref[...], preferred_element_type=F32)
            act = (hg * _sigmoid(hg)) * hu
            yb_ref[s] = _pack_pairs(jnp.dot(act.astype(BF16), wdb_ref[...], preferred_element_type=F32))
            y_copy(j, s).start()
            return c
        lax.fori_loop(0, n, tile, 0)

        @pl.when(n >= 2)
        def _():
            y_copy(n - 2, n & 1).wait()
        y_copy(n - 1, (n - 1) & 1).wait()


def _experts(tile0, tiles, xs, w_eg, w_eu, w_ed):
    any_spec = pl.BlockSpec(memory_space=pl.ANY)
    grid_spec = pltpu.PrefetchScalarGridSpec(
        num_scalar_prefetch=2,
        grid=(MOE_EXPERTS,),
        in_specs=[any_spec] * 4,
        out_specs=any_spec,
        scratch_shapes=[
            pltpu.VMEM((2, D_MODEL, MOE_FF), F32),
            pltpu.VMEM((2, D_MODEL, MOE_FF), F32),
            pltpu.VMEM((2, MOE_FF, D_MODEL), F32),
            pltpu.VMEM((2, TME, D_PACK), U32),
            pltpu.VMEM((2, TME, D_PACK), U32),
            pltpu.VMEM((D_MODEL, MOE_FF), BF16),
            pltpu.VMEM((D_MODEL, MOE_FF), BF16),
            pltpu.VMEM((MOE_FF, D_MODEL), BF16),
            pltpu.SemaphoreType.DMA((2,)),
            pltpu.SemaphoreType.DMA((2,)),
            pltpu.SemaphoreType.DMA((2,)),
        ],
    )
    return pl.pallas_call(
        _expert_body,
        grid_spec=grid_spec,
        out_shape=jax.ShapeDtypeStruct((N_SLOTS, D_PACK), U32),
        compiler_params=_cparams(1),
        name="experts",
    )(tile0, tiles, xs, w_eg, w_eu, w_ed)


FN_TM = 256
FN_PROMPT_BLOCKS = T_PROMPT // FN_TM


def _final_body(h_ref, y0_ref, y1_ref, rt_ref, g_ref, op_ref, os_ref):
    i = pl.program_id(0)
    rt = rt_ref[...]
    z = (h_ref[...] + rt[:, 0:1] * _unpack_pairs(y0_ref[...], F32)
         + rt[:, 1:2] * _unpack_pairs(y1_ref[...], F32))
    inv = lax.rsqrt(jnp.mean(z * z, axis=-1, keepdims=True) + EPS)
    out = (z * inv) * g_ref[...]
    @pl.when(i < FN_PROMPT_BLOCKS)
    def _():
        op_ref[...] = out
    @pl.when(i >= FN_PROMPT_BLOCKS)
    def _():
        os_ref[...] = out


def _final(h, y, route, g_final):
    n = T_ALL // FN_TM
    npb = FN_PROMPT_BLOCKS
    yoff = T_ALL // FN_TM
    return pl.pallas_call(
        _final_body,
        grid=(n,),
        in_specs=[
            pl.BlockSpec((FN_TM, D_MODEL), lambda i: (i, 0)),
            pl.BlockSpec((FN_TM, D_PACK), lambda i: (i, 0)),
            pl.BlockSpec((FN_TM, D_PACK), lambda i: (yoff + i, 0)),
            pl.BlockSpec((FN_TM, ROUTE_LANES), lambda i: (i, 0)),
            pl.BlockSpec((1, D_MODEL), lambda i: (0, 0)),
        ],
        out_specs=(
            pl.BlockSpec((FN_TM, D_MODEL), lambda i: (jnp.minimum(i, npb - 1), 0)),
            pl.BlockSpec((FN_TM, D_MODEL), lambda i: (jnp.maximum(i - npb, 0), 0)),
        ),
        out_shape=(
            jax.ShapeDtypeStruct((T_PROMPT, D_MODEL), F32),
            jax.ShapeDtypeStruct((T_SAMPLE, D_MODEL), F32),
        ),
        compiler_params=_cparams(1),
        name="final",
    )(h, y, y, route, g_final)


def _dispatch_plan(route, cnt):
    counts = cnt[0, EXP_LANE0:EXP_LANE0 + MOE_EXPERTS].astype(I32)
    tiles = (counts + (TME - 1)) // TME
    cumt = jnp.cumsum(tiles)
    pad_off = (cumt - tiles) * TME
    rank = route[:, 2:4].astype(I32)
    eid = route[:, 4:6].astype(I32)
    onehot = eid[..., None] == jnp.arange(MOE_EXPERTS, dtype=I32)
    slots = (rank + jnp.sum(jnp.where(onehot, pad_off, 0), axis=-1)).T
    return slots, cumt - tiles, tiles


def kernel(x_prompt, x_sample, state_pool, state_ssm_re, state_ssm_im, g_mix, w_in, w_pool,
           pool_scale, ssm_a_re, ssm_a_im, ssm_log_dt, ssm_b_re, ssm_b_im, ssm_c_re, ssm_c_im,
           ssm_d, w_glu_a, w_glu_b, w_out, g_ffn, w_router_group, b_router_group,
           w_router_expert, b_router_expert, w_exp_gate, w_exp_up, w_exp_down, g_final):
    l = 0
    xp = x_prompt.reshape(T_PROMPT, D_MODEL)
    xs = x_sample.transpose(1, 0, 2).reshape(T_SAMPLE, D_MODEL)
    w_in_bf = w_in[l].astype(BF16)
    w_pool_bf = w_pool[l].astype(BF16)
    wa_bf = w_glu_a[l].astype(BF16)
    wb_bf = w_glu_b[l].astype(BF16)
    wo_bf = w_out[l].astype(BF16)
    g_mix2 = g_mix[l].reshape(1, D_MODEL)
    scale2 = pool_scale[l].reshape(1, D_MODEL)

    u, gates = _inproj(xp, g_mix2, w_in_bf, 0)
    u, gates = _inproj(xs, g_mix2, w_in_bf, T_PROMPT // IN_TM, dst=(u, gates))

    y_pool, pool_tail = _pool_prompt(u, w_pool_bf, scale2)
    hist_tm = state_pool[l].transpose(1, 0, 2)
    y_pool = _pool_sample(u, hist_tm, w_pool_bf, scale2, y_pool)

    tables = _ssm_tables(ssm_a_re[l], ssm_a_im[l], ssm_log_dt[l], ssm_b_re[l], ssm_b_im[l],
                         ssm_c_re[l], ssm_c_im[l], ssm_d[l])
    h0r = state_ssm_re[l].reshape(DEC_BATCH, SSM_GROUPS * SSM_STATE)
    h0i = state_ssm_im[l].reshape(DEC_BATCH, SSM_GROUPS * SSM_STATE)
    y_act, h_prompt, hs_re, hs_im = _ssm(u, h0r, h0i, tables)

    w_r = jnp.zeros((D_MODEL, ROUTE_LANES), F32)
    w_r = w_r.at[:, :MOE_GROUPS].set(w_router_group[l])
    w_r = w_r.at[:, EXP_LANE0:EXP_LANE0 + MOE_EXPERTS].set(w_router_expert[l])
    wr_hi = w_r.astype(BF16)
    wr_lo = (w_r - wr_hi.astype(F32)).astype(BF16)
    b_r = jnp.zeros((1, ROUTE_LANES), F32)
    b_r = b_r.at[0, :MOE_GROUPS].set(b_router_group[l])
    b_r = b_r.at[0, EXP_LANE0:EXP_LANE0 + MOE_EXPERTS].set(b_router_expert[l])

    h, tn, route, cnt = _postmix(y_act, gates, y_pool, xp, xs, wa_bf, wb_bf, wo_bf,
                                 g_ffn[l].reshape(1, D_MODEL), wr_hi, wr_lo, b_r)
    slots, tile0, tiles = _dispatch_plan(route, cnt)
    xs_sorted = _sc_dispatch(tn, slots)
    ys_sorted = _experts(tile0, tiles, xs_sorted, w_exp_gate[l], w_exp_up[l], w_exp_down[l])
    y = _sc_collect(ys_sorted, slots)
    yp, ys = _final(h, y, route, g_final.reshape(1, D_MODEL))

    y_prompt = yp.reshape(BATCH, SEQ, D_MODEL)
    y_sample = ys.reshape(DEC_SEQ, DEC_BATCH, D_MODEL).transpose(1, 0, 2)
    new_pool_prompt = pool_tail[:, HIST - POOL_BUF:, :][None]
    us = u[T_PROMPT:T_ALL, :POOL_WIDTH].reshape(DEC_SEQ, DEC_BATCH, POOL_WIDTH).transpose(1, 0, 2)
    new_pool_sample = jnp.concatenate([state_pool[l][:, DEC_SEQ:, :], us], axis=1)[None]
    hp = h_prompt.reshape(BATCH, N_OCT, 2, OCT_GROUPS, SSM_STATE).transpose(2, 0, 1, 3, 4)
    hp = hp.reshape(2, BATCH, SSM_GROUPS, SSM_STATE)
    shp = (1, DEC_BATCH, SSM_GROUPS, SSM_STATE)
    return (y_prompt, y_sample, new_pool_prompt, hp[0][None], hp[1][None], new_pool_sample,
            hs_re.reshape(shp), hs_im.reshape(shp))
```

```python
import functools
import math

import jax
import jax.numpy as jnp
from jax import lax
from jax.experimental import pallas as pl
from jax.experimental.pallas import tpu as pltpu
from jax.experimental.pallas import tpu_sc as plsc

F32 = jnp.float32
BF16 = jnp.bfloat16
I32 = jnp.int32
U32 = jnp.uint32

D_MODEL = 2048
BATCH = 4
SEQ = 2048
DEC_BATCH = 128
DEC_SEQ = 8
PAST_LEN = 16384
POOL_WIDTH = D_MODEL // 2
POOL_WINDOWS = (2, 4, 8, 16)
POOL_GROUPS = len(POOL_WINDOWS)
POOL_GROUP_CH = POOL_WIDTH // POOL_GROUPS
POOL_OUT_CH = D_MODEL // POOL_GROUPS
POOL_BUF = max(POOL_WINDOWS) - 1
SSM_WIDTH = D_MODEL // 2
SSM_GROUP_CH = 16
SSM_GROUPS = SSM_WIDTH // SSM_GROUP_CH
SSM_STATE = 64
IN_WIDTH = POOL_WIDTH + SSM_WIDTH + 2 * D_MODEL
D_PACK = D_MODEL // 2
MOE_GROUPS = 4
MOE_EPG = 8
MOE_EXPERTS = MOE_GROUPS * MOE_EPG
MOE_FF = D_MODEL // 4
EPS = 1e-6

T_PROMPT = BATCH * SEQ
T_SAMPLE = DEC_BATCH * DEC_SEQ
T_ALL = T_PROMPT + T_SAMPLE
T_PAD = (BATCH + 1) * SEQ

LANES = 128
SUBLANES = 8
VMEM_LIMIT = 56 * 1024 * 1024

CHUNK = 8
OCT = LANES
N_OCT = SSM_WIDTH // OCT
OCT_GROUPS = OCT // SSM_GROUP_CH
OCT_STATES = OCT_GROUPS * SSM_STATE
CW = CHUNK * OCT
SW = 2 * OCT_STATES

ROUTE_LANES = LANES
EXP_LANE0 = MOE_GROUPS
N_ASSIGN = 2 * T_ALL
TME = 256
N_ITEMS_MAX = N_ASSIGN // TME + MOE_EXPERTS
N_SLOTS = N_ITEMS_MAX * TME


def _cparams(n_axes):
    return pltpu.CompilerParams(dimension_semantics=("arbitrary",) * n_axes,
                                vmem_limit_bytes=VMEM_LIMIT)


def _sigmoid(x):
    return 1.0 / (1.0 + jnp.exp(-x))


def _pack_pairs(x):
    c = x.shape[1] // 2
    hi = lax.bitcast_convert_type(x[:, :c].astype(BF16).astype(F32), U32)
    lo = lax.bitcast_convert_type(x[:, c:].astype(BF16).astype(F32), U32)
    return hi | (lo >> 16)


def _unpack_pairs(u, dtype):
    hi = lax.bitcast_convert_type(u & jnp.uint32(0xFFFF0000), F32)
    lo = lax.bitcast_convert_type(u << 16, F32)
    return jnp.concatenate([hi, lo], axis=1).astype(dtype)


def _gelu_tanh(x):
    c = math.sqrt(2.0 / math.pi)
    return 0.5 * x * (1.0 + jnp.tanh(c * (x + 0.044715 * (x * x * x))))


IN_TM = 1024
IN_TN = 1024
U_WIDTH = POOL_WIDTH + SSM_WIDTH
GATE_WIDTH = 2 * D_MODEL
IN_U_STEPS = U_WIDTH // IN_TN


def _inproj_body(x_ref, g_ref, w_ref, *rest):
    u_ref, gate_ref, xn_ref = rest[-3:]
    j = pl.program_id(1)
    @pl.when(j == 0)
    def _():
        x = x_ref[...]
        inv = lax.rsqrt(jnp.mean(x * x, axis=-1, keepdims=True) + EPS)
        xn_ref[...] = ((x * inv) * g_ref[...]).astype(BF16)
    acc = jnp.dot(xn_ref[...], w_ref[...], preferred_element_type=F32)
    @pl.when(j < IN_U_STEPS)
    def _():
        u_ref[...] = acc
    @pl.when(j >= IN_U_STEPS)
    def _():
        gate_ref[...] = _sigmoid(acc).astype(BF16)


def _inproj(x, g, w_bf, row_block0, dst=None):
    n_i = x.shape[0] // IN_TM
    in_specs = [
        pl.BlockSpec((IN_TM, D_MODEL), lambda i, j: (i, 0)),
        pl.BlockSpec((1, D_MODEL), lambda i, j: (0, 0)),
        pl.BlockSpec((D_MODEL, IN_TN), lambda i, j: (0, j)),
    ]
    args = [x, g, w_bf]
    aliases = {}
    if dst is not None:
        in_specs += [pl.BlockSpec(memory_space=pl.ANY)] * 2
        args += list(dst)
        aliases = {3: 0, 4: 1}
    return pl.pallas_call(
        _inproj_body,
        grid=(n_i, IN_WIDTH // IN_TN),
        in_specs=in_specs,
        out_specs=(
            pl.BlockSpec((IN_TM, IN_TN), lambda i, j: (i + row_block0, jnp.minimum(j, IN_U_STEPS - 1))),
            pl.BlockSpec((IN_TM, IN_TN), lambda i, j: (i + row_block0, jnp.maximum(j - IN_U_STEPS, 0))),
        ),
        out_shape=(
            jax.ShapeDtypeStruct((T_PAD, U_WIDTH), F32),
            jax.ShapeDtypeStruct((T_ALL, GATE_WIDTH), BF16),
        ),
        scratch_shapes=[pltpu.VMEM((IN_TM, D_MODEL), BF16)],
        input_output_aliases=aliases,
        compiler_params=_cparams(2),
        name="inproj",
    )(*args)


PP_TM = 512
HIST = 16


def _pool_project(pooled_g, g, w_ref, sc_ref, o_ref):
    y = jnp.dot(pooled_g.astype(BF16), w_ref[g], preferred_element_type=F32)
    lo, hi = g * POOL_OUT_CH, (g + 1) * POOL_OUT_CH
    o_ref[:, lo:hi] = (y * sc_ref[:, lo:hi]).astype(o_ref.dtype)


def _pool_prompt_body(u_ref, w_ref, sc_ref, o_ref, tail_ref, hist_ref):
    i = pl.program_id(1)
    @pl.when(i == 0)
    def _():
        hist_ref[...] = jnp.zeros_like(hist_ref)
    u = u_ref[...]
    ext = jnp.concatenate([hist_ref[...], u], axis=0)
    hist_ref[...] = u[PP_TM - HIST:, :]
    tail_ref[...] = u[PP_TM - HIST:, :]
    pos = i * PP_TM + lax.broadcasted_iota(I32, (PP_TM, 1), 0)
    for g, w in enumerate(POOL_WINDOWS):
        lo, hi = g * POOL_GROUP_CH, (g + 1) * POOL_GROUP_CH
        s = ext[:, lo:hi]
        d = 1
        while d < w:
            s = s + pltpu.roll(s, d, axis=0)
            d *= 2
        cnt = jnp.minimum(w, pos + 1).astype(F32)
        pooled = s[HIST:, :] / cnt - u[:, lo:hi]
        _pool_project(pooled, g, w_ref, sc_ref, o_ref)


def _pool_prompt(u, w_pool_bf, pool_scale):
    n_i = SEQ // PP_TM
    return pl.pallas_call(
        _pool_prompt_body,
        grid=(BATCH, n_i),
        in_specs=[
            pl.BlockSpec((PP_TM, POOL_WIDTH), lambda b, i: (b * n_i + i, 0)),
            pl.BlockSpec((POOL_GROUPS, POOL_GROUP_CH, POOL_OUT_CH), lambda b, i: (0, 0, 0)),
            pl.BlockSpec((1, D_MODEL), lambda b, i: (0, 0)),
        ],
        out_specs=(
            pl.BlockSpec((PP_TM, D_MODEL), lambda b, i: (b * n_i + i, 0)),
            pl.BlockSpec((None, HIST, POOL_WIDTH), lambda b, i: (b, 0, 0)),
        ),
        out_shape=(
            jax.ShapeDtypeStruct((T_ALL, D_MODEL), BF16),
            jax.ShapeDtypeStruct((BATCH, HIST, POOL_WIDTH), F32),
        ),
        scratch_shapes=[pltpu.VMEM((HIST, POOL_WIDTH), F32)],
        compiler_params=_cparams(2),
        name="pool_prompt",
    )(u, w_pool_bf, pool_scale)


def _pool_sample_body(u_ref, hist_ref, w_ref, sc_ref, _dst, o_ref):
    rows = [hist_ref[k] for k in range(POOL_BUF)]
    rows += [u_ref[DEC_BATCH * t:DEC_BATCH * (t + 1), :] for t in range(DEC_SEQ)]
    n = len(rows)
    for g, w in enumerate(POOL_WINDOWS):
        lo, hi = g * POOL_GROUP_CH, (g + 1) * POOL_GROUP_CH
        f = [r[:, lo:hi] for r in rows]
        cur = f
        d = 1
        while d < w:
            cur = [cur[k] + cur[k - d] if k - d >= 0 else cur[k] for k in range(n)]
            d *= 2
        pooled = jnp.concatenate(
            [cur[POOL_BUF + t] / float(w) - f[POOL_BUF + t] for t in range(DEC_SEQ)], axis=0)
        _pool_project(pooled, g, w_ref, sc_ref, o_ref)


def _pool_sample(u, hist_tm, w_pool_bf, pool_scale, y_pool):
    blk = T_PROMPT // T_SAMPLE
    return pl.pallas_call(
        _pool_sample_body,
        grid=(1,),
        in_specs=[
            pl.BlockSpec((T_SAMPLE, POOL_WIDTH), lambda i: (blk, 0)),
            pl.BlockSpec((POOL_BUF, DEC_BATCH, POOL_WIDTH), lambda i: (0, 0, 0)),
            pl.BlockSpec((POOL_GROUPS, POOL_GROUP_CH, POOL_OUT_CH), lambda i: (0, 0, 0)),
            pl.BlockSpec((1, D_MODEL), lambda i: (0, 0)),
            pl.BlockSpec(memory_space=pl.ANY),
        ],
        out_specs=pl.BlockSpec((T_SAMPLE, D_MODEL), lambda i: (blk, 0)),
        out_shape=jax.ShapeDtypeStruct((T_ALL, D_MODEL), BF16),
        input_output_aliases={4: 0},
        compiler_params=_cparams(1),
        name="pool_sample",
    )(u, hist_tm, w_pool_bf, pool_scale, y_pool)


def _ssm_tables(a_re, a_im, log_dt, b_re, b_im, c_re, c_im, d_skip):
    dt = jnp.exp(log_dt)[:, None]
    lr, li = a_re, a_im
    ab_re = jnp.exp(lr * dt) * jnp.cos(li * dt)
    ab_im = jnp.exp(lr * dt) * jnp.sin(li * dt)
    den = lr * lr + li * li
    nr, ni = ab_re - 1.0, ab_im
    q_re = (nr * lr + ni * li) / den
    q_im = (ni * lr - nr * li) / den
    bb_re = q_re[..., None] * b_re - q_im[..., None] * b_im
    bb_im = q_re[..., None] * b_im + q_im[..., None] * b_re

    def lam_rows(ks):
        k = jnp.asarray(ks, F32)[:, None, None]
        m = jnp.exp(k * lr * dt)
        re = (m * jnp.cos(k * li * dt)).reshape(len(ks), N_OCT, OCT_STATES)
        im = (m * jnp.sin(k * li * dt)).reshape(len(ks), N_OCT, OCT_STATES)
        return jnp.concatenate([re, im], axis=-1).transpose(1, 0, 2)

    def compact(re, im):
        v = jnp.concatenate([re, im], axis=-1)
        return v.reshape(N_OCT, OCT, 2 * SSM_STATE)

    bbc = compact(jnp.swapaxes(bb_re, 1, 2), jnp.swapaxes(bb_im, 1, 2))
    ccc = compact(c_re, c_im)
    pw = lam_rows(list(range(2 * SUBLANES)))
    r = jnp.arange(SUBLANES)[None, :, None]
    parts = [jnp.where(r >= dd, lam_rows([CHUNK * dd]), 0.0) for dd in (1, 2, 4)]
    parts.append(lam_rows([CHUNK * kk for kk in range(1, SUBLANES + 1)]))
    tab = jnp.concatenate(parts, axis=1)
    dsk = d_skip.reshape(N_OCT, 1, OCT)
    return bbc, ccc, pw, tab, dsk


def _split_bf16(x):
    hi = x.astype(BF16)
    return hi, (x - hi.astype(F32)).astype(BF16)


def _dot_nt(a, b):
    return lax.dot_general(a, b, (((1,), (1,)), ((), ())), preferred_element_type=F32)


def _build_weights(bbc_ref, ccc_ref, pw_ref, f_ref, gt_ref, m_ref):
    row_gi = lax.broadcasted_iota(I32, (OCT, 1), 0) >> 4
    col = lax.broadcasted_iota(I32, (1, SW), 1)
    col_gi = (col >> 6) & 7
    src = ((col >> 9) << 6) | (col & 63)
    k128 = lax.broadcasted_iota(I32, (2 * SSM_STATE, 1), 0)
    spread = jnp.where(k128 == src, 1.0, 0.0).astype(BF16)
    diag = row_gi == col_gi

    def expand(c_ref):
        hi, lo = _split_bf16(c_ref[...])
        d = (jnp.dot(hi, spread, preferred_element_type=F32)
             + jnp.dot(lo, spread, preferred_element_type=F32))
        d = jnp.where(diag, d, 0.0)
        return d[:, :OCT_STATES], d[:, OCT_STATES:]

    br, bi = expand(bbc_ref)
    cr, ci = expand(ccc_ref)
    chi_r, clo_r = _split_bf16(cr)
    chi_i, clo_i = _split_bf16(ci)

    def lam(k):
        return pw_ref[k:k + 1, :OCT_STATES], pw_ref[k:k + 1, OCT_STATES:]

    def dot3(a, bhi, blo):
        ahi, alo = _split_bf16(a)
        return _dot_nt(ahi, bhi) + _dot_nt(alo, bhi) + _dot_nt(ahi, blo)

    lags = []
    for k in range(CHUNK):
        pr, pi_ = lam(k)
        fr, fi = _cmul(br, bi, pr, pi_)
        s = CHUNK - 1 - k
        f_ref[s * OCT:(s + 1) * OCT, :] = jnp.concatenate([fr, fi], axis=1).astype(BF16)
        lags.append((dot3(fr, chi_r, clo_r) - dot3(fi, chi_i, clo_i)).astype(BF16))
        pr, pi_ = lam(k + 1)
        gr, gi = _cmul(cr, ci, pr, pi_)
        gt_ref[k * OCT:(k + 1) * OCT, :] = jnp.concatenate([gr, -gi], axis=1).astype(BF16)
    zero = jnp.zeros((OCT, OCT), BF16)
    for s in range(CHUNK):
        for t in range(CHUNK):
            m_ref[s * OCT:(s + 1) * OCT, t * OCT:(t + 1) * OCT] = lags[t - s] if t >= s else zero


def _cmul(ar, ai, br, bi):
    return ar * br - ai * bi, ar * bi + ai * br


def _chunk_scan(sloc, tab_ref):
    R = sloc.shape[0]
    nb = R // SUBLANES
    sr, si = sloc[:, :OCT_STATES], sloc[:, OCT_STATES:]
    rowi = lax.broadcasted_iota(I32, (R, 1), 0)
    tr = jnp.where(rowi == 0, 0.0, pltpu.roll(sr, 1, axis=0))
    ti = jnp.where(rowi == 0, 0.0, pltpu.roll(si, 1, axis=0))
    for lvl, d in enumerate((1, 2, 4)):
        mr = tab_ref[lvl * SUBLANES:(lvl + 1) * SUBLANES, :OCT_STATES]
        mi = tab_ref[lvl * SUBLANES:(lvl + 1) * SUBLANES, OCT_STATES:]
        mr = jnp.concatenate([mr] * nb, axis=0)
        mi = jnp.concatenate([mi] * nb, axis=0)
        pr, pi_ = _cmul(mr, mi, pltpu.roll(tr, d, axis=0), pltpu.roll(ti, d, axis=0))
        tr, ti = tr + pr, ti + pi_
    pwr = tab_ref[3 * SUBLANES:4 * SUBLANES, :OCT_STATES]
    pwi = tab_ref[3 * SUBLANES:4 * SUBLANES, OCT_STATES:]
    cr = jnp.zeros((1, OCT_STATES), F32)
    ci = jnp.zeros((1, OCT_STATES), F32)
    out_r, out_i = [], []
    for k in range(nb):
        ar = tr[k * SUBLANES:(k + 1) * SUBLANES, :]
        ai = ti[k * SUBLANES:(k + 1) * SUBLANES, :]
        pr, pi_ = _cmul(pwr, pwi, jnp.broadcast_to(cr, ar.shape), jnp.broadcast_to(ci, ai.shape))
        hr, hi = ar + pr, ai + pi_
        out_r.append(hr)
        out_i.append(hi)
        cr, ci = hr[SUBLANES - 1:, :], hi[SUBLANES - 1:, :]
    hin = jnp.concatenate([jnp.concatenate(out_r, axis=0), jnp.concatenate(out_i, axis=0)], axis=1)
    lr, li = pwr[0:1, :], pwi[0:1, :]
    fr, fi = _cmul(lr, li, cr, ci)
    fin = jnp.concatenate([fr + sr[R - 1:, :], fi + si[R - 1:, :]], axis=1)
    return hin, fin


def _ssm_body(u_ref, h0r_ref, h0i_ref, bbc_ref, ccc_ref, pw_ref, tab_ref, d_ref,
              y_ref, hout_ref, hr_ref, hi_ref, f_ref, gt_ref, m_ref):
    b = pl.program_id(1)

    @pl.when(b == 0)
    def _():
        _build_weights(bbc_ref, ccc_ref, pw_ref, f_ref, gt_ref, m_ref)

    def outputs(xs, xb, hin_bf):
        y = (jnp.dot(xb, m_ref[...], preferred_element_type=F32) + _dot_nt(hin_bf, gt_ref[...]))
        return [_gelu_tanh(y[:, t * OCT:(t + 1) * OCT] + d_ref[...] * xs[t]) for t in range(CHUNK)]

    @pl.when(b < BATCH)
    def _():
        R = SEQ // CHUNK
        xs = [u_ref[pl.ds(s, R, stride=CHUNK), :] for s in range(CHUNK)]
        xb = jnp.concatenate(xs, axis=1).astype(BF16)
        sloc = jnp.dot(xb, f_ref[...], preferred_element_type=F32)
        hin, fin = _chunk_scan(sloc, tab_ref)
        for t, yt in enumerate(outputs(xs, xb, hin.astype(BF16))):
            y_ref[pl.ds(t, R, stride=CHUNK), :] = yt
        hout_ref[...] = fin

    @pl.when(b == BATCH)
    def _():
        B = DEC_BATCH
        xs = [u_ref[B * s:B * (s + 1), :] for s in range(CHUNK)]
        xb = jnp.concatenate(xs, axis=1).astype(BF16)
        sloc = jnp.dot(xb, f_ref[...], preferred_element_type=F32)
        h0r, h0i = h0r_ref[...], h0i_ref[...]
        hin = jnp.concatenate([h0r, h0i], axis=1).astype(BF16)
        for t, yt in enumerate(outputs(xs, xb, hin)):
            y_ref[B * t:B * (t + 1), :] = yt
        lr = tab_ref[3 * SUBLANES:3 * SUBLANES + 1, :OCT_STATES]
        li = tab_ref[3 * SUBLANES:3 * SUBLANES + 1, OCT_STATES:]
        nr, ni = _cmul(lr, li, h0r, h0i)
        hr_ref[...] = nr + sloc[:, :OCT_STATES]
        hi_ref[...] = ni + sloc[:, OCT_STATES:]


def _ssm(u, h0r, h0i, tables):
    col0 = POOL_WIDTH // OCT
    im3 = lambda o, b: (o, 0, 0)
    st_spec = pl.BlockSpec((DEC_BATCH, OCT_STATES), lambda o, b: (0, o))
    return pl.pallas_call(
        _ssm_body,
        grid=(N_OCT, BATCH + 1),
        in_specs=[
            pl.BlockSpec((SEQ, OCT), lambda o, b: (b, col0 + o)), st_spec, st_spec,
            pl.BlockSpec((None, OCT, 2 * SSM_STATE), im3),
            pl.BlockSpec((None, OCT, 2 * SSM_STATE), im3),
            pl.BlockSpec((None, 2 * SUBLANES, SW), im3),
            pl.BlockSpec((None, 4 * SUBLANES, SW), im3),
            pl.BlockSpec((None, 1, OCT), im3),
        ],
        out_specs=(
            pl.BlockSpec((SEQ, OCT), lambda o, b: (b, o)),
            pl.BlockSpec((None, 1, SW), lambda o, b: (jnp.minimum(b, BATCH - 1) * N_OCT + o, 0, 0)),
            st_spec, st_spec,
        ),
        out_shape=(
            jax.ShapeDtypeStruct((T_PAD, SSM_WIDTH), F32),
            jax.ShapeDtypeStruct((BATCH * N_OCT, 1, SW), F32),
            jax.ShapeDtypeStruct((DEC_BATCH, SSM_GROUPS * SSM_STATE), F32),
            jax.ShapeDtypeStruct((DEC_BATCH, SSM_GROUPS * SSM_STATE), F32),
        ),
        scratch_shapes=[pltpu.VMEM((CW, SW), BF16), pltpu.VMEM((CW, SW), BF16), pltpu.VMEM((CW, CW), BF16)],
        compiler_params=_cparams(2),
        name="ssm",
    )(u, h0r, h0i, *tables)


PM_TM = 256
PM_PROMPT_BLOCKS = T_PROMPT // PM_TM
PM_STEPS = T_ALL // PM_TM


def _route(logits, valid, cnt_ref):
    lane = lax.broadcasted_iota(I32, (PM_TM, ROUTE_LANES), 1)
    neg = jnp.float32(-jnp.inf)
    big = jnp.int32(1 << 20)
    is_g = lane < MOE_GROUPS
    gmax = jnp.max(jnp.where(is_g, logits, neg), axis=1, keepdims=True)
    g_idx = jnp.min(jnp.where(is_g & (logits == gmax), lane, big), axis=1, keepdims=True)
    g_den = jnp.sum(jnp.where(is_g, jnp.exp(logits - gmax), 0.0), axis=1, keepdims=True)
    g_val = 1.0 / g_den
    e_lane = lane - EXP_LANE0
    sel = (e_lane >= 0) & (e_lane < MOE_EXPERTS) & ((e_lane >> 3) == g_idx)
    m1 = jnp.max(jnp.where(sel, logits, neg), axis=1, keepdims=True)
    i1 = jnp.min(jnp.where(sel & (logits == m1), lane, big), axis=1, keepdims=True)
    sel2 = sel & (lane != i1)
    m2 = jnp.max(jnp.where(sel2, logits, neg), axis=1, keepdims=True)
    i2 = jnp.min(jnp.where(sel2 & (logits == m2), lane, big), axis=1, keepdims=True)
    e2 = jnp.exp(m2 - m1)
    w1 = g_val / (1.0 + e2)
    w2 = g_val * e2 / (1.0 + e2)
    oh1 = lane == i1
    oh2 = lane == i2
    oh = jnp.where(oh1 | oh2, valid, 0.0)
    rr = lax.broadcasted_iota(I32, (PM_TM, PM_TM), 0)
    cc = lax.broadcasted_iota(I32, (PM_TM, PM_TM), 1)
    tri = jnp.where(cc < rr, 1.0, 0.0).astype(BF16)
    base = cnt_ref[...] + jnp.dot(tri, oh.astype(BF16), preferred_element_type=F32)
    rank1 = jnp.sum(jnp.where(oh1, base, 0.0), axis=1, keepdims=True)
    rank2 = jnp.sum(jnp.where(oh2, base, 0.0), axis=1, keepdims=True)
    cnt_ref[...] = cnt_ref[...] + jnp.sum(oh, axis=0, keepdims=True)
    rt = jnp.where(lane == 0, w1, 0.0)
    rt = jnp.where(lane == 1, w2, rt)
    rt = jnp.where(lane == 2, rank1, rt)
    rt = jnp.where(lane == 3, rank2, rt)
    rt = jnp.where(lane == 4, (i1 - EXP_LANE0).astype(F32), rt)
    rt = jnp.where(lane == 5, (i2 - EXP_LANE0).astype(F32), rt)
    return rt


def _postmix_body(ya_ref, gp_ref, gs_ref, yp_ref, xp_ref, xs_ref, wa_ref, wb_ref, wo_ref,
                  gf_ref, wr_ref, br_ref, h_ref, tn_ref, rt_ref, cnt_out_ref,
                  cnt_ref, lg_ref):
    i = pl.program_id(0)
    @pl.when(i == 0)
    def _():
        cnt_ref[...] = jnp.zeros_like(cnt_ref)
        lg_ref[...] = jnp.zeros_like(lg_ref)
    prev_logits = lg_ref[...]

    ya = ya_ref[...].astype(BF16)
    a = jnp.dot(ya, wa_ref[...], preferred_element_type=F32)
    bg = jnp.dot(ya, wb_ref[...], preferred_element_type=F32)
    y_ssm = a * _sigmoid(bg)
    merged = (gp_ref[...].astype(F32) * yp_ref[...].astype(F32)
              + gs_ref[...].astype(F32) * y_ssm)
    x = jnp.where(jnp.minimum(i, PM_STEPS - 1) < PM_PROMPT_BLOCKS, xp_ref[...], xs_ref[...])
    h = x + jnp.dot(merged.astype(BF16), wo_ref[...], preferred_element_type=F32)
    h_ref[...] = h
    inv = lax.rsqrt(jnp.mean(h * h, axis=-1, keepdims=True) + EPS)
    tn = (h * inv) * gf_ref[...]
    tn_ref[...] = _pack_pairs(tn)
    t_hi = tn.astype(BF16)
    t_lo = (tn - t_hi.astype(F32)).astype(BF16)
    hh = jnp.dot(t_hi, wr_ref[...], preferred_element_type=F32)
    lh = jnp.dot(t_lo, wr_ref[:, :ROUTE_LANES], preferred_element_type=F32)
    lg_ref[...] = (hh[:, :ROUTE_LANES] + lh + hh[:, ROUTE_LANES:]) + br_ref[...]
    rt_ref[...] = _route(prev_logits, jnp.where(i > 0, 1.0, 0.0), cnt_ref)
    cnt_out_ref[...] = cnt_ref[...]


def _postmix(y_act, gates, y_pool, xp, xs, wa, wb, wo, g_ffn, wr_cat, b_r):
    npb = PM_PROMPT_BLOCKS
    const2 = lambda i: (0, 0)
    tile = lambda i: jnp.minimum(i, PM_STEPS - 1)
    return pl.pallas_call(
        _postmix_body,
        grid=(PM_STEPS + 1,),
        in_specs=[
            pl.BlockSpec((PM_TM, SSM_WIDTH), lambda i: (tile(i), 0)),
            pl.BlockSpec((PM_TM, D_MODEL), lambda i: (tile(i), 0)),
            pl.BlockSpec((PM_TM, D_MODEL), lambda i: (tile(i), 1)),
            pl.BlockSpec((PM_TM, D_MODEL), lambda i: (tile(i), 0)),
            pl.BlockSpec((PM_TM, D_MODEL), lambda i: (jnp.minimum(i, npb - 1), 0)),
            pl.BlockSpec((PM_TM, D_MODEL), lambda i: (jnp.maximum(tile(i) - npb, 0), 0)),
            pl.BlockSpec((SSM_WIDTH, D_MODEL), const2, pipeline_mode=pl.Buffered(1)),
            pl.BlockSpec((SSM_WIDTH, D_MODEL), const2, pipeline_mode=pl.Buffered(1)),
            pl.BlockSpec((D_MODEL, D_MODEL), const2, pipeline_mode=pl.Buffered(1)),
            pl.BlockSpec((1, D_MODEL), const2),
            pl.BlockSpec((D_MODEL, 2 * ROUTE_LANES), const2),
            pl.BlockSpec((1, ROUTE_LANES), const2),
        ],
        out_specs=(
            pl.BlockSpec((PM_TM, D_MODEL), lambda i: (tile(i), 0)),
            pl.BlockSpec((PM_TM, D_PACK), lambda i: (tile(i), 0)),
            pl.BlockSpec((PM_TM, ROUTE_LANES), lambda i: (jnp.maximum(i - 1, 0), 0)),
            pl.BlockSpec((1, ROUTE_LANES), const2),
        ),
        out_shape=(
            jax.ShapeDtypeStruct((T_ALL, D_MODEL), F32),
            jax.ShapeDtypeStruct((T_ALL, D_PACK), U32),
            jax.ShapeDtypeStruct((T_ALL, ROUTE_LANES), F32),
            jax.ShapeDtypeStruct((1, ROUTE_LANES), F32),
        ),
        scratch_shapes=[pltpu.VMEM((1, ROUTE_LANES), F32), pltpu.VMEM((PM_TM, ROUTE_LANES), F32)],
        compiler_params=_cparams(1),
        name="postmix",
    )(y_act, gates, gates, y_pool, xp, xs, wa, wb, wo, g_ffn, wr_cat, b_r)


SC_CH = 32


def _sc_workers():
    info = plsc.get_sparse_core_info()
    return info.num_cores, info.num_cores * info.num_subcores


def _sc_dispatch(tn, slots):
    n_cores, n_workers = _sc_workers()
    per_w = (T_ALL // SC_CH) // n_workers
    assert per_w * n_workers * SC_CH == T_ALL
    slots = slots.reshape(2, n_workers, per_w, SC_CH)

    @functools.partial(
        pl.kernel,
        mesh=plsc.VectorSubcoreMesh(core_axis_name="c", subcore_axis_name="s"),
        out_type=jax.ShapeDtypeStruct((N_SLOTS, D_PACK), U32),
        scratch_types=[pltpu.VMEM((2, per_w, SC_CH), I32), pltpu.VMEM((SC_CH, D_PACK), U32)],
    )
    def k(tn_hbm, slots_hbm, xs_hbm, idx_v, rows_v):
        wid = lax.axis_index("s") * n_cores + lax.axis_index("c")
        c0 = wid * per_w
        pltpu.sync_copy(slots_hbm.at[0, wid], idx_v.at[0])
        pltpu.sync_copy(slots_hbm.at[1, wid], idx_v.at[1])

        @pl.loop(0, per_w)
        def _(c):
            row0 = pl.multiple_of((c0 + c) * SC_CH, SC_CH)
            pltpu.sync_copy(tn_hbm.at[pl.ds(row0, SC_CH)], rows_v)
            pltpu.sync_copy(rows_v, xs_hbm.at[idx_v.at[0, c]])
            pltpu.sync_copy(rows_v, xs_hbm.at[idx_v.at[1, c]])

    return k(tn, slots)


def _sc_collect(ys, slots):
    n_cores, n_workers = _sc_workers()
    per_w = (N_ASSIGN // SC_CH) // n_workers
    assert per_w * n_workers * SC_CH == N_ASSIGN
    slots = slots.reshape(n_workers, per_w, SC_CH)

    @functools.partial(
        pl.kernel,
        mesh=plsc.VectorSubcoreMesh(core_axis_name="c", subcore_axis_name="s"),
        out_type=jax.ShapeDtypeStruct((N_ASSIGN, D_PACK), U32),
        scratch_types=[pltpu.VMEM((per_w, SC_CH), I32), pltpu.VMEM((SC_CH, D_PACK), U32)],
    )
    def k(ys_hbm, slots_hbm, out_hbm, idx_v, rows_v):
        wid = lax.axis_index("s") * n_cores + lax.axis_index("c")
        c0 = wid * per_w
        pltpu.sync_copy(slots_hbm.at[wid], idx_v)

        @pl.loop(0, per_w)
        def _(c):
            row0 = pl.multiple_of((c0 + c) * SC_CH, SC_CH)
            pltpu.sync_copy(ys_hbm.at[idx_v.at[c]], rows_v)
            pltpu.sync_copy(rows_v, out_hbm.at[pl.ds(row0, SC_CH)])

    return k(ys, slots)


W_PARTS = 2


def _expert_body(t0_ref, nt_ref, xs_hbm, wg_hbm, wu_hbm, wd_hbm, ys_hbm,
                 wg_ref, wu_ref, wd_ref, xb_ref, yb_ref, wgb_ref, wub_ref, wdb_ref, wsem, xsem, ysem):
    e = pl.program_id(0)
    n = nt_ref[e]
    g0 = t0_ref[e]
    ws = e & 1

    def w_copies(ex, slot):
        out = []
        for hbm, buf in ((wg_hbm, wg_ref), (wu_hbm, wu_ref), (wd_hbm, wd_ref)):
            rb = buf.shape[1] // W_PARTS
            for p in range(W_PARTS):
                out.append((pltpu.make_async_copy(hbm.at[ex, pl.ds(p * rb, rb)],
                                                  buf.at[slot, pl.ds(p * rb, rb)], wsem.at[slot]), p))
        return out

    @pl.when(e == 0)
    def _():
        for cp, p in w_copies(0, 0):
            cp.start(priority=p)

    @pl.when(e + 1 < MOE_EXPERTS)
    def _():
        for cp, p in w_copies(e + 1, 1 - ws):
            cp.start(priority=p)

    for cp, _ in w_copies(e, ws):
        cp.wait()

    def rows(j):
        return pl.ds(pl.multiple_of((g0 + j) * TME, TME), TME)

    def x_copy(j, s):
        return pltpu.make_async_copy(xs_hbm.at[rows(j)], xb_ref.at[s], xsem.at[s])

    def y_copy(j, s):
        return pltpu.make_async_copy(yb_ref.at[s], ys_hbm.at[rows(j)], ysem.at[s])

    @pl.when(n > 0)
    def _():
        x_copy(0, 0).start()
        wgb_ref[...] = wg_ref[ws].astype(BF16)
        wub_ref[...] = wu_ref[ws].astype(BF16)
        wdb_ref[...] = wd_ref[ws].astype(BF16)

        def tile(j, c):
            s = j & 1
            x_copy(j, s).wait()
            @pl.when(j + 1 < n)
            def _():
                x_copy(j + 1, 1 - s).start()
            @pl.when(j >= 2)
            def _():
                y_copy(j - 2, s).wait()
            x = _unpack_pairs(xb_ref[s], BF16)
            hg = jnp.dot(x, wgb_ref[...], preferred_element_type=F32)
            hu = jnp.dot(x, wub_ref[...], preferred_element_type=F32)
            act = (hg * _sigmoid(hg)) * hu
            yb_ref[s] = _pack_pairs(jnp.dot(act.astype(BF16), wdb_ref[...], preferred_element_type=F32))
            y_copy(j, s).start()
            return c
        lax.fori_loop(0, n, tile, 0)

        @pl.when(n >= 2)
        def _():
            y_copy(n - 2, n & 1).wait()
        y_copy(n - 1, (n - 1) & 1).wait()


def _experts(tile0, tiles, xs, w_eg, w_eu, w_ed):
    any_spec = pl.BlockSpec(memory_space=pl.ANY)
    grid_spec = pltpu.PrefetchScalarGridSpec(
        num_scalar_prefetch=2,
        grid=(MOE_EXPERTS,),
        in_specs=[any_spec] * 4,
        out_specs=any_spec,
        scratch_shapes=[
            pltpu.VMEM((2, D_MODEL, MOE_FF), F32),
            pltpu.VMEM((2, D_MODEL, MOE_FF), F32),
            pltpu.VMEM((2, MOE_FF, D_MODEL), F32),
            pltpu.VMEM((2, TME, D_PACK), U32),
            pltpu.VMEM((2, TME, D_PACK), U32),
            pltpu.VMEM((D_MODEL, MOE_FF), BF16),
            pltpu.VMEM((D_MODEL, MOE_FF), BF16),
            pltpu.VMEM((MOE_FF, D_MODEL), BF16),
            pltpu.SemaphoreType.DMA((2,)),
            pltpu.SemaphoreType.DMA((2,)),
            pltpu.SemaphoreType.DMA((2,)),
        ],
    )
    return pl.pallas_call(
        _expert_body,
        grid_spec=grid_spec,
        out_shape=jax.ShapeDtypeStruct((N_SLOTS, D_PACK), U32),
        compiler_params=_cparams(1),
        name="experts",
    )(tile0, tiles, xs, w_eg, w_eu, w_ed)


FN_TM = 256
FN_PROMPT_BLOCKS = T_PROMPT // FN_TM


def _final_body(h_ref, y0_ref, y1_ref, rt_ref, g_ref, op_ref, os_ref):
    i = pl.program_id(0)
    rt = rt_ref[...]
    z = (h_ref[...] + rt[:, 0:1] * _unpack_pairs(y0_ref[...], F32)
         + rt[:, 1:2] * _unpack_pairs(y1_ref[...], F32))
    inv = lax.rsqrt(jnp.mean(z * z, axis=-1, keepdims=True) + EPS)
    out = (z * inv) * g_ref[...]
    @pl.when(i < FN_PROMPT_BLOCKS)
    def _():
        op_ref[...] = out
    @pl.when(i >= FN_PROMPT_BLOCKS)
    def _():
        os_ref[...] = out


def _final(h, y, route, g_final):
    n = T_ALL // FN_TM
    npb = FN_PROMPT_BLOCKS
    yoff = T_ALL // FN_TM
    return pl.pallas_call(
        _final_body,
        grid=(n,),
        in_specs=[
            pl.BlockSpec((FN_TM, D_MODEL), lambda i: (i, 0)),
            pl.BlockSpec((FN_TM, D_PACK), lambda i: (i, 0)),
            pl.BlockSpec((FN_TM, D_PACK), lambda i: (yoff + i, 0)),
            pl.BlockSpec((FN_TM, ROUTE_LANES), lambda i: (i, 0)),
            pl.BlockSpec((1, D_MODEL), lambda i: (0, 0)),
        ],
        out_specs=(
            pl.BlockSpec((FN_TM, D_MODEL), lambda i: (jnp.minimum(i, npb - 1), 0)),
            pl.BlockSpec((FN_TM, D_MODEL), lambda i: (jnp.maximum(i - npb, 0), 0)),
        ),
        out_shape=(
            jax.ShapeDtypeStruct((T_PROMPT, D_MODEL), F32),
            jax.ShapeDtypeStruct((T_SAMPLE, D_MODEL), F32),
        ),
        compiler_params=_cparams(1),
        name="final",
    )(h, y, y, route, g_final)


def _dispatch_plan(route, cnt):
    counts = cnt[0, EXP_LANE0:EXP_LANE0 + MOE_EXPERTS].astype(I32)
    tiles = (counts + (TME - 1)) // TME
    cumt = jnp.cumsum(tiles)
    pad_off = (cumt - tiles) * TME
    rank = route[:, 2:4].astype(I32)
    eid = route[:, 4:6].astype(I32)
    onehot = eid[..., None] == jnp.arange(MOE_EXPERTS, dtype=I32)
    slots = (rank + jnp.sum(jnp.where(onehot, pad_off, 0), axis=-1)).T
    return slots, cumt - tiles, tiles


def kernel(x_prompt, x_sample, state_pool, state_ssm_re, state_ssm_im, g_mix, w_in, w_pool,
           pool_scale, ssm_a_re, ssm_a_im, ssm_log_dt, ssm_b_re, ssm_b_im, ssm_c_re, ssm_c_im,
           ssm_d, w_glu_a, w_glu_b, w_out, g_ffn, w_router_group, b_router_group,
           w_router_expert, b_router_expert, w_exp_gate, w_exp_up, w_exp_down, g_final):
    l = 0
    xp = x_prompt.reshape(T_PROMPT, D_MODEL)
    xs = x_sample.transpose(1, 0, 2).reshape(T_SAMPLE, D_MODEL)
    w_in_bf = w_in[l].astype(BF16)
    w_pool_bf = w_pool[l].astype(BF16)
    wa_bf = w_glu_a[l].astype(BF16)
    wb_bf = w_glu_b[l].astype(BF16)
    wo_bf = w_out[l].astype(BF16)
    g_mix2 = g_mix[l].reshape(1, D_MODEL)
    scale2 = pool_scale[l].reshape(1, D_MODEL)

    u, gates = _inproj(xp, g_mix2, w_in_bf, 0)
    u, gates = _inproj(xs, g_mix2, w_in_bf, T_PROMPT // IN_TM, dst=(u, gates))

    y_pool, pool_tail = _pool_prompt(u, w_pool_bf, scale2)
    hist_tm = state_pool[l].transpose(1, 0, 2)
    y_pool = _pool_sample(u, hist_tm, w_pool_bf, scale2, y_pool)

    tables = _ssm_tables(ssm_a_re[l], ssm_a_im[l], ssm_log_dt[l], ssm_b_re[l], ssm_b_im[l],
                         ssm_c_re[l], ssm_c_im[l], ssm_d[l])
    h0r = state_ssm_re[l].reshape(DEC_BATCH, SSM_GROUPS * SSM_STATE)
    h0i = state_ssm_im[l].reshape(DEC_BATCH, SSM_GROUPS * SSM_STATE)
    y_act, h_prompt, hs_re, hs_im = _ssm(u, h0r, h0i, tables)

    w_r = jnp.zeros((D_MODEL, ROUTE_LANES), F32)
    w_r = w_r.at[:, :MOE_GROUPS].set(w_router_group[l])
    w_r = w_r.at[:, EXP_LANE0:EXP_LANE0 + MOE_EXPERTS].set(w_router_expert[l])
    wr_hi = w_r.astype(BF16)
    wr_cat = jnp.concatenate([wr_hi, (w_r - wr_hi.astype(F32)).astype(BF16)], axis=1)
    b_r = jnp.zeros((1, ROUTE_LANES), F32)
    b_r = b_r.at[0, :MOE_GROUPS].set(b_router_group[l])
    b_r = b_r.at[0, EXP_LANE0:EXP_LANE0 + MOE_EXPERTS].set(b_router_expert[l])

    h, tn, route, cnt = _postmix(y_act, gates, y_pool, xp, xs, wa_bf, wb_bf, wo_bf,
                                 g_ffn[l].reshape(1, D_MODEL), wr_cat, b_r)
    slots, tile0, tiles = _dispatch_plan(route, cnt)
    xs_sorted = _sc_dispatch(tn, slots)
    ys_sorted = _experts(tile0, tiles, xs_sorted, w_exp_gate[l], w_exp_up[l], w_exp_down[l])
    y = _sc_collect(ys_sorted, slots)
    yp, ys = _final(h, y, route, g_final.reshape(1, D_MODEL))

    y_prompt = yp.reshape(BATCH, SEQ, D_MODEL)
    y_sample = ys.reshape(DEC_SEQ, DEC_BATCH, D_MODEL).transpose(1, 0, 2)
    new_pool_prompt = pool_tail[:, HIST - POOL_BUF:, :][None]
    us = u[T_PROMPT:T_ALL, :POOL_WIDTH].reshape(DEC_SEQ, DEC_BATCH, POOL_WIDTH).transpose(1, 0, 2)
    new_pool_sample = jnp.concatenate([state_pool[l][:, DEC_SEQ:, :], us], axis=1)[None]
    hp = h_prompt.reshape(BATCH, N_OCT, 2, OCT_GROUPS, SSM_STATE).transpose(2, 0, 1, 3, 4)
    hp = hp.reshape(2, BATCH, SSM_GROUPS, SSM_STATE)
    shp = (1, DEC_BATCH, SSM_GROUPS, SSM_STATE)
    return (y_prompt, y_sample, new_pool_prompt, hp[0][None], hp[1][None], new_pool_sample,
            hs_re.reshape(shp), hs_im.reshape(shp))
```

```python
import functools
import math

import jax
import jax.numpy as jnp
from jax import lax
from jax.experimental import pallas as pl
from jax.experimental.pallas import tpu as pltpu
from jax.experimental.pallas import tpu_sc as plsc

F32 = jnp.float32
BF16 = jnp.bfloat16
I32 = jnp.int32
U32 = jnp.uint32

D_MODEL = 2048
BATCH = 4
SEQ = 2048
DEC_BATCH = 128
DEC_SEQ = 8
PAST_LEN = 16384
POOL_WIDTH = D_MODEL // 2
POOL_WINDOWS = (2, 4, 8, 16)
POOL_GROUPS = len(POOL_WINDOWS)
POOL_GROUP_CH = POOL_WIDTH // POOL_GROUPS
POOL_OUT_CH = D_MODEL // POOL_GROUPS
POOL_BUF = max(POOL_WINDOWS) - 1
SSM_WIDTH = D_MODEL // 2
SSM_GROUP_CH = 16
SSM_GROUPS = SSM_WIDTH // SSM_GROUP_CH
SSM_STATE = 64
IN_WIDTH = POOL_WIDTH + SSM_WIDTH + 2 * D_MODEL
D_PACK = D_MODEL // 2
MOE_GROUPS = 4
MOE_EPG = 8
MOE_EXPERTS = MOE_GROUPS * MOE_EPG
MOE_FF = D_MODEL // 4
EPS = 1e-6

T_PROMPT = BATCH * SEQ
T_SAMPLE = DEC_BATCH * DEC_SEQ
T_ALL = T_PROMPT + T_SAMPLE
T_PAD = (BATCH + 1) * SEQ

LANES = 128
SUBLANES = 8
VMEM_LIMIT = 56 * 1024 * 1024

CHUNK = 8
OCT = LANES
N_OCT = SSM_WIDTH // OCT
OCT_GROUPS = OCT // SSM_GROUP_CH
OCT_STATES = OCT_GROUPS * SSM_STATE
CW = CHUNK * OCT
SW = 2 * OCT_STATES

ROUTE_LANES = LANES
EXP_LANE0 = MOE_GROUPS
N_ASSIGN = 2 * T_ALL
TME = 256
N_ITEMS_MAX = N_ASSIGN // TME + MOE_EXPERTS
N_SLOTS = N_ITEMS_MAX * TME


def _cparams(n_axes):
    return pltpu.CompilerParams(dimension_semantics=("arbitrary",) * n_axes,
                                vmem_limit_bytes=VMEM_LIMIT)


def _sigmoid(x):
    return 1.0 / (1.0 + jnp.exp(-x))


def _pack_pairs(x):
    c = x.shape[1] // 2
    hi = lax.bitcast_convert_type(x[:, :c].astype(BF16).astype(F32), U32)
    lo = lax.bitcast_convert_type(x[:, c:].astype(BF16).astype(F32), U32)
    return hi | (lo >> 16)


def _unpack_pairs(u, dtype):
    hi = lax.bitcast_convert_type(u & jnp.uint32(0xFFFF0000), F32)
    lo = lax.bitcast_convert_type(u << 16, F32)
    return jnp.concatenate([hi, lo], axis=1).astype(dtype)


def _gelu_tanh(x):
    c = math.sqrt(2.0 / math.pi)
    return 0.5 * x * (1.0 + jnp.tanh(c * (x + 0.044715 * (x * x * x))))


IN_TM = 1024
IN_TN = 1024
U_WIDTH = POOL_WIDTH + SSM_WIDTH
GATE_WIDTH = 2 * D_MODEL
IN_U_STEPS = U_WIDTH // IN_TN


def _inproj_body(x_ref, g_ref, w_ref, *rest):
    u_ref, gate_ref, xn_ref = rest[-3:]
    j = pl.program_id(1)
    @pl.when(j == 0)
    def _():
        x = x_ref[...]
        inv = lax.rsqrt(jnp.mean(x * x, axis=-1, keepdims=True) + EPS)
        xn_ref[...] = ((x * inv) * g_ref[...]).astype(BF16)
    acc = jnp.dot(xn_ref[...], w_ref[...], preferred_element_type=F32)
    @pl.when(j < IN_U_STEPS)
    def _():
        u_ref[...] = acc
    @pl.when(j >= IN_U_STEPS)
    def _():
        gate_ref[...] = _sigmoid(acc).astype(BF16)


def _inproj(x, g, w_bf, row_block0, dst=None):
    n_i = x.shape[0] // IN_TM
    in_specs = [
        pl.BlockSpec((IN_TM, D_MODEL), lambda i, j: (i, 0)),
        pl.BlockSpec((1, D_MODEL), lambda i, j: (0, 0)),
        pl.BlockSpec((D_MODEL, IN_TN), lambda i, j: (0, j)),
    ]
    args = [x, g, w_bf]
    aliases = {}
    if dst is not None:
        in_specs += [pl.BlockSpec(memory_space=pl.ANY)] * 2
        args += list(dst)
        aliases = {3: 0, 4: 1}
    return pl.pallas_call(
        _inproj_body,
        grid=(n_i, IN_WIDTH // IN_TN),
        in_specs=in_specs,
        out_specs=(
            pl.BlockSpec((IN_TM, IN_TN), lambda i, j: (i + row_block0, jnp.minimum(j, IN_U_STEPS - 1))),
            pl.BlockSpec((IN_TM, IN_TN), lambda i, j: (i + row_block0, jnp.maximum(j - IN_U_STEPS, 0))),
        ),
        out_shape=(
            jax.ShapeDtypeStruct((T_PAD, U_WIDTH), F32),
            jax.ShapeDtypeStruct((T_ALL, GATE_WIDTH), BF16),
        ),
        scratch_shapes=[pltpu.VMEM((IN_TM, D_MODEL), BF16)],
        input_output_aliases=aliases,
        compiler_params=_cparams(2),
        name="inproj",
    )(*args)


PP_TM = 512
HIST = 16


def _pool_project(pooled_g, g, w_ref, sc_ref, o_ref):
    y = jnp.dot(pooled_g.astype(BF16), w_ref[g], preferred_element_type=F32)
    lo, hi = g * POOL_OUT_CH, (g + 1) * POOL_OUT_CH
    o_ref[:, lo:hi] = (y * sc_ref[:, lo:hi]).astype(o_ref.dtype)


def _pool_prompt_body(u_ref, w_ref, sc_ref, o_ref, tail_ref, hist_ref):
    i = pl.program_id(1)
    @pl.when(i == 0)
    def _():
        hist_ref[...] = jnp.zeros_like(hist_ref)
    u = u_ref[...]
    ext = jnp.concatenate([hist_ref[...], u], axis=0)
    hist_ref[...] = u[PP_TM - HIST:, :]
    tail_ref[...] = u[PP_TM - HIST:, :]
    pos = i * PP_TM + lax.broadcasted_iota(I32, (PP_TM, 1), 0)
    for g, w in enumerate(POOL_WINDOWS):
        lo, hi = g * POOL_GROUP_CH, (g + 1) * POOL_GROUP_CH
        s = ext[:, lo:hi]
        d = 1
        while d < w:
            s = s + pltpu.roll(s, d, axis=0)
            d *= 2
        cnt = jnp.minimum(w, pos + 1).astype(F32)
        pooled = s[HIST:, :] / cnt - u[:, lo:hi]
        _pool_project(pooled, g, w_ref, sc_ref, o_ref)


def _pool_prompt(u, w_pool_bf, pool_scale):
    n_i = SEQ // PP_TM
    return pl.pallas_call(
        _pool_prompt_body,
        grid=(BATCH, n_i),
        in_specs=[
            pl.BlockSpec((PP_TM, POOL_WIDTH), lambda b, i: (b * n_i + i, 0)),
            pl.BlockSpec((POOL_GROUPS, POOL_GROUP_CH, POOL_OUT_CH), lambda b, i: (0, 0, 0)),
            pl.BlockSpec((1, D_MODEL), lambda b, i: (0, 0)),
        ],
        out_specs=(
            pl.BlockSpec((PP_TM, D_MODEL), lambda b, i: (b * n_i + i, 0)),
            pl.BlockSpec((None, HIST, POOL_WIDTH), lambda b, i: (b, 0, 0)),
        ),
        out_shape=(
            jax.ShapeDtypeStruct((T_ALL, D_MODEL), BF16),
            jax.ShapeDtypeStruct((BATCH, HIST, POOL_WIDTH), F32),
        ),
        scratch_shapes=[pltpu.VMEM((HIST, POOL_WIDTH), F32)],
        compiler_params=_cparams(2),
        name="pool_prompt",
    )(u, w_pool_bf, pool_scale)


def _pool_sample_body(u_ref, hist_ref, w_ref, sc_ref, _dst, o_ref):
    rows = [hist_ref[k] for k in range(POOL_BUF)]
    rows += [u_ref[DEC_BATCH * t:DEC_BATCH * (t + 1), :] for t in range(DEC_SEQ)]
    n = len(rows)
    for g, w in enumerate(POOL_WINDOWS):
        lo, hi = g * POOL_GROUP_CH, (g + 1) * POOL_GROUP_CH
        f = [r[:, lo:hi] for r in rows]
        cur = f
        d = 1
        while d < w:
            cur = [cur[k] + cur[k - d] if k - d >= 0 else cur[k] for k in range(n)]
            d *= 2
        pooled = jnp.concatenate(
            [cur[POOL_BUF + t] / float(w) - f[POOL_BUF + t] for t in range(DEC_SEQ)], axis=0)
        _pool_project(pooled, g, w_ref, sc_ref, o_ref)


def _pool_sample(u, hist_tm, w_pool_bf, pool_scale, y_pool):
    blk = T_PROMPT // T_SAMPLE
    return pl.pallas_call(
        _pool_sample_body,
        grid=(1,),
        in_specs=[
            pl.BlockSpec((T_SAMPLE, POOL_WIDTH), lambda i: (blk, 0)),
            pl.BlockSpec((POOL_BUF, DEC_BATCH, POOL_WIDTH), lambda i: (0, 0, 0)),
            pl.BlockSpec((POOL_GROUPS, POOL_GROUP_CH, POOL_OUT_CH), lambda i: (0, 0, 0)),
            pl.BlockSpec((1, D_MODEL), lambda i: (0, 0)),
            pl.BlockSpec(memory_space=pl.ANY),
        ],
        out_specs=pl.BlockSpec((T_SAMPLE, D_MODEL), lambda i: (blk, 0)),
        out_shape=jax.ShapeDtypeStruct((T_ALL, D_MODEL), BF16),
        input_output_aliases={4: 0},
        compiler_params=_cparams(1),
        name="pool_sample",
    )(u, hist_tm, w_pool_bf, pool_scale, y_pool)


def _ssm_tables(a_re, a_im, log_dt, b_re, b_im, c_re, c_im, d_skip):
    dt = jnp.exp(log_dt)[:, None]
    lr, li = a_re, a_im
    ab_re = jnp.exp(lr * dt) * jnp.cos(li * dt)
    ab_im = jnp.exp(lr * dt) * jnp.sin(li * dt)
    den = lr * lr + li * li
    nr, ni = ab_re - 1.0, ab_im
    q_re = (nr * lr + ni * li) / den
    q_im = (ni * lr - nr * li) / den
    bb_re = q_re[..., None] * b_re - q_im[..., None] * b_im
    bb_im = q_re[..., None] * b_im + q_im[..., None] * b_re

    def lam_rows(ks):
        k = jnp.asarray(ks, F32)[:, None, None]
        m = jnp.exp(k * lr * dt)
        re = (m * jnp.cos(k * li * dt)).reshape(len(ks), N_OCT, OCT_STATES)
        im = (m * jnp.sin(k * li * dt)).reshape(len(ks), N_OCT, OCT_STATES)
        return jnp.concatenate([re, im], axis=-1).transpose(1, 0, 2)

    def compact(re, im):
        v = jnp.concatenate([re, im], axis=-1)
        return v.reshape(N_OCT, OCT, 2 * SSM_STATE)

    bbc = compact(jnp.swapaxes(bb_re, 1, 2), jnp.swapaxes(bb_im, 1, 2))
    ccc = compact(c_re, c_im)
    pw = lam_rows(list(range(2 * SUBLANES)))
    r = jnp.arange(SUBLANES)[None, :, None]
    parts = [jnp.where(r >= dd, lam_rows([CHUNK * dd]), 0.0) for dd in (1, 2, 4)]
    parts.append(lam_rows([CHUNK * kk for kk in range(1, SUBLANES + 1)]))
    tab = jnp.concatenate(parts, axis=1)
    dsk = d_skip.reshape(N_OCT, 1, OCT)
    return bbc, ccc, pw, tab, dsk


def _split_bf16(x):
    hi = x.astype(BF16)
    return hi, (x - hi.astype(F32)).astype(BF16)


def _dot_nt(a, b):
    return lax.dot_general(a, b, (((1,), (1,)), ((), ())), preferred_element_type=F32)


def _build_weights(bbc_ref, ccc_ref, pw_ref, f_ref, gt_ref, m_ref):
    row_gi = lax.broadcasted_iota(I32, (OCT, 1), 0) >> 4
    col = lax.broadcasted_iota(I32, (1, SW), 1)
    col_gi = (col >> 6) & 7
    src = ((col >> 9) << 6) | (col & 63)
    k128 = lax.broadcasted_iota(I32, (2 * SSM_STATE, 1), 0)
    spread = jnp.where(k128 == src, 1.0, 0.0).astype(BF16)
    diag = row_gi == col_gi

    def expand(c_ref):
        hi, lo = _split_bf16(c_ref[...])
        d = (jnp.dot(hi, spread, preferred_element_type=F32)
             + jnp.dot(lo, spread, preferred_element_type=F32))
        d = jnp.where(diag, d, 0.0)
        return d[:, :OCT_STATES], d[:, OCT_STATES:]

    br, bi = expand(bbc_ref)
    cr, ci = expand(ccc_ref)
    chi_r, clo_r = _split_bf16(cr)
    chi_i, clo_i = _split_bf16(ci)

    def lam(k):
        return pw_ref[k:k + 1, :OCT_STATES], pw_ref[k:k + 1, OCT_STATES:]

    def dot3(a, bhi, blo):
        ahi, alo = _split_bf16(a)
        return _dot_nt(ahi, bhi) + _dot_nt(alo, bhi) + _dot_nt(ahi, blo)

    lags = []
    for k in range(CHUNK):
        pr, pi_ = lam(k)
        fr, fi = _cmul(br, bi, pr, pi_)
        s = CHUNK - 1 - k
        f_ref[s * OCT:(s + 1) * OCT, :] = jnp.concatenate([fr, fi], axis=1).astype(BF16)
        lags.append((dot3(fr, chi_r, clo_r) - dot3(fi, chi_i, clo_i)).astype(BF16))
        pr, pi_ = lam(k + 1)
        gr, gi = _cmul(cr, ci, pr, pi_)
        gt_ref[k * OCT:(k + 1) * OCT, :] = jnp.concatenate([gr, -gi], axis=1).astype(BF16)
    zero = jnp.zeros((OCT, OCT), BF16)
    for s in range(CHUNK):
        for t in range(CHUNK):
            m_ref[s * OCT:(s + 1) * OCT, t * OCT:(t + 1) * OCT] = lags[t - s] if t >= s else zero


def _cmul(ar, ai, br, bi):
    return ar * br - ai * bi, ar * bi + ai * br


def _chunk_scan(sloc, tab_ref):
    R = sloc.shape[0]
    nb = R // SUBLANES
    sr, si = sloc[:, :OCT_STATES], sloc[:, OCT_STATES:]
    rowi = lax.broadcasted_iota(I32, (R, 1), 0)
    tr = jnp.where(rowi == 0, 0.0, pltpu.roll(sr, 1, axis=0))
    ti = jnp.where(rowi == 0, 0.0, pltpu.roll(si, 1, axis=0))
    for lvl, d in enumerate((1, 2, 4)):
        mr = tab_ref[lvl * SUBLANES:(lvl + 1) * SUBLANES, :OCT_STATES]
        mi = tab_ref[lvl * SUBLANES:(lvl + 1) * SUBLANES, OCT_STATES:]
        mr = jnp.concatenate([mr] * nb, axis=0)
        mi = jnp.concatenate([mi] * nb, axis=0)
        pr, pi_ = _cmul(mr, mi, pltpu.roll(tr, d, axis=0), pltpu.roll(ti, d, axis=0))
        tr, ti = tr + pr, ti + pi_
    pwr = tab_ref[3 * SUBLANES:4 * SUBLANES, :OCT_STATES]
    pwi = tab_ref[3 * SUBLANES:4 * SUBLANES, OCT_STATES:]
    cr = jnp.zeros((1, OCT_STATES), F32)
    ci = jnp.zeros((1, OCT_STATES), F32)
    out_r, out_i = [], []
    for k in range(nb):
        ar = tr[k * SUBLANES:(k + 1) * SUBLANES, :]
        ai = ti[k * SUBLANES:(k + 1) * SUBLANES, :]
        pr, pi_ = _cmul(pwr, pwi, jnp.broadcast_to(cr, ar.shape), jnp.broadcast_to(ci, ai.shape))
        hr, hi = ar + pr, ai + pi_
        out_r.append(hr)
        out_i.append(hi)
        cr, ci = hr[SUBLANES - 1:, :], hi[SUBLANES - 1:, :]
    hin = jnp.concatenate([jnp.concatenate(out_r, axis=0), jnp.concatenate(out_i, axis=0)], axis=1)
    lr, li = pwr[0:1, :], pwi[0:1, :]
    fr, fi = _cmul(lr, li, cr, ci)
    fin = jnp.concatenate([fr + sr[R - 1:, :], fi + si[R - 1:, :]], axis=1)
    return hin, fin


def _ssm_body(u_ref, h0r_ref, h0i_ref, bbc_ref, ccc_ref, pw_ref, tab_ref, d_ref,
              y_ref, hout_ref, hr_ref, hi_ref, f_ref, gt_ref, m_ref):
    b = pl.program_id(1)

    @pl.when(b == 0)
    def _():
        _build_weights(bbc_ref, ccc_ref, pw_ref, f_ref, gt_ref, m_ref)

    def outputs(xs, xb, hin_bf):
        y = (jnp.dot(xb, m_ref[...], preferred_element_type=F32) + _dot_nt(hin_bf, gt_ref[...]))
        return [_gelu_tanh(y[:, t * OCT:(t + 1) * OCT] + d_ref[...] * xs[t]) for t in range(CHUNK)]

    @pl.when(b < BATCH)
    def _():
        R = SEQ // CHUNK
        xs = [u_ref[pl.ds(s, R, stride=CHUNK), :] for s in range(CHUNK)]
        xb = jnp.concatenate(xs, axis=1).astype(BF16)
        sloc = jnp.dot(xb, f_ref[...], preferred_element_type=F32)
        hin, fin = _chunk_scan(sloc, tab_ref)
        for t, yt in enumerate(outputs(xs, xb, hin.astype(BF16))):
            y_ref[pl.ds(t, R, stride=CHUNK), :] = yt
        hout_ref[...] = fin

    @pl.when(b == BATCH)
    def _():
        B = DEC_BATCH
        xs = [u_ref[B * s:B * (s + 1), :] for s in range(CHUNK)]
        xb = jnp.concatenate(xs, axis=1).astype(BF16)
        sloc = jnp.dot(xb, f_ref[...], preferred_element_type=F32)
        h0r, h0i = h0r_ref[...], h0i_ref[...]
        hin = jnp.concatenate([h0r, h0i], axis=1).astype(BF16)
        for t, yt in enumerate(outputs(xs, xb, hin)):
            y_ref[B * t:B * (t + 1), :] = yt
        lr = tab_ref[3 * SUBLANES:3 * SUBLANES + 1, :OCT_STATES]
        li = tab_ref[3 * SUBLANES:3 * SUBLANES + 1, OCT_STATES:]
        nr, ni = _cmul(lr, li, h0r, h0i)
        hr_ref[...] = nr + sloc[:, :OCT_STATES]
        hi_ref[...] = ni + sloc[:, OCT_STATES:]


def _ssm(u, h0r, h0i, tables):
    col0 = POOL_WIDTH // OCT
    im3 = lambda o, b: (o, 0, 0)
    st_spec = pl.BlockSpec((DEC_BATCH, OCT_STATES), lambda o, b: (0, o))
    return pl.pallas_call(
        _ssm_body,
        grid=(N_OCT, BATCH + 1),
        in_specs=[
            pl.BlockSpec((SEQ, OCT), lambda o, b: (b, col0 + o)), st_spec, st_spec,
            pl.BlockSpec((None, OCT, 2 * SSM_STATE), im3),
            pl.BlockSpec((None, OCT, 2 * SSM_STATE), im3),
            pl.BlockSpec((None, 2 * SUBLANES, SW), im3),
            pl.BlockSpec((None, 4 * SUBLANES, SW), im3),
            pl.BlockSpec((None, 1, OCT), im3),
        ],
        out_specs=(
            pl.BlockSpec((SEQ, OCT), lambda o, b: (b, o)),
            pl.BlockSpec((None, 1, SW), lambda o, b: (jnp.minimum(b, BATCH - 1) * N_OCT + o, 0, 0)),
            st_spec, st_spec,
        ),
        out_shape=(
            jax.ShapeDtypeStruct((T_PAD, SSM_WIDTH), F32),
            jax.ShapeDtypeStruct((BATCH * N_OCT, 1, SW), F32),
            jax.ShapeDtypeStruct((DEC_BATCH, SSM_GROUPS * SSM_STATE), F32),
            jax.ShapeDtypeStruct((DEC_BATCH, SSM_GROUPS * SSM_STATE), F32),
        ),
        scratch_shapes=[pltpu.VMEM((CW, SW), BF16), pltpu.VMEM((CW, SW), BF16), pltpu.VMEM((CW, CW), BF16)],
        compiler_params=_cparams(2),
        name="ssm",
    )(u, h0r, h0i, *tables)


PM_TM = 256
PM_PROMPT_BLOCKS = T_PROMPT // PM_TM
PM_STEPS = T_ALL // PM_TM


def _route(logits, valid, cnt_ref):
    lane = lax.broadcasted_iota(I32, (PM_TM, ROUTE_LANES), 1)
    neg = jnp.float32(-jnp.inf)
    big = jnp.int32(1 << 20)
    is_g = lane < MOE_GROUPS
    gmax = jnp.max(jnp.where(is_g, logits, neg), axis=1, keepdims=True)
    g_idx = jnp.min(jnp.where(is_g & (logits == gmax), lane, big), axis=1, keepdims=True)
    g_den = jnp.sum(jnp.where(is_g, jnp.exp(logits - gmax), 0.0), axis=1, keepdims=True)
    g_val = 1.0 / g_den
    e_lane = lane - EXP_LANE0
    sel = (e_lane >= 0) & (e_lane < MOE_EXPERTS) & ((e_lane >> 3) == g_idx)
    m1 = jnp.max(jnp.where(sel, logits, neg), axis=1, keepdims=True)
    i1 = jnp.min(jnp.where(sel & (logits == m1), lane, big), axis=1, keepdims=True)
    sel2 = sel & (lane != i1)
    m2 = jnp.max(jnp.where(sel2, logits, neg), axis=1, keepdims=True)
    i2 = jnp.min(jnp.where(sel2 & (logits == m2), lane, big), axis=1, keepdims=True)
    e2 = jnp.exp(m2 - m1)
    w1 = g_val / (1.0 + e2)
    w2 = g_val * e2 / (1.0 + e2)
    oh1 = lane == i1
    oh2 = lane == i2
    oh = jnp.where(oh1 | oh2, valid, 0.0)
    rr = lax.broadcasted_iota(I32, (PM_TM, PM_TM), 0)
    cc = lax.broadcasted_iota(I32, (PM_TM, PM_TM), 1)
    tri = jnp.where(cc < rr, 1.0, 0.0).astype(BF16)
    base = cnt_ref[...] + jnp.dot(tri, oh.astype(BF16), preferred_element_type=F32)
    rank1 = jnp.sum(jnp.where(oh1, base, 0.0), axis=1, keepdims=True)
    rank2 = jnp.sum(jnp.where(oh2, base, 0.0), axis=1, keepdims=True)
    cnt_ref[...] = cnt_ref[...] + jnp.sum(oh, axis=0, keepdims=True)
    rt = jnp.where(lane == 0, w1, 0.0)
    rt = jnp.where(lane == 1, w2, rt)
    rt = jnp.where(lane == 2, rank1, rt)
    rt = jnp.where(lane == 3, rank2, rt)
    rt = jnp.where(lane == 4, (i1 - EXP_LANE0).astype(F32), rt)
    rt = jnp.where(lane == 5, (i2 - EXP_LANE0).astype(F32), rt)
    return rt


def _postmix_body(ya_ref, gp_ref, gs_ref, yp_ref, xp_ref, xs_ref, wa_ref, wb_ref, wo_ref,
                  gf_ref, wr_ref, br_ref, h_ref, tn_ref, rt_ref, cnt_out_ref,
                  cnt_ref, lg_ref):
    i = pl.program_id(0)
    @pl.when(i == 0)
    def _():
        cnt_ref[...] = jnp.zeros_like(cnt_ref)
        lg_ref[...] = jnp.zeros_like(lg_ref)
    prev_logits = lg_ref[...]

    ya = ya_ref[...].astype(BF16)
    a = jnp.dot(ya, wa_ref[...], preferred_element_type=F32)
    bg = jnp.dot(ya, wb_ref[...], preferred_element_type=F32)
    y_ssm = a * _sigmoid(bg)
    merged = (gp_ref[...].astype(F32) * yp_ref[...].astype(F32)
              + gs_ref[...].astype(F32) * y_ssm)
    x = jnp.where(jnp.minimum(i, PM_STEPS - 1) < PM_PROMPT_BLOCKS, xp_ref[...], xs_ref[...])
    h = x + jnp.dot(merged.astype(BF16), wo_ref[...], preferred_element_type=F32)
    h_ref[...] = h
    inv = lax.rsqrt(jnp.mean(h * h, axis=-1, keepdims=True) + EPS)
    tn = (h * inv) * gf_ref[...]
    tn_ref[...] = _pack_pairs(tn)
    t_hi = tn.astype(BF16)
    t_lo = (tn - t_hi.astype(F32)).astype(BF16)
    hh = jnp.dot(t_hi, wr_ref[...], preferred_element_type=F32)
    lh = jnp.dot(t_lo, wr_ref[:, :ROUTE_LANES], preferred_element_type=F32)
    lg_ref[...] = (hh[:, :ROUTE_LANES] + lh + hh[:, ROUTE_LANES:]) + br_ref[...]
    rt_ref[...] = _route(prev_logits, jnp.where(i > 0, 1.0, 0.0), cnt_ref)
    cnt_out_ref[...] = cnt_ref[...]


def _postmix(y_act, gates, y_pool, xp, xs, wa, wb, wo, g_ffn, wr_cat, b_r):
    npb = PM_PROMPT_BLOCKS
    const2 = lambda i: (0, 0)
    tile = lambda i: jnp.minimum(i, PM_STEPS - 1)
    return pl.pallas_call(
        _postmix_body,
        grid=(PM_STEPS + 1,),
        in_specs=[
            pl.BlockSpec((PM_TM, SSM_WIDTH), lambda i: (tile(i), 0)),
            pl.BlockSpec((PM_TM, D_MODEL), lambda i: (tile(i), 0)),
            pl.BlockSpec((PM_TM, D_MODEL), lambda i: (tile(i), 1)),
            pl.BlockSpec((PM_TM, D_MODEL), lambda i: (tile(i), 0)),
            pl.BlockSpec((PM_TM, D_MODEL), lambda i: (jnp.minimum(i, npb - 1), 0)),
            pl.BlockSpec((PM_TM, D_MODEL), lambda i: (jnp.maximum(tile(i) - npb, 0), 0)),
            pl.BlockSpec((SSM_WIDTH, D_MODEL), const2, pipeline_mode=pl.Buffered(1)),
            pl.BlockSpec((SSM_WIDTH, D_MODEL), const2, pipeline_mode=pl.Buffered(1)),
            pl.BlockSpec((D_MODEL, D_MODEL), const2, pipeline_mode=pl.Buffered(1)),
            pl.BlockSpec((1, D_MODEL), const2),
            pl.BlockSpec((D_MODEL, 2 * ROUTE_LANES), const2),
            pl.BlockSpec((1, ROUTE_LANES), const2),
        ],
        out_specs=(
            pl.BlockSpec((PM_TM, D_MODEL), lambda i: (tile(i), 0)),
            pl.BlockSpec((PM_TM, D_PACK), lambda i: (tile(i), 0)),
            pl.BlockSpec((PM_TM, ROUTE_LANES), lambda i: (jnp.maximum(i - 1, 0), 0)),
            pl.BlockSpec((1, ROUTE_LANES), const2),
        ),
        out_shape=(
            jax.ShapeDtypeStruct((T_ALL, D_MODEL), F32),
            jax.ShapeDtypeStruct((T_ALL, D_PACK), U32),
            jax.ShapeDtypeStruct((T_ALL, ROUTE_LANES), F32),
            jax.ShapeDtypeStruct((1, ROUTE_LANES), F32),
        ),
        scratch_shapes=[pltpu.VMEM((1, ROUTE_LANES), F32), pltpu.VMEM((PM_TM, ROUTE_LANES), F32)],
        compiler_params=_cparams(1),
        name="postmix",
    )(y_act, gates, gates, y_pool, xp, xs, wa, wb, wo, g_ffn, wr_cat, b_r)


SC_CH = 32


def _sc_workers():
    info = plsc.get_sparse_core_info()
    return info.num_cores, info.num_cores * info.num_subcores


def _sc_dispatch(tn, slots):
    n_cores, n_workers = _sc_workers()
    per_w = (T_ALL // SC_CH) // n_workers
    assert per_w * n_workers * SC_CH == T_ALL
    slots = slots.reshape(2, n_workers, per_w, SC_CH)

    @functools.partial(
        pl.kernel,
        mesh=plsc.VectorSubcoreMesh(core_axis_name="c", subcore_axis_name="s"),
        out_type=jax.ShapeDtypeStruct((N_SLOTS, D_PACK), U32),
        scratch_types=[pltpu.VMEM((2, per_w, SC_CH), I32), pltpu.VMEM((SC_CH, D_PACK), U32)],
    )
    def k(tn_hbm, slots_hbm, xs_hbm, idx_v, rows_v):
        wid = lax.axis_index("s") * n_cores + lax.axis_index("c")
        c0 = wid * per_w
        pltpu.sync_copy(slots_hbm.at[0, wid], idx_v.at[0])
        pltpu.sync_copy(slots_hbm.at[1, wid], idx_v.at[1])

        @pl.loop(0, per_w)
        def _(c):
            row0 = pl.multiple_of((c0 + c) * SC_CH, SC_CH)
            pltpu.sync_copy(tn_hbm.at[pl.ds(row0, SC_CH)], rows_v)
            pltpu.sync_copy(rows_v, xs_hbm.at[idx_v.at[0, c]])
            pltpu.sync_copy(rows_v, xs_hbm.at[idx_v.at[1, c]])

    return k(tn, slots)


def _sc_collect(ys, slots):
    n_cores, n_workers = _sc_workers()
    per_w = (N_ASSIGN // SC_CH) // n_workers
    assert per_w * n_workers * SC_CH == N_ASSIGN
    slots = slots.reshape(n_workers, per_w, SC_CH)

    @functools.partial(
        pl.kernel,
        mesh=plsc.VectorSubcoreMesh(core_axis_name="c", subcore_axis_name="s"),
        out_type=jax.ShapeDtypeStruct((N_ASSIGN, D_PACK), U32),
        scratch_types=[pltpu.VMEM((per_w, SC_CH), I32), pltpu.VMEM((SC_CH, D_PACK), U32)],
    )
    def k(ys_hbm, slots_hbm, out_hbm, idx_v, rows_v):
        wid = lax.axis_index("s") * n_cores + lax.axis_index("c")
        c0 = wid * per_w
        pltpu.sync_copy(slots_hbm.at[wid], idx_v)

        @pl.loop(0, per_w)
        def _(c):
            row0 = pl.multiple_of((c0 + c) * SC_CH, SC_CH)
            pltpu.sync_copy(ys_hbm.at[idx_v.at[c]], rows_v)
            pltpu.sync_copy(rows_v, out_hbm.at[pl.ds(row0, SC_CH)])

    return k(ys, slots)


W_PARTS = 2
W_SLOTS = 3


def _expert_body(t0_ref, nt_ref, xs_hbm, wg_hbm, wu_hbm, wd_hbm, ys_hbm,
                 wg_ref, wu_ref, wd_ref, xb_ref, yb_ref, wgb_ref, wub_ref, wdb_ref, wsem, xsem, ysem):
    e = pl.program_id(0)
    n = nt_ref[e]
    g0 = t0_ref[e]
    ws = lax.rem(e, W_SLOTS)

    def w_copies(ex, slot):
        out = []
        for hbm, buf in ((wg_hbm, wg_ref), (wu_hbm, wu_ref), (wd_hbm, wd_ref)):
            rb = buf.shape[1] // W_PARTS
            for p in range(W_PARTS):
                out.append((pltpu.make_async_copy(hbm.at[ex, pl.ds(p * rb, rb)],
                                                  buf.at[slot, pl.ds(p * rb, rb)], wsem.at[slot]), p))
        return out

    @pl.when(e == 0)
    def _():
        for ahead in range(W_SLOTS - 1):
            for cp, p in w_copies(ahead, ahead):
                cp.start(priority=p)

    nxt = e + (W_SLOTS - 1)
    @pl.when(nxt < MOE_EXPERTS)
    def _():
        for cp, p in w_copies(nxt, lax.rem(nxt, W_SLOTS)):
            cp.start(priority=p)

    for cp, _ in w_copies(e, ws):
        cp.wait()

    def rows(j):
        return pl.ds(pl.multiple_of((g0 + j) * TME, TME), TME)

    def x_copy(j, s):
        return pltpu.make_async_copy(xs_hbm.at[rows(j)], xb_ref.at[s], xsem.at[s])

    def y_copy(j, s):
        return pltpu.make_async_copy(yb_ref.at[s], ys_hbm.at[rows(j)], ysem.at[s])

    @pl.when(n > 0)
    def _():
        x_copy(0, 0).start()
        wgb_ref[...] = wg_ref[ws].astype(BF16)
        wub_ref[...] = wu_ref[ws].astype(BF16)
        wdb_ref[...] = wd_ref[ws].astype(BF16)

        def tile(j, c):
            s = j & 1
            x_copy(j, s).wait()
            @pl.when(j + 1 < n)
            def _():
                x_copy(j + 1, 1 - s).start()
            @pl.when(j >= 2)
            def _():
                y_copy(j - 2, s).wait()
            x = _unpack_pairs(xb_ref[s], BF16)
            hg = jnp.dot(x, wgb_ref[...], preferred_element_type=F32)
            hu = jnp.dot(x, wub_ref[...], preferred_element_type=F32)
            act = (hg * _sigmoid(hg)) * hu
            yb_ref[s] = _pack_pairs(jnp.dot(act.astype(BF16), wdb_ref[...], preferred_element_type=F32))
            y_copy(j, s).start()
            return c
        lax.fori_loop(0, n, tile, 0)

        @pl.when(n >= 2)
        def _():
            y_copy(n - 2, n & 1).wait()
        y_copy(n - 1, (n - 1) & 1).wait()


def _experts(tile0, tiles, xs, w_eg, w_eu, w_ed):
    any_spec = pl.BlockSpec(memory_space=pl.ANY)
    grid_spec = pltpu.PrefetchScalarGridSpec(
        num_scalar_prefetch=2,
        grid=(MOE_EXPERTS,),
        in_specs=[any_spec] * 4,
        out_specs=any_spec,
        scratch_shapes=[
            pltpu.VMEM((W_SLOTS, D_MODEL, MOE_FF), F32),
            pltpu.VMEM((W_SLOTS, D_MODEL, MOE_FF), F32),
            pltpu.VMEM((W_SLOTS, MOE_FF, D_MODEL), F32),
            pltpu.VMEM((2, TME, D_PACK), U32),
            pltpu.VMEM((2, TME, D_PACK), U32),
            pltpu.VMEM((D_MODEL, MOE_FF), BF16),
            pltpu.VMEM((D_MODEL, MOE_FF), BF16),
            pltpu.VMEM((MOE_FF, D_MODEL), BF16),
            pltpu.SemaphoreType.DMA((W_SLOTS,)),
            pltpu.SemaphoreType.DMA((2,)),
            pltpu.SemaphoreType.DMA((2,)),
        ],
    )
    return pl.pallas_call(
        _expert_body,
        grid_spec=grid_spec,
        out_shape=jax.ShapeDtypeStruct((N_SLOTS, D_PACK), U32),
        compiler_params=_cparams(1),
        name="experts",
    )(tile0, tiles, xs, w_eg, w_eu, w_ed)


FN_TM = 256
FN_PROMPT_BLOCKS = T_PROMPT // FN_TM


def _final_body(h_ref, y0_ref, y1_ref, rt_ref, g_ref, op_ref, os_ref):
    i = pl.program_id(0)
    rt = rt_ref[...]
    z = (h_ref[...] + rt[:, 0:1] * _unpack_pairs(y0_ref[...], F32)
         + rt[:, 1:2] * _unpack_pairs(y1_ref[...], F32))
    inv = lax.rsqrt(jnp.mean(z * z, axis=-1, keepdims=True) + EPS)
    out = (z * inv) * g_ref[...]
    @pl.when(i < FN_PROMPT_BLOCKS)
    def _():
        op_ref[...] = out
    @pl.when(i >= FN_PROMPT_BLOCKS)
    def _():
        os_ref[...] = out


def _final(h, y, route, g_final):
    n = T_ALL // FN_TM
    npb = FN_PROMPT_BLOCKS
    yoff = T_ALL // FN_TM
    return pl.pallas_call(
        _final_body,
        grid=(n,),
        in_specs=[
            pl.BlockSpec((FN_TM, D_MODEL), lambda i: (i, 0)),
            pl.BlockSpec((FN_TM, D_PACK), lambda i: (i, 0)),
            pl.BlockSpec((FN_TM, D_PACK), lambda i: (yoff + i, 0)),
            pl.BlockSpec((FN_TM, ROUTE_LANES), lambda i: (i, 0)),
            pl.BlockSpec((1, D_MODEL), lambda i: (0, 0)),
        ],
        out_specs=(
            pl.BlockSpec((FN_TM, D_MODEL), lambda i: (jnp.minimum(i, npb - 1), 0)),
            pl.BlockSpec((FN_TM, D_MODEL), lambda i: (jnp.maximum(i - npb, 0), 0)),
        ),
        out_shape=(
            jax.ShapeDtypeStruct((T_PROMPT, D_MODEL), F32),
            jax.ShapeDtypeStruct((T_SAMPLE, D_MODEL), F32),
        ),
        compiler_params=_cparams(1),
        name="final",
    )(h, y, y, route, g_final)


def _dispatch_plan(route, cnt):
    counts = cnt[0, EXP_LANE0:EXP_LANE0 + MOE_EXPERTS].astype(I32)
    tiles = (counts + (TME - 1)) // TME
    cumt = jnp.cumsum(tiles)
    pad_off = (cumt - tiles) * TME
    rank = route[:, 2:4].astype(I32)
    eid = route[:, 4:6].astype(I32)
    onehot = eid[..., None] == jnp.arange(MOE_EXPERTS, dtype=I32)
    slots = (rank + jnp.sum(jnp.where(onehot, pad_off, 0), axis=-1)).T
    return slots, cumt - tiles, tiles


def kernel(x_prompt, x_sample, state_pool, state_ssm_re, state_ssm_im, g_mix, w_in, w_pool,
           pool_scale, ssm_a_re, ssm_a_im, ssm_log_dt, ssm_b_re, ssm_b_im, ssm_c_re, ssm_c_im,
           ssm_d, w_glu_a, w_glu_b, w_out, g_ffn, w_router_group, b_router_group,
           w_router_expert, b_router_expert, w_exp_gate, w_exp_up, w_exp_down, g_final):
    l = 0
    xp = x_prompt.reshape(T_PROMPT, D_MODEL)
    xs = x_sample.transpose(1, 0, 2).reshape(T_SAMPLE, D_MODEL)
    w_in_bf = w_in[l].astype(BF16)
    w_pool_bf = w_pool[l].astype(BF16)
    wa_bf = w_glu_a[l].astype(BF16)
    wb_bf = w_glu_b[l].astype(BF16)
    wo_bf = w_out[l].astype(BF16)
    g_mix2 = g_mix[l].reshape(1, D_MODEL)
    scale2 = pool_scale[l].reshape(1, D_MODEL)

    u, gates = _inproj(xp, g_mix2, w_in_bf, 0)
    u, gates = _inproj(xs, g_mix2, w_in_bf, T_PROMPT // IN_TM, dst=(u, gates))

    y_pool, pool_tail = _pool_prompt(u, w_pool_bf, scale2)
    hist_tm = state_pool[l].transpose(1, 0, 2)
    y_pool = _pool_sample(u, hist_tm, w_pool_bf, scale2, y_pool)

    tables = _ssm_tables(ssm_a_re[l], ssm_a_im[l], ssm_log_dt[l], ssm_b_re[l], ssm_b_im[l],
                         ssm_c_re[l], ssm_c_im[l], ssm_d[l])
    h0r = state_ssm_re[l].reshape(DEC_BATCH, SSM_GROUPS * SSM_STATE)
    h0i = state_ssm_im[l].reshape(DEC_BATCH, SSM_GROUPS * SSM_STATE)
    y_act, h_prompt, hs_re, hs_im = _ssm(u, h0r, h0i, tables)

    w_r = jnp.zeros((D_MODEL, ROUTE_LANES), F32)
    w_r = w_r.at[:, :MOE_GROUPS].set(w_router_group[l])
    w_r = w_r.at[:, EXP_LANE0:EXP_LANE0 + MOE_EXPERTS].set(w_router_expert[l])
    wr_hi = w_r.astype(BF16)
    wr_cat = jnp.concatenate([wr_hi, (w_r - wr_hi.astype(F32)).astype(BF16)], axis=1)
    b_r = jnp.zeros((1, ROUTE_LANES), F32)
    b_r = b_r.at[0, :MOE_GROUPS].set(b_router_group[l])
    b_r = b_r.at[0, EXP_LANE0:EXP_LANE0 + MOE_EXPERTS].set(b_router_expert[l])

    h, tn, route, cnt = _postmix(y_act, gates, y_pool, xp, xs, wa_bf, wb_bf, wo_bf,
                                 g_ffn[l].reshape(1, D_MODEL), wr_cat, b_r)
    slots, tile0, tiles = _dispatch_plan(route, cnt)
    xs_sorted = _sc_dispatch(tn, slots)
    ys_sorted = _experts(tile0, tiles, xs_sorted, w_exp_gate[l], w_exp_up[l], w_exp_down[l])
    y = _sc_collect(ys_sorted, slots)
    yp, ys = _final(h, y, route, g_final.reshape(1, D_MODEL))

    y_prompt = yp.reshape(BATCH, SEQ, D_MODEL)
    y_sample = ys.reshape(DEC_SEQ, DEC_BATCH, D_MODEL).transpose(1, 0, 2)
    new_pool_prompt = pool_tail[:, HIST - POOL_BUF:, :][None]
    us = u[T_PROMPT:T_ALL, :POOL_WIDTH].reshape(DEC_SEQ, DEC_BATCH, POOL_WIDTH).transpose(1, 0, 2)
    new_pool_sample = jnp.concatenate([state_pool[l][:, DEC_SEQ:, :], us], axis=1)[None]
    hp = h_prompt.reshape(BATCH, N_OCT, 2, OCT_GROUPS, SSM_STATE).transpose(2, 0, 1, 3, 4)
    hp = hp.reshape(2, BATCH, SSM_GROUPS, SSM_STATE)
    shp = (1, DEC_BATCH, SSM_GROUPS, SSM_STATE)
    return (y_prompt, y_sample, new_pool_prompt, hp[0][None], hp[1][None], new_pool_sample,
            hs_re.reshape(shp), hs_im.reshape(shp))
```

```python
import functools
import math

import jax
import jax.numpy as jnp
from jax import lax
from jax.experimental import pallas as pl
from jax.experimental.pallas import tpu as pltpu
from jax.experimental.pallas import tpu_sc as plsc

F32 = jnp.float32
BF16 = jnp.bfloat16
I32 = jnp.int32
U32 = jnp.uint32

D_MODEL = 2048
BATCH = 4
SEQ = 2048
DEC_BATCH = 128
DEC_SEQ = 8
PAST_LEN = 16384
POOL_WIDTH = D_MODEL // 2
POOL_WINDOWS = (2, 4, 8, 16)
POOL_GROUPS = len(POOL_WINDOWS)
POOL_GROUP_CH = POOL_WIDTH // POOL_GROUPS
POOL_OUT_CH = D_MODEL // POOL_GROUPS
POOL_BUF = max(POOL_WINDOWS) - 1
SSM_WIDTH = D_MODEL // 2
SSM_GROUP_CH = 16
SSM_GROUPS = SSM_WIDTH // SSM_GROUP_CH
SSM_STATE = 64
IN_WIDTH = POOL_WIDTH + SSM_WIDTH + 2 * D_MODEL
D_PACK = D_MODEL // 2
MOE_GROUPS = 4
MOE_EPG = 8
MOE_EXPERTS = MOE_GROUPS * MOE_EPG
MOE_FF = D_MODEL // 4
EPS = 1e-6

T_PROMPT = BATCH * SEQ
T_SAMPLE = DEC_BATCH * DEC_SEQ
T_ALL = T_PROMPT + T_SAMPLE
T_PAD = (BATCH + 1) * SEQ

LANES = 128
SUBLANES = 8
VMEM_LIMIT = 56 * 1024 * 1024

CHUNK = 8
OCT = LANES
N_OCT = SSM_WIDTH // OCT
OCT_GROUPS = OCT // SSM_GROUP_CH
OCT_STATES = OCT_GROUPS * SSM_STATE
CW = CHUNK * OCT
SW = 2 * OCT_STATES

ROUTE_LANES = LANES
EXP_LANE0 = MOE_GROUPS
N_ASSIGN = 2 * T_ALL
TME = 256
N_ITEMS_MAX = N_ASSIGN // TME + MOE_EXPERTS
N_SLOTS = N_ITEMS_MAX * TME


def _cparams(n_axes):
    return pltpu.CompilerParams(dimension_semantics=("arbitrary",) * n_axes,
                                vmem_limit_bytes=VMEM_LIMIT)


def _sigmoid(x):
    return 1.0 / (1.0 + jnp.exp(-x))


def _pack_pairs(x):
    c = x.shape[1] // 2
    hi = lax.bitcast_convert_type(x[:, :c].astype(BF16).astype(F32), U32)
    lo = lax.bitcast_convert_type(x[:, c:].astype(BF16).astype(F32), U32)
    return hi | (lo >> 16)


def _unpack_pairs(u, dtype):
    hi = lax.bitcast_convert_type(u & jnp.uint32(0xFFFF0000), F32)
    lo = lax.bitcast_convert_type(u << 16, F32)
    return jnp.concatenate([hi, lo], axis=1).astype(dtype)


def _gelu_tanh(x):
    c = math.sqrt(2.0 / math.pi)
    return 0.5 * x * (1.0 + jnp.tanh(c * (x + 0.044715 * (x * x * x))))


IN_TM = 1024
IN_TN = 1024
U_WIDTH = POOL_WIDTH + SSM_WIDTH
GATE_WIDTH = 2 * D_MODEL
IN_U_STEPS = U_WIDTH // IN_TN


def _inproj_body(x_ref, g_ref, w_ref, *rest):
    u_ref, gate_ref, xn_ref = rest[-3:]
    j = pl.program_id(1)
    @pl.when(j == 0)
    def _():
        x = x_ref[...]
        inv = lax.rsqrt(jnp.mean(x * x, axis=-1, keepdims=True) + EPS)
        xn_ref[...] = ((x * inv) * g_ref[...]).astype(BF16)
    acc = jnp.dot(xn_ref[...], w_ref[...], preferred_element_type=F32)
    @pl.when(j < IN_U_STEPS)
    def _():
        u_ref[...] = acc
    @pl.when(j >= IN_U_STEPS)
    def _():
        gate_ref[...] = _sigmoid(acc).astype(BF16)


def _inproj(x, g, w_bf, row_block0, dst=None):
    n_i = x.shape[0] // IN_TM
    in_specs = [
        pl.BlockSpec((IN_TM, D_MODEL), lambda i, j: (i, 0)),
        pl.BlockSpec((1, D_MODEL), lambda i, j: (0, 0)),
        pl.BlockSpec((D_MODEL, IN_TN), lambda i, j: (0, j)),
    ]
    args = [x, g, w_bf]
    aliases = {}
    if dst is not None:
        in_specs += [pl.BlockSpec(memory_space=pl.ANY)] * 2
        args += list(dst)
        aliases = {3: 0, 4: 1}
    return pl.pallas_call(
        _inproj_body,
        grid=(n_i, IN_WIDTH // IN_TN),
        in_specs=in_specs,
        out_specs=(
            pl.BlockSpec((IN_TM, IN_TN), lambda i, j: (i + row_block0, jnp.minimum(j, IN_U_STEPS - 1))),
            pl.BlockSpec((IN_TM, IN_TN), lambda i, j: (i + row_block0, jnp.maximum(j - IN_U_STEPS, 0))),
        ),
        out_shape=(
            jax.ShapeDtypeStruct((T_PAD, U_WIDTH), F32),
            jax.ShapeDtypeStruct((T_ALL, GATE_WIDTH), BF16),
        ),
        scratch_shapes=[pltpu.VMEM((IN_TM, D_MODEL), BF16)],
        input_output_aliases=aliases,
        compiler_params=_cparams(2),
        name="inproj",
    )(*args)


PP_TM = 512
HIST = 16


def _pool_project(pooled_g, g, w_ref, sc_ref, o_ref):
    y = jnp.dot(pooled_g.astype(BF16), w_ref[g], preferred_element_type=F32)
    lo, hi = g * POOL_OUT_CH, (g + 1) * POOL_OUT_CH
    o_ref[:, lo:hi] = (y * sc_ref[:, lo:hi]).astype(o_ref.dtype)


def _pool_prompt_body(u_ref, w_ref, sc_ref, o_ref, tail_ref, hist_ref):
    i = pl.program_id(1)
    @pl.when(i == 0)
    def _():
        hist_ref[...] = jnp.zeros_like(hist_ref)
    u = u_ref[...]
    ext = jnp.concatenate([hist_ref[...], u], axis=0)
    hist_ref[...] = u[PP_TM - HIST:, :]
    tail_ref[...] = u[PP_TM - HIST:, :]
    pos = i * PP_TM + lax.broadcasted_iota(I32, (PP_TM, 1), 0)
    for g, w in enumerate(POOL_WINDOWS):
        lo, hi = g * POOL_GROUP_CH, (g + 1) * POOL_GROUP_CH
        s = ext[:, lo:hi]
        d = 1
        while d < w:
            s = s + pltpu.roll(s, d, axis=0)
            d *= 2
        cnt = jnp.minimum(w, pos + 1).astype(F32)
        pooled = s[HIST:, :] / cnt - u[:, lo:hi]
        _pool_project(pooled, g, w_ref, sc_ref, o_ref)


def _pool_prompt(u, w_pool_bf, pool_scale):
    n_i = SEQ // PP_TM
    return pl.pallas_call(
        _pool_prompt_body,
        grid=(BATCH, n_i),
        in_specs=[
            pl.BlockSpec((PP_TM, POOL_WIDTH), lambda b, i: (b * n_i + i, 0)),
            pl.BlockSpec((POOL_GROUPS, POOL_GROUP_CH, POOL_OUT_CH), lambda b, i: (0, 0, 0)),
            pl.BlockSpec((1, D_MODEL), lambda b, i: (0, 0)),
        ],
        out_specs=(
            pl.BlockSpec((PP_TM, D_MODEL), lambda b, i: (b * n_i + i, 0)),
            pl.BlockSpec((None, HIST, POOL_WIDTH), lambda b, i: (b, 0, 0)),
        ),
        out_shape=(
            jax.ShapeDtypeStruct((T_ALL, D_MODEL), BF16),
            jax.ShapeDtypeStruct((BATCH, HIST, POOL_WIDTH), F32),
        ),
        scratch_shapes=[pltpu.VMEM((HIST, POOL_WIDTH), F32)],
        compiler_params=_cparams(2),
        name="pool_prompt",
    )(u, w_pool_bf, pool_scale)


def _pool_sample_body(u_ref, hist_ref, w_ref, sc_ref, _dst, o_ref):
    rows = [hist_ref[k] for k in range(POOL_BUF)]
    rows += [u_ref[DEC_BATCH * t:DEC_BATCH * (t + 1), :] for t in range(DEC_SEQ)]
    n = len(rows)
    for g, w in enumerate(POOL_WINDOWS):
        lo, hi = g * POOL_GROUP_CH, (g + 1) * POOL_GROUP_CH
        f = [r[:, lo:hi] for r in rows]
        cur = f
        d = 1
        while d < w:
            cur = [cur[k] + cur[k - d] if k - d >= 0 else cur[k] for k in range(n)]
            d *= 2
        pooled = jnp.concatenate(
            [cur[POOL_BUF + t] / float(w) - f[POOL_BUF + t] for t in range(DEC_SEQ)], axis=0)
        _pool_project(pooled, g, w_ref, sc_ref, o_ref)


def _pool_sample(u, hist_tm, w_pool_bf, pool_scale, y_pool):
    blk = T_PROMPT // T_SAMPLE
    return pl.pallas_call(
        _pool_sample_body,
        grid=(1,),
        in_specs=[
            pl.BlockSpec((T_SAMPLE, POOL_WIDTH), lambda i: (blk, 0)),
            pl.BlockSpec((POOL_BUF, DEC_BATCH, POOL_WIDTH), lambda i: (0, 0, 0)),
            pl.BlockSpec((POOL_GROUPS, POOL_GROUP_CH, POOL_OUT_CH), lambda i: (0, 0, 0)),
            pl.BlockSpec((1, D_MODEL), lambda i: (0, 0)),
            pl.BlockSpec(memory_space=pl.ANY),
        ],
        out_specs=pl.BlockSpec((T_SAMPLE, D_MODEL), lambda i: (blk, 0)),
        out_shape=jax.ShapeDtypeStruct((T_ALL, D_MODEL), BF16),
        input_output_aliases={4: 0},
        compiler_params=_cparams(1),
        name="pool_sample",
    )(u, hist_tm, w_pool_bf, pool_scale, y_pool)


def _ssm_tables(a_re, a_im, log_dt, b_re, b_im, c_re, c_im, d_skip):
    dt = jnp.exp(log_dt)[:, None]
    lr, li = a_re, a_im
    ab_re = jnp.exp(lr * dt) * jnp.cos(li * dt)
    ab_im = jnp.exp(lr * dt) * jnp.sin(li * dt)
    den = lr * lr + li * li
    nr, ni = ab_re - 1.0, ab_im
    q_re = (nr * lr + ni * li) / den
    q_im = (ni * lr - nr * li) / den
    bb_re = q_re[..., None] * b_re - q_im[..., None] * b_im
    bb_im = q_re[..., None] * b_im + q_im[..., None] * b_re

    def lam_rows(ks):
        k = jnp.asarray(ks, F32)[:, None, None]
        m = jnp.exp(k * lr * dt)
        re = (m * jnp.cos(k * li * dt)).reshape(len(ks), N_OCT, OCT_STATES)
        im = (m * jnp.sin(k * li * dt)).reshape(len(ks), N_OCT, OCT_STATES)
        return jnp.concatenate([re, im], axis=-1).transpose(1, 0, 2)

    def compact(re, im):
        v = jnp.concatenate([re, im], axis=-1)
        return v.reshape(N_OCT, OCT, 2 * SSM_STATE)

    bbc = compact(jnp.swapaxes(bb_re, 1, 2), jnp.swapaxes(bb_im, 1, 2))
    ccc = compact(c_re, c_im)
    pw = lam_rows(list(range(2 * SUBLANES)))
    r = jnp.arange(SUBLANES)[None, :, None]
    parts = [jnp.where(r >= dd, lam_rows([CHUNK * dd]), 0.0) for dd in (1, 2, 4)]
    parts.append(lam_rows([CHUNK * kk for kk in range(1, SUBLANES + 1)]))
    tab = jnp.concatenate(parts, axis=1)
    dsk = d_skip.reshape(N_OCT, 1, OCT)
    return bbc, ccc, pw, tab, dsk


def _split_bf16(x):
    hi = x.astype(BF16)
    return hi, (x - hi.astype(F32)).astype(BF16)


def _dot_nt(a, b):
    return lax.dot_general(a, b, (((1,), (1,)), ((), ())), preferred_element_type=F32)


def _build_weights(bbc_ref, ccc_ref, pw_ref, f_ref, gt_ref, m_ref):
    row_gi = lax.broadcasted_iota(I32, (OCT, 1), 0) >> 4
    col = lax.broadcasted_iota(I32, (1, SW), 1)
    col_gi = (col >> 6) & 7
    src = ((col >> 9) << 6) | (col & 63)
    k128 = lax.broadcasted_iota(I32, (2 * SSM_STATE, 1), 0)
    spread = jnp.where(k128 == src, 1.0, 0.0).astype(BF16)
    diag = row_gi == col_gi

    def expand(c_ref):
        hi, lo = _split_bf16(c_ref[...])
        d = (jnp.dot(hi, spread, preferred_element_type=F32)
             + jnp.dot(lo, spread, preferred_element_type=F32))
        d = jnp.where(diag, d, 0.0)
        return d[:, :OCT_STATES], d[:, OCT_STATES:]

    br, bi = expand(bbc_ref)
    cr, ci = expand(ccc_ref)
    chi_r, clo_r = _split_bf16(cr)
    chi_i, clo_i = _split_bf16(ci)

    def lam(k):
        return pw_ref[k:k + 1, :OCT_STATES], pw_ref[k:k + 1, OCT_STATES:]

    def dot3(a, bhi, blo):
        ahi, alo = _split_bf16(a)
        return _dot_nt(ahi, bhi) + _dot_nt(alo, bhi) + _dot_nt(ahi, blo)

    lags = []
    for k in range(CHUNK):
        pr, pi_ = lam(k)
        fr, fi = _cmul(br, bi, pr, pi_)
        s = CHUNK - 1 - k
        f_ref[s * OCT:(s + 1) * OCT, :] = jnp.concatenate([fr, fi], axis=1).astype(BF16)
        lags.append((dot3(fr, chi_r, clo_r) - dot3(fi, chi_i, clo_i)).astype(BF16))
        pr, pi_ = lam(k + 1)
        gr, gi = _cmul(cr, ci, pr, pi_)
        gt_ref[k * OCT:(k + 1) * OCT, :] = jnp.concatenate([gr, -gi], axis=1).astype(BF16)
    zero = jnp.zeros((OCT, OCT), BF16)
    for s in range(CHUNK):
        for t in range(CHUNK):
            m_ref[s * OCT:(s + 1) * OCT, t * OCT:(t + 1) * OCT] = lags[t - s] if t >= s else zero


def _cmul(ar, ai, br, bi):
    return ar * br - ai * bi, ar * bi + ai * br


def _chunk_scan(sloc, tab_ref):
    R = sloc.shape[0]
    nb = R // SUBLANES
    sr, si = sloc[:, :OCT_STATES], sloc[:, OCT_STATES:]
    rowi = lax.broadcasted_iota(I32, (R, 1), 0)
    tr = jnp.where(rowi == 0, 0.0, pltpu.roll(sr, 1, axis=0))
    ti = jnp.where(rowi == 0, 0.0, pltpu.roll(si, 1, axis=0))
    for lvl, d in enumerate((1, 2, 4)):
        mr = tab_ref[lvl * SUBLANES:(lvl + 1) * SUBLANES, :OCT_STATES]
        mi = tab_ref[lvl * SUBLANES:(lvl + 1) * SUBLANES, OCT_STATES:]
        mr = jnp.concatenate([mr] * nb, axis=0)
        mi = jnp.concatenate([mi] * nb, axis=0)
        pr, pi_ = _cmul(mr, mi, pltpu.roll(tr, d, axis=0), pltpu.roll(ti, d, axis=0))
        tr, ti = tr + pr, ti + pi_
    pwr = tab_ref[3 * SUBLANES:4 * SUBLANES, :OCT_STATES]
    pwi = tab_ref[3 * SUBLANES:4 * SUBLANES, OCT_STATES:]
    cr = jnp.zeros((1, OCT_STATES), F32)
    ci = jnp.zeros((1, OCT_STATES), F32)
    out_r, out_i = [], []
    for k in range(nb):
        ar = tr[k * SUBLANES:(k + 1) * SUBLANES, :]
        ai = ti[k * SUBLANES:(k + 1) * SUBLANES, :]
        pr, pi_ = _cmul(pwr, pwi, jnp.broadcast_to(cr, ar.shape), jnp.broadcast_to(ci, ai.shape))
        hr, hi = ar + pr, ai + pi_
        out_r.append(hr)
        out_i.append(hi)
        cr, ci = hr[SUBLANES - 1:, :], hi[SUBLANES - 1:, :]
    hin = jnp.concatenate([jnp.concatenate(out_r, axis=0), jnp.concatenate(out_i, axis=0)], axis=1)
    lr, li = pwr[0:1, :], pwi[0:1, :]
    fr, fi = _cmul(lr, li, cr, ci)
    fin = jnp.concatenate([fr + sr[R - 1:, :], fi + si[R - 1:, :]], axis=1)
    return hin, fin


def _ssm_body(u_ref, h0r_ref, h0i_ref, bbc_ref, ccc_ref, pw_ref, tab_ref, d_ref,
              y_ref, hout_ref, hr_ref, hi_ref, f_ref, gt_ref, m_ref):
    b = pl.program_id(1)

    @pl.when(b == 0)
    def _():
        _build_weights(bbc_ref, ccc_ref, pw_ref, f_ref, gt_ref, m_ref)

    def outputs(xs, xb, hin_bf):
        y = (jnp.dot(xb, m_ref[...], preferred_element_type=F32) + _dot_nt(hin_bf, gt_ref[...]))
        return [_gelu_tanh(y[:, t * OCT:(t + 1) * OCT] + d_ref[...] * xs[t]) for t in range(CHUNK)]

    @pl.when(b < BATCH)
    def _():
        R = SEQ // CHUNK
        xs = [u_ref[pl.ds(s, R, stride=CHUNK), :] for s in range(CHUNK)]
        xb = jnp.concatenate(xs, axis=1).astype(BF16)
        sloc = jnp.dot(xb, f_ref[...], preferred_element_type=F32)
        hin, fin = _chunk_scan(sloc, tab_ref)
        for t, yt in enumerate(outputs(xs, xb, hin.astype(BF16))):
            y_ref[pl.ds(t, R, stride=CHUNK), :] = yt
        hout_ref[...] = fin

    @pl.when(b == BATCH)
    def _():
        B = DEC_BATCH
        xs = [u_ref[B * s:B * (s + 1), :] for s in range(CHUNK)]
        xb = jnp.concatenate(xs, axis=1).astype(BF16)
        sloc = jnp.dot(xb, f_ref[...], preferred_element_type=F32)
        h0r, h0i = h0r_ref[...], h0i_ref[...]
        hin = jnp.concatenate([h0r, h0i], axis=1).astype(BF16)
        for t, yt in enumerate(outputs(xs, xb, hin)):
            y_ref[B * t:B * (t + 1), :] = yt
        lr = tab_ref[3 * SUBLANES:3 * SUBLANES + 1, :OCT_STATES]
        li = tab_ref[3 * SUBLANES:3 * SUBLANES + 1, OCT_STATES:]
        nr, ni = _cmul(lr, li, h0r, h0i)
        hr_ref[...] = nr + sloc[:, :OCT_STATES]
        hi_ref[...] = ni + sloc[:, OCT_STATES:]


def _ssm(u, h0r, h0i, tables):
    col0 = POOL_WIDTH // OCT
    im3 = lambda o, b: (o, 0, 0)
    st_spec = pl.BlockSpec((DEC_BATCH, OCT_STATES), lambda o, b: (0, o))
    return pl.pallas_call(
        _ssm_body,
        grid=(N_OCT, BATCH + 1),
        in_specs=[
            pl.BlockSpec((SEQ, OCT), lambda o, b: (b, col0 + o)), st_spec, st_spec,
            pl.BlockSpec((None, OCT, 2 * SSM_STATE), im3),
            pl.BlockSpec((None, OCT, 2 * SSM_STATE), im3),
            pl.BlockSpec((None, 2 * SUBLANES, SW), im3),
            pl.BlockSpec((None, 4 * SUBLANES, SW), im3),
            pl.BlockSpec((None, 1, OCT), im3),
        ],
        out_specs=(
            pl.BlockSpec((SEQ, OCT), lambda o, b: (b, o)),
            pl.BlockSpec((None, 1, SW), lambda o, b: (jnp.minimum(b, BATCH - 1) * N_OCT + o, 0, 0)),
            st_spec, st_spec,
        ),
        out_shape=(
            jax.ShapeDtypeStruct((T_PAD, SSM_WIDTH), F32),
            jax.ShapeDtypeStruct((BATCH * N_OCT, 1, SW), F32),
            jax.ShapeDtypeStruct((DEC_BATCH, SSM_GROUPS * SSM_STATE), F32),
            jax.ShapeDtypeStruct((DEC_BATCH, SSM_GROUPS * SSM_STATE), F32),
        ),
        scratch_shapes=[pltpu.VMEM((CW, SW), BF16), pltpu.VMEM((CW, SW), BF16), pltpu.VMEM((CW, CW), BF16)],
        compiler_params=_cparams(2),
        name="ssm",
    )(u, h0r, h0i, *tables)


PM_TM = 256
PM_PROMPT_BLOCKS = T_PROMPT // PM_TM
PM_STEPS = T_ALL // PM_TM


def _route(logits, valid, cnt_ref):
    lane = lax.broadcasted_iota(I32, (PM_TM, ROUTE_LANES), 1)
    neg = jnp.float32(-jnp.inf)
    big = jnp.int32(1 << 20)
    is_g = lane < MOE_GROUPS
    gmax = jnp.max(jnp.where(is_g, logits, neg), axis=1, keepdims=True)
    g_idx = jnp.min(jnp.where(is_g & (logits == gmax), lane, big), axis=1, keepdims=True)
    g_den = jnp.sum(jnp.where(is_g, jnp.exp(logits - gmax), 0.0), axis=1, keepdims=True)
    g_val = 1.0 / g_den
    e_lane = lane - EXP_LANE0
    sel = (e_lane >= 0) & (e_lane < MOE_EXPERTS) & ((e_lane >> 3) == g_idx)
    m1 = jnp.max(jnp.where(sel, logits, neg), axis=1, keepdims=True)
    i1 = jnp.min(jnp.where(sel & (logits == m1), lane, big), axis=1, keepdims=True)
    sel2 = sel & (lane != i1)
    m2 = jnp.max(jnp.where(sel2, logits, neg), axis=1, keepdims=True)
    i2 = jnp.min(jnp.where(sel2 & (logits == m2), lane, big), axis=1, keepdims=True)
    e2 = jnp.exp(m2 - m1)
    w1 = g_val / (1.0 + e2)
    w2 = g_val * e2 / (1.0 + e2)
    oh1 = lane == i1
    oh2 = lane == i2
    oh = jnp.where(oh1 | oh2, valid, 0.0)
    rr = lax.broadcasted_iota(I32, (PM_TM, PM_TM), 0)
    cc = lax.broadcasted_iota(I32, (PM_TM, PM_TM), 1)
    tri = jnp.where(cc < rr, 1.0, 0.0).astype(BF16)
    base = cnt_ref[...] + jnp.dot(tri, oh.astype(BF16), preferred_element_type=F32)
    rank1 = jnp.sum(jnp.where(oh1, base, 0.0), axis=1, keepdims=True)
    rank2 = jnp.sum(jnp.where(oh2, base, 0.0), axis=1, keepdims=True)
    cnt_ref[...] = cnt_ref[...] + jnp.sum(oh, axis=0, keepdims=True)
    rt = jnp.where(lane == 0, w1, 0.0)
    rt = jnp.where(lane == 1, w2, rt)
    rt = jnp.where(lane == 2, rank1, rt)
    rt = jnp.where(lane == 3, rank2, rt)
    rt = jnp.where(lane == 4, (i1 - EXP_LANE0).astype(F32), rt)
    rt = jnp.where(lane == 5, (i2 - EXP_LANE0).astype(F32), rt)
    return rt


def _postmix_body(ya_ref, gp_ref, gs_ref, yp_ref, xp_ref, xs_ref, wa_ref, wb_ref, wo_ref,
                  gf_ref, wr_ref, br_ref, h_ref, tn_ref, rt_ref, cnt_out_ref,
                  cnt_ref, lg_ref):
    i = pl.program_id(0)
    @pl.when(i == 0)
    def _():
        cnt_ref[...] = jnp.zeros_like(cnt_ref)
        lg_ref[...] = jnp.zeros_like(lg_ref)
    prev_logits = lg_ref[...]

    ya = ya_ref[...].astype(BF16)
    a = jnp.dot(ya, wa_ref[...], preferred_element_type=F32)
    bg = jnp.dot(ya, wb_ref[...], preferred_element_type=F32)
    y_ssm = a * _sigmoid(bg)
    merged = (gp_ref[...].astype(F32) * yp_ref[...].astype(F32)
              + gs_ref[...].astype(F32) * y_ssm)
    x = jnp.where(jnp.minimum(i, PM_STEPS - 1) < PM_PROMPT_BLOCKS, xp_ref[...], xs_ref[...])
    h = x + jnp.dot(merged.astype(BF16), wo_ref[...], preferred_element_type=F32)
    h_ref[...] = h
    inv = lax.rsqrt(jnp.mean(h * h, axis=-1, keepdims=True) + EPS)
    tn = (h * inv) * gf_ref[...]
    tn_ref[...] = _pack_pairs(tn)
    t_hi = tn.astype(BF16)
    t_lo = (tn - t_hi.astype(F32)).astype(BF16)
    hh = jnp.dot(t_hi, wr_ref[...], preferred_element_type=F32)
    lh = jnp.dot(t_lo, wr_ref[:, :ROUTE_LANES], preferred_element_type=F32)
    lg_ref[...] = (hh[:, :ROUTE_LANES] + lh + hh[:, ROUTE_LANES:]) + br_ref[...]
    rt_ref[...] = _route(prev_logits, jnp.where(i > 0, 1.0, 0.0), cnt_ref)
    cnt_out_ref[...] = cnt_ref[...]


def _postmix(y_act, gates, y_pool, xp, xs, wa, wb, wo, g_ffn, wr_cat, b_r):
    npb = PM_PROMPT_BLOCKS
    const2 = lambda i: (0, 0)
    tile = lambda i: jnp.minimum(i, PM_STEPS - 1)
    return pl.pallas_call(
        _postmix_body,
        grid=(PM_STEPS + 1,),
        in_specs=[
            pl.BlockSpec((PM_TM, SSM_WIDTH), lambda i: (tile(i), 0)),
            pl.BlockSpec((PM_TM, D_MODEL), lambda i: (tile(i), 0)),
            pl.BlockSpec((PM_TM, D_MODEL), lambda i: (tile(i), 1)),
            pl.BlockSpec((PM_TM, D_MODEL), lambda i: (tile(i), 0)),
            pl.BlockSpec((PM_TM, D_MODEL), lambda i: (jnp.minimum(i, npb - 1), 0)),
            pl.BlockSpec((PM_TM, D_MODEL), lambda i: (jnp.maximum(tile(i) - npb, 0), 0)),
            pl.BlockSpec((SSM_WIDTH, D_MODEL), const2, pipeline_mode=pl.Buffered(1)),
            pl.BlockSpec((SSM_WIDTH, D_MODEL), const2, pipeline_mode=pl.Buffered(1)),
            pl.BlockSpec((D_MODEL, D_MODEL), const2, pipeline_mode=pl.Buffered(1)),
            pl.BlockSpec((1, D_MODEL), const2),
            pl.BlockSpec((D_MODEL, 2 * ROUTE_LANES), const2),
            pl.BlockSpec((1, ROUTE_LANES), const2),
        ],
        out_specs=(
            pl.BlockSpec((PM_TM, D_MODEL), lambda i: (tile(i), 0)),
            pl.BlockSpec((PM_TM, D_PACK), lambda i: (tile(i), 0)),
            pl.BlockSpec((PM_TM, ROUTE_LANES), lambda i: (jnp.maximum(i - 1, 0), 0)),
            pl.BlockSpec((1, ROUTE_LANES), const2),
        ),
        out_shape=(
            jax.ShapeDtypeStruct((T_ALL, D_MODEL), F32),
            jax.ShapeDtypeStruct((T_ALL, D_PACK), U32),
            jax.ShapeDtypeStruct((T_ALL, ROUTE_LANES), F32),
            jax.ShapeDtypeStruct((1, ROUTE_LANES), F32),
        ),
        scratch_shapes=[pltpu.VMEM((1, ROUTE_LANES), F32), pltpu.VMEM((PM_TM, ROUTE_LANES), F32)],
        compiler_params=_cparams(1),
        name="postmix",
    )(y_act, gates, gates, y_pool, xp, xs, wa, wb, wo, g_ffn, wr_cat, b_r)


SC_CH = 96


def _sc_workers():
    info = plsc.get_sparse_core_info()
    return info.num_cores, info.num_cores * info.num_subcores


def _sc_dispatch(tn, slots):
    n_cores, n_workers = _sc_workers()
    per_w = (T_ALL // SC_CH) // n_workers
    assert per_w * n_workers * SC_CH == T_ALL
    slots = slots.reshape(2, n_workers, per_w, SC_CH)

    @functools.partial(
        pl.kernel,
        mesh=plsc.VectorSubcoreMesh(core_axis_name="c", subcore_axis_name="s"),
        out_type=jax.ShapeDtypeStruct((N_SLOTS, D_PACK), U32),
        scratch_types=[pltpu.VMEM((2, per_w, SC_CH), I32), pltpu.VMEM((SC_CH, D_PACK), U32)],
    )
    def k(tn_hbm, slots_hbm, xs_hbm, idx_v, rows_v):
        wid = lax.axis_index("s") * n_cores + lax.axis_index("c")
        c0 = wid * per_w
        pltpu.sync_copy(slots_hbm.at[0, wid], idx_v.at[0])
        pltpu.sync_copy(slots_hbm.at[1, wid], idx_v.at[1])

        @pl.loop(0, per_w)
        def _(c):
            row0 = pl.multiple_of((c0 + c) * SC_CH, SC_CH)
            pltpu.sync_copy(tn_hbm.at[pl.ds(row0, SC_CH)], rows_v)
            pltpu.sync_copy(rows_v, xs_hbm.at[idx_v.at[0, c]])
            pltpu.sync_copy(rows_v, xs_hbm.at[idx_v.at[1, c]])

    return k(tn, slots)


def _sc_collect(ys, slots):
    n_cores, n_workers = _sc_workers()
    per_w = (N_ASSIGN // SC_CH) // n_workers
    assert per_w * n_workers * SC_CH == N_ASSIGN
    slots = slots.reshape(n_workers, per_w, SC_CH)

    @functools.partial(
        pl.kernel,
        mesh=plsc.VectorSubcoreMesh(core_axis_name="c", subcore_axis_name="s"),
        out_type=jax.ShapeDtypeStruct((N_ASSIGN, D_PACK), U32),
        scratch_types=[pltpu.VMEM((per_w, SC_CH), I32), pltpu.VMEM((SC_CH, D_PACK), U32)],
    )
    def k(ys_hbm, slots_hbm, out_hbm, idx_v, rows_v):
        wid = lax.axis_index("s") * n_cores + lax.axis_index("c")
        c0 = wid * per_w
        pltpu.sync_copy(slots_hbm.at[wid], idx_v)

        @pl.loop(0, per_w)
        def _(c):
            row0 = pl.multiple_of((c0 + c) * SC_CH, SC_CH)
            pltpu.sync_copy(ys_hbm.at[idx_v.at[c]], rows_v)
            pltpu.sync_copy(rows_v, out_hbm.at[pl.ds(row0, SC_CH)])

    return k(ys, slots)


W_PARTS = 2


def _expert_body(t0_ref, nt_ref, xs_hbm, wg_hbm, wu_hbm, wd_hbm, ys_hbm,
                 wg_ref, wu_ref, wd_ref, xb_ref, yb_ref, wgb_ref, wub_ref, wdb_ref, wsem, xsem, ysem):
    e = pl.program_id(0)
    n = nt_ref[e]
    g0 = t0_ref[e]
    ws = e & 1

    def w_copies(ex, slot):
        out = []
        for hbm, buf in ((wg_hbm, wg_ref), (wu_hbm, wu_ref), (wd_hbm, wd_ref)):
            rb = buf.shape[1] // W_PARTS
            for p in range(W_PARTS):
                out.append((pltpu.make_async_copy(hbm.at[ex, pl.ds(p * rb, rb)],
                                                  buf.at[slot, pl.ds(p * rb, rb)], wsem.at[slot]), p))
        return out

    @pl.when(e == 0)
    def _():
        for cp, p in w_copies(0, 0):
            cp.start(priority=p)

    @pl.when(e + 1 < MOE_EXPERTS)
    def _():
        for cp, p in w_copies(e + 1, 1 - ws):
            cp.start(priority=p)

    for cp, _ in w_copies(e, ws):
        cp.wait()

    def rows(j):
        return pl.ds(pl.multiple_of((g0 + j) * TME, TME), TME)

    def x_copy(j, s):
        return pltpu.make_async_copy(xs_hbm.at[rows(j)], xb_ref.at[s], xsem.at[s])

    def y_copy(j, s):
        return pltpu.make_async_copy(yb_ref.at[s], ys_hbm.at[rows(j)], ysem.at[s])

    @pl.when(n > 0)
    def _():
        x_copy(0, 0).start()
        wgb_ref[...] = wg_ref[ws].astype(BF16)
        wub_ref[...] = wu_ref[ws].astype(BF16)
        wdb_ref[...] = wd_ref[ws].astype(BF16)

        def tile(j, c):
            s = j & 1
            x_copy(j, s).wait()
            @pl.when(j + 1 < n)
            def _():
                x_copy(j + 1, 1 - s).start()
            @pl.when(j >= 2)
            def _():
                y_copy(j - 2, s).wait()
            x = _unpack_pairs(xb_ref[s], BF16)
            hg = jnp.dot(x, wgb_ref[...], preferred_element_type=F32)
            hu = jnp.dot(x, wub_ref[...], preferred_element_type=F32)
            act = (hg * _sigmoid(hg)) * hu
            yb_ref[s] = _pack_pairs(jnp.dot(act.astype(BF16), wdb_ref[...], preferred_element_type=F32))
            y_copy(j, s).start()
            return c
        lax.fori_loop(0, n, tile, 0)

        @pl.when(n >= 2)
        def _():
            y_copy(n - 2, n & 1).wait()
        y_copy(n - 1, (n - 1) & 1).wait()


def _experts(tile0, tiles, xs, w_eg, w_eu, w_ed):
    any_spec = pl.BlockSpec(memory_space=pl.ANY)
    grid_spec = pltpu.PrefetchScalarGridSpec(
        num_scalar_prefetch=2,
        grid=(MOE_EXPERTS,),
        in_specs=[any_spec] * 4,
        out_specs=any_spec,
        scratch_shapes=[
            pltpu.VMEM((2, D_MODEL, MOE_FF), F32),
            pltpu.VMEM((2, D_MODEL, MOE_FF), F32),
            pltpu.VMEM((2, MOE_FF, D_MODEL), F32),
            pltpu.VMEM((2, TME, D_PACK), U32),
            pltpu.VMEM((2, TME, D_PACK), U32),
            pltpu.VMEM((D_MODEL, MOE_FF), BF16),
            pltpu.VMEM((D_MODEL, MOE_FF), BF16),
            pltpu.VMEM((MOE_FF, D_MODEL), BF16),
            pltpu.SemaphoreType.DMA((2,)),
            pltpu.SemaphoreType.DMA((2,)),
            pltpu.SemaphoreType.DMA((2,)),
        ],
    )
    return pl.pallas_call(
        _expert_body,
        grid_spec=grid_spec,
        out_shape=jax.ShapeDtypeStruct((N_SLOTS, D_PACK), U32),
        compiler_params=_cparams(1),
        name="experts",
    )(tile0, tiles, xs, w_eg, w_eu, w_ed)


FN_TM = 256
FN_PROMPT_BLOCKS = T_PROMPT // FN_TM


def _final_body(h_ref, y0_ref, y1_ref, rt_ref, g_ref, op_ref, os_ref):
    i = pl.program_id(0)
    rt = rt_ref[...]
    z = (h_ref[...] + rt[:, 0:1] * _unpack_pairs(y0_ref[...], F32)
         + rt[:, 1:2] * _unpack_pairs(y1_ref[...], F32))
    inv = lax.rsqrt(jnp.mean(z * z, axis=-1, keepdims=True) + EPS)
    out = (z * inv) * g_ref[...]
    @pl.when(i < FN_PROMPT_BLOCKS)
    def _():
        op_ref[...] = out
    @pl.when(i >= FN_PROMPT_BLOCKS)
    def _():
        os_ref[...] = out


def _final(h, y, route, g_final):
    n = T_ALL // FN_TM
    npb = FN_PROMPT_BLOCKS
    yoff = T_ALL // FN_TM
    return pl.pallas_call(
        _final_body,
        grid=(n,),
        in_specs=[
            pl.BlockSpec((FN_TM, D_MODEL), lambda i: (i, 0)),
            pl.BlockSpec((FN_TM, D_PACK), lambda i: (i, 0)),
            pl.BlockSpec((FN_TM, D_PACK), lambda i: (yoff + i, 0)),
            pl.BlockSpec((FN_TM, ROUTE_LANES), lambda i: (i, 0)),
            pl.BlockSpec((1, D_MODEL), lambda i: (0, 0)),
        ],
        out_specs=(
            pl.BlockSpec((FN_TM, D_MODEL), lambda i: (jnp.minimum(i, npb - 1), 0)),
            pl.BlockSpec((FN_TM, D_MODEL), lambda i: (jnp.maximum(i - npb, 0), 0)),
        ),
        out_shape=(
            jax.ShapeDtypeStruct((T_PROMPT, D_MODEL), F32),
            jax.ShapeDtypeStruct((T_SAMPLE, D_MODEL), F32),
        ),
        compiler_params=_cparams(1),
        name="final",
    )(h, y, y, route, g_final)


def _dispatch_plan(route, cnt):
    counts = cnt[0, EXP_LANE0:EXP_LANE0 + MOE_EXPERTS].astype(I32)
    tiles = (counts + (TME - 1)) // TME
    cumt = jnp.cumsum(tiles)
    pad_off = (cumt - tiles) * TME
    rank = route[:, 2:4].astype(I32)
    eid = route[:, 4:6].astype(I32)
    onehot = eid[..., None] == jnp.arange(MOE_EXPERTS, dtype=I32)
    slots = (rank + jnp.sum(jnp.where(onehot, pad_off, 0), axis=-1)).T
    return slots, cumt - tiles, tiles


def kernel(x_prompt, x_sample, state_pool, state_ssm_re, state_ssm_im, g_mix, w_in, w_pool,
           pool_scale, ssm_a_re, ssm_a_im, ssm_log_dt, ssm_b_re, ssm_b_im, ssm_c_re, ssm_c_im,
           ssm_d, w_glu_a, w_glu_b, w_out, g_ffn, w_router_group, b_router_group,
           w_router_expert, b_router_expert, w_exp_gate, w_exp_up, w_exp_down, g_final):
    l = 0
    xp = x_prompt.reshape(T_PROMPT, D_MODEL)
    xs = x_sample.transpose(1, 0, 2).reshape(T_SAMPLE, D_MODEL)
    w_in_bf = w_in[l].astype(BF16)
    w_pool_bf = w_pool[l].astype(BF16)
    wa_bf = w_glu_a[l].astype(BF16)
    wb_bf = w_glu_b[l].astype(BF16)
    wo_bf = w_out[l].astype(BF16)
    g_mix2 = g_mix[l].reshape(1, D_MODEL)
    scale2 = pool_scale[l].reshape(1, D_MODEL)

    u, gates = _inproj(xp, g_mix2, w_in_bf, 0)
    u, gates = _inproj(xs, g_mix2, w_in_bf, T_PROMPT // IN_TM, dst=(u, gates))

    y_pool, pool_tail = _pool_prompt(u, w_pool_bf, scale2)
    hist_tm = state_pool[l].transpose(1, 0, 2)
    y_pool = _pool_sample(u, hist_tm, w_pool_bf, scale2, y_pool)

    tables = _ssm_tables(ssm_a_re[l], ssm_a_im[l], ssm_log_dt[l], ssm_b_re[l], ssm_b_im[l],
                         ssm_c_re[l], ssm_c_im[l], ssm_d[l])
    h0r = state_ssm_re[l].reshape(DEC_BATCH, SSM_GROUPS * SSM_STATE)
    h0i = state_ssm_im[l].reshape(DEC_BATCH, SSM_GROUPS * SSM_STATE)
    y_act, h_prompt, hs_re, hs_im = _ssm(u, h0r, h0i, tables)

    w_r = jnp.zeros((D_MODEL, ROUTE_LANES), F32)
    w_r = w_r.at[:, :MOE_GROUPS].set(w_router_group[l])
    w_r = w_r.at[:, EXP_LANE0:EXP_LANE0 + MOE_EXPERTS].set(w_router_expert[l])
    wr_hi = w_r.astype(BF16)
    wr_cat = jnp.concatenate([wr_hi, (w_r - wr_hi.astype(F32)).astype(BF16)], axis=1)
    b_r = jnp.zeros((1, ROUTE_LANES), F32)
    b_r = b_r.at[0, :MOE_GROUPS].set(b_router_group[l])
    b_r = b_r.at[0, EXP_LANE0:EXP_LANE0 + MOE_EXPERTS].set(b_router_expert[l])

    h, tn, route, cnt = _postmix(y_act, gates, y_pool, xp, xs, wa_bf, wb_bf, wo_bf,
                                 g_ffn[l].reshape(1, D_MODEL), wr_cat, b_r)
    slots, tile0, tiles = _dispatch_plan(route, cnt)
    xs_sorted = _sc_dispatch(tn, slots)
    ys_sorted = _experts(tile0, tiles, xs_sorted, w_exp_gate[l], w_exp_up[l], w_exp_down[l])
    y = _sc_collect(ys_sorted, slots)
    yp, ys = _final(h, y, route, g_final.reshape(1, D_MODEL))

    y_prompt = yp.reshape(BATCH, SEQ, D_MODEL)
    y_sample = ys.reshape(DEC_SEQ, DEC_BATCH, D_MODEL).transpose(1, 0, 2)
    new_pool_prompt = pool_tail[:, HIST - POOL_BUF:, :][None]
    us = u[T_PROMPT:T_ALL, :POOL_WIDTH].reshape(DEC_SEQ, DEC_BATCH, POOL_WIDTH).transpose(1, 0, 2)
    new_pool_sample = jnp.concatenate([state_pool[l][:, DEC_SEQ:, :], us], axis=1)[None]
    hp = h_prompt.reshape(BATCH, N_OCT, 2, OCT_GROUPS, SSM_STATE).transpose(2, 0, 1, 3, 4)
    hp = hp.reshape(2, BATCH, SSM_GROUPS, SSM_STATE)
    shp = (1, DEC_BATCH, SSM_GROUPS, SSM_STATE)
    return (y_prompt, y_sample, new_pool_prompt, hp[0][None], hp[1][None], new_pool_sample,
            hs_re.reshape(shp), hs_im.reshape(shp))
```

```python
import functools
import math

import jax
import jax.numpy as jnp
from jax import lax
from jax.experimental import pallas as pl
from jax.experimental.pallas import tpu as pltpu
from jax.experimental.pallas import tpu_sc as plsc

F32 = jnp.float32
BF16 = jnp.bfloat16
I32 = jnp.int32
U32 = jnp.uint32

D_MODEL = 2048
BATCH = 4
SEQ = 2048
DEC_BATCH = 128
DEC_SEQ = 8
PAST_LEN = 16384
POOL_WIDTH = D_MODEL // 2
POOL_WINDOWS = (2, 4, 8, 16)
POOL_GROUPS = len(POOL_WINDOWS)
POOL_GROUP_CH = POOL_WIDTH // POOL_GROUPS
POOL_OUT_CH = D_MODEL // POOL_GROUPS
POOL_BUF = max(POOL_WINDOWS) - 1
SSM_WIDTH = D_MODEL // 2
SSM_GROUP_CH = 16
SSM_GROUPS = SSM_WIDTH // SSM_GROUP_CH
SSM_STATE = 64
IN_WIDTH = POOL_WIDTH + SSM_WIDTH + 2 * D_MODEL
D_PACK = D_MODEL // 2
MOE_GROUPS = 4
MOE_EPG = 8
MOE_EXPERTS = MOE_GROUPS * MOE_EPG
MOE_FF = D_MODEL // 4
EPS = 1e-6

T_PROMPT = BATCH * SEQ
T_SAMPLE = DEC_BATCH * DEC_SEQ
T_ALL = T_PROMPT + T_SAMPLE
T_PAD = (BATCH + 1) * SEQ

LANES = 128
SUBLANES = 8
VMEM_LIMIT = 56 * 1024 * 1024

CHUNK = 8
OCT = LANES
N_OCT = SSM_WIDTH // OCT
OCT_GROUPS = OCT // SSM_GROUP_CH
OCT_STATES = OCT_GROUPS * SSM_STATE
CW = CHUNK * OCT
SW = 2 * OCT_STATES

ROUTE_LANES = LANES
EXP_LANE0 = MOE_GROUPS
N_ASSIGN = 2 * T_ALL
TME = 256
N_ITEMS_MAX = N_ASSIGN // TME + MOE_EXPERTS
N_SLOTS = N_ITEMS_MAX * TME


def _cparams(n_axes):
    return pltpu.CompilerParams(dimension_semantics=("arbitrary",) * n_axes,
                                vmem_limit_bytes=VMEM_LIMIT)


def _sigmoid(x):
    return 1.0 / (1.0 + jnp.exp(-x))


def _pack_pairs(x):
    c = x.shape[1] // 2
    hi = lax.bitcast_convert_type(x[:, :c].astype(BF16).astype(F32), U32)
    lo = lax.bitcast_convert_type(x[:, c:].astype(BF16).astype(F32), U32)
    return hi | (lo >> 16)


def _unpack_pairs(u, dtype):
    hi = lax.bitcast_convert_type(u & jnp.uint32(0xFFFF0000), F32)
    lo = lax.bitcast_convert_type(u << 16, F32)
    return jnp.concatenate([hi, lo], axis=1).astype(dtype)


def _gelu_tanh(x):
    c = math.sqrt(2.0 / math.pi)
    return 0.5 * x * (1.0 + jnp.tanh(c * (x + 0.044715 * (x * x * x))))


IN_TM = 1024
IN_TN = 1024
U_WIDTH = POOL_WIDTH + SSM_WIDTH
GATE_WIDTH = 2 * D_MODEL
IN_U_STEPS = U_WIDTH // IN_TN


def _inproj_body(x_ref, g_ref, w_ref, *rest):
    u_ref, gate_ref, xn_ref = rest[-3:]
    j = pl.program_id(1)
    @pl.when(j == 0)
    def _():
        x = x_ref[...]
        inv = lax.rsqrt(jnp.mean(x * x, axis=-1, keepdims=True) + EPS)
        xn_ref[...] = ((x * inv) * g_ref[...]).astype(BF16)
    acc = jnp.dot(xn_ref[...], w_ref[...], preferred_element_type=F32)
    @pl.when(j < IN_U_STEPS)
    def _():
        u_ref[...] = acc
    @pl.when(j >= IN_U_STEPS)
    def _():
        gate_ref[...] = _sigmoid(acc).astype(BF16)


def _inproj(x, g, w_bf, row_block0, dst=None):
    n_i = x.shape[0] // IN_TM
    in_specs = [
        pl.BlockSpec((IN_TM, D_MODEL), lambda i, j: (i, 0)),
        pl.BlockSpec((1, D_MODEL), lambda i, j: (0, 0)),
        pl.BlockSpec((D_MODEL, IN_TN), lambda i, j: (0, j)),
    ]
    args = [x, g, w_bf]
    aliases = {}
    if dst is not None:
        in_specs += [pl.BlockSpec(memory_space=pl.ANY)] * 2
        args += list(dst)
        aliases = {3: 0, 4: 1}
    return pl.pallas_call(
        _inproj_body,
        grid=(n_i, IN_WIDTH // IN_TN),
        in_specs=in_specs,
        out_specs=(
            pl.BlockSpec((IN_TM, IN_TN), lambda i, j: (i + row_block0, jnp.minimum(j, IN_U_STEPS - 1))),
            pl.BlockSpec((IN_TM, IN_TN), lambda i, j: (i + row_block0, jnp.maximum(j - IN_U_STEPS, 0))),
        ),
        out_shape=(
            jax.ShapeDtypeStruct((T_PAD, U_WIDTH), F32),
            jax.ShapeDtypeStruct((T_ALL, GATE_WIDTH), BF16),
        ),
        scratch_shapes=[pltpu.VMEM((IN_TM, D_MODEL), BF16)],
        input_output_aliases=aliases,
        compiler_params=_cparams(2),
        name="inproj",
    )(*args)


PP_TM = 512
HIST = 16


def _pool_project(pooled_g, g, w_ref, sc_ref, o_ref):
    y = jnp.dot(pooled_g.astype(BF16), w_ref[g], preferred_element_type=F32)
    lo, hi = g * POOL_OUT_CH, (g + 1) * POOL_OUT_CH
    o_ref[:, lo:hi] = (y * sc_ref[:, lo:hi]).astype(o_ref.dtype)


def _pool_prompt_body(u_ref, w_ref, sc_ref, o_ref, tail_ref, hist_ref):
    i = pl.program_id(1)
    @pl.when(i == 0)
    def _():
        hist_ref[...] = jnp.zeros_like(hist_ref)
    u = u_ref[...]
    ext = jnp.concatenate([hist_ref[...], u], axis=0)
    hist_ref[...] = u[PP_TM - HIST:, :]
    tail_ref[...] = u[PP_TM - HIST:, :]
    pos = i * PP_TM + lax.broadcasted_iota(I32, (PP_TM, 1), 0)
    for g, w in enumerate(POOL_WINDOWS):
        lo, hi = g * POOL_GROUP_CH, (g + 1) * POOL_GROUP_CH
        s = ext[:, lo:hi]
        d = 1
        while d < w:
            s = s + pltpu.roll(s, d, axis=0)
            d *= 2
        cnt = jnp.minimum(w, pos + 1).astype(F32)
        pooled = s[HIST:, :] / cnt - u[:, lo:hi]
        _pool_project(pooled, g, w_ref, sc_ref, o_ref)


def _pool_prompt(u, w_pool_bf, pool_scale):
    n_i = SEQ // PP_TM
    return pl.pallas_call(
        _pool_prompt_body,
        grid=(BATCH, n_i),
        in_specs=[
            pl.BlockSpec((PP_TM, POOL_WIDTH), lambda b, i: (b * n_i + i, 0)),
            pl.BlockSpec((POOL_GROUPS, POOL_GROUP_CH, POOL_OUT_CH), lambda b, i: (0, 0, 0)),
            pl.BlockSpec((1, D_MODEL), lambda b, i: (0, 0)),
        ],
        out_specs=(
            pl.BlockSpec((PP_TM, D_MODEL), lambda b, i: (b * n_i + i, 0)),
            pl.BlockSpec((None, HIST, POOL_WIDTH), lambda b, i: (b, 0, 0)),
        ),
        out_shape=(
            jax.ShapeDtypeStruct((T_ALL, D_MODEL), BF16),
            jax.ShapeDtypeStruct((BATCH, HIST, POOL_WIDTH), F32),
        ),
        scratch_shapes=[pltpu.VMEM((HIST, POOL_WIDTH), F32)],
        compiler_params=_cparams(2),
        name="pool_prompt",
    )(u, w_pool_bf, pool_scale)


def _pool_sample_body(u_ref, hist_ref, w_ref, sc_ref, _dst, o_ref, buf_ref):
    rows = [hist_ref[k] for k in range(POOL_BUF)]
    rows += [u_ref[DEC_BATCH * t:DEC_BATCH * (t + 1), :] for t in range(DEC_SEQ)]
    n = len(rows)
    for k in range(POOL_BUF):
        buf_ref[k] = rows[n - POOL_BUF + k]
    for g, w in enumerate(POOL_WINDOWS):
        lo, hi = g * POOL_GROUP_CH, (g + 1) * POOL_GROUP_CH
        f = [r[:, lo:hi] for r in rows]
        cur = f
        d = 1
        while d < w:
            cur = [cur[k] + cur[k - d] if k - d >= 0 else cur[k] for k in range(n)]
            d *= 2
        pooled = jnp.concatenate(
            [cur[POOL_BUF + t] / float(w) - f[POOL_BUF + t] for t in range(DEC_SEQ)], axis=0)
        _pool_project(pooled, g, w_ref, sc_ref, o_ref)


def _pool_sample(u, hist_tm, w_pool_bf, pool_scale, y_pool):
    blk = T_PROMPT // T_SAMPLE
    return pl.pallas_call(
        _pool_sample_body,
        grid=(1,),
        in_specs=[
            pl.BlockSpec((T_SAMPLE, POOL_WIDTH), lambda i: (blk, 0)),
            pl.BlockSpec((POOL_BUF, DEC_BATCH, POOL_WIDTH), lambda i: (0, 0, 0)),
            pl.BlockSpec((POOL_GROUPS, POOL_GROUP_CH, POOL_OUT_CH), lambda i: (0, 0, 0)),
            pl.BlockSpec((1, D_MODEL), lambda i: (0, 0)),
            pl.BlockSpec(memory_space=pl.ANY),
        ],
        out_specs=(
            pl.BlockSpec((T_SAMPLE, D_MODEL), lambda i: (blk, 0)),
            pl.BlockSpec((POOL_BUF, DEC_BATCH, POOL_WIDTH), lambda i: (0, 0, 0)),
        ),
        out_shape=(
            jax.ShapeDtypeStruct((T_ALL, D_MODEL), BF16),
            jax.ShapeDtypeStruct((POOL_BUF, DEC_BATCH, POOL_WIDTH), F32),
        ),
        input_output_aliases={4: 0},
        compiler_params=_cparams(1),
        name="pool_sample",
    )(u, hist_tm, w_pool_bf, pool_scale, y_pool)


def _ssm_tables(a_re, a_im, log_dt, b_re, b_im, c_re, c_im, d_skip):
    dt = jnp.exp(log_dt)[:, None]
    lr, li = a_re, a_im
    ab_re = jnp.exp(lr * dt) * jnp.cos(li * dt)
    ab_im = jnp.exp(lr * dt) * jnp.sin(li * dt)
    den = lr * lr + li * li
    nr, ni = ab_re - 1.0, ab_im
    q_re = (nr * lr + ni * li) / den
    q_im = (ni * lr - nr * li) / den
    bb_re = q_re[..., None] * b_re - q_im[..., None] * b_im
    bb_im = q_re[..., None] * b_im + q_im[..., None] * b_re

    def lam_rows(ks):
        k = jnp.asarray(ks, F32)[:, None, None]
        m = jnp.exp(k * lr * dt)
        re = (m * jnp.cos(k * li * dt)).reshape(len(ks), N_OCT, OCT_STATES)
        im = (m * jnp.sin(k * li * dt)).reshape(len(ks), N_OCT, OCT_STATES)
        return jnp.concatenate([re, im], axis=-1).transpose(1, 0, 2)

    def compact(re, im):
        v = jnp.concatenate([re, im], axis=-1)
        return v.reshape(N_OCT, OCT, 2 * SSM_STATE)

    bbc = compact(jnp.swapaxes(bb_re, 1, 2), jnp.swapaxes(bb_im, 1, 2))
    ccc = compact(c_re, c_im)
    pw = lam_rows(list(range(2 * SUBLANES)))
    r = jnp.arange(SUBLANES)[None, :, None]
    parts = [jnp.where(r >= dd, lam_rows([CHUNK * dd]), 0.0) for dd in (1, 2, 4)]
    parts.append(lam_rows([CHUNK * kk for kk in range(1, SUBLANES + 1)]))
    tab = jnp.concatenate(parts, axis=1)
    dsk = d_skip.reshape(N_OCT, 1, OCT)
    return bbc, ccc, pw, tab, dsk


def _split_bf16(x):
    hi = x.astype(BF16)
    return hi, (x - hi.astype(F32)).astype(BF16)


def _dot_nt(a, b):
    return lax.dot_general(a, b, (((1,), (1,)), ((), ())), preferred_element_type=F32)


def _build_weights(bbc_ref, ccc_ref, pw_ref, f_ref, gt_ref, m_ref):
    row_gi = lax.broadcasted_iota(I32, (OCT, 1), 0) >> 4
    col = lax.broadcasted_iota(I32, (1, SW), 1)
    col_gi = (col >> 6) & 7
    src = ((col >> 9) << 6) | (col & 63)
    k128 = lax.broadcasted_iota(I32, (2 * SSM_STATE, 1), 0)
    spread = jnp.where(k128 == src, 1.0, 0.0).astype(BF16)
    diag = row_gi == col_gi

    def expand(c_ref):
        hi, lo = _split_bf16(c_ref[...])
        d = (jnp.dot(hi, spread, preferred_element_type=F32)
             + jnp.dot(lo, spread, preferred_element_type=F32))
        d = jnp.where(diag, d, 0.0)
        return d[:, :OCT_STATES], d[:, OCT_STATES:]

    br, bi = expand(bbc_ref)
    cr, ci = expand(ccc_ref)
    chi_r, clo_r = _split_bf16(cr)
    chi_i, clo_i = _split_bf16(ci)

    def lam(k):
        return pw_ref[k:k + 1, :OCT_STATES], pw_ref[k:k + 1, OCT_STATES:]

    def dot3(a, bhi, blo):
        ahi, alo = _split_bf16(a)
        return _dot_nt(ahi, bhi) + _dot_nt(alo, bhi) + _dot_nt(ahi, blo)

    lags = []
    for k in range(CHUNK):
        pr, pi_ = lam(k)
        fr, fi = _cmul(br, bi, pr, pi_)
        s = CHUNK - 1 - k
        f_ref[s * OCT:(s + 1) * OCT, :] = jnp.concatenate([fr, fi], axis=1).astype(BF16)
        lags.append((dot3(fr, chi_r, clo_r) - dot3(fi, chi_i, clo_i)).astype(BF16))
        pr, pi_ = lam(k + 1)
        gr, gi = _cmul(cr, ci, pr, pi_)
        gt_ref[k * OCT:(k + 1) * OCT, :] = jnp.concatenate([gr, -gi], axis=1).astype(BF16)
    zero = jnp.zeros((OCT, OCT), BF16)
    for s in range(CHUNK):
        for t in range(CHUNK):
            m_ref[s * OCT:(s + 1) * OCT, t * OCT:(t + 1) * OCT] = lags[t - s] if t >= s else zero


def _cmul(ar, ai, br, bi):
    return ar * br - ai * bi, ar * bi + ai * br


def _chunk_scan(sloc, tab_ref):
    R = sloc.shape[0]
    nb = R // SUBLANES
    sr, si = sloc[:, :OCT_STATES], sloc[:, OCT_STATES:]
    rowi = lax.broadcasted_iota(I32, (R, 1), 0)
    tr = jnp.where(rowi == 0, 0.0, pltpu.roll(sr, 1, axis=0))
    ti = jnp.where(rowi == 0, 0.0, pltpu.roll(si, 1, axis=0))
    for lvl, d in enumerate((1, 2, 4)):
        mr = tab_ref[lvl * SUBLANES:(lvl + 1) * SUBLANES, :OCT_STATES]
        mi = tab_ref[lvl * SUBLANES:(lvl + 1) * SUBLANES, OCT_STATES:]
        mr = jnp.concatenate([mr] * nb, axis=0)
        mi = jnp.concatenate([mi] * nb, axis=0)
        pr, pi_ = _cmul(mr, mi, pltpu.roll(tr, d, axis=0), pltpu.roll(ti, d, axis=0))
        tr, ti = tr + pr, ti + pi_
    pwr = tab_ref[3 * SUBLANES:4 * SUBLANES, :OCT_STATES]
    pwi = tab_ref[3 * SUBLANES:4 * SUBLANES, OCT_STATES:]
    cr = jnp.zeros((1, OCT_STATES), F32)
    ci = jnp.zeros((1, OCT_STATES), F32)
    out_r, out_i = [], []
    for k in range(nb):
        ar = tr[k * SUBLANES:(k + 1) * SUBLANES, :]
        ai = ti[k * SUBLANES:(k + 1) * SUBLANES, :]
        pr, pi_ = _cmul(pwr, pwi, jnp.broadcast_to(cr, ar.shape), jnp.broadcast_to(ci, ai.shape))
        hr, hi = ar + pr, ai + pi_
        out_r.append(hr)
        out_i.append(hi)
        cr, ci = hr[SUBLANES - 1:, :], hi[SUBLANES - 1:, :]
    hin = jnp.concatenate([jnp.concatenate(out_r, axis=0), jnp.concatenate(out_i, axis=0)], axis=1)
    lr, li = pwr[0:1, :], pwi[0:1, :]
    fr, fi = _cmul(lr, li, cr, ci)
    fin = jnp.concatenate([fr + sr[R - 1:, :], fi + si[R - 1:, :]], axis=1)
    return hin, fin


def _ssm_body(u_ref, h0r_ref, h0i_ref, bbc_ref, ccc_ref, pw_ref, tab_ref, d_ref,
              y_ref, hout_ref, hr_ref, hi_ref, f_ref, gt_ref, m_ref):
    b = pl.program_id(1)

    @pl.when(b == 0)
    def _():
        _build_weights(bbc_ref, ccc_ref, pw_ref, f_ref, gt_ref, m_ref)

    def outputs(xs, xb, hin_bf):
        y = (jnp.dot(xb, m_ref[...], preferred_element_type=F32) + _dot_nt(hin_bf, gt_ref[...]))
        return [_gelu_tanh(y[:, t * OCT:(t + 1) * OCT] + d_ref[...] * xs[t]) for t in range(CHUNK)]

    @pl.when(b < BATCH)
    def _():
        R = SEQ // CHUNK
        xs = [u_ref[pl.ds(s, R, stride=CHUNK), :] for s in range(CHUNK)]
        xb = jnp.concatenate(xs, axis=1).astype(BF16)
        sloc = jnp.dot(xb, f_ref[...], preferred_element_type=F32)
        hin, fin = _chunk_scan(sloc, tab_ref)
        for t, yt in enumerate(outputs(xs, xb, hin.astype(BF16))):
            y_ref[pl.ds(t, R, stride=CHUNK), :] = yt
        hout_ref[...] = fin

    @pl.when(b == BATCH)
    def _():
        B = DEC_BATCH
        xs = [u_ref[B * s:B * (s + 1), :] for s in range(CHUNK)]
        xb = jnp.concatenate(xs, axis=1).astype(BF16)
        sloc = jnp.dot(xb, f_ref[...], preferred_element_type=F32)
        h0r, h0i = h0r_ref[...], h0i_ref[...]
        hin = jnp.concatenate([h0r, h0i], axis=1).astype(BF16)
        for t, yt in enumerate(outputs(xs, xb, hin)):
            y_ref[B * t:B * (t + 1), :] = yt
        lr = tab_ref[3 * SUBLANES:3 * SUBLANES + 1, :OCT_STATES]
        li = tab_ref[3 * SUBLANES:3 * SUBLANES + 1, OCT_STATES:]
        nr, ni = _cmul(lr, li, h0r, h0i)
        hr_ref[...] = nr + sloc[:, :OCT_STATES]
        hi_ref[...] = ni + sloc[:, OCT_STATES:]


def _ssm(u, h0r, h0i, tables):
    col0 = POOL_WIDTH // OCT
    im3 = lambda o, b: (o, 0, 0)
    st_spec = pl.BlockSpec((DEC_BATCH, OCT_STATES), lambda o, b: (0, o))
    return pl.pallas_call(
        _ssm_body,
        grid=(N_OCT, BATCH + 1),
        in_specs=[
            pl.BlockSpec((SEQ, OCT), lambda o, b: (b, col0 + o)), st_spec, st_spec,
            pl.BlockSpec((None, OCT, 2 * SSM_STATE), im3),
            pl.BlockSpec((None, OCT, 2 * SSM_STATE), im3),
            pl.BlockSpec((None, 2 * SUBLANES, SW), im3),
            pl.BlockSpec((None, 4 * SUBLANES, SW), im3),
            pl.BlockSpec((None, 1, OCT), im3),
        ],
        out_specs=(
            pl.BlockSpec((SEQ, OCT), lambda o, b: (b, o)),
            pl.BlockSpec((None, 1, SW), lambda o, b: (jnp.minimum(b, BATCH - 1) * N_OCT + o, 0, 0)),
            st_spec, st_spec,
        ),
        out_shape=(
            jax.ShapeDtypeStruct((T_PAD, SSM_WIDTH), F32),
            jax.ShapeDtypeStruct((BATCH * N_OCT, 1, SW), F32),
            jax.ShapeDtypeStruct((DEC_BATCH, SSM_GROUPS * SSM_STATE), F32),
            jax.ShapeDtypeStruct((DEC_BATCH, SSM_GROUPS * SSM_STATE), F32),
        ),
        scratch_shapes=[pltpu.VMEM((CW, SW), BF16), pltpu.VMEM((CW, SW), BF16), pltpu.VMEM((CW, CW), BF16)],
        compiler_params=_cparams(2),
        name="ssm",
    )(u, h0r, h0i, *tables)


PM_TM = 256
PM_PROMPT_BLOCKS = T_PROMPT // PM_TM
PM_STEPS = T_ALL // PM_TM


def _route(logits, valid, cnt_ref):
    lane = lax.broadcasted_iota(I32, (PM_TM, ROUTE_LANES), 1)
    neg = jnp.float32(-jnp.inf)
    big = jnp.int32(1 << 20)
    is_g = lane < MOE_GROUPS
    gmax = jnp.max(jnp.where(is_g, logits, neg), axis=1, keepdims=True)
    g_idx = jnp.min(jnp.where(is_g & (logits == gmax), lane, big), axis=1, keepdims=True)
    g_den = jnp.sum(jnp.where(is_g, jnp.exp(logits - gmax), 0.0), axis=1, keepdims=True)
    g_val = 1.0 / g_den
    e_lane = lane - EXP_LANE0
    sel = (e_lane >= 0) & (e_lane < MOE_EXPERTS) & ((e_lane >> 3) == g_idx)
    m1 = jnp.max(jnp.where(sel, logits, neg), axis=1, keepdims=True)
    i1 = jnp.min(jnp.where(sel & (logits == m1), lane, big), axis=1, keepdims=True)
    sel2 = sel & (lane != i1)
    m2 = jnp.max(jnp.where(sel2, logits, neg), axis=1, keepdims=True)
    i2 = jnp.min(jnp.where(sel2 & (logits == m2), lane, big), axis=1, keepdims=True)
    e2 = jnp.exp(m2 - m1)
    w1 = g_val / (1.0 + e2)
    w2 = g_val * e2 / (1.0 + e2)
    oh1 = lane == i1
    oh2 = lane == i2
    oh = jnp.where(oh1 | oh2, valid, 0.0)
    rr = lax.broadcasted_iota(I32, (PM_TM, PM_TM), 0)
    cc = lax.broadcasted_iota(I32, (PM_TM, PM_TM), 1)
    tri = jnp.where(cc < rr, 1.0, 0.0).astype(BF16)
    base = cnt_ref[...] + jnp.dot(tri, oh.astype(BF16), preferred_element_type=F32)
    rank1 = jnp.sum(jnp.where(oh1, base, 0.0), axis=1, keepdims=True)
    rank2 = jnp.sum(jnp.where(oh2, base, 0.0), axis=1, keepdims=True)
    cnt_ref[...] = cnt_ref[...] + jnp.sum(oh, axis=0, keepdims=True)
    rt = jnp.where(lane == 0, w1, 0.0)
    rt = jnp.where(lane == 1, w2, rt)
    rt = jnp.where(lane == 2, rank1, rt)
    rt = jnp.where(lane == 3, rank2, rt)
    rt = jnp.where(lane == 4, (i1 - EXP_LANE0).astype(F32), rt)
    rt = jnp.where(lane == 5, (i2 - EXP_LANE0).astype(F32), rt)
    return rt


def _postmix_body(ya_ref, gp_ref, gs_ref, yp_ref, xp_ref, xs_ref, wa_ref, wb_ref, wo_ref,
                  gf_ref, wr_ref, br_ref, h_ref, tn_ref, rt_ref, cnt_out_ref,
                  cnt_ref, lg_ref):
    i = pl.program_id(0)
    @pl.when(i == 0)
    def _():
        cnt_ref[...] = jnp.zeros_like(cnt_ref)
        lg_ref[...] = jnp.zeros_like(lg_ref)
    prev_logits = lg_ref[...]

    ya = ya_ref[...].astype(BF16)
    a = jnp.dot(ya, wa_ref[...], preferred_element_type=F32)
    bg = jnp.dot(ya, wb_ref[...], preferred_element_type=F32)
    y_ssm = a * _sigmoid(bg)
    merged = (gp_ref[...].astype(F32) * yp_ref[...].astype(F32)
              + gs_ref[...].astype(F32) * y_ssm)
    x = jnp.where(jnp.minimum(i, PM_STEPS - 1) < PM_PROMPT_BLOCKS, xp_ref[...], xs_ref[...])
    h = x + jnp.dot(merged.astype(BF16), wo_ref[...], preferred_element_type=F32)
    h_ref[...] = h
    inv = lax.rsqrt(jnp.mean(h * h, axis=-1, keepdims=True) + EPS)
    tn = (h * inv) * gf_ref[...]
    tn_ref[...] = _pack_pairs(tn)
    t_hi = tn.astype(BF16)
    t_lo = (tn - t_hi.astype(F32)).astype(BF16)
    hh = jnp.dot(t_hi, wr_ref[...], preferred_element_type=F32)
    lh = jnp.dot(t_lo, wr_ref[:, :ROUTE_LANES], preferred_element_type=F32)
    lg_ref[...] = (hh[:, :ROUTE_LANES] + lh + hh[:, ROUTE_LANES:]) + br_ref[...]
    rt_ref[...] = _route(prev_logits, jnp.where(i > 0, 1.0, 0.0), cnt_ref)
    cnt_out_ref[...] = cnt_ref[...]


def _postmix(y_act, gates, y_pool, xp, xs, wa, wb, wo, g_ffn, wr_cat, b_r):
    npb = PM_PROMPT_BLOCKS
    const2 = lambda i: (0, 0)
    tile = lambda i: jnp.minimum(i, PM_STEPS - 1)
    return pl.pallas_call(
        _postmix_body,
        grid=(PM_STEPS + 1,),
        in_specs=[
            pl.BlockSpec((PM_TM, SSM_WIDTH), lambda i: (tile(i), 0)),
            pl.BlockSpec((PM_TM, D_MODEL), lambda i: (tile(i), 0)),
            pl.BlockSpec((PM_TM, D_MODEL), lambda i: (tile(i), 1)),
            pl.BlockSpec((PM_TM, D_MODEL), lambda i: (tile(i), 0)),
            pl.BlockSpec((PM_TM, D_MODEL), lambda i: (jnp.minimum(i, npb - 1), 0)),
            pl.BlockSpec((PM_TM, D_MODEL), lambda i: (jnp.maximum(tile(i) - npb, 0), 0)),
            pl.BlockSpec((SSM_WIDTH, D_MODEL), const2, pipeline_mode=pl.Buffered(1)),
            pl.BlockSpec((SSM_WIDTH, D_MODEL), const2, pipeline_mode=pl.Buffered(1)),
            pl.BlockSpec((D_MODEL, D_MODEL), const2, pipeline_mode=pl.Buffered(1)),
            pl.BlockSpec((1, D_MODEL), const2),
            pl.BlockSpec((D_MODEL, 2 * ROUTE_LANES), const2),
            pl.BlockSpec((1, ROUTE_LANES), const2),
        ],
        out_specs=(
            pl.BlockSpec((PM_TM, D_MODEL), lambda i: (tile(i), 0)),
            pl.BlockSpec((PM_TM, D_PACK), lambda i: (tile(i), 0)),
            pl.BlockSpec((PM_TM, ROUTE_LANES), lambda i: (jnp.maximum(i - 1, 0), 0)),
            pl.BlockSpec((1, ROUTE_LANES), const2),
        ),
        out_shape=(
            jax.ShapeDtypeStruct((T_ALL, D_MODEL), F32),
            jax.ShapeDtypeStruct((T_ALL, D_PACK), U32),
            jax.ShapeDtypeStruct((T_ALL, ROUTE_LANES), F32),
            jax.ShapeDtypeStruct((1, ROUTE_LANES), F32),
        ),
        scratch_shapes=[pltpu.VMEM((1, ROUTE_LANES), F32), pltpu.VMEM((PM_TM, ROUTE_LANES), F32)],
        compiler_params=_cparams(1),
        name="postmix",
    )(y_act, gates, gates, y_pool, xp, xs, wa, wb, wo, g_ffn, wr_cat, b_r)


SC_CH = 96


def _sc_workers():
    info = plsc.get_sparse_core_info()
    return info.num_cores, info.num_cores * info.num_subcores


def _sc_dispatch(tn, slots):
    n_cores, n_workers = _sc_workers()
    per_w = (T_ALL // SC_CH) // n_workers
    assert per_w * n_workers * SC_CH == T_ALL
    slots = slots.reshape(2, n_workers, per_w, SC_CH)

    @functools.partial(
        pl.kernel,
        mesh=plsc.VectorSubcoreMesh(core_axis_name="c", subcore_axis_name="s"),
        out_type=jax.ShapeDtypeStruct((N_SLOTS, D_PACK), U32),
        scratch_types=[pltpu.VMEM((2, per_w, SC_CH), I32), pltpu.VMEM((SC_CH, D_PACK), U32)],
    )
    def k(tn_hbm, slots_hbm, xs_hbm, idx_v, rows_v):
        wid = lax.axis_index("s") * n_cores + lax.axis_index("c")
        c0 = wid * per_w
        pltpu.sync_copy(slots_hbm.at[0, wid], idx_v.at[0])
        pltpu.sync_copy(slots_hbm.at[1, wid], idx_v.at[1])

        @pl.loop(0, per_w)
        def _(c):
            row0 = pl.multiple_of((c0 + c) * SC_CH, SC_CH)
            pltpu.sync_copy(tn_hbm.at[pl.ds(row0, SC_CH)], rows_v)
            pltpu.sync_copy(rows_v, xs_hbm.at[idx_v.at[0, c]])
            pltpu.sync_copy(rows_v, xs_hbm.at[idx_v.at[1, c]])

    return k(tn, slots)


def _sc_collect(ys, slots):
    n_cores, n_workers = _sc_workers()
    per_w = (N_ASSIGN // SC_CH) // n_workers
    assert per_w * n_workers * SC_CH == N_ASSIGN
    slots = slots.reshape(n_workers, per_w, SC_CH)

    @functools.partial(
        pl.kernel,
        mesh=plsc.VectorSubcoreMesh(core_axis_name="c", subcore_axis_name="s"),
        out_type=jax.ShapeDtypeStruct((N_ASSIGN, D_PACK), U32),
        scratch_types=[pltpu.VMEM((per_w, SC_CH), I32), pltpu.VMEM((SC_CH, D_PACK), U32)],
    )
    def k(ys_hbm, slots_hbm, out_hbm, idx_v, rows_v):
        wid = lax.axis_index("s") * n_cores + lax.axis_index("c")
        c0 = wid * per_w
        pltpu.sync_copy(slots_hbm.at[wid], idx_v)

        @pl.loop(0, per_w)
        def _(c):
            row0 = pl.multiple_of((c0 + c) * SC_CH, SC_CH)
            pltpu.sync_copy(ys_hbm.at[idx_v.at[c]], rows_v)
            pltpu.sync_copy(rows_v, out_hbm.at[pl.ds(row0, SC_CH)])

    return k(ys, slots)


W_PARTS = 2


def _expert_body(t0_ref, nt_ref, xs_hbm, wg_hbm, wu_hbm, wd_hbm, ys_hbm,
                 wg_ref, wu_ref, wd_ref, xb_ref, yb_ref, wgb_ref, wub_ref, wdb_ref, wsem, xsem, ysem):
    e = pl.program_id(0)
    n = nt_ref[e]
    g0 = t0_ref[e]
    ws = e & 1

    def w_copies(ex, slot):
        out = []
        for hbm, buf in ((wg_hbm, wg_ref), (wu_hbm, wu_ref), (wd_hbm, wd_ref)):
            rb = buf.shape[1] // W_PARTS
            for p in range(W_PARTS):
                out.append((pltpu.make_async_copy(hbm.at[ex, pl.ds(p * rb, rb)],
                                                  buf.at[slot, pl.ds(p * rb, rb)], wsem.at[slot]), p))
        return out

    @pl.when(e == 0)
    def _():
        for cp, p in w_copies(0, 0):
            cp.start(priority=p)

    @pl.when(e + 1 < MOE_EXPERTS)
    def _():
        for cp, p in w_copies(e + 1, 1 - ws):
            cp.start(priority=p)

    for cp, _ in w_copies(e, ws):
        cp.wait()

    def rows(j):
        return pl.ds(pl.multiple_of((g0 + j) * TME, TME), TME)

    def x_copy(j, s):
        return pltpu.make_async_copy(xs_hbm.at[rows(j)], xb_ref.at[s], xsem.at[s])

    def y_copy(j, s):
        return pltpu.make_async_copy(yb_ref.at[s], ys_hbm.at[rows(j)], ysem.at[s])

    @pl.when(n > 0)
    def _():
        x_copy(0, 0).start()
        wgb_ref[...] = wg_ref[ws].astype(BF16)
        wub_ref[...] = wu_ref[ws].astype(BF16)
        wdb_ref[...] = wd_ref[ws].astype(BF16)

        def tile(j, c):
            s = j & 1
            x_copy(j, s).wait()
            @pl.when(j + 1 < n)
            def _():
                x_copy(j + 1, 1 - s).start()
            @pl.when(j >= 2)
            def _():
                y_copy(j - 2, s).wait()
            x = _unpack_pairs(xb_ref[s], BF16)
            hg = jnp.dot(x, wgb_ref[...], preferred_element_type=F32)
            hu = jnp.dot(x, wub_ref[...], preferred_element_type=F32)
            act = (hg * _sigmoid(hg)) * hu
            yb_ref[s] = _pack_pairs(jnp.dot(act.astype(BF16), wdb_ref[...], preferred_element_type=F32))
            y_copy(j, s).start()
            return c
        lax.fori_loop(0, n, tile, 0)

        @pl.when(n >= 2)
        def _():
            y_copy(n - 2, n & 1).wait()
        y_copy(n - 1, (n - 1) & 1).wait()


def _experts(tile0, tiles, xs, w_eg, w_eu, w_ed):
    any_spec = pl.BlockSpec(memory_space=pl.ANY)
    grid_spec = pltpu.PrefetchScalarGridSpec(
        num_scalar_prefetch=2,
        grid=(MOE_EXPERTS,),
        in_specs=[any_spec] * 4,
        out_specs=any_spec,
        scratch_shapes=[
            pltpu.VMEM((2, D_MODEL, MOE_FF), F32),
            pltpu.VMEM((2, D_MODEL, MOE_FF), F32),
            pltpu.VMEM((2, MOE_FF, D_MODEL), F32),
            pltpu.VMEM((2, TME, D_PACK), U32),
            pltpu.VMEM((2, TME, D_PACK), U32),
            pltpu.VMEM((D_MODEL, MOE_FF), BF16),
            pltpu.VMEM((D_MODEL, MOE_FF), BF16),
            pltpu.VMEM((MOE_FF, D_MODEL), BF16),
            pltpu.SemaphoreType.DMA((2,)),
            pltpu.SemaphoreType.DMA((2,)),
            pltpu.SemaphoreType.DMA((2,)),
        ],
    )
    return pl.pallas_call(
        _expert_body,
        grid_spec=grid_spec,
        out_shape=jax.ShapeDtypeStruct((N_SLOTS, D_PACK), U32),
        compiler_params=_cparams(1),
        name="experts",
    )(tile0, tiles, xs, w_eg, w_eu, w_ed)


FN_TM = 256
FN_PROMPT_BLOCKS = T_PROMPT // FN_TM


def _final_body(h_ref, y0_ref, y1_ref, rt_ref, g_ref, op_ref, os_ref):
    i = pl.program_id(0)
    rt = rt_ref[...]
    z = (h_ref[...] + rt[:, 0:1] * _unpack_pairs(y0_ref[...], F32)
         + rt[:, 1:2] * _unpack_pairs(y1_ref[...], F32))
    inv = lax.rsqrt(jnp.mean(z * z, axis=-1, keepdims=True) + EPS)
    out = (z * inv) * g_ref[...]
    @pl.when(i < FN_PROMPT_BLOCKS)
    def _():
        op_ref[...] = out
    @pl.when(i >= FN_PROMPT_BLOCKS)
    def _():
        os_ref[...] = out


def _final(h, y, route, g_final):
    n = T_ALL // FN_TM
    npb = FN_PROMPT_BLOCKS
    yoff = T_ALL // FN_TM
    return pl.pallas_call(
        _final_body,
        grid=(n,),
        in_specs=[
            pl.BlockSpec((FN_TM, D_MODEL), lambda i: (i, 0)),
            pl.BlockSpec((FN_TM, D_PACK), lambda i: (i, 0)),
            pl.BlockSpec((FN_TM, D_PACK), lambda i: (yoff + i, 0)),
            pl.BlockSpec((FN_TM, ROUTE_LANES), lambda i: (i, 0)),
            pl.BlockSpec((1, D_MODEL), lambda i: (0, 0)),
        ],
        out_specs=(
            pl.BlockSpec((FN_TM, D_MODEL), lambda i: (jnp.minimum(i, npb - 1), 0)),
            pl.BlockSpec((FN_TM, D_MODEL), lambda i: (jnp.maximum(i - npb, 0), 0)),
        ),
        out_shape=(
            jax.ShapeDtypeStruct((T_PROMPT, D_MODEL), F32),
            jax.ShapeDtypeStruct((T_SAMPLE, D_MODEL), F32),
        ),
        compiler_params=_cparams(1),
        name="final",
    )(h, y, y, route, g_final)


def _dispatch_plan(route, cnt):
    counts = cnt[0, EXP_LANE0:EXP_LANE0 + MOE_EXPERTS].astype(I32)
    tiles = (counts + (TME - 1)) // TME
    cumt = jnp.cumsum(tiles)
    pad_off = (cumt - tiles) * TME
    rank = route[:, 2:4].astype(I32)
    eid = route[:, 4:6].astype(I32)
    onehot = eid[..., None] == jnp.arange(MOE_EXPERTS, dtype=I32)
    slots = (rank + jnp.sum(jnp.where(onehot, pad_off, 0), axis=-1)).T
    return slots, cumt - tiles, tiles


def kernel(x_prompt, x_sample, state_pool, state_ssm_re, state_ssm_im, g_mix, w_in, w_pool,
           pool_scale, ssm_a_re, ssm_a_im, ssm_log_dt, ssm_b_re, ssm_b_im, ssm_c_re, ssm_c_im,
           ssm_d, w_glu_a, w_glu_b, w_out, g_ffn, w_router_group, b_router_group,
           w_router_expert, b_router_expert, w_exp_gate, w_exp_up, w_exp_down, g_final):
    l = 0
    xp = x_prompt.reshape(T_PROMPT, D_MODEL)
    xs = x_sample.transpose(1, 0, 2).reshape(T_SAMPLE, D_MODEL)
    w_in_bf = w_in[l].astype(BF16)
    w_pool_bf = w_pool[l].astype(BF16)
    wa_bf = w_glu_a[l].astype(BF16)
    wb_bf = w_glu_b[l].astype(BF16)
    wo_bf = w_out[l].astype(BF16)
    g_mix2 = g_mix[l].reshape(1, D_MODEL)
    scale2 = pool_scale[l].reshape(1, D_MODEL)

    u, gates = _inproj(xp, g_mix2, w_in_bf, 0)
    u, gates = _inproj(xs, g_mix2, w_in_bf, T_PROMPT // IN_TM, dst=(u, gates))

    y_pool, pool_tail = _pool_prompt(u, w_pool_bf, scale2)
    hist_tm = state_pool[l].transpose(1, 0, 2)
    y_pool, pool_buf_tm = _pool_sample(u, hist_tm, w_pool_bf, scale2, y_pool)

    tables = _ssm_tables(ssm_a_re[l], ssm_a_im[l], ssm_log_dt[l], ssm_b_re[l], ssm_b_im[l],
                         ssm_c_re[l], ssm_c_im[l], ssm_d[l])
    h0r = state_ssm_re[l].reshape(DEC_BATCH, SSM_GROUPS * SSM_STATE)
    h0i = state_ssm_im[l].reshape(DEC_BATCH, SSM_GROUPS * SSM_STATE)
    y_act, h_prompt, hs_re, hs_im = _ssm(u, h0r, h0i, tables)

    w_r = jnp.zeros((D_MODEL, ROUTE_LANES), F32)
    w_r = w_r.at[:, :MOE_GROUPS].set(w_router_group[l])
    w_r = w_r.at[:, EXP_LANE0:EXP_LANE0 + MOE_EXPERTS].set(w_router_expert[l])
    wr_hi = w_r.astype(BF16)
    wr_cat = jnp.concatenate([wr_hi, (w_r - wr_hi.astype(F32)).astype(BF16)], axis=1)
    b_r = jnp.zeros((1, ROUTE_LANES), F32)
    b_r = b_r.at[0, :MOE_GROUPS].set(b_router_group[l])
    b_r = b_r.at[0, EXP_LANE0:EXP_LANE0 + MOE_EXPERTS].set(b_router_expert[l])

    h, tn, route, cnt = _postmix(y_act, gates, y_pool, xp, xs, wa_bf, wb_bf, wo_bf,
                                 g_ffn[l].reshape(1, D_MODEL), wr_cat, b_r)
    slots, tile0, tiles = _dispatch_plan(route, cnt)
    xs_sorted = _sc_dispatch(tn, slots)
    ys_sorted = _experts(tile0, tiles, xs_sorted, w_exp_gate[l], w_exp_up[l], w_exp_down[l])
    y = _sc_collect(ys_sorted, slots)
    yp, ys = _final(h, y, route, g_final.reshape(1, D_MODEL))

    y_prompt = yp.reshape(BATCH, SEQ, D_MODEL)
    y_sample = ys.reshape(DEC_SEQ, DEC_BATCH, D_MODEL).transpose(1, 0, 2)
    new_pool_prompt = pool_tail[:, HIST - POOL_BUF:, :][None]
    new_pool_sample = pool_buf_tm.transpose(1, 0, 2)[None]
    hp = h_prompt.reshape(BATCH, N_OCT, 2, OCT_GROUPS, SSM_STATE).transpose(2, 0, 1, 3, 4)
    hp = hp.reshape(2, BATCH, SSM_GROUPS, SSM_STATE)
    shp = (1, DEC_BATCH, SSM_GROUPS, SSM_STATE)
    return (y_prompt, y_sample, new_pool_prompt, hp[0][None], hp[1][None], new_pool_sample,
            hs_re.reshape(shp), hs_im.reshape(shp))
```

```python
import functools
import math

import jax
import jax.numpy as jnp
from jax import lax
from jax.experimental import pallas as pl
from jax.experimental.pallas import tpu as pltpu
from jax.experimental.pallas import tpu_sc as plsc

F32 = jnp.float32
BF16 = jnp.bfloat16
I32 = jnp.int32
U32 = jnp.uint32

D_MODEL = 2048
BATCH = 4
SEQ = 2048
DEC_BATCH = 128
DEC_SEQ = 8
PAST_LEN = 16384
POOL_WIDTH = D_MODEL // 2
POOL_WINDOWS = (2, 4, 8, 16)
POOL_GROUPS = len(POOL_WINDOWS)
POOL_GROUP_CH = POOL_WIDTH // POOL_GROUPS
POOL_OUT_CH = D_MODEL // POOL_GROUPS
POOL_BUF = max(POOL_WINDOWS) - 1
SSM_WIDTH = D_MODEL // 2
SSM_GROUP_CH = 16
SSM_GROUPS = SSM_WIDTH // SSM_GROUP_CH
SSM_STATE = 64
IN_WIDTH = POOL_WIDTH + SSM_WIDTH + 2 * D_MODEL
D_PACK = D_MODEL // 2
MOE_GROUPS = 4
MOE_EPG = 8
MOE_EXPERTS = MOE_GROUPS * MOE_EPG
MOE_FF = D_MODEL // 4
EPS = 1e-6

T_PROMPT = BATCH * SEQ
T_SAMPLE = DEC_BATCH * DEC_SEQ
T_ALL = T_PROMPT + T_SAMPLE
T_PAD = (BATCH + 1) * SEQ

LANES = 128
SUBLANES = 8
VMEM_LIMIT = 56 * 1024 * 1024

CHUNK = 8
OCT = LANES
N_OCT = SSM_WIDTH // OCT
OCT_GROUPS = OCT // SSM_GROUP_CH
OCT_STATES = OCT_GROUPS * SSM_STATE
CW = CHUNK * OCT
SW = 2 * OCT_STATES

ROUTE_LANES = LANES
EXP_LANE0 = MOE_GROUPS
N_ASSIGN = 2 * T_ALL
TME = 256
N_ITEMS_MAX = N_ASSIGN // TME + MOE_EXPERTS
N_SLOTS = N_ITEMS_MAX * TME


def _cparams(n_axes):
    return pltpu.CompilerParams(dimension_semantics=("arbitrary",) * n_axes,
                                vmem_limit_bytes=VMEM_LIMIT)


def _sigmoid(x):
    return 1.0 / (1.0 + jnp.exp(-x))


def _pack_pairs(x):
    c = x.shape[1] // 2
    hi = lax.bitcast_convert_type(x[:, :c].astype(BF16).astype(F32), U32)
    lo = lax.bitcast_convert_type(x[:, c:].astype(BF16).astype(F32), U32)
    return hi | (lo >> 16)


def _unpack_pairs(u, dtype):
    hi = lax.bitcast_convert_type(u & jnp.uint32(0xFFFF0000), F32)
    lo = lax.bitcast_convert_type(u << 16, F32)
    return jnp.concatenate([hi, lo], axis=1).astype(dtype)


def _gelu_tanh(x):
    c = math.sqrt(2.0 / math.pi)
    return 0.5 * x * (1.0 + jnp.tanh(c * (x + 0.044715 * (x * x * x))))


IN_TM = 1024
IN_TN = 1024
U_WIDTH = POOL_WIDTH + SSM_WIDTH
GATE_WIDTH = 2 * D_MODEL
IN_U_STEPS = U_WIDTH // IN_TN


def _inproj_body(x_ref, g_ref, w_ref, *rest):
    u_ref, gate_ref, xn_ref = rest[-3:]
    j = pl.program_id(1)
    @pl.when(j == 0)
    def _():
        x = x_ref[...]
        inv = lax.rsqrt(jnp.mean(x * x, axis=-1, keepdims=True) + EPS)
        xn_ref[...] = ((x * inv) * g_ref[...]).astype(BF16)
    acc = jnp.dot(xn_ref[...], w_ref[...], preferred_element_type=F32)
    @pl.when(j < IN_U_STEPS)
    def _():
        u_ref[...] = acc
    @pl.when(j >= IN_U_STEPS)
    def _():
        gate_ref[...] = _sigmoid(acc).astype(BF16)


def _inproj(x, g, w_bf, row_block0, dst=None):
    n_i = x.shape[0] // IN_TM
    in_specs = [
        pl.BlockSpec((IN_TM, D_MODEL), lambda i, j: (i, 0)),
        pl.BlockSpec((1, D_MODEL), lambda i, j: (0, 0)),
        pl.BlockSpec((D_MODEL, IN_TN), lambda i, j: (0, j)),
    ]
    args = [x, g, w_bf]
    aliases = {}
    if dst is not None:
        in_specs += [pl.BlockSpec(memory_space=pl.ANY)] * 2
        args += list(dst)
        aliases = {3: 0, 4: 1}
    return pl.pallas_call(
        _inproj_body,
        grid=(n_i, IN_WIDTH // IN_TN),
        in_specs=in_specs,
        out_specs=(
            pl.BlockSpec((IN_TM, IN_TN), lambda i, j: (i + row_block0, jnp.minimum(j, IN_U_STEPS - 1))),
            pl.BlockSpec((IN_TM, IN_TN), lambda i, j: (i + row_block0, jnp.maximum(j - IN_U_STEPS, 0))),
        ),
        out_shape=(
            jax.ShapeDtypeStruct((T_PAD, U_WIDTH), F32),
            jax.ShapeDtypeStruct((T_ALL, GATE_WIDTH), BF16),
        ),
        scratch_shapes=[pltpu.VMEM((IN_TM, D_MODEL), BF16)],
        input_output_aliases=aliases,
        compiler_params=_cparams(2),
        name="inproj",
    )(*args)


PP_TM = 512
HIST = 16


def _pool_project(pooled_g, g, w_ref, sc_ref, o_ref):
    y = jnp.dot(pooled_g.astype(BF16), w_ref[g], preferred_element_type=F32)
    lo, hi = g * POOL_OUT_CH, (g + 1) * POOL_OUT_CH
    o_ref[:, lo:hi] = (y * sc_ref[:, lo:hi]).astype(o_ref.dtype)


def _pool_prompt_body(u_ref, w_ref, sc_ref, o_ref, tail_ref, hist_ref):
    i = pl.program_id(1)
    @pl.when(i == 0)
    def _():
        hist_ref[...] = jnp.zeros_like(hist_ref)
    u = u_ref[...]
    ext = jnp.concatenate([hist_ref[...], u], axis=0)
    hist_ref[...] = u[PP_TM - HIST:, :]
    tail_ref[...] = u[PP_TM - HIST:, :]
    pos = i * PP_TM + lax.broadcasted_iota(I32, (PP_TM, 1), 0)
    for g, w in enumerate(POOL_WINDOWS):
        lo, hi = g * POOL_GROUP_CH, (g + 1) * POOL_GROUP_CH
        s = ext[:, lo:hi]
        d = 1
        while d < w:
            s = s + pltpu.roll(s, d, axis=0)
            d *= 2
        cnt = jnp.minimum(w, pos + 1).astype(F32)
        pooled = s[HIST:, :] / cnt - u[:, lo:hi]
        _pool_project(pooled, g, w_ref, sc_ref, o_ref)


def _pool_prompt(u, w_pool_bf, pool_scale):
    n_i = SEQ // PP_TM
    return pl.pallas_call(
        _pool_prompt_body,
        grid=(BATCH, n_i),
        in_specs=[
            pl.BlockSpec((PP_TM, POOL_WIDTH), lambda b, i: (b * n_i + i, 0)),
            pl.BlockSpec((POOL_GROUPS, POOL_GROUP_CH, POOL_OUT_CH), lambda b, i: (0, 0, 0)),
            pl.BlockSpec((1, D_MODEL), lambda b, i: (0, 0)),
        ],
        out_specs=(
            pl.BlockSpec((PP_TM, D_MODEL), lambda b, i: (b * n_i + i, 0)),
            pl.BlockSpec((None, HIST, POOL_WIDTH), lambda b, i: (b, 0, 0)),
        ),
        out_shape=(
            jax.ShapeDtypeStruct((T_ALL, D_MODEL), BF16),
            jax.ShapeDtypeStruct((BATCH, HIST, POOL_WIDTH), F32),
        ),
        scratch_shapes=[pltpu.VMEM((HIST, POOL_WIDTH), F32)],
        compiler_params=_cparams(2),
        name="pool_prompt",
    )(u, w_pool_bf, pool_scale)


def _pool_sample_body(u_ref, hist_ref, w_ref, sc_ref, _dst, o_ref, buf_ref):
    rows = [hist_ref[k] for k in range(POOL_BUF)]
    rows += [u_ref[DEC_BATCH * t:DEC_BATCH * (t + 1), :] for t in range(DEC_SEQ)]
    n = len(rows)
    for k in range(POOL_BUF):
        buf_ref[k] = rows[n - POOL_BUF + k]
    for g, w in enumerate(POOL_WINDOWS):
        lo, hi = g * POOL_GROUP_CH, (g + 1) * POOL_GROUP_CH
        f = [r[:, lo:hi] for r in rows]
        cur = f
        d = 1
        while d < w:
            cur = [cur[k] + cur[k - d] if k - d >= 0 else cur[k] for k in range(n)]
            d *= 2
        pooled = jnp.concatenate(
            [cur[POOL_BUF + t] / float(w) - f[POOL_BUF + t] for t in range(DEC_SEQ)], axis=0)
        _pool_project(pooled, g, w_ref, sc_ref, o_ref)


def _pool_sample(u, hist_tm, w_pool_bf, pool_scale, y_pool):
    blk = T_PROMPT // T_SAMPLE
    return pl.pallas_call(
        _pool_sample_body,
        grid=(1,),
        in_specs=[
            pl.BlockSpec((T_SAMPLE, POOL_WIDTH), lambda i: (blk, 0)),
            pl.BlockSpec((POOL_BUF, DEC_BATCH, POOL_WIDTH), lambda i: (0, 0, 0)),
            pl.BlockSpec((POOL_GROUPS, POOL_GROUP_CH, POOL_OUT_CH), lambda i: (0, 0, 0)),
            pl.BlockSpec((1, D_MODEL), lambda i: (0, 0)),
            pl.BlockSpec(memory_space=pl.ANY),
        ],
        out_specs=(
            pl.BlockSpec((T_SAMPLE, D_MODEL), lambda i: (blk, 0)),
            pl.BlockSpec((POOL_BUF, DEC_BATCH, POOL_WIDTH), lambda i: (0, 0, 0)),
        ),
        out_shape=(
            jax.ShapeDtypeStruct((T_ALL, D_MODEL), BF16),
            jax.ShapeDtypeStruct((POOL_BUF, DEC_BATCH, POOL_WIDTH), F32),
        ),
        input_output_aliases={4: 0},
        compiler_params=_cparams(1),
        name="pool_sample",
    )(u, hist_tm, w_pool_bf, pool_scale, y_pool)


def _ssm_tables(a_re, a_im, log_dt, b_re, b_im, c_re, c_im, d_skip):
    dt = jnp.exp(log_dt)[:, None]
    lr, li = a_re, a_im
    ab_re = jnp.exp(lr * dt) * jnp.cos(li * dt)
    ab_im = jnp.exp(lr * dt) * jnp.sin(li * dt)
    den = lr * lr + li * li
    nr, ni = ab_re - 1.0, ab_im
    q_re = (nr * lr + ni * li) / den
    q_im = (ni * lr - nr * li) / den
    bb_re = q_re[..., None] * b_re - q_im[..., None] * b_im
    bb_im = q_re[..., None] * b_im + q_im[..., None] * b_re

    def lam_rows(ks):
        k = jnp.asarray(ks, F32)[:, None, None]
        m = jnp.exp(k * lr * dt)
        re = (m * jnp.cos(k * li * dt)).reshape(len(ks), N_OCT, OCT_STATES)
        im = (m * jnp.sin(k * li * dt)).reshape(len(ks), N_OCT, OCT_STATES)
        return jnp.concatenate([re, im], axis=-1).transpose(1, 0, 2)

    def compact(re, im):
        v = jnp.concatenate([re, im], axis=-1)
        return v.reshape(N_OCT, OCT, 2 * SSM_STATE)

    bbc = compact(jnp.swapaxes(bb_re, 1, 2), jnp.swapaxes(bb_im, 1, 2))
    ccc = compact(c_re, c_im)
    pw = lam_rows(list(range(2 * SUBLANES)))
    r = jnp.arange(SUBLANES)[None, :, None]
    parts = [jnp.where(r >= dd, lam_rows([CHUNK * dd]), 0.0) for dd in (1, 2, 4)]
    parts.append(lam_rows([CHUNK * kk for kk in range(1, SUBLANES + 1)]))
    tab = jnp.concatenate(parts, axis=1)
    dsk = d_skip.reshape(N_OCT, 1, OCT)
    return bbc, ccc, pw, tab, dsk


def _split_bf16(x):
    hi = x.astype(BF16)
    return hi, (x - hi.astype(F32)).astype(BF16)


def _dot_nt(a, b):
    return lax.dot_general(a, b, (((1,), (1,)), ((), ())), preferred_element_type=F32)


def _build_weights(bbc_ref, ccc_ref, pw_ref, f_ref, gt_ref, m_ref):
    row_gi = lax.broadcasted_iota(I32, (OCT, 1), 0) >> 4
    col = lax.broadcasted_iota(I32, (1, SW), 1)
    col_gi = (col >> 6) & 7
    src = ((col >> 9) << 6) | (col & 63)
    k128 = lax.broadcasted_iota(I32, (2 * SSM_STATE, 1), 0)
    spread = jnp.where(k128 == src, 1.0, 0.0).astype(BF16)
    diag = row_gi == col_gi

    def expand(c_ref):
        hi, lo = _split_bf16(c_ref[...])
        d = (jnp.dot(hi, spread, preferred_element_type=F32)
             + jnp.dot(lo, spread, preferred_element_type=F32))
        d = jnp.where(diag, d, 0.0)
        return d[:, :OCT_STATES], d[:, OCT_STATES:]

    br, bi = expand(bbc_ref)
    cr, ci = expand(ccc_ref)
    chi_r, clo_r = _split_bf16(cr)
    chi_i, clo_i = _split_bf16(ci)

    def lam(k):
        return pw_ref[k:k + 1, :OCT_STATES], pw_ref[k:k + 1, OCT_STATES:]

    def dot3(a, bhi, blo):
        ahi, alo = _split_bf16(a)
        return _dot_nt(ahi, bhi) + _dot_nt(alo, bhi) + _dot_nt(ahi, blo)

    lags = []
    for k in range(CHUNK):
        pr, pi_ = lam(k)
        fr, fi = _cmul(br, bi, pr, pi_)
        s = CHUNK - 1 - k
        f_ref[s * OCT:(s + 1) * OCT, :] = jnp.concatenate([fr, fi], axis=1).astype(BF16)
        lags.append((dot3(fr, chi_r, clo_r) - dot3(fi, chi_i, clo_i)).astype(BF16))
        pr, pi_ = lam(k + 1)
        gr, gi = _cmul(cr, ci, pr, pi_)
        gt_ref[k * OCT:(k + 1) * OCT, :] = jnp.concatenate([gr, -gi], axis=1).astype(BF16)
    zero = jnp.zeros((OCT, OCT), BF16)
    for s in range(CHUNK):
        for t in range(CHUNK):
            m_ref[s * OCT:(s + 1) * OCT, t * OCT:(t + 1) * OCT] = lags[t - s] if t >= s else zero


def _cmul(ar, ai, br, bi):
    return ar * br - ai * bi, ar * bi + ai * br


def _chunk_scan(sloc, tab_ref):
    R = sloc.shape[0]
    nb = R // SUBLANES
    sr, si = sloc[:, :OCT_STATES], sloc[:, OCT_STATES:]
    rowi = lax.broadcasted_iota(I32, (R, 1), 0)
    tr = jnp.where(rowi == 0, 0.0, pltpu.roll(sr, 1, axis=0))
    ti = jnp.where(rowi == 0, 0.0, pltpu.roll(si, 1, axis=0))
    for lvl, d in enumerate((1, 2, 4)):
        mr = tab_ref[lvl * SUBLANES:(lvl + 1) * SUBLANES, :OCT_STATES]
        mi = tab_ref[lvl * SUBLANES:(lvl + 1) * SUBLANES, OCT_STATES:]
        mr = jnp.concatenate([mr] * nb, axis=0)
        mi = jnp.concatenate([mi] * nb, axis=0)
        pr, pi_ = _cmul(mr, mi, pltpu.roll(tr, d, axis=0), pltpu.roll(ti, d, axis=0))
        tr, ti = tr + pr, ti + pi_
    pwr = tab_ref[3 * SUBLANES:4 * SUBLANES, :OCT_STATES]
    pwi = tab_ref[3 * SUBLANES:4 * SUBLANES, OCT_STATES:]
    cr = jnp.zeros((1, OCT_STATES), F32)
    ci = jnp.zeros((1, OCT_STATES), F32)
    out_r, out_i = [], []
    for k in range(nb):
        ar = tr[k * SUBLANES:(k + 1) * SUBLANES, :]
        ai = ti[k * SUBLANES:(k + 1) * SUBLANES, :]
        pr, pi_ = _cmul(pwr, pwi, jnp.broadcast_to(cr, ar.shape), jnp.broadcast_to(ci, ai.shape))
        hr, hi = ar + pr, ai + pi_
        out_r.append(hr)
        out_i.append(hi)
        cr, ci = hr[SUBLANES - 1:, :], hi[SUBLANES - 1:, :]
    hin = jnp.concatenate([jnp.concatenate(out_r, axis=0), jnp.concatenate(out_i, axis=0)], axis=1)
    lr, li = pwr[0:1, :], pwi[0:1, :]
    fr, fi = _cmul(lr, li, cr, ci)
    fin = jnp.concatenate([fr + sr[R - 1:, :], fi + si[R - 1:, :]], axis=1)
    return hin, fin


def _ssm_body(u_ref, h0r_ref, h0i_ref, bbc_ref, ccc_ref, pw_ref, tab_ref, d_ref,
              y_ref, hout_ref, hr_ref, hi_ref, f_ref, gt_ref, m_ref):
    b = pl.program_id(1)

    @pl.when(b == 0)
    def _():
        _build_weights(bbc_ref, ccc_ref, pw_ref, f_ref, gt_ref, m_ref)

    def outputs(xs, xb, hin_bf):
        y = (jnp.dot(xb, m_ref[...], preferred_element_type=F32) + _dot_nt(hin_bf, gt_ref[...]))
        return [_gelu_tanh(y[:, t * OCT:(t + 1) * OCT] + d_ref[...] * xs[t]) for t in range(CHUNK)]

    @pl.when(b < BATCH)
    def _():
        R = SEQ // CHUNK
        xs = [u_ref[pl.ds(s, R, stride=CHUNK), :] for s in range(CHUNK)]
        xb = jnp.concatenate(xs, axis=1).astype(BF16)
        sloc = jnp.dot(xb, f_ref[...], preferred_element_type=F32)
        hin, fin = _chunk_scan(sloc, tab_ref)
        for t, yt in enumerate(outputs(xs, xb, hin.astype(BF16))):
            y_ref[pl.ds(t, R, stride=CHUNK), :] = yt
        hout_ref[...] = fin

    @pl.when(b == BATCH)
    def _():
        B = DEC_BATCH
        xs = [u_ref[B * s:B * (s + 1), :] for s in range(CHUNK)]
        xb = jnp.concatenate(xs, axis=1).astype(BF16)
        sloc = jnp.dot(xb, f_ref[...], preferred_element_type=F32)
        h0r, h0i = h0r_ref[...], h0i_ref[...]
        hin = jnp.concatenate([h0r, h0i], axis=1).astype(BF16)
        for t, yt in enumerate(outputs(xs, xb, hin)):
            y_ref[B * t:B * (t + 1), :] = yt
        lr = tab_ref[3 * SUBLANES:3 * SUBLANES + 1, :OCT_STATES]
        li = tab_ref[3 * SUBLANES:3 * SUBLANES + 1, OCT_STATES:]
        nr, ni = _cmul(lr, li, h0r, h0i)
        hr_ref[...] = nr + sloc[:, :OCT_STATES]
        hi_ref[...] = ni + sloc[:, OCT_STATES:]


def _ssm(u, h0r, h0i, tables):
    col0 = POOL_WIDTH // OCT
    im3 = lambda o, b: (o, 0, 0)
    st_spec = pl.BlockSpec((DEC_BATCH, OCT_STATES), lambda o, b: (0, o))
    return pl.pallas_call(
        _ssm_body,
        grid=(N_OCT, BATCH + 1),
        in_specs=[
            pl.BlockSpec((SEQ, OCT), lambda o, b: (b, col0 + o)), st_spec, st_spec,
            pl.BlockSpec((None, OCT, 2 * SSM_STATE), im3),
            pl.BlockSpec((None, OCT, 2 * SSM_STATE), im3),
            pl.BlockSpec((None, 2 * SUBLANES, SW), im3),
            pl.BlockSpec((None, 4 * SUBLANES, SW), im3),
            pl.BlockSpec((None, 1, OCT), im3),
        ],
        out_specs=(
            pl.BlockSpec((SEQ, OCT), lambda o, b: (b, o)),
            pl.BlockSpec((None, 1, SW), lambda o, b: (jnp.minimum(b, BATCH - 1) * N_OCT + o, 0, 0)),
            st_spec, st_spec,
        ),
        out_shape=(
            jax.ShapeDtypeStruct((T_PAD, SSM_WIDTH), F32),
            jax.ShapeDtypeStruct((BATCH * N_OCT, 1, SW), F32),
            jax.ShapeDtypeStruct((DEC_BATCH, SSM_GROUPS * SSM_STATE), F32),
            jax.ShapeDtypeStruct((DEC_BATCH, SSM_GROUPS * SSM_STATE), F32),
        ),
        scratch_shapes=[pltpu.VMEM((CW, SW), BF16), pltpu.VMEM((CW, SW), BF16), pltpu.VMEM((CW, CW), BF16)],
        compiler_params=_cparams(2),
        name="ssm",
    )(u, h0r, h0i, *tables)


PM_TM = 256
PM_PROMPT_BLOCKS = T_PROMPT // PM_TM
PM_STEPS = T_ALL // PM_TM


def _route(logits, valid, cnt_ref):
    lane = lax.broadcasted_iota(I32, (PM_TM, ROUTE_LANES), 1)
    neg = jnp.float32(-jnp.inf)
    big = jnp.int32(1 << 20)
    is_g = lane < MOE_GROUPS
    gmax = jnp.max(jnp.where(is_g, logits, neg), axis=1, keepdims=True)
    g_idx = jnp.min(jnp.where(is_g & (logits == gmax), lane, big), axis=1, keepdims=True)
    g_den = jnp.sum(jnp.where(is_g, jnp.exp(logits - gmax), 0.0), axis=1, keepdims=True)
    g_val = 1.0 / g_den
    e_lane = lane - EXP_LANE0
    sel = (e_lane >= 0) & (e_lane < MOE_EXPERTS) & ((e_lane >> 3) == g_idx)
    m1 = jnp.max(jnp.where(sel, logits, neg), axis=1, keepdims=True)
    i1 = jnp.min(jnp.where(sel & (logits == m1), lane, big), axis=1, keepdims=True)
    sel2 = sel & (lane != i1)
    m2 = jnp.max(jnp.where(sel2, logits, neg), axis=1, keepdims=True)
    i2 = jnp.min(jnp.where(sel2 & (logits == m2), lane, big), axis=1, keepdims=True)
    e2 = jnp.exp(m2 - m1)
    w1 = g_val / (1.0 + e2)
    w2 = g_val * e2 / (1.0 + e2)
    oh1 = lane == i1
    oh2 = lane == i2
    oh = jnp.where(oh1 | oh2, valid, 0.0)
    rr = lax.broadcasted_iota(I32, (PM_TM, PM_TM), 0)
    cc = lax.broadcasted_iota(I32, (PM_TM, PM_TM), 1)
    tri = jnp.where(cc < rr, 1.0, 0.0).astype(BF16)
    base = cnt_ref[...] + jnp.dot(tri, oh.astype(BF16), preferred_element_type=F32)
    rank1 = jnp.sum(jnp.where(oh1, base, 0.0), axis=1, keepdims=True)
    rank2 = jnp.sum(jnp.where(oh2, base, 0.0), axis=1, keepdims=True)
    cnt_ref[...] = cnt_ref[...] + jnp.sum(oh, axis=0, keepdims=True)
    rt = jnp.where(lane == 0, w1, 0.0)
    rt = jnp.where(lane == 1, w2, rt)
    rt = jnp.where(lane == 2, rank1, rt)
    rt = jnp.where(lane == 3, rank2, rt)
    rt = jnp.where(lane == 4, (i1 - EXP_LANE0).astype(F32), rt)
    rt = jnp.where(lane == 5, (i2 - EXP_LANE0).astype(F32), rt)
    return rt


def _postmix_body(ya_ref, gp_ref, gs_ref, yp_ref, xp_ref, xs_ref, wa_ref, wb_ref, wo_ref,
                  gf_ref, wr_ref, br_ref, h_ref, tn_ref, rt_ref, rtt_ref, cnt_out_ref,
                  cnt_ref, lg_ref):
    i = pl.program_id(0)
    @pl.when(i == 0)
    def _():
        cnt_ref[...] = jnp.zeros_like(cnt_ref)
        lg_ref[...] = jnp.zeros_like(lg_ref)
    prev_logits = lg_ref[...]

    ya = ya_ref[...].astype(BF16)
    a = jnp.dot(ya, wa_ref[...], preferred_element_type=F32)
    bg = jnp.dot(ya, wb_ref[...], preferred_element_type=F32)
    y_ssm = a * _sigmoid(bg)
    merged = (gp_ref[...].astype(F32) * yp_ref[...].astype(F32)
              + gs_ref[...].astype(F32) * y_ssm)
    x = jnp.where(jnp.minimum(i, PM_STEPS - 1) < PM_PROMPT_BLOCKS, xp_ref[...], xs_ref[...])
    h = x + jnp.dot(merged.astype(BF16), wo_ref[...], preferred_element_type=F32)
    h_ref[...] = h
    inv = lax.rsqrt(jnp.mean(h * h, axis=-1, keepdims=True) + EPS)
    tn = (h * inv) * gf_ref[...]
    tn_ref[...] = _pack_pairs(tn)
    t_hi = tn.astype(BF16)
    t_lo = (tn - t_hi.astype(F32)).astype(BF16)
    hh = jnp.dot(t_hi, wr_ref[...], preferred_element_type=F32)
    lh = jnp.dot(t_lo, wr_ref[:, :ROUTE_LANES], preferred_element_type=F32)
    lg_ref[...] = (hh[:, :ROUTE_LANES] + lh + hh[:, ROUTE_LANES:]) + br_ref[...]
    rt = _route(prev_logits, jnp.where(i > 0, 1.0, 0.0), cnt_ref)
    rt_ref[...] = rt
    rtt_ref[...] = rt.T[:8, :]
    cnt_out_ref[...] = cnt_ref[...]


def _postmix(y_act, gates, y_pool, xp, xs, wa, wb, wo, g_ffn, wr_cat, b_r):
    npb = PM_PROMPT_BLOCKS
    const2 = lambda i: (0, 0)
    tile = lambda i: jnp.minimum(i, PM_STEPS - 1)
    return pl.pallas_call(
        _postmix_body,
        grid=(PM_STEPS + 1,),
        in_specs=[
            pl.BlockSpec((PM_TM, SSM_WIDTH), lambda i: (tile(i), 0)),
            pl.BlockSpec((PM_TM, D_MODEL), lambda i: (tile(i), 0)),
            pl.BlockSpec((PM_TM, D_MODEL), lambda i: (tile(i), 1)),
            pl.BlockSpec((PM_TM, D_MODEL), lambda i: (tile(i), 0)),
            pl.BlockSpec((PM_TM, D_MODEL), lambda i: (jnp.minimum(i, npb - 1), 0)),
            pl.BlockSpec((PM_TM, D_MODEL), lambda i: (jnp.maximum(tile(i) - npb, 0), 0)),
            pl.BlockSpec((SSM_WIDTH, D_MODEL), const2, pipeline_mode=pl.Buffered(1)),
            pl.BlockSpec((SSM_WIDTH, D_MODEL), const2, pipeline_mode=pl.Buffered(1)),
            pl.BlockSpec((D_MODEL, D_MODEL), const2, pipeline_mode=pl.Buffered(1)),
            pl.BlockSpec((1, D_MODEL), const2),
            pl.BlockSpec((D_MODEL, 2 * ROUTE_LANES), const2),
            pl.BlockSpec((1, ROUTE_LANES), const2),
        ],
        out_specs=(
            pl.BlockSpec((PM_TM, D_MODEL), lambda i: (tile(i), 0)),
            pl.BlockSpec((PM_TM, D_PACK), lambda i: (tile(i), 0)),
            pl.BlockSpec((PM_TM, ROUTE_LANES), lambda i: (jnp.maximum(i - 1, 0), 0)),
            pl.BlockSpec((8, PM_TM), lambda i: (0, jnp.maximum(i - 1, 0))),
            pl.BlockSpec((1, ROUTE_LANES), const2),
        ),
        out_shape=(
            jax.ShapeDtypeStruct((T_ALL, D_MODEL), F32),
            jax.ShapeDtypeStruct((T_ALL, D_PACK), U32),
            jax.ShapeDtypeStruct((T_ALL, ROUTE_LANES), F32),
            jax.ShapeDtypeStruct((8, T_ALL), F32),
            jax.ShapeDtypeStruct((1, ROUTE_LANES), F32),
        ),
        scratch_shapes=[pltpu.VMEM((1, ROUTE_LANES), F32), pltpu.VMEM((PM_TM, ROUTE_LANES), F32)],
        compiler_params=_cparams(1),
        name="postmix",
    )(y_act, gates, gates, y_pool, xp, xs, wa, wb, wo, g_ffn, wr_cat, b_r)


SC_CH = 96


def _sc_workers():
    info = plsc.get_sparse_core_info()
    return info.num_cores, info.num_cores * info.num_subcores


def _sc_dispatch(tn, slots):
    n_cores, n_workers = _sc_workers()
    per_w = (T_ALL // SC_CH) // n_workers
    assert per_w * n_workers * SC_CH == T_ALL
    slots = slots.reshape(2, n_workers, per_w, SC_CH)

    @functools.partial(
        pl.kernel,
        mesh=plsc.VectorSubcoreMesh(core_axis_name="c", subcore_axis_name="s"),
        out_type=jax.ShapeDtypeStruct((N_SLOTS, D_PACK), U32),
        scratch_types=[pltpu.VMEM((2, per_w, SC_CH), I32), pltpu.VMEM((SC_CH, D_PACK), U32)],
    )
    def k(tn_hbm, slots_hbm, xs_hbm, idx_v, rows_v):
        wid = lax.axis_index("s") * n_cores + lax.axis_index("c")
        c0 = wid * per_w
        pltpu.sync_copy(slots_hbm.at[0, wid], idx_v.at[0])
        pltpu.sync_copy(slots_hbm.at[1, wid], idx_v.at[1])

        @pl.loop(0, per_w)
        def _(c):
            row0 = pl.multiple_of((c0 + c) * SC_CH, SC_CH)
            pltpu.sync_copy(tn_hbm.at[pl.ds(row0, SC_CH)], rows_v)
            pltpu.sync_copy(rows_v, xs_hbm.at[idx_v.at[0, c]])
            pltpu.sync_copy(rows_v, xs_hbm.at[idx_v.at[1, c]])

    return k(tn, slots)


def _sc_collect(ys, slots):
    n_cores, n_workers = _sc_workers()
    per_w = (N_ASSIGN // SC_CH) // n_workers
    assert per_w * n_workers * SC_CH == N_ASSIGN
    slots = slots.reshape(n_workers, per_w, SC_CH)

    @functools.partial(
        pl.kernel,
        mesh=plsc.VectorSubcoreMesh(core_axis_name="c", subcore_axis_name="s"),
        out_type=jax.ShapeDtypeStruct((N_ASSIGN, D_PACK), U32),
        scratch_types=[pltpu.VMEM((per_w, SC_CH), I32), pltpu.VMEM((SC_CH, D_PACK), U32)],
    )
    def k(ys_hbm, slots_hbm, out_hbm, idx_v, rows_v):
        wid = lax.axis_index("s") * n_cores + lax.axis_index("c")
        c0 = wid * per_w
        pltpu.sync_copy(slots_hbm.at[wid], idx_v)

        @pl.loop(0, per_w)
        def _(c):
            row0 = pl.multiple_of((c0 + c) * SC_CH, SC_CH)
            pltpu.sync_copy(ys_hbm.at[idx_v.at[c]], rows_v)
            pltpu.sync_copy(rows_v, out_hbm.at[pl.ds(row0, SC_CH)])

    return k(ys, slots)


W_PARTS = 2


def _expert_body(t0_ref, nt_ref, xs_hbm, wg_hbm, wu_hbm, wd_hbm, ys_hbm,
                 wg_ref, wu_ref, wd_ref, xb_ref, yb_ref, wgb_ref, wub_ref, wdb_ref, wsem, xsem, ysem):
    e = pl.program_id(0)
    n = nt_ref[e]
    g0 = t0_ref[e]
    ws = e & 1

    def w_copies(ex, slot):
        out = []
        for hbm, buf in ((wg_hbm, wg_ref), (wu_hbm, wu_ref), (wd_hbm, wd_ref)):
            rb = buf.shape[1] // W_PARTS
            for p in range(W_PARTS):
                out.append((pltpu.make_async_copy(hbm.at[ex, pl.ds(p * rb, rb)],
                                                  buf.at[slot, pl.ds(p * rb, rb)], wsem.at[slot]), p))
        return out

    @pl.when(e == 0)
    def _():
        for cp, p in w_copies(0, 0):
            cp.start(priority=p)

    @pl.when(e + 1 < MOE_EXPERTS)
    def _():
        for cp, p in w_copies(e + 1, 1 - ws):
            cp.start(priority=p)

    for cp, _ in w_copies(e, ws):
        cp.wait()

    def rows(j):
        return pl.ds(pl.multiple_of((g0 + j) * TME, TME), TME)

    def x_copy(j, s):
        return pltpu.make_async_copy(xs_hbm.at[rows(j)], xb_ref.at[s], xsem.at[s])

    def y_copy(j, s):
        return pltpu.make_async_copy(yb_ref.at[s], ys_hbm.at[rows(j)], ysem.at[s])

    @pl.when(n > 0)
    def _():
        x_copy(0, 0).start()
        wgb_ref[...] = wg_ref[ws].astype(BF16)
        wub_ref[...] = wu_ref[ws].astype(BF16)
        wdb_ref[...] = wd_ref[ws].astype(BF16)

        def tile(j, c):
            s = j & 1
            x_copy(j, s).wait()
            @pl.when(j + 1 < n)
            def _():
                x_copy(j + 1, 1 - s).start()
            @pl.when(j >= 2)
            def _():
                y_copy(j - 2, s).wait()
            x = _unpack_pairs(xb_ref[s], BF16)
            hg = jnp.dot(x, wgb_ref[...], preferred_element_type=F32)
            hu = jnp.dot(x, wub_ref[...], preferred_element_type=F32)
            act = (hg * _sigmoid(hg)) * hu
            yb_ref[s] = _pack_pairs(jnp.dot(act.astype(BF16), wdb_ref[...], preferred_element_type=F32))
            y_copy(j, s).start()
            return c
        lax.fori_loop(0, n, tile, 0)

        @pl.when(n >= 2)
        def _():
            y_copy(n - 2, n & 1).wait()
        y_copy(n - 1, (n - 1) & 1).wait()


def _experts(tile0, tiles, xs, w_eg, w_eu, w_ed):
    any_spec = pl.BlockSpec(memory_space=pl.ANY)
    grid_spec = pltpu.PrefetchScalarGridSpec(
        num_scalar_prefetch=2,
        grid=(MOE_EXPERTS,),
        in_specs=[any_spec] * 4,
        out_specs=any_spec,
        scratch_shapes=[
            pltpu.VMEM((2, D_MODEL, MOE_FF), F32),
            pltpu.VMEM((2, D_MODEL, MOE_FF), F32),
            pltpu.VMEM((2, MOE_FF, D_MODEL), F32),
            pltpu.VMEM((2, TME, D_PACK), U32),
            pltpu.VMEM((2, TME, D_PACK), U32),
            pltpu.VMEM((D_MODEL, MOE_FF), BF16),
            pltpu.VMEM((D_MODEL, MOE_FF), BF16),
            pltpu.VMEM((MOE_FF, D_MODEL), BF16),
            pltpu.SemaphoreType.DMA((2,)),
            pltpu.SemaphoreType.DMA((2,)),
            pltpu.SemaphoreType.DMA((2,)),
        ],
    )
    return pl.pallas_call(
        _expert_body,
        grid_spec=grid_spec,
        out_shape=jax.ShapeDtypeStruct((N_SLOTS, D_PACK), U32),
        compiler_params=_cparams(1),
        name="experts",
    )(tile0, tiles, xs, w_eg, w_eu, w_ed)


FN_TM = 256
FN_PROMPT_BLOCKS = T_PROMPT // FN_TM


def _final_body(h_ref, y0_ref, y1_ref, rt_ref, g_ref, op_ref, os_ref):
    i = pl.program_id(0)
    rt = rt_ref[...]
    z = (h_ref[...] + rt[:, 0:1] * _unpack_pairs(y0_ref[...], F32)
         + rt[:, 1:2] * _unpack_pairs(y1_ref[...], F32))
    inv = lax.rsqrt(jnp.mean(z * z, axis=-1, keepdims=True) + EPS)
    out = (z * inv) * g_ref[...]
    @pl.when(i < FN_PROMPT_BLOCKS)
    def _():
        op_ref[...] = out
    @pl.when(i >= FN_PROMPT_BLOCKS)
    def _():
        os_ref[...] = out


def _final(h, y, route, g_final):
    n = T_ALL // FN_TM
    npb = FN_PROMPT_BLOCKS
    yoff = T_ALL // FN_TM
    return pl.pallas_call(
        _final_body,
        grid=(n,),
        in_specs=[
            pl.BlockSpec((FN_TM, D_MODEL), lambda i: (i, 0)),
            pl.BlockSpec((FN_TM, D_PACK), lambda i: (i, 0)),
            pl.BlockSpec((FN_TM, D_PACK), lambda i: (yoff + i, 0)),
            pl.BlockSpec((FN_TM, ROUTE_LANES), lambda i: (i, 0)),
            pl.BlockSpec((1, D_MODEL), lambda i: (0, 0)),
        ],
        out_specs=(
            pl.BlockSpec((FN_TM, D_MODEL), lambda i: (jnp.minimum(i, npb - 1), 0)),
            pl.BlockSpec((FN_TM, D_MODEL), lambda i: (jnp.maximum(i - npb, 0), 0)),
        ),
        out_shape=(
            jax.ShapeDtypeStruct((T_PROMPT, D_MODEL), F32),
            jax.ShapeDtypeStruct((T_SAMPLE, D_MODEL), F32),
        ),
        compiler_params=_cparams(1),
        name="final",
    )(h, y, y, route, g_final)


def _dispatch_plan(route_t, cnt):
    counts = cnt[0, EXP_LANE0:EXP_LANE0 + MOE_EXPERTS].astype(I32)
    tiles = (counts + (TME - 1)) // TME
    cumt = jnp.cumsum(tiles)
    pad_off = (cumt - tiles) * TME
    rank = route_t[2:4].astype(I32)
    eid = route_t[4:6].astype(I32)
    onehot = eid[None] == jnp.arange(MOE_EXPERTS, dtype=I32)[:, None, None]
    slots = rank + jnp.sum(jnp.where(onehot, pad_off[:, None, None], 0), axis=0)
    return slots, cumt - tiles, tiles


def kernel(x_prompt, x_sample, state_pool, state_ssm_re, state_ssm_im, g_mix, w_in, w_pool,
           pool_scale, ssm_a_re, ssm_a_im, ssm_log_dt, ssm_b_re, ssm_b_im, ssm_c_re, ssm_c_im,
           ssm_d, w_glu_a, w_glu_b, w_out, g_ffn, w_router_group, b_router_group,
           w_router_expert, b_router_expert, w_exp_gate, w_exp_up, w_exp_down, g_final):
    l = 0
    xp = x_prompt.reshape(T_PROMPT, D_MODEL)
    xs = x_sample.transpose(1, 0, 2).reshape(T_SAMPLE, D_MODEL)
    w_in_bf = w_in[l].astype(BF16)
    w_pool_bf = w_pool[l].astype(BF16)
    wa_bf = w_glu_a[l].astype(BF16)
    wb_bf = w_glu_b[l].astype(BF16)
    wo_bf = w_out[l].astype(BF16)
    g_mix2 = g_mix[l].reshape(1, D_MODEL)
    scale2 = pool_scale[l].reshape(1, D_MODEL)

    u, gates = _inproj(xp, g_mix2, w_in_bf, 0)
    u, gates = _inproj(xs, g_mix2, w_in_bf, T_PROMPT // IN_TM, dst=(u, gates))

    y_pool, pool_tail = _pool_prompt(u, w_pool_bf, scale2)
    hist_tm = state_pool[l].transpose(1, 0, 2)
    y_pool, pool_buf_tm = _pool_sample(u, hist_tm, w_pool_bf, scale2, y_pool)

    tables = _ssm_tables(ssm_a_re[l], ssm_a_im[l], ssm_log_dt[l], ssm_b_re[l], ssm_b_im[l],
                         ssm_c_re[l], ssm_c_im[l], ssm_d[l])
    h0r = state_ssm_re[l].reshape(DEC_BATCH, SSM_GROUPS * SSM_STATE)
    h0i = state_ssm_im[l].reshape(DEC_BATCH, SSM_GROUPS * SSM_STATE)
    y_act, h_prompt, hs_re, hs_im = _ssm(u, h0r, h0i, tables)

    assert EXP_LANE0 == MOE_GROUPS
    w_r = jnp.concatenate([w_router_group[l], w_router_expert[l],
                           jnp.zeros((D_MODEL, ROUTE_LANES - EXP_LANE0 - MOE_EXPERTS), F32)], axis=1)
    wr_hi = w_r.astype(BF16)
    wr_cat = jnp.concatenate([wr_hi, (w_r - wr_hi.astype(F32)).astype(BF16)], axis=1)
    b_r = jnp.zeros((1, ROUTE_LANES), F32)
    b_r = b_r.at[0, :MOE_GROUPS].set(b_router_group[l])
    b_r = b_r.at[0, EXP_LANE0:EXP_LANE0 + MOE_EXPERTS].set(b_router_expert[l])

    h, tn, route, route_t, cnt = _postmix(y_act, gates, y_pool, xp, xs, wa_bf, wb_bf, wo_bf,
                                          g_ffn[l].reshape(1, D_MODEL), wr_cat, b_r)
    slots, tile0, tiles = _dispatch_plan(route_t, cnt)
    xs_sorted = _sc_dispatch(tn, slots)
    ys_sorted = _experts(tile0, tiles, xs_sorted, w_exp_gate[l], w_exp_up[l], w_exp_down[l])
    y = _sc_collect(ys_sorted, slots)
    yp, ys = _final(h, y, route, g_final.reshape(1, D_MODEL))

    y_prompt = yp.reshape(BATCH, SEQ, D_MODEL)
    y_sample = ys.reshape(DEC_SEQ, DEC_BATCH, D_MODEL).transpose(1, 0, 2)
    new_pool_prompt = pool_tail[:, HIST - POOL_BUF:, :][None]
    new_pool_sample = pool_buf_tm.transpose(1, 0, 2)[None]
    hp = h_prompt.reshape(BATCH, N_OCT, 2, OCT_GROUPS, SSM_STATE).transpose(2, 0, 1, 3, 4)
    hp = hp.reshape(2, BATCH, SSM_GROUPS, SSM_STATE)
    shp = (1, DEC_BATCH, SSM_GROUPS, SSM_STATE)
    return (y_prompt, y_sample, new_pool_prompt, hp[0][None], hp[1][None], new_pool_sample,
            hs_re.reshape(shp), hs_im.reshape(shp))
```

```python
import functools
import math

import jax
import jax.numpy as jnp
from jax import lax
from jax.experimental import pallas as pl
from jax.experimental.pallas import tpu as pltpu
from jax.experimental.pallas import tpu_sc as plsc

F32 = jnp.float32
BF16 = jnp.bfloat16
I32 = jnp.int32
U32 = jnp.uint32

D_MODEL = 2048
BATCH = 4
SEQ = 2048
DEC_BATCH = 128
DEC_SEQ = 8
PAST_LEN = 16384
POOL_WIDTH = D_MODEL // 2
POOL_WINDOWS = (2, 4, 8, 16)
POOL_GROUPS = len(POOL_WINDOWS)
POOL_GROUP_CH = POOL_WIDTH // POOL_GROUPS
POOL_OUT_CH = D_MODEL // POOL_GROUPS
POOL_BUF = max(POOL_WINDOWS) - 1
SSM_WIDTH = D_MODEL // 2
SSM_GROUP_CH = 16
SSM_GROUPS = SSM_WIDTH // SSM_GROUP_CH
SSM_STATE = 64
IN_WIDTH = POOL_WIDTH + SSM_WIDTH + 2 * D_MODEL
D_PACK = D_MODEL // 2
MOE_GROUPS = 4
MOE_EPG = 8
MOE_EXPERTS = MOE_GROUPS * MOE_EPG
MOE_FF = D_MODEL // 4
EPS = 1e-6

T_PROMPT = BATCH * SEQ
T_SAMPLE = DEC_BATCH * DEC_SEQ
T_ALL = T_PROMPT + T_SAMPLE
T_PAD = (BATCH + 1) * SEQ

LANES = 128
SUBLANES = 8
VMEM_LIMIT = 56 * 1024 * 1024

CHUNK = 8
OCT = LANES
N_OCT = SSM_WIDTH // OCT
OCT_GROUPS = OCT // SSM_GROUP_CH
OCT_STATES = OCT_GROUPS * SSM_STATE
CW = CHUNK * OCT
SW = 2 * OCT_STATES

ROUTE_LANES = LANES
EXP_LANE0 = MOE_GROUPS
N_ASSIGN = 2 * T_ALL
TME = 256
N_ITEMS_MAX = N_ASSIGN // TME + MOE_EXPERTS
N_SLOTS = N_ITEMS_MAX * TME


def _cparams(n_axes):
    return pltpu.CompilerParams(dimension_semantics=("arbitrary",) * n_axes,
                                vmem_limit_bytes=VMEM_LIMIT)


def _sigmoid(x):
    return 1.0 / (1.0 + jnp.exp(-x))


def _pack_pairs(x):
    c = x.shape[1] // 2
    hi = lax.bitcast_convert_type(x[:, :c].astype(BF16).astype(F32), U32)
    lo = lax.bitcast_convert_type(x[:, c:].astype(BF16).astype(F32), U32)
    return hi | (lo >> 16)


def _unpack_pairs(u, dtype):
    hi = lax.bitcast_convert_type(u & jnp.uint32(0xFFFF0000), F32)
    lo = lax.bitcast_convert_type(u << 16, F32)
    return jnp.concatenate([hi, lo], axis=1).astype(dtype)


def _gelu_tanh(x):
    c = math.sqrt(2.0 / math.pi)
    return 0.5 * x * (1.0 + jnp.tanh(c * (x + 0.044715 * (x * x * x))))


IN_TM = 1024
IN_TN = 1024
U_WIDTH = POOL_WIDTH + SSM_WIDTH
GATE_WIDTH = 2 * D_MODEL
IN_U_STEPS = U_WIDTH // IN_TN


def _inproj_body(x_ref, g_ref, w_ref, *rest):
    u_ref, gate_ref, xn_ref = rest[-3:]
    j = pl.program_id(1)
    @pl.when(j == 0)
    def _():
        x = x_ref[...]
        inv = lax.rsqrt(jnp.mean(x * x, axis=-1, keepdims=True) + EPS)
        xn_ref[...] = ((x * inv) * g_ref[...]).astype(BF16)
    acc = jnp.dot(xn_ref[...], w_ref[...], preferred_element_type=F32)
    @pl.when(j < IN_U_STEPS)
    def _():
        u_ref[...] = acc
    @pl.when(j >= IN_U_STEPS)
    def _():
        gate_ref[...] = _sigmoid(acc).astype(BF16)


def _inproj(x, g, w_bf, row_block0, dst=None):
    n_i = x.shape[0] // IN_TM
    in_specs = [
        pl.BlockSpec((IN_TM, D_MODEL), lambda i, j: (i, 0)),
        pl.BlockSpec((1, D_MODEL), lambda i, j: (0, 0)),
        pl.BlockSpec((D_MODEL, IN_TN), lambda i, j: (0, j)),
    ]
    args = [x, g, w_bf]
    aliases = {}
    if dst is not None:
        in_specs += [pl.BlockSpec(memory_space=pl.ANY)] * 2
        args += list(dst)
        aliases = {3: 0, 4: 1}
    return pl.pallas_call(
        _inproj_body,
        grid=(n_i, IN_WIDTH // IN_TN),
        in_specs=in_specs,
        out_specs=(
            pl.BlockSpec((IN_TM, IN_TN), lambda i, j: (i + row_block0, jnp.minimum(j, IN_U_STEPS - 1))),
            pl.BlockSpec((IN_TM, IN_TN), lambda i, j: (i + row_block0, jnp.maximum(j - IN_U_STEPS, 0))),
        ),
        out_shape=(
            jax.ShapeDtypeStruct((T_PAD, U_WIDTH), F32),
            jax.ShapeDtypeStruct((T_ALL, GATE_WIDTH), BF16),
        ),
        scratch_shapes=[pltpu.VMEM((IN_TM, D_MODEL), BF16)],
        input_output_aliases=aliases,
        compiler_params=_cparams(2),
        name="inproj",
    )(*args)


PP_TM = 512
HIST = 16


def _pool_project(pooled_g, g, w_ref, sc_ref, o_ref):
    y = jnp.dot(pooled_g.astype(BF16), w_ref[g], preferred_element_type=F32)
    lo, hi = g * POOL_OUT_CH, (g + 1) * POOL_OUT_CH
    o_ref[:, lo:hi] = (y * sc_ref[:, lo:hi]).astype(o_ref.dtype)


def _pool_prompt_body(u_ref, w_ref, sc_ref, o_ref, tail_ref, hist_ref):
    i = pl.program_id(1)
    @pl.when(i == 0)
    def _():
        hist_ref[...] = jnp.zeros_like(hist_ref)
    u = u_ref[...]
    ext = jnp.concatenate([hist_ref[...], u], axis=0)
    hist_ref[...] = u[PP_TM - HIST:, :]
    tail_ref[...] = u[PP_TM - HIST:, :]
    pos = i * PP_TM + lax.broadcasted_iota(I32, (PP_TM, 1), 0)
    for g, w in enumerate(POOL_WINDOWS):
        lo, hi = g * POOL_GROUP_CH, (g + 1) * POOL_GROUP_CH
        s = ext[:, lo:hi]
        d = 1
        while d < w:
            s = s + pltpu.roll(s, d, axis=0)
            d *= 2
        cnt = jnp.minimum(w, pos + 1).astype(F32)
        pooled = s[HIST:, :] / cnt - u[:, lo:hi]
        _pool_project(pooled, g, w_ref, sc_ref, o_ref)


def _pool_prompt(u, w_pool_bf, pool_scale):
    n_i = SEQ // PP_TM
    return pl.pallas_call(
        _pool_prompt_body,
        grid=(BATCH, n_i),
        in_specs=[
            pl.BlockSpec((PP_TM, POOL_WIDTH), lambda b, i: (b * n_i + i, 0)),
            pl.BlockSpec((POOL_GROUPS, POOL_GROUP_CH, POOL_OUT_CH), lambda b, i: (0, 0, 0)),
            pl.BlockSpec((1, D_MODEL), lambda b, i: (0, 0)),
        ],
        out_specs=(
            pl.BlockSpec((PP_TM, D_MODEL), lambda b, i: (b * n_i + i, 0)),
            pl.BlockSpec((None, HIST, POOL_WIDTH), lambda b, i: (b, 0, 0)),
        ),
        out_shape=(
            jax.ShapeDtypeStruct((T_ALL, D_MODEL), BF16),
            jax.ShapeDtypeStruct((BATCH, HIST, POOL_WIDTH), F32),
        ),
        scratch_shapes=[pltpu.VMEM((HIST, POOL_WIDTH), F32)],
        compiler_params=_cparams(2),
        name="pool_prompt",
    )(u, w_pool_bf, pool_scale)


def _pool_sample_body(u_ref, hist_ref, w_ref, sc_ref, _dst, o_ref, buf_ref):
    rows = [hist_ref[k] for k in range(POOL_BUF)]
    rows += [u_ref[DEC_BATCH * t:DEC_BATCH * (t + 1), :] for t in range(DEC_SEQ)]
    n = len(rows)
    for k in range(POOL_BUF):
        buf_ref[k] = rows[n - POOL_BUF + k]
    for g, w in enumerate(POOL_WINDOWS):
        lo, hi = g * POOL_GROUP_CH, (g + 1) * POOL_GROUP_CH
        f = [r[:, lo:hi] for r in rows]
        cur = f
        d = 1
        while d < w:
            cur = [cur[k] + cur[k - d] if k - d >= 0 else cur[k] for k in range(n)]
            d *= 2
        pooled = jnp.concatenate(
            [cur[POOL_BUF + t] / float(w) - f[POOL_BUF + t] for t in range(DEC_SEQ)], axis=0)
        _pool_project(pooled, g, w_ref, sc_ref, o_ref)


def _pool_sample(u, hist_tm, w_pool_bf, pool_scale, y_pool):
    blk = T_PROMPT // T_SAMPLE
    return pl.pallas_call(
        _pool_sample_body,
        grid=(1,),
        in_specs=[
            pl.BlockSpec((T_SAMPLE, POOL_WIDTH), lambda i: (blk, 0)),
            pl.BlockSpec((POOL_BUF, DEC_BATCH, POOL_WIDTH), lambda i: (0, 0, 0)),
            pl.BlockSpec((POOL_GROUPS, POOL_GROUP_CH, POOL_OUT_CH), lambda i: (0, 0, 0)),
            pl.BlockSpec((1, D_MODEL), lambda i: (0, 0)),
            pl.BlockSpec(memory_space=pl.ANY),
        ],
        out_specs=(
            pl.BlockSpec((T_SAMPLE, D_MODEL), lambda i: (blk, 0)),
            pl.BlockSpec((POOL_BUF, DEC_BATCH, POOL_WIDTH), lambda i: (0, 0, 0)),
        ),
        out_shape=(
            jax.ShapeDtypeStruct((T_ALL, D_MODEL), BF16),
            jax.ShapeDtypeStruct((POOL_BUF, DEC_BATCH, POOL_WIDTH), F32),
        ),
        input_output_aliases={4: 0},
        compiler_params=_cparams(1),
        name="pool_sample",
    )(u, hist_tm, w_pool_bf, pool_scale, y_pool)


def _ssm_tables(a_re, a_im, log_dt, b_re, b_im, c_re, c_im, d_skip):
    dt = jnp.exp(log_dt)[:, None]
    lr, li = a_re, a_im
    ab_re = jnp.exp(lr * dt) * jnp.cos(li * dt)
    ab_im = jnp.exp(lr * dt) * jnp.sin(li * dt)
    den = lr * lr + li * li
    nr, ni = ab_re - 1.0, ab_im
    q_re = (nr * lr + ni * li) / den
    q_im = (ni * lr - nr * li) / den
    bb_re = q_re[..., None] * b_re - q_im[..., None] * b_im
    bb_im = q_re[..., None] * b_im + q_im[..., None] * b_re

    def lam_rows(ks):
        k = jnp.asarray(ks, F32)[:, None, None]
        m = jnp.exp(k * lr * dt)
        re = (m * jnp.cos(k * li * dt)).reshape(len(ks), N_OCT, OCT_STATES)
        im = (m * jnp.sin(k * li * dt)).reshape(len(ks), N_OCT, OCT_STATES)
        return jnp.concatenate([re, im], axis=-1).transpose(1, 0, 2)

    def compact(re, im):
        v = jnp.concatenate([re, im], axis=-1)
        return v.reshape(N_OCT, OCT, 2 * SSM_STATE)

    bbc = compact(jnp.swapaxes(bb_re, 1, 2), jnp.swapaxes(bb_im, 1, 2))
    ccc = compact(c_re, c_im)
    pw = lam_rows(list(range(2 * SUBLANES)))
    r = jnp.arange(SUBLANES)[None, :, None]
    parts = [jnp.where(r >= dd, lam_rows([CHUNK * dd]), 0.0) for dd in (1, 2, 4)]
    parts.append(lam_rows([CHUNK * kk for kk in range(1, SUBLANES + 1)]))
    tab = jnp.concatenate(parts, axis=1)
    dsk = d_skip.reshape(N_OCT, 1, OCT)
    return bbc, ccc, pw, tab, dsk


def _split_bf16(x):
    hi = x.astype(BF16)
    return hi, (x - hi.astype(F32)).astype(BF16)


def _dot_nt(a, b):
    return lax.dot_general(a, b, (((1,), (1,)), ((), ())), preferred_element_type=F32)


def _build_weights(bbc_ref, ccc_ref, pw_ref, f_ref, gt_ref, m_ref):
    row_gi = lax.broadcasted_iota(I32, (OCT, 1), 0) >> 4
    col = lax.broadcasted_iota(I32, (1, SW), 1)
    col_gi = (col >> 6) & 7
    src = ((col >> 9) << 6) | (col & 63)
    k128 = lax.broadcasted_iota(I32, (2 * SSM_STATE, 1), 0)
    spread = jnp.where(k128 == src, 1.0, 0.0).astype(BF16)
    diag = row_gi == col_gi

    def expand(c_ref):
        hi, lo = _split_bf16(c_ref[...])
        d = (jnp.dot(hi, spread, preferred_element_type=F32)
             + jnp.dot(lo, spread, preferred_element_type=F32))
        d = jnp.where(diag, d, 0.0)
        return d[:, :OCT_STATES], d[:, OCT_STATES:]

    br, bi = expand(bbc_ref)
    cr, ci = expand(ccc_ref)
    chi_r, clo_r = _split_bf16(cr)
    chi_i, clo_i = _split_bf16(ci)

    def lam(k):
        return pw_ref[k:k + 1, :OCT_STATES], pw_ref[k:k + 1, OCT_STATES:]

    def dot3(a, bhi, blo):
        ahi, alo = _split_bf16(a)
        return _dot_nt(ahi, bhi) + _dot_nt(alo, bhi) + _dot_nt(ahi, blo)

    lags = []
    for k in range(CHUNK):
        pr, pi_ = lam(k)
        fr, fi = _cmul(br, bi, pr, pi_)
        s = CHUNK - 1 - k
        f_ref[s * OCT:(s + 1) * OCT, :] = jnp.concatenate([fr, fi], axis=1).astype(BF16)
        lags.append((dot3(fr, chi_r, clo_r) - dot3(fi, chi_i, clo_i)).astype(BF16))
        pr, pi_ = lam(k + 1)
        gr, gi = _cmul(cr, ci, pr, pi_)
        gt_ref[k * OCT:(k + 1) * OCT, :] = jnp.concatenate([gr, -gi], axis=1).astype(BF16)
    zero = jnp.zeros((OCT, OCT), BF16)
    for s in range(CHUNK):
        for t in range(CHUNK):
            m_ref[s * OCT:(s + 1) * OCT, t * OCT:(t + 1) * OCT] = lags[t - s] if t >= s else zero


def _cmul(ar, ai, br, bi):
    return ar * br - ai * bi, ar * bi + ai * br


def _chunk_scan(sloc, tab_ref):
    R = sloc.shape[0]
    nb = R // SUBLANES
    sr, si = sloc[:, :OCT_STATES], sloc[:, OCT_STATES:]
    rowi = lax.broadcasted_iota(I32, (R, 1), 0)
    tr = jnp.where(rowi == 0, 0.0, pltpu.roll(sr, 1, axis=0))
    ti = jnp.where(rowi == 0, 0.0, pltpu.roll(si, 1, axis=0))
    for lvl, d in enumerate((1, 2, 4)):
        mr = tab_ref[lvl * SUBLANES:(lvl + 1) * SUBLANES, :OCT_STATES]
        mi = tab_ref[lvl * SUBLANES:(lvl + 1) * SUBLANES, OCT_STATES:]
        mr = jnp.concatenate([mr] * nb, axis=0)
        mi = jnp.concatenate([mi] * nb, axis=0)
        pr, pi_ = _cmul(mr, mi, pltpu.roll(tr, d, axis=0), pltpu.roll(ti, d, axis=0))
        tr, ti = tr + pr, ti + pi_
    pwr = tab_ref[3 * SUBLANES:4 * SUBLANES, :OCT_STATES]
    pwi = tab_ref[3 * SUBLANES:4 * SUBLANES, OCT_STATES:]
    cr = jnp.zeros((1, OCT_STATES), F32)
    ci = jnp.zeros((1, OCT_STATES), F32)
    out_r, out_i = [], []
    for k in range(nb):
        ar = tr[k * SUBLANES:(k + 1) * SUBLANES, :]
        ai = ti[k * SUBLANES:(k + 1) * SUBLANES, :]
        pr, pi_ = _cmul(pwr, pwi, jnp.broadcast_to(cr, ar.shape), jnp.broadcast_to(ci, ai.shape))
        hr, hi = ar + pr, ai + pi_
        out_r.append(hr)
        out_i.append(hi)
        cr, ci = hr[SUBLANES - 1:, :], hi[SUBLANES - 1:, :]
    hin = jnp.concatenate([jnp.concatenate(out_r, axis=0), jnp.concatenate(out_i, axis=0)], axis=1)
    lr, li = pwr[0:1, :], pwi[0:1, :]
    fr, fi = _cmul(lr, li, cr, ci)
    fin = jnp.concatenate([fr + sr[R - 1:, :], fi + si[R - 1:, :]], axis=1)
    return hin, fin


def _ssm_body(u_ref, h0r_ref, h0i_ref, bbc_ref, ccc_ref, pw_ref, tab_ref, d_ref,
              y_ref, hout_ref, hr_ref, hi_ref, f_ref, gt_ref, m_ref):
    b = pl.program_id(1)

    @pl.when(b == 0)
    def _():
        _build_weights(bbc_ref, ccc_ref, pw_ref, f_ref, gt_ref, m_ref)

    def outputs(xs, xb, hin_bf):
        y = (jnp.dot(xb, m_ref[...], preferred_element_type=F32) + _dot_nt(hin_bf, gt_ref[...]))
        return [_gelu_tanh(y[:, t * OCT:(t + 1) * OCT] + d_ref[...] * xs[t]) for t in range(CHUNK)]

    @pl.when(b < BATCH)
    def _():
        R = SEQ // CHUNK
        xs = [u_ref[pl.ds(s, R, stride=CHUNK), :] for s in range(CHUNK)]
        xb = jnp.concatenate(xs, axis=1).astype(BF16)
        sloc = jnp.dot(xb, f_ref[...], preferred_element_type=F32)
        hin, fin = _chunk_scan(sloc, tab_ref)
        for t, yt in enumerate(outputs(xs, xb, hin.astype(BF16))):
            y_ref[pl.ds(t, R, stride=CHUNK), :] = yt
        hout_ref[...] = fin

    @pl.when(b == BATCH)
    def _():
        B = DEC_BATCH
        xs = [u_ref[B * s:B * (s + 1), :] for s in range(CHUNK)]
        xb = jnp.concatenate(xs, axis=1).astype(BF16)
        sloc = jnp.dot(xb, f_ref[...], preferred_element_type=F32)
        h0r, h0i = h0r_ref[...], h0i_ref[...]
        hin = jnp.concatenate([h0r, h0i], axis=1).astype(BF16)
        for t, yt in enumerate(outputs(xs, xb, hin)):
            y_ref[B * t:B * (t + 1), :] = yt
        lr = tab_ref[3 * SUBLANES:3 * SUBLANES + 1, :OCT_STATES]
        li = tab_ref[3 * SUBLANES:3 * SUBLANES + 1, OCT_STATES:]
        nr, ni = _cmul(lr, li, h0r, h0i)
        hr_ref[...] = nr + sloc[:, :OCT_STATES]
        hi_ref[...] = ni + sloc[:, OCT_STATES:]


def _ssm(u, h0r, h0i, tables):
    col0 = POOL_WIDTH // OCT
    im3 = lambda o, b: (o, 0, 0)
    st_spec = pl.BlockSpec((DEC_BATCH, OCT_STATES), lambda o, b: (0, o))
    return pl.pallas_call(
        _ssm_body,
        grid=(N_OCT, BATCH + 1),
        in_specs=[
            pl.BlockSpec((SEQ, OCT), lambda o, b: (b, col0 + o)), st_spec, st_spec,
            pl.BlockSpec((None, OCT, 2 * SSM_STATE), im3),
            pl.BlockSpec((None, OCT, 2 * SSM_STATE), im3),
            pl.BlockSpec((None, 2 * SUBLANES, SW), im3),
            pl.BlockSpec((None, 4 * SUBLANES, SW), im3),
            pl.BlockSpec((None, 1, OCT), im3),
        ],
        out_specs=(
            pl.BlockSpec((SEQ, OCT), lambda o, b: (b, o)),
            pl.BlockSpec((None, 1, SW), lambda o, b: (jnp.minimum(b, BATCH - 1) * N_OCT + o, 0, 0)),
            st_spec, st_spec,
        ),
        out_shape=(
            jax.ShapeDtypeStruct((T_PAD, SSM_WIDTH), F32),
            jax.ShapeDtypeStruct((BATCH * N_OCT, 1, SW), F32),
            jax.ShapeDtypeStruct((DEC_BATCH, SSM_GROUPS * SSM_STATE), F32),
            jax.ShapeDtypeStruct((DEC_BATCH, SSM_GROUPS * SSM_STATE), F32),
        ),
        scratch_shapes=[pltpu.VMEM((CW, SW), BF16), pltpu.VMEM((CW, SW), BF16), pltpu.VMEM((CW, CW), BF16)],
        compiler_params=_cparams(2),
        name="ssm",
    )(u, h0r, h0i, *tables)


PM_TM = 256
PM_PROMPT_BLOCKS = T_PROMPT // PM_TM
PM_STEPS = T_ALL // PM_TM


def _route(logits, valid, cnt_ref):
    lane = lax.broadcasted_iota(I32, (PM_TM, ROUTE_LANES), 1)
    neg = jnp.float32(-jnp.inf)
    big = jnp.int32(1 << 20)
    is_g = lane < MOE_GROUPS
    gmax = jnp.max(jnp.where(is_g, logits, neg), axis=1, keepdims=True)
    g_idx = jnp.min(jnp.where(is_g & (logits == gmax), lane, big), axis=1, keepdims=True)
    g_den = jnp.sum(jnp.where(is_g, jnp.exp(logits - gmax), 0.0), axis=1, keepdims=True)
    g_val = 1.0 / g_den
    e_lane = lane - EXP_LANE0
    sel = (e_lane >= 0) & (e_lane < MOE_EXPERTS) & ((e_lane >> 3) == g_idx)
    m1 = jnp.max(jnp.where(sel, logits, neg), axis=1, keepdims=True)
    i1 = jnp.min(jnp.where(sel & (logits == m1), lane, big), axis=1, keepdims=True)
    sel2 = sel & (lane != i1)
    m2 = jnp.max(jnp.where(sel2, logits, neg), axis=1, keepdims=True)
    i2 = jnp.min(jnp.where(sel2 & (logits == m2), lane, big), axis=1, keepdims=True)
    e2 = jnp.exp(m2 - m1)
    w1 = g_val / (1.0 + e2)
    w2 = g_val * e2 / (1.0 + e2)
    oh1 = lane == i1
    oh2 = lane == i2
    oh = jnp.where(oh1 | oh2, valid, 0.0)
    rr = lax.broadcasted_iota(I32, (PM_TM, PM_TM), 0)
    cc = lax.broadcasted_iota(I32, (PM_TM, PM_TM), 1)
    tri = jnp.where(cc < rr, 1.0, 0.0).astype(BF16)
    base = cnt_ref[...] + jnp.dot(tri, oh.astype(BF16), preferred_element_type=F32)
    rank1 = jnp.sum(jnp.where(oh1, base, 0.0), axis=1, keepdims=True)
    rank2 = jnp.sum(jnp.where(oh2, base, 0.0), axis=1, keepdims=True)
    cnt_ref[...] = cnt_ref[...] + jnp.sum(oh, axis=0, keepdims=True)
    rt = jnp.where(lane == 0, w1, 0.0)
    rt = jnp.where(lane == 1, w2, rt)
    rt = jnp.where(lane == 2, rank1, rt)
    rt = jnp.where(lane == 3, rank2, rt)
    rt = jnp.where(lane == 4, (i1 - EXP_LANE0).astype(F32), rt)
    rt = jnp.where(lane == 5, (i2 - EXP_LANE0).astype(F32), rt)
    return rt


def _postmix_body(ya_ref, gp_ref, gs_ref, yp_ref, xp_ref, xs_ref, wa_ref, wb_ref, wo_ref,
                  gf_ref, wr_ref, br_ref, h_ref, tn_ref, rt_ref, rtt_ref, cnt_out_ref,
                  cnt_ref, lg_ref):
    i = pl.program_id(0)
    @pl.when(i == 0)
    def _():
        cnt_ref[...] = jnp.zeros_like(cnt_ref)
        lg_ref[...] = jnp.zeros_like(lg_ref)
    prev_logits = lg_ref[...]

    ya = ya_ref[...].astype(BF16)
    a = jnp.dot(ya, wa_ref[...], preferred_element_type=F32)
    bg = jnp.dot(ya, wb_ref[...], preferred_element_type=F32)
    y_ssm = a * _sigmoid(bg)
    merged = (gp_ref[...].astype(F32) * yp_ref[...].astype(F32)
              + gs_ref[...].astype(F32) * y_ssm)
    x = jnp.where(jnp.minimum(i, PM_STEPS - 1) < PM_PROMPT_BLOCKS, xp_ref[...], xs_ref[...])
    h = x + jnp.dot(merged.astype(BF16), wo_ref[...], preferred_element_type=F32)
    h_ref[...] = h
    inv = lax.rsqrt(jnp.mean(h * h, axis=-1, keepdims=True) + EPS)
    tn = (h * inv) * gf_ref[...]
    tn_ref[...] = _pack_pairs(tn)
    t_hi = tn.astype(BF16)
    t_lo = (tn - t_hi.astype(F32)).astype(BF16)
    hh = jnp.dot(t_hi, wr_ref[...], preferred_element_type=F32)
    lh = jnp.dot(t_lo, wr_ref[:, :ROUTE_LANES], preferred_element_type=F32)
    lg_ref[...] = (hh[:, :ROUTE_LANES] + lh + hh[:, ROUTE_LANES:]) + br_ref[...]
    rt = _route(prev_logits, jnp.where(i > 0, 1.0, 0.0), cnt_ref)
    rt_ref[...] = rt
    rtt_ref[...] = rt.T[:8, :]
    cnt_out_ref[...] = cnt_ref[...]


def _postmix(y_act, gates, y_pool, xp, xs, wa, wb, wo, g_ffn, wr_cat, b_r):
    npb = PM_PROMPT_BLOCKS
    const2 = lambda i: (0, 0)
    tile = lambda i: jnp.minimum(i, PM_STEPS - 1)
    return pl.pallas_call(
        _postmix_body,
        grid=(PM_STEPS + 1,),
        in_specs=[
            pl.BlockSpec((PM_TM, SSM_WIDTH), lambda i: (tile(i), 0)),
            pl.BlockSpec((PM_TM, D_MODEL), lambda i: (tile(i), 0)),
            pl.BlockSpec((PM_TM, D_MODEL), lambda i: (tile(i), 1)),
            pl.BlockSpec((PM_TM, D_MODEL), lambda i: (tile(i), 0)),
            pl.BlockSpec((PM_TM, D_MODEL), lambda i: (jnp.minimum(i, npb - 1), 0)),
            pl.BlockSpec((PM_TM, D_MODEL), lambda i: (jnp.maximum(tile(i) - npb, 0), 0)),
            pl.BlockSpec((SSM_WIDTH, D_MODEL), const2, pipeline_mode=pl.Buffered(1)),
            pl.BlockSpec((SSM_WIDTH, D_MODEL), const2, pipeline_mode=pl.Buffered(1)),
            pl.BlockSpec((D_MODEL, D_MODEL), const2, pipeline_mode=pl.Buffered(1)),
            pl.BlockSpec((1, D_MODEL), const2),
            pl.BlockSpec((D_MODEL, 2 * ROUTE_LANES), const2),
            pl.BlockSpec((1, ROUTE_LANES), const2),
        ],
        out_specs=(
            pl.BlockSpec((PM_TM, D_MODEL), lambda i: (tile(i), 0)),
            pl.BlockSpec((PM_TM, D_PACK), lambda i: (tile(i), 0)),
            pl.BlockSpec((PM_TM, ROUTE_LANES), lambda i: (jnp.maximum(i - 1, 0), 0)),
            pl.BlockSpec((8, PM_TM), lambda i: (0, jnp.maximum(i - 1, 0))),
            pl.BlockSpec((1, ROUTE_LANES), const2),
        ),
        out_shape=(
            jax.ShapeDtypeStruct((T_ALL, D_MODEL), F32),
            jax.ShapeDtypeStruct((T_ALL, D_PACK), U32),
            jax.ShapeDtypeStruct((T_ALL, ROUTE_LANES), F32),
            jax.ShapeDtypeStruct((8, T_ALL), F32),
            jax.ShapeDtypeStruct((1, ROUTE_LANES), F32),
        ),
        scratch_shapes=[pltpu.VMEM((1, ROUTE_LANES), F32), pltpu.VMEM((PM_TM, ROUTE_LANES), F32)],
        compiler_params=_cparams(1),
        name="postmix",
    )(y_act, gates, gates, y_pool, xp, xs, wa, wb, wo, g_ffn, wr_cat, b_r)


SC_CH = 96


def _sc_workers():
    info = plsc.get_sparse_core_info()
    return info.num_cores, info.num_cores * info.num_subcores


def _sc_dispatch(tn, slots):
    n_cores, n_workers = _sc_workers()
    per_w = (T_ALL // SC_CH) // n_workers
    assert per_w * n_workers * SC_CH == T_ALL
    slots = slots.reshape(2, n_workers, per_w, SC_CH)

    @functools.partial(
        pl.kernel,
        mesh=plsc.VectorSubcoreMesh(core_axis_name="c", subcore_axis_name="s"),
        out_type=jax.ShapeDtypeStruct((N_SLOTS, D_PACK), U32),
        scratch_types=[pltpu.VMEM((2, per_w, SC_CH), I32), pltpu.VMEM((SC_CH, D_PACK), U32)],
    )
    def k(tn_hbm, slots_hbm, xs_hbm, idx_v, rows_v):
        wid = lax.axis_index("s") * n_cores + lax.axis_index("c")
        c0 = wid * per_w
        pltpu.sync_copy(slots_hbm.at[0, wid], idx_v.at[0])
        pltpu.sync_copy(slots_hbm.at[1, wid], idx_v.at[1])

        @pl.loop(0, per_w)
        def _(c):
            row0 = pl.multiple_of((c0 + c) * SC_CH, SC_CH)
            pltpu.sync_copy(tn_hbm.at[pl.ds(row0, SC_CH)], rows_v)
            pltpu.sync_copy(rows_v, xs_hbm.at[idx_v.at[0, c]])
            pltpu.sync_copy(rows_v, xs_hbm.at[idx_v.at[1, c]])

    return k(tn, slots)


def _sc_collect(ys, slots):
    n_cores, n_workers = _sc_workers()
    n_rows = slots.size
    per_w = (n_rows // SC_CH) // n_workers
    assert per_w * n_workers * SC_CH == n_rows
    slots = slots.reshape(n_workers, per_w, SC_CH)

    @functools.partial(
        pl.kernel,
        mesh=plsc.VectorSubcoreMesh(core_axis_name="c", subcore_axis_name="s"),
        out_type=jax.ShapeDtypeStruct((n_rows, D_PACK), U32),
        scratch_types=[pltpu.VMEM((per_w, SC_CH), I32), pltpu.VMEM((SC_CH, D_PACK), U32)],
    )
    def k(ys_hbm, slots_hbm, out_hbm, idx_v, rows_v):
        wid = lax.axis_index("s") * n_cores + lax.axis_index("c")
        c0 = wid * per_w
        pltpu.sync_copy(slots_hbm.at[wid], idx_v)

        @pl.loop(0, per_w)
        def _(c):
            row0 = pl.multiple_of((c0 + c) * SC_CH, SC_CH)
            pltpu.sync_copy(ys_hbm.at[idx_v.at[c]], rows_v)
            pltpu.sync_copy(rows_v, out_hbm.at[pl.ds(row0, SC_CH)])

    return k(ys, slots)


W_PARTS = 2


def _expert_body(t0_ref, nt_ref, xs_hbm, wg_hbm, wu_hbm, wd_hbm, ys_hbm,
                 wg_ref, wu_ref, wd_ref, xb_ref, yb_ref, wgb_ref, wub_ref, wdb_ref, wsem, xsem, ysem):
    e = pl.program_id(0)
    n = nt_ref[e]
    g0 = t0_ref[e]
    ws = e & 1

    def w_copies(ex, slot):
        out = []
        for hbm, buf in ((wg_hbm, wg_ref), (wu_hbm, wu_ref), (wd_hbm, wd_ref)):
            rb = buf.shape[1] // W_PARTS
            for p in range(W_PARTS):
                out.append((pltpu.make_async_copy(hbm.at[ex, pl.ds(p * rb, rb)],
                                                  buf.at[slot, pl.ds(p * rb, rb)], wsem.at[slot]), p))
        return out

    @pl.when(e == 0)
    def _():
        for cp, p in w_copies(0, 0):
            cp.start(priority=p)

    @pl.when(e + 1 < MOE_EXPERTS)
    def _():
        for cp, p in w_copies(e + 1, 1 - ws):
            cp.start(priority=p)

    for cp, _ in w_copies(e, ws):
        cp.wait()

    def rows(j):
        return pl.ds(pl.multiple_of((g0 + j) * TME, TME), TME)

    def x_copy(j, s):
        return pltpu.make_async_copy(xs_hbm.at[rows(j)], xb_ref.at[s], xsem.at[s])

    def y_copy(j, s):
        return pltpu.make_async_copy(yb_ref.at[s], ys_hbm.at[rows(j)], ysem.at[s])

    @pl.when(n > 0)
    def _():
        x_copy(0, 0).start()
        wgb_ref[...] = wg_ref[ws].astype(BF16)
        wub_ref[...] = wu_ref[ws].astype(BF16)
        wdb_ref[...] = wd_ref[ws].astype(BF16)

        def tile(j, c):
            s = j & 1
            x_copy(j, s).wait()
            @pl.when(j + 1 < n)
            def _():
                x_copy(j + 1, 1 - s).start()
            @pl.when(j >= 2)
            def _():
                y_copy(j - 2, s).wait()
            x = _unpack_pairs(xb_ref[s], BF16)
            hg = jnp.dot(x, wgb_ref[...], preferred_element_type=F32)
            hu = jnp.dot(x, wub_ref[...], preferred_element_type=F32)
            act = (hg * _sigmoid(hg)) * hu
            yb_ref[s] = _pack_pairs(jnp.dot(act.astype(BF16), wdb_ref[...], preferred_element_type=F32))
            y_copy(j, s).start()
            return c
        lax.fori_loop(0, n, tile, 0)

        @pl.when(n >= 2)
        def _():
            y_copy(n - 2, n & 1).wait()
        y_copy(n - 1, (n - 1) & 1).wait()


def _experts(tile0, tiles, xs, w_eg, w_eu, w_ed):
    any_spec = pl.BlockSpec(memory_space=pl.ANY)
    grid_spec = pltpu.PrefetchScalarGridSpec(
        num_scalar_prefetch=2,
        grid=(MOE_EXPERTS,),
        in_specs=[any_spec] * 4,
        out_specs=any_spec,
        scratch_shapes=[
            pltpu.VMEM((2, D_MODEL, MOE_FF), F32),
            pltpu.VMEM((2, D_MODEL, MOE_FF), F32),
            pltpu.VMEM((2, MOE_FF, D_MODEL), F32),
            pltpu.VMEM((2, TME, D_PACK), U32),
            pltpu.VMEM((2, TME, D_PACK), U32),
            pltpu.VMEM((D_MODEL, MOE_FF), BF16),
            pltpu.VMEM((D_MODEL, MOE_FF), BF16),
            pltpu.VMEM((MOE_FF, D_MODEL), BF16),
            pltpu.SemaphoreType.DMA((2,)),
            pltpu.SemaphoreType.DMA((2,)),
            pltpu.SemaphoreType.DMA((2,)),
        ],
    )
    return pl.pallas_call(
        _expert_body,
        grid_spec=grid_spec,
        out_shape=jax.ShapeDtypeStruct((N_SLOTS, D_PACK), U32),
        compiler_params=_cparams(1),
        name="experts",
    )(tile0, tiles, xs, w_eg, w_eu, w_ed)


FN_TM = 256
FN_PROMPT_BLOCKS = T_PROMPT // FN_TM


FN_HALF_BLOCKS = T_ALL // FN_TM // 2
assert FN_HALF_BLOCKS <= FN_PROMPT_BLOCKS


def _final_rows(h_ref, y0_ref, y1_ref, rt_ref, g_ref):
    rt = rt_ref[...]
    z = (h_ref[...] + rt[:, 0:1] * _unpack_pairs(y0_ref[...], F32)
         + rt[:, 1:2] * _unpack_pairs(y1_ref[...], F32))
    inv = lax.rsqrt(jnp.mean(z * z, axis=-1, keepdims=True) + EPS)
    return (z * inv) * g_ref[...]


def _final_a_body(h_ref, y0_ref, y1_ref, rt_ref, g_ref, op_ref):
    op_ref[...] = _final_rows(h_ref, y0_ref, y1_ref, rt_ref, g_ref)


def _final_b_body(h_ref, y0_ref, y1_ref, rt_ref, g_ref, _dst, op_ref, os_ref):
    i = pl.program_id(0) + FN_HALF_BLOCKS
    out = _final_rows(h_ref, y0_ref, y1_ref, rt_ref, g_ref)
    @pl.when(i < FN_PROMPT_BLOCKS)
    def _():
        op_ref[...] = out
    @pl.when(i >= FN_PROMPT_BLOCKS)
    def _():
        os_ref[...] = out


def _final(h, y, route, g_final, half, y_p=None):
    n = FN_HALF_BLOCKS
    npb = FN_PROMPT_BLOCKS
    b0 = half * n
    in_specs = [
        pl.BlockSpec((FN_TM, D_MODEL), lambda i: (b0 + i, 0)),
        pl.BlockSpec((FN_TM, D_PACK), lambda i: (i, 0)),
        pl.BlockSpec((FN_TM, D_PACK), lambda i: (n + i, 0)),
        pl.BlockSpec((FN_TM, ROUTE_LANES), lambda i: (b0 + i, 0)),
        pl.BlockSpec((1, D_MODEL), lambda i: (0, 0)),
    ]
    yp_shape = jax.ShapeDtypeStruct((T_PROMPT, D_MODEL), F32)
    if half == 0:
        return pl.pallas_call(
            _final_a_body,
            grid=(n,),
            in_specs=in_specs,
            out_specs=pl.BlockSpec((FN_TM, D_MODEL), lambda i: (i, 0)),
            out_shape=yp_shape,
            compiler_params=_cparams(1),
            name="final_a",
        )(h, y, y, route, g_final)
    return pl.pallas_call(
        _final_b_body,
        grid=(n,),
        in_specs=in_specs + [pl.BlockSpec(memory_space=pl.ANY)],
        out_specs=(
            pl.BlockSpec((FN_TM, D_MODEL), lambda i: (jnp.minimum(b0 + i, npb - 1), 0)),
            pl.BlockSpec((FN_TM, D_MODEL), lambda i: (jnp.maximum(b0 + i - npb, 0), 0)),
        ),
        out_shape=(yp_shape, jax.ShapeDtypeStruct((T_SAMPLE, D_MODEL), F32)),
        input_output_aliases={5: 0},
        compiler_params=_cparams(1),
        name="final_b",
    )(h, y, y, route, g_final, y_p)


def _dispatch_plan(route_t, cnt):
    counts = cnt[0, EXP_LANE0:EXP_LANE0 + MOE_EXPERTS].astype(I32)
    tiles = (counts + (TME - 1)) // TME
    cumt = jnp.cumsum(tiles)
    pad_off = (cumt - tiles) * TME
    rank = route_t[2:4].astype(I32)
    eid = route_t[4:6].astype(I32)
    onehot = eid[None] == jnp.arange(MOE_EXPERTS, dtype=I32)[:, None, None]
    slots = rank + jnp.sum(jnp.where(onehot, pad_off[:, None, None], 0), axis=0)
    return slots, cumt - tiles, tiles


def kernel(x_prompt, x_sample, state_pool, state_ssm_re, state_ssm_im, g_mix, w_in, w_pool,
           pool_scale, ssm_a_re, ssm_a_im, ssm_log_dt, ssm_b_re, ssm_b_im, ssm_c_re, ssm_c_im,
           ssm_d, w_glu_a, w_glu_b, w_out, g_ffn, w_router_group, b_router_group,
           w_router_expert, b_router_expert, w_exp_gate, w_exp_up, w_exp_down, g_final):
    l = 0
    xp = x_prompt.reshape(T_PROMPT, D_MODEL)
    xs = x_sample.transpose(1, 0, 2).reshape(T_SAMPLE, D_MODEL)
    w_in_bf = w_in[l].astype(BF16)
    w_pool_bf = w_pool[l].astype(BF16)
    wa_bf = w_glu_a[l].astype(BF16)
    wb_bf = w_glu_b[l].astype(BF16)
    wo_bf = w_out[l].astype(BF16)
    g_mix2 = g_mix[l].reshape(1, D_MODEL)
    scale2 = pool_scale[l].reshape(1, D_MODEL)

    u, gates = _inproj(xp, g_mix2, w_in_bf, 0)
    u, gates = _inproj(xs, g_mix2, w_in_bf, T_PROMPT // IN_TM, dst=(u, gates))

    y_pool, pool_tail = _pool_prompt(u, w_pool_bf, scale2)
    hist_tm = state_pool[l].transpose(1, 0, 2)
    y_pool, pool_buf_tm = _pool_sample(u, hist_tm, w_pool_bf, scale2, y_pool)

    tables = _ssm_tables(ssm_a_re[l], ssm_a_im[l], ssm_log_dt[l], ssm_b_re[l], ssm_b_im[l],
                         ssm_c_re[l], ssm_c_im[l], ssm_d[l])
    h0r = state_ssm_re[l].reshape(DEC_BATCH, SSM_GROUPS * SSM_STATE)
    h0i = state_ssm_im[l].reshape(DEC_BATCH, SSM_GROUPS * SSM_STATE)
    y_act, h_prompt, hs_re, hs_im = _ssm(u, h0r, h0i, tables)

    assert EXP_LANE0 == MOE_GROUPS
    w_r = jnp.concatenate([w_router_group[l], w_router_expert[l],
                           jnp.zeros((D_MODEL, ROUTE_LANES - EXP_LANE0 - MOE_EXPERTS), F32)], axis=1)
    wr_hi = w_r.astype(BF16)
    wr_cat = jnp.concatenate([wr_hi, (w_r - wr_hi.astype(F32)).astype(BF16)], axis=1)
    b_r = jnp.zeros((1, ROUTE_LANES), F32)
    b_r = b_r.at[0, :MOE_GROUPS].set(b_router_group[l])
    b_r = b_r.at[0, EXP_LANE0:EXP_LANE0 + MOE_EXPERTS].set(b_router_expert[l])

    h, tn, route, route_t, cnt = _postmix(y_act, gates, y_pool, xp, xs, wa_bf, wb_bf, wo_bf,
                                          g_ffn[l].reshape(1, D_MODEL), wr_cat, b_r)
    slots, tile0, tiles = _dispatch_plan(route_t, cnt)
    xs_sorted = _sc_dispatch(tn, slots)
    ys_sorted = _experts(tile0, tiles, xs_sorted, w_exp_gate[l], w_exp_up[l], w_exp_down[l])
    half = T_ALL // 2
    y_a = _sc_collect(ys_sorted, slots[:, :half])
    y_b = _sc_collect(ys_sorted, slots[:, half:])
    g_fin = g_final.reshape(1, D_MODEL)
    yp = _final(h, y_a, route, g_fin, 0)
    yp, ys = _final(h, y_b, route, g_fin, 1, yp)

    y_prompt = yp.reshape(BATCH, SEQ, D_MODEL)
    y_sample = ys.reshape(DEC_SEQ, DEC_BATCH, D_MODEL).transpose(1, 0, 2)
    new_pool_prompt = pool_tail[:, HIST - POOL_BUF:, :][None]
    new_pool_sample = pool_buf_tm.transpose(1, 0, 2)[None]
    hp = h_prompt.reshape(BATCH, N_OCT, 2, OCT_GROUPS, SSM_STATE).transpose(2, 0, 1, 3, 4)
    hp = hp.reshape(2, BATCH, SSM_GROUPS, SSM_STATE)
    shp = (1, DEC_BATCH, SSM_GROUPS, SSM_STATE)
    return (y_prompt, y_sample, new_pool_prompt, hp[0][None], hp[1][None], new_pool_sample,
            hs_re.reshape(shp), hs_im.reshape(shp))
```

```python
import functools
import math

import jax
import jax.numpy as jnp
from jax import lax
from jax.experimental import pallas as pl
from jax.experimental.pallas import tpu as pltpu
from jax.experimental.pallas import tpu_sc as plsc

F32 = jnp.float32
BF16 = jnp.bfloat16
I32 = jnp.int32
U32 = jnp.uint32

D_MODEL = 2048
BATCH = 4
SEQ = 2048
DEC_BATCH = 128
DEC_SEQ = 8
PAST_LEN = 16384
POOL_WIDTH = D_MODEL // 2
POOL_WINDOWS = (2, 4, 8, 16)
POOL_GROUPS = len(POOL_WINDOWS)
POOL_GROUP_CH = POOL_WIDTH // POOL_GROUPS
POOL_OUT_CH = D_MODEL // POOL_GROUPS
POOL_BUF = max(POOL_WINDOWS) - 1
SSM_WIDTH = D_MODEL // 2
SSM_GROUP_CH = 16
SSM_GROUPS = SSM_WIDTH // SSM_GROUP_CH
SSM_STATE = 64
IN_WIDTH = POOL_WIDTH + SSM_WIDTH + 2 * D_MODEL
D_PACK = D_MODEL // 2
MOE_GROUPS = 4
MOE_EPG = 8
MOE_EXPERTS = MOE_GROUPS * MOE_EPG
MOE_FF = D_MODEL // 4
EPS = 1e-6

T_PROMPT = BATCH * SEQ
T_SAMPLE = DEC_BATCH * DEC_SEQ
T_ALL = T_PROMPT + T_SAMPLE
T_PAD = (BATCH + 1) * SEQ

LANES = 128
SUBLANES = 8
VMEM_LIMIT = 56 * 1024 * 1024

CHUNK = 8
OCT = LANES
N_OCT = SSM_WIDTH // OCT
OCT_GROUPS = OCT // SSM_GROUP_CH
OCT_STATES = OCT_GROUPS * SSM_STATE
CW = CHUNK * OCT
SW = 2 * OCT_STATES

ROUTE_LANES = LANES
EXP_LANE0 = MOE_GROUPS
N_ASSIGN = 2 * T_ALL
TME = 256
N_ITEMS_MAX = N_ASSIGN // TME + MOE_EXPERTS
N_SLOTS = N_ITEMS_MAX * TME


def _cparams(n_axes):
    return pltpu.CompilerParams(dimension_semantics=("arbitrary",) * n_axes,
                                vmem_limit_bytes=VMEM_LIMIT)


def _sigmoid(x):
    return 1.0 / (1.0 + jnp.exp(-x))


def _pack_pairs(x):
    c = x.shape[1] // 2
    hi = lax.bitcast_convert_type(x[:, :c].astype(BF16).astype(F32), U32)
    lo = lax.bitcast_convert_type(x[:, c:].astype(BF16).astype(F32), U32)
    return hi | (lo >> 16)


def _unpack_pairs(u, dtype):
    hi = lax.bitcast_convert_type(u & jnp.uint32(0xFFFF0000), F32)
    lo = lax.bitcast_convert_type(u << 16, F32)
    return jnp.concatenate([hi, lo], axis=1).astype(dtype)


def _gelu_tanh(x):
    c = math.sqrt(2.0 / math.pi)
    return 0.5 * x * (1.0 + jnp.tanh(c * (x + 0.044715 * (x * x * x))))


IN_TM = 1024
IN_TN = 1024
U_WIDTH = POOL_WIDTH + SSM_WIDTH
GATE_WIDTH = 2 * D_MODEL
IN_U_STEPS = U_WIDTH // IN_TN


CAST_RB = 128


def _inproj_body(n_alias, n_cast, x_ref, g_ref, w_ref, *rest):
    cast_in = rest[n_alias:n_alias + n_cast]
    u_ref, gate_ref = rest[n_alias + n_cast:n_alias + n_cast + 2]
    cast_out = rest[n_alias + n_cast + 2:-1]
    xn_ref = rest[-1]
    j = pl.program_id(1)
    for ci, co in zip(cast_in, cast_out):
        co[...] = ci[...].astype(BF16)
    @pl.when(j == 0)
    def _():
        x = x_ref[...]
        inv = lax.rsqrt(jnp.mean(x * x, axis=-1, keepdims=True) + EPS)
        xn_ref[...] = ((x * inv) * g_ref[...]).astype(BF16)
    acc = jnp.dot(xn_ref[...], w_ref[...], preferred_element_type=F32)
    @pl.when(j < IN_U_STEPS)
    def _():
        u_ref[...] = acc
    @pl.when(j >= IN_U_STEPS)
    def _():
        gate_ref[...] = _sigmoid(acc).astype(BF16)


def _inproj(x, g, w_bf, row_block0, dst=None, cast=()):
    n_i = x.shape[0] // IN_TM
    n_j = IN_WIDTH // IN_TN
    in_specs = [
        pl.BlockSpec((IN_TM, D_MODEL), lambda i, j: (i, 0)),
        pl.BlockSpec((1, D_MODEL), lambda i, j: (0, 0)),
        pl.BlockSpec((D_MODEL, IN_TN), lambda i, j: (0, j)),
    ]
    args = [x, g, w_bf]
    aliases = {}
    if dst is not None:
        in_specs += [pl.BlockSpec(memory_space=pl.ANY)] * 2
        args += list(dst)
        aliases = {3: 0, 4: 1}
    cast_specs, cast_shapes, off = [], [], 0
    for wc in cast:
        nb = wc.shape[0] // CAST_RB
        cast_specs.append(pl.BlockSpec(
            (CAST_RB, wc.shape[1]), lambda i, j, o=off, nb=nb: (jnp.clip(i * n_j + j - o, 0, nb - 1), 0)))
        cast_shapes.append(jax.ShapeDtypeStruct(wc.shape, BF16))
        off += nb
    assert off <= n_i * n_j
    args += list(cast)
    return pl.pallas_call(
        functools.partial(_inproj_body, len(aliases), len(cast)),
        grid=(n_i, n_j),
        in_specs=in_specs + cast_specs,
        out_specs=(
            pl.BlockSpec((IN_TM, IN_TN), lambda i, j: (i + row_block0, jnp.minimum(j, IN_U_STEPS - 1))),
            pl.BlockSpec((IN_TM, IN_TN), lambda i, j: (i + row_block0, jnp.maximum(j - IN_U_STEPS, 0))),
            *cast_specs,
        ),
        out_shape=(
            jax.ShapeDtypeStruct((T_PAD, U_WIDTH), F32),
            jax.ShapeDtypeStruct((T_ALL, GATE_WIDTH), BF16),
            *cast_shapes,
        ),
        scratch_shapes=[pltpu.VMEM((IN_TM, D_MODEL), BF16)],
        input_output_aliases=aliases,
        compiler_params=_cparams(2),
        name="inproj",
    )(*args)


PP_TM = 512
HIST = 16


def _pool_project(pooled_g, g, w_ref, sc_ref, o_ref):
    y = jnp.dot(pooled_g.astype(BF16), w_ref[g], preferred_element_type=F32)
    lo, hi = g * POOL_OUT_CH, (g + 1) * POOL_OUT_CH
    o_ref[:, lo:hi] = (y * sc_ref[:, lo:hi]).astype(o_ref.dtype)


def _pool_prompt_body(u_ref, w_ref, sc_ref, o_ref, tail_ref, hist_ref):
    i = pl.program_id(1)
    @pl.when(i == 0)
    def _():
        hist_ref[...] = jnp.zeros_like(hist_ref)
    u = u_ref[...]
    ext = jnp.concatenate([hist_ref[...], u], axis=0)
    hist_ref[...] = u[PP_TM - HIST:, :]
    tail_ref[...] = u[PP_TM - HIST:, :]
    pos = i * PP_TM + lax.broadcasted_iota(I32, (PP_TM, 1), 0)
    for g, w in enumerate(POOL_WINDOWS):
        lo, hi = g * POOL_GROUP_CH, (g + 1) * POOL_GROUP_CH
        s = ext[:, lo:hi]
        d = 1
        while d < w:
            s = s + pltpu.roll(s, d, axis=0)
            d *= 2
        cnt = jnp.minimum(w, pos + 1).astype(F32)
        pooled = s[HIST:, :] / cnt - u[:, lo:hi]
        _pool_project(pooled, g, w_ref, sc_ref, o_ref)


def _pool_prompt(u, w_pool_bf, pool_scale):
    n_i = SEQ // PP_TM
    return pl.pallas_call(
        _pool_prompt_body,
        grid=(BATCH, n_i),
        in_specs=[
            pl.BlockSpec((PP_TM, POOL_WIDTH), lambda b, i: (b * n_i + i, 0)),
            pl.BlockSpec((POOL_GROUPS, POOL_GROUP_CH, POOL_OUT_CH), lambda b, i: (0, 0, 0)),
            pl.BlockSpec((1, D_MODEL), lambda b, i: (0, 0)),
        ],
        out_specs=(
            pl.BlockSpec((PP_TM, D_MODEL), lambda b, i: (b * n_i + i, 0)),
            pl.BlockSpec((None, HIST, POOL_WIDTH), lambda b, i: (b, 0, 0)),
        ),
        out_shape=(
            jax.ShapeDtypeStruct((T_ALL, D_MODEL), BF16),
            jax.ShapeDtypeStruct((BATCH, HIST, POOL_WIDTH), F32),
        ),
        scratch_shapes=[pltpu.VMEM((HIST, POOL_WIDTH), F32)],
        compiler_params=_cparams(2),
        name="pool_prompt",
    )(u, w_pool_bf, pool_scale)


def _pool_sample_body(u_ref, hist_ref, w_ref, sc_ref, _dst, o_ref, buf_ref):
    rows = [hist_ref[k] for k in range(POOL_BUF)]
    rows += [u_ref[DEC_BATCH * t:DEC_BATCH * (t + 1), :] for t in range(DEC_SEQ)]
    n = len(rows)
    for k in range(POOL_BUF):
        buf_ref[k] = rows[n - POOL_BUF + k]
    for g, w in enumerate(POOL_WINDOWS):
        lo, hi = g * POOL_GROUP_CH, (g + 1) * POOL_GROUP_CH
        f = [r[:, lo:hi] for r in rows]
        cur = f
        d = 1
        while d < w:
            cur = [cur[k] + cur[k - d] if k - d >= 0 else cur[k] for k in range(n)]
            d *= 2
        pooled = jnp.concatenate(
            [cur[POOL_BUF + t] / float(w) - f[POOL_BUF + t] for t in range(DEC_SEQ)], axis=0)
        _pool_project(pooled, g, w_ref, sc_ref, o_ref)


def _pool_sample(u, hist_tm, w_pool_bf, pool_scale, y_pool):
    blk = T_PROMPT // T_SAMPLE
    return pl.pallas_call(
        _pool_sample_body,
        grid=(1,),
        in_specs=[
            pl.BlockSpec((T_SAMPLE, POOL_WIDTH), lambda i: (blk, 0)),
            pl.BlockSpec((POOL_BUF, DEC_BATCH, POOL_WIDTH), lambda i: (0, 0, 0)),
            pl.BlockSpec((POOL_GROUPS, POOL_GROUP_CH, POOL_OUT_CH), lambda i: (0, 0, 0)),
            pl.BlockSpec((1, D_MODEL), lambda i: (0, 0)),
            pl.BlockSpec(memory_space=pl.ANY),
        ],
        out_specs=(
            pl.BlockSpec((T_SAMPLE, D_MODEL), lambda i: (blk, 0)),
            pl.BlockSpec((POOL_BUF, DEC_BATCH, POOL_WIDTH), lambda i: (0, 0, 0)),
        ),
        out_shape=(
            jax.ShapeDtypeStruct((T_ALL, D_MODEL), BF16),
            jax.ShapeDtypeStruct((POOL_BUF, DEC_BATCH, POOL_WIDTH), F32),
        ),
        input_output_aliases={4: 0},
        compiler_params=_cparams(1),
        name="pool_sample",
    )(u, hist_tm, w_pool_bf, pool_scale, y_pool)


def _ssm_tables(a_re, a_im, log_dt, b_re, b_im, c_re, c_im, d_skip):
    dt = jnp.exp(log_dt)[:, None]
    lr, li = a_re, a_im
    ab_re = jnp.exp(lr * dt) * jnp.cos(li * dt)
    ab_im = jnp.exp(lr * dt) * jnp.sin(li * dt)
    den = lr * lr + li * li
    nr, ni = ab_re - 1.0, ab_im
    q_re = (nr * lr + ni * li) / den
    q_im = (ni * lr - nr * li) / den
    bb_re = q_re[..., None] * b_re - q_im[..., None] * b_im
    bb_im = q_re[..., None] * b_im + q_im[..., None] * b_re

    def lam_rows(ks):
        k = jnp.asarray(ks, F32)[:, None, None]
        m = jnp.exp(k * lr * dt)
        re = (m * jnp.cos(k * li * dt)).reshape(len(ks), N_OCT, OCT_STATES)
        im = (m * jnp.sin(k * li * dt)).reshape(len(ks), N_OCT, OCT_STATES)
        return jnp.concatenate([re, im], axis=-1).transpose(1, 0, 2)

    def compact(re, im):
        v = jnp.concatenate([re, im], axis=-1)
        return v.reshape(N_OCT, OCT, 2 * SSM_STATE)

    bbc = compact(jnp.swapaxes(bb_re, 1, 2), jnp.swapaxes(bb_im, 1, 2))
    ccc = compact(c_re, c_im)
    pw = lam_rows(list(range(2 * SUBLANES)))
    r = jnp.arange(SUBLANES)[None, :, None]
    parts = [jnp.where(r >= dd, lam_rows([CHUNK * dd]), 0.0) for dd in (1, 2, 4)]
    parts.append(lam_rows([CHUNK * kk for kk in range(1, SUBLANES + 1)]))
    tab = jnp.concatenate(parts, axis=1)
    dsk = d_skip.reshape(N_OCT, 1, OCT)
    return bbc, ccc, pw, tab, dsk


def _split_bf16(x):
    hi = x.astype(BF16)
    return hi, (x - hi.astype(F32)).astype(BF16)


def _dot_nt(a, b):
    return lax.dot_general(a, b, (((1,), (1,)), ((), ())), preferred_element_type=F32)


def _build_weights(bbc_ref, ccc_ref, pw_ref, f_ref, gt_ref, m_ref):
    row_gi = lax.broadcasted_iota(I32, (OCT, 1), 0) >> 4
    col = lax.broadcasted_iota(I32, (1, SW), 1)
    col_gi = (col >> 6) & 7
    src = ((col >> 9) << 6) | (col & 63)
    k128 = lax.broadcasted_iota(I32, (2 * SSM_STATE, 1), 0)
    spread = jnp.where(k128 == src, 1.0, 0.0).astype(BF16)
    diag = row_gi == col_gi

    def expand(c_ref):
        hi, lo = _split_bf16(c_ref[...])
        d = (jnp.dot(hi, spread, preferred_element_type=F32)
             + jnp.dot(lo, spread, preferred_element_type=F32))
        d = jnp.where(diag, d, 0.0)
        return d[:, :OCT_STATES], d[:, OCT_STATES:]

    br, bi = expand(bbc_ref)
    cr, ci = expand(ccc_ref)
    chi_r, clo_r = _split_bf16(cr)
    chi_i, clo_i = _split_bf16(ci)

    def lam(k):
        return pw_ref[k:k + 1, :OCT_STATES], pw_ref[k:k + 1, OCT_STATES:]

    def dot3(a, bhi, blo):
        ahi, alo = _split_bf16(a)
        return _dot_nt(ahi, bhi) + _dot_nt(alo, bhi) + _dot_nt(ahi, blo)

    lags = []
    for k in range(CHUNK):
        pr, pi_ = lam(k)
        fr, fi = _cmul(br, bi, pr, pi_)
        s = CHUNK - 1 - k
        f_ref[s * OCT:(s + 1) * OCT, :] = jnp.concatenate([fr, fi], axis=1).astype(BF16)
        lags.append((dot3(fr, chi_r, clo_r) - dot3(fi, chi_i, clo_i)).astype(BF16))
        pr, pi_ = lam(k + 1)
        gr, gi = _cmul(cr, ci, pr, pi_)
        gt_ref[k * OCT:(k + 1) * OCT, :] = jnp.concatenate([gr, -gi], axis=1).astype(BF16)
    zero = jnp.zeros((OCT, OCT), BF16)
    for s in range(CHUNK):
        for t in range(CHUNK):
            m_ref[s * OCT:(s + 1) * OCT, t * OCT:(t + 1) * OCT] = lags[t - s] if t >= s else zero


def _cmul(ar, ai, br, bi):
    return ar * br - ai * bi, ar * bi + ai * br


def _chunk_scan(sloc, tab_ref):
    R = sloc.shape[0]
    nb = R // SUBLANES
    sr, si = sloc[:, :OCT_STATES], sloc[:, OCT_STATES:]
    rowi = lax.broadcasted_iota(I32, (R, 1), 0)
    tr = jnp.where(rowi == 0, 0.0, pltpu.roll(sr, 1, axis=0))
    ti = jnp.where(rowi == 0, 0.0, pltpu.roll(si, 1, axis=0))
    for lvl, d in enumerate((1, 2, 4)):
        mr = tab_ref[lvl * SUBLANES:(lvl + 1) * SUBLANES, :OCT_STATES]
        mi = tab_ref[lvl * SUBLANES:(lvl + 1) * SUBLANES, OCT_STATES:]
        mr = jnp.concatenate([mr] * nb, axis=0)
        mi = jnp.concatenate([mi] * nb, axis=0)
        pr, pi_ = _cmul(mr, mi, pltpu.roll(tr, d, axis=0), pltpu.roll(ti, d, axis=0))
        tr, ti = tr + pr, ti + pi_
    pwr = tab_ref[3 * SUBLANES:4 * SUBLANES, :OCT_STATES]
    pwi = tab_ref[3 * SUBLANES:4 * SUBLANES, OCT_STATES:]
    cr = jnp.zeros((1, OCT_STATES), F32)
    ci = jnp.zeros((1, OCT_STATES), F32)
    out_r, out_i = [], []
    for k in range(nb):
        ar = tr[k * SUBLANES:(k + 1) * SUBLANES, :]
        ai = ti[k * SUBLANES:(k + 1) * SUBLANES, :]
        pr, pi_ = _cmul(pwr, pwi, jnp.broadcast_to(cr, ar.shape), jnp.broadcast_to(ci, ai.shape))
        hr, hi = ar + pr, ai + pi_
        out_r.append(hr)
        out_i.append(hi)
        cr, ci = hr[SUBLANES - 1:, :], hi[SUBLANES - 1:, :]
    hin = jnp.concatenate([jnp.concatenate(out_r, axis=0), jnp.concatenate(out_i, axis=0)], axis=1)
    lr, li = pwr[0:1, :], pwi[0:1, :]
    fr, fi = _cmul(lr, li, cr, ci)
    fin = jnp.concatenate([fr + sr[R - 1:, :], fi + si[R - 1:, :]], axis=1)
    return hin, fin


def _ssm_body(u_ref, h0r_ref, h0i_ref, bbc_ref, ccc_ref, pw_ref, tab_ref, d_ref,
              y_ref, hout_ref, hr_ref, hi_ref, f_ref, gt_ref, m_ref):
    b = pl.program_id(1)

    @pl.when(b == 0)
    def _():
        _build_weights(bbc_ref, ccc_ref, pw_ref, f_ref, gt_ref, m_ref)

    def outputs(xs, xb, hin_bf):
        y = (jnp.dot(xb, m_ref[...], preferred_element_type=F32) + _dot_nt(hin_bf, gt_ref[...]))
        return [_gelu_tanh(y[:, t * OCT:(t + 1) * OCT] + d_ref[...] * xs[t]) for t in range(CHUNK)]

    @pl.when(b < BATCH)
    def _():
        R = SEQ // CHUNK
        xs = [u_ref[pl.ds(s, R, stride=CHUNK), :] for s in range(CHUNK)]
        xb = jnp.concatenate(xs, axis=1).astype(BF16)
        sloc = jnp.dot(xb, f_ref[...], preferred_element_type=F32)
        hin, fin = _chunk_scan(sloc, tab_ref)
        for t, yt in enumerate(outputs(xs, xb, hin.astype(BF16))):
            y_ref[pl.ds(t, R, stride=CHUNK), :] = yt
        hout_ref[...] = fin

    @pl.when(b == BATCH)
    def _():
        B = DEC_BATCH
        xs = [u_ref[B * s:B * (s + 1), :] for s in range(CHUNK)]
        xb = jnp.concatenate(xs, axis=1).astype(BF16)
        sloc = jnp.dot(xb, f_ref[...], preferred_element_type=F32)
        h0r, h0i = h0r_ref[...], h0i_ref[...]
        hin = jnp.concatenate([h0r, h0i], axis=1).astype(BF16)
        for t, yt in enumerate(outputs(xs, xb, hin)):
            y_ref[B * t:B * (t + 1), :] = yt
        lr = tab_ref[3 * SUBLANES:3 * SUBLANES + 1, :OCT_STATES]
        li = tab_ref[3 * SUBLANES:3 * SUBLANES + 1, OCT_STATES:]
        nr, ni = _cmul(lr, li, h0r, h0i)
        hr_ref[...] = nr + sloc[:, :OCT_STATES]
        hi_ref[...] = ni + sloc[:, OCT_STATES:]


def _ssm(u, h0r, h0i, tables):
    col0 = POOL_WIDTH // OCT
    im3 = lambda o, b: (o, 0, 0)
    st_spec = pl.BlockSpec((DEC_BATCH, OCT_STATES), lambda o, b: (0, o))
    return pl.pallas_call(
        _ssm_body,
        grid=(N_OCT, BATCH + 1),
        in_specs=[
            pl.BlockSpec((SEQ, OCT), lambda o, b: (b, col0 + o)), st_spec, st_spec,
            pl.BlockSpec((None, OCT, 2 * SSM_STATE), im3),
            pl.BlockSpec((None, OCT, 2 * SSM_STATE), im3),
            pl.BlockSpec((None, 2 * SUBLANES, SW), im3),
            pl.BlockSpec((None, 4 * SUBLANES, SW), im3),
            pl.BlockSpec((None, 1, OCT), im3),
        ],
        out_specs=(
            pl.BlockSpec((SEQ, OCT), lambda o, b: (b, o)),
            pl.BlockSpec((None, 1, SW), lambda o, b: (jnp.minimum(b, BATCH - 1) * N_OCT + o, 0, 0)),
            st_spec, st_spec,
        ),
        out_shape=(
            jax.ShapeDtypeStruct((T_PAD, SSM_WIDTH), F32),
            jax.ShapeDtypeStruct((BATCH * N_OCT, 1, SW), F32),
            jax.ShapeDtypeStruct((DEC_BATCH, SSM_GROUPS * SSM_STATE), F32),
            jax.ShapeDtypeStruct((DEC_BATCH, SSM_GROUPS * SSM_STATE), F32),
        ),
        scratch_shapes=[pltpu.VMEM((CW, SW), BF16), pltpu.VMEM((CW, SW), BF16), pltpu.VMEM((CW, CW), BF16)],
        compiler_params=_cparams(2),
        name="ssm",
    )(u, h0r, h0i, *tables)


PM_TM = 256
PM_PROMPT_BLOCKS = T_PROMPT // PM_TM
PM_STEPS = T_ALL // PM_TM


def _route(logits, valid, cnt_ref):
    lane = lax.broadcasted_iota(I32, (PM_TM, ROUTE_LANES), 1)
    neg = jnp.float32(-jnp.inf)
    big = jnp.int32(1 << 20)
    is_g = lane < MOE_GROUPS
    gmax = jnp.max(jnp.where(is_g, logits, neg), axis=1, keepdims=True)
    g_idx = jnp.min(jnp.where(is_g & (logits == gmax), lane, big), axis=1, keepdims=True)
    g_den = jnp.sum(jnp.where(is_g, jnp.exp(logits - gmax), 0.0), axis=1, keepdims=True)
    g_val = 1.0 / g_den
    e_lane = lane - EXP_LANE0
    sel = (e_lane >= 0) & (e_lane < MOE_EXPERTS) & ((e_lane >> 3) == g_idx)
    m1 = jnp.max(jnp.where(sel, logits, neg), axis=1, keepdims=True)
    i1 = jnp.min(jnp.where(sel & (logits == m1), lane, big), axis=1, keepdims=True)
    sel2 = sel & (lane != i1)
    m2 = jnp.max(jnp.where(sel2, logits, neg), axis=1, keepdims=True)
    i2 = jnp.min(jnp.where(sel2 & (logits == m2), lane, big), axis=1, keepdims=True)
    e2 = jnp.exp(m2 - m1)
    w1 = g_val / (1.0 + e2)
    w2 = g_val * e2 / (1.0 + e2)
    oh1 = lane == i1
    oh2 = lane == i2
    oh = jnp.where(oh1 | oh2, valid, 0.0)
    rr = lax.broadcasted_iota(I32, (PM_TM, PM_TM), 0)
    cc = lax.broadcasted_iota(I32, (PM_TM, PM_TM), 1)
    tri = jnp.where(cc < rr, 1.0, 0.0).astype(BF16)
    base = cnt_ref[...] + jnp.dot(tri, oh.astype(BF16), preferred_element_type=F32)
    rank1 = jnp.sum(jnp.where(oh1, base, 0.0), axis=1, keepdims=True)
    rank2 = jnp.sum(jnp.where(oh2, base, 0.0), axis=1, keepdims=True)
    cnt_ref[...] = cnt_ref[...] + jnp.sum(oh, axis=0, keepdims=True)
    rt = jnp.where(lane == 0, w1, 0.0)
    rt = jnp.where(lane == 1, w2, rt)
    rt = jnp.where(lane == 2, rank1, rt)
    rt = jnp.where(lane == 3, rank2, rt)
    rt = jnp.where(lane == 4, (i1 - EXP_LANE0).astype(F32), rt)
    rt = jnp.where(lane == 5, (i2 - EXP_LANE0).astype(F32), rt)
    return rt


def _postmix_body(ya_ref, gp_ref, gs_ref, yp_ref, xp_ref, xs_ref, wa_ref, wb_ref, wo_ref,
                  gf_ref, wr_ref, br_ref, h_ref, tn_ref, rt_ref, rtt_ref, cnt_out_ref,
                  cnt_ref, lg_ref):
    i = pl.program_id(0)
    @pl.when(i == 0)
    def _():
        cnt_ref[...] = jnp.zeros_like(cnt_ref)
        lg_ref[...] = jnp.zeros_like(lg_ref)
    prev_logits = lg_ref[...]

    ya = ya_ref[...].astype(BF16)
    a = jnp.dot(ya, wa_ref[...], preferred_element_type=F32)
    bg = jnp.dot(ya, wb_ref[...], preferred_element_type=F32)
    y_ssm = a * _sigmoid(bg)
    merged = (gp_ref[...].astype(F32) * yp_ref[...].astype(F32)
              + gs_ref[...].astype(F32) * y_ssm)
    x = jnp.where(jnp.minimum(i, PM_STEPS - 1) < PM_PROMPT_BLOCKS, xp_ref[...], xs_ref[...])
    h = x + jnp.dot(merged.astype(BF16), wo_ref[...], preferred_element_type=F32)
    h_ref[...] = h
    inv = lax.rsqrt(jnp.mean(h * h, axis=-1, keepdims=True) + EPS)
    tn = (h * inv) * gf_ref[...]
    tn_ref[...] = _pack_pairs(tn)
    t_hi = tn.astype(BF16)
    t_lo = (tn - t_hi.astype(F32)).astype(BF16)
    hh = jnp.dot(t_hi, wr_ref[...], preferred_element_type=F32)
    lh = jnp.dot(t_lo, wr_ref[:, :ROUTE_LANES], preferred_element_type=F32)
    lg_ref[...] = (hh[:, :ROUTE_LANES] + lh + hh[:, ROUTE_LANES:]) + br_ref[...]
    rt = _route(prev_logits, jnp.where(i > 0, 1.0, 0.0), cnt_ref)
    rt_ref[...] = rt
    rtt_ref[...] = rt.T[:8, :]
    cnt_out_ref[...] = cnt_ref[...]


def _postmix(y_act, gates, y_pool, xp, xs, wa, wb, wo, g_ffn, wr_cat, b_r):
    npb = PM_PROMPT_BLOCKS
    const2 = lambda i: (0, 0)
    tile = lambda i: jnp.minimum(i, PM_STEPS - 1)
    return pl.pallas_call(
        _postmix_body,
        grid=(PM_STEPS + 1,),
        in_specs=[
            pl.BlockSpec((PM_TM, SSM_WIDTH), lambda i: (tile(i), 0)),
            pl.BlockSpec((PM_TM, D_MODEL), lambda i: (tile(i), 0)),
            pl.BlockSpec((PM_TM, D_MODEL), lambda i: (tile(i), 1)),
            pl.BlockSpec((PM_TM, D_MODEL), lambda i: (tile(i), 0)),
            pl.BlockSpec((PM_TM, D_MODEL), lambda i: (jnp.minimum(i, npb - 1), 0)),
            pl.BlockSpec((PM_TM, D_MODEL), lambda i: (jnp.maximum(tile(i) - npb, 0), 0)),
            pl.BlockSpec((SSM_WIDTH, D_MODEL), const2, pipeline_mode=pl.Buffered(1)),
            pl.BlockSpec((SSM_WIDTH, D_MODEL), const2, pipeline_mode=pl.Buffered(1)),
            pl.BlockSpec((D_MODEL, D_MODEL), const2, pipeline_mode=pl.Buffered(1)),
            pl.BlockSpec((1, D_MODEL), const2),
            pl.BlockSpec((D_MODEL, 2 * ROUTE_LANES), const2),
            pl.BlockSpec((1, ROUTE_LANES), const2),
        ],
        out_specs=(
            pl.BlockSpec((PM_TM, D_MODEL), lambda i: (tile(i), 0)),
            pl.BlockSpec((PM_TM, D_PACK), lambda i: (tile(i), 0)),
            pl.BlockSpec((PM_TM, ROUTE_LANES), lambda i: (jnp.maximum(i - 1, 0), 0)),
            pl.BlockSpec((8, PM_TM), lambda i: (0, jnp.maximum(i - 1, 0))),
            pl.BlockSpec((1, ROUTE_LANES), const2),
        ),
        out_shape=(
            jax.ShapeDtypeStruct((T_ALL, D_MODEL), F32),
            jax.ShapeDtypeStruct((T_ALL, D_PACK), U32),
            jax.ShapeDtypeStruct((T_ALL, ROUTE_LANES), F32),
            jax.ShapeDtypeStruct((8, T_ALL), F32),
            jax.ShapeDtypeStruct((1, ROUTE_LANES), F32),
        ),
        scratch_shapes=[pltpu.VMEM((1, ROUTE_LANES), F32), pltpu.VMEM((PM_TM, ROUTE_LANES), F32)],
        compiler_params=_cparams(1),
        name="postmix",
    )(y_act, gates, gates, y_pool, xp, xs, wa, wb, wo, g_ffn, wr_cat, b_r)


SC_CH = 96


def _sc_workers():
    info = plsc.get_sparse_core_info()
    return info.num_cores, info.num_cores * info.num_subcores


def _sc_dispatch(tn, slots):
    n_cores, n_workers = _sc_workers()
    per_w = (T_ALL // SC_CH) // n_workers
    assert per_w * n_workers * SC_CH == T_ALL
    slots = slots.reshape(2, n_workers, per_w, SC_CH)

    @functools.partial(
        pl.kernel,
        mesh=plsc.VectorSubcoreMesh(core_axis_name="c", subcore_axis_name="s"),
        out_type=jax.ShapeDtypeStruct((N_SLOTS, D_PACK), U32),
        scratch_types=[pltpu.VMEM((2, per_w, SC_CH), I32), pltpu.VMEM((SC_CH, D_PACK), U32)],
    )
    def k(tn_hbm, slots_hbm, xs_hbm, idx_v, rows_v):
        wid = lax.axis_index("s") * n_cores + lax.axis_index("c")
        c0 = wid * per_w
        pltpu.sync_copy(slots_hbm.at[0, wid], idx_v.at[0])
        pltpu.sync_copy(slots_hbm.at[1, wid], idx_v.at[1])

        @pl.loop(0, per_w)
        def _(c):
            row0 = pl.multiple_of((c0 + c) * SC_CH, SC_CH)
            pltpu.sync_copy(tn_hbm.at[pl.ds(row0, SC_CH)], rows_v)
            pltpu.sync_copy(rows_v, xs_hbm.at[idx_v.at[0, c]])
            pltpu.sync_copy(rows_v, xs_hbm.at[idx_v.at[1, c]])

    return k(tn, slots)


def _sc_collect(ys, slots):
    n_cores, n_workers = _sc_workers()
    per_w = (N_ASSIGN // SC_CH) // n_workers
    assert per_w * n_workers * SC_CH == N_ASSIGN
    slots = slots.reshape(n_workers, per_w, SC_CH)

    @functools.partial(
        pl.kernel,
        mesh=plsc.VectorSubcoreMesh(core_axis_name="c", subcore_axis_name="s"),
        out_type=jax.ShapeDtypeStruct((N_ASSIGN, D_PACK), U32),
        scratch_types=[pltpu.VMEM((per_w, SC_CH), I32), pltpu.VMEM((SC_CH, D_PACK), U32)],
    )
    def k(ys_hbm, slots_hbm, out_hbm, idx_v, rows_v):
        wid = lax.axis_index("s") * n_cores + lax.axis_index("c")
        c0 = wid * per_w
        pltpu.sync_copy(slots_hbm.at[wid], idx_v)

        @pl.loop(0, per_w)
        def _(c):
            row0 = pl.multiple_of((c0 + c) * SC_CH, SC_CH)
            pltpu.sync_copy(ys_hbm.at[idx_v.at[c]], rows_v)
            pltpu.sync_copy(rows_v, out_hbm.at[pl.ds(row0, SC_CH)])

    return k(ys, slots)


W_PARTS = 2


def _expert_body(t0_ref, nt_ref, xs_hbm, wg_hbm, wu_hbm, wd_hbm, ys_hbm,
                 wg_ref, wu_ref, wd_ref, xb_ref, yb_ref, wgb_ref, wub_ref, wdb_ref, wsem, xsem, ysem):
    e = pl.program_id(0)
    n = nt_ref[e]
    g0 = t0_ref[e]
    ws = e & 1

    def w_copies(ex, slot):
        out = []
        for hbm, buf in ((wg_hbm, wg_ref), (wu_hbm, wu_ref), (wd_hbm, wd_ref)):
            rb = buf.shape[1] // W_PARTS
            for p in range(W_PARTS):
                out.append((pltpu.make_async_copy(hbm.at[ex, pl.ds(p * rb, rb)],
                                                  buf.at[slot, pl.ds(p * rb, rb)], wsem.at[slot]), p))
        return out

    @pl.when(e == 0)
    def _():
        for cp, p in w_copies(0, 0):
            cp.start(priority=p)

    @pl.when(e + 1 < MOE_EXPERTS)
    def _():
        for cp, p in w_copies(e + 1, 1 - ws):
            cp.start(priority=p)

    for cp, _ in w_copies(e, ws):
        cp.wait()

    def rows(j):
        return pl.ds(pl.multiple_of((g0 + j) * TME, TME), TME)

    def x_copy(j, s):
        return pltpu.make_async_copy(xs_hbm.at[rows(j)], xb_ref.at[s], xsem.at[s])

    def y_copy(j, s):
        return pltpu.make_async_copy(yb_ref.at[s], ys_hbm.at[rows(j)], ysem.at[s])

    @pl.when(n > 0)
    def _():
        x_copy(0, 0).start()
        wgb_ref[...] = wg_ref[ws].astype(BF16)
        wub_ref[...] = wu_ref[ws].astype(BF16)
        wdb_ref[...] = wd_ref[ws].astype(BF16)

        def tile(j, c):
            s = j & 1
            x_copy(j, s).wait()
            @pl.when(j + 1 < n)
            def _():
                x_copy(j + 1, 1 - s).start()
            @pl.when(j >= 2)
            def _():
                y_copy(j - 2, s).wait()
            x = _unpack_pairs(xb_ref[s], BF16)
            hg = jnp.dot(x, wgb_ref[...], preferred_element_type=F32)
            hu = jnp.dot(x, wub_ref[...], preferred_element_type=F32)
            act = (hg * _sigmoid(hg)) * hu
            yb_ref[s] = _pack_pairs(jnp.dot(act.astype(BF16), wdb_ref[...], preferred_element_type=F32))
            y_copy(j, s).start()
            return c
        lax.fori_loop(0, n, tile, 0)

        @pl.when(n >= 2)
        def _():
            y_copy(n - 2, n & 1).wait()
        y_copy(n - 1, (n - 1) & 1).wait()


def _experts(tile0, tiles, xs, w_eg, w_eu, w_ed):
    any_spec = pl.BlockSpec(memory_space=pl.ANY)
    grid_spec = pltpu.PrefetchScalarGridSpec(
        num_scalar_prefetch=2,
        grid=(MOE_EXPERTS,),
        in_specs=[any_spec] * 4,
        out_specs=any_spec,
        scratch_shapes=[
            pltpu.VMEM((2, D_MODEL, MOE_FF), F32),
            pltpu.VMEM((2, D_MODEL, MOE_FF), F32),
            pltpu.VMEM((2, MOE_FF, D_MODEL), F32),
            pltpu.VMEM((2, TME, D_PACK), U32),
            pltpu.VMEM((2, TME, D_PACK), U32),
            pltpu.VMEM((D_MODEL, MOE_FF), BF16),
            pltpu.VMEM((D_MODEL, MOE_FF), BF16),
            pltpu.VMEM((MOE_FF, D_MODEL), BF16),
            pltpu.SemaphoreType.DMA((2,)),
            pltpu.SemaphoreType.DMA((2,)),
            pltpu.SemaphoreType.DMA((2,)),
        ],
    )
    return pl.pallas_call(
        _expert_body,
        grid_spec=grid_spec,
        out_shape=jax.ShapeDtypeStruct((N_SLOTS, D_PACK), U32),
        compiler_params=_cparams(1),
        name="experts",
    )(tile0, tiles, xs, w_eg, w_eu, w_ed)


FN_TM = 256
FN_PROMPT_BLOCKS = T_PROMPT // FN_TM


def _final_body(h_ref, y0_ref, y1_ref, rt_ref, g_ref, op_ref, os_ref):
    i = pl.program_id(0)
    rt = rt_ref[...]
    z = (h_ref[...] + rt[:, 0:1] * _unpack_pairs(y0_ref[...], F32)
         + rt[:, 1:2] * _unpack_pairs(y1_ref[...], F32))
    inv = lax.rsqrt(jnp.mean(z * z, axis=-1, keepdims=True) + EPS)
    out = (z * inv) * g_ref[...]
    @pl.when(i < FN_PROMPT_BLOCKS)
    def _():
        op_ref[...] = out
    @pl.when(i >= FN_PROMPT_BLOCKS)
    def _():
        os_ref[...] = out


def _final(h, y, route, g_final):
    n = T_ALL // FN_TM
    npb = FN_PROMPT_BLOCKS
    yoff = T_ALL // FN_TM
    return pl.pallas_call(
        _final_body,
        grid=(n,),
        in_specs=[
            pl.BlockSpec((FN_TM, D_MODEL), lambda i: (i, 0)),
            pl.BlockSpec((FN_TM, D_PACK), lambda i: (i, 0)),
            pl.BlockSpec((FN_TM, D_PACK), lambda i: (yoff + i, 0)),
            pl.BlockSpec((FN_TM, ROUTE_LANES), lambda i: (i, 0)),
            pl.BlockSpec((1, D_MODEL), lambda i: (0, 0)),
        ],
        out_specs=(
            pl.BlockSpec((FN_TM, D_MODEL), lambda i: (jnp.minimum(i, npb - 1), 0)),
            pl.BlockSpec((FN_TM, D_MODEL), lambda i: (jnp.maximum(i - npb, 0), 0)),
        ),
        out_shape=(
            jax.ShapeDtypeStruct((T_PROMPT, D_MODEL), F32),
            jax.ShapeDtypeStruct((T_SAMPLE, D_MODEL), F32),
        ),
        compiler_params=_cparams(1),
        name="final",
    )(h, y, y, route, g_final)


def _dispatch_plan(route_t, cnt):
    counts = cnt[0, EXP_LANE0:EXP_LANE0 + MOE_EXPERTS].astype(I32)
    tiles = (counts + (TME - 1)) // TME
    cumt = jnp.cumsum(tiles)
    pad_off = (cumt - tiles) * TME
    rank = route_t[2:4].astype(I32)
    eid = route_t[4:6].astype(I32)
    onehot = eid[None] == jnp.arange(MOE_EXPERTS, dtype=I32)[:, None, None]
    slots = rank + jnp.sum(jnp.where(onehot, pad_off[:, None, None], 0), axis=0)
    return slots, cumt - tiles, tiles


def kernel(x_prompt, x_sample, state_pool, state_ssm_re, state_ssm_im, g_mix, w_in, w_pool,
           pool_scale, ssm_a_re, ssm_a_im, ssm_log_dt, ssm_b_re, ssm_b_im, ssm_c_re, ssm_c_im,
           ssm_d, w_glu_a, w_glu_b, w_out, g_ffn, w_router_group, b_router_group,
           w_router_expert, b_router_expert, w_exp_gate, w_exp_up, w_exp_down, g_final):
    l = 0
    xp = x_prompt.reshape(T_PROMPT, D_MODEL)
    xs = x_sample.transpose(1, 0, 2).reshape(T_SAMPLE, D_MODEL)
    w_in_bf = w_in[l].astype(BF16)
    w_pool_bf = w_pool[l].astype(BF16)
    g_mix2 = g_mix[l].reshape(1, D_MODEL)
    scale2 = pool_scale[l].reshape(1, D_MODEL)

    u, gates, wa_bf, wb_bf, wo_bf = _inproj(xp, g_mix2, w_in_bf, 0,
                                            cast=(w_glu_a[l], w_glu_b[l], w_out[l]))
    u, gates = _inproj(xs, g_mix2, w_in_bf, T_PROMPT // IN_TM, dst=(u, gates))

    y_pool, pool_tail = _pool_prompt(u, w_pool_bf, scale2)
    hist_tm = state_pool[l].transpose(1, 0, 2)
    y_pool, pool_buf_tm = _pool_sample(u, hist_tm, w_pool_bf, scale2, y_pool)

    tables = _ssm_tables(ssm_a_re[l], ssm_a_im[l], ssm_log_dt[l], ssm_b_re[l], ssm_b_im[l],
                         ssm_c_re[l], ssm_c_im[l], ssm_d[l])
    h0r = state_ssm_re[l].reshape(DEC_BATCH, SSM_GROUPS * SSM_STATE)
    h0i = state_ssm_im[l].reshape(DEC_BATCH, SSM_GROUPS * SSM_STATE)
    y_act, h_prompt, hs_re, hs_im = _ssm(u, h0r, h0i, tables)

    assert EXP_LANE0 == MOE_GROUPS
    w_r = jnp.concatenate([w_router_group[l], w_router_expert[l],
                           jnp.zeros((D_MODEL, ROUTE_LANES - EXP_LANE0 - MOE_EXPERTS), F32)], axis=1)
    wr_hi = w_r.astype(BF16)
    wr_cat = jnp.concatenate([wr_hi, (w_r - wr_hi.astype(F32)).astype(BF16)], axis=1)
    b_r = jnp.zeros((1, ROUTE_LANES), F32)
    b_r = b_r.at[0, :MOE_GROUPS].set(b_router_group[l])
    b_r = b_r.at[0, EXP_LANE0:EXP_LANE0 + MOE_EXPERTS].set(b_router_expert[l])

    h, tn, route, route_t, cnt = _postmix(y_act, gates, y_pool, xp, xs, wa_bf, wb_bf, wo_bf,
                                          g_ffn[l].reshape(1, D_MODEL), wr_cat, b_r)
    slots, tile0, tiles = _dispatch_plan(route_t, cnt)
    xs_sorted = _sc_dispatch(tn, slots)
    ys_sorted = _experts(tile0, tiles, xs_sorted, w_exp_gate[l], w_exp_up[l], w_exp_down[l])
    y = _sc_collect(ys_sorted, slots)
    yp, ys = _final(h, y, route, g_final.reshape(1, D_MODEL))

    y_prompt = yp.reshape(BATCH, SEQ, D_MODEL)
    y_sample = ys.reshape(DEC_SEQ, DEC_BATCH, D_MODEL).transpose(1, 0, 2)
    new_pool_prompt = pool_tail[:, HIST - POOL_BUF:, :][None]
    new_pool_sample = pool_buf_tm.transpose(1, 0, 2)[None]
    hp = h_prompt.reshape(BATCH, N_OCT, 2, OCT_GROUPS, SSM_STATE).transpose(2, 0, 1, 3, 4)
    hp = hp.reshape(2, BATCH, SSM_GROUPS, SSM_STATE)
    shp = (1, DEC_BATCH, SSM_GROUPS, SSM_STATE)
    return (y_prompt, y_sample, new_pool_prompt, hp[0][None], hp[1][None], new_pool_sample,
            hs_re.reshape(shp), hs_im.reshape(shp))
```

```python
import functools
import math

import jax
import jax.numpy as jnp
from jax import lax
from jax.experimental import pallas as pl
from jax.experimental.pallas import tpu as pltpu
from jax.experimental.pallas import tpu_sc as plsc

F32 = jnp.float32
BF16 = jnp.bfloat16
I32 = jnp.int32
U32 = jnp.uint32

D_MODEL = 2048
BATCH = 4
SEQ = 2048
DEC_BATCH = 128
DEC_SEQ = 8
PAST_LEN = 16384
POOL_WIDTH = D_MODEL // 2
POOL_WINDOWS = (2, 4, 8, 16)
POOL_GROUPS = len(POOL_WINDOWS)
POOL_GROUP_CH = POOL_WIDTH // POOL_GROUPS
POOL_OUT_CH = D_MODEL // POOL_GROUPS
POOL_BUF = max(POOL_WINDOWS) - 1
SSM_WIDTH = D_MODEL // 2
SSM_GROUP_CH = 16
SSM_GROUPS = SSM_WIDTH // SSM_GROUP_CH
SSM_STATE = 64
IN_WIDTH = POOL_WIDTH + SSM_WIDTH + 2 * D_MODEL
D_PACK = D_MODEL // 2
MOE_GROUPS = 4
MOE_EPG = 8
MOE_EXPERTS = MOE_GROUPS * MOE_EPG
MOE_FF = D_MODEL // 4
EPS = 1e-6

T_PROMPT = BATCH * SEQ
T_SAMPLE = DEC_BATCH * DEC_SEQ
T_ALL = T_PROMPT + T_SAMPLE
T_PAD = (BATCH + 1) * SEQ

LANES = 128
SUBLANES = 8
VMEM_LIMIT = 56 * 1024 * 1024

CHUNK = 8
OCT = LANES
N_OCT = SSM_WIDTH // OCT
OCT_GROUPS = OCT // SSM_GROUP_CH
OCT_STATES = OCT_GROUPS * SSM_STATE
CW = CHUNK * OCT
SW = 2 * OCT_STATES

ROUTE_LANES = LANES
EXP_LANE0 = MOE_GROUPS
N_ASSIGN = 2 * T_ALL
TME = 256
N_ITEMS_MAX = N_ASSIGN // TME + MOE_EXPERTS
N_SLOTS = N_ITEMS_MAX * TME


def _cparams(n_axes):
    return pltpu.CompilerParams(dimension_semantics=("arbitrary",) * n_axes,
                                vmem_limit_bytes=VMEM_LIMIT)


def _sigmoid(x):
    return 1.0 / (1.0 + jnp.exp(-x))


def _pack_pairs(x):
    c = x.shape[1] // 2
    hi = lax.bitcast_convert_type(x[:, :c].astype(BF16).astype(F32), U32)
    lo = lax.bitcast_convert_type(x[:, c:].astype(BF16).astype(F32), U32)
    return hi | (lo >> 16)


def _unpack_pairs(u, dtype):
    hi = lax.bitcast_convert_type(u & jnp.uint32(0xFFFF0000), F32)
    lo = lax.bitcast_convert_type(u << 16, F32)
    return jnp.concatenate([hi, lo], axis=1).astype(dtype)


def _gelu_tanh(x):
    c = math.sqrt(2.0 / math.pi)
    return 0.5 * x * (1.0 + jnp.tanh(c * (x + 0.044715 * (x * x * x))))


IN_TM = 1024
IN_TN = 1024
U_WIDTH = POOL_WIDTH + SSM_WIDTH
GATE_WIDTH = 2 * D_MODEL
IN_U_STEPS = U_WIDTH // IN_TN


CAST_RB = 128


def _inproj_body(n_alias, cast_ranges, x_ref, g_ref, w_ref, *rest):
    n_cast = len(cast_ranges)
    cast_in = rest[n_alias:n_alias + n_cast]
    u_ref, gate_ref = rest[n_alias + n_cast:n_alias + n_cast + 2]
    cast_out = rest[n_alias + n_cast + 2:-1]
    xn_ref = rest[-1]
    j = pl.program_id(1)
    step = pl.program_id(0) * pl.num_programs(1) + j
    for (lo, hi), ci, co in zip(cast_ranges, cast_in, cast_out):
        @pl.when((step >= lo) & (step < hi))
        def _():
            co[...] = ci[...].astype(BF16)
    @pl.when(j == 0)
    def _():
        x = x_ref[...]
        inv = lax.rsqrt(jnp.mean(x * x, axis=-1, keepdims=True) + EPS)
        xn_ref[...] = ((x * inv) * g_ref[...]).astype(BF16)
    acc = jnp.dot(xn_ref[...], w_ref[...], preferred_element_type=F32)
    @pl.when(j < IN_U_STEPS)
    def _():
        u_ref[...] = acc
    @pl.when(j >= IN_U_STEPS)
    def _():
        gate_ref[...] = _sigmoid(acc).astype(BF16)


def _inproj(x, g, w_bf, row_block0, dst=None, cast=()):
    n_i = x.shape[0] // IN_TM
    n_j = IN_WIDTH // IN_TN
    in_specs = [
        pl.BlockSpec((IN_TM, D_MODEL), lambda i, j: (i, 0)),
        pl.BlockSpec((1, D_MODEL), lambda i, j: (0, 0)),
        pl.BlockSpec((D_MODEL, IN_TN), lambda i, j: (0, j)),
    ]
    args = [x, g, w_bf]
    aliases = {}
    if dst is not None:
        in_specs += [pl.BlockSpec(memory_space=pl.ANY)] * 2
        args += list(dst)
        aliases = {3: 0, 4: 1}
    cast_specs, cast_shapes, cast_ranges, off = [], [], [], 0
    for wc in cast:
        nb = wc.shape[0] // CAST_RB
        cast_specs.append(pl.BlockSpec(
            (CAST_RB, wc.shape[1]), lambda i, j, o=off, nb=nb: (jnp.clip(i * n_j + j - o, 0, nb - 1), 0)))
        cast_shapes.append(jax.ShapeDtypeStruct(wc.shape, BF16))
        cast_ranges.append((off, off + nb))
        off += nb
    assert off <= n_i * n_j
    args += list(cast)
    return pl.pallas_call(
        functools.partial(_inproj_body, len(aliases), tuple(cast_ranges)),
        grid=(n_i, n_j),
        in_specs=in_specs + cast_specs,
        out_specs=(
            pl.BlockSpec((IN_TM, IN_TN), lambda i, j: (i + row_block0, jnp.minimum(j, IN_U_STEPS - 1))),
            pl.BlockSpec((IN_TM, IN_TN), lambda i, j: (i + row_block0, jnp.maximum(j - IN_U_STEPS, 0))),
            *cast_specs,
        ),
        out_shape=(
            jax.ShapeDtypeStruct((T_PAD, U_WIDTH), F32),
            jax.ShapeDtypeStruct((T_ALL, GATE_WIDTH), BF16),
            *cast_shapes,
        ),
        scratch_shapes=[pltpu.VMEM((IN_TM, D_MODEL), BF16)],
        input_output_aliases=aliases,
        compiler_params=_cparams(2),
        name="inproj",
    )(*args)


PP_TM = 512
HIST = 16


def _pool_project(pooled_g, g, w_ref, sc_ref, o_ref):
    y = jnp.dot(pooled_g.astype(BF16), w_ref[g], preferred_element_type=F32)
    lo, hi = g * POOL_OUT_CH, (g + 1) * POOL_OUT_CH
    o_ref[:, lo:hi] = (y * sc_ref[:, lo:hi]).astype(o_ref.dtype)


def _pool_prompt_body(u_ref, w_ref, sc_ref, o_ref, tail_ref, hist_ref):
    i = pl.program_id(1)
    @pl.when(i == 0)
    def _():
        hist_ref[...] = jnp.zeros_like(hist_ref)
    u = u_ref[...]
    ext = jnp.concatenate([hist_ref[...], u], axis=0)
    hist_ref[...] = u[PP_TM - HIST:, :]
    tail_ref[...] = u[PP_TM - HIST:, :]
    pos = i * PP_TM + lax.broadcasted_iota(I32, (PP_TM, 1), 0)
    for g, w in enumerate(POOL_WINDOWS):
        lo, hi = g * POOL_GROUP_CH, (g + 1) * POOL_GROUP_CH
        s = ext[:, lo:hi]
        d = 1
        while d < w:
            s = s + pltpu.roll(s, d, axis=0)
            d *= 2
        cnt = jnp.minimum(w, pos + 1).astype(F32)
        pooled = s[HIST:, :] / cnt - u[:, lo:hi]
        _pool_project(pooled, g, w_ref, sc_ref, o_ref)


def _pool_prompt(u, w_pool_bf, pool_scale):
    n_i = SEQ // PP_TM
    return pl.pallas_call(
        _pool_prompt_body,
        grid=(BATCH, n_i),
        in_specs=[
            pl.BlockSpec((PP_TM, POOL_WIDTH), lambda b, i: (b * n_i + i, 0)),
            pl.BlockSpec((POOL_GROUPS, POOL_GROUP_CH, POOL_OUT_CH), lambda b, i: (0, 0, 0)),
            pl.BlockSpec((1, D_MODEL), lambda b, i: (0, 0)),
        ],
        out_specs=(
            pl.BlockSpec((PP_TM, D_MODEL), lambda b, i: (b * n_i + i, 0)),
            pl.BlockSpec((None, HIST, POOL_WIDTH), lambda b, i: (b, 0, 0)),
        ),
        out_shape=(
            jax.ShapeDtypeStruct((T_ALL, D_MODEL), BF16),
            jax.ShapeDtypeStruct((BATCH, HIST, POOL_WIDTH), F32),
        ),
        scratch_shapes=[pltpu.VMEM((HIST, POOL_WIDTH), F32)],
        compiler_params=_cparams(2),
        name="pool_prompt",
    )(u, w_pool_bf, pool_scale)


def _pool_sample_body(u_ref, hist_ref, w_ref, sc_ref, _dst, o_ref, buf_ref):
    rows = [hist_ref[k] for k in range(POOL_BUF)]
    rows += [u_ref[DEC_BATCH * t:DEC_BATCH * (t + 1), :] for t in range(DEC_SEQ)]
    n = len(rows)
    for k in range(POOL_BUF):
        buf_ref[k] = rows[n - POOL_BUF + k]
    for g, w in enumerate(POOL_WINDOWS):
        lo, hi = g * POOL_GROUP_CH, (g + 1) * POOL_GROUP_CH
        f = [r[:, lo:hi] for r in rows]
        cur = f
        d = 1
        while d < w:
            cur = [cur[k] + cur[k - d] if k - d >= 0 else cur[k] for k in range(n)]
            d *= 2
        pooled = jnp.concatenate(
            [cur[POOL_BUF + t] / float(w) - f[POOL_BUF + t] for t in range(DEC_SEQ)], axis=0)
        _pool_project(pooled, g, w_ref, sc_ref, o_ref)


def _pool_sample(u, hist_tm, w_pool_bf, pool_scale, y_pool):
    blk = T_PROMPT // T_SAMPLE
    return pl.pallas_call(
        _pool_sample_body,
        grid=(1,),
        in_specs=[
            pl.BlockSpec((T_SAMPLE, POOL_WIDTH), lambda i: (blk, 0)),
            pl.BlockSpec((POOL_BUF, DEC_BATCH, POOL_WIDTH), lambda i: (0, 0, 0)),
            pl.BlockSpec((POOL_GROUPS, POOL_GROUP_CH, POOL_OUT_CH), lambda i: (0, 0, 0)),
            pl.BlockSpec((1, D_MODEL), lambda i: (0, 0)),
            pl.BlockSpec(memory_space=pl.ANY),
        ],
        out_specs=(
            pl.BlockSpec((T_SAMPLE, D_MODEL), lambda i: (blk, 0)),
            pl.BlockSpec((POOL_BUF, DEC_BATCH, POOL_WIDTH), lambda i: (0, 0, 0)),
        ),
        out_shape=(
            jax.ShapeDtypeStruct((T_ALL, D_MODEL), BF16),
            jax.ShapeDtypeStruct((POOL_BUF, DEC_BATCH, POOL_WIDTH), F32),
        ),
        input_output_aliases={4: 0},
        compiler_params=_cparams(1),
        name="pool_sample",
    )(u, hist_tm, w_pool_bf, pool_scale, y_pool)


def _ssm_tables(a_re, a_im, log_dt, b_re, b_im, c_re, c_im, d_skip):
    dt = jnp.exp(log_dt)[:, None]
    lr, li = a_re, a_im
    ab_re = jnp.exp(lr * dt) * jnp.cos(li * dt)
    ab_im = jnp.exp(lr * dt) * jnp.sin(li * dt)
    den = lr * lr + li * li
    nr, ni = ab_re - 1.0, ab_im
    q_re = (nr * lr + ni * li) / den
    q_im = (ni * lr - nr * li) / den
    bb_re = q_re[..., None] * b_re - q_im[..., None] * b_im
    bb_im = q_re[..., None] * b_im + q_im[..., None] * b_re

    def lam_rows(ks):
        k = jnp.asarray(ks, F32)[:, None, None]
        m = jnp.exp(k * lr * dt)
        re = (m * jnp.cos(k * li * dt)).reshape(len(ks), N_OCT, OCT_STATES)
        im = (m * jnp.sin(k * li * dt)).reshape(len(ks), N_OCT, OCT_STATES)
        return jnp.concatenate([re, im], axis=-1).transpose(1, 0, 2)

    def compact(re, im):
        v = jnp.concatenate([re, im], axis=-1)
        return v.reshape(N_OCT, OCT, 2 * SSM_STATE)

    bbc = compact(jnp.swapaxes(bb_re, 1, 2), jnp.swapaxes(bb_im, 1, 2))
    ccc = compact(c_re, c_im)
    pw = lam_rows(list(range(2 * SUBLANES)))
    r = jnp.arange(SUBLANES)[None, :, None]
    parts = [jnp.where(r >= dd, lam_rows([CHUNK * dd]), 0.0) for dd in (1, 2, 4)]
    parts.append(lam_rows([CHUNK * kk for kk in range(1, SUBLANES + 1)]))
    tab = jnp.concatenate(parts, axis=1)
    dsk = d_skip.reshape(N_OCT, 1, OCT)
    return bbc, ccc, pw, tab, dsk


def _split_bf16(x):
    hi = x.astype(BF16)
    return hi, (x - hi.astype(F32)).astype(BF16)


def _dot_nt(a, b):
    return lax.dot_general(a, b, (((1,), (1,)), ((), ())), preferred_element_type=F32)


def _build_weights(bbc_ref, ccc_ref, pw_ref, f_ref, gt_ref, m_ref):
    row_gi = lax.broadcasted_iota(I32, (OCT, 1), 0) >> 4
    col = lax.broadcasted_iota(I32, (1, SW), 1)
    col_gi = (col >> 6) & 7
    src = ((col >> 9) << 6) | (col & 63)
    k128 = lax.broadcasted_iota(I32, (2 * SSM_STATE, 1), 0)
    spread = jnp.where(k128 == src, 1.0, 0.0).astype(BF16)
    diag = row_gi == col_gi

    def expand(c_ref):
        hi, lo = _split_bf16(c_ref[...])
        d = (jnp.dot(hi, spread, preferred_element_type=F32)
             + jnp.dot(lo, spread, preferred_element_type=F32))
        d = jnp.where(diag, d, 0.0)
        return d[:, :OCT_STATES], d[:, OCT_STATES:]

    br, bi = expand(bbc_ref)
    cr, ci = expand(ccc_ref)
    chi_r, clo_r = _split_bf16(cr)
    chi_i, clo_i = _split_bf16(ci)

    def lam(k):
        return pw_ref[k:k + 1, :OCT_STATES], pw_ref[k:k + 1, OCT_STATES:]

    def dot3(a, bhi, blo):
        ahi, alo = _split_bf16(a)
        return _dot_nt(ahi, bhi) + _dot_nt(alo, bhi) + _dot_nt(ahi, blo)

    lags = []
    for k in range(CHUNK):
        pr, pi_ = lam(k)
        fr, fi = _cmul(br, bi, pr, pi_)
        s = CHUNK - 1 - k
        f_ref[s * OCT:(s + 1) * OCT, :] = jnp.concatenate([fr, fi], axis=1).astype(BF16)
        lags.append((dot3(fr, chi_r, clo_r) - dot3(fi, chi_i, clo_i)).astype(BF16))
        pr, pi_ = lam(k + 1)
        gr, gi = _cmul(cr, ci, pr, pi_)
        gt_ref[k * OCT:(k + 1) * OCT, :] = jnp.concatenate([gr, -gi], axis=1).astype(BF16)
    zero = jnp.zeros((OCT, OCT), BF16)
    for s in range(CHUNK):
        for t in range(CHUNK):
            m_ref[s * OCT:(s + 1) * OCT, t * OCT:(t + 1) * OCT] = lags[t - s] if t >= s else zero


def _cmul(ar, ai, br, bi):
    return ar * br - ai * bi, ar * bi + ai * br


def _chunk_scan(sloc, tab_ref):
    R = sloc.shape[0]
    nb = R // SUBLANES
    sr, si = sloc[:, :OCT_STATES], sloc[:, OCT_STATES:]
    rowi = lax.broadcasted_iota(I32, (R, 1), 0)
    tr = jnp.where(rowi == 0, 0.0, pltpu.roll(sr, 1, axis=0))
    ti = jnp.where(rowi == 0, 0.0, pltpu.roll(si, 1, axis=0))
    for lvl, d in enumerate((1, 2, 4)):
        mr = tab_ref[lvl * SUBLANES:(lvl + 1) * SUBLANES, :OCT_STATES]
        mi = tab_ref[lvl * SUBLANES:(lvl + 1) * SUBLANES, OCT_STATES:]
        mr = jnp.concatenate([mr] * nb, axis=0)
        mi = jnp.concatenate([mi] * nb, axis=0)
        pr, pi_ = _cmul(mr, mi, pltpu.roll(tr, d, axis=0), pltpu.roll(ti, d, axis=0))
        tr, ti = tr + pr, ti + pi_
    pwr = tab_ref[3 * SUBLANES:4 * SUBLANES, :OCT_STATES]
    pwi = tab_ref[3 * SUBLANES:4 * SUBLANES, OCT_STATES:]
    cr = jnp.zeros((1, OCT_STATES), F32)
    ci = jnp.zeros((1, OCT_STATES), F32)
    out_r, out_i = [], []
    for k in range(nb):
        ar = tr[k * SUBLANES:(k + 1) * SUBLANES, :]
        ai = ti[k * SUBLANES:(k + 1) * SUBLANES, :]
        pr, pi_ = _cmul(pwr, pwi, jnp.broadcast_to(cr, ar.shape), jnp.broadcast_to(ci, ai.shape))
        hr, hi = ar + pr, ai + pi_
        out_r.append(hr)
        out_i.append(hi)
        cr, ci = hr[SUBLANES - 1:, :], hi[SUBLANES - 1:, :]
    hin = jnp.concatenate([jnp.concatenate(out_r, axis=0), jnp.concatenate(out_i, axis=0)], axis=1)
    lr, li = pwr[0:1, :], pwi[0:1, :]
    fr, fi = _cmul(lr, li, cr, ci)
    fin = jnp.concatenate([fr + sr[R - 1:, :], fi + si[R - 1:, :]], axis=1)
    return hin, fin


def _ssm_body(u_ref, h0r_ref, h0i_ref, bbc_ref, ccc_ref, pw_ref, tab_ref, d_ref,
              y_ref, hout_ref, hr_ref, hi_ref, f_ref, gt_ref, m_ref):
    b = pl.program_id(1)

    @pl.when(b == 0)
    def _():
        _build_weights(bbc_ref, ccc_ref, pw_ref, f_ref, gt_ref, m_ref)

    def outputs(xs, xb, hin_bf):
        y = (jnp.dot(xb, m_ref[...], preferred_element_type=F32) + _dot_nt(hin_bf, gt_ref[...]))
        return [_gelu_tanh(y[:, t * OCT:(t + 1) * OCT] + d_ref[...] * xs[t]) for t in range(CHUNK)]

    @pl.when(b < BATCH)
    def _():
        R = SEQ // CHUNK
        xs = [u_ref[pl.ds(s, R, stride=CHUNK), :] for s in range(CHUNK)]
        xb = jnp.concatenate(xs, axis=1).astype(BF16)
        sloc = jnp.dot(xb, f_ref[...], preferred_element_type=F32)
        hin, fin = _chunk_scan(sloc, tab_ref)
        for t, yt in enumerate(outputs(xs, xb, hin.astype(BF16))):
            y_ref[pl.ds(t, R, stride=CHUNK), :] = yt
        hout_ref[...] = fin

    @pl.when(b == BATCH)
    def _():
        B = DEC_BATCH
        xs = [u_ref[B * s:B * (s + 1), :] for s in range(CHUNK)]
        xb = jnp.concatenate(xs, axis=1).astype(BF16)
        sloc = jnp.dot(xb, f_ref[...], preferred_element_type=F32)
        h0r, h0i = h0r_ref[...], h0i_ref[...]
        hin = jnp.concatenate([h0r, h0i], axis=1).astype(BF16)
        for t, yt in enumerate(outputs(xs, xb, hin)):
            y_ref[B * t:B * (t + 1), :] = yt
        lr = tab_ref[3 * SUBLANES:3 * SUBLANES + 1, :OCT_STATES]
        li = tab_ref[3 * SUBLANES:3 * SUBLANES + 1, OCT_STATES:]
        nr, ni = _cmul(lr, li, h0r, h0i)
        hr_ref[...] = nr + sloc[:, :OCT_STATES]
        hi_ref[...] = ni + sloc[:, OCT_STATES:]


def _ssm(u, h0r, h0i, tables):
    col0 = POOL_WIDTH // OCT
    im3 = lambda o, b: (o, 0, 0)
    st_spec = pl.BlockSpec((DEC_BATCH, OCT_STATES), lambda o, b: (0, o))
    return pl.pallas_call(
        _ssm_body,
        grid=(N_OCT, BATCH + 1),
        in_specs=[
            pl.BlockSpec((SEQ, OCT), lambda o, b: (b, col0 + o)), st_spec, st_spec,
            pl.BlockSpec((None, OCT, 2 * SSM_STATE), im3),
            pl.BlockSpec((None, OCT, 2 * SSM_STATE), im3),
            pl.BlockSpec((None, 2 * SUBLANES, SW), im3),
            pl.BlockSpec((None, 4 * SUBLANES, SW), im3),
            pl.BlockSpec((None, 1, OCT), im3),
        ],
        out_specs=(
            pl.BlockSpec((SEQ, OCT), lambda o, b: (b, o)),
            pl.BlockSpec((None, 1, SW), lambda o, b: (jnp.minimum(b, BATCH - 1) * N_OCT + o, 0, 0)),
            st_spec, st_spec,
        ),
        out_shape=(
            jax.ShapeDtypeStruct((T_PAD, SSM_WIDTH), F32),
            jax.ShapeDtypeStruct((BATCH * N_OCT, 1, SW), F32),
            jax.ShapeDtypeStruct((DEC_BATCH, SSM_GROUPS * SSM_STATE), F32),
            jax.ShapeDtypeStruct((DEC_BATCH, SSM_GROUPS * SSM_STATE), F32),
        ),
        scratch_shapes=[pltpu.VMEM((CW, SW), BF16), pltpu.VMEM((CW, SW), BF16), pltpu.VMEM((CW, CW), BF16)],
        compiler_params=_cparams(2),
        name="ssm",
    )(u, h0r, h0i, *tables)


PM_TM = 256
PM_PROMPT_BLOCKS = T_PROMPT // PM_TM
PM_STEPS = T_ALL // PM_TM


def _route(logits, valid, cnt_ref):
    lane = lax.broadcasted_iota(I32, (PM_TM, ROUTE_LANES), 1)
    neg = jnp.float32(-jnp.inf)
    big = jnp.int32(1 << 20)
    is_g = lane < MOE_GROUPS
    gmax = jnp.max(jnp.where(is_g, logits, neg), axis=1, keepdims=True)
    g_idx = jnp.min(jnp.where(is_g & (logits == gmax), lane, big), axis=1, keepdims=True)
    g_den = jnp.sum(jnp.where(is_g, jnp.exp(logits - gmax), 0.0), axis=1, keepdims=True)
    g_val = 1.0 / g_den
    e_lane = lane - EXP_LANE0
    sel = (e_lane >= 0) & (e_lane < MOE_EXPERTS) & ((e_lane >> 3) == g_idx)
    m1 = jnp.max(jnp.where(sel, logits, neg), axis=1, keepdims=True)
    i1 = jnp.min(jnp.where(sel & (logits == m1), lane, big), axis=1, keepdims=True)
    sel2 = sel & (lane != i1)
    m2 = jnp.max(jnp.where(sel2, logits, neg), axis=1, keepdims=True)
    i2 = jnp.min(jnp.where(sel2 & (logits == m2), lane, big), axis=1, keepdims=True)
    e2 = jnp.exp(m2 - m1)
    w1 = g_val / (1.0 + e2)
    w2 = g_val * e2 / (1.0 + e2)
    oh1 = lane == i1
    oh2 = lane == i2
    oh = jnp.where(oh1 | oh2, valid, 0.0)
    rr = lax.broadcasted_iota(I32, (PM_TM, PM_TM), 0)
    cc = lax.broadcasted_iota(I32, (PM_TM, PM_TM), 1)
    tri = jnp.where(cc < rr, 1.0, 0.0).astype(BF16)
    base = cnt_ref[...] + jnp.dot(tri, oh.astype(BF16), preferred_element_type=F32)
    rank1 = jnp.sum(jnp.where(oh1, base, 0.0), axis=1, keepdims=True)
    rank2 = jnp.sum(jnp.where(oh2, base, 0.0), axis=1, keepdims=True)
    cnt_ref[...] = cnt_ref[...] + jnp.sum(oh, axis=0, keepdims=True)
    rt = jnp.where(lane == 0, w1, 0.0)
    rt = jnp.where(lane == 1, w2, rt)
    rt = jnp.where(lane == 2, rank1, rt)
    rt = jnp.where(lane == 3, rank2, rt)
    rt = jnp.where(lane == 4, (i1 - EXP_LANE0).astype(F32), rt)
    rt = jnp.where(lane == 5, (i2 - EXP_LANE0).astype(F32), rt)
    return rt


def _postmix_body(ya_ref, gp_ref, gs_ref, yp_ref, xp_ref, xs_ref, wa_ref, wb_ref, wo_ref,
                  gf_ref, wr_ref, br_ref, h_ref, tn_ref, rt_ref, rtt_ref, cnt_out_ref,
                  cnt_ref, lg_ref):
    i = pl.program_id(0)
    @pl.when(i == 0)
    def _():
        cnt_ref[...] = jnp.zeros_like(cnt_ref)
        lg_ref[...] = jnp.zeros_like(lg_ref)
    prev_logits = lg_ref[...]

    ya = ya_ref[...].astype(BF16)
    a = jnp.dot(ya, wa_ref[...], preferred_element_type=F32)
    bg = jnp.dot(ya, wb_ref[...], preferred_element_type=F32)
    y_ssm = a * _sigmoid(bg)
    merged = (gp_ref[...].astype(F32) * yp_ref[...].astype(F32)
              + gs_ref[...].astype(F32) * y_ssm)
    x = jnp.where(jnp.minimum(i, PM_STEPS - 1) < PM_PROMPT_BLOCKS, xp_ref[...], xs_ref[...])
    h = x + jnp.dot(merged.astype(BF16), wo_ref[...], preferred_element_type=F32)
    h_ref[...] = h
    inv = lax.rsqrt(jnp.mean(h * h, axis=-1, keepdims=True) + EPS)
    tn = (h * inv) * gf_ref[...]
    tn_ref[...] = _pack_pairs(tn)
    t_hi = tn.astype(BF16)
    t_lo = (tn - t_hi.astype(F32)).astype(BF16)
    hh = jnp.dot(t_hi, wr_ref[...], preferred_element_type=F32)
    lh = jnp.dot(t_lo, wr_ref[:, :ROUTE_LANES], preferred_element_type=F32)
    lg_ref[...] = (hh[:, :ROUTE_LANES] + lh + hh[:, ROUTE_LANES:]) + br_ref[...]
    rt = _route(prev_logits, jnp.where(i > 0, 1.0, 0.0), cnt_ref)
    rt_ref[...] = rt
    rtt_ref[...] = rt.T[:8, :]
    cnt_out_ref[...] = cnt_ref[...]


def _postmix(y_act, gates, y_pool, xp, xs, wa, wb, wo, g_ffn, wr_cat, b_r):
    npb = PM_PROMPT_BLOCKS
    const2 = lambda i: (0, 0)
    tile = lambda i: jnp.minimum(i, PM_STEPS - 1)
    return pl.pallas_call(
        _postmix_body,
        grid=(PM_STEPS + 1,),
        in_specs=[
            pl.BlockSpec((PM_TM, SSM_WIDTH), lambda i: (tile(i), 0)),
            pl.BlockSpec((PM_TM, D_MODEL), lambda i: (tile(i), 0)),
            pl.BlockSpec((PM_TM, D_MODEL), lambda i: (tile(i), 1)),
            pl.BlockSpec((PM_TM, D_MODEL), lambda i: (tile(i), 0)),
            pl.BlockSpec((PM_TM, D_MODEL), lambda i: (jnp.minimum(i, npb - 1), 0)),
            pl.BlockSpec((PM_TM, D_MODEL), lambda i: (jnp.maximum(tile(i) - npb, 0), 0)),
            pl.BlockSpec((SSM_WIDTH, D_MODEL), const2, pipeline_mode=pl.Buffered(1)),
            pl.BlockSpec((SSM_WIDTH, D_MODEL), const2, pipeline_mode=pl.Buffered(1)),
            pl.BlockSpec((D_MODEL, D_MODEL), const2, pipeline_mode=pl.Buffered(1)),
            pl.BlockSpec((1, D_MODEL), const2),
            pl.BlockSpec((D_MODEL, 2 * ROUTE_LANES), const2),
            pl.BlockSpec((1, ROUTE_LANES), const2),
        ],
        out_specs=(
            pl.BlockSpec((PM_TM, D_MODEL), lambda i: (tile(i), 0)),
            pl.BlockSpec((PM_TM, D_PACK), lambda i: (tile(i), 0)),
            pl.BlockSpec((PM_TM, ROUTE_LANES), lambda i: (jnp.maximum(i - 1, 0), 0)),
            pl.BlockSpec((8, PM_TM), lambda i: (0, jnp.maximum(i - 1, 0))),
            pl.BlockSpec((1, ROUTE_LANES), const2),
        ),
        out_shape=(
            jax.ShapeDtypeStruct((T_ALL, D_MODEL), F32),
            jax.ShapeDtypeStruct((T_ALL, D_PACK), U32),
            jax.ShapeDtypeStruct((T_ALL, ROUTE_LANES), F32),
            jax.ShapeDtypeStruct((8, T_ALL), F32),
            jax.ShapeDtypeStruct((1, ROUTE_LANES), F32),
        ),
        scratch_shapes=[pltpu.VMEM((1, ROUTE_LANES), F32), pltpu.VMEM((PM_TM, ROUTE_LANES), F32)],
        compiler_params=_cparams(1),
        name="postmix",
    )(y_act, gates, gates, y_pool, xp, xs, wa, wb, wo, g_ffn, wr_cat, b_r)


SC_CH = 96


def _sc_workers():
    info = plsc.get_sparse_core_info()
    return info.num_cores, info.num_cores * info.num_subcores


def _sc_dispatch(tn, slots):
    n_cores, n_workers = _sc_workers()
    per_w = (T_ALL // SC_CH) // n_workers
    assert per_w * n_workers * SC_CH == T_ALL
    slots = slots.reshape(2, n_workers, per_w, SC_CH)

    @functools.partial(
        pl.kernel,
        mesh=plsc.VectorSubcoreMesh(core_axis_name="c", subcore_axis_name="s"),
        out_type=jax.ShapeDtypeStruct((N_SLOTS, D_PACK), U32),
        scratch_types=[pltpu.VMEM((2, per_w, SC_CH), I32), pltpu.VMEM((SC_CH, D_PACK), U32)],
    )
    def k(tn_hbm, slots_hbm, xs_hbm, idx_v, rows_v):
        wid = lax.axis_index("s") * n_cores + lax.axis_index("c")
        c0 = wid * per_w
        pltpu.sync_copy(slots_hbm.at[0, wid], idx_v.at[0])
        pltpu.sync_copy(slots_hbm.at[1, wid], idx_v.at[1])

        @pl.loop(0, per_w)
        def _(c):
            row0 = pl.multiple_of((c0 + c) * SC_CH, SC_CH)
            pltpu.sync_copy(tn_hbm.at[pl.ds(row0, SC_CH)], rows_v)
            pltpu.sync_copy(rows_v, xs_hbm.at[idx_v.at[0, c]])
            pltpu.sync_copy(rows_v, xs_hbm.at[idx_v.at[1, c]])

    return k(tn, slots)


def _sc_collect(ys, slots):
    n_cores, n_workers = _sc_workers()
    per_w = (N_ASSIGN // SC_CH) // n_workers
    assert per_w * n_workers * SC_CH == N_ASSIGN
    slots = slots.reshape(n_workers, per_w, SC_CH)

    @functools.partial(
        pl.kernel,
        mesh=plsc.VectorSubcoreMesh(core_axis_name="c", subcore_axis_name="s"),
        out_type=jax.ShapeDtypeStruct((N_ASSIGN, D_PACK), U32),
        scratch_types=[pltpu.VMEM((per_w, SC_CH), I32), pltpu.VMEM((SC_CH, D_PACK), U32)],
    )
    def k(ys_hbm, slots_hbm, out_hbm, idx_v, rows_v):
        wid = lax.axis_index("s") * n_cores + lax.axis_index("c")
        c0 = wid * per_w
        pltpu.sync_copy(slots_hbm.at[wid], idx_v)

        @pl.loop(0, per_w)
        def _(c):
            row0 = pl.multiple_of((c0 + c) * SC_CH, SC_CH)
            pltpu.sync_copy(ys_hbm.at[idx_v.at[c]], rows_v)
            pltpu.sync_copy(rows_v, out_hbm.at[pl.ds(row0, SC_CH)])

    return k(ys, slots)


W_PARTS = 2


def _expert_body(t0_ref, nt_ref, xs_hbm, wg_hbm, wu_hbm, wd_hbm, ys_hbm,
                 wg_ref, wu_ref, wd_ref, xb_ref, yb_ref, wgb_ref, wub_ref, wdb_ref, wsem, xsem, ysem):
    e = pl.program_id(0)
    n = nt_ref[e]
    g0 = t0_ref[e]
    ws = e & 1

    def w_copies(ex, slot):
        out = []
        for hbm, buf in ((wg_hbm, wg_ref), (wu_hbm, wu_ref), (wd_hbm, wd_ref)):
            rb = buf.shape[1] // W_PARTS
            for p in range(W_PARTS):
                out.append((pltpu.make_async_copy(hbm.at[ex, pl.ds(p * rb, rb)],
                                                  buf.at[slot, pl.ds(p * rb, rb)], wsem.at[slot]), p))
        return out

    @pl.when(e == 0)
    def _():
        for cp, p in w_copies(0, 0):
            cp.start(priority=p)

    @pl.when(e + 1 < MOE_EXPERTS)
    def _():
        for cp, p in w_copies(e + 1, 1 - ws):
            cp.start(priority=p)

    for cp, _ in w_copies(e, ws):
        cp.wait()

    def rows(j):
        return pl.ds(pl.multiple_of((g0 + j) * TME, TME), TME)

    def x_copy(j, s):
        return pltpu.make_async_copy(xs_hbm.at[rows(j)], xb_ref.at[s], xsem.at[s])

    def y_copy(j, s):
        return pltpu.make_async_copy(yb_ref.at[s], ys_hbm.at[rows(j)], ysem.at[s])

    @pl.when(n > 0)
    def _():
        x_copy(0, 0).start()
        wgb_ref[...] = wg_ref[ws].astype(BF16)
        wub_ref[...] = wu_ref[ws].astype(BF16)
        wdb_ref[...] = wd_ref[ws].astype(BF16)

        def tile(j, c):
            s = j & 1
            x_copy(j, s).wait()
            @pl.when(j + 1 < n)
            def _():
                x_copy(j + 1, 1 - s).start()
            @pl.when(j >= 2)
            def _():
                y_copy(j - 2, s).wait()
            x = _unpack_pairs(xb_ref[s], BF16)
            hg = jnp.dot(x, wgb_ref[...], preferred_element_type=F32)
            hu = jnp.dot(x, wub_ref[...], preferred_element_type=F32)
            act = (hg * _sigmoid(hg)) * hu
            yb_ref[s] = _pack_pairs(jnp.dot(act.astype(BF16), wdb_ref[...], preferred_element_type=F32))
            y_copy(j, s).start()
            return c
        lax.fori_loop(0, n, tile, 0)

        @pl.when(n >= 2)
        def _():
            y_copy(n - 2, n & 1).wait()
        y_copy(n - 1, (n - 1) & 1).wait()


def _experts(tile0, tiles, xs, w_eg, w_eu, w_ed):
    any_spec = pl.BlockSpec(memory_space=pl.ANY)
    grid_spec = pltpu.PrefetchScalarGridSpec(
        num_scalar_prefetch=2,
        grid=(MOE_EXPERTS,),
        in_specs=[any_spec] * 4,
        out_specs=any_spec,
        scratch_shapes=[
            pltpu.VMEM((2, D_MODEL, MOE_FF), F32),
            pltpu.VMEM((2, D_MODEL, MOE_FF), F32),
            pltpu.VMEM((2, MOE_FF, D_MODEL), F32),
            pltpu.VMEM((2, TME, D_PACK), U32),
            pltpu.VMEM((2, TME, D_PACK), U32),
            pltpu.VMEM((D_MODEL, MOE_FF), BF16),
            pltpu.VMEM((D_MODEL, MOE_FF), BF16),
            pltpu.VMEM((MOE_FF, D_MODEL), BF16),
            pltpu.SemaphoreType.DMA((2,)),
            pltpu.SemaphoreType.DMA((2,)),
            pltpu.SemaphoreType.DMA((2,)),
        ],
    )
    return pl.pallas_call(
        _expert_body,
        grid_spec=grid_spec,
        out_shape=jax.ShapeDtypeStruct((N_SLOTS, D_PACK), U32),
        compiler_params=_cparams(1),
        name="experts",
    )(tile0, tiles, xs, w_eg, w_eu, w_ed)


FN_TM = 256
FN_PROMPT_BLOCKS = T_PROMPT // FN_TM


def _final_body(h_ref, y0_ref, y1_ref, rt_ref, g_ref, op_ref, os_ref):
    i = pl.program_id(0)
    rt = rt_ref[...]
    z = (h_ref[...] + rt[:, 0:1] * _unpack_pairs(y0_ref[...], F32)
         + rt[:, 1:2] * _unpack_pairs(y1_ref[...], F32))
    inv = lax.rsqrt(jnp.mean(z * z, axis=-1, keepdims=True) + EPS)
    out = (z * inv) * g_ref[...]
    @pl.when(i < FN_PROMPT_BLOCKS)
    def _():
        op_ref[...] = out
    @pl.when(i >= FN_PROMPT_BLOCKS)
    def _():
        os_ref[...] = out


def _final(h, y, route, g_final):
    n = T_ALL // FN_TM
    npb = FN_PROMPT_BLOCKS
    yoff = T_ALL // FN_TM
    return pl.pallas_call(
        _final_body,
        grid=(n,),
        in_specs=[
            pl.BlockSpec((FN_TM, D_MODEL), lambda i: (i, 0)),
            pl.BlockSpec((FN_TM, D_PACK), lambda i: (i, 0)),
            pl.BlockSpec((FN_TM, D_PACK), lambda i: (yoff + i, 0)),
            pl.BlockSpec((FN_TM, ROUTE_LANES), lambda i: (i, 0)),
            pl.BlockSpec((1, D_MODEL), lambda i: (0, 0)),
        ],
        out_specs=(
            pl.BlockSpec((FN_TM, D_MODEL), lambda i: (jnp.minimum(i, npb - 1), 0)),
            pl.BlockSpec((FN_TM, D_MODEL), lambda i: (jnp.maximum(i - npb, 0), 0)),
        ),
        out_shape=(
            jax.ShapeDtypeStruct((T_PROMPT, D_MODEL), F32),
            jax.ShapeDtypeStruct((T_SAMPLE, D_MODEL), F32),
        ),
        compiler_params=_cparams(1),
        name="final",
    )(h, y, y, route, g_final)


def _dispatch_plan(route_t, cnt):
    counts = cnt[0, EXP_LANE0:EXP_LANE0 + MOE_EXPERTS].astype(I32)
    tiles = (counts + (TME - 1)) // TME
    cumt = jnp.cumsum(tiles)
    pad_off = (cumt - tiles) * TME
    rank = route_t[2:4].astype(I32)
    eid = route_t[4:6].astype(I32)
    onehot = eid[None] == jnp.arange(MOE_EXPERTS, dtype=I32)[:, None, None]
    slots = rank + jnp.sum(jnp.where(onehot, pad_off[:, None, None], 0), axis=0)
    return slots, cumt - tiles, tiles


def kernel(x_prompt, x_sample, state_pool, state_ssm_re, state_ssm_im, g_mix, w_in, w_pool,
           pool_scale, ssm_a_re, ssm_a_im, ssm_log_dt, ssm_b_re, ssm_b_im, ssm_c_re, ssm_c_im,
           ssm_d, w_glu_a, w_glu_b, w_out, g_ffn, w_router_group, b_router_group,
           w_router_expert, b_router_expert, w_exp_gate, w_exp_up, w_exp_down, g_final):
    l = 0
    xp = x_prompt.reshape(T_PROMPT, D_MODEL)
    xs = x_sample.transpose(1, 0, 2).reshape(T_SAMPLE, D_MODEL)
    w_in_bf = w_in[l].astype(BF16)
    w_pool_bf = w_pool[l].astype(BF16)
    g_mix2 = g_mix[l].reshape(1, D_MODEL)
    scale2 = pool_scale[l].reshape(1, D_MODEL)

    u, gates, wa_bf, wb_bf, wo_bf = _inproj(xp, g_mix2, w_in_bf, 0,
                                            cast=(w_glu_a[l], w_glu_b[l], w_out[l]))
    u, gates = _inproj(xs, g_mix2, w_in_bf, T_PROMPT // IN_TM, dst=(u, gates))

    y_pool, pool_tail = _pool_prompt(u, w_pool_bf, scale2)
    hist_tm = state_pool[l].transpose(1, 0, 2)
    y_pool, pool_buf_tm = _pool_sample(u, hist_tm, w_pool_bf, scale2, y_pool)

    tables = _ssm_tables(ssm_a_re[l], ssm_a_im[l], ssm_log_dt[l], ssm_b_re[l], ssm_b_im[l],
                         ssm_c_re[l], ssm_c_im[l], ssm_d[l])
    h0r = state_ssm_re[l].reshape(DEC_BATCH, SSM_GROUPS * SSM_STATE)
    h0i = state_ssm_im[l].reshape(DEC_BATCH, SSM_GROUPS * SSM_STATE)
    y_act, h_prompt, hs_re, hs_im = _ssm(u, h0r, h0i, tables)

    assert EXP_LANE0 == MOE_GROUPS
    w_r = jnp.concatenate([w_router_group[l], w_router_expert[l],
                           jnp.zeros((D_MODEL, ROUTE_LANES - EXP_LANE0 - MOE_EXPERTS), F32)], axis=1)
    wr_hi = w_r.astype(BF16)
    wr_cat = jnp.concatenate([wr_hi, (w_r - wr_hi.astype(F32)).astype(BF16)], axis=1)
    b_r = jnp.zeros((1, ROUTE_LANES), F32)
    b_r = b_r.at[0, :MOE_GROUPS].set(b_router_group[l])
    b_r = b_r.at[0, EXP_LANE0:EXP_LANE0 + MOE_EXPERTS].set(b_router_expert[l])

    h, tn, route, route_t, cnt = _postmix(y_act, gates, y_pool, xp, xs, wa_bf, wb_bf, wo_bf,
                                          g_ffn[l].reshape(1, D_MODEL), wr_cat, b_r)
    slots, tile0, tiles = _dispatch_plan(route_t, cnt)
    xs_sorted = _sc_dispatch(tn, slots)
    ys_sorted = _experts(tile0, tiles, xs_sorted, w_exp_gate[l], w_exp_up[l], w_exp_down[l])
    y = _sc_collect(ys_sorted, slots)
    yp, ys = _final(h, y, route, g_final.reshape(1, D_MODEL))

    y_prompt = yp.reshape(BATCH, SEQ, D_MODEL)
    y_sample = ys.reshape(DEC_SEQ, DEC_BATCH, D_MODEL).transpose(1, 0, 2)
    new_pool_prompt = pool_tail[:, HIST - POOL_BUF:, :][None]
    new_pool_sample = pool_buf_tm.transpose(1, 0, 2)[None]
    hp = h_prompt.reshape(BATCH, N_OCT, 2, OCT_GROUPS, SSM_STATE).transpose(2, 0, 1, 3, 4)
    hp = hp.reshape(2, BATCH, SSM_GROUPS, SSM_STATE)
    shp = (1, DEC_BATCH, SSM_GROUPS, SSM_STATE)
    return (y_prompt, y_sample, new_pool_prompt, hp[0][None], hp[1][None], new_pool_sample,
            hs_re.reshape(shp), hs_im.reshape(shp))
```

```python
import functools
import math

import jax
import jax.numpy as jnp
from jax import lax
from jax.experimental import pallas as pl
from jax.experimental.pallas import tpu as pltpu
from jax.experimental.pallas import tpu_sc as plsc

F32 = jnp.float32
BF16 = jnp.bfloat16
I32 = jnp.int32
U32 = jnp.uint32

D_MODEL = 2048
BATCH = 4
SEQ = 2048
DEC_BATCH = 128
DEC_SEQ = 8
PAST_LEN = 16384
POOL_WIDTH = D_MODEL // 2
POOL_WINDOWS = (2, 4, 8, 16)
POOL_GROUPS = len(POOL_WINDOWS)
POOL_GROUP_CH = POOL_WIDTH // POOL_GROUPS
POOL_OUT_CH = D_MODEL // POOL_GROUPS
POOL_BUF = max(POOL_WINDOWS) - 1
SSM_WIDTH = D_MODEL // 2
SSM_GROUP_CH = 16
SSM_GROUPS = SSM_WIDTH // SSM_GROUP_CH
SSM_STATE = 64
IN_WIDTH = POOL_WIDTH + SSM_WIDTH + 2 * D_MODEL
D_PACK = D_MODEL // 2
MOE_GROUPS = 4
MOE_EPG = 8
MOE_EXPERTS = MOE_GROUPS * MOE_EPG
MOE_FF = D_MODEL // 4
EPS = 1e-6

T_PROMPT = BATCH * SEQ
T_SAMPLE = DEC_BATCH * DEC_SEQ
T_ALL = T_PROMPT + T_SAMPLE
T_PAD = (BATCH + 1) * SEQ

LANES = 128
SUBLANES = 8
VMEM_LIMIT = 56 * 1024 * 1024

CHUNK = 8
OCT = LANES
N_OCT = SSM_WIDTH // OCT
OCT_GROUPS = OCT // SSM_GROUP_CH
OCT_STATES = OCT_GROUPS * SSM_STATE
CW = CHUNK * OCT
SW = 2 * OCT_STATES

ROUTE_LANES = LANES
EXP_LANE0 = MOE_GROUPS
N_ASSIGN = 2 * T_ALL
TME = 256
N_ITEMS_MAX = N_ASSIGN // TME + MOE_EXPERTS
N_SLOTS = N_ITEMS_MAX * TME


def _cparams(n_axes):
    return pltpu.CompilerParams(dimension_semantics=("arbitrary",) * n_axes,
                                vmem_limit_bytes=VMEM_LIMIT)


def _sigmoid(x):
    return 1.0 / (1.0 + jnp.exp(-x))


def _pack_pairs(x):
    c = x.shape[1] // 2
    hi = lax.bitcast_convert_type(x[:, :c].astype(BF16).astype(F32), U32)
    lo = lax.bitcast_convert_type(x[:, c:].astype(BF16).astype(F32), U32)
    return hi | (lo >> 16)


def _unpack_pairs(u, dtype):
    hi = lax.bitcast_convert_type(u & jnp.uint32(0xFFFF0000), F32)
    lo = lax.bitcast_convert_type(u << 16, F32)
    return jnp.concatenate([hi, lo], axis=1).astype(dtype)


def _gelu_tanh(x):
    c = math.sqrt(2.0 / math.pi)
    return 0.5 * x * (1.0 + jnp.tanh(c * (x + 0.044715 * (x * x * x))))


IN_TM = 1024
IN_TN = 1024
U_WIDTH = POOL_WIDTH + SSM_WIDTH
GATE_WIDTH = 2 * D_MODEL
IN_U_STEPS = U_WIDTH // IN_TN


CAST_RB = 128
IN_RC = 512


def _inproj_body(n_alias, cast_ranges, x_ref, g_ref, w_ref, *rest):
    n_cast = len(cast_ranges)
    cast_in = rest[n_alias:n_alias + n_cast]
    u_ref, gate_ref = rest[n_alias + n_cast:n_alias + n_cast + 2]
    cast_out = rest[n_alias + n_cast + 2:-1]
    xn_ref = rest[-1]
    j = pl.program_id(1)
    step = pl.program_id(0) * pl.num_programs(1) + j
    for (lo, hi), ci, co in zip(cast_ranges, cast_in, cast_out):
        @pl.when((step >= lo) & (step < hi))
        def _():
            co[...] = ci[...].astype(BF16)
    chunks = [slice(r * IN_RC, (r + 1) * IN_RC) for r in range(IN_TM // IN_RC)]
    @pl.when(j == 0)
    def _():
        for rows in chunks:
            x = x_ref[rows, :]
            inv = lax.rsqrt(jnp.mean(x * x, axis=-1, keepdims=True) + EPS)
            xn = ((x * inv) * g_ref[...]).astype(BF16)
            xn_ref[rows, :] = xn
            u_ref[rows, :] = jnp.dot(xn, w_ref[...], preferred_element_type=F32)
    @pl.when((j > 0) & (j < IN_U_STEPS))
    def _():
        u_ref[...] = jnp.dot(xn_ref[...], w_ref[...], preferred_element_type=F32)
    @pl.when(j >= IN_U_STEPS)
    def _():
        for rows in chunks:
            acc = jnp.dot(xn_ref[rows, :], w_ref[...], preferred_element_type=F32)
            gate_ref[rows, :] = _sigmoid(acc).astype(BF16)


def _inproj(x, g, w_bf, row_block0, dst=None, cast=()):
    n_i = x.shape[0] // IN_TM
    n_j = IN_WIDTH // IN_TN
    in_specs = [
        pl.BlockSpec((IN_TM, D_MODEL), lambda i, j: (i, 0)),
        pl.BlockSpec((1, D_MODEL), lambda i, j: (0, 0)),
        pl.BlockSpec((D_MODEL, IN_TN), lambda i, j: (0, j)),
    ]
    args = [x, g, w_bf]
    aliases = {}
    if dst is not None:
        in_specs += [pl.BlockSpec(memory_space=pl.ANY)] * 2
        args += list(dst)
        aliases = {3: 0, 4: 1}
    cast_specs, cast_shapes, cast_ranges, off = [], [], [], 0
    for wc in cast:
        nb = wc.shape[0] // CAST_RB
        cast_specs.append(pl.BlockSpec(
            (CAST_RB, wc.shape[1]), lambda i, j, o=off, nb=nb: (jnp.clip(i * n_j + j - o, 0, nb - 1), 0)))
        cast_shapes.append(jax.ShapeDtypeStruct(wc.shape, BF16))
        cast_ranges.append((off, off + nb))
        off += nb
    assert off <= n_i * n_j
    args += list(cast)
    return pl.pallas_call(
        functools.partial(_inproj_body, len(aliases), tuple(cast_ranges)),
        grid=(n_i, n_j),
        in_specs=in_specs + cast_specs,
        out_specs=(
            pl.BlockSpec((IN_TM, IN_TN), lambda i, j: (i + row_block0, jnp.minimum(j, IN_U_STEPS - 1))),
            pl.BlockSpec((IN_TM, IN_TN), lambda i, j: (i + row_block0, jnp.maximum(j - IN_U_STEPS, 0))),
            *cast_specs,
        ),
        out_shape=(
            jax.ShapeDtypeStruct((T_PAD, U_WIDTH), F32),
            jax.ShapeDtypeStruct((T_ALL, GATE_WIDTH), BF16),
            *cast_shapes,
        ),
        scratch_shapes=[pltpu.VMEM((IN_TM, D_MODEL), BF16)],
        input_output_aliases=aliases,
        compiler_params=_cparams(2),
        name="inproj",
    )(*args)


PP_TM = 512
HIST = 16


def _pool_project(pooled_g, g, w_ref, sc_ref, o_ref):
    y = jnp.dot(pooled_g.astype(BF16), w_ref[g], preferred_element_type=F32)
    lo, hi = g * POOL_OUT_CH, (g + 1) * POOL_OUT_CH
    o_ref[:, lo:hi] = (y * sc_ref[:, lo:hi]).astype(o_ref.dtype)


def _pool_prompt_body(u_ref, w_ref, sc_ref, o_ref, tail_ref, hist_ref):
    i = pl.program_id(1)
    @pl.when(i == 0)
    def _():
        hist_ref[...] = jnp.zeros_like(hist_ref)
    u = u_ref[...]
    ext = jnp.concatenate([hist_ref[...], u], axis=0)
    hist_ref[...] = u[PP_TM - HIST:, :]
    tail_ref[...] = u[PP_TM - HIST:, :]
    pos = i * PP_TM + lax.broadcasted_iota(I32, (PP_TM, 1), 0)
    for g, w in enumerate(POOL_WINDOWS):
        lo, hi = g * POOL_GROUP_CH, (g + 1) * POOL_GROUP_CH
        s = ext[:, lo:hi]
        d = 1
        while d < w:
            s = s + pltpu.roll(s, d, axis=0)
            d *= 2
        cnt = jnp.minimum(w, pos + 1).astype(F32)
        pooled = s[HIST:, :] / cnt - u[:, lo:hi]
        _pool_project(pooled, g, w_ref, sc_ref, o_ref)


def _pool_prompt(u, w_pool_bf, pool_scale):
    n_i = SEQ // PP_TM
    return pl.pallas_call(
        _pool_prompt_body,
        grid=(BATCH, n_i),
        in_specs=[
            pl.BlockSpec((PP_TM, POOL_WIDTH), lambda b, i: (b * n_i + i, 0)),
            pl.BlockSpec((POOL_GROUPS, POOL_GROUP_CH, POOL_OUT_CH), lambda b, i: (0, 0, 0)),
            pl.BlockSpec((1, D_MODEL), lambda b, i: (0, 0)),
        ],
        out_specs=(
            pl.BlockSpec((PP_TM, D_MODEL), lambda b, i: (b * n_i + i, 0)),
            pl.BlockSpec((None, HIST, POOL_WIDTH), lambda b, i: (b, 0, 0)),
        ),
        out_shape=(
            jax.ShapeDtypeStruct((T_ALL, D_MODEL), BF16),
            jax.ShapeDtypeStruct((BATCH, HIST, POOL_WIDTH), F32),
        ),
        scratch_shapes=[pltpu.VMEM((HIST, POOL_WIDTH), F32)],
        compiler_params=_cparams(2),
        name="pool_prompt",
    )(u, w_pool_bf, pool_scale)


def _pool_sample_body(u_ref, hist_ref, w_ref, sc_ref, _dst, o_ref, buf_ref):
    rows = [hist_ref[k] for k in range(POOL_BUF)]
    rows += [u_ref[DEC_BATCH * t:DEC_BATCH * (t + 1), :] for t in range(DEC_SEQ)]
    n = len(rows)
    for k in range(POOL_BUF):
        buf_ref[k] = rows[n - POOL_BUF + k]
    for g, w in enumerate(POOL_WINDOWS):
        lo, hi = g * POOL_GROUP_CH, (g + 1) * POOL_GROUP_CH
        f = [r[:, lo:hi] for r in rows]
        cur = f
        d = 1
        while d < w:
            cur = [cur[k] + cur[k - d] if k - d >= 0 else cur[k] for k in range(n)]
            d *= 2
        pooled = jnp.concatenate(
            [cur[POOL_BUF + t] / float(w) - f[POOL_BUF + t] for t in range(DEC_SEQ)], axis=0)
        _pool_project(pooled, g, w_ref, sc_ref, o_ref)


def _pool_sample(u, hist_tm, w_pool_bf, pool_scale, y_pool):
    blk = T_PROMPT // T_SAMPLE
    return pl.pallas_call(
        _pool_sample_body,
        grid=(1,),
        in_specs=[
            pl.BlockSpec((T_SAMPLE, POOL_WIDTH), lambda i: (blk, 0)),
            pl.BlockSpec((POOL_BUF, DEC_BATCH, POOL_WIDTH), lambda i: (0, 0, 0)),
            pl.BlockSpec((POOL_GROUPS, POOL_GROUP_CH, POOL_OUT_CH), lambda i: (0, 0, 0)),
            pl.BlockSpec((1, D_MODEL), lambda i: (0, 0)),
            pl.BlockSpec(memory_space=pl.ANY),
        ],
        out_specs=(
            pl.BlockSpec((T_SAMPLE, D_MODEL), lambda i: (blk, 0)),
            pl.BlockSpec((POOL_BUF, DEC_BATCH, POOL_WIDTH), lambda i: (0, 0, 0)),
        ),
        out_shape=(
            jax.ShapeDtypeStruct((T_ALL, D_MODEL), BF16),
            jax.ShapeDtypeStruct((POOL_BUF, DEC_BATCH, POOL_WIDTH), F32),
        ),
        input_output_aliases={4: 0},
        compiler_params=_cparams(1),
        name="pool_sample",
    )(u, hist_tm, w_pool_bf, pool_scale, y_pool)


def _ssm_tables(a_re, a_im, log_dt, b_re, b_im, c_re, c_im, d_skip):
    dt = jnp.exp(log_dt)[:, None]
    lr, li = a_re, a_im
    ab_re = jnp.exp(lr * dt) * jnp.cos(li * dt)
    ab_im = jnp.exp(lr * dt) * jnp.sin(li * dt)
    den = lr * lr + li * li
    nr, ni = ab_re - 1.0, ab_im
    q_re = (nr * lr + ni * li) / den
    q_im = (ni * lr - nr * li) / den
    bb_re = q_re[..., None] * b_re - q_im[..., None] * b_im
    bb_im = q_re[..., None] * b_im + q_im[..., None] * b_re

    def lam_rows(ks):
        k = jnp.asarray(ks, F32)[:, None, None]
        m = jnp.exp(k * lr * dt)
        re = (m * jnp.cos(k * li * dt)).reshape(len(ks), N_OCT, OCT_STATES)
        im = (m * jnp.sin(k * li * dt)).reshape(len(ks), N_OCT, OCT_STATES)
        return jnp.concatenate([re, im], axis=-1).transpose(1, 0, 2)

    def compact(re, im):
        v = jnp.concatenate([re, im], axis=-1)
        return v.reshape(N_OCT, OCT, 2 * SSM_STATE)

    bbc = compact(jnp.swapaxes(bb_re, 1, 2), jnp.swapaxes(bb_im, 1, 2))
    ccc = compact(c_re, c_im)
    pw = lam_rows(list(range(2 * SUBLANES)))
    r = jnp.arange(SUBLANES)[None, :, None]
    parts = [jnp.where(r >= dd, lam_rows([CHUNK * dd]), 0.0) for dd in (1, 2, 4)]
    parts.append(lam_rows([CHUNK * kk for kk in range(1, SUBLANES + 1)]))
    tab = jnp.concatenate(parts, axis=1)
    dsk = d_skip.reshape(N_OCT, 1, OCT)
    return bbc, ccc, pw, tab, dsk


def _split_bf16(x):
    hi = x.astype(BF16)
    return hi, (x - hi.astype(F32)).astype(BF16)


def _dot_nt(a, b):
    return lax.dot_general(a, b, (((1,), (1,)), ((), ())), preferred_element_type=F32)


def _build_weights(bbc_ref, ccc_ref, pw_ref, f_ref, gt_ref, m_ref):
    row_gi = lax.broadcasted_iota(I32, (OCT, 1), 0) >> 4
    col = lax.broadcasted_iota(I32, (1, SW), 1)
    col_gi = (col >> 6) & 7
    src = ((col >> 9) << 6) | (col & 63)
    k128 = lax.broadcasted_iota(I32, (2 * SSM_STATE, 1), 0)
    spread = jnp.where(k128 == src, 1.0, 0.0).astype(BF16)
    diag = row_gi == col_gi

    def expand(c_ref):
        hi, lo = _split_bf16(c_ref[...])
        d = (jnp.dot(hi, spread, preferred_element_type=F32)
             + jnp.dot(lo, spread, preferred_element_type=F32))
        d = jnp.where(diag, d, 0.0)
        return d[:, :OCT_STATES], d[:, OCT_STATES:]

    br, bi = expand(bbc_ref)
    cr, ci = expand(ccc_ref)
    chi_r, clo_r = _split_bf16(cr)
    chi_i, clo_i = _split_bf16(ci)

    def lam(k):
        return pw_ref[k:k + 1, :OCT_STATES], pw_ref[k:k + 1, OCT_STATES:]

    def dot3(a, bhi, blo):
        ahi, alo = _split_bf16(a)
        return _dot_nt(ahi, bhi) + _dot_nt(alo, bhi) + _dot_nt(ahi, blo)

    lags = []
    for k in range(CHUNK):
        pr, pi_ = lam(k)
        fr, fi = _cmul(br, bi, pr, pi_)
        s = CHUNK - 1 - k
        f_ref[s * OCT:(s + 1) * OCT, :] = jnp.concatenate([fr, fi], axis=1).astype(BF16)
        lags.append((dot3(fr, chi_r, clo_r) - dot3(fi, chi_i, clo_i)).astype(BF16))
        pr, pi_ = lam(k + 1)
        gr, gi = _cmul(cr, ci, pr, pi_)
        gt_ref[k * OCT:(k + 1) * OCT, :] = jnp.concatenate([gr, -gi], axis=1).astype(BF16)
    zero = jnp.zeros((OCT, OCT), BF16)
    for s in range(CHUNK):
        for t in range(CHUNK):
            m_ref[s * OCT:(s + 1) * OCT, t * OCT:(t + 1) * OCT] = lags[t - s] if t >= s else zero


def _cmul(ar, ai, br, bi):
    return ar * br - ai * bi, ar * bi + ai * br


def _chunk_scan(sloc, tab_ref):
    R = sloc.shape[0]
    nb = R // SUBLANES
    sr, si = sloc[:, :OCT_STATES], sloc[:, OCT_STATES:]
    rowi = lax.broadcasted_iota(I32, (R, 1), 0)
    tr = jnp.where(rowi == 0, 0.0, pltpu.roll(sr, 1, axis=0))
    ti = jnp.where(rowi == 0, 0.0, pltpu.roll(si, 1, axis=0))
    for lvl, d in enumerate((1, 2, 4)):
        mr = tab_ref[lvl * SUBLANES:(lvl + 1) * SUBLANES, :OCT_STATES]
        mi = tab_ref[lvl * SUBLANES:(lvl + 1) * SUBLANES, OCT_STATES:]
        mr = jnp.concatenate([mr] * nb, axis=0)
        mi = jnp.concatenate([mi] * nb, axis=0)
        pr, pi_ = _cmul(mr, mi, pltpu.roll(tr, d, axis=0), pltpu.roll(ti, d, axis=0))
        tr, ti = tr + pr, ti + pi_
    pwr = tab_ref[3 * SUBLANES:4 * SUBLANES, :OCT_STATES]
    pwi = tab_ref[3 * SUBLANES:4 * SUBLANES, OCT_STATES:]
    cr = jnp.zeros((1, OCT_STATES), F32)
    ci = jnp.zeros((1, OCT_STATES), F32)
    out_r, out_i = [], []
    for k in range(nb):
        ar = tr[k * SUBLANES:(k + 1) * SUBLANES, :]
        ai = ti[k * SUBLANES:(k + 1) * SUBLANES, :]
        pr, pi_ = _cmul(pwr, pwi, jnp.broadcast_to(cr, ar.shape), jnp.broadcast_to(ci, ai.shape))
        hr, hi = ar + pr, ai + pi_
        out_r.append(hr)
        out_i.append(hi)
        cr, ci = hr[SUBLANES - 1:, :], hi[SUBLANES - 1:, :]
    hin = jnp.concatenate([jnp.concatenate(out_r, axis=0), jnp.concatenate(out_i, axis=0)], axis=1)
    lr, li = pwr[0:1, :], pwi[0:1, :]
    fr, fi = _cmul(lr, li, cr, ci)
    fin = jnp.concatenate([fr + sr[R - 1:, :], fi + si[R - 1:, :]], axis=1)
    return hin, fin


def _ssm_body(u_ref, h0r_ref, h0i_ref, bbc_ref, ccc_ref, pw_ref, tab_ref, d_ref,
              y_ref, hout_ref, hr_ref, hi_ref, f_ref, gt_ref, m_ref):
    b = pl.program_id(1)

    @pl.when(b == 0)
    def _():
        _build_weights(bbc_ref, ccc_ref, pw_ref, f_ref, gt_ref, m_ref)

    def outputs(xs, xb, hin_bf):
        y = (jnp.dot(xb, m_ref[...], preferred_element_type=F32) + _dot_nt(hin_bf, gt_ref[...]))
        return [_gelu_tanh(y[:, t * OCT:(t + 1) * OCT] + d_ref[...] * xs[t]) for t in range(CHUNK)]

    @pl.when(b < BATCH)
    def _():
        R = SEQ // CHUNK
        xs = [u_ref[pl.ds(s, R, stride=CHUNK), :] for s in range(CHUNK)]
        xb = jnp.concatenate(xs, axis=1).astype(BF16)
        sloc = jnp.dot(xb, f_ref[...], preferred_element_type=F32)
        hin, fin = _chunk_scan(sloc, tab_ref)
        for t, yt in enumerate(outputs(xs, xb, hin.astype(BF16))):
            y_ref[pl.ds(t, R, stride=CHUNK), :] = yt
        hout_ref[...] = fin

    @pl.when(b == BATCH)
    def _():
        B = DEC_BATCH
        xs = [u_ref[B * s:B * (s + 1), :] for s in range(CHUNK)]
        xb = jnp.concatenate(xs, axis=1).astype(BF16)
        sloc = jnp.dot(xb, f_ref[...], preferred_element_type=F32)
        h0r, h0i = h0r_ref[...], h0i_ref[...]
        hin = jnp.concatenate([h0r, h0i], axis=1).astype(BF16)
        for t, yt in enumerate(outputs(xs, xb, hin)):
            y_ref[B * t:B * (t + 1), :] = yt
        lr = tab_ref[3 * SUBLANES:3 * SUBLANES + 1, :OCT_STATES]
        li = tab_ref[3 * SUBLANES:3 * SUBLANES + 1, OCT_STATES:]
        nr, ni = _cmul(lr, li, h0r, h0i)
        hr_ref[...] = nr + sloc[:, :OCT_STATES]
        hi_ref[...] = ni + sloc[:, OCT_STATES:]


def _ssm(u, h0r, h0i, tables):
    col0 = POOL_WIDTH // OCT
    im3 = lambda o, b: (o, 0, 0)
    st_spec = pl.BlockSpec((DEC_BATCH, OCT_STATES), lambda o, b: (0, o))
    return pl.pallas_call(
        _ssm_body,
        grid=(N_OCT, BATCH + 1),
        in_specs=[
            pl.BlockSpec((SEQ, OCT), lambda o, b: (b, col0 + o)), st_spec, st_spec,
            pl.BlockSpec((None, OCT, 2 * SSM_STATE), im3),
            pl.BlockSpec((None, OCT, 2 * SSM_STATE), im3),
            pl.BlockSpec((None, 2 * SUBLANES, SW), im3),
            pl.BlockSpec((None, 4 * SUBLANES, SW), im3),
            pl.BlockSpec((None, 1, OCT), im3),
        ],
        out_specs=(
            pl.BlockSpec((SEQ, OCT), lambda o, b: (b, o)),
            pl.BlockSpec((None, 1, SW), lambda o, b: (jnp.minimum(b, BATCH - 1) * N_OCT + o, 0, 0)),
            st_spec, st_spec,
        ),
        out_shape=(
            jax.ShapeDtypeStruct((T_PAD, SSM_WIDTH), F32),
            jax.ShapeDtypeStruct((BATCH * N_OCT, 1, SW), F32),
            jax.ShapeDtypeStruct((DEC_BATCH, SSM_GROUPS * SSM_STATE), F32),
            jax.ShapeDtypeStruct((DEC_BATCH, SSM_GROUPS * SSM_STATE), F32),
        ),
        scratch_shapes=[pltpu.VMEM((CW, SW), BF16), pltpu.VMEM((CW, SW), BF16), pltpu.VMEM((CW, CW), BF16)],
        compiler_params=_cparams(2),
        name="ssm",
    )(u, h0r, h0i, *tables)


PM_TM = 256
PM_PROMPT_BLOCKS = T_PROMPT // PM_TM
PM_STEPS = T_ALL // PM_TM


def _route(logits, valid, cnt_ref):
    lane = lax.broadcasted_iota(I32, (PM_TM, ROUTE_LANES), 1)
    neg = jnp.float32(-jnp.inf)
    big = jnp.int32(1 << 20)
    is_g = lane < MOE_GROUPS
    gmax = jnp.max(jnp.where(is_g, logits, neg), axis=1, keepdims=True)
    g_idx = jnp.min(jnp.where(is_g & (logits == gmax), lane, big), axis=1, keepdims=True)
    g_den = jnp.sum(jnp.where(is_g, jnp.exp(logits - gmax), 0.0), axis=1, keepdims=True)
    g_val = 1.0 / g_den
    e_lane = lane - EXP_LANE0
    sel = (e_lane >= 0) & (e_lane < MOE_EXPERTS) & ((e_lane >> 3) == g_idx)
    m1 = jnp.max(jnp.where(sel, logits, neg), axis=1, keepdims=True)
    i1 = jnp.min(jnp.where(sel & (logits == m1), lane, big), axis=1, keepdims=True)
    sel2 = sel & (lane != i1)
    m2 = jnp.max(jnp.where(sel2, logits, neg), axis=1, keepdims=True)
    i2 = jnp.min(jnp.where(sel2 & (logits == m2), lane, big), axis=1, keepdims=True)
    e2 = jnp.exp(m2 - m1)
    w1 = g_val / (1.0 + e2)
    w2 = g_val * e2 / (1.0 + e2)
    oh1 = lane == i1
    oh2 = lane == i2
    oh = jnp.where(oh1 | oh2, valid, 0.0)
    rr = lax.broadcasted_iota(I32, (PM_TM, PM_TM), 0)
    cc = lax.broadcasted_iota(I32, (PM_TM, PM_TM), 1)
    tri = jnp.where(cc < rr, 1.0, 0.0).astype(BF16)
    base = cnt_ref[...] + jnp.dot(tri, oh.astype(BF16), preferred_element_type=F32)
    rank1 = jnp.sum(jnp.where(oh1, base, 0.0), axis=1, keepdims=True)
    rank2 = jnp.sum(jnp.where(oh2, base, 0.0), axis=1, keepdims=True)
    cnt_ref[...] = cnt_ref[...] + jnp.sum(oh, axis=0, keepdims=True)
    rt = jnp.where(lane == 0, w1, 0.0)
    rt = jnp.where(lane == 1, w2, rt)
    rt = jnp.where(lane == 2, rank1, rt)
    rt = jnp.where(lane == 3, rank2, rt)
    rt = jnp.where(lane == 4, (i1 - EXP_LANE0).astype(F32), rt)
    rt = jnp.where(lane == 5, (i2 - EXP_LANE0).astype(F32), rt)
    return rt


def _postmix_body(ya_ref, gp_ref, gs_ref, yp_ref, xp_ref, xs_ref, wa_ref, wb_ref, wo_ref,
                  gf_ref, wr_ref, br_ref, h_ref, tn_ref, rt_ref, rtt_ref, cnt_out_ref,
                  cnt_ref, lg_ref):
    i = pl.program_id(0)
    @pl.when(i == 0)
    def _():
        cnt_ref[...] = jnp.zeros_like(cnt_ref)
        lg_ref[...] = jnp.zeros_like(lg_ref)
    prev_logits = lg_ref[...]

    ya = ya_ref[...].astype(BF16)
    a = jnp.dot(ya, wa_ref[...], preferred_element_type=F32)
    bg = jnp.dot(ya, wb_ref[...], preferred_element_type=F32)
    y_ssm = a * _sigmoid(bg)
    merged = (gp_ref[...].astype(F32) * yp_ref[...].astype(F32)
              + gs_ref[...].astype(F32) * y_ssm)
    x = jnp.where(jnp.minimum(i, PM_STEPS - 1) < PM_PROMPT_BLOCKS, xp_ref[...], xs_ref[...])
    h = x + jnp.dot(merged.astype(BF16), wo_ref[...], preferred_element_type=F32)
    h_ref[...] = h
    inv = lax.rsqrt(jnp.mean(h * h, axis=-1, keepdims=True) + EPS)
    tn = (h * inv) * gf_ref[...]
    tn_ref[...] = _pack_pairs(tn)
    t_hi = tn.astype(BF16)
    t_lo = (tn - t_hi.astype(F32)).astype(BF16)
    hh = jnp.dot(t_hi, wr_ref[...], preferred_element_type=F32)
    lh = jnp.dot(t_lo, wr_ref[:, :ROUTE_LANES], preferred_element_type=F32)
    lg_ref[...] = (hh[:, :ROUTE_LANES] + lh + hh[:, ROUTE_LANES:]) + br_ref[...]
    rt = _route(prev_logits, jnp.where(i > 0, 1.0, 0.0), cnt_ref)
    rt_ref[...] = rt
    rtt_ref[...] = rt.T[:8, :]
    cnt_out_ref[...] = cnt_ref[...]


def _postmix(y_act, gates, y_pool, xp, xs, wa, wb, wo, g_ffn, wr_cat, b_r):
    npb = PM_PROMPT_BLOCKS
    const2 = lambda i: (0, 0)
    tile = lambda i: jnp.minimum(i, PM_STEPS - 1)
    return pl.pallas_call(
        _postmix_body,
        grid=(PM_STEPS + 1,),
        in_specs=[
            pl.BlockSpec((PM_TM, SSM_WIDTH), lambda i: (tile(i), 0)),
            pl.BlockSpec((PM_TM, D_MODEL), lambda i: (tile(i), 0)),
            pl.BlockSpec((PM_TM, D_MODEL), lambda i: (tile(i), 1)),
            pl.BlockSpec((PM_TM, D_MODEL), lambda i: (tile(i), 0)),
            pl.BlockSpec((PM_TM, D_MODEL), lambda i: (jnp.minimum(i, npb - 1), 0)),
            pl.BlockSpec((PM_TM, D_MODEL), lambda i: (jnp.maximum(tile(i) - npb, 0), 0)),
            pl.BlockSpec((SSM_WIDTH, D_MODEL), const2, pipeline_mode=pl.Buffered(1)),
            pl.BlockSpec((SSM_WIDTH, D_MODEL), const2, pipeline_mode=pl.Buffered(1)),
            pl.BlockSpec((D_MODEL, D_MODEL), const2, pipeline_mode=pl.Buffered(1)),
            pl.BlockSpec((1, D_MODEL), const2),
            pl.BlockSpec((D_MODEL, 2 * ROUTE_LANES), const2),
            pl.BlockSpec((1, ROUTE_LANES), const2),
        ],
        out_specs=(
            pl.BlockSpec((PM_TM, D_MODEL), lambda i: (tile(i), 0)),
            pl.BlockSpec((PM_TM, D_PACK), lambda i: (tile(i), 0)),
            pl.BlockSpec((PM_TM, ROUTE_LANES), lambda i: (jnp.maximum(i - 1, 0), 0)),
            pl.BlockSpec((8, PM_TM), lambda i: (0, jnp.maximum(i - 1, 0))),
            pl.BlockSpec((1, ROUTE_LANES), const2),
        ),
        out_shape=(
            jax.ShapeDtypeStruct((T_ALL, D_MODEL), F32),
            jax.ShapeDtypeStruct((T_ALL, D_PACK), U32),
            jax.ShapeDtypeStruct((T_ALL, ROUTE_LANES), F32),
            jax.ShapeDtypeStruct((8, T_ALL), F32),
            jax.ShapeDtypeStruct((1, ROUTE_LANES), F32),
        ),
        scratch_shapes=[pltpu.VMEM((1, ROUTE_LANES), F32), pltpu.VMEM((PM_TM, ROUTE_LANES), F32)],
        compiler_params=_cparams(1),
        name="postmix",
    )(y_act, gates, gates, y_pool, xp, xs, wa, wb, wo, g_ffn, wr_cat, b_r)


SC_CH = 96


def _sc_workers():
    info = plsc.get_sparse_core_info()
    return info.num_cores, info.num_cores * info.num_subcores


def _sc_dispatch(tn, slots):
    n_cores, n_workers = _sc_workers()
    per_w = (T_ALL // SC_CH) // n_workers
    assert per_w * n_workers * SC_CH == T_ALL
    slots = slots.reshape(2, n_workers, per_w, SC_CH)

    @functools.partial(
        pl.kernel,
        mesh=plsc.VectorSubcoreMesh(core_axis_name="c", subcore_axis_name="s"),
        out_type=jax.ShapeDtypeStruct((N_SLOTS, D_PACK), U32),
        scratch_types=[pltpu.VMEM((2, per_w, SC_CH), I32), pltpu.VMEM((SC_CH, D_PACK), U32)],
    )
    def k(tn_hbm, slots_hbm, xs_hbm, idx_v, rows_v):
        wid = lax.axis_index("s") * n_cores + lax.axis_index("c")
        c0 = wid * per_w
        pltpu.sync_copy(slots_hbm.at[0, wid], idx_v.at[0])
        pltpu.sync_copy(slots_hbm.at[1, wid], idx_v.at[1])

        @pl.loop(0, per_w)
        def _(c):
            row0 = pl.multiple_of((c0 + c) * SC_CH, SC_CH)
            pltpu.sync_copy(tn_hbm.at[pl.ds(row0, SC_CH)], rows_v)
            pltpu.sync_copy(rows_v, xs_hbm.at[idx_v.at[0, c]])
            pltpu.sync_copy(rows_v, xs_hbm.at[idx_v.at[1, c]])

    return k(tn, slots)


def _sc_collect(ys, slots):
    n_cores, n_workers = _sc_workers()
    per_w = (N_ASSIGN // SC_CH) // n_workers
    assert per_w * n_workers * SC_CH == N_ASSIGN
    slots = slots.reshape(n_workers, per_w, SC_CH)

    @functools.partial(
        pl.kernel,
        mesh=plsc.VectorSubcoreMesh(core_axis_name="c", subcore_axis_name="s"),
        out_type=jax.ShapeDtypeStruct((N_ASSIGN, D_PACK), U32),
        scratch_types=[pltpu.VMEM((per_w, SC_CH), I32), pltpu.VMEM((SC_CH, D_PACK), U32)],
    )
    def k(ys_hbm, slots_hbm, out_hbm, idx_v, rows_v):
        wid = lax.axis_index("s") * n_cores + lax.axis_index("c")
        c0 = wid * per_w
        pltpu.sync_copy(slots_hbm.at[wid], idx_v)

        @pl.loop(0, per_w)
        def _(c):
            row0 = pl.multiple_of((c0 + c) * SC_CH, SC_CH)
            pltpu.sync_copy(ys_hbm.at[idx_v.at[c]], rows_v)
            pltpu.sync_copy(rows_v, out_hbm.at[pl.ds(row0, SC_CH)])

    return k(ys, slots)


W_PARTS = 2


def _expert_body(t0_ref, nt_ref, xs_hbm, wg_hbm, wu_hbm, wd_hbm, ys_hbm,
                 wg_ref, wu_ref, wd_ref, xb_ref, yb_ref, wgb_ref, wub_ref, wdb_ref, wsem, xsem, ysem):
    e = pl.program_id(0)
    n = nt_ref[e]
    g0 = t0_ref[e]
    ws = e & 1

    def w_copies(ex, slot):
        out = []
        for hbm, buf in ((wg_hbm, wg_ref), (wu_hbm, wu_ref), (wd_hbm, wd_ref)):
            rb = buf.shape[1] // W_PARTS
            for p in range(W_PARTS):
                out.append((pltpu.make_async_copy(hbm.at[ex, pl.ds(p * rb, rb)],
                                                  buf.at[slot, pl.ds(p * rb, rb)], wsem.at[slot]), p))
        return out

    @pl.when(e == 0)
    def _():
        for cp, p in w_copies(0, 0):
            cp.start(priority=p)

    @pl.when(e + 1 < MOE_EXPERTS)
    def _():
        for cp, p in w_copies(e + 1, 1 - ws):
            cp.start(priority=p)

    for cp, _ in w_copies(e, ws):
        cp.wait()

    def rows(j):
        return pl.ds(pl.multiple_of((g0 + j) * TME, TME), TME)

    def x_copy(j, s):
        return pltpu.make_async_copy(xs_hbm.at[rows(j)], xb_ref.at[s], xsem.at[s])

    def y_copy(j, s):
        return pltpu.make_async_copy(yb_ref.at[s], ys_hbm.at[rows(j)], ysem.at[s])

    @pl.when(n > 0)
    def _():
        x_copy(0, 0).start()
        wgb_ref[...] = wg_ref[ws].astype(BF16)
        wub_ref[...] = wu_ref[ws].astype(BF16)
        wdb_ref[...] = wd_ref[ws].astype(BF16)

        def tile(j, c):
            s = j & 1
            x_copy(j, s).wait()
            @pl.when(j + 1 < n)
            def _():
                x_copy(j + 1, 1 - s).start()
            @pl.when(j >= 2)
            def _():
                y_copy(j - 2, s).wait()
            x = _unpack_pairs(xb_ref[s], BF16)
            hg = jnp.dot(x, wgb_ref[...], preferred_element_type=F32)
            hu = jnp.dot(x, wub_ref[...], preferred_element_type=F32)
            act = (hg * _sigmoid(hg)) * hu
            yb_ref[s] = _pack_pairs(jnp.dot(act.astype(BF16), wdb_ref[...], preferred_element_type=F32))
            y_copy(j, s).start()
            return c
        lax.fori_loop(0, n, tile, 0)

        @pl.when(n >= 2)
        def _():
            y_copy(n - 2, n & 1).wait()
        y_copy(n - 1, (n - 1) & 1).wait()


def _experts(tile0, tiles, xs, w_eg, w_eu, w_ed):
    any_spec = pl.BlockSpec(memory_space=pl.ANY)
    grid_spec = pltpu.PrefetchScalarGridSpec(
        num_scalar_prefetch=2,
        grid=(MOE_EXPERTS,),
        in_specs=[any_spec] * 4,
        out_specs=any_spec,
        scratch_shapes=[
            pltpu.VMEM((2, D_MODEL, MOE_FF), F32),
            pltpu.VMEM((2, D_MODEL, MOE_FF), F32),
            pltpu.VMEM((2, MOE_FF, D_MODEL), F32),
            pltpu.VMEM((2, TME, D_PACK), U32),
            pltpu.VMEM((2, TME, D_PACK), U32),
            pltpu.VMEM((D_MODEL, MOE_FF), BF16),
            pltpu.VMEM((D_MODEL, MOE_FF), BF16),
            pltpu.VMEM((MOE_FF, D_MODEL), BF16),
            pltpu.SemaphoreType.DMA((2,)),
            pltpu.SemaphoreType.DMA((2,)),
            pltpu.SemaphoreType.DMA((2,)),
        ],
    )
    return pl.pallas_call(
        _expert_body,
        grid_spec=grid_spec,
        out_shape=jax.ShapeDtypeStruct((N_SLOTS, D_PACK), U32),
        compiler_params=_cparams(1),
        name="experts",
    )(tile0, tiles, xs, w_eg, w_eu, w_ed)


FN_TM = 256
FN_PROMPT_BLOCKS = T_PROMPT // FN_TM


def _final_body(h_ref, y0_ref, y1_ref, rt_ref, g_ref, op_ref, os_ref):
    i = pl.program_id(0)
    rt = rt_ref[...]
    z = (h_ref[...] + rt[:, 0:1] * _unpack_pairs(y0_ref[...], F32)
         + rt[:, 1:2] * _unpack_pairs(y1_ref[...], F32))
    inv = lax.rsqrt(jnp.mean(z * z, axis=-1, keepdims=True) + EPS)
    out = (z * inv) * g_ref[...]
    @pl.when(i < FN_PROMPT_BLOCKS)
    def _():
        op_ref[...] = out
    @pl.when(i >= FN_PROMPT_BLOCKS)
    def _():
        os_ref[...] = out


def _final(h, y, route, g_final):
    n = T_ALL // FN_TM
    npb = FN_PROMPT_BLOCKS
    yoff = T_ALL // FN_TM
    return pl.pallas_call(
        _final_body,
        grid=(n,),
        in_specs=[
            pl.BlockSpec((FN_TM, D_MODEL), lambda i: (i, 0)),
            pl.BlockSpec((FN_TM, D_PACK), lambda i: (i, 0)),
            pl.BlockSpec((FN_TM, D_PACK), lambda i: (yoff + i, 0)),
            pl.BlockSpec((FN_TM, ROUTE_LANES), lambda i: (i, 0)),
            pl.BlockSpec((1, D_MODEL), lambda i: (0, 0)),
        ],
        out_specs=(
            pl.BlockSpec((FN_TM, D_MODEL), lambda i: (jnp.minimum(i, npb - 1), 0)),
            pl.BlockSpec((FN_TM, D_MODEL), lambda i: (jnp.maximum(i - npb, 0), 0)),
        ),
        out_shape=(
            jax.ShapeDtypeStruct((T_PROMPT, D_MODEL), F32),
            jax.ShapeDtypeStruct((T_SAMPLE, D_MODEL), F32),
        ),
        compiler_params=_cparams(1),
        name="final",
    )(h, y, y, route, g_final)


def _dispatch_plan(route_t, cnt):
    counts = cnt[0, EXP_LANE0:EXP_LANE0 + MOE_EXPERTS].astype(I32)
    tiles = (counts + (TME - 1)) // TME
    cumt = jnp.cumsum(tiles)
    pad_off = (cumt - tiles) * TME
    rank = route_t[2:4].astype(I32)
    eid = route_t[4:6].astype(I32)
    onehot = eid[None] == jnp.arange(MOE_EXPERTS, dtype=I32)[:, None, None]
    slots = rank + jnp.sum(jnp.where(onehot, pad_off[:, None, None], 0), axis=0)
    return slots, cumt - tiles, tiles


def kernel(x_prompt, x_sample, state_pool, state_ssm_re, state_ssm_im, g_mix, w_in, w_pool,
           pool_scale, ssm_a_re, ssm_a_im, ssm_log_dt, ssm_b_re, ssm_b_im, ssm_c_re, ssm_c_im,
           ssm_d, w_glu_a, w_glu_b, w_out, g_ffn, w_router_group, b_router_group,
           w_router_expert, b_router_expert, w_exp_gate, w_exp_up, w_exp_down, g_final):
    l = 0
    xp = x_prompt.reshape(T_PROMPT, D_MODEL)
    xs = x_sample.transpose(1, 0, 2).reshape(T_SAMPLE, D_MODEL)
    w_in_bf = w_in[l].astype(BF16)
    w_pool_bf = w_pool[l].astype(BF16)
    g_mix2 = g_mix[l].reshape(1, D_MODEL)
    scale2 = pool_scale[l].reshape(1, D_MODEL)

    u, gates, wa_bf, wb_bf, wo_bf = _inproj(xp, g_mix2, w_in_bf, 0,
                                            cast=(w_glu_a[l], w_glu_b[l], w_out[l]))
    u, gates = _inproj(xs, g_mix2, w_in_bf, T_PROMPT // IN_TM, dst=(u, gates))

    y_pool, pool_tail = _pool_prompt(u, w_pool_bf, scale2)
    hist_tm = state_pool[l].transpose(1, 0, 2)
    y_pool, pool_buf_tm = _pool_sample(u, hist_tm, w_pool_bf, scale2, y_pool)

    tables = _ssm_tables(ssm_a_re[l], ssm_a_im[l], ssm_log_dt[l], ssm_b_re[l], ssm_b_im[l],
                         ssm_c_re[l], ssm_c_im[l], ssm_d[l])
    h0r = state_ssm_re[l].reshape(DEC_BATCH, SSM_GROUPS * SSM_STATE)
    h0i = state_ssm_im[l].reshape(DEC_BATCH, SSM_GROUPS * SSM_STATE)
    y_act, h_prompt, hs_re, hs_im = _ssm(u, h0r, h0i, tables)

    assert EXP_LANE0 == MOE_GROUPS
    w_r = jnp.concatenate([w_router_group[l], w_router_expert[l],
                           jnp.zeros((D_MODEL, ROUTE_LANES - EXP_LANE0 - MOE_EXPERTS), F32)], axis=1)
    wr_hi = w_r.astype(BF16)
    wr_cat = jnp.concatenate([wr_hi, (w_r - wr_hi.astype(F32)).astype(BF16)], axis=1)
    b_r = jnp.zeros((1, ROUTE_LANES), F32)
    b_r = b_r.at[0, :MOE_GROUPS].set(b_router_group[l])
    b_r = b_r.at[0, EXP_LANE0:EXP_LANE0 + MOE_EXPERTS].set(b_router_expert[l])

    h, tn, route, route_t, cnt = _postmix(y_act, gates, y_pool, xp, xs, wa_bf, wb_bf, wo_bf,
                                          g_ffn[l].reshape(1, D_MODEL), wr_cat, b_r)
    slots, tile0, tiles = _dispatch_plan(route_t, cnt)
    xs_sorted = _sc_dispatch(tn, slots)
    ys_sorted = _experts(tile0, tiles, xs_sorted, w_exp_gate[l], w_exp_up[l], w_exp_down[l])
    y = _sc_collect(ys_sorted, slots)
    yp, ys = _final(h, y, route, g_final.reshape(1, D_MODEL))

    y_prompt = yp.reshape(BATCH, SEQ, D_MODEL)
    y_sample = ys.reshape(DEC_SEQ, DEC_BATCH, D_MODEL).transpose(1, 0, 2)
    new_pool_prompt = pool_tail[:, HIST - POOL_BUF:, :][None]
    new_pool_sample = pool_buf_tm.transpose(1, 0, 2)[None]
    hp = h_prompt.reshape(BATCH, N_OCT, 2, OCT_GROUPS, SSM_STATE).transpose(2, 0, 1, 3, 4)
    hp = hp.reshape(2, BATCH, SSM_GROUPS, SSM_STATE)
    shp = (1, DEC_BATCH, SSM_GROUPS, SSM_STATE)
    return (y_prompt, y_sample, new_pool_prompt, hp[0][None], hp[1][None], new_pool_sample,
            hs_re.reshape(shp), hs_im.reshape(shp))
```

```python
import functools
import math

import jax
import jax.numpy as jnp
from jax import lax
from jax.experimental import pallas as pl
from jax.experimental.pallas import tpu as pltpu
from jax.experimental.pallas import tpu_sc as plsc

F32 = jnp.float32
BF16 = jnp.bfloat16
I32 = jnp.int32
U32 = jnp.uint32

D_MODEL = 2048
BATCH = 4
SEQ = 2048
DEC_BATCH = 128
DEC_SEQ = 8
PAST_LEN = 16384
POOL_WIDTH = D_MODEL // 2
POOL_WINDOWS = (2, 4, 8, 16)
POOL_GROUPS = len(POOL_WINDOWS)
POOL_GROUP_CH = POOL_WIDTH // POOL_GROUPS
POOL_OUT_CH = D_MODEL // POOL_GROUPS
POOL_BUF = max(POOL_WINDOWS) - 1
SSM_WIDTH = D_MODEL // 2
SSM_GROUP_CH = 16
SSM_GROUPS = SSM_WIDTH // SSM_GROUP_CH
SSM_STATE = 64
IN_WIDTH = POOL_WIDTH + SSM_WIDTH + 2 * D_MODEL
D_PACK = D_MODEL // 2
MOE_GROUPS = 4
MOE_EPG = 8
MOE_EXPERTS = MOE_GROUPS * MOE_EPG
MOE_FF = D_MODEL // 4
EPS = 1e-6

T_PROMPT = BATCH * SEQ
T_SAMPLE = DEC_BATCH * DEC_SEQ
T_ALL = T_PROMPT + T_SAMPLE
T_PAD = (BATCH + 1) * SEQ

LANES = 128
SUBLANES = 8
VMEM_LIMIT = 56 * 1024 * 1024

CHUNK = 8
OCT = LANES
N_OCT = SSM_WIDTH // OCT
OCT_GROUPS = OCT // SSM_GROUP_CH
OCT_STATES = OCT_GROUPS * SSM_STATE
CW = CHUNK * OCT
SW = 2 * OCT_STATES

ROUTE_LANES = LANES
EXP_LANE0 = MOE_GROUPS
N_ASSIGN = 2 * T_ALL
TME = 256
N_ITEMS_MAX = N_ASSIGN // TME + MOE_EXPERTS
N_SLOTS = N_ITEMS_MAX * TME


def _cparams(n_axes):
    return pltpu.CompilerParams(dimension_semantics=("arbitrary",) * n_axes,
                                vmem_limit_bytes=VMEM_LIMIT)


def _sigmoid(x):
    return 1.0 / (1.0 + jnp.exp(-x))


def _pack_pairs(x):
    c = x.shape[1] // 2
    hi = lax.bitcast_convert_type(x[:, :c].astype(BF16).astype(F32), U32)
    lo = lax.bitcast_convert_type(x[:, c:].astype(BF16).astype(F32), U32)
    return hi | (lo >> 16)


def _unpack_pairs(u, dtype):
    hi = lax.bitcast_convert_type(u & jnp.uint32(0xFFFF0000), F32)
    lo = lax.bitcast_convert_type(u << 16, F32)
    return jnp.concatenate([hi, lo], axis=1).astype(dtype)


def _gelu_tanh(x):
    c = math.sqrt(2.0 / math.pi)
    return 0.5 * x * (1.0 + jnp.tanh(c * (x + 0.044715 * (x * x * x))))


IN_TM = 1024
IN_TN = 1024
U_WIDTH = POOL_WIDTH + SSM_WIDTH
GATE_WIDTH = 2 * D_MODEL
IN_U_STEPS = U_WIDTH // IN_TN


CAST_RB = 128
IN_RC = 256


def _inproj_body(n_alias, cast_ranges, x_ref, g_ref, w_ref, *rest):
    n_cast = len(cast_ranges)
    cast_in = rest[n_alias:n_alias + n_cast]
    u_ref, gate_ref = rest[n_alias + n_cast:n_alias + n_cast + 2]
    cast_out = rest[n_alias + n_cast + 2:-1]
    xn_ref = rest[-1]
    j = pl.program_id(1)
    step = pl.program_id(0) * pl.num_programs(1) + j
    for (lo, hi), ci, co in zip(cast_ranges, cast_in, cast_out):
        @pl.when((step >= lo) & (step < hi))
        def _():
            co[...] = ci[...].astype(BF16)
    chunks = [slice(r * IN_RC, (r + 1) * IN_RC) for r in range(IN_TM // IN_RC)]
    @pl.when(j == 0)
    def _():
        for rows in chunks:
            x = x_ref[rows, :]
            inv = lax.rsqrt(jnp.mean(x * x, axis=-1, keepdims=True) + EPS)
            xn = ((x * inv) * g_ref[...]).astype(BF16)
            xn_ref[rows, :] = xn
            u_ref[rows, :] = jnp.dot(xn, w_ref[...], preferred_element_type=F32)
    @pl.when((j > 0) & (j < IN_U_STEPS))
    def _():
        u_ref[...] = jnp.dot(xn_ref[...], w_ref[...], preferred_element_type=F32)
    @pl.when(j >= IN_U_STEPS)
    def _():
        for rows in chunks:
            acc = jnp.dot(xn_ref[rows, :], w_ref[...], preferred_element_type=F32)
            gate_ref[rows, :] = _sigmoid(acc).astype(BF16)


def _inproj(x, g, w_bf, row_block0, dst=None, cast=()):
    n_i = x.shape[0] // IN_TM
    n_j = IN_WIDTH // IN_TN
    in_specs = [
        pl.BlockSpec((IN_TM, D_MODEL), lambda i, j: (i, 0)),
        pl.BlockSpec((1, D_MODEL), lambda i, j: (0, 0)),
        pl.BlockSpec((D_MODEL, IN_TN), lambda i, j: (0, j)),
    ]
    args = [x, g, w_bf]
    aliases = {}
    if dst is not None:
        in_specs += [pl.BlockSpec(memory_space=pl.ANY)] * 2
        args += list(dst)
        aliases = {3: 0, 4: 1}
    cast_specs, cast_shapes, cast_ranges, off = [], [], [], 0
    for wc in cast:
        nb = wc.shape[0] // CAST_RB
        cast_specs.append(pl.BlockSpec(
            (CAST_RB, wc.shape[1]), lambda i, j, o=off, nb=nb: (jnp.clip(i * n_j + j - o, 0, nb - 1), 0)))
        cast_shapes.append(jax.ShapeDtypeStruct(wc.shape, BF16))
        cast_ranges.append((off, off + nb))
        off += nb
    assert off <= n_i * n_j
    args += list(cast)
    return pl.pallas_call(
        functools.partial(_inproj_body, len(aliases), tuple(cast_ranges)),
        grid=(n_i, n_j),
        in_specs=in_specs + cast_specs,
        out_specs=(
            pl.BlockSpec((IN_TM, IN_TN), lambda i, j: (i + row_block0, jnp.minimum(j, IN_U_STEPS - 1))),
            pl.BlockSpec((IN_TM, IN_TN), lambda i, j: (i + row_block0, jnp.maximum(j - IN_U_STEPS, 0))),
            *cast_specs,
        ),
        out_shape=(
            jax.ShapeDtypeStruct((T_PAD, U_WIDTH), F32),
            jax.ShapeDtypeStruct((T_ALL, GATE_WIDTH), BF16),
            *cast_shapes,
        ),
        scratch_shapes=[pltpu.VMEM((IN_TM, D_MODEL), BF16)],
        input_output_aliases=aliases,
        compiler_params=_cparams(2),
        name="inproj",
    )(*args)


PP_TM = 512
HIST = 16


def _pool_project(pooled_g, g, w_ref, sc_ref, o_ref):
    y = jnp.dot(pooled_g.astype(BF16), w_ref[g], preferred_element_type=F32)
    lo, hi = g * POOL_OUT_CH, (g + 1) * POOL_OUT_CH
    o_ref[:, lo:hi] = (y * sc_ref[:, lo:hi]).astype(o_ref.dtype)


def _pool_prompt_body(u_ref, w_ref, sc_ref, o_ref, tail_ref, hist_ref):
    i = pl.program_id(1)
    @pl.when(i == 0)
    def _():
        hist_ref[...] = jnp.zeros_like(hist_ref)
    u = u_ref[...]
    ext = jnp.concatenate([hist_ref[...], u], axis=0)
    hist_ref[...] = u[PP_TM - HIST:, :]
    tail_ref[...] = u[PP_TM - HIST:, :]
    pos = i * PP_TM + lax.broadcasted_iota(I32, (PP_TM, 1), 0)
    for g, w in enumerate(POOL_WINDOWS):
        lo, hi = g * POOL_GROUP_CH, (g + 1) * POOL_GROUP_CH
        s = ext[:, lo:hi]
        d = 1
        while d < w:
            s = s + pltpu.roll(s, d, axis=0)
            d *= 2
        cnt = jnp.minimum(w, pos + 1).astype(F32)
        pooled = s[HIST:, :] / cnt - u[:, lo:hi]
        _pool_project(pooled, g, w_ref, sc_ref, o_ref)


def _pool_prompt(u, w_pool_bf, pool_scale):
    n_i = SEQ // PP_TM
    return pl.pallas_call(
        _pool_prompt_body,
        grid=(BATCH, n_i),
        in_specs=[
            pl.BlockSpec((PP_TM, POOL_WIDTH), lambda b, i: (b * n_i + i, 0)),
            pl.BlockSpec((POOL_GROUPS, POOL_GROUP_CH, POOL_OUT_CH), lambda b, i: (0, 0, 0)),
            pl.BlockSpec((1, D_MODEL), lambda b, i: (0, 0)),
        ],
        out_specs=(
            pl.BlockSpec((PP_TM, D_MODEL), lambda b, i: (b * n_i + i, 0)),
            pl.BlockSpec((None, HIST, POOL_WIDTH), lambda b, i: (b, 0, 0)),
        ),
        out_shape=(
            jax.ShapeDtypeStruct((T_ALL, D_MODEL), BF16),
            jax.ShapeDtypeStruct((BATCH, HIST, POOL_WIDTH), F32),
        ),
        scratch_shapes=[pltpu.VMEM((HIST, POOL_WIDTH), F32)],
        compiler_params=_cparams(2),
        name="pool_prompt",
    )(u, w_pool_bf, pool_scale)


def _pool_sample_body(u_ref, hist_ref, w_ref, sc_ref, _dst, o_ref, buf_ref):
    rows = [hist_ref[k] for k in range(POOL_BUF)]
    rows += [u_ref[DEC_BATCH * t:DEC_BATCH * (t + 1), :] for t in range(DEC_SEQ)]
    n = len(rows)
    for k in range(POOL_BUF):
        buf_ref[k] = rows[n - POOL_BUF + k]
    for g, w in enumerate(POOL_WINDOWS):
        lo, hi = g * POOL_GROUP_CH, (g + 1) * POOL_GROUP_CH
        f = [r[:, lo:hi] for r in rows]
        cur = f
        d = 1
        while d < w:
            cur = [cur[k] + cur[k - d] if k - d >= 0 else cur[k] for k in range(n)]
            d *= 2
        pooled = jnp.concatenate(
            [cur[POOL_BUF + t] / float(w) - f[POOL_BUF + t] for t in range(DEC_SEQ)], axis=0)
        _pool_project(pooled, g, w_ref, sc_ref, o_ref)


def _pool_sample(u, hist_tm, w_pool_bf, pool_scale, y_pool):
    blk = T_PROMPT // T_SAMPLE
    return pl.pallas_call(
        _pool_sample_body,
        grid=(1,),
        in_specs=[
            pl.BlockSpec((T_SAMPLE, POOL_WIDTH), lambda i: (blk, 0)),
            pl.BlockSpec((POOL_BUF, DEC_BATCH, POOL_WIDTH), lambda i: (0, 0, 0)),
            pl.BlockSpec((POOL_GROUPS, POOL_GROUP_CH, POOL_OUT_CH), lambda i: (0, 0, 0)),
            pl.BlockSpec((1, D_MODEL), lambda i: (0, 0)),
            pl.BlockSpec(memory_space=pl.ANY),
        ],
        out_specs=(
            pl.BlockSpec((T_SAMPLE, D_MODEL), lambda i: (blk, 0)),
            pl.BlockSpec((POOL_BUF, DEC_BATCH, POOL_WIDTH), lambda i: (0, 0, 0)),
        ),
        out_shape=(
            jax.ShapeDtypeStruct((T_ALL, D_MODEL), BF16),
            jax.ShapeDtypeStruct((POOL_BUF, DEC_BATCH, POOL_WIDTH), F32),
        ),
        input_output_aliases={4: 0},
        compiler_params=_cparams(1),
        name="pool_sample",
    )(u, hist_tm, w_pool_bf, pool_scale, y_pool)


def _ssm_tables(a_re, a_im, log_dt, b_re, b_im, c_re, c_im, d_skip):
    dt = jnp.exp(log_dt)[:, None]
    lr, li = a_re, a_im
    ab_re = jnp.exp(lr * dt) * jnp.cos(li * dt)
    ab_im = jnp.exp(lr * dt) * jnp.sin(li * dt)
    den = lr * lr + li * li
    nr, ni = ab_re - 1.0, ab_im
    q_re = (nr * lr + ni * li) / den
    q_im = (ni * lr - nr * li) / den
    bb_re = q_re[..., None] * b_re - q_im[..., None] * b_im
    bb_im = q_re[..., None] * b_im + q_im[..., None] * b_re

    def lam_rows(ks):
        k = jnp.asarray(ks, F32)[:, None, None]
        m = jnp.exp(k * lr * dt)
        re = (m * jnp.cos(k * li * dt)).reshape(len(ks), N_OCT, OCT_STATES)
        im = (m * jnp.sin(k * li * dt)).reshape(len(ks), N_OCT, OCT_STATES)
        return jnp.concatenate([re, im], axis=-1).transpose(1, 0, 2)

    def compact(re, im):
        v = jnp.concatenate([re, im], axis=-1)
        return v.reshape(N_OCT, OCT, 2 * SSM_STATE)

    bbc = compact(jnp.swapaxes(bb_re, 1, 2), jnp.swapaxes(bb_im, 1, 2))
    ccc = compact(c_re, c_im)
    pw = lam_rows(list(range(2 * SUBLANES)))
    r = jnp.arange(SUBLANES)[None, :, None]
    parts = [jnp.where(r >= dd, lam_rows([CHUNK * dd]), 0.0) for dd in (1, 2, 4)]
    parts.append(lam_rows([CHUNK * kk for kk in range(1, SUBLANES + 1)]))
    tab = jnp.concatenate(parts, axis=1)
    dsk = d_skip.reshape(N_OCT, 1, OCT)
    return bbc, ccc, pw, tab, dsk


def _split_bf16(x):
    hi = x.astype(BF16)
    return hi, (x - hi.astype(F32)).astype(BF16)


def _dot_nt(a, b):
    return lax.dot_general(a, b, (((1,), (1,)), ((), ())), preferred_element_type=F32)


def _build_weights(bbc_ref, ccc_ref, pw_ref, f_ref, gt_ref, m_ref):
    row_gi = lax.broadcasted_iota(I32, (OCT, 1), 0) >> 4
    col = lax.broadcasted_iota(I32, (1, SW), 1)
    col_gi = (col >> 6) & 7
    src = ((col >> 9) << 6) | (col & 63)
    k128 = lax.broadcasted_iota(I32, (2 * SSM_STATE, 1), 0)
    spread = jnp.where(k128 == src, 1.0, 0.0).astype(BF16)
    diag = row_gi == col_gi

    def expand(c_ref):
        hi, lo = _split_bf16(c_ref[...])
        d = (jnp.dot(hi, spread, preferred_element_type=F32)
             + jnp.dot(lo, spread, preferred_element_type=F32))
        d = jnp.where(diag, d, 0.0)
        return d[:, :OCT_STATES], d[:, OCT_STATES:]

    br, bi = expand(bbc_ref)
    cr, ci = expand(ccc_ref)
    chi_r, clo_r = _split_bf16(cr)
    chi_i, clo_i = _split_bf16(ci)

    def lam(k):
        return pw_ref[k:k + 1, :OCT_STATES], pw_ref[k:k + 1, OCT_STATES:]

    def dot3(a, bhi, blo):
        ahi, alo = _split_bf16(a)
        return _dot_nt(ahi, bhi) + _dot_nt(alo, bhi) + _dot_nt(ahi, blo)

    lags = []
    for k in range(CHUNK):
        pr, pi_ = lam(k)
        fr, fi = _cmul(br, bi, pr, pi_)
        s = CHUNK - 1 - k
        f_ref[s * OCT:(s + 1) * OCT, :] = jnp.concatenate([fr, fi], axis=1).astype(BF16)
        lags.append((dot3(fr, chi_r, clo_r) - dot3(fi, chi_i, clo_i)).astype(BF16))
        pr, pi_ = lam(k + 1)
        gr, gi = _cmul(cr, ci, pr, pi_)
        gt_ref[k * OCT:(k + 1) * OCT, :] = jnp.concatenate([gr, -gi], axis=1).astype(BF16)
    zero = jnp.zeros((OCT, OCT), BF16)
    for s in range(CHUNK):
        for t in range(CHUNK):
            m_ref[s * OCT:(s + 1) * OCT, t * OCT:(t + 1) * OCT] = lags[t - s] if t >= s else zero


def _cmul(ar, ai, br, bi):
    return ar * br - ai * bi, ar * bi + ai * br


def _chunk_scan(sloc, tab_ref):
    R = sloc.shape[0]
    nb = R // SUBLANES
    sr, si = sloc[:, :OCT_STATES], sloc[:, OCT_STATES:]
    rowi = lax.broadcasted_iota(I32, (R, 1), 0)
    tr = jnp.where(rowi == 0, 0.0, pltpu.roll(sr, 1, axis=0))
    ti = jnp.where(rowi == 0, 0.0, pltpu.roll(si, 1, axis=0))
    for lvl, d in enumerate((1, 2, 4)):
        mr = tab_ref[lvl * SUBLANES:(lvl + 1) * SUBLANES, :OCT_STATES]
        mi = tab_ref[lvl * SUBLANES:(lvl + 1) * SUBLANES, OCT_STATES:]
        mr = jnp.concatenate([mr] * nb, axis=0)
        mi = jnp.concatenate([mi] * nb, axis=0)
        pr, pi_ = _cmul(mr, mi, pltpu.roll(tr, d, axis=0), pltpu.roll(ti, d, axis=0))
        tr, ti = tr + pr, ti + pi_
    pwr = tab_ref[3 * SUBLANES:4 * SUBLANES, :OCT_STATES]
    pwi = tab_ref[3 * SUBLANES:4 * SUBLANES, OCT_STATES:]
    cr = jnp.zeros((1, OCT_STATES), F32)
    ci = jnp.zeros((1, OCT_STATES), F32)
    out_r, out_i = [], []
    for k in range(nb):
        ar = tr[k * SUBLANES:(k + 1) * SUBLANES, :]
        ai = ti[k * SUBLANES:(k + 1) * SUBLANES, :]
        pr, pi_ = _cmul(pwr, pwi, jnp.broadcast_to(cr, ar.shape), jnp.broadcast_to(ci, ai.shape))
        hr, hi = ar + pr, ai + pi_
        out_r.append(hr)
        out_i.append(hi)
        cr, ci = hr[SUBLANES - 1:, :], hi[SUBLANES - 1:, :]
    hin = jnp.concatenate([jnp.concatenate(out_r, axis=0), jnp.concatenate(out_i, axis=0)], axis=1)
    lr, li = pwr[0:1, :], pwi[0:1, :]
    fr, fi = _cmul(lr, li, cr, ci)
    fin = jnp.concatenate([fr + sr[R - 1:, :], fi + si[R - 1:, :]], axis=1)
    return hin, fin


def _ssm_body(u_ref, h0r_ref, h0i_ref, bbc_ref, ccc_ref, pw_ref, tab_ref, d_ref,
              y_ref, hout_ref, hr_ref, hi_ref, f_ref, gt_ref, m_ref):
    b = pl.program_id(1)

    @pl.when(b == 0)
    def _():
        _build_weights(bbc_ref, ccc_ref, pw_ref, f_ref, gt_ref, m_ref)

    def outputs(xs, xb, hin_bf):
        y = (jnp.dot(xb, m_ref[...], preferred_element_type=F32) + _dot_nt(hin_bf, gt_ref[...]))
        return [_gelu_tanh(y[:, t * OCT:(t + 1) * OCT] + d_ref[...] * xs[t]) for t in range(CHUNK)]

    @pl.when(b < BATCH)
    def _():
        R = SEQ // CHUNK
        xs = [u_ref[pl.ds(s, R, stride=CHUNK), :] for s in range(CHUNK)]
        xb = jnp.concatenate(xs, axis=1).astype(BF16)
        sloc = jnp.dot(xb, f_ref[...], preferred_element_type=F32)
        hin, fin = _chunk_scan(sloc, tab_ref)
        for t, yt in enumerate(outputs(xs, xb, hin.astype(BF16))):
            y_ref[pl.ds(t, R, stride=CHUNK), :] = yt
        hout_ref[...] = fin

    @pl.when(b == BATCH)
    def _():
        B = DEC_BATCH
        xs = [u_ref[B * s:B * (s + 1), :] for s in range(CHUNK)]
        xb = jnp.concatenate(xs, axis=1).astype(BF16)
        sloc = jnp.dot(xb, f_ref[...], preferred_element_type=F32)
        h0r, h0i = h0r_ref[...], h0i_ref[...]
        hin = jnp.concatenate([h0r, h0i], axis=1).astype(BF16)
        for t, yt in enumerate(outputs(xs, xb, hin)):
            y_ref[B * t:B * (t + 1), :] = yt
        lr = tab_ref[3 * SUBLANES:3 * SUBLANES + 1, :OCT_STATES]
        li = tab_ref[3 * SUBLANES:3 * SUBLANES + 1, OCT_STATES:]
        nr, ni = _cmul(lr, li, h0r, h0i)
        hr_ref[...] = nr + sloc[:, :OCT_STATES]
        hi_ref[...] = ni + sloc[:, OCT_STATES:]


def _ssm(u, h0r, h0i, tables):
    col0 = POOL_WIDTH // OCT
    im3 = lambda o, b: (o, 0, 0)
    st_spec = pl.BlockSpec((DEC_BATCH, OCT_STATES), lambda o, b: (0, o))
    return pl.pallas_call(
        _ssm_body,
        grid=(N_OCT, BATCH + 1),
        in_specs=[
            pl.BlockSpec((SEQ, OCT), lambda o, b: (b, col0 + o)), st_spec, st_spec,
            pl.BlockSpec((None, OCT, 2 * SSM_STATE), im3),
            pl.BlockSpec((None, OCT, 2 * SSM_STATE), im3),
            pl.BlockSpec((None, 2 * SUBLANES, SW), im3),
            pl.BlockSpec((None, 4 * SUBLANES, SW), im3),
            pl.BlockSpec((None, 1, OCT), im3),
        ],
        out_specs=(
            pl.BlockSpec((SEQ, OCT), lambda o, b: (b, o)),
            pl.BlockSpec((None, 1, SW), lambda o, b: (jnp.minimum(b, BATCH - 1) * N_OCT + o, 0, 0)),
            st_spec, st_spec,
        ),
        out_shape=(
            jax.ShapeDtypeStruct((T_PAD, SSM_WIDTH), F32),
            jax.ShapeDtypeStruct((BATCH * N_OCT, 1, SW), F32),
            jax.ShapeDtypeStruct((DEC_BATCH, SSM_GROUPS * SSM_STATE), F32),
            jax.ShapeDtypeStruct((DEC_BATCH, SSM_GROUPS * SSM_STATE), F32),
        ),
        scratch_shapes=[pltpu.VMEM((CW, SW), BF16), pltpu.VMEM((CW, SW), BF16), pltpu.VMEM((CW, CW), BF16)],
        compiler_params=_cparams(2),
        name="ssm",
    )(u, h0r, h0i, *tables)


PM_TM = 256
PM_PROMPT_BLOCKS = T_PROMPT // PM_TM
PM_STEPS = T_ALL // PM_TM


def _route(logits, valid, cnt_ref):
    lane = lax.broadcasted_iota(I32, (PM_TM, ROUTE_LANES), 1)
    neg = jnp.float32(-jnp.inf)
    big = jnp.int32(1 << 20)
    is_g = lane < MOE_GROUPS
    gmax = jnp.max(jnp.where(is_g, logits, neg), axis=1, keepdims=True)
    g_idx = jnp.min(jnp.where(is_g & (logits == gmax), lane, big), axis=1, keepdims=True)
    g_den = jnp.sum(jnp.where(is_g, jnp.exp(logits - gmax), 0.0), axis=1, keepdims=True)
    g_val = 1.0 / g_den
    e_lane = lane - EXP_LANE0
    sel = (e_lane >= 0) & (e_lane < MOE_EXPERTS) & ((e_lane >> 3) == g_idx)
    m1 = jnp.max(jnp.where(sel, logits, neg), axis=1, keepdims=True)
    i1 = jnp.min(jnp.where(sel & (logits == m1), lane, big), axis=1, keepdims=True)
    sel2 = sel & (lane != i1)
    m2 = jnp.max(jnp.where(sel2, logits, neg), axis=1, keepdims=True)
    i2 = jnp.min(jnp.where(sel2 & (logits == m2), lane, big), axis=1, keepdims=True)
    e2 = jnp.exp(m2 - m1)
    w1 = g_val / (1.0 + e2)
    w2 = g_val * e2 / (1.0 + e2)
    oh1 = lane == i1
    oh2 = lane == i2
    oh = jnp.where(oh1 | oh2, valid, 0.0)
    rr = lax.broadcasted_iota(I32, (PM_TM, PM_TM), 0)
    cc = lax.broadcasted_iota(I32, (PM_TM, PM_TM), 1)
    tri = jnp.where(cc < rr, 1.0, 0.0).astype(BF16)
    base = cnt_ref[...] + jnp.dot(tri, oh.astype(BF16), preferred_element_type=F32)
    rank1 = jnp.sum(jnp.where(oh1, base, 0.0), axis=1, keepdims=True)
    rank2 = jnp.sum(jnp.where(oh2, base, 0.0), axis=1, keepdims=True)
    cnt_ref[...] = cnt_ref[...] + jnp.sum(oh, axis=0, keepdims=True)
    rt = jnp.where(lane == 0, w1, 0.0)
    rt = jnp.where(lane == 1, w2, rt)
    rt = jnp.where(lane == 2, rank1, rt)
    rt = jnp.where(lane == 3, rank2, rt)
    rt = jnp.where(lane == 4, (i1 - EXP_LANE0).astype(F32), rt)
    rt = jnp.where(lane == 5, (i2 - EXP_LANE0).astype(F32), rt)
    return rt


def _postmix_body(ya_ref, gp_ref, gs_ref, yp_ref, xp_ref, xs_ref, wa_ref, wb_ref, wo_ref,
                  gf_ref, wr_ref, br_ref, h_ref, tn_ref, rt_ref, rtt_ref, cnt_out_ref,
                  cnt_ref, lg_ref):
    i = pl.program_id(0)
    @pl.when(i == 0)
    def _():
        cnt_ref[...] = jnp.zeros_like(cnt_ref)
        lg_ref[...] = jnp.zeros_like(lg_ref)
    prev_logits = lg_ref[...]

    ya = ya_ref[...].astype(BF16)
    a = jnp.dot(ya, wa_ref[...], preferred_element_type=F32)
    bg = jnp.dot(ya, wb_ref[...], preferred_element_type=F32)
    y_ssm = a * _sigmoid(bg)
    merged = (gp_ref[...].astype(F32) * yp_ref[...].astype(F32)
              + gs_ref[...].astype(F32) * y_ssm)
    x = jnp.where(jnp.minimum(i, PM_STEPS - 1) < PM_PROMPT_BLOCKS, xp_ref[...], xs_ref[...])
    h = x + jnp.dot(merged.astype(BF16), wo_ref[...], preferred_element_type=F32)
    h_ref[...] = h
    inv = lax.rsqrt(jnp.mean(h * h, axis=-1, keepdims=True) + EPS)
    tn = (h * inv) * gf_ref[...]
    tn_ref[...] = _pack_pairs(tn)
    t_hi = tn.astype(BF16)
    t_lo = (tn - t_hi.astype(F32)).astype(BF16)
    hh = jnp.dot(t_hi, wr_ref[...], preferred_element_type=F32)
    lh = jnp.dot(t_lo, wr_ref[:, :ROUTE_LANES], preferred_element_type=F32)
    lg_ref[...] = (hh[:, :ROUTE_LANES] + lh + hh[:, ROUTE_LANES:]) + br_ref[...]
    rt = _route(prev_logits, jnp.where(i > 0, 1.0, 0.0), cnt_ref)
    rt_ref[...] = rt
    rtt_ref[...] = rt.T[:8, :]
    cnt_out_ref[...] = cnt_ref[...]


def _postmix(y_act, gates, y_pool, xp, xs, wa, wb, wo, g_ffn, wr_cat, b_r):
    npb = PM_PROMPT_BLOCKS
    const2 = lambda i: (0, 0)
    tile = lambda i: jnp.minimum(i, PM_STEPS - 1)
    return pl.pallas_call(
        _postmix_body,
        grid=(PM_STEPS + 1,),
        in_specs=[
            pl.BlockSpec((PM_TM, SSM_WIDTH), lambda i: (tile(i), 0)),
            pl.BlockSpec((PM_TM, D_MODEL), lambda i: (tile(i), 0)),
            pl.BlockSpec((PM_TM, D_MODEL), lambda i: (tile(i), 1)),
            pl.BlockSpec((PM_TM, D_MODEL), lambda i: (tile(i), 0)),
            pl.BlockSpec((PM_TM, D_MODEL), lambda i: (jnp.minimum(i, npb - 1), 0)),
            pl.BlockSpec((PM_TM, D_MODEL), lambda i: (jnp.maximum(tile(i) - npb, 0), 0)),
            pl.BlockSpec((SSM_WIDTH, D_MODEL), const2, pipeline_mode=pl.Buffered(1)),
            pl.BlockSpec((SSM_WIDTH, D_MODEL), const2, pipeline_mode=pl.Buffered(1)),
            pl.BlockSpec((D_MODEL, D_MODEL), const2, pipeline_mode=pl.Buffered(1)),
            pl.BlockSpec((1, D_MODEL), const2),
            pl.BlockSpec((D_MODEL, 2 * ROUTE_LANES), const2),
            pl.BlockSpec((1, ROUTE_LANES), const2),
        ],
        out_specs=(
            pl.BlockSpec((PM_TM, D_MODEL), lambda i: (tile(i), 0)),
            pl.BlockSpec((PM_TM, D_PACK), lambda i: (tile(i), 0)),
            pl.BlockSpec((PM_TM, ROUTE_LANES), lambda i: (jnp.maximum(i - 1, 0), 0)),
            pl.BlockSpec((8, PM_TM), lambda i: (0, jnp.maximum(i - 1, 0))),
            pl.BlockSpec((1, ROUTE_LANES), const2),
        ),
        out_shape=(
            jax.ShapeDtypeStruct((T_ALL, D_MODEL), F32),
            jax.ShapeDtypeStruct((T_ALL, D_PACK), U32),
            jax.ShapeDtypeStruct((T_ALL, ROUTE_LANES), F32),
            jax.ShapeDtypeStruct((8, T_ALL), F32),
            jax.ShapeDtypeStruct((1, ROUTE_LANES), F32),
        ),
        scratch_shapes=[pltpu.VMEM((1, ROUTE_LANES), F32), pltpu.VMEM((PM_TM, ROUTE_LANES), F32)],
        compiler_params=_cparams(1),
        name="postmix",
    )(y_act, gates, gates, y_pool, xp, xs, wa, wb, wo, g_ffn, wr_cat, b_r)


SC_CH = 96


def _sc_workers():
    info = plsc.get_sparse_core_info()
    return info.num_cores, info.num_cores * info.num_subcores


def _sc_dispatch(tn, slots):
    n_cores, n_workers = _sc_workers()
    per_w = (T_ALL // SC_CH) // n_workers
    assert per_w * n_workers * SC_CH == T_ALL
    slots = slots.reshape(2, n_workers, per_w, SC_CH)

    @functools.partial(
        pl.kernel,
        mesh=plsc.VectorSubcoreMesh(core_axis_name="c", subcore_axis_name="s"),
        out_type=jax.ShapeDtypeStruct((N_SLOTS, D_PACK), U32),
        scratch_types=[pltpu.VMEM((2, per_w, SC_CH), I32), pltpu.VMEM((SC_CH, D_PACK), U32)],
    )
    def k(tn_hbm, slots_hbm, xs_hbm, idx_v, rows_v):
        wid = lax.axis_index("s") * n_cores + lax.axis_index("c")
        c0 = wid * per_w
        pltpu.sync_copy(slots_hbm.at[0, wid], idx_v.at[0])
        pltpu.sync_copy(slots_hbm.at[1, wid], idx_v.at[1])

        @pl.loop(0, per_w)
        def _(c):
            row0 = pl.multiple_of((c0 + c) * SC_CH, SC_CH)
            pltpu.sync_copy(tn_hbm.at[pl.ds(row0, SC_CH)], rows_v)
            pltpu.sync_copy(rows_v, xs_hbm.at[idx_v.at[0, c]])
            pltpu.sync_copy(rows_v, xs_hbm.at[idx_v.at[1, c]])

    return k(tn, slots)


def _sc_collect(ys, slots):
    n_cores, n_workers = _sc_workers()
    per_w = (N_ASSIGN // SC_CH) // n_workers
    assert per_w * n_workers * SC_CH == N_ASSIGN
    slots = slots.reshape(n_workers, per_w, SC_CH)

    @functools.partial(
        pl.kernel,
        mesh=plsc.VectorSubcoreMesh(core_axis_name="c", subcore_axis_name="s"),
        out_type=jax.ShapeDtypeStruct((N_ASSIGN, D_PACK), U32),
        scratch_types=[pltpu.VMEM((per_w, SC_CH), I32), pltpu.VMEM((SC_CH, D_PACK), U32)],
    )
    def k(ys_hbm, slots_hbm, out_hbm, idx_v, rows_v):
        wid = lax.axis_index("s") * n_cores + lax.axis_index("c")
        c0 = wid * per_w
        pltpu.sync_copy(slots_hbm.at[wid], idx_v)

        @pl.loop(0, per_w)
        def _(c):
            row0 = pl.multiple_of((c0 + c) * SC_CH, SC_CH)
            pltpu.sync_copy(ys_hbm.at[idx_v.at[c]], rows_v)
            pltpu.sync_copy(rows_v, out_hbm.at[pl.ds(row0, SC_CH)])

    return k(ys, slots)


W_PARTS = 2


def _expert_body(t0_ref, nt_ref, xs_hbm, wg_hbm, wu_hbm, wd_hbm, ys_hbm,
                 wg_ref, wu_ref, wd_ref, xb_ref, yb_ref, wgb_ref, wub_ref, wdb_ref, wsem, xsem, ysem):
    e = pl.program_id(0)
    n = nt_ref[e]
    g0 = t0_ref[e]
    ws = e & 1

    def w_copies(ex, slot):
        out = []
        for hbm, buf in ((wg_hbm, wg_ref), (wu_hbm, wu_ref), (wd_hbm, wd_ref)):
            rb = buf.shape[1] // W_PARTS
            for p in range(W_PARTS):
                out.append((pltpu.make_async_copy(hbm.at[ex, pl.ds(p * rb, rb)],
                                                  buf.at[slot, pl.ds(p * rb, rb)], wsem.at[slot]), p))
        return out

    @pl.when(e == 0)
    def _():
        for cp, p in w_copies(0, 0):
            cp.start(priority=p)

    @pl.when(e + 1 < MOE_EXPERTS)
    def _():
        for cp, p in w_copies(e + 1, 1 - ws):
            cp.start(priority=p)

    for cp, _ in w_copies(e, ws):
        cp.wait()

    def rows(j):
        return pl.ds(pl.multiple_of((g0 + j) * TME, TME), TME)

    def x_copy(j, s):
        return pltpu.make_async_copy(xs_hbm.at[rows(j)], xb_ref.at[s], xsem.at[s])

    def y_copy(j, s):
        return pltpu.make_async_copy(yb_ref.at[s], ys_hbm.at[rows(j)], ysem.at[s])

    @pl.when(n > 0)
    def _():
        x_copy(0, 0).start()
        wgb_ref[...] = wg_ref[ws].astype(BF16)
        wub_ref[...] = wu_ref[ws].astype(BF16)
        wdb_ref[...] = wd_ref[ws].astype(BF16)

        def tile(j, c):
            s = j & 1
            x_copy(j, s).wait()
            @pl.when(j + 1 < n)
            def _():
                x_copy(j + 1, 1 - s).start()
            @pl.when(j >= 2)
            def _():
                y_copy(j - 2, s).wait()
            x = _unpack_pairs(xb_ref[s], BF16)
            hg = jnp.dot(x, wgb_ref[...], preferred_element_type=F32)
            hu = jnp.dot(x, wub_ref[...], preferred_element_type=F32)
            act = (hg * _sigmoid(hg)) * hu
            yb_ref[s] = _pack_pairs(jnp.dot(act.astype(BF16), wdb_ref[...], preferred_element_type=F32))
            y_copy(j, s).start()
            return c
        lax.fori_loop(0, n, tile, 0)

        @pl.when(n >= 2)
        def _():
            y_copy(n - 2, n & 1).wait()
        y_copy(n - 1, (n - 1) & 1).wait()


def _experts(tile0, tiles, xs, w_eg, w_eu, w_ed):
    any_spec = pl.BlockSpec(memory_space=pl.ANY)
    grid_spec = pltpu.PrefetchScalarGridSpec(
        num_scalar_prefetch=2,
        grid=(MOE_EXPERTS,),
        in_specs=[any_spec] * 4,
        out_specs=any_spec,
        scratch_shapes=[
            pltpu.VMEM((2, D_MODEL, MOE_FF), F32),
            pltpu.VMEM((2, D_MODEL, MOE_FF), F32),
            pltpu.VMEM((2, MOE_FF, D_MODEL), F32),
            pltpu.VMEM((2, TME, D_PACK), U32),
            pltpu.VMEM((2, TME, D_PACK), U32),
            pltpu.VMEM((D_MODEL, MOE_FF), BF16),
            pltpu.VMEM((D_MODEL, MOE_FF), BF16),
            pltpu.VMEM((MOE_FF, D_MODEL), BF16),
            pltpu.SemaphoreType.DMA((2,)),
            pltpu.SemaphoreType.DMA((2,)),
            pltpu.SemaphoreType.DMA((2,)),
        ],
    )
    return pl.pallas_call(
        _expert_body,
        grid_spec=grid_spec,
        out_shape=jax.ShapeDtypeStruct((N_SLOTS, D_PACK), U32),
        compiler_params=_cparams(1),
        name="experts",
    )(tile0, tiles, xs, w_eg, w_eu, w_ed)


FN_TM = 256
FN_PROMPT_BLOCKS = T_PROMPT // FN_TM


def _final_body(h_ref, y0_ref, y1_ref, rt_ref, g_ref, op_ref, os_ref):
    i = pl.program_id(0)
    rt = rt_ref[...]
    z = (h_ref[...] + rt[:, 0:1] * _unpack_pairs(y0_ref[...], F32)
         + rt[:, 1:2] * _unpack_pairs(y1_ref[...], F32))
    inv = lax.rsqrt(jnp.mean(z * z, axis=-1, keepdims=True) + EPS)
    out = (z * inv) * g_ref[...]
    @pl.when(i < FN_PROMPT_BLOCKS)
    def _():
        op_ref[...] = out
    @pl.when(i >= FN_PROMPT_BLOCKS)
    def _():
        os_ref[...] = out


def _final(h, y, route, g_final):
    n = T_ALL // FN_TM
    npb = FN_PROMPT_BLOCKS
    yoff = T_ALL // FN_TM
    return pl.pallas_call(
        _final_body,
        grid=(n,),
        in_specs=[
            pl.BlockSpec((FN_TM, D_MODEL), lambda i: (i, 0)),
            pl.BlockSpec((FN_TM, D_PACK), lambda i: (i, 0)),
            pl.BlockSpec((FN_TM, D_PACK), lambda i: (yoff + i, 0)),
            pl.BlockSpec((FN_TM, ROUTE_LANES), lambda i: (i, 0)),
            pl.BlockSpec((1, D_MODEL), lambda i: (0, 0)),
        ],
        out_specs=(
            pl.BlockSpec((FN_TM, D_MODEL), lambda i: (jnp.minimum(i, npb - 1), 0)),
            pl.BlockSpec((FN_TM, D_MODEL), lambda i: (jnp.maximum(i - npb, 0), 0)),
        ),
        out_shape=(
            jax.ShapeDtypeStruct((T_PROMPT, D_MODEL), F32),
            jax.ShapeDtypeStruct((T_SAMPLE, D_MODEL), F32),
        ),
        compiler_params=_cparams(1),
        name="final",
    )(h, y, y, route, g_final)


def _dispatch_plan(route_t, cnt):
    counts = cnt[0, EXP_LANE0:EXP_LANE0 + MOE_EXPERTS].astype(I32)
    tiles = (counts + (TME - 1)) // TME
    cumt = jnp.cumsum(tiles)
    pad_off = (cumt - tiles) * TME
    rank = route_t[2:4].astype(I32)
    eid = route_t[4:6].astype(I32)
    onehot = eid[None] == jnp.arange(MOE_EXPERTS, dtype=I32)[:, None, None]
    slots = rank + jnp.sum(jnp.where(onehot, pad_off[:, None, None], 0), axis=0)
    return slots, cumt - tiles, tiles


def kernel(x_prompt, x_sample, state_pool, state_ssm_re, state_ssm_im, g_mix, w_in, w_pool,
           pool_scale, ssm_a_re, ssm_a_im, ssm_log_dt, ssm_b_re, ssm_b_im, ssm_c_re, ssm_c_im,
           ssm_d, w_glu_a, w_glu_b, w_out, g_ffn, w_router_group, b_router_group,
           w_router_expert, b_router_expert, w_exp_gate, w_exp_up, w_exp_down, g_final):
    l = 0
    xp = x_prompt.reshape(T_PROMPT, D_MODEL)
    xs = x_sample.transpose(1, 0, 2).reshape(T_SAMPLE, D_MODEL)
    w_in_bf = w_in[l].astype(BF16)
    w_pool_bf = w_pool[l].astype(BF16)
    g_mix2 = g_mix[l].reshape(1, D_MODEL)
    scale2 = pool_scale[l].reshape(1, D_MODEL)

    u, gates, wa_bf, wb_bf, wo_bf = _inproj(xp, g_mix2, w_in_bf, 0,
                                            cast=(w_glu_a[l], w_glu_b[l], w_out[l]))
    u, gates = _inproj(xs, g_mix2, w_in_bf, T_PROMPT // IN_TM, dst=(u, gates))

    y_pool, pool_tail = _pool_prompt(u, w_pool_bf, scale2)
    hist_tm = state_pool[l].transpose(1, 0, 2)
    y_pool, pool_buf_tm = _pool_sample(u, hist_tm, w_pool_bf, scale2, y_pool)

    tables = _ssm_tables(ssm_a_re[l], ssm_a_im[l], ssm_log_dt[l], ssm_b_re[l], ssm_b_im[l],
                         ssm_c_re[l], ssm_c_im[l], ssm_d[l])
    h0r = state_ssm_re[l].reshape(DEC_BATCH, SSM_GROUPS * SSM_STATE)
    h0i = state_ssm_im[l].reshape(DEC_BATCH, SSM_GROUPS * SSM_STATE)
    y_act, h_prompt, hs_re, hs_im = _ssm(u, h0r, h0i, tables)

    assert EXP_LANE0 == MOE_GROUPS
    w_r = jnp.concatenate([w_router_group[l], w_router_expert[l],
                           jnp.zeros((D_MODEL, ROUTE_LANES - EXP_LANE0 - MOE_EXPERTS), F32)], axis=1)
    wr_hi = w_r.astype(BF16)
    wr_cat = jnp.concatenate([wr_hi, (w_r - wr_hi.astype(F32)).astype(BF16)], axis=1)
    b_r = jnp.zeros((1, ROUTE_LANES), F32)
    b_r = b_r.at[0, :MOE_GROUPS].set(b_router_group[l])
    b_r = b_r.at[0, EXP_LANE0:EXP_LANE0 + MOE_EXPERTS].set(b_router_expert[l])

    h, tn, route, route_t, cnt = _postmix(y_act, gates, y_pool, xp, xs, wa_bf, wb_bf, wo_bf,
                                          g_ffn[l].reshape(1, D_MODEL), wr_cat, b_r)
    slots, tile0, tiles = _dispatch_plan(route_t, cnt)
    xs_sorted = _sc_dispatch(tn, slots)
    ys_sorted = _experts(tile0, tiles, xs_sorted, w_exp_gate[l], w_exp_up[l], w_exp_down[l])
    y = _sc_collect(ys_sorted, slots)
    yp, ys = _final(h, y, route, g_final.reshape(1, D_MODEL))

    y_prompt = yp.reshape(BATCH, SEQ, D_MODEL)
    y_sample = ys.reshape(DEC_SEQ, DEC_BATCH, D_MODEL).transpose(1, 0, 2)
    new_pool_prompt = pool_tail[:, HIST - POOL_BUF:, :][None]
    new_pool_sample = pool_buf_tm.transpose(1, 0, 2)[None]
    hp = h_prompt.reshape(BATCH, N_OCT, 2, OCT_GROUPS, SSM_STATE).transpose(2, 0, 1, 3, 4)
    hp = hp.reshape(2, BATCH, SSM_GROUPS, SSM_STATE)
    shp = (1, DEC_BATCH, SSM_GROUPS, SSM_STATE)
    return (y_prompt, y_sample, new_pool_prompt, hp[0][None], hp[1][None], new_pool_sample,
            hs_re.reshape(shp), hs_im.reshape(shp))
```

```python
import functools
import math

import jax
import jax.numpy as jnp
from jax import lax
from jax.experimental import pallas as pl
from jax.experimental.pallas import tpu as pltpu
from jax.experimental.pallas import tpu_sc as plsc

F32 = jnp.float32
BF16 = jnp.bfloat16
I32 = jnp.int32
U32 = jnp.uint32

D_MODEL = 2048
BATCH = 4
SEQ = 2048
DEC_BATCH = 128
DEC_SEQ = 8
PAST_LEN = 16384
POOL_WIDTH = D_MODEL // 2
POOL_WINDOWS = (2, 4, 8, 16)
POOL_GROUPS = len(POOL_WINDOWS)
POOL_GROUP_CH = POOL_WIDTH // POOL_GROUPS
POOL_OUT_CH = D_MODEL // POOL_GROUPS
POOL_BUF = max(POOL_WINDOWS) - 1
SSM_WIDTH = D_MODEL // 2
SSM_GROUP_CH = 16
SSM_GROUPS = SSM_WIDTH // SSM_GROUP_CH
SSM_STATE = 64
IN_WIDTH = POOL_WIDTH + SSM_WIDTH + 2 * D_MODEL
D_PACK = D_MODEL // 2
MOE_GROUPS = 4
MOE_EPG = 8
MOE_EXPERTS = MOE_GROUPS * MOE_EPG
MOE_FF = D_MODEL // 4
EPS = 1e-6

T_PROMPT = BATCH * SEQ
T_SAMPLE = DEC_BATCH * DEC_SEQ
T_ALL = T_PROMPT + T_SAMPLE
T_PAD = (BATCH + 1) * SEQ

LANES = 128
SUBLANES = 8
VMEM_LIMIT = 56 * 1024 * 1024

CHUNK = 8
OCT = LANES
N_OCT = SSM_WIDTH // OCT
OCT_GROUPS = OCT // SSM_GROUP_CH
OCT_STATES = OCT_GROUPS * SSM_STATE
CW = CHUNK * OCT
SW = 2 * OCT_STATES

ROUTE_LANES = LANES
EXP_LANE0 = MOE_GROUPS
N_ASSIGN = 2 * T_ALL
TME = 256
N_ITEMS_MAX = N_ASSIGN // TME + MOE_EXPERTS
N_SLOTS = N_ITEMS_MAX * TME


def _cparams(n_axes):
    return pltpu.CompilerParams(dimension_semantics=("arbitrary",) * n_axes,
                                vmem_limit_bytes=VMEM_LIMIT)


def _sigmoid(x):
    return 1.0 / (1.0 + jnp.exp(-x))


def _pack_pairs(x):
    c = x.shape[1] // 2
    hi = lax.bitcast_convert_type(x[:, :c].astype(BF16).astype(F32), U32)
    lo = lax.bitcast_convert_type(x[:, c:].astype(BF16).astype(F32), U32)
    return hi | (lo >> 16)


def _unpack_pairs(u, dtype):
    hi = lax.bitcast_convert_type(u & jnp.uint32(0xFFFF0000), F32)
    lo = lax.bitcast_convert_type(u << 16, F32)
    return jnp.concatenate([hi, lo], axis=1).astype(dtype)


def _gelu_tanh(x):
    c = math.sqrt(2.0 / math.pi)
    return 0.5 * x * (1.0 + jnp.tanh(c * (x + 0.044715 * (x * x * x))))


IN_TM = 1024
IN_TN = 1024
U_WIDTH = POOL_WIDTH + SSM_WIDTH
GATE_WIDTH = 2 * D_MODEL
IN_U_STEPS = U_WIDTH // IN_TN


CAST_RB = 128
IN_RC = 128


def _inproj_body(n_alias, cast_ranges, x_ref, g_ref, w_ref, *rest):
    n_cast = len(cast_ranges)
    cast_in = rest[n_alias:n_alias + n_cast]
    u_ref, gate_ref = rest[n_alias + n_cast:n_alias + n_cast + 2]
    cast_out = rest[n_alias + n_cast + 2:-1]
    xn_ref = rest[-1]
    j = pl.program_id(1)
    step = pl.program_id(0) * pl.num_programs(1) + j
    for (lo, hi), ci, co in zip(cast_ranges, cast_in, cast_out):
        @pl.when((step >= lo) & (step < hi))
        def _():
            co[...] = ci[...].astype(BF16)
    chunks = [slice(r * IN_RC, (r + 1) * IN_RC) for r in range(IN_TM // IN_RC)]
    @pl.when(j == 0)
    def _():
        for rows in chunks:
            x = x_ref[rows, :]
            inv = lax.rsqrt(jnp.mean(x * x, axis=-1, keepdims=True) + EPS)
            xn = ((x * inv) * g_ref[...]).astype(BF16)
            xn_ref[rows, :] = xn
            u_ref[rows, :] = jnp.dot(xn, w_ref[...], preferred_element_type=F32)
    @pl.when((j > 0) & (j < IN_U_STEPS))
    def _():
        u_ref[...] = jnp.dot(xn_ref[...], w_ref[...], preferred_element_type=F32)
    @pl.when(j >= IN_U_STEPS)
    def _():
        for rows in chunks:
            acc = jnp.dot(xn_ref[rows, :], w_ref[...], preferred_element_type=F32)
            gate_ref[rows, :] = _sigmoid(acc).astype(BF16)


def _inproj(x, g, w_bf, row_block0, dst=None, cast=()):
    n_i = x.shape[0] // IN_TM
    n_j = IN_WIDTH // IN_TN
    in_specs = [
        pl.BlockSpec((IN_TM, D_MODEL), lambda i, j: (i, 0)),
        pl.BlockSpec((1, D_MODEL), lambda i, j: (0, 0)),
        pl.BlockSpec((D_MODEL, IN_TN), lambda i, j: (0, j)),
    ]
    args = [x, g, w_bf]
    aliases = {}
    if dst is not None:
        in_specs += [pl.BlockSpec(memory_space=pl.ANY)] * 2
        args += list(dst)
        aliases = {3: 0, 4: 1}
    cast_specs, cast_shapes, cast_ranges, off = [], [], [], 0
    for wc in cast:
        nb = wc.shape[0] // CAST_RB
        cast_specs.append(pl.BlockSpec(
            (CAST_RB, wc.shape[1]), lambda i, j, o=off, nb=nb: (jnp.clip(i * n_j + j - o, 0, nb - 1), 0)))
        cast_shapes.append(jax.ShapeDtypeStruct(wc.shape, BF16))
        cast_ranges.append((off, off + nb))
        off += nb
    assert off <= n_i * n_j
    args += list(cast)
    return pl.pallas_call(
        functools.partial(_inproj_body, len(aliases), tuple(cast_ranges)),
        grid=(n_i, n_j),
        in_specs=in_specs + cast_specs,
        out_specs=(
            pl.BlockSpec((IN_TM, IN_TN), lambda i, j: (i + row_block0, jnp.minimum(j, IN_U_STEPS - 1))),
            pl.BlockSpec((IN_TM, IN_TN), lambda i, j: (i + row_block0, jnp.maximum(j - IN_U_STEPS, 0))),
            *cast_specs,
        ),
        out_shape=(
            jax.ShapeDtypeStruct((T_PAD, U_WIDTH), F32),
            jax.ShapeDtypeStruct((T_ALL, GATE_WIDTH), BF16),
            *cast_shapes,
        ),
        scratch_shapes=[pltpu.VMEM((IN_TM, D_MODEL), BF16)],
        input_output_aliases=aliases,
        compiler_params=_cparams(2),
        name="inproj",
    )(*args)


PP_TM = 512
HIST = 16


def _pool_project(pooled_g, g, w_ref, sc_ref, o_ref):
    y = jnp.dot(pooled_g.astype(BF16), w_ref[g], preferred_element_type=F32)
    lo, hi = g * POOL_OUT_CH, (g + 1) * POOL_OUT_CH
    o_ref[:, lo:hi] = (y * sc_ref[:, lo:hi]).astype(o_ref.dtype)


def _pool_prompt_body(u_ref, w_ref, sc_ref, o_ref, tail_ref, hist_ref):
    i = pl.program_id(1)
    @pl.when(i == 0)
    def _():
        hist_ref[...] = jnp.zeros_like(hist_ref)
    u = u_ref[...]
    ext = jnp.concatenate([hist_ref[...], u], axis=0)
    hist_ref[...] = u[PP_TM - HIST:, :]
    tail_ref[...] = u[PP_TM - HIST:, :]
    pos = i * PP_TM + lax.broadcasted_iota(I32, (PP_TM, 1), 0)
    for g, w in enumerate(POOL_WINDOWS):
        lo, hi = g * POOL_GROUP_CH, (g + 1) * POOL_GROUP_CH
        s = ext[:, lo:hi]
        d = 1
        while d < w:
            s = s + pltpu.roll(s, d, axis=0)
            d *= 2
        cnt = jnp.minimum(w, pos + 1).astype(F32)
        pooled = s[HIST:, :] / cnt - u[:, lo:hi]
        _pool_project(pooled, g, w_ref, sc_ref, o_ref)


def _pool_prompt(u, w_pool_bf, pool_scale):
    n_i = SEQ // PP_TM
    return pl.pallas_call(
        _pool_prompt_body,
        grid=(BATCH, n_i),
        in_specs=[
            pl.BlockSpec((PP_TM, POOL_WIDTH), lambda b, i: (b * n_i + i, 0)),
            pl.BlockSpec((POOL_GROUPS, POOL_GROUP_CH, POOL_OUT_CH), lambda b, i: (0, 0, 0)),
            pl.BlockSpec((1, D_MODEL), lambda b, i: (0, 0)),
        ],
        out_specs=(
            pl.BlockSpec((PP_TM, D_MODEL), lambda b, i: (b * n_i + i, 0)),
            pl.BlockSpec((None, HIST, POOL_WIDTH), lambda b, i: (b, 0, 0)),
        ),
        out_shape=(
            jax.ShapeDtypeStruct((T_ALL, D_MODEL), BF16),
            jax.ShapeDtypeStruct((BATCH, HIST, POOL_WIDTH), F32),
        ),
        scratch_shapes=[pltpu.VMEM((HIST, POOL_WIDTH), F32)],
        compiler_params=_cparams(2),
        name="pool_prompt",
    )(u, w_pool_bf, pool_scale)


def _pool_sample_body(u_ref, hist_ref, w_ref, sc_ref, _dst, o_ref, buf_ref):
    rows = [hist_ref[k] for k in range(POOL_BUF)]
    rows += [u_ref[DEC_BATCH * t:DEC_BATCH * (t + 1), :] for t in range(DEC_SEQ)]
    n = len(rows)
    for k in range(POOL_BUF):
        buf_ref[k] = rows[n - POOL_BUF + k]
    for g, w in enumerate(POOL_WINDOWS):
        lo, hi = g * POOL_GROUP_CH, (g + 1) * POOL_GROUP_CH
        f = [r[:, lo:hi] for r in rows]
        cur = f
        d = 1
        while d < w:
            cur = [cur[k] + cur[k - d] if k - d >= 0 else cur[k] for k in range(n)]
            d *= 2
        pooled = jnp.concatenate(
            [cur[POOL_BUF + t] / float(w) - f[POOL_BUF + t] for t in range(DEC_SEQ)], axis=0)
        _pool_project(pooled, g, w_ref, sc_ref, o_ref)


def _pool_sample(u, hist_tm, w_pool_bf, pool_scale, y_pool):
    blk = T_PROMPT // T_SAMPLE
    return pl.pallas_call(
        _pool_sample_body,
        grid=(1,),
        in_specs=[
            pl.BlockSpec((T_SAMPLE, POOL_WIDTH), lambda i: (blk, 0)),
            pl.BlockSpec((POOL_BUF, DEC_BATCH, POOL_WIDTH), lambda i: (0, 0, 0)),
            pl.BlockSpec((POOL_GROUPS, POOL_GROUP_CH, POOL_OUT_CH), lambda i: (0, 0, 0)),
            pl.BlockSpec((1, D_MODEL), lambda i: (0, 0)),
            pl.BlockSpec(memory_space=pl.ANY),
        ],
        out_specs=(
            pl.BlockSpec((T_SAMPLE, D_MODEL), lambda i: (blk, 0)),
            pl.BlockSpec((POOL_BUF, DEC_BATCH, POOL_WIDTH), lambda i: (0, 0, 0)),
        ),
        out_shape=(
            jax.ShapeDtypeStruct((T_ALL, D_MODEL), BF16),
            jax.ShapeDtypeStruct((POOL_BUF, DEC_BATCH, POOL_WIDTH), F32),
        ),
        input_output_aliases={4: 0},
        compiler_params=_cparams(1),
        name="pool_sample",
    )(u, hist_tm, w_pool_bf, pool_scale, y_pool)


def _ssm_tables(a_re, a_im, log_dt, b_re, b_im, c_re, c_im, d_skip):
    dt = jnp.exp(log_dt)[:, None]
    lr, li = a_re, a_im
    ab_re = jnp.exp(lr * dt) * jnp.cos(li * dt)
    ab_im = jnp.exp(lr * dt) * jnp.sin(li * dt)
    den = lr * lr + li * li
    nr, ni = ab_re - 1.0, ab_im
    q_re = (nr * lr + ni * li) / den
    q_im = (ni * lr - nr * li) / den
    bb_re = q_re[..., None] * b_re - q_im[..., None] * b_im
    bb_im = q_re[..., None] * b_im + q_im[..., None] * b_re

    def lam_rows(ks):
        k = jnp.asarray(ks, F32)[:, None, None]
        m = jnp.exp(k * lr * dt)
        re = (m * jnp.cos(k * li * dt)).reshape(len(ks), N_OCT, OCT_STATES)
        im = (m * jnp.sin(k * li * dt)).reshape(len(ks), N_OCT, OCT_STATES)
        return jnp.concatenate([re, im], axis=-1).transpose(1, 0, 2)

    def compact(re, im):
        v = jnp.concatenate([re, im], axis=-1)
        return v.reshape(N_OCT, OCT, 2 * SSM_STATE)

    bbc = compact(jnp.swapaxes(bb_re, 1, 2), jnp.swapaxes(bb_im, 1, 2))
    ccc = compact(c_re, c_im)
    pw = lam_rows(list(range(2 * SUBLANES)))
    r = jnp.arange(SUBLANES)[None, :, None]
    parts = [jnp.where(r >= dd, lam_rows([CHUNK * dd]), 0.0) for dd in (1, 2, 4)]
    parts.append(lam_rows([CHUNK * kk for kk in range(1, SUBLANES + 1)]))
    tab = jnp.concatenate(parts, axis=1)
    dsk = d_skip.reshape(N_OCT, 1, OCT)
    return bbc, ccc, pw, tab, dsk


def _split_bf16(x):
    hi = x.astype(BF16)
    return hi, (x - hi.astype(F32)).astype(BF16)


def _dot_nt(a, b):
    return lax.dot_general(a, b, (((1,), (1,)), ((), ())), preferred_element_type=F32)


def _build_weights(bbc_ref, ccc_ref, pw_ref, f_ref, gt_ref, m_ref):
    row_gi = lax.broadcasted_iota(I32, (OCT, 1), 0) >> 4
    col = lax.broadcasted_iota(I32, (1, SW), 1)
    col_gi = (col >> 6) & 7
    src = ((col >> 9) << 6) | (col & 63)
    k128 = lax.broadcasted_iota(I32, (2 * SSM_STATE, 1), 0)
    spread = jnp.where(k128 == src, 1.0, 0.0).astype(BF16)
    diag = row_gi == col_gi

    def expand(c_ref):
        hi, lo = _split_bf16(c_ref[...])
        d = (jnp.dot(hi, spread, preferred_element_type=F32)
             + jnp.dot(lo, spread, preferred_element_type=F32))
        d = jnp.where(diag, d, 0.0)
        return d[:, :OCT_STATES], d[:, OCT_STATES:]

    br, bi = expand(bbc_ref)
    cr, ci = expand(ccc_ref)
    chi_r, clo_r = _split_bf16(cr)
    chi_i, clo_i = _split_bf16(ci)

    def lam(k):
        return pw_ref[k:k + 1, :OCT_STATES], pw_ref[k:k + 1, OCT_STATES:]

    def dot3(a, bhi, blo):
        ahi, alo = _split_bf16(a)
        return _dot_nt(ahi, bhi) + _dot_nt(alo, bhi) + _dot_nt(ahi, blo)

    lags = []
    for k in range(CHUNK):
        pr, pi_ = lam(k)
        fr, fi = _cmul(br, bi, pr, pi_)
        s = CHUNK - 1 - k
        f_ref[s * OCT:(s + 1) * OCT, :] = jnp.concatenate([fr, fi], axis=1).astype(BF16)
        lags.append((dot3(fr, chi_r, clo_r) - dot3(fi, chi_i, clo_i)).astype(BF16))
        pr, pi_ = lam(k + 1)
        gr, gi = _cmul(cr, ci, pr, pi_)
        gt_ref[k * OCT:(k + 1) * OCT, :] = jnp.concatenate([gr, -gi], axis=1).astype(BF16)
    zero = jnp.zeros((OCT, OCT), BF16)
    for s in range(CHUNK):
        for t in range(CHUNK):
            m_ref[s * OCT:(s + 1) * OCT, t * OCT:(t + 1) * OCT] = lags[t - s] if t >= s else zero


def _cmul(ar, ai, br, bi):
    return ar * br - ai * bi, ar * bi + ai * br


def _chunk_scan(sloc, tab_ref):
    R = sloc.shape[0]
    nb = R // SUBLANES
    sr, si = sloc[:, :OCT_STATES], sloc[:, OCT_STATES:]
    rowi = lax.broadcasted_iota(I32, (R, 1), 0)
    tr = jnp.where(rowi == 0, 0.0, pltpu.roll(sr, 1, axis=0))
    ti = jnp.where(rowi == 0, 0.0, pltpu.roll(si, 1, axis=0))
    for lvl, d in enumerate((1, 2, 4)):
        mr = tab_ref[lvl * SUBLANES:(lvl + 1) * SUBLANES, :OCT_STATES]
        mi = tab_ref[lvl * SUBLANES:(lvl + 1) * SUBLANES, OCT_STATES:]
        mr = jnp.concatenate([mr] * nb, axis=0)
        mi = jnp.concatenate([mi] * nb, axis=0)
        pr, pi_ = _cmul(mr, mi, pltpu.roll(tr, d, axis=0), pltpu.roll(ti, d, axis=0))
        tr, ti = tr + pr, ti + pi_
    pwr = tab_ref[3 * SUBLANES:4 * SUBLANES, :OCT_STATES]
    pwi = tab_ref[3 * SUBLANES:4 * SUBLANES, OCT_STATES:]
    cr = jnp.zeros((1, OCT_STATES), F32)
    ci = jnp.zeros((1, OCT_STATES), F32)
    out_r, out_i = [], []
    for k in range(nb):
        ar = tr[k * SUBLANES:(k + 1) * SUBLANES, :]
        ai = ti[k * SUBLANES:(k + 1) * SUBLANES, :]
        pr, pi_ = _cmul(pwr, pwi, jnp.broadcast_to(cr, ar.shape), jnp.broadcast_to(ci, ai.shape))
        hr, hi = ar + pr, ai + pi_
        out_r.append(hr)
        out_i.append(hi)
        cr, ci = hr[SUBLANES - 1:, :], hi[SUBLANES - 1:, :]
    hin = jnp.concatenate([jnp.concatenate(out_r, axis=0), jnp.concatenate(out_i, axis=0)], axis=1)
    lr, li = pwr[0:1, :], pwi[0:1, :]
    fr, fi = _cmul(lr, li, cr, ci)
    fin = jnp.concatenate([fr + sr[R - 1:, :], fi + si[R - 1:, :]], axis=1)
    return hin, fin


def _ssm_body(u_ref, h0r_ref, h0i_ref, bbc_ref, ccc_ref, pw_ref, tab_ref, d_ref,
              y_ref, hout_ref, hr_ref, hi_ref, f_ref, gt_ref, m_ref):
    b = pl.program_id(1)

    @pl.when(b == 0)
    def _():
        _build_weights(bbc_ref, ccc_ref, pw_ref, f_ref, gt_ref, m_ref)

    def outputs(xs, xb, hin_bf):
        y = (jnp.dot(xb, m_ref[...], preferred_element_type=F32) + _dot_nt(hin_bf, gt_ref[...]))
        return [_gelu_tanh(y[:, t * OCT:(t + 1) * OCT] + d_ref[...] * xs[t]) for t in range(CHUNK)]

    @pl.when(b < BATCH)
    def _():
        R = SEQ // CHUNK
        xs = [u_ref[pl.ds(s, R, stride=CHUNK), :] for s in range(CHUNK)]
        xb = jnp.concatenate(xs, axis=1).astype(BF16)
        sloc = jnp.dot(xb, f_ref[...], preferred_element_type=F32)
        hin, fin = _chunk_scan(sloc, tab_ref)
        for t, yt in enumerate(outputs(xs, xb, hin.astype(BF16))):
            y_ref[pl.ds(t, R, stride=CHUNK), :] = yt
        hout_ref[...] = fin

    @pl.when(b == BATCH)
    def _():
        B = DEC_BATCH
        xs = [u_ref[B * s:B * (s + 1), :] for s in range(CHUNK)]
        xb = jnp.concatenate(xs, axis=1).astype(BF16)
        sloc = jnp.dot(xb, f_ref[...], preferred_element_type=F32)
        h0r, h0i = h0r_ref[...], h0i_ref[...]
        hin = jnp.concatenate([h0r, h0i], axis=1).astype(BF16)
        for t, yt in enumerate(outputs(xs, xb, hin)):
            y_ref[B * t:B * (t + 1), :] = yt
        lr = tab_ref[3 * SUBLANES:3 * SUBLANES + 1, :OCT_STATES]
        li = tab_ref[3 * SUBLANES:3 * SUBLANES + 1, OCT_STATES:]
        nr, ni = _cmul(lr, li, h0r, h0i)
        hr_ref[...] = nr + sloc[:, :OCT_STATES]
        hi_ref[...] = ni + sloc[:, OCT_STATES:]


def _ssm(u, h0r, h0i, tables):
    col0 = POOL_WIDTH // OCT
    im3 = lambda o, b: (o, 0, 0)
    st_spec = pl.BlockSpec((DEC_BATCH, OCT_STATES), lambda o, b: (0, o))
    return pl.pallas_call(
        _ssm_body,
        grid=(N_OCT, BATCH + 1),
        in_specs=[
            pl.BlockSpec((SEQ, OCT), lambda o, b: (b, col0 + o)), st_spec, st_spec,
            pl.BlockSpec((None, OCT, 2 * SSM_STATE), im3),
            pl.BlockSpec((None, OCT, 2 * SSM_STATE), im3),
            pl.BlockSpec((None, 2 * SUBLANES, SW), im3),
            pl.BlockSpec((None, 4 * SUBLANES, SW), im3),
            pl.BlockSpec((None, 1, OCT), im3),
        ],
        out_specs=(
            pl.BlockSpec((SEQ, OCT), lambda o, b: (b, o)),
            pl.BlockSpec((None, 1, SW), lambda o, b: (jnp.minimum(b, BATCH - 1) * N_OCT + o, 0, 0)),
            st_spec, st_spec,
        ),
        out_shape=(
            jax.ShapeDtypeStruct((T_PAD, SSM_WIDTH), F32),
            jax.ShapeDtypeStruct((BATCH * N_OCT, 1, SW), F32),
            jax.ShapeDtypeStruct((DEC_BATCH, SSM_GROUPS * SSM_STATE), F32),
            jax.ShapeDtypeStruct((DEC_BATCH, SSM_GROUPS * SSM_STATE), F32),
        ),
        scratch_shapes=[pltpu.VMEM((CW, SW), BF16), pltpu.VMEM((CW, SW), BF16), pltpu.VMEM((CW, CW), BF16)],
        compiler_params=_cparams(2),
        name="ssm",
    )(u, h0r, h0i, *tables)


PM_TM = 256
PM_PROMPT_BLOCKS = T_PROMPT // PM_TM
PM_STEPS = T_ALL // PM_TM


def _route(logits, valid, cnt_ref):
    lane = lax.broadcasted_iota(I32, (PM_TM, ROUTE_LANES), 1)
    neg = jnp.float32(-jnp.inf)
    big = jnp.int32(1 << 20)
    is_g = lane < MOE_GROUPS
    gmax = jnp.max(jnp.where(is_g, logits, neg), axis=1, keepdims=True)
    g_idx = jnp.min(jnp.where(is_g & (logits == gmax), lane, big), axis=1, keepdims=True)
    g_den = jnp.sum(jnp.where(is_g, jnp.exp(logits - gmax), 0.0), axis=1, keepdims=True)
    g_val = 1.0 / g_den
    e_lane = lane - EXP_LANE0
    sel = (e_lane >= 0) & (e_lane < MOE_EXPERTS) & ((e_lane >> 3) == g_idx)
    m1 = jnp.max(jnp.where(sel, logits, neg), axis=1, keepdims=True)
    i1 = jnp.min(jnp.where(sel & (logits == m1), lane, big), axis=1, keepdims=True)
    sel2 = sel & (lane != i1)
    m2 = jnp.max(jnp.where(sel2, logits, neg), axis=1, keepdims=True)
    i2 = jnp.min(jnp.where(sel2 & (logits == m2), lane, big), axis=1, keepdims=True)
    e2 = jnp.exp(m2 - m1)
    w1 = g_val / (1.0 + e2)
    w2 = g_val * e2 / (1.0 + e2)
    oh1 = lane == i1
    oh2 = lane == i2
    oh = jnp.where(oh1 | oh2, valid, 0.0)
    rr = lax.broadcasted_iota(I32, (PM_TM, PM_TM), 0)
    cc = lax.broadcasted_iota(I32, (PM_TM, PM_TM), 1)
    tri = jnp.where(cc < rr, 1.0, 0.0).astype(BF16)
    base = cnt_ref[...] + jnp.dot(tri, oh.astype(BF16), preferred_element_type=F32)
    rank1 = jnp.sum(jnp.where(oh1, base, 0.0), axis=1, keepdims=True)
    rank2 = jnp.sum(jnp.where(oh2, base, 0.0), axis=1, keepdims=True)
    cnt_ref[...] = cnt_ref[...] + jnp.sum(oh, axis=0, keepdims=True)
    rt = jnp.where(lane == 0, w1, 0.0)
    rt = jnp.where(lane == 1, w2, rt)
    rt = jnp.where(lane == 2, rank1, rt)
    rt = jnp.where(lane == 3, rank2, rt)
    rt = jnp.where(lane == 4, (i1 - EXP_LANE0).astype(F32), rt)
    rt = jnp.where(lane == 5, (i2 - EXP_LANE0).astype(F32), rt)
    return rt


def _postmix_body(ya_ref, gp_ref, gs_ref, yp_ref, xp_ref, xs_ref, wa_ref, wb_ref, wo_ref,
                  gf_ref, wr_ref, br_ref, h_ref, tn_ref, rt_ref, rtt_ref, cnt_out_ref,
                  cnt_ref, lg_ref):
    i = pl.program_id(0)
    @pl.when(i == 0)
    def _():
        cnt_ref[...] = jnp.zeros_like(cnt_ref)
        lg_ref[...] = jnp.zeros_like(lg_ref)
    prev_logits = lg_ref[...]

    ya = ya_ref[...].astype(BF16)
    a = jnp.dot(ya, wa_ref[...], preferred_element_type=F32)
    bg = jnp.dot(ya, wb_ref[...], preferred_element_type=F32)
    y_ssm = a * _sigmoid(bg)
    merged = (gp_ref[...].astype(F32) * yp_ref[...].astype(F32)
              + gs_ref[...].astype(F32) * y_ssm)
    x = jnp.where(jnp.minimum(i, PM_STEPS - 1) < PM_PROMPT_BLOCKS, xp_ref[...], xs_ref[...])
    h = x + jnp.dot(merged.astype(BF16), wo_ref[...], preferred_element_type=F32)
    h_ref[...] = h
    inv = lax.rsqrt(jnp.mean(h * h, axis=-1, keepdims=True) + EPS)
    tn = (h * inv) * gf_ref[...]
    tn_ref[...] = _pack_pairs(tn)
    t_hi = tn.astype(BF16)
    t_lo = (tn - t_hi.astype(F32)).astype(BF16)
    hh = jnp.dot(t_hi, wr_ref[...], preferred_element_type=F32)
    lh = jnp.dot(t_lo, wr_ref[:, :ROUTE_LANES], preferred_element_type=F32)
    lg_ref[...] = (hh[:, :ROUTE_LANES] + lh + hh[:, ROUTE_LANES:]) + br_ref[...]
    rt = _route(prev_logits, jnp.where(i > 0, 1.0, 0.0), cnt_ref)
    rt_ref[...] = rt
    rtt_ref[...] = rt.T[:8, :]
    cnt_out_ref[...] = cnt_ref[...]


def _postmix(y_act, gates, y_pool, xp, xs, wa, wb, wo, g_ffn, wr_cat, b_r):
    npb = PM_PROMPT_BLOCKS
    const2 = lambda i: (0, 0)
    tile = lambda i: jnp.minimum(i, PM_STEPS - 1)
    return pl.pallas_call(
        _postmix_body,
        grid=(PM_STEPS + 1,),
        in_specs=[
            pl.BlockSpec((PM_TM, SSM_WIDTH), lambda i: (tile(i), 0)),
            pl.BlockSpec((PM_TM, D_MODEL), lambda i: (tile(i), 0)),
            pl.BlockSpec((PM_TM, D_MODEL), lambda i: (tile(i), 1)),
            pl.BlockSpec((PM_TM, D_MODEL), lambda i: (tile(i), 0)),
            pl.BlockSpec((PM_TM, D_MODEL), lambda i: (jnp.minimum(i, npb - 1), 0)),
            pl.BlockSpec((PM_TM, D_MODEL), lambda i: (jnp.maximum(tile(i) - npb, 0), 0)),
            pl.BlockSpec((SSM_WIDTH, D_MODEL), const2, pipeline_mode=pl.Buffered(1)),
            pl.BlockSpec((SSM_WIDTH, D_MODEL), const2, pipeline_mode=pl.Buffered(1)),
            pl.BlockSpec((D_MODEL, D_MODEL), const2, pipeline_mode=pl.Buffered(1)),
            pl.BlockSpec((1, D_MODEL), const2),
            pl.BlockSpec((D_MODEL, 2 * ROUTE_LANES), const2),
            pl.BlockSpec((1, ROUTE_LANES), const2),
        ],
        out_specs=(
            pl.BlockSpec((PM_TM, D_MODEL), lambda i: (tile(i), 0)),
            pl.BlockSpec((PM_TM, D_PACK), lambda i: (tile(i), 0)),
            pl.BlockSpec((PM_TM, ROUTE_LANES), lambda i: (jnp.maximum(i - 1, 0), 0)),
            pl.BlockSpec((8, PM_TM), lambda i: (0, jnp.maximum(i - 1, 0))),
            pl.BlockSpec((1, ROUTE_LANES), const2),
        ),
        out_shape=(
            jax.ShapeDtypeStruct((T_ALL, D_MODEL), F32),
            jax.ShapeDtypeStruct((T_ALL, D_PACK), U32),
            jax.ShapeDtypeStruct((T_ALL, ROUTE_LANES), F32),
            jax.ShapeDtypeStruct((8, T_ALL), F32),
            jax.ShapeDtypeStruct((1, ROUTE_LANES), F32),
        ),
        scratch_shapes=[pltpu.VMEM((1, ROUTE_LANES), F32), pltpu.VMEM((PM_TM, ROUTE_LANES), F32)],
        compiler_params=_cparams(1),
        name="postmix",
    )(y_act, gates, gates, y_pool, xp, xs, wa, wb, wo, g_ffn, wr_cat, b_r)


SC_CH = 96


def _sc_workers():
    info = plsc.get_sparse_core_info()
    return info.num_cores, info.num_cores * info.num_subcores


def _sc_dispatch(tn, slots):
    n_cores, n_workers = _sc_workers()
    per_w = (T_ALL // SC_CH) // n_workers
    assert per_w * n_workers * SC_CH == T_ALL
    slots = slots.reshape(2, n_workers, per_w, SC_CH)

    @functools.partial(
        pl.kernel,
        mesh=plsc.VectorSubcoreMesh(core_axis_name="c", subcore_axis_name="s"),
        out_type=jax.ShapeDtypeStruct((N_SLOTS, D_PACK), U32),
        scratch_types=[pltpu.VMEM((2, per_w, SC_CH), I32), pltpu.VMEM((SC_CH, D_PACK), U32)],
    )
    def k(tn_hbm, slots_hbm, xs_hbm, idx_v, rows_v):
        wid = lax.axis_index("s") * n_cores + lax.axis_index("c")
        c0 = wid * per_w
        pltpu.sync_copy(slots_hbm.at[0, wid], idx_v.at[0])
        pltpu.sync_copy(slots_hbm.at[1, wid], idx_v.at[1])

        @pl.loop(0, per_w)
        def _(c):
            row0 = pl.multiple_of((c0 + c) * SC_CH, SC_CH)
            pltpu.sync_copy(tn_hbm.at[pl.ds(row0, SC_CH)], rows_v)
            pltpu.sync_copy(rows_v, xs_hbm.at[idx_v.at[0, c]])
            pltpu.sync_copy(rows_v, xs_hbm.at[idx_v.at[1, c]])

    return k(tn, slots)


def _sc_collect(ys, slots):
    n_cores, n_workers = _sc_workers()
    per_w = (N_ASSIGN // SC_CH) // n_workers
    assert per_w * n_workers * SC_CH == N_ASSIGN
    slots = slots.reshape(n_workers, per_w, SC_CH)

    @functools.partial(
        pl.kernel,
        mesh=plsc.VectorSubcoreMesh(core_axis_name="c", subcore_axis_name="s"),
        out_type=jax.ShapeDtypeStruct((N_ASSIGN, D_PACK), U32),
        scratch_types=[pltpu.VMEM((per_w, SC_CH), I32), pltpu.VMEM((SC_CH, D_PACK), U32)],
    )
    def k(ys_hbm, slots_hbm, out_hbm, idx_v, rows_v):
        wid = lax.axis_index("s") * n_cores + lax.axis_index("c")
        c0 = wid * per_w
        pltpu.sync_copy(slots_hbm.at[wid], idx_v)

        @pl.loop(0, per_w)
        def _(c):
            row0 = pl.multiple_of((c0 + c) * SC_CH, SC_CH)
            pltpu.sync_copy(ys_hbm.at[idx_v.at[c]], rows_v)
            pltpu.sync_copy(rows_v, out_hbm.at[pl.ds(row0, SC_CH)])

    return k(ys, slots)


W_PARTS = 2


def _expert_body(t0_ref, nt_ref, xs_hbm, wg_hbm, wu_hbm, wd_hbm, ys_hbm,
                 wg_ref, wu_ref, wd_ref, xb_ref, yb_ref, wgb_ref, wub_ref, wdb_ref, wsem, xsem, ysem):
    e = pl.program_id(0)
    n = nt_ref[e]
    g0 = t0_ref[e]
    ws = e & 1

    def w_copies(ex, slot):
        out = []
        for hbm, buf in ((wg_hbm, wg_ref), (wu_hbm, wu_ref), (wd_hbm, wd_ref)):
            rb = buf.shape[1] // W_PARTS
            for p in range(W_PARTS):
                out.append((pltpu.make_async_copy(hbm.at[ex, pl.ds(p * rb, rb)],
                                                  buf.at[slot, pl.ds(p * rb, rb)], wsem.at[slot]), p))
        return out

    @pl.when(e == 0)
    def _():
        for cp, p in w_copies(0, 0):
            cp.start(priority=p)

    @pl.when(e + 1 < MOE_EXPERTS)
    def _():
        for cp, p in w_copies(e + 1, 1 - ws):
            cp.start(priority=p)

    for cp, _ in w_copies(e, ws):
        cp.wait()

    def rows(j):
        return pl.ds(pl.multiple_of((g0 + j) * TME, TME), TME)

    def x_copy(j, s):
        return pltpu.make_async_copy(xs_hbm.at[rows(j)], xb_ref.at[s], xsem.at[s])

    def y_copy(j, s):
        return pltpu.make_async_copy(yb_ref.at[s], ys_hbm.at[rows(j)], ysem.at[s])

    @pl.when(n > 0)
    def _():
        x_copy(0, 0).start()
        wgb_ref[...] = wg_ref[ws].astype(BF16)
        wub_ref[...] = wu_ref[ws].astype(BF16)
        wdb_ref[...] = wd_ref[ws].astype(BF16)

        def tile(j, c):
            s = j & 1
            x_copy(j, s).wait()
            @pl.when(j + 1 < n)
            def _():
                x_copy(j + 1, 1 - s).start()
            @pl.when(j >= 2)
            def _():
                y_copy(j - 2, s).wait()
            x = _unpack_pairs(xb_ref[s], BF16)
            hg = jnp.dot(x, wgb_ref[...], preferred_element_type=F32)
            hu = jnp.dot(x, wub_ref[...], preferred_element_type=F32)
            act = (hg * _sigmoid(hg)) * hu
            yb_ref[s] = _pack_pairs(jnp.dot(act.astype(BF16), wdb_ref[...], preferred_element_type=F32))
            y_copy(j, s).start()
            return c
        lax.fori_loop(0, n, tile, 0)

        @pl.when(n >= 2)
        def _():
            y_copy(n - 2, n & 1).wait()
        y_copy(n - 1, (n - 1) & 1).wait()


def _experts(tile0, tiles, xs, w_eg, w_eu, w_ed):
    any_spec = pl.BlockSpec(memory_space=pl.ANY)
    grid_spec = pltpu.PrefetchScalarGridSpec(
        num_scalar_prefetch=2,
        grid=(MOE_EXPERTS,),
        in_specs=[any_spec] * 4,
        out_specs=any_spec,
        scratch_shapes=[
            pltpu.VMEM((2, D_MODEL, MOE_FF), F32),
            pltpu.VMEM((2, D_MODEL, MOE_FF), F32),
            pltpu.VMEM((2, MOE_FF, D_MODEL), F32),
            pltpu.VMEM((2, TME, D_PACK), U32),
            pltpu.VMEM((2, TME, D_PACK), U32),
            pltpu.VMEM((D_MODEL, MOE_FF), BF16),
            pltpu.VMEM((D_MODEL, MOE_FF), BF16),
            pltpu.VMEM((MOE_FF, D_MODEL), BF16),
            pltpu.SemaphoreType.DMA((2,)),
            pltpu.SemaphoreType.DMA((2,)),
            pltpu.SemaphoreType.DMA((2,)),
        ],
    )
    return pl.pallas_call(
        _expert_body,
        grid_spec=grid_spec,
        out_shape=jax.ShapeDtypeStruct((N_SLOTS, D_PACK), U32),
        compiler_params=_cparams(1),
        name="experts",
    )(tile0, tiles, xs, w_eg, w_eu, w_ed)


FN_TM = 512
FN_PROMPT_BLOCKS = T_PROMPT // FN_TM


def _final_body(h_ref, y0_ref, y1_ref, rt_ref, g_ref, op_ref, os_ref):
    i = pl.program_id(0)
    rt = rt_ref[...]
    z = (h_ref[...] + rt[:, 0:1] * _unpack_pairs(y0_ref[...], F32)
         + rt[:, 1:2] * _unpack_pairs(y1_ref[...], F32))
    inv = lax.rsqrt(jnp.mean(z * z, axis=-1, keepdims=True) + EPS)
    out = (z * inv) * g_ref[...]
    @pl.when(i < FN_PROMPT_BLOCKS)
    def _():
        op_ref[...] = out
    @pl.when(i >= FN_PROMPT_BLOCKS)
    def _():
        os_ref[...] = out


def _final(h, y, route, g_final):
    n = T_ALL // FN_TM
    npb = FN_PROMPT_BLOCKS
    yoff = T_ALL // FN_TM
    return pl.pallas_call(
        _final_body,
        grid=(n,),
        in_specs=[
            pl.BlockSpec((FN_TM, D_MODEL), lambda i: (i, 0)),
            pl.BlockSpec((FN_TM, D_PACK), lambda i: (i, 0)),
            pl.BlockSpec((FN_TM, D_PACK), lambda i: (yoff + i, 0)),
            pl.BlockSpec((FN_TM, ROUTE_LANES), lambda i: (i, 0)),
            pl.BlockSpec((1, D_MODEL), lambda i: (0, 0)),
        ],
        out_specs=(
            pl.BlockSpec((FN_TM, D_MODEL), lambda i: (jnp.minimum(i, npb - 1), 0)),
            pl.BlockSpec((FN_TM, D_MODEL), lambda i: (jnp.maximum(i - npb, 0), 0)),
        ),
        out_shape=(
            jax.ShapeDtypeStruct((T_PROMPT, D_MODEL), F32),
            jax.ShapeDtypeStruct((T_SAMPLE, D_MODEL), F32),
        ),
        compiler_params=_cparams(1),
        name="final",
    )(h, y, y, route, g_final)


def _dispatch_plan(route_t, cnt):
    counts = cnt[0, EXP_LANE0:EXP_LANE0 + MOE_EXPERTS].astype(I32)
    tiles = (counts + (TME - 1)) // TME
    cumt = jnp.cumsum(tiles)
    pad_off = (cumt - tiles) * TME
    rank = route_t[2:4].astype(I32)
    eid = route_t[4:6].astype(I32)
    onehot = eid[None] == jnp.arange(MOE_EXPERTS, dtype=I32)[:, None, None]
    slots = rank + jnp.sum(jnp.where(onehot, pad_off[:, None, None], 0), axis=0)
    return slots, cumt - tiles, tiles


def kernel(x_prompt, x_sample, state_pool, state_ssm_re, state_ssm_im, g_mix, w_in, w_pool,
           pool_scale, ssm_a_re, ssm_a_im, ssm_log_dt, ssm_b_re, ssm_b_im, ssm_c_re, ssm_c_im,
           ssm_d, w_glu_a, w_glu_b, w_out, g_ffn, w_router_group, b_router_group,
           w_router_expert, b_router_expert, w_exp_gate, w_exp_up, w_exp_down, g_final):
    l = 0
    xp = x_prompt.reshape(T_PROMPT, D_MODEL)
    xs = x_sample.transpose(1, 0, 2).reshape(T_SAMPLE, D_MODEL)
    w_in_bf = w_in[l].astype(BF16)
    w_pool_bf = w_pool[l].astype(BF16)
    g_mix2 = g_mix[l].reshape(1, D_MODEL)
    scale2 = pool_scale[l].reshape(1, D_MODEL)

    u, gates, wa_bf, wb_bf, wo_bf = _inproj(xp, g_mix2, w_in_bf, 0,
                                            cast=(w_glu_a[l], w_glu_b[l], w_out[l]))
    u, gates = _inproj(xs, g_mix2, w_in_bf, T_PROMPT // IN_TM, dst=(u, gates))

    y_pool, pool_tail = _pool_prompt(u, w_pool_bf, scale2)
    hist_tm = state_pool[l].transpose(1, 0, 2)
    y_pool, pool_buf_tm = _pool_sample(u, hist_tm, w_pool_bf, scale2, y_pool)

    tables = _ssm_tables(ssm_a_re[l], ssm_a_im[l], ssm_log_dt[l], ssm_b_re[l], ssm_b_im[l],
                         ssm_c_re[l], ssm_c_im[l], ssm_d[l])
    h0r = state_ssm_re[l].reshape(DEC_BATCH, SSM_GROUPS * SSM_STATE)
    h0i = state_ssm_im[l].reshape(DEC_BATCH, SSM_GROUPS * SSM_STATE)
    y_act, h_prompt, hs_re, hs_im = _ssm(u, h0r, h0i, tables)

    assert EXP_LANE0 == MOE_GROUPS
    w_r = jnp.concatenate([w_router_group[l], w_router_expert[l],
                           jnp.zeros((D_MODEL, ROUTE_LANES - EXP_LANE0 - MOE_EXPERTS), F32)], axis=1)
    wr_hi = w_r.astype(BF16)
    wr_cat = jnp.concatenate([wr_hi, (w_r - wr_hi.astype(F32)).astype(BF16)], axis=1)
    b_r = jnp.zeros((1, ROUTE_LANES), F32)
    b_r = b_r.at[0, :MOE_GROUPS].set(b_router_group[l])
    b_r = b_r.at[0, EXP_LANE0:EXP_LANE0 + MOE_EXPERTS].set(b_router_expert[l])

    h, tn, route, route_t, cnt = _postmix(y_act, gates, y_pool, xp, xs, wa_bf, wb_bf, wo_bf,
                                          g_ffn[l].reshape(1, D_MODEL), wr_cat, b_r)
    slots, tile0, tiles = _dispatch_plan(route_t, cnt)
    xs_sorted = _sc_dispatch(tn, slots)
    ys_sorted = _experts(tile0, tiles, xs_sorted, w_exp_gate[l], w_exp_up[l], w_exp_down[l])
    y = _sc_collect(ys_sorted, slots)
    yp, ys = _final(h, y, route, g_final.reshape(1, D_MODEL))

    y_prompt = yp.reshape(BATCH, SEQ, D_MODEL)
    y_sample = ys.reshape(DEC_SEQ, DEC_BATCH, D_MODEL).transpose(1, 0, 2)
    new_pool_prompt = pool_tail[:, HIST - POOL_BUF:, :][None]
    new_pool_sample = pool_buf_tm.transpose(1, 0, 2)[None]
    hp = h_prompt.reshape(BATCH, N_OCT, 2, OCT_GROUPS, SSM_STATE).transpose(2, 0, 1, 3, 4)
    hp = hp.reshape(2, BATCH, SSM_GROUPS, SSM_STATE)
    shp = (1, DEC_BATCH, SSM_GROUPS, SSM_STATE)
    return (y_prompt, y_sample, new_pool_prompt, hp[0][None], hp[1][None], new_pool_sample,
            hs_re.reshape(shp), hs_im.reshape(shp))
```

```python
import functools
import math

import jax
import jax.numpy as jnp
from jax import lax
from jax.experimental import pallas as pl
from jax.experimental.pallas import tpu as pltpu
from jax.experimental.pallas import tpu_sc as plsc

F32 = jnp.float32
BF16 = jnp.bfloat16
I32 = jnp.int32
U32 = jnp.uint32

D_MODEL = 2048
BATCH = 4
SEQ = 2048
DEC_BATCH = 128
DEC_SEQ = 8
PAST_LEN = 16384
POOL_WIDTH = D_MODEL // 2
POOL_WINDOWS = (2, 4, 8, 16)
POOL_GROUPS = len(POOL_WINDOWS)
POOL_GROUP_CH = POOL_WIDTH // POOL_GROUPS
POOL_OUT_CH = D_MODEL // POOL_GROUPS
POOL_BUF = max(POOL_WINDOWS) - 1
SSM_WIDTH = D_MODEL // 2
SSM_GROUP_CH = 16
SSM_GROUPS = SSM_WIDTH // SSM_GROUP_CH
SSM_STATE = 64
IN_WIDTH = POOL_WIDTH + SSM_WIDTH + 2 * D_MODEL
D_PACK = D_MODEL // 2
MOE_GROUPS = 4
MOE_EPG = 8
MOE_EXPERTS = MOE_GROUPS * MOE_EPG
MOE_FF = D_MODEL // 4
EPS = 1e-6

T_PROMPT = BATCH * SEQ
T_SAMPLE = DEC_BATCH * DEC_SEQ
T_ALL = T_PROMPT + T_SAMPLE
T_PAD = (BATCH + 1) * SEQ

LANES = 128
SUBLANES = 8
VMEM_LIMIT = 56 * 1024 * 1024

CHUNK = 8
OCT = LANES
N_OCT = SSM_WIDTH // OCT
OCT_GROUPS = OCT // SSM_GROUP_CH
OCT_STATES = OCT_GROUPS * SSM_STATE
CW = CHUNK * OCT
SW = 2 * OCT_STATES

ROUTE_LANES = LANES
EXP_LANE0 = MOE_GROUPS
N_ASSIGN = 2 * T_ALL
TME = 256
N_ITEMS_MAX = N_ASSIGN // TME + MOE_EXPERTS
N_SLOTS = N_ITEMS_MAX * TME


def _cparams(n_axes):
    return pltpu.CompilerParams(dimension_semantics=("arbitrary",) * n_axes,
                                vmem_limit_bytes=VMEM_LIMIT)


def _sigmoid(x):
    return 1.0 / (1.0 + jnp.exp(-x))


def _pack_pairs(x):
    c = x.shape[1] // 2
    hi = lax.bitcast_convert_type(x[:, :c].astype(BF16).astype(F32), U32)
    lo = lax.bitcast_convert_type(x[:, c:].astype(BF16).astype(F32), U32)
    return hi | (lo >> 16)


def _unpack_pairs(u, dtype):
    hi = lax.bitcast_convert_type(u & jnp.uint32(0xFFFF0000), F32)
    lo = lax.bitcast_convert_type(u << 16, F32)
    return jnp.concatenate([hi, lo], axis=1).astype(dtype)


def _gelu_tanh(x):
    c = math.sqrt(2.0 / math.pi)
    return 0.5 * x * (1.0 + jnp.tanh(c * (x + 0.044715 * (x * x * x))))


IN_TM = 1024
IN_TN = 1024
U_WIDTH = POOL_WIDTH + SSM_WIDTH
GATE_WIDTH = 2 * D_MODEL
IN_U_STEPS = U_WIDTH // IN_TN


CAST_RB = 128
IN_RC = 128


def _inproj_body(n_alias, cast_ranges, emit_w, x_ref, g_ref, w_ref, *rest):
    n_cast = len(cast_ranges)
    cast_in = rest[n_alias:n_alias + n_cast]
    u_ref, gate_ref = rest[n_alias + n_cast:n_alias + n_cast + 2]
    cast_out = rest[n_alias + n_cast + 2:n_alias + 2 * n_cast + 2]
    xn_ref = rest[-1]
    j = pl.program_id(1)

    def weights():
        if not emit_w:
            return w_ref
        wbf_ref = rest[-2]
        wbf_ref[...] = w_ref[...].astype(BF16)
        return wbf_ref

    step = pl.program_id(0) * pl.num_programs(1) + j
    for (lo, hi), ci, co in zip(cast_ranges, cast_in, cast_out):
        @pl.when((step >= lo) & (step < hi))
        def _():
            co[...] = ci[...].astype(BF16)
    chunks = [slice(r * IN_RC, (r + 1) * IN_RC) for r in range(IN_TM // IN_RC)]
    @pl.when(j == 0)
    def _():
        w = weights()
        for rows in chunks:
            x = x_ref[rows, :]
            inv = lax.rsqrt(jnp.mean(x * x, axis=-1, keepdims=True) + EPS)
            xn = ((x * inv) * g_ref[...]).astype(BF16)
            xn_ref[rows, :] = xn
            u_ref[rows, :] = jnp.dot(xn, w[...], preferred_element_type=F32)
    @pl.when((j > 0) & (j < IN_U_STEPS))
    def _():
        u_ref[...] = jnp.dot(xn_ref[...], weights()[...], preferred_element_type=F32)
    @pl.when(j >= IN_U_STEPS)
    def _():
        w = weights()
        for rows in chunks:
            acc = jnp.dot(xn_ref[rows, :], w[...], preferred_element_type=F32)
            gate_ref[rows, :] = _sigmoid(acc).astype(BF16)


def _inproj(x, g, w, row_block0, dst=None, cast=(), x_block0=0, n_i=None, emit_w=False):
    if n_i is None:
        n_i = x.shape[0] // IN_TM
    assert not emit_w or n_i == 1
    n_j = IN_WIDTH // IN_TN
    x_mode = dict(pipeline_mode=pl.Buffered(1)) if emit_w else {}
    in_specs = [
        pl.BlockSpec((IN_TM, D_MODEL), lambda i, j: (i + x_block0, 0), **x_mode),
        pl.BlockSpec((1, D_MODEL), lambda i, j: (0, 0)),
        pl.BlockSpec((D_MODEL, IN_TN), lambda i, j: (0, j)),
    ]
    args = [x, g, w]
    aliases = {}
    if dst is not None:
        in_specs += [pl.BlockSpec(memory_space=pl.ANY)] * 2
        args += list(dst)
        aliases = {3: 0, 4: 1}
    cast_specs, cast_shapes, cast_ranges, off = [], [], [], 0
    for wc in cast:
        nb = wc.shape[0] // CAST_RB
        cast_specs.append(pl.BlockSpec(
            (CAST_RB, wc.shape[1]), lambda i, j, o=off, nb=nb: (jnp.clip(i * n_j + j - o, 0, nb - 1), 0)))
        cast_shapes.append(jax.ShapeDtypeStruct(wc.shape, BF16))
        cast_ranges.append((off, off + nb))
        off += nb
    assert off <= n_i * n_j
    args += list(cast)
    w_specs, w_shapes = [], []
    if emit_w:
        w_specs = [pl.BlockSpec((D_MODEL, IN_TN), lambda i, j: (0, j))]
        w_shapes = [jax.ShapeDtypeStruct((D_MODEL, IN_WIDTH), BF16)]
    return pl.pallas_call(
        functools.partial(_inproj_body, len(aliases), tuple(cast_ranges), emit_w),
        grid=(n_i, n_j),
        in_specs=in_specs + cast_specs,
        out_specs=(
            pl.BlockSpec((IN_TM, IN_TN), lambda i, j: (i + row_block0, jnp.minimum(j, IN_U_STEPS - 1))),
            pl.BlockSpec((IN_TM, IN_TN), lambda i, j: (i + row_block0, jnp.maximum(j - IN_U_STEPS, 0))),
            *cast_specs, *w_specs,
        ),
        out_shape=(
            jax.ShapeDtypeStruct((T_PAD, U_WIDTH), F32),
            jax.ShapeDtypeStruct((T_ALL, GATE_WIDTH), BF16),
            *cast_shapes, *w_shapes,
        ),
        scratch_shapes=[pltpu.VMEM((IN_TM, D_MODEL), BF16)],
        input_output_aliases=aliases,
        compiler_params=_cparams(2),
        name="inproj",
    )(*args)


PP_TM = 512
HIST = 16


def _pool_project(pooled_g, g, w_ref, sc_ref, o_ref):
    y = jnp.dot(pooled_g.astype(BF16), w_ref[g], preferred_element_type=F32)
    lo, hi = g * POOL_OUT_CH, (g + 1) * POOL_OUT_CH
    o_ref[:, lo:hi] = (y * sc_ref[:, lo:hi]).astype(o_ref.dtype)


def _pool_prompt_body(u_ref, w_ref, sc_ref, o_ref, tail_ref, hist_ref):
    i = pl.program_id(1)
    @pl.when(i == 0)
    def _():
        hist_ref[...] = jnp.zeros_like(hist_ref)
    u = u_ref[...]
    ext = jnp.concatenate([hist_ref[...], u], axis=0)
    hist_ref[...] = u[PP_TM - HIST:, :]
    tail_ref[...] = u[PP_TM - HIST:, :]
    pos = i * PP_TM + lax.broadcasted_iota(I32, (PP_TM, 1), 0)
    for g, w in enumerate(POOL_WINDOWS):
        lo, hi = g * POOL_GROUP_CH, (g + 1) * POOL_GROUP_CH
        s = ext[:, lo:hi]
        d = 1
        while d < w:
            s = s + pltpu.roll(s, d, axis=0)
            d *= 2
        cnt = jnp.minimum(w, pos + 1).astype(F32)
        pooled = s[HIST:, :] / cnt - u[:, lo:hi]
        _pool_project(pooled, g, w_ref, sc_ref, o_ref)


def _pool_prompt(u, w_pool_bf, pool_scale):
    n_i = SEQ // PP_TM
    return pl.pallas_call(
        _pool_prompt_body,
        grid=(BATCH, n_i),
        in_specs=[
            pl.BlockSpec((PP_TM, POOL_WIDTH), lambda b, i: (b * n_i + i, 0)),
            pl.BlockSpec((POOL_GROUPS, POOL_GROUP_CH, POOL_OUT_CH), lambda b, i: (0, 0, 0)),
            pl.BlockSpec((1, D_MODEL), lambda b, i: (0, 0)),
        ],
        out_specs=(
            pl.BlockSpec((PP_TM, D_MODEL), lambda b, i: (b * n_i + i, 0)),
            pl.BlockSpec((None, HIST, POOL_WIDTH), lambda b, i: (b, 0, 0)),
        ),
        out_shape=(
            jax.ShapeDtypeStruct((T_ALL, D_MODEL), BF16),
            jax.ShapeDtypeStruct((BATCH, HIST, POOL_WIDTH), F32),
        ),
        scratch_shapes=[pltpu.VMEM((HIST, POOL_WIDTH), F32)],
        compiler_params=_cparams(2),
        name="pool_prompt",
    )(u, w_pool_bf, pool_scale)


def _pool_sample_body(u_ref, hist_ref, w_ref, sc_ref, _dst, o_ref, buf_ref):
    rows = [hist_ref[k] for k in range(POOL_BUF)]
    rows += [u_ref[DEC_BATCH * t:DEC_BATCH * (t + 1), :] for t in range(DEC_SEQ)]
    n = len(rows)
    for k in range(POOL_BUF):
        buf_ref[k] = rows[n - POOL_BUF + k]
    for g, w in enumerate(POOL_WINDOWS):
        lo, hi = g * POOL_GROUP_CH, (g + 1) * POOL_GROUP_CH
        f = [r[:, lo:hi] for r in rows]
        cur = f
        d = 1
        while d < w:
            cur = [cur[k] + cur[k - d] if k - d >= 0 else cur[k] for k in range(n)]
            d *= 2
        pooled = jnp.concatenate(
            [cur[POOL_BUF + t] / float(w) - f[POOL_BUF + t] for t in range(DEC_SEQ)], axis=0)
        _pool_project(pooled, g, w_ref, sc_ref, o_ref)


def _pool_sample(u, hist_tm, w_pool_bf, pool_scale, y_pool):
    blk = T_PROMPT // T_SAMPLE
    return pl.pallas_call(
        _pool_sample_body,
        grid=(1,),
        in_specs=[
            pl.BlockSpec((T_SAMPLE, POOL_WIDTH), lambda i: (blk, 0)),
            pl.BlockSpec((POOL_BUF, DEC_BATCH, POOL_WIDTH), lambda i: (0, 0, 0)),
            pl.BlockSpec((POOL_GROUPS, POOL_GROUP_CH, POOL_OUT_CH), lambda i: (0, 0, 0)),
            pl.BlockSpec((1, D_MODEL), lambda i: (0, 0)),
            pl.BlockSpec(memory_space=pl.ANY),
        ],
        out_specs=(
            pl.BlockSpec((T_SAMPLE, D_MODEL), lambda i: (blk, 0)),
            pl.BlockSpec((POOL_BUF, DEC_BATCH, POOL_WIDTH), lambda i: (0, 0, 0)),
        ),
        out_shape=(
            jax.ShapeDtypeStruct((T_ALL, D_MODEL), BF16),
            jax.ShapeDtypeStruct((POOL_BUF, DEC_BATCH, POOL_WIDTH), F32),
        ),
        input_output_aliases={4: 0},
        compiler_params=_cparams(1),
        name="pool_sample",
    )(u, hist_tm, w_pool_bf, pool_scale, y_pool)


def _ssm_tables(a_re, a_im, log_dt, b_re, b_im, c_re, c_im, d_skip):
    dt = jnp.exp(log_dt)[:, None]
    lr, li = a_re, a_im
    ab_re = jnp.exp(lr * dt) * jnp.cos(li * dt)
    ab_im = jnp.exp(lr * dt) * jnp.sin(li * dt)
    den = lr * lr + li * li
    nr, ni = ab_re - 1.0, ab_im
    q_re = (nr * lr + ni * li) / den
    q_im = (ni * lr - nr * li) / den
    bb_re = q_re[..., None] * b_re - q_im[..., None] * b_im
    bb_im = q_re[..., None] * b_im + q_im[..., None] * b_re

    def lam_rows(ks):
        k = jnp.asarray(ks, F32)[:, None, None]
        m = jnp.exp(k * lr * dt)
        re = (m * jnp.cos(k * li * dt)).reshape(len(ks), N_OCT, OCT_STATES)
        im = (m * jnp.sin(k * li * dt)).reshape(len(ks), N_OCT, OCT_STATES)
        return jnp.concatenate([re, im], axis=-1).transpose(1, 0, 2)

    def compact(re, im):
        v = jnp.concatenate([re, im], axis=-1)
        return v.reshape(N_OCT, OCT, 2 * SSM_STATE)

    bbc = compact(jnp.swapaxes(bb_re, 1, 2), jnp.swapaxes(bb_im, 1, 2))
    ccc = compact(c_re, c_im)
    pw = lam_rows(list(range(2 * SUBLANES)))
    r = jnp.arange(SUBLANES)[None, :, None]
    parts = [jnp.where(r >= dd, lam_rows([CHUNK * dd]), 0.0) for dd in (1, 2, 4)]
    parts.append(lam_rows([CHUNK * kk for kk in range(1, SUBLANES + 1)]))
    tab = jnp.concatenate(parts, axis=1)
    dsk = d_skip.reshape(N_OCT, 1, OCT)
    return bbc, ccc, pw, tab, dsk


def _split_bf16(x):
    hi = x.astype(BF16)
    return hi, (x - hi.astype(F32)).astype(BF16)


def _dot_nt(a, b):
    return lax.dot_general(a, b, (((1,), (1,)), ((), ())), preferred_element_type=F32)


def _build_weights(bbc_ref, ccc_ref, pw_ref, f_ref, gt_ref, m_ref):
    row_gi = lax.broadcasted_iota(I32, (OCT, 1), 0) >> 4
    col = lax.broadcasted_iota(I32, (1, SW), 1)
    col_gi = (col >> 6) & 7
    src = ((col >> 9) << 6) | (col & 63)
    k128 = lax.broadcasted_iota(I32, (2 * SSM_STATE, 1), 0)
    spread = jnp.where(k128 == src, 1.0, 0.0).astype(BF16)
    diag = row_gi == col_gi

    def expand(c_ref):
        hi, lo = _split_bf16(c_ref[...])
        d = (jnp.dot(hi, spread, preferred_element_type=F32)
             + jnp.dot(lo, spread, preferred_element_type=F32))
        d = jnp.where(diag, d, 0.0)
        return d[:, :OCT_STATES], d[:, OCT_STATES:]

    br, bi = expand(bbc_ref)
    cr, ci = expand(ccc_ref)
    chi_r, clo_r = _split_bf16(cr)
    chi_i, clo_i = _split_bf16(ci)

    def lam(k):
        return pw_ref[k:k + 1, :OCT_STATES], pw_ref[k:k + 1, OCT_STATES:]

    def dot3(a, bhi, blo):
        ahi, alo = _split_bf16(a)
        return _dot_nt(ahi, bhi) + _dot_nt(alo, bhi) + _dot_nt(ahi, blo)

    lags = []
    for k in range(CHUNK):
        pr, pi_ = lam(k)
        fr, fi = _cmul(br, bi, pr, pi_)
        s = CHUNK - 1 - k
        f_ref[s * OCT:(s + 1) * OCT, :] = jnp.concatenate([fr, fi], axis=1).astype(BF16)
        lags.append((dot3(fr, chi_r, clo_r) - dot3(fi, chi_i, clo_i)).astype(BF16))
        pr, pi_ = lam(k + 1)
        gr, gi = _cmul(cr, ci, pr, pi_)
        gt_ref[k * OCT:(k + 1) * OCT, :] = jnp.concatenate([gr, -gi], axis=1).astype(BF16)
    zero = jnp.zeros((OCT, OCT), BF16)
    for s in range(CHUNK):
        for t in range(CHUNK):
            m_ref[s * OCT:(s + 1) * OCT, t * OCT:(t + 1) * OCT] = lags[t - s] if t >= s else zero


def _cmul(ar, ai, br, bi):
    return ar * br - ai * bi, ar * bi + ai * br


def _chunk_scan(sloc, tab_ref):
    R = sloc.shape[0]
    nb = R // SUBLANES
    sr, si = sloc[:, :OCT_STATES], sloc[:, OCT_STATES:]
    rowi = lax.broadcasted_iota(I32, (R, 1), 0)
    tr = jnp.where(rowi == 0, 0.0, pltpu.roll(sr, 1, axis=0))
    ti = jnp.where(rowi == 0, 0.0, pltpu.roll(si, 1, axis=0))
    for lvl, d in enumerate((1, 2, 4)):
        mr = tab_ref[lvl * SUBLANES:(lvl + 1) * SUBLANES, :OCT_STATES]
        mi = tab_ref[lvl * SUBLANES:(lvl + 1) * SUBLANES, OCT_STATES:]
        mr = jnp.concatenate([mr] * nb, axis=0)
        mi = jnp.concatenate([mi] * nb, axis=0)
        pr, pi_ = _cmul(mr, mi, pltpu.roll(tr, d, axis=0), pltpu.roll(ti, d, axis=0))
        tr, ti = tr + pr, ti + pi_
    pwr = tab_ref[3 * SUBLANES:4 * SUBLANES, :OCT_STATES]
    pwi = tab_ref[3 * SUBLANES:4 * SUBLANES, OCT_STATES:]
    cr = jnp.zeros((1, OCT_STATES), F32)
    ci = jnp.zeros((1, OCT_STATES), F32)
    out_r, out_i = [], []
    for k in range(nb):
        ar = tr[k * SUBLANES:(k + 1) * SUBLANES, :]
        ai = ti[k * SUBLANES:(k + 1) * SUBLANES, :]
        pr, pi_ = _cmul(pwr, pwi, jnp.broadcast_to(cr, ar.shape), jnp.broadcast_to(ci, ai.shape))
        hr, hi = ar + pr, ai + pi_
        out_r.append(hr)
        out_i.append(hi)
        cr, ci = hr[SUBLANES - 1:, :], hi[SUBLANES - 1:, :]
    hin = jnp.concatenate([jnp.concatenate(out_r, axis=0), jnp.concatenate(out_i, axis=0)], axis=1)
    lr, li = pwr[0:1, :], pwi[0:1, :]
    fr, fi = _cmul(lr, li, cr, ci)
    fin = jnp.concatenate([fr + sr[R - 1:, :], fi + si[R - 1:, :]], axis=1)
    return hin, fin


def _ssm_body(u_ref, h0r_ref, h0i_ref, bbc_ref, ccc_ref, pw_ref, tab_ref, d_ref,
              y_ref, hout_ref, hr_ref, hi_ref, f_ref, gt_ref, m_ref):
    b = pl.program_id(1)

    @pl.when(b == 0)
    def _():
        _build_weights(bbc_ref, ccc_ref, pw_ref, f_ref, gt_ref, m_ref)

    def outputs(xs, xb, hin_bf):
        y = (jnp.dot(xb, m_ref[...], preferred_element_type=F32) + _dot_nt(hin_bf, gt_ref[...]))
        return [_gelu_tanh(y[:, t * OCT:(t + 1) * OCT] + d_ref[...] * xs[t]) for t in range(CHUNK)]

    @pl.when(b < BATCH)
    def _():
        R = SEQ // CHUNK
        xs = [u_ref[pl.ds(s, R, stride=CHUNK), :] for s in range(CHUNK)]
        xb = jnp.concatenate(xs, axis=1).astype(BF16)
        sloc = jnp.dot(xb, f_ref[...], preferred_element_type=F32)
        hin, fin = _chunk_scan(sloc, tab_ref)
        for t, yt in enumerate(outputs(xs, xb, hin.astype(BF16))):
            y_ref[pl.ds(t, R, stride=CHUNK), :] = yt
        hout_ref[...] = fin

    @pl.when(b == BATCH)
    def _():
        B = DEC_BATCH
        xs = [u_ref[B * s:B * (s + 1), :] for s in range(CHUNK)]
        xb = jnp.concatenate(xs, axis=1).astype(BF16)
        sloc = jnp.dot(xb, f_ref[...], preferred_element_type=F32)
        h0r, h0i = h0r_ref[...], h0i_ref[...]
        hin = jnp.concatenate([h0r, h0i], axis=1).astype(BF16)
        for t, yt in enumerate(outputs(xs, xb, hin)):
            y_ref[B * t:B * (t + 1), :] = yt
        lr = tab_ref[3 * SUBLANES:3 * SUBLANES + 1, :OCT_STATES]
        li = tab_ref[3 * SUBLANES:3 * SUBLANES + 1, OCT_STATES:]
        nr, ni = _cmul(lr, li, h0r, h0i)
        hr_ref[...] = nr + sloc[:, :OCT_STATES]
        hi_ref[...] = ni + sloc[:, OCT_STATES:]


def _ssm(u, h0r, h0i, tables):
    col0 = POOL_WIDTH // OCT
    im3 = lambda o, b: (o, 0, 0)
    st_spec = pl.BlockSpec((DEC_BATCH, OCT_STATES), lambda o, b: (0, o))
    return pl.pallas_call(
        _ssm_body,
        grid=(N_OCT, BATCH + 1),
        in_specs=[
            pl.BlockSpec((SEQ, OCT), lambda o, b: (b, col0 + o)), st_spec, st_spec,
            pl.BlockSpec((None, OCT, 2 * SSM_STATE), im3),
            pl.BlockSpec((None, OCT, 2 * SSM_STATE), im3),
            pl.BlockSpec((None, 2 * SUBLANES, SW), im3),
            pl.BlockSpec((None, 4 * SUBLANES, SW), im3),
            pl.BlockSpec((None, 1, OCT), im3),
        ],
        out_specs=(
            pl.BlockSpec((SEQ, OCT), lambda o, b: (b, o)),
            pl.BlockSpec((None, 1, SW), lambda o, b: (jnp.minimum(b, BATCH - 1) * N_OCT + o, 0, 0)),
            st_spec, st_spec,
        ),
        out_shape=(
            jax.ShapeDtypeStruct((T_PAD, SSM_WIDTH), F32),
            jax.ShapeDtypeStruct((BATCH * N_OCT, 1, SW), F32),
            jax.ShapeDtypeStruct((DEC_BATCH, SSM_GROUPS * SSM_STATE), F32),
            jax.ShapeDtypeStruct((DEC_BATCH, SSM_GROUPS * SSM_STATE), F32),
        ),
        scratch_shapes=[pltpu.VMEM((CW, SW), BF16), pltpu.VMEM((CW, SW), BF16), pltpu.VMEM((CW, CW), BF16)],
        compiler_params=_cparams(2),
        name="ssm",
    )(u, h0r, h0i, *tables)


PM_TM = 256
PM_PROMPT_BLOCKS = T_PROMPT // PM_TM
PM_STEPS = T_ALL // PM_TM


def _route(logits, valid, cnt_ref):
    lane = lax.broadcasted_iota(I32, (PM_TM, ROUTE_LANES), 1)
    neg = jnp.float32(-jnp.inf)
    big = jnp.int32(1 << 20)
    is_g = lane < MOE_GROUPS
    gmax = jnp.max(jnp.where(is_g, logits, neg), axis=1, keepdims=True)
    g_idx = jnp.min(jnp.where(is_g & (logits == gmax), lane, big), axis=1, keepdims=True)
    g_den = jnp.sum(jnp.where(is_g, jnp.exp(logits - gmax), 0.0), axis=1, keepdims=True)
    g_val = 1.0 / g_den
    e_lane = lane - EXP_LANE0
    sel = (e_lane >= 0) & (e_lane < MOE_EXPERTS) & ((e_lane >> 3) == g_idx)
    m1 = jnp.max(jnp.where(sel, logits, neg), axis=1, keepdims=True)
    i1 = jnp.min(jnp.where(sel & (logits == m1), lane, big), axis=1, keepdims=True)
    sel2 = sel & (lane != i1)
    m2 = jnp.max(jnp.where(sel2, logits, neg), axis=1, keepdims=True)
    i2 = jnp.min(jnp.where(sel2 & (logits == m2), lane, big), axis=1, keepdims=True)
    e2 = jnp.exp(m2 - m1)
    w1 = g_val / (1.0 + e2)
    w2 = g_val * e2 / (1.0 + e2)
    oh1 = lane == i1
    oh2 = lane == i2
    oh = jnp.where(oh1 | oh2, valid, 0.0)
    rr = lax.broadcasted_iota(I32, (PM_TM, PM_TM), 0)
    cc = lax.broadcasted_iota(I32, (PM_TM, PM_TM), 1)
    tri = jnp.where(cc < rr, 1.0, 0.0).astype(BF16)
    base = cnt_ref[...] + jnp.dot(tri, oh.astype(BF16), preferred_element_type=F32)
    rank1 = jnp.sum(jnp.where(oh1, base, 0.0), axis=1, keepdims=True)
    rank2 = jnp.sum(jnp.where(oh2, base, 0.0), axis=1, keepdims=True)
    cnt_ref[...] = cnt_ref[...] + jnp.sum(oh, axis=0, keepdims=True)
    rt = jnp.where(lane == 0, w1, 0.0)
    rt = jnp.where(lane == 1, w2, rt)
    rt = jnp.where(lane == 2, rank1, rt)
    rt = jnp.where(lane == 3, rank2, rt)
    rt = jnp.where(lane == 4, (i1 - EXP_LANE0).astype(F32), rt)
    rt = jnp.where(lane == 5, (i2 - EXP_LANE0).astype(F32), rt)
    return rt


def _postmix_body(ya_ref, gp_ref, gs_ref, yp_ref, xp_ref, xs_ref, wa_ref, wb_ref, wo_ref,
                  gf_ref, wr_ref, br_ref, h_ref, tn_ref, rt_ref, rtt_ref, cnt_out_ref,
                  cnt_ref, lg_ref):
    i = pl.program_id(0)
    @pl.when(i == 0)
    def _():
        cnt_ref[...] = jnp.zeros_like(cnt_ref)
        lg_ref[...] = jnp.zeros_like(lg_ref)
    prev_logits = lg_ref[...]

    ya = ya_ref[...].astype(BF16)
    a = jnp.dot(ya, wa_ref[...], preferred_element_type=F32)
    bg = jnp.dot(ya, wb_ref[...], preferred_element_type=F32)
    y_ssm = a * _sigmoid(bg)
    merged = (gp_ref[...].astype(F32) * yp_ref[...].astype(F32)
              + gs_ref[...].astype(F32) * y_ssm)
    x = jnp.where(jnp.minimum(i, PM_STEPS - 1) < PM_PROMPT_BLOCKS, xp_ref[...], xs_ref[...])
    h = x + jnp.dot(merged.astype(BF16), wo_ref[...], preferred_element_type=F32)
    h_ref[...] = h
    inv = lax.rsqrt(jnp.mean(h * h, axis=-1, keepdims=True) + EPS)
    tn = (h * inv) * gf_ref[...]
    tn_ref[...] = _pack_pairs(tn)
    t_hi = tn.astype(BF16)
    t_lo = (tn - t_hi.astype(F32)).astype(BF16)
    hh = jnp.dot(t_hi, wr_ref[...], preferred_element_type=F32)
    lh = jnp.dot(t_lo, wr_ref[:, :ROUTE_LANES], preferred_element_type=F32)
    lg_ref[...] = (hh[:, :ROUTE_LANES] + lh + hh[:, ROUTE_LANES:]) + br_ref[...]
    rt = _route(prev_logits, jnp.where(i > 0, 1.0, 0.0), cnt_ref)
    rt_ref[...] = rt
    rtt_ref[...] = rt.T[:8, :]
    cnt_out_ref[...] = cnt_ref[...]


def _postmix(y_act, gates, y_pool, xp, xs, wa, wb, wo, g_ffn, wr_cat, b_r):
    npb = PM_PROMPT_BLOCKS
    const2 = lambda i: (0, 0)
    tile = lambda i: jnp.minimum(i, PM_STEPS - 1)
    return pl.pallas_call(
        _postmix_body,
        grid=(PM_STEPS + 1,),
        in_specs=[
            pl.BlockSpec((PM_TM, SSM_WIDTH), lambda i: (tile(i), 0)),
            pl.BlockSpec((PM_TM, D_MODEL), lambda i: (tile(i), 0)),
            pl.BlockSpec((PM_TM, D_MODEL), lambda i: (tile(i), 1)),
            pl.BlockSpec((PM_TM, D_MODEL), lambda i: (tile(i), 0)),
            pl.BlockSpec((PM_TM, D_MODEL), lambda i: (jnp.minimum(i, npb - 1), 0)),
            pl.BlockSpec((PM_TM, D_MODEL), lambda i: (jnp.maximum(tile(i) - npb, 0), 0)),
            pl.BlockSpec((SSM_WIDTH, D_MODEL), const2, pipeline_mode=pl.Buffered(1)),
            pl.BlockSpec((SSM_WIDTH, D_MODEL), const2, pipeline_mode=pl.Buffered(1)),
            pl.BlockSpec((D_MODEL, D_MODEL), const2, pipeline_mode=pl.Buffered(1)),
            pl.BlockSpec((1, D_MODEL), const2),
            pl.BlockSpec((D_MODEL, 2 * ROUTE_LANES), const2),
            pl.BlockSpec((1, ROUTE_LANES), const2),
        ],
        out_specs=(
            pl.BlockSpec((PM_TM, D_MODEL), lambda i: (tile(i), 0)),
            pl.BlockSpec((PM_TM, D_PACK), lambda i: (tile(i), 0)),
            pl.BlockSpec((PM_TM, ROUTE_LANES), lambda i: (jnp.maximum(i - 1, 0), 0)),
            pl.BlockSpec((8, PM_TM), lambda i: (0, jnp.maximum(i - 1, 0))),
            pl.BlockSpec((1, ROUTE_LANES), const2),
        ),
        out_shape=(
            jax.ShapeDtypeStruct((T_ALL, D_MODEL), F32),
            jax.ShapeDtypeStruct((T_ALL, D_PACK), U32),
            jax.ShapeDtypeStruct((T_ALL, ROUTE_LANES), F32),
            jax.ShapeDtypeStruct((8, T_ALL), F32),
            jax.ShapeDtypeStruct((1, ROUTE_LANES), F32),
        ),
        scratch_shapes=[pltpu.VMEM((1, ROUTE_LANES), F32), pltpu.VMEM((PM_TM, ROUTE_LANES), F32)],
        compiler_params=_cparams(1),
        name="postmix",
    )(y_act, gates, gates, y_pool, xp, xs, wa, wb, wo, g_ffn, wr_cat, b_r)


SC_CH = 96


def _sc_workers():
    info = plsc.get_sparse_core_info()
    return info.num_cores, info.num_cores * info.num_subcores


def _sc_dispatch(tn, slots):
    n_cores, n_workers = _sc_workers()
    per_w = (T_ALL // SC_CH) // n_workers
    assert per_w * n_workers * SC_CH == T_ALL
    slots = slots.reshape(2, n_workers, per_w, SC_CH)

    @functools.partial(
        pl.kernel,
        mesh=plsc.VectorSubcoreMesh(core_axis_name="c", subcore_axis_name="s"),
        out_type=jax.ShapeDtypeStruct((N_SLOTS, D_PACK), U32),
        scratch_types=[pltpu.VMEM((2, per_w, SC_CH), I32), pltpu.VMEM((SC_CH, D_PACK), U32)],
    )
    def k(tn_hbm, slots_hbm, xs_hbm, idx_v, rows_v):
        wid = lax.axis_index("s") * n_cores + lax.axis_index("c")
        c0 = wid * per_w
        pltpu.sync_copy(slots_hbm.at[0, wid], idx_v.at[0])
        pltpu.sync_copy(slots_hbm.at[1, wid], idx_v.at[1])

        @pl.loop(0, per_w)
        def _(c):
            row0 = pl.multiple_of((c0 + c) * SC_CH, SC_CH)
            pltpu.sync_copy(tn_hbm.at[pl.ds(row0, SC_CH)], rows_v)
            pltpu.sync_copy(rows_v, xs_hbm.at[idx_v.at[0, c]])
            pltpu.sync_copy(rows_v, xs_hbm.at[idx_v.at[1, c]])

    return k(tn, slots)


def _sc_collect(ys, slots):
    n_cores, n_workers = _sc_workers()
    per_w = (N_ASSIGN // SC_CH) // n_workers
    assert per_w * n_workers * SC_CH == N_ASSIGN
    slots = slots.reshape(n_workers, per_w, SC_CH)

    @functools.partial(
        pl.kernel,
        mesh=plsc.VectorSubcoreMesh(core_axis_name="c", subcore_axis_name="s"),
        out_type=jax.ShapeDtypeStruct((N_ASSIGN, D_PACK), U32),
        scratch_types=[pltpu.VMEM((per_w, SC_CH), I32), pltpu.VMEM((SC_CH, D_PACK), U32)],
    )
    def k(ys_hbm, slots_hbm, out_hbm, idx_v, rows_v):
        wid = lax.axis_index("s") * n_cores + lax.axis_index("c")
        c0 = wid * per_w
        pltpu.sync_copy(slots_hbm.at[wid], idx_v)

        @pl.loop(0, per_w)
        def _(c):
            row0 = pl.multiple_of((c0 + c) * SC_CH, SC_CH)
            pltpu.sync_copy(ys_hbm.at[idx_v.at[c]], rows_v)
            pltpu.sync_copy(rows_v, out_hbm.at[pl.ds(row0, SC_CH)])

    return k(ys, slots)


W_PARTS = 2


def _expert_body(t0_ref, nt_ref, xs_hbm, wg_hbm, wu_hbm, wd_hbm, ys_hbm,
                 wg_ref, wu_ref, wd_ref, xb_ref, yb_ref, wgb_ref, wub_ref, wdb_ref, wsem, xsem, ysem):
    e = pl.program_id(0)
    n = nt_ref[e]
    g0 = t0_ref[e]
    ws = e & 1

    def w_copies(ex, slot):
        out = []
        for hbm, buf in ((wg_hbm, wg_ref), (wu_hbm, wu_ref), (wd_hbm, wd_ref)):
            rb = buf.shape[1] // W_PARTS
            for p in range(W_PARTS):
                out.append((pltpu.make_async_copy(hbm.at[ex, pl.ds(p * rb, rb)],
                                                  buf.at[slot, pl.ds(p * rb, rb)], wsem.at[slot]), p))
        return out

    @pl.when(e == 0)
    def _():
        for cp, p in w_copies(0, 0):
            cp.start(priority=p)

    @pl.when(e + 1 < MOE_EXPERTS)
    def _():
        for cp, p in w_copies(e + 1, 1 - ws):
            cp.start(priority=p)

    for cp, _ in w_copies(e, ws):
        cp.wait()

    def rows(j):
        return pl.ds(pl.multiple_of((g0 + j) * TME, TME), TME)

    def x_copy(j, s):
        return pltpu.make_async_copy(xs_hbm.at[rows(j)], xb_ref.at[s], xsem.at[s])

    def y_copy(j, s):
        return pltpu.make_async_copy(yb_ref.at[s], ys_hbm.at[rows(j)], ysem.at[s])

    @pl.when(n > 0)
    def _():
        x_copy(0, 0).start()
        wgb_ref[...] = wg_ref[ws].astype(BF16)
        wub_ref[...] = wu_ref[ws].astype(BF16)
        wdb_ref[...] = wd_ref[ws].astype(BF16)

        def tile(j, c):
            s = j & 1
            x_copy(j, s).wait()
            @pl.when(j + 1 < n)
            def _():
                x_copy(j + 1, 1 - s).start()
            @pl.when(j >= 2)
            def _():
                y_copy(j - 2, s).wait()
            x = _unpack_pairs(xb_ref[s], BF16)
            hg = jnp.dot(x, wgb_ref[...], preferred_element_type=F32)
            hu = jnp.dot(x, wub_ref[...], preferred_element_type=F32)
            act = (hg * _sigmoid(hg)) * hu
            yb_ref[s] = _pack_pairs(jnp.dot(act.astype(BF16), wdb_ref[...], preferred_element_type=F32))
            y_copy(j, s).start()
            return c
        lax.fori_loop(0, n, tile, 0)

        @pl.when(n >= 2)
        def _():
            y_copy(n - 2, n & 1).wait()
        y_copy(n - 1, (n - 1) & 1).wait()


def _experts(tile0, tiles, xs, w_eg, w_eu, w_ed):
    any_spec = pl.BlockSpec(memory_space=pl.ANY)
    grid_spec = pltpu.PrefetchScalarGridSpec(
        num_scalar_prefetch=2,
        grid=(MOE_EXPERTS,),
        in_specs=[any_spec] * 4,
        out_specs=any_spec,
        scratch_shapes=[
            pltpu.VMEM((2, D_MODEL, MOE_FF), F32),
            pltpu.VMEM((2, D_MODEL, MOE_FF), F32),
            pltpu.VMEM((2, MOE_FF, D_MODEL), F32),
            pltpu.VMEM((2, TME, D_PACK), U32),
            pltpu.VMEM((2, TME, D_PACK), U32),
            pltpu.VMEM((D_MODEL, MOE_FF), BF16),
            pltpu.VMEM((D_MODEL, MOE_FF), BF16),
            pltpu.VMEM((MOE_FF, D_MODEL), BF16),
            pltpu.SemaphoreType.DMA((2,)),
            pltpu.SemaphoreType.DMA((2,)),
            pltpu.SemaphoreType.DMA((2,)),
        ],
    )
    return pl.pallas_call(
        _expert_body,
        grid_spec=grid_spec,
        out_shape=jax.ShapeDtypeStruct((N_SLOTS, D_PACK), U32),
        compiler_params=_cparams(1),
        name="experts",
    )(tile0, tiles, xs, w_eg, w_eu, w_ed)


FN_TM = 512
FN_PROMPT_BLOCKS = T_PROMPT // FN_TM


def _final_body(h_ref, y0_ref, y1_ref, rt_ref, g_ref, op_ref, os_ref):
    i = pl.program_id(0)
    rt = rt_ref[...]
    z = (h_ref[...] + rt[:, 0:1] * _unpack_pairs(y0_ref[...], F32)
         + rt[:, 1:2] * _unpack_pairs(y1_ref[...], F32))
    inv = lax.rsqrt(jnp.mean(z * z, axis=-1, keepdims=True) + EPS)
    out = (z * inv) * g_ref[...]
    @pl.when(i < FN_PROMPT_BLOCKS)
    def _():
        op_ref[...] = out
    @pl.when(i >= FN_PROMPT_BLOCKS)
    def _():
        os_ref[...] = out


def _final(h, y, route, g_final):
    n = T_ALL // FN_TM
    npb = FN_PROMPT_BLOCKS
    yoff = T_ALL // FN_TM
    return pl.pallas_call(
        _final_body,
        grid=(n,),
        in_specs=[
            pl.BlockSpec((FN_TM, D_MODEL), lambda i: (i, 0)),
            pl.BlockSpec((FN_TM, D_PACK), lambda i: (i, 0)),
            pl.BlockSpec((FN_TM, D_PACK), lambda i: (yoff + i, 0)),
            pl.BlockSpec((FN_TM, ROUTE_LANES), lambda i: (i, 0)),
            pl.BlockSpec((1, D_MODEL), lambda i: (0, 0)),
        ],
        out_specs=(
            pl.BlockSpec((FN_TM, D_MODEL), lambda i: (jnp.minimum(i, npb - 1), 0)),
            pl.BlockSpec((FN_TM, D_MODEL), lambda i: (jnp.maximum(i - npb, 0), 0)),
        ),
        out_shape=(
            jax.ShapeDtypeStruct((T_PROMPT, D_MODEL), F32),
            jax.ShapeDtypeStruct((T_SAMPLE, D_MODEL), F32),
        ),
        compiler_params=_cparams(1),
        name="final",
    )(h, y, y, route, g_final)


def _dispatch_plan(route_t, cnt):
    counts = cnt[0, EXP_LANE0:EXP_LANE0 + MOE_EXPERTS].astype(I32)
    tiles = (counts + (TME - 1)) // TME
    cumt = jnp.cumsum(tiles)
    pad_off = (cumt - tiles) * TME
    rank = route_t[2:4].astype(I32)
    eid = route_t[4:6].astype(I32)
    onehot = eid[None] == jnp.arange(MOE_EXPERTS, dtype=I32)[:, None, None]
    slots = rank + jnp.sum(jnp.where(onehot, pad_off[:, None, None], 0), axis=0)
    return slots, cumt - tiles, tiles


def kernel(x_prompt, x_sample, state_pool, state_ssm_re, state_ssm_im, g_mix, w_in, w_pool,
           pool_scale, ssm_a_re, ssm_a_im, ssm_log_dt, ssm_b_re, ssm_b_im, ssm_c_re, ssm_c_im,
           ssm_d, w_glu_a, w_glu_b, w_out, g_ffn, w_router_group, b_router_group,
           w_router_expert, b_router_expert, w_exp_gate, w_exp_up, w_exp_down, g_final):
    l = 0
    xp = x_prompt.reshape(T_PROMPT, D_MODEL)
    xs = x_sample.transpose(1, 0, 2).reshape(T_SAMPLE, D_MODEL)
    w_pool_bf = w_pool[l].astype(BF16)
    g_mix2 = g_mix[l].reshape(1, D_MODEL)
    scale2 = pool_scale[l].reshape(1, D_MODEL)

    n_pt = T_PROMPT // IN_TM
    u, gates, w_in_bf = _inproj(xp, g_mix2, w_in[l], 0, n_i=1, emit_w=True)
    u, gates, wa_bf, wb_bf, wo_bf = _inproj(xp, g_mix2, w_in_bf, 1, dst=(u, gates), x_block0=1, n_i=n_pt - 1,
                                            cast=(w_glu_a[l], w_glu_b[l], w_out[l]))
    u, gates = _inproj(xs, g_mix2, w_in_bf, n_pt, dst=(u, gates))

    y_pool, pool_tail = _pool_prompt(u, w_pool_bf, scale2)
    hist_tm = state_pool[l].transpose(1, 0, 2)
    y_pool, pool_buf_tm = _pool_sample(u, hist_tm, w_pool_bf, scale2, y_pool)

    tables = _ssm_tables(ssm_a_re[l], ssm_a_im[l], ssm_log_dt[l], ssm_b_re[l], ssm_b_im[l],
                         ssm_c_re[l], ssm_c_im[l], ssm_d[l])
    h0r = state_ssm_re[l].reshape(DEC_BATCH, SSM_GROUPS * SSM_STATE)
    h0i = state_ssm_im[l].reshape(DEC_BATCH, SSM_GROUPS * SSM_STATE)
    y_act, h_prompt, hs_re, hs_im = _ssm(u, h0r, h0i, tables)

    assert EXP_LANE0 == MOE_GROUPS
    w_r = jnp.concatenate([w_router_group[l], w_router_expert[l],
                           jnp.zeros((D_MODEL, ROUTE_LANES - EXP_LANE0 - MOE_EXPERTS), F32)], axis=1)
    wr_hi = w_r.astype(BF16)
    wr_cat = jnp.concatenate([wr_hi, (w_r - wr_hi.astype(F32)).astype(BF16)], axis=1)
    b_r = jnp.zeros((1, ROUTE_LANES), F32)
    b_r = b_r.at[0, :MOE_GROUPS].set(b_router_group[l])
    b_r = b_r.at[0, EXP_LANE0:EXP_LANE0 + MOE_EXPERTS].set(b_router_expert[l])

    h, tn, route, route_t, cnt = _postmix(y_act, gates, y_pool, xp, xs, wa_bf, wb_bf, wo_bf,
                                          g_ffn[l].reshape(1, D_MODEL), wr_cat, b_r)
    slots, tile0, tiles = _dispatch_plan(route_t, cnt)
    xs_sorted = _sc_dispatch(tn, slots)
    ys_sorted = _experts(tile0, tiles, xs_sorted, w_exp_gate[l], w_exp_up[l], w_exp_down[l])
    y = _sc_collect(ys_sorted, slots)
    yp, ys = _final(h, y, route, g_final.reshape(1, D_MODEL))

    y_prompt = yp.reshape(BATCH, SEQ, D_MODEL)
    y_sample = ys.reshape(DEC_SEQ, DEC_BATCH, D_MODEL).transpose(1, 0, 2)
    new_pool_prompt = pool_tail[:, HIST - POOL_BUF:, :][None]
    new_pool_sample = pool_buf_tm.transpose(1, 0, 2)[None]
    hp = h_prompt.reshape(BATCH, N_OCT, 2, OCT_GROUPS, SSM_STATE).transpose(2, 0, 1, 3, 4)
    hp = hp.reshape(2, BATCH, SSM_GROUPS, SSM_STATE)
    shp = (1, DEC_BATCH, SSM_GROUPS, SSM_STATE)
    return (y_prompt, y_sample, new_pool_prompt, hp[0][None], hp[1][None], new_pool_sample,
            hs_re.reshape(shp), hs_im.reshape(shp))
```

```python
import functools
import math

import jax
import jax.numpy as jnp
from jax import lax
from jax.experimental import pallas as pl
from jax.experimental.pallas import tpu as pltpu
from jax.experimental.pallas import tpu_sc as plsc

F32 = jnp.float32
BF16 = jnp.bfloat16
I32 = jnp.int32
U32 = jnp.uint32

D_MODEL = 2048
BATCH = 4
SEQ = 2048
DEC_BATCH = 128
DEC_SEQ = 8
PAST_LEN = 16384
POOL_WIDTH = D_MODEL // 2
POOL_WINDOWS = (2, 4, 8, 16)
POOL_GROUPS = len(POOL_WINDOWS)
POOL_GROUP_CH = POOL_WIDTH // POOL_GROUPS
POOL_OUT_CH = D_MODEL // POOL_GROUPS
POOL_BUF = max(POOL_WINDOWS) - 1
SSM_WIDTH = D_MODEL // 2
SSM_GROUP_CH = 16
SSM_GROUPS = SSM_WIDTH // SSM_GROUP_CH
SSM_STATE = 64
IN_WIDTH = POOL_WIDTH + SSM_WIDTH + 2 * D_MODEL
D_PACK = D_MODEL // 2
MOE_GROUPS = 4
MOE_EPG = 8
MOE_EXPERTS = MOE_GROUPS * MOE_EPG
MOE_FF = D_MODEL // 4
EPS = 1e-6

T_PROMPT = BATCH * SEQ
T_SAMPLE = DEC_BATCH * DEC_SEQ
T_ALL = T_PROMPT + T_SAMPLE
T_PAD = (BATCH + 1) * SEQ

LANES = 128
SUBLANES = 8
VMEM_LIMIT = 56 * 1024 * 1024

CHUNK = 8
OCT = LANES
N_OCT = SSM_WIDTH // OCT
OCT_GROUPS = OCT // SSM_GROUP_CH
OCT_STATES = OCT_GROUPS * SSM_STATE
CW = CHUNK * OCT
SW = 2 * OCT_STATES

ROUTE_LANES = LANES
EXP_LANE0 = MOE_GROUPS
N_ASSIGN = 2 * T_ALL
TME = 256
N_ITEMS_MAX = N_ASSIGN // TME + MOE_EXPERTS
N_SLOTS = N_ITEMS_MAX * TME


def _cparams(n_axes):
    return pltpu.CompilerParams(dimension_semantics=("arbitrary",) * n_axes,
                                vmem_limit_bytes=VMEM_LIMIT)


def _sigmoid(x):
    return 1.0 / (1.0 + jnp.exp(-x))


def _pack_pairs(x):
    c = x.shape[1] // 2
    hi = lax.bitcast_convert_type(x[:, :c].astype(BF16).astype(F32), U32)
    lo = lax.bitcast_convert_type(x[:, c:].astype(BF16).astype(F32), U32)
    return hi | (lo >> 16)


def _unpack_pairs(u, dtype):
    hi = lax.bitcast_convert_type(u & jnp.uint32(0xFFFF0000), F32)
    lo = lax.bitcast_convert_type(u << 16, F32)
    return jnp.concatenate([hi, lo], axis=1).astype(dtype)


def _gelu_tanh(x):
    c = math.sqrt(2.0 / math.pi)
    return 0.5 * x * (1.0 + jnp.tanh(c * (x + 0.044715 * (x * x * x))))


IN_TM = 1024
IN_TN = 1024
U_WIDTH = POOL_WIDTH + SSM_WIDTH
GATE_WIDTH = 2 * D_MODEL
IN_U_STEPS = U_WIDTH // IN_TN


CAST_RB = 128
IN_RC = 128


def _inproj_body(n_alias, cast_ranges, emit_w, x_ref, g_ref, w_ref, *rest):
    n_cast = len(cast_ranges)
    cast_in = rest[n_alias:n_alias + n_cast]
    u_ref, gate_ref = rest[n_alias + n_cast:n_alias + n_cast + 2]
    cast_out = rest[n_alias + n_cast + 2:n_alias + 2 * n_cast + 2]
    xn_ref = rest[-1]
    j = pl.program_id(1)

    def weights():
        if not emit_w:
            return w_ref
        wbf_ref = rest[-2]
        wbf_ref[...] = w_ref[...].astype(BF16)
        return wbf_ref

    step = pl.program_id(0) * pl.num_programs(1) + j
    for (lo, hi), ci, co in zip(cast_ranges, cast_in, cast_out):
        @pl.when((step >= lo) & (step < hi))
        def _():
            co[...] = ci[...].astype(BF16)
    chunks = [slice(r * IN_RC, (r + 1) * IN_RC) for r in range(IN_TM // IN_RC)]
    @pl.when(j == 0)
    def _():
        w = weights()
        for rows in chunks:
            x = x_ref[rows, :]
            inv = lax.rsqrt(jnp.mean(x * x, axis=-1, keepdims=True) + EPS)
            xn = ((x * inv) * g_ref[...]).astype(BF16)
            xn_ref[rows, :] = xn
            u_ref[rows, :] = jnp.dot(xn, w[...], preferred_element_type=F32)
    @pl.when((j > 0) & (j < IN_U_STEPS))
    def _():
        u_ref[...] = jnp.dot(xn_ref[...], weights()[...], preferred_element_type=F32)
    @pl.when(j >= IN_U_STEPS)
    def _():
        w = weights()
        for rows in chunks:
            acc = jnp.dot(xn_ref[rows, :], w[...], preferred_element_type=F32)
            gate_ref[rows, :] = _sigmoid(acc).astype(BF16)


def _inproj(x, g, w, row_block0, dst=None, cast=(), x_block0=0, n_i=None, emit_w=False):
    if n_i is None:
        n_i = x.shape[0] // IN_TM
    assert not emit_w or n_i == 1
    n_j = IN_WIDTH // IN_TN
    x_mode = dict(pipeline_mode=pl.Buffered(1)) if emit_w else {}
    in_specs = [
        pl.BlockSpec((IN_TM, D_MODEL), lambda i, j: (i + x_block0, 0), **x_mode),
        pl.BlockSpec((1, D_MODEL), lambda i, j: (0, 0)),
        pl.BlockSpec((D_MODEL, IN_TN), lambda i, j: (0, j)),
    ]
    args = [x, g, w]
    aliases = {}
    if dst is not None:
        in_specs += [pl.BlockSpec(memory_space=pl.ANY)] * 2
        args += list(dst)
        aliases = {3: 0, 4: 1}
    cast_specs, cast_shapes, cast_ranges, off = [], [], [], 0
    for wc in cast:
        nb = wc.shape[0] // CAST_RB
        cast_specs.append(pl.BlockSpec(
            (CAST_RB, wc.shape[1]), lambda i, j, o=off, nb=nb: (jnp.clip(i * n_j + j - o, 0, nb - 1), 0)))
        cast_shapes.append(jax.ShapeDtypeStruct(wc.shape, BF16))
        cast_ranges.append((off, off + nb))
        off += nb
    assert off <= n_i * n_j
    args += list(cast)
    w_specs, w_shapes = [], []
    if emit_w:
        w_specs = [pl.BlockSpec((D_MODEL, IN_TN), lambda i, j: (0, j))]
        w_shapes = [jax.ShapeDtypeStruct((D_MODEL, IN_WIDTH), BF16)]
    return pl.pallas_call(
        functools.partial(_inproj_body, len(aliases), tuple(cast_ranges), emit_w),
        grid=(n_i, n_j),
        in_specs=in_specs + cast_specs,
        out_specs=(
            pl.BlockSpec((IN_TM, IN_TN), lambda i, j: (i + row_block0, jnp.minimum(j, IN_U_STEPS - 1))),
            pl.BlockSpec((IN_TM, IN_TN), lambda i, j: (i + row_block0, jnp.maximum(j - IN_U_STEPS, 0))),
            *cast_specs, *w_specs,
        ),
        out_shape=(
            jax.ShapeDtypeStruct((T_PAD, U_WIDTH), F32),
            jax.ShapeDtypeStruct((T_ALL, GATE_WIDTH), BF16),
            *cast_shapes, *w_shapes,
        ),
        scratch_shapes=[pltpu.VMEM((IN_TM, D_MODEL), BF16)],
        input_output_aliases=aliases,
        compiler_params=_cparams(2),
        name="inproj",
    )(*args)


PP_TM = 1024
HIST = 16


def _pool_project(pooled_g, g, w_ref, sc_ref, o_ref):
    y = jnp.dot(pooled_g.astype(BF16), w_ref[g], preferred_element_type=F32)
    lo, hi = g * POOL_OUT_CH, (g + 1) * POOL_OUT_CH
    o_ref[:, lo:hi] = (y * sc_ref[:, lo:hi]).astype(o_ref.dtype)


def _pool_prompt_body(u_ref, w_ref, sc_ref, o_ref, tail_ref, hist_ref):
    i = pl.program_id(1)
    @pl.when(i == 0)
    def _():
        hist_ref[...] = jnp.zeros_like(hist_ref)
    u = u_ref[...]
    ext = jnp.concatenate([hist_ref[...], u], axis=0)
    hist_ref[...] = u[PP_TM - HIST:, :]
    tail_ref[...] = u[PP_TM - HIST:, :]
    pos = i * PP_TM + lax.broadcasted_iota(I32, (PP_TM, 1), 0)
    for g, w in enumerate(POOL_WINDOWS):
        lo, hi = g * POOL_GROUP_CH, (g + 1) * POOL_GROUP_CH
        s = ext[:, lo:hi]
        d = 1
        while d < w:
            s = s + pltpu.roll(s, d, axis=0)
            d *= 2
        cnt = jnp.minimum(w, pos + 1).astype(F32)
        pooled = s[HIST:, :] / cnt - u[:, lo:hi]
        _pool_project(pooled, g, w_ref, sc_ref, o_ref)


def _pool_prompt(u, w_pool_bf, pool_scale):
    n_i = SEQ // PP_TM
    return pl.pallas_call(
        _pool_prompt_body,
        grid=(BATCH, n_i),
        in_specs=[
            pl.BlockSpec((PP_TM, POOL_WIDTH), lambda b, i: (b * n_i + i, 0)),
            pl.BlockSpec((POOL_GROUPS, POOL_GROUP_CH, POOL_OUT_CH), lambda b, i: (0, 0, 0)),
            pl.BlockSpec((1, D_MODEL), lambda b, i: (0, 0)),
        ],
        out_specs=(
            pl.BlockSpec((PP_TM, D_MODEL), lambda b, i: (b * n_i + i, 0)),
            pl.BlockSpec((None, HIST, POOL_WIDTH), lambda b, i: (b, 0, 0)),
        ),
        out_shape=(
            jax.ShapeDtypeStruct((T_ALL, D_MODEL), BF16),
            jax.ShapeDtypeStruct((BATCH, HIST, POOL_WIDTH), F32),
        ),
        scratch_shapes=[pltpu.VMEM((HIST, POOL_WIDTH), F32)],
        compiler_params=_cparams(2),
        name="pool_prompt",
    )(u, w_pool_bf, pool_scale)


def _pool_sample_body(u_ref, hist_ref, w_ref, sc_ref, _dst, o_ref, buf_ref):
    rows = [hist_ref[k] for k in range(POOL_BUF)]
    rows += [u_ref[DEC_BATCH * t:DEC_BATCH * (t + 1), :] for t in range(DEC_SEQ)]
    n = len(rows)
    for k in range(POOL_BUF):
        buf_ref[k] = rows[n - POOL_BUF + k]
    for g, w in enumerate(POOL_WINDOWS):
        lo, hi = g * POOL_GROUP_CH, (g + 1) * POOL_GROUP_CH
        f = [r[:, lo:hi] for r in rows]
        cur = f
        d = 1
        while d < w:
            cur = [cur[k] + cur[k - d] if k - d >= 0 else cur[k] for k in range(n)]
            d *= 2
        pooled = jnp.concatenate(
            [cur[POOL_BUF + t] / float(w) - f[POOL_BUF + t] for t in range(DEC_SEQ)], axis=0)
        _pool_project(pooled, g, w_ref, sc_ref, o_ref)


def _pool_sample(u, hist_tm, w_pool_bf, pool_scale, y_pool):
    blk = T_PROMPT // T_SAMPLE
    return pl.pallas_call(
        _pool_sample_body,
        grid=(1,),
        in_specs=[
            pl.BlockSpec((T_SAMPLE, POOL_WIDTH), lambda i: (blk, 0)),
            pl.BlockSpec((POOL_BUF, DEC_BATCH, POOL_WIDTH), lambda i: (0, 0, 0)),
            pl.BlockSpec((POOL_GROUPS, POOL_GROUP_CH, POOL_OUT_CH), lambda i: (0, 0, 0)),
            pl.BlockSpec((1, D_MODEL), lambda i: (0, 0)),
            pl.BlockSpec(memory_space=pl.ANY),
        ],
        out_specs=(
            pl.BlockSpec((T_SAMPLE, D_MODEL), lambda i: (blk, 0)),
            pl.BlockSpec((POOL_BUF, DEC_BATCH, POOL_WIDTH), lambda i: (0, 0, 0)),
        ),
        out_shape=(
            jax.ShapeDtypeStruct((T_ALL, D_MODEL), BF16),
            jax.ShapeDtypeStruct((POOL_BUF, DEC_BATCH, POOL_WIDTH), F32),
        ),
        input_output_aliases={4: 0},
        compiler_params=_cparams(1),
        name="pool_sample",
    )(u, hist_tm, w_pool_bf, pool_scale, y_pool)


def _ssm_tables(a_re, a_im, log_dt, b_re, b_im, c_re, c_im, d_skip):
    dt = jnp.exp(log_dt)[:, None]
    lr, li = a_re, a_im
    ab_re = jnp.exp(lr * dt) * jnp.cos(li * dt)
    ab_im = jnp.exp(lr * dt) * jnp.sin(li * dt)
    den = lr * lr + li * li
    nr, ni = ab_re - 1.0, ab_im
    q_re = (nr * lr + ni * li) / den
    q_im = (ni * lr - nr * li) / den
    bb_re = q_re[..., None] * b_re - q_im[..., None] * b_im
    bb_im = q_re[..., None] * b_im + q_im[..., None] * b_re

    def lam_rows(ks):
        k = jnp.asarray(ks, F32)[:, None, None]
        m = jnp.exp(k * lr * dt)
        re = (m * jnp.cos(k * li * dt)).reshape(len(ks), N_OCT, OCT_STATES)
        im = (m * jnp.sin(k * li * dt)).reshape(len(ks), N_OCT, OCT_STATES)
        return jnp.concatenate([re, im], axis=-1).transpose(1, 0, 2)

    def compact(re, im):
        v = jnp.concatenate([re, im], axis=-1)
        return v.reshape(N_OCT, OCT, 2 * SSM_STATE)

    bbc = compact(jnp.swapaxes(bb_re, 1, 2), jnp.swapaxes(bb_im, 1, 2))
    ccc = compact(c_re, c_im)
    pw = lam_rows(list(range(2 * SUBLANES)))
    r = jnp.arange(SUBLANES)[None, :, None]
    parts = [jnp.where(r >= dd, lam_rows([CHUNK * dd]), 0.0) for dd in (1, 2, 4)]
    parts.append(lam_rows([CHUNK * kk for kk in range(1, SUBLANES + 1)]))
    tab = jnp.concatenate(parts, axis=1)
    dsk = d_skip.reshape(N_OCT, 1, OCT)
    return bbc, ccc, pw, tab, dsk


def _split_bf16(x):
    hi = x.astype(BF16)
    return hi, (x - hi.astype(F32)).astype(BF16)


def _dot_nt(a, b):
    return lax.dot_general(a, b, (((1,), (1,)), ((), ())), preferred_element_type=F32)


def _build_weights(bbc_ref, ccc_ref, pw_ref, f_ref, gt_ref, m_ref):
    row_gi = lax.broadcasted_iota(I32, (OCT, 1), 0) >> 4
    col = lax.broadcasted_iota(I32, (1, SW), 1)
    col_gi = (col >> 6) & 7
    src = ((col >> 9) << 6) | (col & 63)
    k128 = lax.broadcasted_iota(I32, (2 * SSM_STATE, 1), 0)
    spread = jnp.where(k128 == src, 1.0, 0.0).astype(BF16)
    diag = row_gi == col_gi

    def expand(c_ref):
        hi, lo = _split_bf16(c_ref[...])
        d = (jnp.dot(hi, spread, preferred_element_type=F32)
             + jnp.dot(lo, spread, preferred_element_type=F32))
        d = jnp.where(diag, d, 0.0)
        return d[:, :OCT_STATES], d[:, OCT_STATES:]

    br, bi = expand(bbc_ref)
    cr, ci = expand(ccc_ref)
    chi_r, clo_r = _split_bf16(cr)
    chi_i, clo_i = _split_bf16(ci)

    def lam(k):
        return pw_ref[k:k + 1, :OCT_STATES], pw_ref[k:k + 1, OCT_STATES:]

    def dot3(a, bhi, blo):
        ahi, alo = _split_bf16(a)
        return _dot_nt(ahi, bhi) + _dot_nt(alo, bhi) + _dot_nt(ahi, blo)

    lags = []
    for k in range(CHUNK):
        pr, pi_ = lam(k)
        fr, fi = _cmul(br, bi, pr, pi_)
        s = CHUNK - 1 - k
        f_ref[s * OCT:(s + 1) * OCT, :] = jnp.concatenate([fr, fi], axis=1).astype(BF16)
        lags.append((dot3(fr, chi_r, clo_r) - dot3(fi, chi_i, clo_i)).astype(BF16))
        pr, pi_ = lam(k + 1)
        gr, gi = _cmul(cr, ci, pr, pi_)
        gt_ref[k * OCT:(k + 1) * OCT, :] = jnp.concatenate([gr, -gi], axis=1).astype(BF16)
    zero = jnp.zeros((OCT, OCT), BF16)
    for s in range(CHUNK):
        for t in range(CHUNK):
            m_ref[s * OCT:(s + 1) * OCT, t * OCT:(t + 1) * OCT] = lags[t - s] if t >= s else zero


def _cmul(ar, ai, br, bi):
    return ar * br - ai * bi, ar * bi + ai * br


def _chunk_scan(sloc, tab_ref):
    R = sloc.shape[0]
    nb = R // SUBLANES
    sr, si = sloc[:, :OCT_STATES], sloc[:, OCT_STATES:]
    rowi = lax.broadcasted_iota(I32, (R, 1), 0)
    tr = jnp.where(rowi == 0, 0.0, pltpu.roll(sr, 1, axis=0))
    ti = jnp.where(rowi == 0, 0.0, pltpu.roll(si, 1, axis=0))
    for lvl, d in enumerate((1, 2, 4)):
        mr = tab_ref[lvl * SUBLANES:(lvl + 1) * SUBLANES, :OCT_STATES]
        mi = tab_ref[lvl * SUBLANES:(lvl + 1) * SUBLANES, OCT_STATES:]
        mr = jnp.concatenate([mr] * nb, axis=0)
        mi = jnp.concatenate([mi] * nb, axis=0)
        pr, pi_ = _cmul(mr, mi, pltpu.roll(tr, d, axis=0), pltpu.roll(ti, d, axis=0))
        tr, ti = tr + pr, ti + pi_
    pwr = tab_ref[3 * SUBLANES:4 * SUBLANES, :OCT_STATES]
    pwi = tab_ref[3 * SUBLANES:4 * SUBLANES, OCT_STATES:]
    cr = jnp.zeros((1, OCT_STATES), F32)
    ci = jnp.zeros((1, OCT_STATES), F32)
    out_r, out_i = [], []
    for k in range(nb):
        ar = tr[k * SUBLANES:(k + 1) * SUBLANES, :]
        ai = ti[k * SUBLANES:(k + 1) * SUBLANES, :]
        pr, pi_ = _cmul(pwr, pwi, jnp.broadcast_to(cr, ar.shape), jnp.broadcast_to(ci, ai.shape))
        hr, hi = ar + pr, ai + pi_
        out_r.append(hr)
        out_i.append(hi)
        cr, ci = hr[SUBLANES - 1:, :], hi[SUBLANES - 1:, :]
    hin = jnp.concatenate([jnp.concatenate(out_r, axis=0), jnp.concatenate(out_i, axis=0)], axis=1)
    lr, li = pwr[0:1, :], pwi[0:1, :]
    fr, fi = _cmul(lr, li, cr, ci)
    fin = jnp.concatenate([fr + sr[R - 1:, :], fi + si[R - 1:, :]], axis=1)
    return hin, fin


def _ssm_body(u_ref, h0r_ref, h0i_ref, bbc_ref, ccc_ref, pw_ref, tab_ref, d_ref,
              y_ref, hout_ref, hr_ref, hi_ref, f_ref, gt_ref, m_ref):
    b = pl.program_id(1)

    @pl.when(b == 0)
    def _():
        _build_weights(bbc_ref, ccc_ref, pw_ref, f_ref, gt_ref, m_ref)

    def outputs(xs, xb, hin_bf):
        y = (jnp.dot(xb, m_ref[...], preferred_element_type=F32) + _dot_nt(hin_bf, gt_ref[...]))
        return [_gelu_tanh(y[:, t * OCT:(t + 1) * OCT] + d_ref[...] * xs[t]) for t in range(CHUNK)]

    @pl.when(b < BATCH)
    def _():
        R = SEQ // CHUNK
        xs = [u_ref[pl.ds(s, R, stride=CHUNK), :] for s in range(CHUNK)]
        xb = jnp.concatenate(xs, axis=1).astype(BF16)
        sloc = jnp.dot(xb, f_ref[...], preferred_element_type=F32)
        hin, fin = _chunk_scan(sloc, tab_ref)
        for t, yt in enumerate(outputs(xs, xb, hin.astype(BF16))):
            y_ref[pl.ds(t, R, stride=CHUNK), :] = yt
        hout_ref[...] = fin

    @pl.when(b == BATCH)
    def _():
        B = DEC_BATCH
        xs = [u_ref[B * s:B * (s + 1), :] for s in range(CHUNK)]
        xb = jnp.concatenate(xs, axis=1).astype(BF16)
        sloc = jnp.dot(xb, f_ref[...], preferred_element_type=F32)
        h0r, h0i = h0r_ref[...], h0i_ref[...]
        hin = jnp.concatenate([h0r, h0i], axis=1).astype(BF16)
        for t, yt in enumerate(outputs(xs, xb, hin)):
            y_ref[B * t:B * (t + 1), :] = yt
        lr = tab_ref[3 * SUBLANES:3 * SUBLANES + 1, :OCT_STATES]
        li = tab_ref[3 * SUBLANES:3 * SUBLANES + 1, OCT_STATES:]
        nr, ni = _cmul(lr, li, h0r, h0i)
        hr_ref[...] = nr + sloc[:, :OCT_STATES]
        hi_ref[...] = ni + sloc[:, OCT_STATES:]


def _ssm(u, h0r, h0i, tables):
    col0 = POOL_WIDTH // OCT
    im3 = lambda o, b: (o, 0, 0)
    st_spec = pl.BlockSpec((DEC_BATCH, OCT_STATES), lambda o, b: (0, o))
    return pl.pallas_call(
        _ssm_body,
        grid=(N_OCT, BATCH + 1),
        in_specs=[
            pl.BlockSpec((SEQ, OCT), lambda o, b: (b, col0 + o)), st_spec, st_spec,
            pl.BlockSpec((None, OCT, 2 * SSM_STATE), im3),
            pl.BlockSpec((None, OCT, 2 * SSM_STATE), im3),
            pl.BlockSpec((None, 2 * SUBLANES, SW), im3),
            pl.BlockSpec((None, 4 * SUBLANES, SW), im3),
            pl.BlockSpec((None, 1, OCT), im3),
        ],
        out_specs=(
            pl.BlockSpec((SEQ, OCT), lambda o, b: (b, o)),
            pl.BlockSpec((None, 1, SW), lambda o, b: (jnp.minimum(b, BATCH - 1) * N_OCT + o, 0, 0)),
            st_spec, st_spec,
        ),
        out_shape=(
            jax.ShapeDtypeStruct((T_PAD, SSM_WIDTH), F32),
            jax.ShapeDtypeStruct((BATCH * N_OCT, 1, SW), F32),
            jax.ShapeDtypeStruct((DEC_BATCH, SSM_GROUPS * SSM_STATE), F32),
            jax.ShapeDtypeStruct((DEC_BATCH, SSM_GROUPS * SSM_STATE), F32),
        ),
        scratch_shapes=[pltpu.VMEM((CW, SW), BF16), pltpu.VMEM((CW, SW), BF16), pltpu.VMEM((CW, CW), BF16)],
        compiler_params=_cparams(2),
        name="ssm",
    )(u, h0r, h0i, *tables)


PM_TM = 256
PM_PROMPT_BLOCKS = T_PROMPT // PM_TM
PM_STEPS = T_ALL // PM_TM


def _route(logits, valid, cnt_ref):
    lane = lax.broadcasted_iota(I32, (PM_TM, ROUTE_LANES), 1)
    neg = jnp.float32(-jnp.inf)
    big = jnp.int32(1 << 20)
    is_g = lane < MOE_GROUPS
    gmax = jnp.max(jnp.where(is_g, logits, neg), axis=1, keepdims=True)
    g_idx = jnp.min(jnp.where(is_g & (logits == gmax), lane, big), axis=1, keepdims=True)
    g_den = jnp.sum(jnp.where(is_g, jnp.exp(logits - gmax), 0.0), axis=1, keepdims=True)
    g_val = 1.0 / g_den
    e_lane = lane - EXP_LANE0
    sel = (e_lane >= 0) & (e_lane < MOE_EXPERTS) & ((e_lane >> 3) == g_idx)
    m1 = jnp.max(jnp.where(sel, logits, neg), axis=1, keepdims=True)
    i1 = jnp.min(jnp.where(sel & (logits == m1), lane, big), axis=1, keepdims=True)
    sel2 = sel & (lane != i1)
    m2 = jnp.max(jnp.where(sel2, logits, neg), axis=1, keepdims=True)
    i2 = jnp.min(jnp.where(sel2 & (logits == m2), lane, big), axis=1, keepdims=True)
    e2 = jnp.exp(m2 - m1)
    w1 = g_val / (1.0 + e2)
    w2 = g_val * e2 / (1.0 + e2)
    oh1 = lane == i1
    oh2 = lane == i2
    oh = jnp.where(oh1 | oh2, valid, 0.0)
    rr = lax.broadcasted_iota(I32, (PM_TM, PM_TM), 0)
    cc = lax.broadcasted_iota(I32, (PM_TM, PM_TM), 1)
    tri = jnp.where(cc < rr, 1.0, 0.0).astype(BF16)
    base = cnt_ref[...] + jnp.dot(tri, oh.astype(BF16), preferred_element_type=F32)
    rank1 = jnp.sum(jnp.where(oh1, base, 0.0), axis=1, keepdims=True)
    rank2 = jnp.sum(jnp.where(oh2, base, 0.0), axis=1, keepdims=True)
    cnt_ref[...] = cnt_ref[...] + jnp.sum(oh, axis=0, keepdims=True)
    rt = jnp.where(lane == 0, w1, 0.0)
    rt = jnp.where(lane == 1, w2, rt)
    rt = jnp.where(lane == 2, rank1, rt)
    rt = jnp.where(lane == 3, rank2, rt)
    rt = jnp.where(lane == 4, (i1 - EXP_LANE0).astype(F32), rt)
    rt = jnp.where(lane == 5, (i2 - EXP_LANE0).astype(F32), rt)
    return rt


def _postmix_body(ya_ref, gp_ref, gs_ref, yp_ref, xp_ref, xs_ref, wa_ref, wb_ref, wo_ref,
                  gf_ref, wr_ref, br_ref, h_ref, tn_ref, rt_ref, rtt_ref, cnt_out_ref,
                  cnt_ref, lg_ref):
    i = pl.program_id(0)
    @pl.when(i == 0)
    def _():
        cnt_ref[...] = jnp.zeros_like(cnt_ref)
        lg_ref[...] = jnp.zeros_like(lg_ref)
    prev_logits = lg_ref[...]

    ya = ya_ref[...].astype(BF16)
    a = jnp.dot(ya, wa_ref[...], preferred_element_type=F32)
    bg = jnp.dot(ya, wb_ref[...], preferred_element_type=F32)
    y_ssm = a * _sigmoid(bg)
    merged = (gp_ref[...].astype(F32) * yp_ref[...].astype(F32)
              + gs_ref[...].astype(F32) * y_ssm)
    x = jnp.where(jnp.minimum(i, PM_STEPS - 1) < PM_PROMPT_BLOCKS, xp_ref[...], xs_ref[...])
    h = x + jnp.dot(merged.astype(BF16), wo_ref[...], preferred_element_type=F32)
    h_ref[...] = h
    inv = lax.rsqrt(jnp.mean(h * h, axis=-1, keepdims=True) + EPS)
    tn = (h * inv) * gf_ref[...]
    tn_ref[...] = _pack_pairs(tn)
    t_hi = tn.astype(BF16)
    t_lo = (tn - t_hi.astype(F32)).astype(BF16)
    hh = jnp.dot(t_hi, wr_ref[...], preferred_element_type=F32)
    lh = jnp.dot(t_lo, wr_ref[:, :ROUTE_LANES], preferred_element_type=F32)
    lg_ref[...] = (hh[:, :ROUTE_LANES] + lh + hh[:, ROUTE_LANES:]) + br_ref[...]
    rt = _route(prev_logits, jnp.where(i > 0, 1.0, 0.0), cnt_ref)
    rt_ref[...] = rt
    rtt_ref[...] = rt.T[:8, :]
    cnt_out_ref[...] = cnt_ref[...]


def _postmix(y_act, gates, y_pool, xp, xs, wa, wb, wo, g_ffn, wr_cat, b_r):
    npb = PM_PROMPT_BLOCKS
    const2 = lambda i: (0, 0)
    tile = lambda i: jnp.minimum(i, PM_STEPS - 1)
    return pl.pallas_call(
        _postmix_body,
        grid=(PM_STEPS + 1,),
        in_specs=[
            pl.BlockSpec((PM_TM, SSM_WIDTH), lambda i: (tile(i), 0)),
            pl.BlockSpec((PM_TM, D_MODEL), lambda i: (tile(i), 0)),
            pl.BlockSpec((PM_TM, D_MODEL), lambda i: (tile(i), 1)),
            pl.BlockSpec((PM_TM, D_MODEL), lambda i: (tile(i), 0)),
            pl.BlockSpec((PM_TM, D_MODEL), lambda i: (jnp.minimum(i, npb - 1), 0)),
            pl.BlockSpec((PM_TM, D_MODEL), lambda i: (jnp.maximum(tile(i) - npb, 0), 0)),
            pl.BlockSpec((SSM_WIDTH, D_MODEL), const2, pipeline_mode=pl.Buffered(1)),
            pl.BlockSpec((SSM_WIDTH, D_MODEL), const2, pipeline_mode=pl.Buffered(1)),
            pl.BlockSpec((D_MODEL, D_MODEL), const2, pipeline_mode=pl.Buffered(1)),
            pl.BlockSpec((1, D_MODEL), const2),
            pl.BlockSpec((D_MODEL, 2 * ROUTE_LANES), const2),
            pl.BlockSpec((1, ROUTE_LANES), const2),
        ],
        out_specs=(
            pl.BlockSpec((PM_TM, D_MODEL), lambda i: (tile(i), 0)),
            pl.BlockSpec((PM_TM, D_PACK), lambda i: (tile(i), 0)),
            pl.BlockSpec((PM_TM, ROUTE_LANES), lambda i: (jnp.maximum(i - 1, 0), 0)),
            pl.BlockSpec((8, PM_TM), lambda i: (0, jnp.maximum(i - 1, 0))),
            pl.BlockSpec((1, ROUTE_LANES), const2),
        ),
        out_shape=(
            jax.ShapeDtypeStruct((T_ALL, D_MODEL), F32),
            jax.ShapeDtypeStruct((T_ALL, D_PACK), U32),
            jax.ShapeDtypeStruct((T_ALL, ROUTE_LANES), F32),
            jax.ShapeDtypeStruct((8, T_ALL), F32),
            jax.ShapeDtypeStruct((1, ROUTE_LANES), F32),
        ),
        scratch_shapes=[pltpu.VMEM((1, ROUTE_LANES), F32), pltpu.VMEM((PM_TM, ROUTE_LANES), F32)],
        compiler_params=_cparams(1),
        name="postmix",
    )(y_act, gates, gates, y_pool, xp, xs, wa, wb, wo, g_ffn, wr_cat, b_r)


SC_CH = 96


def _sc_workers():
    info = plsc.get_sparse_core_info()
    return info.num_cores, info.num_cores * info.num_subcores


def _sc_dispatch(tn, slots):
    n_cores, n_workers = _sc_workers()
    per_w = (T_ALL // SC_CH) // n_workers
    assert per_w * n_workers * SC_CH == T_ALL
    slots = slots.reshape(2, n_workers, per_w, SC_CH)

    @functools.partial(
        pl.kernel,
        mesh=plsc.VectorSubcoreMesh(core_axis_name="c", subcore_axis_name="s"),
        out_type=jax.ShapeDtypeStruct((N_SLOTS, D_PACK), U32),
        scratch_types=[pltpu.VMEM((2, per_w, SC_CH), I32), pltpu.VMEM((SC_CH, D_PACK), U32)],
    )
    def k(tn_hbm, slots_hbm, xs_hbm, idx_v, rows_v):
        wid = lax.axis_index("s") * n_cores + lax.axis_index("c")
        c0 = wid * per_w
        pltpu.sync_copy(slots_hbm.at[0, wid], idx_v.at[0])
        pltpu.sync_copy(slots_hbm.at[1, wid], idx_v.at[1])

        @pl.loop(0, per_w)
        def _(c):
            row0 = pl.multiple_of((c0 + c) * SC_CH, SC_CH)
            pltpu.sync_copy(tn_hbm.at[pl.ds(row0, SC_CH)], rows_v)
            pltpu.sync_copy(rows_v, xs_hbm.at[idx_v.at[0, c]])
            pltpu.sync_copy(rows_v, xs_hbm.at[idx_v.at[1, c]])

    return k(tn, slots)


def _sc_collect(ys, slots):
    n_cores, n_workers = _sc_workers()
    per_w = (N_ASSIGN // SC_CH) // n_workers
    assert per_w * n_workers * SC_CH == N_ASSIGN
    slots = slots.reshape(n_workers, per_w, SC_CH)

    @functools.partial(
        pl.kernel,
        mesh=plsc.VectorSubcoreMesh(core_axis_name="c", subcore_axis_name="s"),
        out_type=jax.ShapeDtypeStruct((N_ASSIGN, D_PACK), U32),
        scratch_types=[pltpu.VMEM((per_w, SC_CH), I32), pltpu.VMEM((SC_CH, D_PACK), U32)],
    )
    def k(ys_hbm, slots_hbm, out_hbm, idx_v, rows_v):
        wid = lax.axis_index("s") * n_cores + lax.axis_index("c")
        c0 = wid * per_w
        pltpu.sync_copy(slots_hbm.at[wid], idx_v)

        @pl.loop(0, per_w)
        def _(c):
            row0 = pl.multiple_of((c0 + c) * SC_CH, SC_CH)
            pltpu.sync_copy(ys_hbm.at[idx_v.at[c]], rows_v)
            pltpu.sync_copy(rows_v, out_hbm.at[pl.ds(row0, SC_CH)])

    return k(ys, slots)


W_PARTS = 2


def _expert_body(t0_ref, nt_ref, xs_hbm, wg_hbm, wu_hbm, wd_hbm, ys_hbm,
                 wg_ref, wu_ref, wd_ref, xb_ref, yb_ref, wgb_ref, wub_ref, wdb_ref, wsem, xsem, ysem):
    e = pl.program_id(0)
    n = nt_ref[e]
    g0 = t0_ref[e]
    ws = e & 1

    def w_copies(ex, slot):
        out = []
        for hbm, buf in ((wg_hbm, wg_ref), (wu_hbm, wu_ref), (wd_hbm, wd_ref)):
            rb = buf.shape[1] // W_PARTS
            for p in range(W_PARTS):
                out.append((pltpu.make_async_copy(hbm.at[ex, pl.ds(p * rb, rb)],
                                                  buf.at[slot, pl.ds(p * rb, rb)], wsem.at[slot]), p))
        return out

    @pl.when(e == 0)
    def _():
        for cp, p in w_copies(0, 0):
            cp.start(priority=p)

    @pl.when(e + 1 < MOE_EXPERTS)
    def _():
        for cp, p in w_copies(e + 1, 1 - ws):
            cp.start(priority=p)

    for cp, _ in w_copies(e, ws):
        cp.wait()

    def rows(j):
        return pl.ds(pl.multiple_of((g0 + j) * TME, TME), TME)

    def x_copy(j, s):
        return pltpu.make_async_copy(xs_hbm.at[rows(j)], xb_ref.at[s], xsem.at[s])

    def y_copy(j, s):
        return pltpu.make_async_copy(yb_ref.at[s], ys_hbm.at[rows(j)], ysem.at[s])

    @pl.when(n > 0)
    def _():
        x_copy(0, 0).start()
        wgb_ref[...] = wg_ref[ws].astype(BF16)
        wub_ref[...] = wu_ref[ws].astype(BF16)
        wdb_ref[...] = wd_ref[ws].astype(BF16)

        def tile(j, c):
            s = j & 1
            x_copy(j, s).wait()
            @pl.when(j + 1 < n)
            def _():
                x_copy(j + 1, 1 - s).start()
            @pl.when(j >= 2)
            def _():
                y_copy(j - 2, s).wait()
            x = _unpack_pairs(xb_ref[s], BF16)
            hg = jnp.dot(x, wgb_ref[...], preferred_element_type=F32)
            hu = jnp.dot(x, wub_ref[...], preferred_element_type=F32)
            act = (hg * _sigmoid(hg)) * hu
            yb_ref[s] = _pack_pairs(jnp.dot(act.astype(BF16), wdb_ref[...], preferred_element_type=F32))
            y_copy(j, s).start()
            return c
        lax.fori_loop(0, n, tile, 0)

        @pl.when(n >= 2)
        def _():
            y_copy(n - 2, n & 1).wait()
        y_copy(n - 1, (n - 1) & 1).wait()


def _experts(tile0, tiles, xs, w_eg, w_eu, w_ed):
    any_spec = pl.BlockSpec(memory_space=pl.ANY)
    grid_spec = pltpu.PrefetchScalarGridSpec(
        num_scalar_prefetch=2,
        grid=(MOE_EXPERTS,),
        in_specs=[any_spec] * 4,
        out_specs=any_spec,
        scratch_shapes=[
            pltpu.VMEM((2, D_MODEL, MOE_FF), F32),
            pltpu.VMEM((2, D_MODEL, MOE_FF), F32),
            pltpu.VMEM((2, MOE_FF, D_MODEL), F32),
            pltpu.VMEM((2, TME, D_PACK), U32),
            pltpu.VMEM((2, TME, D_PACK), U32),
            pltpu.VMEM((D_MODEL, MOE_FF), BF16),
            pltpu.VMEM((D_MODEL, MOE_FF), BF16),
            pltpu.VMEM((MOE_FF, D_MODEL), BF16),
            pltpu.SemaphoreType.DMA((2,)),
            pltpu.SemaphoreType.DMA((2,)),
            pltpu.SemaphoreType.DMA((2,)),
        ],
    )
    return pl.pallas_call(
        _expert_body,
        grid_spec=grid_spec,
        out_shape=jax.ShapeDtypeStruct((N_SLOTS, D_PACK), U32),
        compiler_params=_cparams(1),
        name="experts",
    )(tile0, tiles, xs, w_eg, w_eu, w_ed)


FN_TM = 512
FN_PROMPT_BLOCKS = T_PROMPT // FN_TM


def _final_body(h_ref, y0_ref, y1_ref, rt_ref, g_ref, op_ref, os_ref):
    i = pl.program_id(0)
    rt = rt_ref[...]
    z = (h_ref[...] + rt[:, 0:1] * _unpack_pairs(y0_ref[...], F32)
         + rt[:, 1:2] * _unpack_pairs(y1_ref[...], F32))
    inv = lax.rsqrt(jnp.mean(z * z, axis=-1, keepdims=True) + EPS)
    out = (z * inv) * g_ref[...]
    @pl.when(i < FN_PROMPT_BLOCKS)
    def _():
        op_ref[...] = out
    @pl.when(i >= FN_PROMPT_BLOCKS)
    def _():
        os_ref[...] = out


def _final(h, y, route, g_final):
    n = T_ALL // FN_TM
    npb = FN_PROMPT_BLOCKS
    yoff = T_ALL // FN_TM
    return pl.pallas_call(
        _final_body,
        grid=(n,),
        in_specs=[
            pl.BlockSpec((FN_TM, D_MODEL), lambda i: (i, 0)),
            pl.BlockSpec((FN_TM, D_PACK), lambda i: (i, 0)),
            pl.BlockSpec((FN_TM, D_PACK), lambda i: (yoff + i, 0)),
            pl.BlockSpec((FN_TM, ROUTE_LANES), lambda i: (i, 0)),
            pl.BlockSpec((1, D_MODEL), lambda i: (0, 0)),
        ],
        out_specs=(
            pl.BlockSpec((FN_TM, D_MODEL), lambda i: (jnp.minimum(i, npb - 1), 0)),
            pl.BlockSpec((FN_TM, D_MODEL), lambda i: (jnp.maximum(i - npb, 0), 0)),
        ),
        out_shape=(
            jax.ShapeDtypeStruct((T_PROMPT, D_MODEL), F32),
            jax.ShapeDtypeStruct((T_SAMPLE, D_MODEL), F32),
        ),
        compiler_params=_cparams(1),
        name="final",
    )(h, y, y, route, g_final)


def _dispatch_plan(route_t, cnt):
    counts = cnt[0, EXP_LANE0:EXP_LANE0 + MOE_EXPERTS].astype(I32)
    tiles = (counts + (TME - 1)) // TME
    cumt = jnp.cumsum(tiles)
    pad_off = (cumt - tiles) * TME
    rank = route_t[2:4].astype(I32)
    eid = route_t[4:6].astype(I32)
    onehot = eid[None] == jnp.arange(MOE_EXPERTS, dtype=I32)[:, None, None]
    slots = rank + jnp.sum(jnp.where(onehot, pad_off[:, None, None], 0), axis=0)
    return slots, cumt - tiles, tiles


def kernel(x_prompt, x_sample, state_pool, state_ssm_re, state_ssm_im, g_mix, w_in, w_pool,
           pool_scale, ssm_a_re, ssm_a_im, ssm_log_dt, ssm_b_re, ssm_b_im, ssm_c_re, ssm_c_im,
           ssm_d, w_glu_a, w_glu_b, w_out, g_ffn, w_router_group, b_router_group,
           w_router_expert, b_router_expert, w_exp_gate, w_exp_up, w_exp_down, g_final):
    l = 0
    xp = x_prompt.reshape(T_PROMPT, D_MODEL)
    xs = x_sample.transpose(1, 0, 2).reshape(T_SAMPLE, D_MODEL)
    w_pool_bf = w_pool[l].astype(BF16)
    g_mix2 = g_mix[l].reshape(1, D_MODEL)
    scale2 = pool_scale[l].reshape(1, D_MODEL)

    n_pt = T_PROMPT // IN_TM
    u, gates, w_in_bf = _inproj(xp, g_mix2, w_in[l], 0, n_i=1, emit_w=True)
    u, gates, wa_bf, wb_bf, wo_bf = _inproj(xp, g_mix2, w_in_bf, 1, dst=(u, gates), x_block0=1, n_i=n_pt - 1,
                                            cast=(w_glu_a[l], w_glu_b[l], w_out[l]))
    u, gates = _inproj(xs, g_mix2, w_in_bf, n_pt, dst=(u, gates))

    y_pool, pool_tail = _pool_prompt(u, w_pool_bf, scale2)
    hist_tm = state_pool[l].transpose(1, 0, 2)
    y_pool, pool_buf_tm = _pool_sample(u, hist_tm, w_pool_bf, scale2, y_pool)

    tables = _ssm_tables(ssm_a_re[l], ssm_a_im[l], ssm_log_dt[l], ssm_b_re[l], ssm_b_im[l],
                         ssm_c_re[l], ssm_c_im[l], ssm_d[l])
    h0r = state_ssm_re[l].reshape(DEC_BATCH, SSM_GROUPS * SSM_STATE)
    h0i = state_ssm_im[l].reshape(DEC_BATCH, SSM_GROUPS * SSM_STATE)
    y_act, h_prompt, hs_re, hs_im = _ssm(u, h0r, h0i, tables)

    assert EXP_LANE0 == MOE_GROUPS
    w_r = jnp.concatenate([w_router_group[l], w_router_expert[l],
                           jnp.zeros((D_MODEL, ROUTE_LANES - EXP_LANE0 - MOE_EXPERTS), F32)], axis=1)
    wr_hi = w_r.astype(BF16)
    wr_cat = jnp.concatenate([wr_hi, (w_r - wr_hi.astype(F32)).astype(BF16)], axis=1)
    b_r = jnp.zeros((1, ROUTE_LANES), F32)
    b_r = b_r.at[0, :MOE_GROUPS].set(b_router_group[l])
    b_r = b_r.at[0, EXP_LANE0:EXP_LANE0 + MOE_EXPERTS].set(b_router_expert[l])

    h, tn, route, route_t, cnt = _postmix(y_act, gates, y_pool, xp, xs, wa_bf, wb_bf, wo_bf,
                                          g_ffn[l].reshape(1, D_MODEL), wr_cat, b_r)
    slots, tile0, tiles = _dispatch_plan(route_t, cnt)
    xs_sorted = _sc_dispatch(tn, slots)
    ys_sorted = _experts(tile0, tiles, xs_sorted, w_exp_gate[l], w_exp_up[l], w_exp_down[l])
    y = _sc_collect(ys_sorted, slots)
    yp, ys = _final(h, y, route, g_final.reshape(1, D_MODEL))

    y_prompt = yp.reshape(BATCH, SEQ, D_MODEL)
    y_sample = ys.reshape(DEC_SEQ, DEC_BATCH, D_MODEL).transpose(1, 0, 2)
    new_pool_prompt = pool_tail[:, HIST - POOL_BUF:, :][None]
    new_pool_sample = pool_buf_tm.transpose(1, 0, 2)[None]
    hp = h_prompt.reshape(BATCH, N_OCT, 2, OCT_GROUPS, SSM_STATE).transpose(2, 0, 1, 3, 4)
    hp = hp.reshape(2, BATCH, SSM_GROUPS, SSM_STATE)
    shp = (1, DEC_BATCH, SSM_GROUPS, SSM_STATE)
    return (y_prompt, y_sample, new_pool_prompt, hp[0][None], hp[1][None], new_pool_sample,
            hs_re.reshape(shp), hs_im.reshape(shp))
```

```python
import functools
import math

import jax
import jax.numpy as jnp
from jax import lax
from jax.experimental import pallas as pl
from jax.experimental.pallas import tpu as pltpu
from jax.experimental.pallas import tpu_sc as plsc

F32 = jnp.float32
BF16 = jnp.bfloat16
I32 = jnp.int32
U32 = jnp.uint32

D_MODEL = 2048
BATCH = 4
SEQ = 2048
DEC_BATCH = 128
DEC_SEQ = 8
PAST_LEN = 16384
POOL_WIDTH = D_MODEL // 2
POOL_WINDOWS = (2, 4, 8, 16)
POOL_GROUPS = len(POOL_WINDOWS)
POOL_GROUP_CH = POOL_WIDTH // POOL_GROUPS
POOL_OUT_CH = D_MODEL // POOL_GROUPS
POOL_BUF = max(POOL_WINDOWS) - 1
SSM_WIDTH = D_MODEL // 2
SSM_GROUP_CH = 16
SSM_GROUPS = SSM_WIDTH // SSM_GROUP_CH
SSM_STATE = 64
IN_WIDTH = POOL_WIDTH + SSM_WIDTH + 2 * D_MODEL
D_PACK = D_MODEL // 2
MOE_GROUPS = 4
MOE_EPG = 8
MOE_EXPERTS = MOE_GROUPS * MOE_EPG
MOE_FF = D_MODEL // 4
EPS = 1e-6

T_PROMPT = BATCH * SEQ
T_SAMPLE = DEC_BATCH * DEC_SEQ
T_ALL = T_PROMPT + T_SAMPLE
T_PAD = (BATCH + 1) * SEQ

LANES = 128
SUBLANES = 8
VMEM_LIMIT = 56 * 1024 * 1024

CHUNK = 8
OCT = LANES
N_OCT = SSM_WIDTH // OCT
OCT_GROUPS = OCT // SSM_GROUP_CH
OCT_STATES = OCT_GROUPS * SSM_STATE
CW = CHUNK * OCT
SW = 2 * OCT_STATES

ROUTE_LANES = LANES
EXP_LANE0 = MOE_GROUPS
N_ASSIGN = 2 * T_ALL
TME = 256
N_ITEMS_MAX = N_ASSIGN // TME + MOE_EXPERTS
N_SLOTS = N_ITEMS_MAX * TME


def _cparams(n_axes):
    return pltpu.CompilerParams(dimension_semantics=("arbitrary",) * n_axes,
                                vmem_limit_bytes=VMEM_LIMIT)


def _sigmoid(x):
    return 1.0 / (1.0 + jnp.exp(-x))


def _pack_pairs(x):
    c = x.shape[1] // 2
    hi = lax.bitcast_convert_type(x[:, :c].astype(BF16).astype(F32), U32)
    lo = lax.bitcast_convert_type(x[:, c:].astype(BF16).astype(F32), U32)
    return hi | (lo >> 16)


def _unpack_pairs(u, dtype):
    hi = lax.bitcast_convert_type(u & jnp.uint32(0xFFFF0000), F32)
    lo = lax.bitcast_convert_type(u << 16, F32)
    return jnp.concatenate([hi, lo], axis=1).astype(dtype)


def _gelu_tanh(x):
    c = math.sqrt(2.0 / math.pi)
    return 0.5 * x * (1.0 + jnp.tanh(c * (x + 0.044715 * (x * x * x))))


IN_TM = 1024
IN_TN = 1024
U_WIDTH = POOL_WIDTH + SSM_WIDTH
GATE_WIDTH = 2 * D_MODEL
IN_U_STEPS = U_WIDTH // IN_TN


CAST_RB = 128
IN_RC = 128


def _inproj_body(n_alias, cast_ranges, emit_w, x_ref, g_ref, w_ref, *rest):
    n_cast = len(cast_ranges)
    cast_in = rest[n_alias:n_alias + n_cast]
    u_ref, gate_ref = rest[n_alias + n_cast:n_alias + n_cast + 2]
    cast_out = rest[n_alias + n_cast + 2:n_alias + 2 * n_cast + 2]
    xn_ref = rest[-1]
    j = pl.program_id(1)

    def weights():
        if not emit_w:
            return w_ref
        wbf_ref = rest[-2]
        wbf_ref[...] = w_ref[...].astype(BF16)
        return wbf_ref

    step = pl.program_id(0) * pl.num_programs(1) + j
    for (lo, hi), ci, co in zip(cast_ranges, cast_in, cast_out):
        @pl.when((step >= lo) & (step < hi))
        def _():
            co[...] = ci[...].astype(BF16)
    chunks = [slice(r * IN_RC, (r + 1) * IN_RC) for r in range(IN_TM // IN_RC)]
    @pl.when(j == 0)
    def _():
        w = weights()
        for rows in chunks:
            x = x_ref[rows, :]
            inv = lax.rsqrt(jnp.mean(x * x, axis=-1, keepdims=True) + EPS)
            xn = ((x * inv) * g_ref[...]).astype(BF16)
            xn_ref[rows, :] = xn
            u_ref[rows, :] = jnp.dot(xn, w[...], preferred_element_type=F32)
    @pl.when((j > 0) & (j < IN_U_STEPS))
    def _():
        u_ref[...] = jnp.dot(xn_ref[...], weights()[...], preferred_element_type=F32)
    @pl.when(j >= IN_U_STEPS)
    def _():
        w = weights()
        for rows in chunks:
            acc = jnp.dot(xn_ref[rows, :], w[...], preferred_element_type=F32)
            gate_ref[rows, :] = _sigmoid(acc).astype(BF16)


def _inproj(x, g, w, row_block0, dst=None, cast=(), x_block0=0, n_i=None, emit_w=False):
    if n_i is None:
        n_i = x.shape[0] // IN_TM
    assert not emit_w or n_i == 1
    n_j = IN_WIDTH // IN_TN
    x_mode = dict(pipeline_mode=pl.Buffered(1)) if emit_w else {}
    in_specs = [
        pl.BlockSpec((IN_TM, D_MODEL), lambda i, j: (i + x_block0, 0), **x_mode),
        pl.BlockSpec((1, D_MODEL), lambda i, j: (0, 0)),
        pl.BlockSpec((D_MODEL, IN_TN), lambda i, j: (0, j)),
    ]
    args = [x, g, w]
    aliases = {}
    if dst is not None:
        in_specs += [pl.BlockSpec(memory_space=pl.ANY)] * 2
        args += list(dst)
        aliases = {3: 0, 4: 1}
    cast_specs, cast_shapes, cast_ranges, off = [], [], [], 0
    for wc in cast:
        nb = wc.shape[0] // CAST_RB
        cast_specs.append(pl.BlockSpec(
            (CAST_RB, wc.shape[1]), lambda i, j, o=off, nb=nb: (jnp.clip(i * n_j + j - o, 0, nb - 1), 0)))
        cast_shapes.append(jax.ShapeDtypeStruct(wc.shape, BF16))
        cast_ranges.append((off, off + nb))
        off += nb
    assert off <= n_i * n_j
    args += list(cast)
    w_specs, w_shapes = [], []
    if emit_w:
        w_specs = [pl.BlockSpec((D_MODEL, IN_TN), lambda i, j: (0, j))]
        w_shapes = [jax.ShapeDtypeStruct((D_MODEL, IN_WIDTH), BF16)]
    return pl.pallas_call(
        functools.partial(_inproj_body, len(aliases), tuple(cast_ranges), emit_w),
        grid=(n_i, n_j),
        in_specs=in_specs + cast_specs,
        out_specs=(
            pl.BlockSpec((IN_TM, IN_TN), lambda i, j: (i + row_block0, jnp.minimum(j, IN_U_STEPS - 1))),
            pl.BlockSpec((IN_TM, IN_TN), lambda i, j: (i + row_block0, jnp.maximum(j - IN_U_STEPS, 0))),
            *cast_specs, *w_specs,
        ),
        out_shape=(
            jax.ShapeDtypeStruct((T_PAD, U_WIDTH), F32),
            jax.ShapeDtypeStruct((T_ALL, GATE_WIDTH), BF16),
            *cast_shapes, *w_shapes,
        ),
        scratch_shapes=[pltpu.VMEM((IN_TM, D_MODEL), BF16)],
        input_output_aliases=aliases,
        compiler_params=_cparams(2),
        name="inproj",
    )(*args)


PP_TM = 512
HIST = 16


def _pool_project(pooled_g, g, w_ref, sc_ref, o_ref):
    y = jnp.dot(pooled_g.astype(BF16), w_ref[g], preferred_element_type=F32)
    lo, hi = g * POOL_OUT_CH, (g + 1) * POOL_OUT_CH
    o_ref[:, lo:hi] = (y * sc_ref[:, lo:hi]).astype(o_ref.dtype)


PP_RING = 3


def _pool_prompt_body(u_hbm, w_ref, sc_ref, o_ref, tail_ref, hist_ref, ubuf_ref, usem):
    i = pl.program_id(1)
    n_steps = pl.num_programs(0) * pl.num_programs(1)
    step = pl.program_id(0) * pl.num_programs(1) + i

    def fetch(t):
        if isinstance(t, int):
            slot, row0 = t % PP_RING, t * PP_TM
        else:
            slot, row0 = lax.rem(t, PP_RING), pl.multiple_of(t * PP_TM, PP_TM)
        return pltpu.make_async_copy(
            u_hbm.at[pl.ds(row0, PP_TM), pl.ds(0, POOL_WIDTH)],
            ubuf_ref.at[slot], usem.at[slot])

    @pl.when(step == 0)
    def _():
        for t in range(PP_RING - 1):
            fetch(t).start()
    @pl.when(step + (PP_RING - 1) < n_steps)
    def _():
        fetch(step + (PP_RING - 1)).start()
    fetch(step).wait()

    @pl.when(i == 0)
    def _():
        hist_ref[...] = jnp.zeros_like(hist_ref)
    u = ubuf_ref[lax.rem(step, PP_RING)]
    ext = jnp.concatenate([hist_ref[...], u], axis=0)
    hist_ref[...] = u[PP_TM - HIST:, :]
    tail_ref[...] = u[PP_TM - HIST:, :]
    pos = i * PP_TM + lax.broadcasted_iota(I32, (PP_TM, 1), 0)
    for g, w in enumerate(POOL_WINDOWS):
        lo, hi = g * POOL_GROUP_CH, (g + 1) * POOL_GROUP_CH
        s = ext[:, lo:hi]
        d = 1
        while d < w:
            s = s + pltpu.roll(s, d, axis=0)
            d *= 2
        cnt = jnp.minimum(w, pos + 1).astype(F32)
        pooled = s[HIST:, :] / cnt - u[:, lo:hi]
        _pool_project(pooled, g, w_ref, sc_ref, o_ref)


def _pool_prompt(u, w_pool_bf, pool_scale):
    n_i = SEQ // PP_TM
    return pl.pallas_call(
        _pool_prompt_body,
        grid=(BATCH, n_i),
        in_specs=[
            pl.BlockSpec(memory_space=pl.ANY),
            pl.BlockSpec((POOL_GROUPS, POOL_GROUP_CH, POOL_OUT_CH), lambda b, i: (0, 0, 0)),
            pl.BlockSpec((1, D_MODEL), lambda b, i: (0, 0)),
        ],
        out_specs=(
            pl.BlockSpec((PP_TM, D_MODEL), lambda b, i: (b * n_i + i, 0)),
            pl.BlockSpec((None, HIST, POOL_WIDTH), lambda b, i: (b, 0, 0)),
        ),
        out_shape=(
            jax.ShapeDtypeStruct((T_ALL, D_MODEL), BF16),
            jax.ShapeDtypeStruct((BATCH, HIST, POOL_WIDTH), F32),
        ),
        scratch_shapes=[pltpu.VMEM((HIST, POOL_WIDTH), F32),
                        pltpu.VMEM((PP_RING, PP_TM, POOL_WIDTH), F32),
                        pltpu.SemaphoreType.DMA((PP_RING,))],
        compiler_params=_cparams(2),
        name="pool_prompt",
    )(u, w_pool_bf, pool_scale)


def _pool_sample_body(u_ref, hist_ref, w_ref, sc_ref, _dst, o_ref, buf_ref):
    rows = [hist_ref[k] for k in range(POOL_BUF)]
    rows += [u_ref[DEC_BATCH * t:DEC_BATCH * (t + 1), :] for t in range(DEC_SEQ)]
    n = len(rows)
    for k in range(POOL_BUF):
        buf_ref[k] = rows[n - POOL_BUF + k]
    for g, w in enumerate(POOL_WINDOWS):
        lo, hi = g * POOL_GROUP_CH, (g + 1) * POOL_GROUP_CH
        f = [r[:, lo:hi] for r in rows]
        cur = f
        d = 1
        while d < w:
            cur = [cur[k] + cur[k - d] if k - d >= 0 else cur[k] for k in range(n)]
            d *= 2
        pooled = jnp.concatenate(
            [cur[POOL_BUF + t] / float(w) - f[POOL_BUF + t] for t in range(DEC_SEQ)], axis=0)
        _pool_project(pooled, g, w_ref, sc_ref, o_ref)


def _pool_sample(u, hist_tm, w_pool_bf, pool_scale, y_pool):
    blk = T_PROMPT // T_SAMPLE
    return pl.pallas_call(
        _pool_sample_body,
        grid=(1,),
        in_specs=[
            pl.BlockSpec((T_SAMPLE, POOL_WIDTH), lambda i: (blk, 0)),
            pl.BlockSpec((POOL_BUF, DEC_BATCH, POOL_WIDTH), lambda i: (0, 0, 0)),
            pl.BlockSpec((POOL_GROUPS, POOL_GROUP_CH, POOL_OUT_CH), lambda i: (0, 0, 0)),
            pl.BlockSpec((1, D_MODEL), lambda i: (0, 0)),
            pl.BlockSpec(memory_space=pl.ANY),
        ],
        out_specs=(
            pl.BlockSpec((T_SAMPLE, D_MODEL), lambda i: (blk, 0)),
            pl.BlockSpec((POOL_BUF, DEC_BATCH, POOL_WIDTH), lambda i: (0, 0, 0)),
        ),
        out_shape=(
            jax.ShapeDtypeStruct((T_ALL, D_MODEL), BF16),
            jax.ShapeDtypeStruct((POOL_BUF, DEC_BATCH, POOL_WIDTH), F32),
        ),
        input_output_aliases={4: 0},
        compiler_params=_cparams(1),
        name="pool_sample",
    )(u, hist_tm, w_pool_bf, pool_scale, y_pool)


def _ssm_tables(a_re, a_im, log_dt, b_re, b_im, c_re, c_im, d_skip):
    dt = jnp.exp(log_dt)[:, None]
    lr, li = a_re, a_im
    ab_re = jnp.exp(lr * dt) * jnp.cos(li * dt)
    ab_im = jnp.exp(lr * dt) * jnp.sin(li * dt)
    den = lr * lr + li * li
    nr, ni = ab_re - 1.0, ab_im
    q_re = (nr * lr + ni * li) / den
    q_im = (ni * lr - nr * li) / den
    bb_re = q_re[..., None] * b_re - q_im[..., None] * b_im
    bb_im = q_re[..., None] * b_im + q_im[..., None] * b_re

    def lam_rows(ks):
        k = jnp.asarray(ks, F32)[:, None, None]
        m = jnp.exp(k * lr * dt)
        re = (m * jnp.cos(k * li * dt)).reshape(len(ks), N_OCT, OCT_STATES)
        im = (m * jnp.sin(k * li * dt)).reshape(len(ks), N_OCT, OCT_STATES)
        return jnp.concatenate([re, im], axis=-1).transpose(1, 0, 2)

    def compact(re, im):
        v = jnp.concatenate([re, im], axis=-1)
        return v.reshape(N_OCT, OCT, 2 * SSM_STATE)

    bbc = compact(jnp.swapaxes(bb_re, 1, 2), jnp.swapaxes(bb_im, 1, 2))
    ccc = compact(c_re, c_im)
    pw = lam_rows(list(range(2 * SUBLANES)))
    r = jnp.arange(SUBLANES)[None, :, None]
    parts = [jnp.where(r >= dd, lam_rows([CHUNK * dd]), 0.0) for dd in (1, 2, 4)]
    parts.append(lam_rows([CHUNK * kk for kk in range(1, SUBLANES + 1)]))
    tab = jnp.concatenate(parts, axis=1)
    dsk = d_skip.reshape(N_OCT, 1, OCT)
    return bbc, ccc, pw, tab, dsk


def _split_bf16(x):
    hi = x.astype(BF16)
    return hi, (x - hi.astype(F32)).astype(BF16)


def _dot_nt(a, b):
    return lax.dot_general(a, b, (((1,), (1,)), ((), ())), preferred_element_type=F32)


def _build_weights(bbc_ref, ccc_ref, pw_ref, f_ref, gt_ref, m_ref):
    row_gi = lax.broadcasted_iota(I32, (OCT, 1), 0) >> 4
    col = lax.broadcasted_iota(I32, (1, SW), 1)
    col_gi = (col >> 6) & 7
    src = ((col >> 9) << 6) | (col & 63)
    k128 = lax.broadcasted_iota(I32, (2 * SSM_STATE, 1), 0)
    spread = jnp.where(k128 == src, 1.0, 0.0).astype(BF16)
    diag = row_gi == col_gi

    def expand(c_ref):
        hi, lo = _split_bf16(c_ref[...])
        d = (jnp.dot(hi, spread, preferred_element_type=F32)
             + jnp.dot(lo, spread, preferred_element_type=F32))
        d = jnp.where(diag, d, 0.0)
        return d[:, :OCT_STATES], d[:, OCT_STATES:]

    br, bi = expand(bbc_ref)
    cr, ci = expand(ccc_ref)
    chi_r, clo_r = _split_bf16(cr)
    chi_i, clo_i = _split_bf16(ci)

    def lam(k):
        return pw_ref[k:k + 1, :OCT_STATES], pw_ref[k:k + 1, OCT_STATES:]

    def dot3(a, bhi, blo):
        ahi, alo = _split_bf16(a)
        return _dot_nt(ahi, bhi) + _dot_nt(alo, bhi) + _dot_nt(ahi, blo)

    lags = []
    for k in range(CHUNK):
        pr, pi_ = lam(k)
        fr, fi = _cmul(br, bi, pr, pi_)
        s = CHUNK - 1 - k
        f_ref[s * OCT:(s + 1) * OCT, :] = jnp.concatenate([fr, fi], axis=1).astype(BF16)
        lags.append((dot3(fr, chi_r, clo_r) - dot3(fi, chi_i, clo_i)).astype(BF16))
        pr, pi_ = lam(k + 1)
        gr, gi = _cmul(cr, ci, pr, pi_)
        gt_ref[k * OCT:(k + 1) * OCT, :] = jnp.concatenate([gr, -gi], axis=1).astype(BF16)
    zero = jnp.zeros((OCT, OCT), BF16)
    for s in range(CHUNK):
        for t in range(CHUNK):
            m_ref[s * OCT:(s + 1) * OCT, t * OCT:(t + 1) * OCT] = lags[t - s] if t >= s else zero


def _cmul(ar, ai, br, bi):
    return ar * br - ai * bi, ar * bi + ai * br


def _chunk_scan(sloc, tab_ref):
    R = sloc.shape[0]
    nb = R // SUBLANES
    sr, si = sloc[:, :OCT_STATES], sloc[:, OCT_STATES:]
    rowi = lax.broadcasted_iota(I32, (R, 1), 0)
    tr = jnp.where(rowi == 0, 0.0, pltpu.roll(sr, 1, axis=0))
    ti = jnp.where(rowi == 0, 0.0, pltpu.roll(si, 1, axis=0))
    for lvl, d in enumerate((1, 2, 4)):
        mr = tab_ref[lvl * SUBLANES:(lvl + 1) * SUBLANES, :OCT_STATES]
        mi = tab_ref[lvl * SUBLANES:(lvl + 1) * SUBLANES, OCT_STATES:]
        mr = jnp.concatenate([mr] * nb, axis=0)
        mi = jnp.concatenate([mi] * nb, axis=0)
        pr, pi_ = _cmul(mr, mi, pltpu.roll(tr, d, axis=0), pltpu.roll(ti, d, axis=0))
        tr, ti = tr + pr, ti + pi_
    pwr = tab_ref[3 * SUBLANES:4 * SUBLANES, :OCT_STATES]
    pwi = tab_ref[3 * SUBLANES:4 * SUBLANES, OCT_STATES:]
    cr = jnp.zeros((1, OCT_STATES), F32)
    ci = jnp.zeros((1, OCT_STATES), F32)
    out_r, out_i = [], []
    for k in range(nb):
        ar = tr[k * SUBLANES:(k + 1) * SUBLANES, :]
        ai = ti[k * SUBLANES:(k + 1) * SUBLANES, :]
        pr, pi_ = _cmul(pwr, pwi, jnp.broadcast_to(cr, ar.shape), jnp.broadcast_to(ci, ai.shape))
        hr, hi = ar + pr, ai + pi_
        out_r.append(hr)
        out_i.append(hi)
        cr, ci = hr[SUBLANES - 1:, :], hi[SUBLANES - 1:, :]
    hin = jnp.concatenate([jnp.concatenate(out_r, axis=0), jnp.concatenate(out_i, axis=0)], axis=1)
    lr, li = pwr[0:1, :], pwi[0:1, :]
    fr, fi = _cmul(lr, li, cr, ci)
    fin = jnp.concatenate([fr + sr[R - 1:, :], fi + si[R - 1:, :]], axis=1)
    return hin, fin


def _ssm_body(u_ref, h0r_ref, h0i_ref, bbc_ref, ccc_ref, pw_ref, tab_ref, d_ref,
              y_ref, hout_ref, hr_ref, hi_ref, f_ref, gt_ref, m_ref):
    b = pl.program_id(1)

    @pl.when(b == 0)
    def _():
        _build_weights(bbc_ref, ccc_ref, pw_ref, f_ref, gt_ref, m_ref)

    def outputs(xs, xb, hin_bf):
        y = (jnp.dot(xb, m_ref[...], preferred_element_type=F32) + _dot_nt(hin_bf, gt_ref[...]))
        return [_gelu_tanh(y[:, t * OCT:(t + 1) * OCT] + d_ref[...] * xs[t]) for t in range(CHUNK)]

    @pl.when(b < BATCH)
    def _():
        R = SEQ // CHUNK
        xs = [u_ref[pl.ds(s, R, stride=CHUNK), :] for s in range(CHUNK)]
        xb = jnp.concatenate(xs, axis=1).astype(BF16)
        sloc = jnp.dot(xb, f_ref[...], preferred_element_type=F32)
        hin, fin = _chunk_scan(sloc, tab_ref)
        for t, yt in enumerate(outputs(xs, xb, hin.astype(BF16))):
            y_ref[pl.ds(t, R, stride=CHUNK), :] = yt
        hout_ref[...] = fin

    @pl.when(b == BATCH)
    def _():
        B = DEC_BATCH
        xs = [u_ref[B * s:B * (s + 1), :] for s in range(CHUNK)]
        xb = jnp.concatenate(xs, axis=1).astype(BF16)
        sloc = jnp.dot(xb, f_ref[...], preferred_element_type=F32)
        h0r, h0i = h0r_ref[...], h0i_ref[...]
        hin = jnp.concatenate([h0r, h0i], axis=1).astype(BF16)
        for t, yt in enumerate(outputs(xs, xb, hin)):
            y_ref[B * t:B * (t + 1), :] = yt
        lr = tab_ref[3 * SUBLANES:3 * SUBLANES + 1, :OCT_STATES]
        li = tab_ref[3 * SUBLANES:3 * SUBLANES + 1, OCT_STATES:]
        nr, ni = _cmul(lr, li, h0r, h0i)
        hr_ref[...] = nr + sloc[:, :OCT_STATES]
        hi_ref[...] = ni + sloc[:, OCT_STATES:]


def _ssm(u, h0r, h0i, tables):
    col0 = POOL_WIDTH // OCT
    im3 = lambda o, b: (o, 0, 0)
    st_spec = pl.BlockSpec((DEC_BATCH, OCT_STATES), lambda o, b: (0, o))
    return pl.pallas_call(
        _ssm_body,
        grid=(N_OCT, BATCH + 1),
        in_specs=[
            pl.BlockSpec((SEQ, OCT), lambda o, b: (b, col0 + o)), st_spec, st_spec,
            pl.BlockSpec((None, OCT, 2 * SSM_STATE), im3),
            pl.BlockSpec((None, OCT, 2 * SSM_STATE), im3),
            pl.BlockSpec((None, 2 * SUBLANES, SW), im3),
            pl.BlockSpec((None, 4 * SUBLANES, SW), im3),
            pl.BlockSpec((None, 1, OCT), im3),
        ],
        out_specs=(
            pl.BlockSpec((SEQ, OCT), lambda o, b: (b, o)),
            pl.BlockSpec((None, 1, SW), lambda o, b: (jnp.minimum(b, BATCH - 1) * N_OCT + o, 0, 0)),
            st_spec, st_spec,
        ),
        out_shape=(
            jax.ShapeDtypeStruct((T_PAD, SSM_WIDTH), F32),
            jax.ShapeDtypeStruct((BATCH * N_OCT, 1, SW), F32),
            jax.ShapeDtypeStruct((DEC_BATCH, SSM_GROUPS * SSM_STATE), F32),
            jax.ShapeDtypeStruct((DEC_BATCH, SSM_GROUPS * SSM_STATE), F32),
        ),
        scratch_shapes=[pltpu.VMEM((CW, SW), BF16), pltpu.VMEM((CW, SW), BF16), pltpu.VMEM((CW, CW), BF16)],
        compiler_params=_cparams(2),
        name="ssm",
    )(u, h0r, h0i, *tables)


PM_TM = 256
PM_PROMPT_BLOCKS = T_PROMPT // PM_TM
PM_STEPS = T_ALL // PM_TM


def _route(logits, valid, cnt_ref):
    lane = lax.broadcasted_iota(I32, (PM_TM, ROUTE_LANES), 1)
    neg = jnp.float32(-jnp.inf)
    big = jnp.int32(1 << 20)
    is_g = lane < MOE_GROUPS
    gmax = jnp.max(jnp.where(is_g, logits, neg), axis=1, keepdims=True)
    g_idx = jnp.min(jnp.where(is_g & (logits == gmax), lane, big), axis=1, keepdims=True)
    g_den = jnp.sum(jnp.where(is_g, jnp.exp(logits - gmax), 0.0), axis=1, keepdims=True)
    g_val = 1.0 / g_den
    e_lane = lane - EXP_LANE0
    sel = (e_lane >= 0) & (e_lane < MOE_EXPERTS) & ((e_lane >> 3) == g_idx)
    m1 = jnp.max(jnp.where(sel, logits, neg), axis=1, keepdims=True)
    i1 = jnp.min(jnp.where(sel & (logits == m1), lane, big), axis=1, keepdims=True)
    sel2 = sel & (lane != i1)
    m2 = jnp.max(jnp.where(sel2, logits, neg), axis=1, keepdims=True)
    i2 = jnp.min(jnp.where(sel2 & (logits == m2), lane, big), axis=1, keepdims=True)
    e2 = jnp.exp(m2 - m1)
    w1 = g_val / (1.0 + e2)
    w2 = g_val * e2 / (1.0 + e2)
    oh1 = lane == i1
    oh2 = lane == i2
    oh = jnp.where(oh1 | oh2, valid, 0.0)
    rr = lax.broadcasted_iota(I32, (PM_TM, PM_TM), 0)
    cc = lax.broadcasted_iota(I32, (PM_TM, PM_TM), 1)
    tri = jnp.where(cc < rr, 1.0, 0.0).astype(BF16)
    base = cnt_ref[...] + jnp.dot(tri, oh.astype(BF16), preferred_element_type=F32)
    rank1 = jnp.sum(jnp.where(oh1, base, 0.0), axis=1, keepdims=True)
    rank2 = jnp.sum(jnp.where(oh2, base, 0.0), axis=1, keepdims=True)
    cnt_ref[...] = cnt_ref[...] + jnp.sum(oh, axis=0, keepdims=True)
    rt = jnp.where(lane == 0, w1, 0.0)
    rt = jnp.where(lane == 1, w2, rt)
    rt = jnp.where(lane == 2, rank1, rt)
    rt = jnp.where(lane == 3, rank2, rt)
    rt = jnp.where(lane == 4, (i1 - EXP_LANE0).astype(F32), rt)
    rt = jnp.where(lane == 5, (i2 - EXP_LANE0).astype(F32), rt)
    return rt


def _postmix_body(ya_ref, gp_ref, gs_ref, yp_ref, xp_ref, xs_ref, wa_ref, wb_ref, wo_ref,
                  gf_ref, wr_ref, br_ref, h_ref, tn_ref, rt_ref, rtt_ref, cnt_out_ref,
                  cnt_ref, lg_ref):
    i = pl.program_id(0)
    @pl.when(i == 0)
    def _():
        cnt_ref[...] = jnp.zeros_like(cnt_ref)
        lg_ref[...] = jnp.zeros_like(lg_ref)
    prev_logits = lg_ref[...]

    ya = ya_ref[...].astype(BF16)
    a = jnp.dot(ya, wa_ref[...], preferred_element_type=F32)
    bg = jnp.dot(ya, wb_ref[...], preferred_element_type=F32)
    y_ssm = a * _sigmoid(bg)
    merged = (gp_ref[...].astype(F32) * yp_ref[...].astype(F32)
              + gs_ref[...].astype(F32) * y_ssm)
    x = jnp.where(jnp.minimum(i, PM_STEPS - 1) < PM_PROMPT_BLOCKS, xp_ref[...], xs_ref[...])
    h = x + jnp.dot(merged.astype(BF16), wo_ref[...], preferred_element_type=F32)
    h_ref[...] = h
    inv = lax.rsqrt(jnp.mean(h * h, axis=-1, keepdims=True) + EPS)
    tn = (h * inv) * gf_ref[...]
    tn_ref[...] = _pack_pairs(tn)
    t_hi = tn.astype(BF16)
    t_lo = (tn - t_hi.astype(F32)).astype(BF16)
    hh = jnp.dot(t_hi, wr_ref[...], preferred_element_type=F32)
    lh = jnp.dot(t_lo, wr_ref[:, :ROUTE_LANES], preferred_element_type=F32)
    lg_ref[...] = (hh[:, :ROUTE_LANES] + lh + hh[:, ROUTE_LANES:]) + br_ref[...]
    rt = _route(prev_logits, jnp.where(i > 0, 1.0, 0.0), cnt_ref)
    rt_ref[...] = rt
    rtt_ref[...] = rt.T[:8, :]
    cnt_out_ref[...] = cnt_ref[...]


def _postmix(y_act, gates, y_pool, xp, xs, wa, wb, wo, g_ffn, wr_cat, b_r):
    npb = PM_PROMPT_BLOCKS
    const2 = lambda i: (0, 0)
    tile = lambda i: jnp.minimum(i, PM_STEPS - 1)
    return pl.pallas_call(
        _postmix_body,
        grid=(PM_STEPS + 1,),
        in_specs=[
            pl.BlockSpec((PM_TM, SSM_WIDTH), lambda i: (tile(i), 0)),
            pl.BlockSpec((PM_TM, D_MODEL), lambda i: (tile(i), 0)),
            pl.BlockSpec((PM_TM, D_MODEL), lambda i: (tile(i), 1)),
            pl.BlockSpec((PM_TM, D_MODEL), lambda i: (tile(i), 0)),
            pl.BlockSpec((PM_TM, D_MODEL), lambda i: (jnp.minimum(i, npb - 1), 0)),
            pl.BlockSpec((PM_TM, D_MODEL), lambda i: (jnp.maximum(tile(i) - npb, 0), 0)),
            pl.BlockSpec((SSM_WIDTH, D_MODEL), const2, pipeline_mode=pl.Buffered(1)),
            pl.BlockSpec((SSM_WIDTH, D_MODEL), const2, pipeline_mode=pl.Buffered(1)),
            pl.BlockSpec((D_MODEL, D_MODEL), const2, pipeline_mode=pl.Buffered(1)),
            pl.BlockSpec((1, D_MODEL), const2),
            pl.BlockSpec((D_MODEL, 2 * ROUTE_LANES), const2),
            pl.BlockSpec((1, ROUTE_LANES), const2),
        ],
        out_specs=(
            pl.BlockSpec((PM_TM, D_MODEL), lambda i: (tile(i), 0)),
            pl.BlockSpec((PM_TM, D_PACK), lambda i: (tile(i), 0)),
            pl.BlockSpec((PM_TM, ROUTE_LANES), lambda i: (jnp.maximum(i - 1, 0), 0)),
            pl.BlockSpec((8, PM_TM), lambda i: (0, jnp.maximum(i - 1, 0))),
            pl.BlockSpec((1, ROUTE_LANES), const2),
        ),
        out_shape=(
            jax.ShapeDtypeStruct((T_ALL, D_MODEL), F32),
            jax.ShapeDtypeStruct((T_ALL, D_PACK), U32),
            jax.ShapeDtypeStruct((T_ALL, ROUTE_LANES), F32),
            jax.ShapeDtypeStruct((8, T_ALL), F32),
            jax.ShapeDtypeStruct((1, ROUTE_LANES), F32),
        ),
        scratch_shapes=[pltpu.VMEM((1, ROUTE_LANES), F32), pltpu.VMEM((PM_TM, ROUTE_LANES), F32)],
        compiler_params=_cparams(1),
        name="postmix",
    )(y_act, gates, gates, y_pool, xp, xs, wa, wb, wo, g_ffn, wr_cat, b_r)


SC_CH = 96


def _sc_workers():
    info = plsc.get_sparse_core_info()
    return info.num_cores, info.num_cores * info.num_subcores


def _sc_dispatch(tn, slots):
    n_cores, n_workers = _sc_workers()
    per_w = (T_ALL // SC_CH) // n_workers
    assert per_w * n_workers * SC_CH == T_ALL
    slots = slots.reshape(2, n_workers, per_w, SC_CH)

    @functools.partial(
        pl.kernel,
        mesh=plsc.VectorSubcoreMesh(core_axis_name="c", subcore_axis_name="s"),
        out_type=jax.ShapeDtypeStruct((N_SLOTS, D_PACK), U32),
        scratch_types=[pltpu.VMEM((2, per_w, SC_CH), I32), pltpu.VMEM((SC_CH, D_PACK), U32)],
    )
    def k(tn_hbm, slots_hbm, xs_hbm, idx_v, rows_v):
        wid = lax.axis_index("s") * n_cores + lax.axis_index("c")
        c0 = wid * per_w
        pltpu.sync_copy(slots_hbm.at[0, wid], idx_v.at[0])
        pltpu.sync_copy(slots_hbm.at[1, wid], idx_v.at[1])

        @pl.loop(0, per_w)
        def _(c):
            row0 = pl.multiple_of((c0 + c) * SC_CH, SC_CH)
            pltpu.sync_copy(tn_hbm.at[pl.ds(row0, SC_CH)], rows_v)
            pltpu.sync_copy(rows_v, xs_hbm.at[idx_v.at[0, c]])
            pltpu.sync_copy(rows_v, xs_hbm.at[idx_v.at[1, c]])

    return k(tn, slots)


def _sc_collect(ys, slots):
    n_cores, n_workers = _sc_workers()
    per_w = (N_ASSIGN // SC_CH) // n_workers
    assert per_w * n_workers * SC_CH == N_ASSIGN
    slots = slots.reshape(n_workers, per_w, SC_CH)

    @functools.partial(
        pl.kernel,
        mesh=plsc.VectorSubcoreMesh(core_axis_name="c", subcore_axis_name="s"),
        out_type=jax.ShapeDtypeStruct((N_ASSIGN, D_PACK), U32),
        scratch_types=[pltpu.VMEM((per_w, SC_CH), I32), pltpu.VMEM((SC_CH, D_PACK), U32)],
    )
    def k(ys_hbm, slots_hbm, out_hbm, idx_v, rows_v):
        wid = lax.axis_index("s") * n_cores + lax.axis_index("c")
        c0 = wid * per_w
        pltpu.sync_copy(slots_hbm.at[wid], idx_v)

        @pl.loop(0, per_w)
        def _(c):
            row0 = pl.multiple_of((c0 + c) * SC_CH, SC_CH)
            pltpu.sync_copy(ys_hbm.at[idx_v.at[c]], rows_v)
            pltpu.sync_copy(rows_v, out_hbm.at[pl.ds(row0, SC_CH)])

    return k(ys, slots)


W_PARTS = 2


def _expert_body(t0_ref, nt_ref, xs_hbm, wg_hbm, wu_hbm, wd_hbm, ys_hbm,
                 wg_ref, wu_ref, wd_ref, xb_ref, yb_ref, wgb_ref, wub_ref, wdb_ref, wsem, xsem, ysem):
    e = pl.program_id(0)
    n = nt_ref[e]
    g0 = t0_ref[e]
    ws = e & 1

    def w_copies(ex, slot):
        out = []
        for hbm, buf in ((wg_hbm, wg_ref), (wu_hbm, wu_ref), (wd_hbm, wd_ref)):
            rb = buf.shape[1] // W_PARTS
            for p in range(W_PARTS):
                out.append((pltpu.make_async_copy(hbm.at[ex, pl.ds(p * rb, rb)],
                                                  buf.at[slot, pl.ds(p * rb, rb)], wsem.at[slot]), p))
        return out

    @pl.when(e == 0)
    def _():
        for cp, p in w_copies(0, 0):
            cp.start(priority=p)

    @pl.when(e + 1 < MOE_EXPERTS)
    def _():
        for cp, p in w_copies(e + 1, 1 - ws):
            cp.start(priority=p)

    for cp, _ in w_copies(e, ws):
        cp.wait()

    def rows(j):
        return pl.ds(pl.multiple_of((g0 + j) * TME, TME), TME)

    def x_copy(j, s):
        return pltpu.make_async_copy(xs_hbm.at[rows(j)], xb_ref.at[s], xsem.at[s])

    def y_copy(j, s):
        return pltpu.make_async_copy(yb_ref.at[s], ys_hbm.at[rows(j)], ysem.at[s])

    @pl.when(n > 0)
    def _():
        x_copy(0, 0).start()
        wgb_ref[...] = wg_ref[ws].astype(BF16)
        wub_ref[...] = wu_ref[ws].astype(BF16)
        wdb_ref[...] = wd_ref[ws].astype(BF16)

        def tile(j, c):
            s = j & 1
            x_copy(j, s).wait()
            @pl.when(j + 1 < n)
            def _():
                x_copy(j + 1, 1 - s).start()
            @pl.when(j >= 2)
            def _():
                y_copy(j - 2, s).wait()
            x = _unpack_pairs(xb_ref[s], BF16)
            hg = jnp.dot(x, wgb_ref[...], preferred_element_type=F32)
            hu = jnp.dot(x, wub_ref[...], preferred_element_type=F32)
            act = (hg * _sigmoid(hg)) * hu
            yb_ref[s] = _pack_pairs(jnp.dot(act.astype(BF16), wdb_ref[...], preferred_element_type=F32))
            y_copy(j, s).start()
            return c
        lax.fori_loop(0, n, tile, 0)

        @pl.when(n >= 2)
        def _():
            y_copy(n - 2, n & 1).wait()
        y_copy(n - 1, (n - 1) & 1).wait()


def _experts(tile0, tiles, xs, w_eg, w_eu, w_ed):
    any_spec = pl.BlockSpec(memory_space=pl.ANY)
    grid_spec = pltpu.PrefetchScalarGridSpec(
        num_scalar_prefetch=2,
        grid=(MOE_EXPERTS,),
        in_specs=[any_spec] * 4,
        out_specs=any_spec,
        scratch_shapes=[
            pltpu.VMEM((2, D_MODEL, MOE_FF), F32),
            pltpu.VMEM((2, D_MODEL, MOE_FF), F32),
            pltpu.VMEM((2, MOE_FF, D_MODEL), F32),
            pltpu.VMEM((2, TME, D_PACK), U32),
            pltpu.VMEM((2, TME, D_PACK), U32),
            pltpu.VMEM((D_MODEL, MOE_FF), BF16),
            pltpu.VMEM((D_MODEL, MOE_FF), BF16),
            pltpu.VMEM((MOE_FF, D_MODEL), BF16),
            pltpu.SemaphoreType.DMA((2,)),
            pltpu.SemaphoreType.DMA((2,)),
            pltpu.SemaphoreType.DMA((2,)),
        ],
    )
    return pl.pallas_call(
        _expert_body,
        grid_spec=grid_spec,
        out_shape=jax.ShapeDtypeStruct((N_SLOTS, D_PACK), U32),
        compiler_params=_cparams(1),
        name="experts",
    )(tile0, tiles, xs, w_eg, w_eu, w_ed)


FN_TM = 512
FN_PROMPT_BLOCKS = T_PROMPT // FN_TM


def _final_body(h_ref, y0_ref, y1_ref, rt_ref, g_ref, op_ref, os_ref):
    i = pl.program_id(0)
    rt = rt_ref[...]
    z = (h_ref[...] + rt[:, 0:1] * _unpack_pairs(y0_ref[...], F32)
         + rt[:, 1:2] * _unpack_pairs(y1_ref[...], F32))
    inv = lax.rsqrt(jnp.mean(z * z, axis=-1, keepdims=True) + EPS)
    out = (z * inv) * g_ref[...]
    @pl.when(i < FN_PROMPT_BLOCKS)
    def _():
        op_ref[...] = out
    @pl.when(i >= FN_PROMPT_BLOCKS)
    def _():
        os_ref[...] = out


def _final(h, y, route, g_final):
    n = T_ALL // FN_TM
    npb = FN_PROMPT_BLOCKS
    yoff = T_ALL // FN_TM
    return pl.pallas_call(
        _final_body,
        grid=(n,),
        in_specs=[
            pl.BlockSpec((FN_TM, D_MODEL), lambda i: (i, 0)),
            pl.BlockSpec((FN_TM, D_PACK), lambda i: (i, 0)),
            pl.BlockSpec((FN_TM, D_PACK), lambda i: (yoff + i, 0)),
            pl.BlockSpec((FN_TM, ROUTE_LANES), lambda i: (i, 0)),
            pl.BlockSpec((1, D_MODEL), lambda i: (0, 0)),
        ],
        out_specs=(
            pl.BlockSpec((FN_TM, D_MODEL), lambda i: (jnp.minimum(i, npb - 1), 0)),
            pl.BlockSpec((FN_TM, D_MODEL), lambda i: (jnp.maximum(i - npb, 0), 0)),
        ),
        out_shape=(
            jax.ShapeDtypeStruct((T_PROMPT, D_MODEL), F32),
            jax.ShapeDtypeStruct((T_SAMPLE, D_MODEL), F32),
        ),
        compiler_params=_cparams(1),
        name="final",
    )(h, y, y, route, g_final)


def _dispatch_plan(route_t, cnt):
    counts = cnt[0, EXP_LANE0:EXP_LANE0 + MOE_EXPERTS].astype(I32)
    tiles = (counts + (TME - 1)) // TME
    cumt = jnp.cumsum(tiles)
    pad_off = (cumt - tiles) * TME
    rank = route_t[2:4].astype(I32)
    eid = route_t[4:6].astype(I32)
    onehot = eid[None] == jnp.arange(MOE_EXPERTS, dtype=I32)[:, None, None]
    slots = rank + jnp.sum(jnp.where(onehot, pad_off[:, None, None], 0), axis=0)
    return slots, cumt - tiles, tiles


def kernel(x_prompt, x_sample, state_pool, state_ssm_re, state_ssm_im, g_mix, w_in, w_pool,
           pool_scale, ssm_a_re, ssm_a_im, ssm_log_dt, ssm_b_re, ssm_b_im, ssm_c_re, ssm_c_im,
           ssm_d, w_glu_a, w_glu_b, w_out, g_ffn, w_router_group, b_router_group,
           w_router_expert, b_router_expert, w_exp_gate, w_exp_up, w_exp_down, g_final):
    l = 0
    xp = x_prompt.reshape(T_PROMPT, D_MODEL)
    xs = x_sample.transpose(1, 0, 2).reshape(T_SAMPLE, D_MODEL)
    w_pool_bf = w_pool[l].astype(BF16)
    g_mix2 = g_mix[l].reshape(1, D_MODEL)
    scale2 = pool_scale[l].reshape(1, D_MODEL)

    n_pt = T_PROMPT // IN_TM
    u, gates, w_in_bf = _inproj(xp, g_mix2, w_in[l], 0, n_i=1, emit_w=True)
    u, gates, wa_bf, wb_bf, wo_bf = _inproj(xp, g_mix2, w_in_bf, 1, dst=(u, gates), x_block0=1, n_i=n_pt - 1,
                                            cast=(w_glu_a[l], w_glu_b[l], w_out[l]))
    u, gates = _inproj(xs, g_mix2, w_in_bf, n_pt, dst=(u, gates))

    y_pool, pool_tail = _pool_prompt(u, w_pool_bf, scale2)
    hist_tm = state_pool[l].transpose(1, 0, 2)
    y_pool, pool_buf_tm = _pool_sample(u, hist_tm, w_pool_bf, scale2, y_pool)

    tables = _ssm_tables(ssm_a_re[l], ssm_a_im[l], ssm_log_dt[l], ssm_b_re[l], ssm_b_im[l],
                         ssm_c_re[l], ssm_c_im[l], ssm_d[l])
    h0r = state_ssm_re[l].reshape(DEC_BATCH, SSM_GROUPS * SSM_STATE)
    h0i = state_ssm_im[l].reshape(DEC_BATCH, SSM_GROUPS * SSM_STATE)
    y_act, h_prompt, hs_re, hs_im = _ssm(u, h0r, h0i, tables)

    assert EXP_LANE0 == MOE_GROUPS
    w_r = jnp.concatenate([w_router_group[l], w_router_expert[l],
                           jnp.zeros((D_MODEL, ROUTE_LANES - EXP_LANE0 - MOE_EXPERTS), F32)], axis=1)
    wr_hi = w_r.astype(BF16)
    wr_cat = jnp.concatenate([wr_hi, (w_r - wr_hi.astype(F32)).astype(BF16)], axis=1)
    b_r = jnp.zeros((1, ROUTE_LANES), F32)
    b_r = b_r.at[0, :MOE_GROUPS].set(b_router_group[l])
    b_r = b_r.at[0, EXP_LANE0:EXP_LANE0 + MOE_EXPERTS].set(b_router_expert[l])

    h, tn, route, route_t, cnt = _postmix(y_act, gates, y_pool, xp, xs, wa_bf, wb_bf, wo_bf,
                                          g_ffn[l].reshape(1, D_MODEL), wr_cat, b_r)
    slots, tile0, tiles = _dispatch_plan(route_t, cnt)
    xs_sorted = _sc_dispatch(tn, slots)
    ys_sorted = _experts(tile0, tiles, xs_sorted, w_exp_gate[l], w_exp_up[l], w_exp_down[l])
    y = _sc_collect(ys_sorted, slots)
    yp, ys = _final(h, y, route, g_final.reshape(1, D_MODEL))

    y_prompt = yp.reshape(BATCH, SEQ, D_MODEL)
    y_sample = ys.reshape(DEC_SEQ, DEC_BATCH, D_MODEL).transpose(1, 0, 2)
    new_pool_prompt = pool_tail[:, HIST - POOL_BUF:, :][None]
    new_pool_sample = pool_buf_tm.transpose(1, 0, 2)[None]
    hp = h_prompt.reshape(BATCH, N_OCT, 2, OCT_GROUPS, SSM_STATE).transpose(2, 0, 1, 3, 4)
    hp = hp.reshape(2, BATCH, SSM_GROUPS, SSM_STATE)
    shp = (1, DEC_BATCH, SSM_GROUPS, SSM_STATE)
    return (y_prompt, y_sample, new_pool_prompt, hp[0][None], hp[1][None], new_pool_sample,
            hs_re.reshape(shp), hs_im.reshape(shp))
```
